```python
import math
import jax, jax.numpy as jnp
from jax import lax
import numpy as np


D_MODEL = 1024
BATCH = 8
SEQ = 4096
DEPTH = 4

N_MIXERS = 2
N_SSD = (DEPTH + 1) // 2
N_GMLP = DEPTH // 2

SSD_EXPAND = 2
SSD_INNER = SSD_EXPAND * D_MODEL
SSD_HEADDIM = 64
SSD_HEADS = SSD_INNER // SSD_HEADDIM
SSD_GROUPS = 8
SSD_STATE = 128
SSD_CONV_DIM = SSD_INNER + 2 * SSD_GROUPS * SSD_STATE
SSD_IN_DIM = 2 * SSD_INNER + 2 * SSD_GROUPS * SSD_STATE + SSD_HEADS
CONV_K = 4
CHUNK = 128
DT_MIN = 0.001
DT_MAX = 0.1

GMLP_INNER = 2 * D_MODEL
GMLP_GROUPS = 16
GMLP_GROUP_DIM = GMLP_INNER // GMLP_GROUPS
GMLP_CHUNK = 128

FFN_DIM = int(math.ceil((8 * D_MODEL / 3) / 256) * 256)

PLE_DIM = 256

RMS_EPS = 1e-6
LN_EPS = 1e-5

kernel_name = 'hybrid_ssd_gmlp_trunk'


def rmsnorm(x, w, eps=RMS_EPS):
    xf = x.astype(jnp.float32)
    y = xf * lax.rsqrt(jnp.mean(xf * xf, axis=-1, keepdims=True) + eps)
    return (y * w.astype(jnp.float32)).astype(x.dtype)


def layernorm(x, w, b, eps=LN_EPS):
    xf = x.astype(jnp.float32)
    mu = jnp.mean(xf, axis=-1, keepdims=True)
    xc = xf - mu
    y = xc * lax.rsqrt(jnp.mean(xc * xc, axis=-1, keepdims=True) + eps)
    return (y * w.astype(jnp.float32) + b.astype(jnp.float32)).astype(x.dtype)


def gated_rmsnorm(y, z, w, eps=LN_EPS):
    g = (y * jax.nn.silu(z)).astype(jnp.float32)
    shp = g.shape
    g = g.reshape(shp[:-1] + (SSD_GROUPS, shp[-1] // SSD_GROUPS))
    g = g * lax.rsqrt(jnp.mean(g * g, axis=-1, keepdims=True) + eps)
    return (g.reshape(shp) * w.astype(jnp.float32)).astype(y.dtype)


def causal_dwconv(x, w, b):
    k, c = w.shape
    y = lax.conv_general_dilated(
        x, w[:, None, :].astype(x.dtype), window_strides=(1,), padding=[(k - 1, 0)],
        dimension_numbers=('NWC', 'WIO', 'NWC'), feature_group_count=c)
    return y + b.astype(x.dtype)


def segsum(a):
    t = a.shape[-1]
    cs = jnp.cumsum(a, axis=-1)
    diff = cs[..., :, None] - cs[..., None, :]
    mask = jnp.tril(jnp.ones((t, t), dtype=bool))
    return jnp.where(mask, diff, -jnp.inf)


def ssd_scan(x, dt, a, bm, cm):
    b, s, h, p = x.shape
    g, n = bm.shape[-2], bm.shape[-1]
    r = h // g
    c = s // CHUNK
    dtype = x.dtype
    xr = (x * dt[..., None]).reshape(b, c, CHUNK, g, r, p)
    da = (dt.astype(jnp.float32) * a).reshape(b, c, CHUNK, g, r).transpose(0, 1, 3, 4, 2)
    da_cs = jnp.cumsum(da, axis=-1)
    br = bm.reshape(b, c, CHUNK, g, n)
    cr = cm.reshape(b, c, CHUNK, g, n)
    lmat = jnp.exp(segsum(da)).astype(dtype)
    cb = jnp.einsum('bclgn,bcsgn->bcgls', cr, br)
    wmat = cb[:, :, :, None] * lmat
    y_diag = jnp.einsum('bcgrls,bcsgrp->bclgrp', wmat, xr)
    dec_states = jnp.exp(da_cs[..., -1:] - da_cs).astype(dtype).transpose(0, 1, 4, 2, 3)
    states = jnp.einsum('bclgn,bclgrp->bcgrpn', br, xr * dec_states[..., None])
    chunk_decay = jnp.exp(da_cs[..., -1]).astype(dtype)

    def step(carry, inp):
        dec_c, st_c = inp
        new = dec_c[..., None, None] * carry + st_c
        return new, carry

    init = jnp.zeros((b, g, r, p, n), dtype=states.dtype)
    _, prev = lax.scan(step, init, (jnp.moveaxis(chunk_decay, 1, 0), jnp.moveaxis(states, 1, 0)))
    prev = jnp.moveaxis(prev, 0, 1)
    dec_out = jnp.exp(da_cs).astype(dtype).transpose(0, 1, 4, 2, 3)
    y_off = jnp.einsum('bclgn,bcgrpn->bclgrp', cr, prev) * dec_out[..., None]
    return (y_diag + y_off).reshape(b, s, h, p)


def ssd_mixer(u, w_in, conv_w, conv_b, dt_bias, a_log, d_skip, norm_w, w_out):
    b, s, _ = u.shape
    zxbcdt = u @ w_in
    z = zxbcdt[..., :SSD_INNER]
    xbc = zxbcdt[..., SSD_INNER:SSD_INNER + SSD_CONV_DIM]
    dt = zxbcdt[..., SSD_INNER + SSD_CONV_DIM:]
    xbc = jax.nn.silu(causal_dwconv(xbc, conv_w, conv_b))
    xs = xbc[..., :SSD_INNER].reshape(b, s, SSD_HEADS, SSD_HEADDIM)
    bm = xbc[..., SSD_INNER:SSD_INNER + SSD_GROUPS * SSD_STATE].reshape(b, s, SSD_GROUPS, SSD_STATE)
    cm = xbc[..., SSD_INNER + SSD_GROUPS * SSD_STATE:].reshape(b, s, SSD_GROUPS, SSD_STATE)
    dt = jax.nn.softplus(dt + dt_bias)
    a = -jnp.exp(a_log.astype(jnp.float32))
    y = ssd_scan(xs, dt, a, bm, cm) + xs * d_skip[:, None]
    y = gated_rmsnorm(y.reshape(b, s, SSD_INNER), z, norm_w)
    return y @ w_out


def gmlp_mixer(u, w_in, b_in, ln_w, ln_b, w_s, b_s, w_out):
    b, s, _ = u.shape
    c = s // GMLP_CHUNK
    hp = jax.nn.gelu(u @ w_in + b_in, approximate=False)
    uu = hp[..., :GMLP_INNER]
    vv = layernorm(hp[..., GMLP_INNER:], ln_w, ln_b)
    vv = vv.reshape(b, c, GMLP_CHUNK, GMLP_GROUPS, GMLP_GROUP_DIM)
    ws = jnp.tril(w_s)
    mixed = jnp.einsum('gts,bcsgd->bctgd', ws, vv) + b_s.T[None, None, :, :, None]
    return (uu * mixed.reshape(b, s, GMLP_INNER)) @ w_out


def swiglu(u, w_gate, w_up, w_down):
    return (jax.nn.silu(u @ w_gate) * (u @ w_up)) @ w_down


def _fwd_setup_inputs(seed: int = 0) -> dict:
    key = jax.random.key(seed)
    ks = jax.random.split(key, 32)
    f32 = jnp.float32

    def nrm(k, shape, scale):
        return jax.random.normal(k, shape, f32) * scale

    x = nrm(ks[0], (BATCH, SEQ, D_MODEL), 1.0)
    p = nrm(ks[1], (DEPTH, BATCH, SEQ, PLE_DIM), 1.0)
    norm_mix = 1.0 + nrm(ks[2], (DEPTH, D_MODEL), 0.02)
    norm_ffn = 1.0 + nrm(ks[3], (DEPTH, D_MODEL), 0.02)
    ssd_w_in = nrm(ks[4], (N_SSD, D_MODEL, SSD_IN_DIM), D_MODEL ** -0.5)
    ssd_conv_w = nrm(ks[5], (N_SSD, CONV_K, SSD_CONV_DIM), CONV_K ** -0.5)
    ssd_conv_b = nrm(ks[6], (N_SSD, SSD_CONV_DIM), 0.02)
    dt0 = jnp.exp(jax.random.uniform(ks[7], (N_SSD, SSD_HEADS), f32)
                  * (math.log(DT_MAX) - math.log(DT_MIN)) + math.log(DT_MIN))
    ssd_dt_bias = dt0 + jnp.log(-jnp.expm1(-dt0))
    ssd_a_log = jnp.log(jax.random.uniform(ks[8], (N_SSD, SSD_HEADS), f32, minval=1.0, maxval=16.0))
    ssd_d = 1.0 + nrm(ks[9], (N_SSD, SSD_HEADS), 0.02)
    ssd_norm_w = 1.0 + nrm(ks[10], (N_SSD, SSD_INNER), 0.02)
    ssd_w_out = nrm(ks[11], (N_SSD, SSD_INNER, D_MODEL), SSD_INNER ** -0.5)
    gmlp_w_in = nrm(ks[12], (N_GMLP, D_MODEL, 2 * GMLP_INNER), D_MODEL ** -0.5)
    gmlp_b_in = nrm(ks[13], (N_GMLP, 2 * GMLP_INNER), 0.02)
    gmlp_ln_w = 1.0 + nrm(ks[14], (N_GMLP, GMLP_INNER), 0.02)
    gmlp_ln_b = nrm(ks[15], (N_GMLP, GMLP_INNER), 0.02)
    gmlp_w_s = nrm(ks[16], (N_GMLP, GMLP_GROUPS, GMLP_CHUNK, GMLP_CHUNK), GMLP_CHUNK ** -0.5)
    gmlp_b_s = 1.0 + nrm(ks[17], (N_GMLP, GMLP_GROUPS, GMLP_CHUNK), 0.02)
    gmlp_w_out = nrm(ks[18], (N_GMLP, GMLP_INNER, D_MODEL), GMLP_INNER ** -0.5)
    ffn_w_gate = nrm(ks[19], (DEPTH, D_MODEL, FFN_DIM), D_MODEL ** -0.5)
    ffn_w_up = nrm(ks[20], (DEPTH, D_MODEL, FFN_DIM), D_MODEL ** -0.5)
    ffn_w_down = nrm(ks[21], (DEPTH, FFN_DIM, D_MODEL), FFN_DIM ** -0.5)
    ple_w_proj = nrm(ks[22], (DEPTH, PLE_DIM, D_MODEL), PLE_DIM ** -0.5)
    ple_norm = 1.0 + nrm(ks[23], (DEPTH, D_MODEL), 0.02)
    ple_gate_norm = 1.0 + nrm(ks[24], (DEPTH, D_MODEL), 0.02)
    ple_w_gate = nrm(ks[25], (DEPTH, D_MODEL, D_MODEL), D_MODEL ** -0.5)
    final_norm = 1.0 + nrm(ks[26], (D_MODEL,), 0.02)
    return {'x': x, 'p': p, 'norm_mix': norm_mix, 'norm_ffn': norm_ffn,
            'ssd_w_in': ssd_w_in, 'ssd_conv_w': ssd_conv_w, 'ssd_conv_b': ssd_conv_b,
            'ssd_dt_bias': ssd_dt_bias, 'ssd_a_log': ssd_a_log, 'ssd_d': ssd_d,
            'ssd_norm_w': ssd_norm_w, 'ssd_w_out': ssd_w_out,
            'gmlp_w_in': gmlp_w_in, 'gmlp_b_in': gmlp_b_in, 'gmlp_ln_w': gmlp_ln_w,
            'gmlp_ln_b': gmlp_ln_b, 'gmlp_w_s': gmlp_w_s, 'gmlp_b_s': gmlp_b_s,
            'gmlp_w_out': gmlp_w_out,
            'ffn_w_gate': ffn_w_gate, 'ffn_w_up': ffn_w_up, 'ffn_w_down': ffn_w_down,
            'ple_w_proj': ple_w_proj, 'ple_norm': ple_norm, 'ple_gate_norm': ple_gate_norm,
            'ple_w_gate': ple_w_gate, 'final_norm': final_norm}


def _fwd_reference(x, p, norm_mix, norm_ffn,
              ssd_w_in, ssd_conv_w, ssd_conv_b, ssd_dt_bias, ssd_a_log, ssd_d, ssd_norm_w, ssd_w_out,
              gmlp_w_in, gmlp_b_in, gmlp_ln_w, gmlp_ln_b, gmlp_w_s, gmlp_b_s, gmlp_w_out,
              ffn_w_gate, ffn_w_up, ffn_w_down,
              ple_w_proj, ple_norm, ple_gate_norm, ple_w_gate, final_norm):
    h = x
    for i in range(DEPTH):
        j = i // N_MIXERS
        hn = rmsnorm(h, norm_mix[i])
        if i % N_MIXERS == 0:
            mix = ssd_mixer(hn, ssd_w_in[j], ssd_conv_w[j], ssd_conv_b[j], ssd_dt_bias[j],
                            ssd_a_log[j], ssd_d[j], ssd_norm_w[j], ssd_w_out[j])
        else:
            mix = gmlp_mixer(hn, gmlp_w_in[j], gmlp_b_in[j], gmlp_ln_w[j], gmlp_ln_b[j],
                             gmlp_w_s[j], gmlp_b_s[j], gmlp_w_out[j])
        h = h + mix
        h = h + swiglu(rmsnorm(h, norm_ffn[i]), ffn_w_gate[i], ffn_w_up[i], ffn_w_down[i])
        e = rmsnorm(p[i] @ ple_w_proj[i], ple_norm[i])
        gate = jax.nn.sigmoid(rmsnorm(h, ple_gate_norm[i]) @ ple_w_gate[i])
        h = h + gate * e
    return rmsnorm(h, final_norm)


import jax as _jax
import jax.numpy as _jnp

TWIN_FORMAT = 'train_step'
FWD_PARAMS = ['x', 'p', 'norm_mix', 'norm_ffn', 'ssd_w_in', 'ssd_conv_w', 'ssd_conv_b', 'ssd_dt_bias', 'ssd_a_log', 'ssd_d', 'ssd_norm_w', 'ssd_w_out', 'gmlp_w_in', 'gmlp_b_in', 'gmlp_ln_w', 'gmlp_ln_b', 'gmlp_w_s', 'gmlp_b_s', 'gmlp_w_out', 'ffn_w_gate', 'ffn_w_up', 'ffn_w_down', 'ple_w_proj', 'ple_norm', 'ple_gate_norm', 'ple_w_gate', 'final_norm']
TWIN_WEIGHTS = ['norm_mix', 'norm_ffn', 'ssd_w_in', 'ssd_conv_w', 'ssd_conv_b', 'ssd_dt_bias', 'ssd_a_log', 'ssd_d', 'ssd_norm_w', 'ssd_w_out', 'gmlp_w_in', 'gmlp_b_in', 'gmlp_ln_w', 'gmlp_ln_b', 'gmlp_w_s', 'gmlp_b_s', 'gmlp_w_out', 'ffn_w_gate', 'ffn_w_up', 'ffn_w_down', 'ple_w_proj', 'ple_norm', 'ple_gate_norm', 'ple_w_gate', 'final_norm']
TWIN_DIFF_INPUT = 'x'
TWIN_INPUTS = ['x', 'p', 'norm_mix', 'norm_ffn', 'ssd_w_in', 'ssd_conv_w', 'ssd_conv_b', 'ssd_dt_bias', 'ssd_a_log', 'ssd_d', 'ssd_norm_w', 'ssd_w_out', 'gmlp_w_in', 'gmlp_b_in', 'gmlp_ln_w', 'gmlp_ln_b', 'gmlp_w_s', 'gmlp_b_s', 'gmlp_w_out', 'ffn_w_gate', 'ffn_w_up', 'ffn_w_down', 'ple_w_proj', 'ple_norm', 'ple_gate_norm', 'ple_w_gate', 'final_norm', 'loss_target', 'm_norm_mix', 'm_norm_ffn', 'm_ssd_w_in', 'm_ssd_conv_w', 'm_ssd_conv_b', 'm_ssd_dt_bias', 'm_ssd_a_log', 'm_ssd_d', 'm_ssd_norm_w', 'm_ssd_w_out', 'm_gmlp_w_in', 'm_gmlp_b_in', 'm_gmlp_ln_w', 'm_gmlp_ln_b', 'm_gmlp_w_s', 'm_gmlp_b_s', 'm_gmlp_w_out', 'm_ffn_w_gate', 'm_ffn_w_up', 'm_ffn_w_down', 'm_ple_w_proj', 'm_ple_norm', 'm_ple_gate_norm', 'm_ple_w_gate', 'm_final_norm', 'v_norm_mix', 'v_norm_ffn', 'v_ssd_w_in', 'v_ssd_conv_w', 'v_ssd_conv_b', 'v_ssd_dt_bias', 'v_ssd_a_log', 'v_ssd_d', 'v_ssd_norm_w', 'v_ssd_w_out', 'v_gmlp_w_in', 'v_gmlp_b_in', 'v_gmlp_ln_w', 'v_gmlp_ln_b', 'v_gmlp_w_s', 'v_gmlp_b_s', 'v_gmlp_w_out', 'v_ffn_w_gate', 'v_ffn_w_up', 'v_ffn_w_down', 'v_ple_w_proj', 'v_ple_norm', 'v_ple_gate_norm', 'v_ple_w_gate', 'v_final_norm']
TWIN_OUTPUTS = ['loss', 'grad_x', 'grad_norm_mix', 'grad_norm_ffn', 'grad_ssd_w_in', 'grad_ssd_conv_w', 'grad_ssd_conv_b', 'grad_ssd_dt_bias', 'grad_ssd_a_log', 'grad_ssd_d', 'grad_ssd_norm_w', 'grad_ssd_w_out', 'grad_gmlp_w_in', 'grad_gmlp_b_in', 'grad_gmlp_ln_w', 'grad_gmlp_ln_b', 'grad_gmlp_w_s', 'grad_gmlp_b_s', 'grad_gmlp_w_out', 'grad_ffn_w_gate', 'grad_ffn_w_up', 'grad_ffn_w_down', 'grad_ple_w_proj', 'grad_ple_norm', 'grad_ple_gate_norm', 'grad_ple_w_gate', 'grad_final_norm', 'delta_norm_mix', 'delta_norm_ffn', 'delta_ssd_w_in', 'delta_ssd_conv_w', 'delta_ssd_conv_b', 'delta_ssd_dt_bias', 'delta_ssd_a_log', 'delta_ssd_d', 'delta_ssd_norm_w', 'delta_ssd_w_out', 'delta_gmlp_w_in', 'delta_gmlp_b_in', 'delta_gmlp_ln_w', 'delta_gmlp_ln_b', 'delta_gmlp_w_s', 'delta_gmlp_b_s', 'delta_gmlp_w_out', 'delta_ffn_w_gate', 'delta_ffn_w_up', 'delta_ffn_w_down', 'delta_ple_w_proj', 'delta_ple_norm', 'delta_ple_gate_norm', 'delta_ple_w_gate', 'delta_final_norm', 'new_m_norm_mix', 'new_m_norm_ffn', 'new_m_ssd_w_in', 'new_m_ssd_conv_w', 'new_m_ssd_conv_b', 'new_m_ssd_dt_bias', 'new_m_ssd_a_log', 'new_m_ssd_d', 'new_m_ssd_norm_w', 'new_m_ssd_w_out', 'new_m_gmlp_w_in', 'new_m_gmlp_b_in', 'new_m_gmlp_ln_w', 'new_m_gmlp_ln_b', 'new_m_gmlp_w_s', 'new_m_gmlp_b_s', 'new_m_gmlp_w_out', 'new_m_ffn_w_gate', 'new_m_ffn_w_up', 'new_m_ffn_w_down', 'new_m_ple_w_proj', 'new_m_ple_norm', 'new_m_ple_gate_norm', 'new_m_ple_w_gate', 'new_m_final_norm', 'new_v_norm_mix', 'new_v_norm_ffn', 'new_v_ssd_w_in', 'new_v_ssd_conv_w', 'new_v_ssd_conv_b', 'new_v_ssd_dt_bias', 'new_v_ssd_a_log', 'new_v_ssd_d', 'new_v_ssd_norm_w', 'new_v_ssd_w_out', 'new_v_gmlp_w_in', 'new_v_gmlp_b_in', 'new_v_gmlp_ln_w', 'new_v_gmlp_ln_b', 'new_v_gmlp_w_s', 'new_v_gmlp_b_s', 'new_v_gmlp_w_out', 'new_v_ffn_w_gate', 'new_v_ffn_w_up', 'new_v_ffn_w_down', 'new_v_ple_w_proj', 'new_v_ple_norm', 'new_v_ple_gate_norm', 'new_v_ple_w_gate', 'new_v_final_norm']
TWIN_LEAF_KINDS = {'loss': 'loss', 'grad_x': 'grad_x', 'grad_norm_mix': 'grad_w', 'grad_norm_ffn': 'grad_w', 'grad_ssd_w_in': 'grad_w', 'grad_ssd_conv_w': 'grad_w', 'grad_ssd_conv_b': 'grad_w', 'grad_ssd_dt_bias': 'grad_w', 'grad_ssd_a_log': 'grad_w', 'grad_ssd_d': 'grad_w', 'grad_ssd_norm_w': 'grad_w', 'grad_ssd_w_out': 'grad_w', 'grad_gmlp_w_in': 'grad_w', 'grad_gmlp_b_in': 'grad_w', 'grad_gmlp_ln_w': 'grad_w', 'grad_gmlp_ln_b': 'grad_w', 'grad_gmlp_w_s': 'grad_w', 'grad_gmlp_b_s': 'grad_w', 'grad_gmlp_w_out': 'grad_w', 'grad_ffn_w_gate': 'grad_w', 'grad_ffn_w_up': 'grad_w', 'grad_ffn_w_down': 'grad_w', 'grad_ple_w_proj': 'grad_w', 'grad_ple_norm': 'grad_w', 'grad_ple_gate_norm': 'grad_w', 'grad_ple_w_gate': 'grad_w', 'grad_final_norm': 'grad_w', 'delta_norm_mix': 'delta_w', 'delta_norm_ffn': 'delta_w', 'delta_ssd_w_in': 'delta_w', 'delta_ssd_conv_w': 'delta_w', 'delta_ssd_conv_b': 'delta_w', 'delta_ssd_dt_bias': 'delta_w', 'delta_ssd_a_log': 'delta_w', 'delta_ssd_d': 'delta_w', 'delta_ssd_norm_w': 'delta_w', 'delta_ssd_w_out': 'delta_w', 'delta_gmlp_w_in': 'delta_w', 'delta_gmlp_b_in': 'delta_w', 'delta_gmlp_ln_w': 'delta_w', 'delta_gmlp_ln_b': 'delta_w', 'delta_gmlp_w_s': 'delta_w', 'delta_gmlp_b_s': 'delta_w', 'delta_gmlp_w_out': 'delta_w', 'delta_ffn_w_gate': 'delta_w', 'delta_ffn_w_up': 'delta_w', 'delta_ffn_w_down': 'delta_w', 'delta_ple_w_proj': 'delta_w', 'delta_ple_norm': 'delta_w', 'delta_ple_gate_norm': 'delta_w', 'delta_ple_w_gate': 'delta_w', 'delta_final_norm': 'delta_w', 'new_m_norm_mix': 'new_m', 'new_m_norm_ffn': 'new_m', 'new_m_ssd_w_in': 'new_m', 'new_m_ssd_conv_w': 'new_m', 'new_m_ssd_conv_b': 'new_m', 'new_m_ssd_dt_bias': 'new_m', 'new_m_ssd_a_log': 'new_m', 'new_m_ssd_d': 'new_m', 'new_m_ssd_norm_w': 'new_m', 'new_m_ssd_w_out': 'new_m', 'new_m_gmlp_w_in': 'new_m', 'new_m_gmlp_b_in': 'new_m', 'new_m_gmlp_ln_w': 'new_m', 'new_m_gmlp_ln_b': 'new_m', 'new_m_gmlp_w_s': 'new_m', 'new_m_gmlp_b_s': 'new_m', 'new_m_gmlp_w_out': 'new_m', 'new_m_ffn_w_gate': 'new_m', 'new_m_ffn_w_up': 'new_m', 'new_m_ffn_w_down': 'new_m', 'new_m_ple_w_proj': 'new_m', 'new_m_ple_norm': 'new_m', 'new_m_ple_gate_norm': 'new_m', 'new_m_ple_w_gate': 'new_m', 'new_m_final_norm': 'new_m', 'new_v_norm_mix': 'new_v', 'new_v_norm_ffn': 'new_v', 'new_v_ssd_w_in': 'new_v', 'new_v_ssd_conv_w': 'new_v', 'new_v_ssd_conv_b': 'new_v', 'new_v_ssd_dt_bias': 'new_v', 'new_v_ssd_a_log': 'new_v', 'new_v_ssd_d': 'new_v', 'new_v_ssd_norm_w': 'new_v', 'new_v_ssd_w_out': 'new_v', 'new_v_gmlp_w_in': 'new_v', 'new_v_gmlp_b_in': 'new_v', 'new_v_gmlp_ln_w': 'new_v', 'new_v_gmlp_ln_b': 'new_v', 'new_v_gmlp_w_s': 'new_v', 'new_v_gmlp_b_s': 'new_v', 'new_v_gmlp_w_out': 'new_v', 'new_v_ffn_w_gate': 'new_v', 'new_v_ffn_w_up': 'new_v', 'new_v_ffn_w_down': 'new_v', 'new_v_ple_w_proj': 'new_v', 'new_v_ple_norm': 'new_v', 'new_v_ple_gate_norm': 'new_v', 'new_v_ple_w_gate': 'new_v', 'new_v_final_norm': 'new_v'}


def _forward(args):
    return _fwd_reference(*[args[k] for k in FWD_PARAMS])


def _output_shape():
    out = _jax.eval_shape(lambda: _forward(_fwd_setup_inputs(0)))
    return out.shape, out.dtype

N_MICROBATCH = 1
ADAM_LR = 0.001
ADAM_B1 = 0.9
ADAM_B2 = 0.999
ADAM_EPS = 1e-08
ADAM_WD = 0.01
ADAM_STEP = 10
PER_EXAMPLE_BATCH_AXIS = {'x': 0, 'p': 1, 'loss_target': 0}
SHARED_INPUTS = []
_WEIGHT_DTYPES = {'norm_mix': _jnp.float32, 'norm_ffn': _jnp.float32, 'ssd_w_in': _jnp.float32, 'ssd_conv_w': _jnp.float32, 'ssd_conv_b': _jnp.float32, 'ssd_dt_bias': _jnp.float32, 'ssd_a_log': _jnp.float32, 'ssd_d': _jnp.float32, 'ssd_norm_w': _jnp.float32, 'ssd_w_out': _jnp.float32, 'gmlp_w_in': _jnp.float32, 'gmlp_b_in': _jnp.float32, 'gmlp_ln_w': _jnp.float32, 'gmlp_ln_b': _jnp.float32, 'gmlp_w_s': _jnp.float32, 'gmlp_b_s': _jnp.float32, 'gmlp_w_out': _jnp.float32, 'ffn_w_gate': _jnp.float32, 'ffn_w_up': _jnp.float32, 'ffn_w_down': _jnp.float32, 'ple_w_proj': _jnp.float32, 'ple_norm': _jnp.float32, 'ple_gate_norm': _jnp.float32, 'ple_w_gate': _jnp.float32, 'final_norm': _jnp.float32}
MOMENT_SCALE = {'norm_mix': 1.584787e-01, 'norm_ffn': 8.615879e-02, 'ssd_w_in': 7.762106e-02, 'ssd_conv_w': 6.854438e-02, 'ssd_conv_b': 1.037003e-01, 'ssd_dt_bias': 1.952119e-01, 'ssd_a_log': 3.865906e-01, 'ssd_d': 4.236039e-01, 'ssd_norm_w': 9.560398e-02, 'ssd_w_out': 1.295488e-01, 'gmlp_w_in': 4.749765e-02, 'gmlp_b_in': 5.453172e-02, 'gmlp_ln_w': 3.265159e-02, 'gmlp_ln_b': 3.228630e-02, 'gmlp_w_s': 3.125731e-02, 'gmlp_b_s': 4.443068e-02, 'gmlp_w_out': 8.017988e-02, 'ffn_w_gate': 3.738649e-02, 'ffn_w_up': 3.630412e-02, 'ffn_w_down': 6.029297e-02, 'ple_w_proj': 5.419864e-02, 'ple_norm': 6.211145e-02, 'ple_gate_norm': 2.158702e-02, 'ple_w_gate': 2.131147e-02, 'final_norm': 3.207915e+01}


def _to_microbatches(a, axis):
    t = _jnp.moveaxis(a, axis, 0)
    t = t.reshape((N_MICROBATCH, t.shape[0] // N_MICROBATCH) + t.shape[1:])
    return _jnp.moveaxis(t, 1, axis + 1)


def setup_inputs(seed: int = 0) -> dict:
    inp = _fwd_setup_inputs(seed)
    key = _jax.random.fold_in(_jax.random.key(seed), 7919)
    shape, _ = _output_shape()
    out = dict(inp)
    out["loss_target"] = _jax.random.normal(_jax.random.fold_in(key, 0), shape, _jnp.float32)
    for i, name in enumerate(TWIN_WEIGHTS):
        w = inp[name].astype(_jnp.float32)
        if MOMENT_SCALE is None:
            s = _jnp.sqrt(_jnp.mean(_jnp.square(w)) + 1e-30)
        else:
            s = MOMENT_SCALE[name]
        km, kv = _jax.random.split(_jax.random.fold_in(key, i + 1))
        out[name] = w
        out["m_" + name] = s * _jax.random.normal(km, w.shape, _jnp.float32)
        out["v_" + name] = (s * s) * _jax.random.uniform(kv, w.shape, _jnp.float32, 0.5, 1.5)
    if N_MICROBATCH > 1:
        for name, axis in PER_EXAMPLE_BATCH_AXIS.items():
            out[name] = _to_microbatches(out[name], axis)
    return {'x': out['x'], 'p': out['p'], 'norm_mix': out['norm_mix'], 'norm_ffn': out['norm_ffn'], 'ssd_w_in': out['ssd_w_in'], 'ssd_conv_w': out['ssd_conv_w'], 'ssd_conv_b': out['ssd_conv_b'], 'ssd_dt_bias': out['ssd_dt_bias'], 'ssd_a_log': out['ssd_a_log'], 'ssd_d': out['ssd_d'], 'ssd_norm_w': out['ssd_norm_w'], 'ssd_w_out': out['ssd_w_out'], 'gmlp_w_in': out['gmlp_w_in'], 'gmlp_b_in': out['gmlp_b_in'], 'gmlp_ln_w': out['gmlp_ln_w'], 'gmlp_ln_b': out['gmlp_ln_b'], 'gmlp_w_s': out['gmlp_w_s'], 'gmlp_b_s': out['gmlp_b_s'], 'gmlp_w_out': out['gmlp_w_out'], 'ffn_w_gate': out['ffn_w_gate'], 'ffn_w_up': out['ffn_w_up'], 'ffn_w_down': out['ffn_w_down'], 'ple_w_proj': out['ple_w_proj'], 'ple_norm': out['ple_norm'], 'ple_gate_norm': out['ple_gate_norm'], 'ple_w_gate': out['ple_w_gate'], 'final_norm': out['final_norm'], 'loss_target': out['loss_target'], 'm_norm_mix': out['m_norm_mix'], 'm_norm_ffn': out['m_norm_ffn'], 'm_ssd_w_in': out['m_ssd_w_in'], 'm_ssd_conv_w': out['m_ssd_conv_w'], 'm_ssd_conv_b': out['m_ssd_conv_b'], 'm_ssd_dt_bias': out['m_ssd_dt_bias'], 'm_ssd_a_log': out['m_ssd_a_log'], 'm_ssd_d': out['m_ssd_d'], 'm_ssd_norm_w': out['m_ssd_norm_w'], 'm_ssd_w_out': out['m_ssd_w_out'], 'm_gmlp_w_in': out['m_gmlp_w_in'], 'm_gmlp_b_in': out['m_gmlp_b_in'], 'm_gmlp_ln_w': out['m_gmlp_ln_w'], 'm_gmlp_ln_b': out['m_gmlp_ln_b'], 'm_gmlp_w_s': out['m_gmlp_w_s'], 'm_gmlp_b_s': out['m_gmlp_b_s'], 'm_gmlp_w_out': out['m_gmlp_w_out'], 'm_ffn_w_gate': out['m_ffn_w_gate'], 'm_ffn_w_up': out['m_ffn_w_up'], 'm_ffn_w_down': out['m_ffn_w_down'], 'm_ple_w_proj': out['m_ple_w_proj'], 'm_ple_norm': out['m_ple_norm'], 'm_ple_gate_norm': out['m_ple_gate_norm'], 'm_ple_w_gate': out['m_ple_w_gate'], 'm_final_norm': out['m_final_norm'], 'v_norm_mix': out['v_norm_mix'], 'v_norm_ffn': out['v_norm_ffn'], 'v_ssd_w_in': out['v_ssd_w_in'], 'v_ssd_conv_w': out['v_ssd_conv_w'], 'v_ssd_conv_b': out['v_ssd_conv_b'], 'v_ssd_dt_bias': out['v_ssd_dt_bias'], 'v_ssd_a_log': out['v_ssd_a_log'], 'v_ssd_d': out['v_ssd_d'], 'v_ssd_norm_w': out['v_ssd_norm_w'], 'v_ssd_w_out': out['v_ssd_w_out'], 'v_gmlp_w_in': out['v_gmlp_w_in'], 'v_gmlp_b_in': out['v_gmlp_b_in'], 'v_gmlp_ln_w': out['v_gmlp_ln_w'], 'v_gmlp_ln_b': out['v_gmlp_ln_b'], 'v_gmlp_w_s': out['v_gmlp_w_s'], 'v_gmlp_b_s': out['v_gmlp_b_s'], 'v_gmlp_w_out': out['v_gmlp_w_out'], 'v_ffn_w_gate': out['v_ffn_w_gate'], 'v_ffn_w_up': out['v_ffn_w_up'], 'v_ffn_w_down': out['v_ffn_w_down'], 'v_ple_w_proj': out['v_ple_w_proj'], 'v_ple_norm': out['v_ple_norm'], 'v_ple_gate_norm': out['v_ple_gate_norm'], 'v_ple_w_gate': out['v_ple_w_gate'], 'v_final_norm': out['v_final_norm']}


def _loss(weights, diff, rest, loss_target):
    with _jax.named_scope("forward"):
        args = {**rest, TWIN_DIFF_INPUT: diff, **{k: w.astype(_WEIGHT_DTYPES[k]) for k, w in weights.items()}}
        y = _forward(args)
    with _jax.named_scope("loss_head"):
        err = _jnp.square(y.astype(_jnp.float32) - loss_target)
        return 0.5 * _jnp.sum(_jnp.mean(err, axis=-1)) if err.ndim else 0.5 * err


def _adamw(w, g, m, v):
    m = ADAM_B1 * m + (1.0 - ADAM_B1) * g
    v = ADAM_B2 * v + (1.0 - ADAM_B2) * _jnp.square(g)
    m_hat = m / (1.0 - ADAM_B1 ** ADAM_STEP)
    v_hat = v / (1.0 - ADAM_B2 ** ADAM_STEP)
    delta = -ADAM_LR * (m_hat / (_jnp.sqrt(v_hat) + ADAM_EPS) + ADAM_WD * w)
    return delta, m, v


def reference(x, p, norm_mix, norm_ffn, ssd_w_in, ssd_conv_w, ssd_conv_b, ssd_dt_bias, ssd_a_log, ssd_d, ssd_norm_w, ssd_w_out, gmlp_w_in, gmlp_b_in, gmlp_ln_w, gmlp_ln_b, gmlp_w_s, gmlp_b_s, gmlp_w_out, ffn_w_gate, ffn_w_up, ffn_w_down, ple_w_proj, ple_norm, ple_gate_norm, ple_w_gate, final_norm, loss_target, m_norm_mix, m_norm_ffn, m_ssd_w_in, m_ssd_conv_w, m_ssd_conv_b, m_ssd_dt_bias, m_ssd_a_log, m_ssd_d, m_ssd_norm_w, m_ssd_w_out, m_gmlp_w_in, m_gmlp_b_in, m_gmlp_ln_w, m_gmlp_ln_b, m_gmlp_w_s, m_gmlp_b_s, m_gmlp_w_out, m_ffn_w_gate, m_ffn_w_up, m_ffn_w_down, m_ple_w_proj, m_ple_norm, m_ple_gate_norm, m_ple_w_gate, m_final_norm, v_norm_mix, v_norm_ffn, v_ssd_w_in, v_ssd_conv_w, v_ssd_conv_b, v_ssd_dt_bias, v_ssd_a_log, v_ssd_d, v_ssd_norm_w, v_ssd_w_out, v_gmlp_w_in, v_gmlp_b_in, v_gmlp_ln_w, v_gmlp_ln_b, v_gmlp_w_s, v_gmlp_b_s, v_gmlp_w_out, v_ffn_w_gate, v_ffn_w_up, v_ffn_w_down, v_ple_w_proj, v_ple_norm, v_ple_gate_norm, v_ple_w_gate, v_final_norm):
    given = dict(x=x, p=p, norm_mix=norm_mix, norm_ffn=norm_ffn, ssd_w_in=ssd_w_in, ssd_conv_w=ssd_conv_w, ssd_conv_b=ssd_conv_b, ssd_dt_bias=ssd_dt_bias, ssd_a_log=ssd_a_log, ssd_d=ssd_d, ssd_norm_w=ssd_norm_w, ssd_w_out=ssd_w_out, gmlp_w_in=gmlp_w_in, gmlp_b_in=gmlp_b_in, gmlp_ln_w=gmlp_ln_w, gmlp_ln_b=gmlp_ln_b, gmlp_w_s=gmlp_w_s, gmlp_b_s=gmlp_b_s, gmlp_w_out=gmlp_w_out, ffn_w_gate=ffn_w_gate, ffn_w_up=ffn_w_up, ffn_w_down=ffn_w_down, ple_w_proj=ple_w_proj, ple_norm=ple_norm, ple_gate_norm=ple_gate_norm, ple_w_gate=ple_w_gate, final_norm=final_norm, loss_target=loss_target, m_norm_mix=m_norm_mix, m_norm_ffn=m_norm_ffn, m_ssd_w_in=m_ssd_w_in, m_ssd_conv_w=m_ssd_conv_w, m_ssd_conv_b=m_ssd_conv_b, m_ssd_dt_bias=m_ssd_dt_bias, m_ssd_a_log=m_ssd_a_log, m_ssd_d=m_ssd_d, m_ssd_norm_w=m_ssd_norm_w, m_ssd_w_out=m_ssd_w_out, m_gmlp_w_in=m_gmlp_w_in, m_gmlp_b_in=m_gmlp_b_in, m_gmlp_ln_w=m_gmlp_ln_w, m_gmlp_ln_b=m_gmlp_ln_b, m_gmlp_w_s=m_gmlp_w_s, m_gmlp_b_s=m_gmlp_b_s, m_gmlp_w_out=m_gmlp_w_out, m_ffn_w_gate=m_ffn_w_gate, m_ffn_w_up=m_ffn_w_up, m_ffn_w_down=m_ffn_w_down, m_ple_w_proj=m_ple_w_proj, m_ple_norm=m_ple_norm, m_ple_gate_norm=m_ple_gate_norm, m_ple_w_gate=m_ple_w_gate, m_final_norm=m_final_norm, v_norm_mix=v_norm_mix, v_norm_ffn=v_norm_ffn, v_ssd_w_in=v_ssd_w_in, v_ssd_conv_w=v_ssd_conv_w, v_ssd_conv_b=v_ssd_conv_b, v_ssd_dt_bias=v_ssd_dt_bias, v_ssd_a_log=v_ssd_a_log, v_ssd_d=v_ssd_d, v_ssd_norm_w=v_ssd_norm_w, v_ssd_w_out=v_ssd_w_out, v_gmlp_w_in=v_gmlp_w_in, v_gmlp_b_in=v_gmlp_b_in, v_gmlp_ln_w=v_gmlp_ln_w, v_gmlp_ln_b=v_gmlp_ln_b, v_gmlp_w_s=v_gmlp_w_s, v_gmlp_b_s=v_gmlp_b_s, v_gmlp_w_out=v_gmlp_w_out, v_ffn_w_gate=v_ffn_w_gate, v_ffn_w_up=v_ffn_w_up, v_ffn_w_down=v_ffn_w_down, v_ple_w_proj=v_ple_w_proj, v_ple_norm=v_ple_norm, v_ple_gate_norm=v_ple_gate_norm, v_ple_w_gate=v_ple_w_gate, v_final_norm=v_final_norm)
    weights = {n: given[n] for n in TWIN_WEIGHTS}
    shared = {n: given[n] for n in SHARED_INPUTS}
    per_example = {n: given[n] for n in ['x', 'p']}
    grad_fn = _jax.value_and_grad(_loss, argnums=(0, 1))

    def one_microbatch(ex, loss_target):
        ex = dict(ex)
        diff = ex.pop(TWIN_DIFF_INPUT)
        return grad_fn(weights, diff, {**shared, **ex}, loss_target)

    if N_MICROBATCH == 1:
        loss, (grad_w, grad_x) = one_microbatch(per_example, given["loss_target"])
    else:
        def body(carry, xs):
            loss_sum, grad_sum = carry
            l_k, (gw_k, gx_k) = one_microbatch(xs[0], xs[1])
            with _jax.named_scope("update"):
                return (loss_sum + l_k, _jax.tree.map(_jnp.add, grad_sum, gw_k)), gx_k

        init = (_jnp.zeros((), _jnp.float32), _jax.tree.map(_jnp.zeros_like, weights))
        (loss, grad_w), grad_x = _jax.lax.scan(body, init, (per_example, given["loss_target"]))
    with _jax.named_scope("update"):
        delta_w, new_m, new_v = {}, {}, {}
        for n in TWIN_WEIGHTS:
            delta_w[n], new_m[n], new_v[n] = _adamw(weights[n], grad_w[n], given["m_" + n], given["v_" + n])
    return (loss, grad_x, *[grad_w[n] for n in TWIN_WEIGHTS], *[delta_w[n] for n in TWIN_WEIGHTS],
            *[new_m[n] for n in TWIN_WEIGHTS], *[new_v[n] for n in TWIN_WEIGHTS])
```

```python
import functools
import math

import jax
import jax.numpy as jnp
from jax import lax
from jax.experimental import pallas as pl
from jax.experimental.pallas import tpu as pltpu

f32 = jnp.float32
bf16 = jnp.bfloat16
HI = lax.Precision.HIGHEST

LANES = 128
SUBLANES = 8
VMEM_LIMIT_BYTES = 56 * 1024 * 1024

HEADDIM = 64
STATE = 128
CHUNK = 128
CONV_K = 4
RMS_EPS = 1e-6
LN_EPS = 1e-5
ADAM_LR = 0.001
ADAM_B1 = 0.9
ADAM_B2 = 0.999
ADAM_EPS = 1e-08
ADAM_WD = 0.01
ADAM_STEP = 10

N_CHIPS = 4
N_DEV = 8
MESH = pl.DeviceIdType.MESH


def _cparams(sem):
    return pltpu.CompilerParams(dimension_semantics=sem, vmem_limit_bytes=VMEM_LIMIT_BYTES)


def _tile(n, want):
    if n <= want:
        return n
    t = want
    while n % t:
        t //= 2
    return t


def _row_spec(tm, c):
    return pl.BlockSpec((tm, c), lambda i: (i, 0))


def _full_spec(shape):
    nd = len(shape)
    return pl.BlockSpec(tuple(shape), lambda *_: (0,) * nd)


def _sigmoid(x):
    return 1.0 / (1.0 + jnp.exp(-x))


def _silu(x):
    return x * _sigmoid(x)


def _dsilu(x):
    s = _sigmoid(x)
    return s * (1.0 + x * (1.0 - s))


def _gelu(x):
    return 0.5 * x * (1.0 + lax.erf(x * (1.0 / math.sqrt(2.0))))


def _dgelu(x):
    return 0.5 * (1.0 + lax.erf(x * (1.0 / math.sqrt(2.0)))) + x * jnp.exp(-0.5 * x * x) * (1.0 / math.sqrt(2.0 * math.pi))


def _softplus(x):
    return jnp.maximum(x, 0.0) + jnp.log(1.0 + jnp.exp(-jnp.abs(x)))


def _rms(x, w, eps):
    r = lax.rsqrt(jnp.mean(x * x, axis=-1, keepdims=True) + eps)
    return x * r * w


def _rms_bwd(dy, x, w, eps):
    r = lax.rsqrt(jnp.mean(x * x, axis=-1, keepdims=True) + eps)
    xh = x * r
    g = dy * w
    dx = r * (g - xh * jnp.mean(g * xh, axis=-1, keepdims=True))
    dw = jnp.sum(dy * xh, axis=0, keepdims=True)
    return dx, dw


def _dot(a, b, dims=(((1,), (0,)), ((), ())), precision=None):
    return lax.dot_general(a, b, dims, precision=precision, preferred_element_type=f32)


NN = (((1,), (0,)), ((), ()))
NT = (((1,), (1,)), ((), ()))
TN = (((0,), (0,)), ((), ()))


def _mm(a, b, *, mode="nn", out_dtype=f32, res=None, kbatch=False, tm=1024, tn=512, tk=2048, name):
    a3, b3 = a.ndim == 3, b.ndim == 3
    nb = a.shape[0] if a3 else (b.shape[0] if b3 else 1)
    ash, bsh = a.shape[-2:], b.shape[-2:]
    if mode == "nn":
        M, K, N = ash[0], ash[1], bsh[1]
    elif mode == "nt":
        M, K, N = ash[0], ash[1], bsh[0]
    else:
        K, M, N = ash[0], ash[1], bsh[1]
    tm, tn, tk = _tile(M, tm), (N if N % LANES else _tile(N, tn)), (K if K % LANES else _tile(K, tk))
    nk = K // tk
    if kbatch:
        assert a3 and b3
        grid = (1, M // tm, N // tn, nb * nk)
        bi = lambda g, k: k // nk
        ki = lambda g, k: k % nk
    else:
        grid = (nb, M // tm, N // tn, nk)
        bi = lambda g, k: g
        ki = lambda g, k: k
    nsteps = grid[3]

    def spec(is3, blk, imap):
        if is3:
            return pl.BlockSpec((None,) + blk, lambda g, i, j, k: (bi(g, k),) + imap(i, j, ki(g, k)))
        return pl.BlockSpec(blk, lambda g, i, j, k: imap(i, j, ki(g, k)))

    if mode == "nn":
        a_spec = spec(a3, (tm, tk), lambda i, j, k: (i, k))
        b_spec = spec(b3, (tk, tn), lambda i, j, k: (k, j))
        dims = NN
    elif mode == "nt":
        a_spec = spec(a3, (tm, tk), lambda i, j, k: (i, k))
        b_spec = spec(b3, (tn, tk), lambda i, j, k: (j, k))
        dims = NT
    else:
        a_spec = spec(a3, (tk, tm), lambda i, j, k: (k, i))
        b_spec = spec(b3, (tk, tn), lambda i, j, k: (k, j))
        dims = TN
    out3 = (a3 or b3) and not kbatch
    if out3:
        o_spec = pl.BlockSpec((None, tm, tn), lambda g, i, j, k: (g, i, j))
        o_shape = jax.ShapeDtypeStruct((nb, M, N), out_dtype)
    else:
        o_spec = pl.BlockSpec((tm, tn), lambda g, i, j, k: (i, j))
        o_shape = jax.ShapeDtypeStruct((M, N), out_dtype)
    in_specs = [a_spec, b_spec]
    args = [a, b]
    if res is not None:
        in_specs.append(pl.BlockSpec((tm, tn), lambda g, i, j, k: (i, j)))
        args.append(res)

    def body(*refs):
        if res is not None:
            a_ref, b_ref, r_ref, o_ref, acc_ref = refs
        else:
            a_ref, b_ref, o_ref, acc_ref = refs
        k = pl.program_id(3)

        @pl.when(k == 0)
        def _():
            acc_ref[...] = jnp.zeros_like(acc_ref)

        acc_ref[...] += _dot(a_ref[...].astype(bf16), b_ref[...].astype(bf16), dims)

        @pl.when(k == nsteps - 1)
        def _():
            r = acc_ref[...]
            if res is not None:
                r = r + r_ref[...]
            o_ref[...] = r.astype(o_ref.dtype)

    return pl.pallas_call(
        body, name=name, grid=grid, in_specs=in_specs, out_specs=o_spec, out_shape=o_shape,
        scratch_shapes=[pltpu.VMEM((tm, tn), f32)],
        compiler_params=_cparams(("parallel", "parallel", "parallel", "arbitrary")),
    )(*args)


def _rowcall(fn, *, name, rows, fulls, out_rows, out_accs=(), tm=512):
    S = rows[0].shape[0]
    tm = _tile(S, tm)
    n_r, n_f, n_or, n_oa = len(rows), len(fulls), len(out_rows), len(out_accs)

    def body(*refs):
        ins = [r[...] for r in refs[:n_r + n_f]]
        outs = fn(*ins)
        if not isinstance(outs, (tuple, list)):
            outs = (outs,)
        o_refs = refs[n_r + n_f:]
        for o_ref, v in zip(o_refs[:n_or], outs[:n_or]):
            o_ref[...] = v.astype(o_ref.dtype)
        if n_oa:
            first = pl.program_id(0) == 0

            @pl.when(first)
            def _():
                for o_ref, v in zip(o_refs[n_or:], outs[n_or:]):
                    o_ref[...] = v

            @pl.when(jnp.logical_not(first))
            def _():
                for o_ref, v in zip(o_refs[n_or:], outs[n_or:]):
                    o_ref[...] += v

    in_specs = [_row_spec(tm, r.shape[1]) for r in rows] + [_full_spec(f.shape) for f in fulls]
    out_specs = [_row_spec(tm, c) for c, _ in out_rows] + [_full_spec(s) for s in out_accs]
    out_shape = [jax.ShapeDtypeStruct((S, c), d) for c, d in out_rows] + [jax.ShapeDtypeStruct(s, f32) for s in out_accs]
    res = pl.pallas_call(
        body, name=name, grid=(S // tm,), in_specs=in_specs, out_specs=out_specs, out_shape=out_shape,
        compiler_params=_cparams(("arbitrary",) if n_oa else ("parallel",)),
    )(*rows, *fulls)
    return res


def _row2(v):
    return v.reshape(1, -1)


def _rms_fwd(h, w, name):
    D = h.shape[1]
    return _rowcall(lambda x, w_: _rms(x, w_, RMS_EPS), name=name, rows=[h], fulls=[_row2(w)], out_rows=[(D, bf16)])[0]


def _conv_fwd(xpre, w, b, name):
    S, C = xpre.shape
    tm, tc = _tile(S, 512), _tile(C, 1024)
    hb = tm // SUBLANES

    def body(x_ref, halo_ref, w_ref, b_ref, c_ref, o_ref):
        i = pl.program_id(1)
        x = x_ref[...]
        halo = jnp.where(i > 0, halo_ref[...], 0.0)
        row = lax.broadcasted_iota(jnp.int32, x.shape, 0)
        row8 = lax.broadcasted_iota(jnp.int32, halo.shape, 0)
        x0 = x[0:SUBLANES, :]
        acc = x * w_ref[CONV_K - 1:CONV_K, :] + b_ref[...]
        acc0 = x0 * w_ref[CONV_K - 1:CONV_K, :] + b_ref[...]
        for k in range(1, CONV_K):
            wk = w_ref[CONV_K - 1 - k:CONV_K - k, :]
            acc = acc + pltpu.roll(x, k, axis=0) * wk
            acc0 = acc0 + jnp.where(row8 < k, pltpu.roll(halo, k, axis=0), pltpu.roll(x0, k, axis=0)) * wk
        c_ref[...] = acc
        o_ref[...] = _silu(acc)
        c_ref[0:SUBLANES, :] = acc0
        o_ref[0:SUBLANES, :] = _silu(acc0)

    return pl.pallas_call(
        body, name=name, grid=(C // tc, S // tm),
        in_specs=[pl.BlockSpec((tm, tc), lambda j, i: (i, j)),
                  pl.BlockSpec((SUBLANES, tc), lambda j, i: (jnp.maximum(i * hb - 1, 0), j)),
                  pl.BlockSpec((CONV_K, tc), lambda j, i: (0, j)),
                  pl.BlockSpec((1, tc), lambda j, i: (0, j))],
        out_specs=[pl.BlockSpec((tm, tc), lambda j, i: (i, j))] * 2,
        out_shape=[jax.ShapeDtypeStruct((S, C), f32)] * 2,
        compiler_params=_cparams(("parallel", "parallel")),
    )(xpre, xpre, w, _row2(b))


def _conv_bwd_dc(dxbc, c, xpre, name):
    S, C = xpre.shape
    tm, tc = _tile(S, 512), _tile(C, 1024)
    hb = tm // SUBLANES

    def body(d_ref, c_ref, x_ref, halo_ref, dc_ref, dw_ref, db_ref):
        i = pl.program_id(1)
        x = x_ref[...]
        dc = d_ref[...] * _dsilu(c_ref[...])
        dc_ref[...] = dc
        halo = jnp.where(i > 0, halo_ref[...], 0.0)
        row = lax.broadcasted_iota(jnp.int32, x.shape, 0)
        row8 = lax.broadcasted_iota(jnp.int32, halo.shape, 0)
        x0 = x[0:SUBLANES, :]
        dc0 = dc[0:SUBLANES, :]
        parts = [jnp.sum(dc * x, axis=0, keepdims=True)]
        for k in range(1, CONV_K):
            xs_big = jnp.where(row < SUBLANES, 0.0, pltpu.roll(x, k, axis=0))
            xs0 = jnp.where(row8 < k, pltpu.roll(halo, k, axis=0), pltpu.roll(x0, k, axis=0))
            parts.append(jnp.sum(dc * xs_big, axis=0, keepdims=True) + jnp.sum(dc0 * xs0, axis=0, keepdims=True))
        dw = jnp.concatenate([parts[CONV_K - 1 - k] for k in range(CONV_K)] + [jnp.zeros((SUBLANES - CONV_K, x.shape[1]), f32)], axis=0)
        db = jnp.sum(dc, axis=0, keepdims=True)

        @pl.when(i == 0)
        def _():
            dw_ref[...] = dw
            db_ref[...] = db

        @pl.when(i > 0)
        def _():
            dw_ref[...] += dw
            db_ref[...] += db

    return pl.pallas_call(
        body, name=name, grid=(C // tc, S // tm),
        in_specs=[pl.BlockSpec((tm, tc), lambda j, i: (i, j))] * 3 +
                 [pl.BlockSpec((SUBLANES, tc), lambda j, i: (jnp.maximum(i * hb - 1, 0), j))],
        out_specs=[pl.BlockSpec((tm, tc), lambda j, i: (i, j)),
                   pl.BlockSpec((SUBLANES, tc), lambda j, i: (0, j)),
                   pl.BlockSpec((1, tc), lambda j, i: (0, j))],
        out_shape=[jax.ShapeDtypeStruct((S, C), f32), jax.ShapeDtypeStruct((SUBLANES, C), f32), jax.ShapeDtypeStruct((1, C), f32)],
        compiler_params=_cparams(("parallel", "arbitrary")),
    )(dxbc, c, xpre, xpre)


def _conv_bwd_dx(dc, w, name):
    S, C = dc.shape
    tm, tc = _tile(S, 512), _tile(C, 1024)
    hb = tm // SUBLANES
    nrow = S // tm
    last8 = S // SUBLANES - 1

    def body(d_ref, nxt_ref, w_ref, o_ref):
        i = pl.program_id(1)
        d = d_ref[...]
        nxt = jnp.where(i < nrow - 1, nxt_ref[...], 0.0)
        row8 = lax.broadcasted_iota(jnp.int32, nxt.shape, 0)
        dl = d[tm - SUBLANES:tm, :]
        acc = d * w_ref[CONV_K - 1:CONV_K, :]
        accl = dl * w_ref[CONV_K - 1:CONV_K, :]
        for j in range(1, CONV_K):
            wk = w_ref[CONV_K - 1 - j:CONV_K - j, :]
            acc = acc + pltpu.roll(d, tm - j, axis=0) * wk
            accl = accl + jnp.where(row8 >= SUBLANES - j, pltpu.roll(nxt, SUBLANES - j, axis=0), pltpu.roll(dl, SUBLANES - j, axis=0)) * wk
        o_ref[...] = acc.astype(o_ref.dtype)
        o_ref[tm - SUBLANES:tm, :] = accl.astype(o_ref.dtype)

    return pl.pallas_call(
        body, name=name, grid=(C // tc, nrow),
        in_specs=[pl.BlockSpec((tm, tc), lambda j, i: (i, j)),
                  pl.BlockSpec((SUBLANES, tc), lambda j, i: (jnp.minimum((i + 1) * hb, last8), j)),
                  pl.BlockSpec((CONV_K, tc), lambda j, i: (0, j))],
        out_specs=pl.BlockSpec((tm, tc), lambda j, i: (i, j)),
        out_shape=jax.ShapeDtypeStruct((S, C), f32),
        compiler_params=_cparams(("parallel", "parallel")),
    )(dc, dc, w)


def _halfsum(v, lane_lo):
    s0 = jnp.sum(jnp.where(lane_lo, v, 0.0), axis=1, keepdims=True)
    s1 = jnp.sum(jnp.where(lane_lo, 0.0, v), axis=1, keepdims=True)
    return jnp.where(lane_lo, s0, s1)


def _ssd_specs(S, inner, GN, nchunks, rev):
    L = CHUNK
    cm = (lambda c: nchunks - 1 - c) if rev else (lambda c: c)
    xs = pl.BlockSpec((L, inner), lambda c: (cm(c), 0))
    bb = pl.BlockSpec((L, GN), lambda c: (cm(c), inner // GN))
    cc = pl.BlockSpec((L, GN), lambda c: (cm(c), inner // GN + 1))
    row = pl.BlockSpec((L, inner), lambda c: (cm(c), 0))
    vec = pl.BlockSpec((1, inner), lambda c: (0, 0))
    st = pl.BlockSpec((None, inner, STATE), lambda c: (cm(c), 0, 0))
    return xs, bb, cc, row, vec, st


def _ssd_fwd(xbc, dtx, ax, dx, G, name):
    S, inner = dtx.shape
    GN = G * STATE
    L = CHUNK
    nchunks = S // L
    npairs = inner // LANES
    ppg = npairs // G
    assert inner % GN == 0 and L == LANES and STATE == LANES

    def body(xs_ref, b_ref, c_ref, dtx_ref, ax_ref, dx_ref, y_ref, so_ref, st_ref):
        ci = pl.program_id(0)

        @pl.when(ci == 0)
        def _():
            st_ref[...] = jnp.zeros_like(st_ref)

        r = lax.broadcasted_iota(jnp.int32, (L, L), 0)
        cidx = lax.broadcasted_iota(jnp.int32, (L, L), 1)
        tril = cidx <= r
        lane_lo = cidx < HEADDIM
        xs = xs_ref[...]
        dtv = dtx_ref[...]
        X = xs * dtv
        da = dtv * ax_ref[...]
        cs = _dot(tril.astype(f32), da, NN, HI)
        cs_last = jnp.sum(da, axis=0, keepdims=True)
        so_ref[...] = st_ref[...]
        for g in range(G):
            Bg = b_ref[:, g * STATE:(g + 1) * STATE].astype(bf16)
            Cg = c_ref[:, g * STATE:(g + 1) * STATE].astype(bf16)
            CB = _dot(Cg, Bg, NT)
            for j in range(ppg):
                lo = (g * ppg + j) * LANES
                tile = cs[:, lo:lo + LANES]
                rl = pltpu.roll(tile, HEADDIM, axis=1)
                Xp = X[:, lo:lo + LANES]
                prev = st_ref[lo:lo + LANES, :]
                ypair = _dot(Cg, prev.astype(bf16), NT) * jnp.exp(tile)
                for half in (0, 1):
                    hm = lane_lo if half == 0 else jnp.logical_not(lane_lo)
                    colb = jnp.where(hm, tile, rl)
                    Lm = jnp.exp(jnp.where(tril, colb - colb.T, -1e30))
                    W = (CB * Lm).astype(bf16)
                    ypair = ypair + _dot(W, jnp.where(hm, Xp, 0.0).astype(bf16), NN)
                y_ref[:, lo:lo + LANES] = ypair + xs[:, lo:lo + LANES] * dx_ref[:, lo:lo + LANES]
                last = cs_last[:, lo:lo + LANES]
                snew = _dot((Xp * jnp.exp(last - tile)).astype(bf16), Bg, TN)
                dec_rows = jnp.broadcast_to(jnp.exp(last), (L, LANES)).T
                st_ref[lo:lo + LANES, :] = dec_rows * prev + snew

    xs_s, b_s, c_s, row_s, vec_s, st_s = _ssd_specs(S, inner, GN, nchunks, False)
    return pl.pallas_call(
        body, name=name, grid=(nchunks,),
        in_specs=[xs_s, b_s, c_s, row_s, vec_s, vec_s],
        out_specs=[row_s, st_s],
        out_shape=[jax.ShapeDtypeStruct((S, inner), f32), jax.ShapeDtypeStruct((nchunks, inner, STATE), f32)],
        scratch_shapes=[pltpu.VMEM((inner, STATE), f32)],
        compiler_params=_cparams(("arbitrary",)),
    )(xbc, xbc, xbc, dtx, ax, dx)


def _ssd_bwd(dy, xbc, dtx, ax, dx, states, et, G, name):
    S, inner = dtx.shape
    H = et.shape[1]
    GN = G * STATE
    Cc = inner + 2 * GN
    L = CHUNK
    nchunks = S // L
    npairs = inner // LANES
    ppg = npairs // G

    def body(dy_ref, xs_ref, b_ref, c_ref, dtx_ref, ax_ref, dx_ref, si_ref, et_ref,
             dxbc_ref, ddt_ref, dax_ref, ddx_ref, dst_ref, dA_ref, dAl_ref, ddtp_ref):
        ci = pl.program_id(0)

        @pl.when(ci == 0)
        def _():
            dst_ref[...] = jnp.zeros_like(dst_ref)
            dax_ref[...] = jnp.zeros_like(dax_ref)
            ddx_ref[...] = jnp.zeros_like(ddx_ref)

        r = lax.broadcasted_iota(jnp.int32, (L, L), 0)
        cidx = lax.broadcasted_iota(jnp.int32, (L, L), 1)
        tril = cidx <= r
        lane_lo = cidx < HEADDIM
        lane_lo1 = lax.broadcasted_iota(jnp.int32, (1, LANES), 1) < HEADDIM
        xs = xs_ref[...]
        dtv = dtx_ref[...]
        dyv = dy_ref[...]
        X = xs * dtv
        da = dtv * ax_ref[...]
        cs = _dot(tril.astype(f32), da, NN, HI)
        cs_last = jnp.sum(da, axis=0, keepdims=True)
        ones8 = jnp.ones((SUBLANES, STATE), f32)
        for g in range(G):
            Bg = b_ref[:, g * STATE:(g + 1) * STATE].astype(bf16)
            Cg = c_ref[:, g * STATE:(g + 1) * STATE].astype(bf16)
            CB = _dot(Cg, Bg, NT)
            dCB = jnp.zeros((L, L), f32)
            dBg = jnp.zeros((L, STATE), f32)
            dCg = jnp.zeros((L, STATE), f32)
            for j in range(ppg):
                lo = (g * ppg + j) * LANES
                tile = cs[:, lo:lo + LANES]
                rl = pltpu.roll(tile, HEADDIM, axis=1)
                eA = jnp.exp(tile)
                Xp = X[:, lo:lo + LANES]
                dYp = dyv[:, lo:lo + LANES]
                xsp = xs[:, lo:lo + LANES]
                prev = si_ref[lo:lo + LANES, :]
                dSn = dst_ref[lo:lo + LANES, :]
                prev_b = prev.astype(bf16)
                dSn_b = dSn.astype(bf16)
                yoff = _dot(Cg, prev_b, NT) * eA
                dA_t = _halfsum(dYp * yoff, lane_lo)
                dYe = (dYp * eA).astype(bf16)
                dCg = dCg + _dot(dYe, prev_b, NN)
                dprev = _dot(dYe, Cg, TN)
                last = cs_last[:, lo:lo + LANES]
                w = jnp.exp(last - tile)
                BdS = _dot(Bg, dSn_b, NT)
                Xw = Xp * w
                dXp = w * BdS
                Gt = _halfsum(Xw * BdS, lane_lo)
                dA_t = dA_t - Gt
                dAl_t = jnp.sum(Gt, axis=0, keepdims=True)
                dBg = dBg + _dot(Xw.astype(bf16), dSn_b, NN)
                dec_rows = jnp.broadcast_to(jnp.exp(last), (L, LANES)).T
                dprev = dprev + dec_rows * dSn
                v = _dot(ones8, dSn * prev * dec_rows, NT, HI)
                dAl_t = dAl_t + _halfsum(jnp.sum(v, axis=0, keepdims=True) * (1.0 / SUBLANES), lane_lo1)
                for half in (0, 1):
                    hm = lane_lo if half == 0 else jnp.logical_not(lane_lo)
                    colb = jnp.where(hm, tile, rl)
                    Lm = jnp.exp(jnp.where(tril, colb - colb.T, -1e30))
                    W = CB * Lm
                    Xh = jnp.where(hm, Xp, 0.0).astype(bf16)
                    dYh = jnp.where(hm, dYp, 0.0).astype(bf16)
                    dW = _dot(dYh, Xh, NT)
                    dXp = dXp + _dot(W.astype(bf16), dYh, TN)
                    E = dW * W
                    dCB = dCB + dW * Lm
                    rs = jnp.sum(E, axis=1, keepdims=True)
                    ccol = jnp.broadcast_to(jnp.sum(E, axis=0, keepdims=True), (L, L)).T
                    dA_t = dA_t + jnp.where(hm, rs - ccol, 0.0)
                dxbc_ref[:, lo:lo + LANES] = dXp * dtv[:, lo:lo + LANES] + dYp * dx_ref[:, lo:lo + LANES]
                ddtp_ref[:, lo:lo + LANES] = dXp * xsp
                ddx_ref[:, lo:lo + LANES] += jnp.sum(dYp * xsp, axis=0, keepdims=True)
                dA_ref[:, lo:lo + LANES] = dA_t
                dAl_ref[:, lo:lo + LANES] = dAl_t
                dst_ref[lo:lo + LANES, :] = dprev
            dCBb = dCB.astype(bf16)
            dxbc_ref[:, inner + g * STATE:inner + (g + 1) * STATE] = dBg + _dot(dCBb, Cg, TN)
            dxbc_ref[:, inner + GN + g * STATE:inner + GN + (g + 1) * STATE] = dCg + _dot(dCBb, Bg, NN)
        triu = (cidx >= r).astype(f32)
        dda = _dot(triu, dA_ref[...], NN, HI) + dAl_ref[...]
        ddt_full = ddtp_ref[...] + dda * ax_ref[...] * (1.0 / HEADDIM)
        ddt_ref[...] = _dot(ddt_full, et_ref[...], NN, HI)
        dax_ref[...] += jnp.sum(dda * dtv, axis=0, keepdims=True)

    xs_s, b_s, c_s, row_s, vec_s, st_s = _ssd_specs(S, inner, GN, nchunks, True)
    return pl.pallas_call(
        body, name=name, grid=(nchunks,),
        in_specs=[row_s, xs_s, b_s, c_s, row_s, vec_s, vec_s, st_s, _full_spec(et.shape)],
        out_specs=[pl.BlockSpec((L, Cc), lambda c: (nchunks - 1 - c, 0)),
                   pl.BlockSpec((L, H), lambda c: (nchunks - 1 - c, 0)), vec_s, vec_s],
        out_shape=[jax.ShapeDtypeStruct((S, Cc), f32), jax.ShapeDtypeStruct((S, H), f32),
                   jax.ShapeDtypeStruct((1, inner), f32), jax.ShapeDtypeStruct((1, inner), f32)],
        scratch_shapes=[pltpu.VMEM((inner, STATE), f32), pltpu.VMEM((L, inner), f32),
                        pltpu.VMEM((1, inner), f32), pltpu.VMEM((L, inner), f32)],
        compiler_params=_cparams(("arbitrary",)),
    )(dy, xbc, xbc, xbc, dtx, ax, dx, states, et)


def _dt_fwd(dt_pre, bias, e, name):
    H, inner = e.shape

    def fn(dp, b, e_):
        dt = _softplus(dp + b)
        return dt, _dot(dt, e_, NN, HI)

    return _rowcall(fn, name=name, rows=[dt_pre], fulls=[_row2(bias), e], out_rows=[(H, f32), (inner, f32)])


def _dt_bwd(ddt, dt_pre, bias, name):
    H = ddt.shape[1]

    def fn(dd, dp, b):
        g = dd * _sigmoid(dp + b)
        return g, jnp.sum(g, axis=0, keepdims=True)

    return _rowcall(fn, name=name, rows=[ddt, dt_pre], fulls=[_row2(bias)], out_rows=[(H, f32)], out_accs=[(1, H)])


def _gnorm_fwd(y, z, w, G, name):
    inner = y.shape[1]
    gs = inner // G

    def fn(y_, z_, w_):
        gg = y_ * _silu(z_)
        outs = []
        for g in range(G):
            sl = slice(g * gs, (g + 1) * gs)
            outs.append(_rms(gg[:, sl], w_[:, sl], LN_EPS))
        return jnp.concatenate(outs, axis=1)

    return _rowcall(fn, name=name, rows=[y, z], fulls=[_row2(w)], out_rows=[(inner, bf16)], tm=256)[0]


def _gnorm_bwd(dyn, y, z, w, G, name):
    inner = y.shape[1]
    gs = inner // G

    def fn(d_, y_, z_, w_):
        sz = _silu(z_)
        gg = y_ * sz
        dgs, dws = [], []
        for g in range(G):
            sl = slice(g * gs, (g + 1) * gs)
            dg, dw = _rms_bwd(d_[:, sl], gg[:, sl], w_[:, sl], LN_EPS)
            dgs.append(dg)
            dws.append(dw)
        dgg = jnp.concatenate(dgs, axis=1)
        return dgg * sz, dgg * y_ * _dsilu(z_), jnp.concatenate(dws, axis=1)

    return _rowcall(fn, name=name, rows=[dyn, y, z], fulls=[_row2(w)], out_rows=[(inner, f32), (inner, f32)],
                    out_accs=[(1, inner)], tm=256)


def _gmlp_parts(pre, lw, lb, I):
    hp = _gelu(pre)
    uu = hp[:, :I]
    vp = hp[:, I:]
    xc = vp - jnp.mean(vp, axis=-1, keepdims=True)
    rstd = lax.rsqrt(jnp.mean(xc * xc, axis=-1, keepdims=True) + LN_EPS)
    vhat = xc * rstd
    return uu, vhat, rstd, vhat * lw + lb


def _gmlp_mid_fwd(pre, b_in, ln_w, ln_b, w_s, bsx, name):
    S, two_i = pre.shape
    I = two_i // 2
    NG = w_s.shape[0]
    gd = I // NG
    L = CHUNK

    def body(pre_ref, bi_ref, lw_ref, lb_ref, ws_ref, bsx_ref, o_ref):
        uu, _, _, vv = _gmlp_parts(pre_ref[...] + bi_ref[...], lw_ref[...], lb_ref[...], I)
        r = lax.broadcasted_iota(jnp.int32, (L, L), 0)
        cidx = lax.broadcasted_iota(jnp.int32, (L, L), 1)
        tril = cidx <= r
        for g in range(NG):
            sl = slice(g * gd, (g + 1) * gd)
            wg = jnp.where(tril, ws_ref[g], 0.0).astype(bf16)
            mixed = _dot(wg, vv[:, sl].astype(bf16), NN) + bsx_ref[:, sl]
            o_ref[:, sl] = (uu[:, sl] * mixed).astype(o_ref.dtype)

    return pl.pallas_call(
        body, name=name, grid=(S // L,),
        in_specs=[_row_spec(L, two_i), _full_spec((1, two_i)), _full_spec((1, I)), _full_spec((1, I)), _full_spec(w_s.shape), _full_spec(bsx.shape)],
        out_specs=_row_spec(L, I), out_shape=jax.ShapeDtypeStruct((S, I), bf16),
        compiler_params=_cparams(("parallel",)),
    )(pre, _row2(b_in), _row2(ln_w), _row2(ln_b), w_s, bsx)


def _gmlp_mid_bwd(do, pre, b_in, ln_w, ln_b, w_s, bsx, name):
    S, two_i = pre.shape
    I = two_i // 2
    NG = w_s.shape[0]
    gd = I // NG
    L = CHUNK

    def body(do_ref, pre_ref, bi_ref, lw_ref, lb_ref, ws_ref, bsx_ref, dpre_ref, dbi_ref, dlw_ref, dlb_ref, dws_ref, dbs_ref, dvv_ref):
        ci = pl.program_id(0)

        @pl.when(ci == 0)
        def _():
            for ref in (dbi_ref, dlw_ref, dlb_ref, dws_ref, dbs_ref):
                ref[...] = jnp.zeros_like(ref)

        pre = pre_ref[...] + bi_ref[...]
        lw = lw_ref[...]
        uu, vhat, rstd, vv = _gmlp_parts(pre, lw, lb_ref[...], I)
        dov = do_ref[...]
        r = lax.broadcasted_iota(jnp.int32, (L, L), 0)
        cidx = lax.broadcasted_iota(jnp.int32, (L, L), 1)
        tril = cidx <= r
        duus = []
        for g in range(NG):
            sl = slice(g * gd, (g + 1) * gd)
            wg = jnp.where(tril, ws_ref[g], 0.0).astype(bf16)
            vg = vv[:, sl].astype(bf16)
            mixed = _dot(wg, vg, NN) + bsx_ref[:, sl]
            duus.append(dov[:, sl] * mixed)
            dmixed = dov[:, sl] * uu[:, sl]
            dbs_ref[:, sl] += dmixed
            dmb = dmixed.astype(bf16)
            dvv_ref[:, sl] = _dot(wg, dmb, TN)
            dws_ref[g] += jnp.where(tril, _dot(dmb, vg, NT), 0.0)
        duu = jnp.concatenate(duus, axis=1)
        dvv = dvv_ref[...]
        dlw_ref[...] += jnp.sum(dvv * vhat, axis=0, keepdims=True)
        dlb_ref[...] += jnp.sum(dvv, axis=0, keepdims=True)
        dvh = dvv * lw
        dvp = rstd * (dvh - jnp.mean(dvh, axis=-1, keepdims=True) - vhat * jnp.mean(dvh * vhat, axis=-1, keepdims=True))
        dpre = jnp.concatenate([duu, dvp], axis=1) * _dgelu(pre)
        dbi_ref[...] += jnp.sum(dpre, axis=0, keepdims=True)
        dpre_ref[...] = dpre.astype(dpre_ref.dtype)

    return pl.pallas_call(
        body, name=name, grid=(S // L,),
        in_specs=[_row_spec(L, I), _row_spec(L, two_i), _full_spec((1, two_i)), _full_spec((1, I)), _full_spec((1, I)),
                  _full_spec(w_s.shape), _full_spec(bsx.shape)],
        out_specs=[_row_spec(L, two_i), _full_spec((1, two_i)), _full_spec((1, I)), _full_spec((1, I)), _full_spec(w_s.shape), _full_spec((L, I))],
        out_shape=[jax.ShapeDtypeStruct((S, two_i), bf16), jax.ShapeDtypeStruct((1, two_i), f32), jax.ShapeDtypeStruct((1, I), f32),
                   jax.ShapeDtypeStruct((1, I), f32), jax.ShapeDtypeStruct(w_s.shape, f32), jax.ShapeDtypeStruct((L, I), f32)],
        scratch_shapes=[pltpu.VMEM((L, I), f32)],
        compiler_params=_cparams(("arbitrary",)),
    )(do, pre, _row2(b_in), _row2(ln_w), _row2(ln_b), w_s, bsx)


def _lane_group_sum(acc, eg, name):
    NG = eg.shape[1]
    return _rowcall(lambda a, e: _dot(a, e, NN, HI), name=name, rows=[acc], fulls=[eg], out_rows=[(NG, f32)])[0]


def _ffn_up(u, wg, wu, name):
    S, D = u.shape
    nb, _, F4 = wg.shape
    tm = _tile(S, 1024)

    def body(u_ref, wg_ref, wu_ref, g_ref, up_ref, a_ref):
        uv = u_ref[...]
        g = _dot(uv, wg_ref[...], NN)
        up = _dot(uv, wu_ref[...], NN)
        g_ref[...] = g
        up_ref[...] = up
        a_ref[...] = (_silu(g) * up).astype(a_ref.dtype)

    wspec = pl.BlockSpec((None, D, F4), lambda k, i: (k, 0, 0))
    ospec = pl.BlockSpec((None, tm, F4), lambda k, i: (k, i, 0))
    return pl.pallas_call(
        body, name=name, grid=(nb, S // tm),
        in_specs=[pl.BlockSpec((tm, D), lambda k, i: (i, 0)), wspec, wspec],
        out_specs=[ospec, ospec, ospec],
        out_shape=[jax.ShapeDtypeStruct((nb, S, F4), f32), jax.ShapeDtypeStruct((nb, S, F4), f32), jax.ShapeDtypeStruct((nb, S, F4), bf16)],
        compiler_params=_cparams(("parallel", "parallel")),
    )(u, wg, wu)


def _ffn_bwd_act(dh, wd, G, U, name):
    S, D = dh.shape
    nb, F4, _ = wd.shape
    tm = _tile(S, 1024)

    def body(dh_ref, wd_ref, g_ref, up_ref, dg_ref, du_ref):
        dA = _dot(dh_ref[...].astype(bf16), wd_ref[...], NT)
        g = g_ref[...]
        dg_ref[...] = (dA * up_ref[...] * _dsilu(g)).astype(dg_ref.dtype)
        du_ref[...] = (dA * _silu(g)).astype(du_ref.dtype)

    ospec = pl.BlockSpec((None, tm, F4), lambda k, i: (k, i, 0))
    return pl.pallas_call(
        body, name=name, grid=(nb, S // tm),
        in_specs=[pl.BlockSpec((tm, D), lambda k, i: (i, 0)), pl.BlockSpec((None, F4, D), lambda k, i: (k, 0, 0)), ospec, ospec],
        out_specs=[ospec, ospec],
        out_shape=[jax.ShapeDtypeStruct((nb, S, F4), bf16)] * 2,
        compiler_params=_cparams(("parallel", "parallel")),
    )(dh, wd, G, U)


def _rms_bwd_add(dres, du, h, w, name):
    D = h.shape[1]

    def fn(dr, du_, h_, w_):
        dx, dw = _rms_bwd(du_, h_, w_, RMS_EPS)
        return dr + dx, dw

    return _rowcall(fn, name=name, rows=[dres, du, h], fulls=[_row2(w)], out_rows=[(D, f32)], out_accs=[(1, D)])


def _ple_fwd(h, p_i, wp, pn, gn, wgate, name):
    D = h.shape[1]

    def fn(h_, p_, wp_, pn_, gn_, wg_):
        pe = _dot(p_.astype(bf16), wp_, NN)
        e = _rms(pe, pn_, RMS_EPS)
        q = _rms(h_, gn_, RMS_EPS)
        gate = _sigmoid(_dot(q.astype(bf16), wg_, NN))
        return h_ + gate * e, pe, gate

    return _rowcall(fn, name=name, rows=[h, p_i], fulls=[wp, _row2(pn), _row2(gn), wgate],
                    out_rows=[(D, f32), (D, f32), (D, f32)], tm=256)


def _ple_bwd(dh3, h, pe, gate, pn, gn, wgate, name):
    D = h.shape[1]

    def fn(d_, h_, pe_, gate_, pn_, gn_, wg_):
        e = _rms(pe_, pn_, RMS_EPS)
        dzg = d_ * e * gate_ * (1.0 - gate_)
        dq = _dot(dzg.astype(bf16), wg_, NT)
        dxq, dgn = _rms_bwd(dq, h_, gn_, RMS_EPS)
        dpe, dpn = _rms_bwd(d_ * gate_, pe_, pn_, RMS_EPS)
        return d_ + dxq, dzg, dpe, _rms(h_, gn_, RMS_EPS), dpn, dgn

    return _rowcall(fn, name=name, rows=[dh3, h, pe, gate], fulls=[_row2(pn), _row2(gn), wgate],
                    out_rows=[(D, f32), (D, bf16), (D, bf16), (D, bf16)], out_accs=[(1, D), (1, D)], tm=256)


def _loss_head(h, target, fn_w, name):
    D = h.shape[1]

    def fn(h_, t_, w_):
        diff = _rms(h_, w_, RMS_EPS) - t_
        loss = 0.5 * jnp.sum(jnp.mean(diff * diff, axis=-1, keepdims=True), axis=0, keepdims=True)
        dh, dw = _rms_bwd(diff * (1.0 / D), h_, w_, RMS_EPS)
        return dh, jnp.broadcast_to(loss, (1, LANES)), dw

    return _rowcall(fn, name=name, rows=[h, target], fulls=[_row2(fn_w)], out_rows=[(D, f32)], out_accs=[(1, LANES), (1, D)])


def _adamw(w, m, v, gparts, name):
    R, C = w.shape
    n = gparts.shape[0]
    tr = R
    while tr * C > 128 * 1024 and tr % (2 * SUBLANES) == 0:
        tr //= 2

    def body(w_ref, m_ref, v_ref, g_ref, go_ref, d_ref, mo_ref, vo_ref):
        g = g_ref[0].astype(f32)
        for k in range(1, n):
            g = g + g_ref[k].astype(f32)
        mn = ADAM_B1 * m_ref[...] + (1.0 - ADAM_B1) * g
        vn = ADAM_B2 * v_ref[...] + (1.0 - ADAM_B2) * (g * g)
        m_hat = mn / (1.0 - ADAM_B1 ** ADAM_STEP)
        v_hat = vn / (1.0 - ADAM_B2 ** ADAM_STEP)
        go_ref[...] = g
        d_ref[...] = -ADAM_LR * (m_hat / (jnp.sqrt(v_hat) + ADAM_EPS) + ADAM_WD * w_ref[...])
        mo_ref[...] = mn
        vo_ref[...] = vn

    spec = pl.BlockSpec((tr, C), lambda i: (i, 0))
    return pl.pallas_call(
        body, name=name, grid=(R // tr,),
        in_specs=[spec, spec, spec, pl.BlockSpec((n, tr, C), lambda i: (0, i, 0))],
        out_specs=[spec] * 4, out_shape=[jax.ShapeDtypeStruct((R, C), f32)] * 4,
        compiler_params=_cparams(("parallel",)),
    )(w, m, v, gparts)


def _expand_onehot(n, per):
    lane = lax.broadcasted_iota(jnp.int32, (n, n * per), 1)
    row = lax.broadcasted_iota(jnp.int32, (n, n * per), 0)
    return (lane // per == row).astype(f32)


def _ssd_layer_fwd(h, nm_w, W, t):
    H = W["dt_bias"].shape[0]
    inner = H * HEADDIM
    G = (W["conv_b"].shape[0] - inner) // (2 * STATE)
    hn = _rms_fwd(h, nm_w, f"rms_mix_{t}")
    z = _mm(hn, W["wz"], name=f"ssd_z_{t}")
    xpre = _mm(hn, W["wxbc"], name=f"ssd_xbc_{t}")
    dt_pre = _mm(hn, W["wdt"], name=f"ssd_dt_{t}")
    c, xbc = _conv_fwd(xpre, W["conv_w"], W["conv_b"], f"ssd_conv_{t}")
    _, dtx = _dt_fwd(dt_pre, W["dt_bias"], _expand_onehot(H, HEADDIM), f"ssd_dtx_{t}")
    a = -jnp.exp(W["a_log"])
    ax = _row2(jnp.repeat(a, HEADDIM))
    dx = _row2(jnp.repeat(W["d"], HEADDIM))
    y, states = _ssd_fwd(xbc, dtx, ax, dx, G, f"ssd_scan_{t}")
    yn = _gnorm_fwd(y, z, W["norm_w"], G, f"ssd_gnorm_{t}")
    h1 = _mm(yn, W["wout"], res=h, name=f"ssd_out_{t}")
    return h1, (h, hn, z, xpre, dt_pre, c, xbc, dtx, a, ax, dx, y, states, yn)


def _ssd_layer_bwd(dh1, saved, nm_w, W, t):
    h, hn, z, xpre, dt_pre, c, xbc, dtx, a, ax, dx, y, states, yn = saved
    H = W["dt_bias"].shape[0]
    inner = H * HEADDIM
    G = (W["conv_b"].shape[0] - inner) // (2 * STATE)
    dyn = _mm(dh1, W["wout"], mode="nt", name=f"ssd_out_dx_{t}")
    g_wout = _mm(yn, dh1, mode="tn", out_dtype=bf16, name=f"ssd_out_dw_{t}")
    dy, dz, g_normw = _gnorm_bwd(dyn, y, z, W["norm_w"], G, f"ssd_gnorm_bwd_{t}")
    dxbc, ddt, dax, ddx = _ssd_bwd(dy, xbc, dtx, ax, dx, states, _expand_onehot(H, HEADDIM).T, G, f"ssd_scan_bwd_{t}")
    dc, g_convw8, g_convb = _conv_bwd_dc(dxbc, c, xpre, f"ssd_conv_bwd_dc_{t}")
    dxpre = _conv_bwd_dx(dc, W["conv_w"], f"ssd_conv_bwd_dx_{t}")
    ddt_pre, g_dtb = _dt_bwd(ddt, dt_pre, W["dt_bias"], f"ssd_dt_bwd_{t}")
    g_wz = _mm(hn, dz, mode="tn", out_dtype=bf16, name=f"ssd_z_dw_{t}")
    g_wxbc = _mm(hn, dxpre, mode="tn", out_dtype=bf16, name=f"ssd_xbc_dw_{t}")
    g_wdt = _mm(hn, ddt_pre, mode="tn", out_dtype=bf16, name=f"ssd_dt_dw_{t}")
    dhn = _mm(dz, W["wz"], mode="nt", name=f"ssd_z_dx_{t}")
    dhn = _mm(dxpre, W["wxbc"], mode="nt", res=dhn, name=f"ssd_xbc_dx_{t}")
    dhn = _mm(ddt_pre, W["wdt"], mode="nt", res=dhn, name=f"ssd_dt_dx_{t}")
    dh, g_nm = _rms_bwd_add(dh1, dhn, h, nm_w, f"rms_mix_bwd_{t}")
    grads = dict(
        w_in=jnp.concatenate([g_wz, g_wxbc, g_wdt], axis=1), wout=g_wout,
        conv_w=g_convw8[:CONV_K], conv_b=g_convb[0], dt_bias=g_dtb[0],
        a_log=dax[0].reshape(H, HEADDIM)[:, 0] * a, d=jnp.sum(ddx[0].reshape(H, HEADDIM), axis=1),
        norm_w=g_normw[0], norm_mix=g_nm[0])
    return dh, grads


def _gmlp_layer_fwd(h, nm_w, W, t):
    NG, L, _ = W["w_s"].shape
    I = W["ln_w"].shape[0]
    hn = _rms_fwd(h, nm_w, f"rms_mix_{t}")
    pre = _mm(hn, W["win"], name=f"gmlp_in_{t}")
    bsx = jnp.repeat(W["b_s"].T, I // NG, axis=1)
    o = _gmlp_mid_fwd(pre, W["b_in"], W["ln_w"], W["ln_b"], W["w_s"], bsx, f"gmlp_mid_{t}")
    h1 = _mm(o, W["wout"], res=h, name=f"gmlp_out_{t}")
    return h1, (h, hn, pre, bsx, o)


def _gmlp_layer_bwd(dh1, saved, nm_w, W, t):
    h, hn, pre, bsx, o = saved
    NG = W["w_s"].shape[0]
    I = W["ln_w"].shape[0]
    do = _mm(dh1, W["wout"], mode="nt", name=f"gmlp_out_dx_{t}")
    g_wout = _mm(o, dh1, mode="tn", out_dtype=bf16, name=f"gmlp_out_dw_{t}")
    dpre, g_bin, g_lnw, g_lnb, g_ws, dbs = _gmlp_mid_bwd(do, pre, W["b_in"], W["ln_w"], W["ln_b"], W["w_s"], bsx, f"gmlp_mid_bwd_{t}")
    g_bs = _lane_group_sum(dbs, _expand_onehot(NG, I // NG).T, f"gmlp_bs_{t}").T
    g_win = _mm(hn, dpre, mode="tn", out_dtype=bf16, name=f"gmlp_in_dw_{t}")
    dhn = _mm(dpre, W["win"], mode="nt", name=f"gmlp_in_dx_{t}")
    dh, g_nm = _rms_bwd_add(dh1, dhn, h, nm_w, f"rms_mix_bwd_{t}")
    grads = dict(win=g_win, wout=g_wout, b_in=g_bin[0], ln_w=g_lnw[0], ln_b=g_lnb[0], w_s=g_ws, b_s=g_bs, norm_mix=g_nm[0])
    return dh, grads


def _ffn_fwd(h1, nf_w, W, t):
    u = _rms_fwd(h1, nf_w, f"rms_ffn_{t}")
    Gm, Um, A = _ffn_up(u, W["wg"], W["wu"], f"ffn_up_{t}")
    h2 = _mm(A, W["wd"], kbatch=True, res=h1, name=f"ffn_down_{t}")
    return h2, (h1, u, Gm, Um, A)


def _ffn_bwd(dh2, saved, nf_w, W, t):
    h1, u, Gm, Um, A = saved
    dG, dU = _ffn_bwd_act(dh2, W["wd"], Gm, Um, f"ffn_act_bwd_{t}")
    g_wd = _mm(A, dh2, mode="tn", out_dtype=bf16, name=f"ffn_down_dw_{t}")
    g_wg = _mm(u, dG, mode="tn", out_dtype=bf16, name=f"ffn_gate_dw_{t}")
    g_wu = _mm(u, dU, mode="tn", out_dtype=bf16, name=f"ffn_up_dw_{t}")
    du = _mm(dG, W["wg"], mode="nt", kbatch=True, name=f"ffn_gate_dx_{t}")
    du = _mm(dU, W["wu"], mode="nt", kbatch=True, res=du, name=f"ffn_up_dx_{t}")
    dh1, g_nf = _rms_bwd_add(dh2, du, h1, nf_w, f"rms_ffn_bwd_{t}")
    return dh1, dict(wg=g_wg, wu=g_wu, wd=g_wd, norm_ffn=g_nf[0])


def _local_step(x, p, target, W):
    depth = p.shape[0]
    h = x
    saved = []
    for i in range(depth):
        j = i // 2
        if i % 2 == 0:
            h1, s_mix = _ssd_layer_fwd(h, W["norm_mix"][i], W["ssd"][j], i)
        else:
            h1, s_mix = _gmlp_layer_fwd(h, W["norm_mix"][i], W["gmlp"][j], i)
        h2, s_ffn = _ffn_fwd(h1, W["norm_ffn"][i], W["ffn"][i], i)
        P = W["ple"][i]
        h3, pe, gate = _ple_fwd(h2, p[i], P["wp"], P["pn"], P["gn"], P["wgate"], f"ple_{i}")
        saved.append((s_mix, s_ffn, (h2, pe, gate)))
        h = h3
    dh, loss, g_fn = _loss_head(h, target, W["final_norm"], "loss_head")
    grads = dict(final_norm=g_fn[0], ssd=[None] * len(W["ssd"]), gmlp=[None] * len(W["gmlp"]), ffn=[None] * depth, ple=[None] * depth)
    for i in reversed(range(depth)):
        j = i // 2
        s_mix, s_ffn, (h2, pe, gate) = saved[i]
        P = W["ple"][i]
        dh, dzg, dpe, q, g_pn, g_gn = _ple_bwd(dh, h2, pe, gate, P["pn"], P["gn"], P["wgate"], f"ple_bwd_{i}")
        grads["ple"][i] = dict(
            wgate=_mm(q, dzg, mode="tn", out_dtype=bf16, name=f"ple_gate_dw_{i}"),
            wp=_mm(p[i], dpe, mode="tn", out_dtype=bf16, name=f"ple_proj_dw_{i}"), pn=g_pn[0], gn=g_gn[0])
        dh, grads["ffn"][i] = _ffn_bwd(dh, s_ffn, W["norm_ffn"][i], W["ffn"][i], i)
        if i % 2 == 0:
            dh, grads["ssd"][j] = _ssd_layer_bwd(dh, s_mix, W["norm_mix"][i], W["ssd"][j], i)
        else:
            dh, grads["gmlp"][j] = _gmlp_layer_bwd(dh, s_mix, W["norm_mix"][i], W["gmlp"][j], i)
    return loss[0, 0], dh, grads


def _flip(v, f):
    return 1 - v if f else v


_ANY = pl.BlockSpec(memory_space=pl.ANY)


def _gather_chips(arrs):
    flat = [(t, l) for t, a in enumerate(arrs) for l in range(a.shape[0])]
    n_in, n_out = len(arrs), len(flat)
    flips = [(1, 0), (0, 1), (1, 1)]
    nf = len(flips)

    def body(*refs):
        ins, outs = refs[:n_in], refs[n_in:n_in + n_out]
        send_sems, recv_sems, loc_sems = refs[n_in + n_out:]
        x, y, c = lax.axis_index("x"), lax.axis_index("y"), lax.axis_index("c")
        mychip = 2 * x + y

        def copy(o, j, slot):
            t, l = flat[o]
            fx, fy = flips[j]
            return pltpu.make_async_remote_copy(
                src_ref=ins[t].at[l], dst_ref=outs[o].at[slot], send_sem=send_sems.at[nf * o + j], recv_sem=recv_sems.at[nf * o + j],
                device_id=(_flip(x, fx), _flip(y, fy), c), device_id_type=MESH)

        local = [pltpu.make_async_copy(ins[t].at[l], outs[o].at[mychip], loc_sems.at[o]) for o, (t, l) in enumerate(flat)]
        sends = [copy(o, j, mychip) for o in range(n_out) for j in range(nf)]
        for cp in local + sends:
            cp.start()
        for o in range(n_out):
            for j, (fx, fy) in enumerate(flips):
                copy(o, j, 2 * _flip(x, fx) + _flip(y, fy)).wait_recv()
        for cp in sends:
            cp.wait_send()
        for cp in local:
            cp.wait()

    outs = pl.pallas_call(
        body, name="gather_weights", in_specs=[_ANY] * n_in, out_specs=[_ANY] * n_out,
        out_shape=[jax.ShapeDtypeStruct((N_CHIPS,) + arrs[t].shape[1:], arrs[t].dtype) for t, _ in flat],
        scratch_shapes=[pltpu.SemaphoreType.DMA((nf * n_out,)), pltpu.SemaphoreType.DMA((nf * n_out,)), pltpu.SemaphoreType.DMA((n_out,))],
    )(*arrs)
    res = [[] for _ in arrs]
    for o, (t, _) in enumerate(flat):
        res[t].append(outs[o])
    return res


def _exchange_grads(groups):
    flat = [(gi, l) for gi, (arrs, _) in enumerate(groups) for l in range(len(arrs))]
    n_in, n_out = len(flat), len(groups)
    flips = [(f >> 2 & 1, f >> 1 & 1, f & 1) for f in range(1, N_DEV)]
    nf = len(flips)

    def body(*refs):
        ins, outs = refs[:n_in], refs[n_in:n_in + n_out]
        send_sems, recv_sems, loc_sems = refs[n_in + n_out:]
        x, y, c = lax.axis_index("x"), lax.axis_index("y"), lax.axis_index("c")
        me = 4 * x + 2 * y + c

        def part(i, chip):
            return ins[i].at[chip] if groups[flat[i][0]][1] else ins[i]

        def copy(i, j, slot):
            gi, l = flat[i]
            px, py, pc = (_flip(v, f) for v, f in zip((x, y, c), flips[j]))
            return pltpu.make_async_remote_copy(
                src_ref=part(i, 2 * px + py), dst_ref=outs[gi].at[slot, l], send_sem=send_sems.at[nf * i + j],
                recv_sem=recv_sems.at[nf * i + j], device_id=(px, py, pc), device_id_type=MESH)

        local = [pltpu.make_async_copy(part(i, 2 * x + y), outs[gi].at[me, l], loc_sems.at[i]) for i, (gi, l) in enumerate(flat)]
        sends = [copy(i, j, me) for i in range(n_in) for j in range(nf)]
        for cp in local + sends:
            cp.start()
        for i in range(n_in):
            for j, (fx, fy, fc) in enumerate(flips):
                copy(i, j, 4 * _flip(x, fx) + 2 * _flip(y, fy) + _flip(c, fc)).wait_recv()
        for cp in sends:
            cp.wait_send()
        for cp in local:
            cp.wait()

    def out_struct(arrs, sharded):
        s = arrs[0].shape[1:] if sharded else arrs[0].shape
        return jax.ShapeDtypeStruct((N_DEV, len(arrs)) + s, arrs[0].dtype)

    return pl.pallas_call(
        body, name="exchange_grads", in_specs=[_ANY] * n_in, out_specs=[_ANY] * n_out,
        out_shape=[out_struct(arrs, sharded) for arrs, sharded in groups],
        scratch_shapes=[pltpu.SemaphoreType.DMA((nf * n_in,)), pltpu.SemaphoreType.DMA((nf * n_in,)), pltpu.SemaphoreType.DMA((n_in,))],
    )(*[a for arrs, _ in groups for a in arrs])


PACK_COLS = 1024
PACK_ROW_MULTIPLE = 64

BIG = ("ssd_w_in", "ssd_w_out", "gmlp_w_in", "gmlp_w_out", "ffn_w_gate", "ffn_w_up", "ffn_w_down", "ple_w_proj", "ple_w_gate")
SMALL_SHARDED = ("ssd_conv_w", "gmlp_b_in", "gmlp_ln_w", "gmlp_ln_b")
REPLICATED = ("norm_mix", "norm_ffn", "ssd_conv_b", "ssd_dt_bias", "ssd_a_log", "ssd_d", "ssd_norm_w", "gmlp_w_s", "gmlp_b_s",
              "ple_norm", "ple_gate_norm", "final_norm")
WEIGHTS = ("norm_mix", "norm_ffn", "ssd_w_in", "ssd_conv_w", "ssd_conv_b", "ssd_dt_bias", "ssd_a_log", "ssd_d", "ssd_norm_w", "ssd_w_out",
           "gmlp_w_in", "gmlp_b_in", "gmlp_ln_w", "gmlp_ln_b", "gmlp_w_s", "gmlp_b_s", "gmlp_w_out", "ffn_w_gate", "ffn_w_up",
           "ffn_w_down", "ple_w_proj", "ple_norm", "ple_gate_norm", "ple_w_gate", "final_norm")
COLUMN_SHARDED = ("ssd_w_in", "gmlp_w_in", "ple_w_proj")


def _pack(arrs):
    flat = jnp.concatenate([a.reshape(-1).astype(f32) for a in arrs])
    per = PACK_COLS * PACK_ROW_MULTIPLE
    n = -(-flat.shape[0] // per) * per
    return jnp.pad(flat, (0, n - flat.shape[0])).reshape(-1, PACK_COLS)


def _unpack(buf, shapes):
    flat = buf.reshape(-1)
    out, o = [], 0
    for s in shapes:
        n = math.prod(s)
        out.append(flat[o:o + n].reshape(s))
        o += n
    return out


def _chip_major(g):
    r, c4 = g.shape
    return g.reshape(r, N_CHIPS, c4 // N_CHIPS).transpose(1, 0, 2)


def _from_chip_major(g):
    k, r, c = g.shape
    return g.transpose(1, 0, 2).reshape(r, k * c)


def _adamw_nd(w, m, v, gparts, name):
    shp = w.shape
    cols = shp[-1] if w.ndim > 1 else shp[0]
    two = lambda a: a.reshape(-1, cols)
    outs = _adamw(two(w), two(m), two(v), gparts.reshape(gparts.shape[0], -1, cols), name)
    return [o.reshape(shp) for o in outs]


def kernel(x, p, norm_mix, norm_ffn, ssd_w_in, ssd_conv_w, ssd_conv_b, ssd_dt_bias, ssd_a_log, ssd_d, ssd_norm_w, ssd_w_out, gmlp_w_in, gmlp_b_in, gmlp_ln_w, gmlp_ln_b, gmlp_w_s, gmlp_b_s, gmlp_w_out, ffn_w_gate, ffn_w_up, ffn_w_down, ple_w_proj, ple_norm, ple_gate_norm, ple_w_gate, final_norm, loss_target, m_norm_mix, m_norm_ffn, m_ssd_w_in, m_ssd_conv_w, m_ssd_conv_b, m_ssd_dt_bias, m_ssd_a_log, m_ssd_d, m_ssd_norm_w, m_ssd_w_out, m_gmlp_w_in, m_gmlp_b_in, m_gmlp_ln_w, m_gmlp_ln_b, m_gmlp_w_s, m_gmlp_b_s, m_gmlp_w_out, m_ffn_w_gate, m_ffn_w_up, m_ffn_w_down, m_ple_w_proj, m_ple_norm, m_ple_gate_norm, m_ple_w_gate, m_final_norm, v_norm_mix, v_norm_ffn, v_ssd_w_in, v_ssd_conv_w, v_ssd_conv_b, v_ssd_dt_bias, v_ssd_a_log, v_ssd_d, v_ssd_norm_w, v_ssd_w_out, v_gmlp_w_in, v_gmlp_b_in, v_gmlp_ln_w, v_gmlp_ln_b, v_gmlp_w_s, v_gmlp_b_s, v_gmlp_w_out, v_ffn_w_gate, v_ffn_w_up, v_ffn_w_down, v_ple_w_proj, v_ple_norm, v_ple_gate_norm, v_ple_w_gate, v_final_norm):
    given = dict(locals())
    w = {n: given[n] for n in WEIGHTS}
    mom = {n: given["m_" + n] for n in WEIGHTS}
    var = {n: given["v_" + n] for n in WEIGHTS}
    depth = p.shape[0]
    n_ssd, n_gmlp = ssd_w_in.shape[0], gmlp_w_in.shape[0]
    inner = ssd_dt_bias.shape[1] * HEADDIM
    conv_dim = ssd_conv_b.shape[1]

    small_shapes = [w[n].shape for n in SMALL_SHARDED]
    gathered = _gather_chips([w[n].astype(bf16) for n in BIG] + [_pack([w[n] for n in SMALL_SHARDED])[None]])
    gw = dict(zip(BIG, gathered[:-1]))
    small_by_chip = [_unpack(gathered[-1][0][k], small_shapes) for k in range(N_CHIPS)]
    small_full = {n: jnp.concatenate([small_by_chip[k][i] for k in range(N_CHIPS)], axis=-1) for i, n in enumerate(SMALL_SHARDED)}

    W = dict(norm_mix=norm_mix, norm_ffn=norm_ffn, final_norm=final_norm, ssd=[], gmlp=[], ffn=[], ple=[])
    for j in range(n_ssd):
        w_in = _from_chip_major(gw["ssd_w_in"][j])
        W["ssd"].append(dict(
            wz=w_in[:, :inner], wxbc=w_in[:, inner:inner + conv_dim], wdt=w_in[:, inner + conv_dim:],
            conv_w=small_full["ssd_conv_w"][j], conv_b=ssd_conv_b[j], dt_bias=ssd_dt_bias[j], a_log=ssd_a_log[j], d=ssd_d[j],
            norm_w=ssd_norm_w[j], wout=gw["ssd_w_out"][j].reshape(-1, gw["ssd_w_out"][j].shape[-1])))
    for j in range(n_gmlp):
        W["gmlp"].append(dict(
            win=_from_chip_major(gw["gmlp_w_in"][j]), b_in=small_full["gmlp_b_in"][j], ln_w=small_full["gmlp_ln_w"][j],
            ln_b=small_full["gmlp_ln_b"][j], w_s=gmlp_w_s[j], b_s=gmlp_b_s[j],
            wout=gw["gmlp_w_out"][j].reshape(-1, gw["gmlp_w_out"][j].shape[-1])))
    for i in range(depth):
        W["ffn"].append(dict(wg=gw["ffn_w_gate"][i], wu=gw["ffn_w_up"][i], wd=gw["ffn_w_down"][i]))
        W["ple"].append(dict(wp=_from_chip_major(gw["ple_w_proj"][i]), pn=ple_norm[i], gn=ple_gate_norm[i],
                             wgate=gw["ple_w_gate"][i].reshape(-1, gw["ple_w_gate"][i].shape[-1])))

    loss_part, grad_x, g = _local_step(x[0], p[:, 0], loss_target[0], W)
    loss = lax.psum(loss_part, ("x", "y", "c"))

    rows4 = lambda a: a.reshape((N_CHIPS, a.shape[0] // N_CHIPS) + a.shape[1:])
    big = dict(
        ssd_w_in=[_chip_major(s["w_in"]) for s in g["ssd"]], ssd_w_out=[rows4(s["wout"]) for s in g["ssd"]],
        gmlp_w_in=[_chip_major(s["win"]) for s in g["gmlp"]], gmlp_w_out=[rows4(s["wout"]) for s in g["gmlp"]],
        ffn_w_gate=[s["wg"] for s in g["ffn"]], ffn_w_up=[s["wu"] for s in g["ffn"]], ffn_w_down=[s["wd"] for s in g["ffn"]],
        ple_w_proj=[_chip_major(s["wp"]) for s in g["ple"]], ple_w_gate=[rows4(s["wgate"]) for s in g["ple"]])
    small_g = dict(ssd_conv_w=jnp.stack([s["conv_w"] for s in g["ssd"]]), gmlp_b_in=jnp.stack([s["b_in"] for s in g["gmlp"]]),
                   gmlp_ln_w=jnp.stack([s["ln_w"] for s in g["gmlp"]]), gmlp_ln_b=jnp.stack([s["ln_b"] for s in g["gmlp"]]))
    cut = lambda a, k: a[..., k * (a.shape[-1] // N_CHIPS):(k + 1) * (a.shape[-1] // N_CHIPS)]
    small_packed = jnp.stack([_pack([cut(small_g[n], k) for n in SMALL_SHARDED]) for k in range(N_CHIPS)])
    mix = [g["ssd"][i // 2]["norm_mix"] if i % 2 == 0 else g["gmlp"][i // 2]["norm_mix"] for i in range(depth)]
    rep_g = dict(
        norm_mix=jnp.stack(mix), norm_ffn=jnp.stack([s["norm_ffn"] for s in g["ffn"]]),
        ssd_conv_b=jnp.stack([s["conv_b"] for s in g["ssd"]]), ssd_dt_bias=jnp.stack([s["dt_bias"] for s in g["ssd"]]),
        ssd_a_log=jnp.stack([s["a_log"] for s in g["ssd"]]), ssd_d=jnp.stack([s["d"] for s in g["ssd"]]),
        ssd_norm_w=jnp.stack([s["norm_w"] for s in g["ssd"]]), gmlp_w_s=jnp.stack([s["w_s"] for s in g["gmlp"]]),
        gmlp_b_s=jnp.stack([s["b_s"] for s in g["gmlp"]]), ple_norm=jnp.stack([s["pn"] for s in g["ple"]]),
        ple_gate_norm=jnp.stack([s["gn"] for s in g["ple"]]), final_norm=g["final_norm"])
    rep_packed = _pack([rep_g[n] for n in REPLICATED])
    landed = _exchange_grads([(big[n], True) for n in BIG] + [([small_packed], True), ([rep_packed], False)])

    res = {}
    for n, parts in zip(BIG, landed):
        res[n] = _adamw_nd(w[n], mom[n], var[n], parts.reshape((N_DEV,) + w[n].shape), "adamw_" + n)
    for names, parts, tag in ((SMALL_SHARDED, landed[-2], "adamw_small_sharded"), (REPLICATED, landed[-1], "adamw_replicated")):
        packs = _adamw(_pack([w[n] for n in names]), _pack([mom[n] for n in names]), _pack([var[n] for n in names]), parts[:, 0], tag)
        per_kind = [_unpack(pk, [w[n].shape for n in names]) for pk in packs]
        for i, n in enumerate(names):
            res[n] = [per_kind[k][i] for k in range(4)]
    return (loss, grad_x[None], *[res[n][0] for n in WEIGHTS], *[res[n][1] for n in WEIGHTS],
            *[res[n][2] for n in WEIGHTS], *[res[n][3] for n in WEIGHTS])
```

```python
import functools
import math

import jax
import jax.numpy as jnp
from jax import lax
from jax.experimental import pallas as pl
from jax.experimental.pallas import tpu as pltpu

f32 = jnp.float32
bf16 = jnp.bfloat16
HI = lax.Precision.HIGHEST

LANES = 128
SUBLANES = 8
VMEM_LIMIT_BYTES = 56 * 1024 * 1024

HEADDIM = 64
STATE = 128
CHUNK = 128
CONV_K = 4
RMS_EPS = 1e-6
LN_EPS = 1e-5
ADAM_LR = 0.001
ADAM_B1 = 0.9
ADAM_B2 = 0.999
ADAM_EPS = 1e-08
ADAM_WD = 0.01
ADAM_STEP = 10

N_CHIPS = 4
N_DEV = 8
MESH = pl.DeviceIdType.MESH


def _cparams(sem):
    return pltpu.CompilerParams(dimension_semantics=sem, vmem_limit_bytes=VMEM_LIMIT_BYTES)


def _tile(n, want):
    if n <= want:
        return n
    t = want
    while n % t:
        t //= 2
    return t


def _row_spec(tm, c):
    return pl.BlockSpec((tm, c), lambda i: (i, 0))


def _full_spec(shape):
    nd = len(shape)
    return pl.BlockSpec(tuple(shape), lambda *_: (0,) * nd)


def _sigmoid(x):
    return 1.0 / (1.0 + jnp.exp(-x))


def _silu(x):
    return x * _sigmoid(x)


def _dsilu(x):
    s = _sigmoid(x)
    return s * (1.0 + x * (1.0 - s))


def _gelu(x):
    return 0.5 * x * (1.0 + lax.erf(x * (1.0 / math.sqrt(2.0))))


def _dgelu(x):
    return 0.5 * (1.0 + lax.erf(x * (1.0 / math.sqrt(2.0)))) + x * jnp.exp(-0.5 * x * x) * (1.0 / math.sqrt(2.0 * math.pi))


def _softplus(x):
    return jnp.maximum(x, 0.0) + jnp.log(1.0 + jnp.exp(-jnp.abs(x)))


def _rms(x, w, eps):
    r = lax.rsqrt(jnp.mean(x * x, axis=-1, keepdims=True) + eps)
    return x * r * w


def _rms_bwd(dy, x, w, eps):
    r = lax.rsqrt(jnp.mean(x * x, axis=-1, keepdims=True) + eps)
    xh = x * r
    g = dy * w
    dx = r * (g - xh * jnp.mean(g * xh, axis=-1, keepdims=True))
    dw = jnp.sum(dy * xh, axis=0, keepdims=True)
    return dx, dw


def _dot(a, b, dims=(((1,), (0,)), ((), ())), precision=None):
    return lax.dot_general(a, b, dims, precision=precision, preferred_element_type=f32)


NN = (((1,), (0,)), ((), ()))
NT = (((1,), (1,)), ((), ()))
TN = (((0,), (0,)), ((), ()))


def _mm(a, b, *, mode="nn", out_dtype=f32, res=None, kbatch=False, tm=1024, tn=512, tk=2048, name):
    a3, b3 = a.ndim == 3, b.ndim == 3
    nb = a.shape[0] if a3 else (b.shape[0] if b3 else 1)
    ash, bsh = a.shape[-2:], b.shape[-2:]
    if mode == "nn":
        M, K, N = ash[0], ash[1], bsh[1]
    elif mode == "nt":
        M, K, N = ash[0], ash[1], bsh[0]
    else:
        K, M, N = ash[0], ash[1], bsh[1]
    tm, tn, tk = _tile(M, tm), (N if N % LANES else _tile(N, tn)), (K if K % LANES else _tile(K, tk))
    nk = K // tk
    if kbatch:
        assert a3 and b3
        grid = (1, M // tm, N // tn, nb * nk)
        bi = lambda g, k: k // nk
        ki = lambda g, k: k % nk
    else:
        grid = (nb, M // tm, N // tn, nk)
        bi = lambda g, k: g
        ki = lambda g, k: k
    nsteps = grid[3]

    def spec(is3, blk, imap):
        if is3:
            return pl.BlockSpec((None,) + blk, lambda g, i, j, k: (bi(g, k),) + imap(i, j, ki(g, k)))
        return pl.BlockSpec(blk, lambda g, i, j, k: imap(i, j, ki(g, k)))

    if mode == "nn":
        a_spec = spec(a3, (tm, tk), lambda i, j, k: (i, k))
        b_spec = spec(b3, (tk, tn), lambda i, j, k: (k, j))
        dims = NN
    elif mode == "nt":
        a_spec = spec(a3, (tm, tk), lambda i, j, k: (i, k))
        b_spec = spec(b3, (tn, tk), lambda i, j, k: (j, k))
        dims = NT
    else:
        a_spec = spec(a3, (tk, tm), lambda i, j, k: (k, i))
        b_spec = spec(b3, (tk, tn), lambda i, j, k: (k, j))
        dims = TN
    out3 = (a3 or b3) and not kbatch
    if out3:
        o_spec = pl.BlockSpec((None, tm, tn), lambda g, i, j, k: (g, i, j))
        o_shape = jax.ShapeDtypeStruct((nb, M, N), out_dtype)
    else:
        o_spec = pl.BlockSpec((tm, tn), lambda g, i, j, k: (i, j))
        o_shape = jax.ShapeDtypeStruct((M, N), out_dtype)
    in_specs = [a_spec, b_spec]
    args = [a, b]
    if res is not None:
        in_specs.append(pl.BlockSpec((tm, tn), lambda g, i, j, k: (i, j)))
        args.append(res)

    def body(*refs):
        if res is not None:
            a_ref, b_ref, r_ref, o_ref, acc_ref = refs
        else:
            a_ref, b_ref, o_ref, acc_ref = refs
        k = pl.program_id(3)

        @pl.when(k == 0)
        def _():
            acc_ref[...] = jnp.zeros_like(acc_ref)

        acc_ref[...] += _dot(a_ref[...].astype(bf16), b_ref[...].astype(bf16), dims)

        @pl.when(k == nsteps - 1)
        def _():
            r = acc_ref[...]
            if res is not None:
                r = r + r_ref[...]
            o_ref[...] = r.astype(o_ref.dtype)

    return pl.pallas_call(
        body, name=name, grid=grid, in_specs=in_specs, out_specs=o_spec, out_shape=o_shape,
        scratch_shapes=[pltpu.VMEM((tm, tn), f32)],
        compiler_params=_cparams(("parallel", "parallel", "parallel", "arbitrary")),
    )(*args)


def _rowcall(fn, *, name, rows, fulls, out_rows, out_accs=(), tm=512):
    S = rows[0].shape[0]
    tm = _tile(S, tm)
    n_r, n_f, n_or, n_oa = len(rows), len(fulls), len(out_rows), len(out_accs)

    def body(*refs):
        ins = [r[...] for r in refs[:n_r + n_f]]
        outs = fn(*ins)
        if not isinstance(outs, (tuple, list)):
            outs = (outs,)
        o_refs = refs[n_r + n_f:]
        for o_ref, v in zip(o_refs[:n_or], outs[:n_or]):
            o_ref[...] = v.astype(o_ref.dtype)
        if n_oa:
            first = pl.program_id(0) == 0

            @pl.when(first)
            def _():
                for o_ref, v in zip(o_refs[n_or:], outs[n_or:]):
                    o_ref[...] = v

            @pl.when(jnp.logical_not(first))
            def _():
                for o_ref, v in zip(o_refs[n_or:], outs[n_or:]):
                    o_ref[...] += v

    in_specs = [_row_spec(tm, r.shape[1]) for r in rows] + [_full_spec(f.shape) for f in fulls]
    out_specs = [_row_spec(tm, c) for c, _ in out_rows] + [_full_spec(s) for s in out_accs]
    out_shape = [jax.ShapeDtypeStruct((S, c), d) for c, d in out_rows] + [jax.ShapeDtypeStruct(s, f32) for s in out_accs]
    res = pl.pallas_call(
        body, name=name, grid=(S // tm,), in_specs=in_specs, out_specs=out_specs, out_shape=out_shape,
        compiler_params=_cparams(("arbitrary",) if n_oa else ("parallel",)),
    )(*rows, *fulls)
    return res


def _row2(v):
    return v.reshape(1, -1)


def _rms_fwd(h, w, name):
    D = h.shape[1]
    return _rowcall(lambda x, w_: _rms(x, w_, RMS_EPS), name=name, rows=[h], fulls=[_row2(w)], out_rows=[(D, bf16)])[0]


def _conv_fwd(xpre, w, b, name):
    S, C = xpre.shape
    tm, tc = _tile(S, 512), _tile(C, 1024)
    hb = tm // SUBLANES

    def body(x_ref, halo_ref, w_ref, b_ref, c_ref, o_ref):
        i = pl.program_id(1)
        x = x_ref[...]
        halo = jnp.where(i > 0, halo_ref[...], 0.0)
        row = lax.broadcasted_iota(jnp.int32, x.shape, 0)
        row8 = lax.broadcasted_iota(jnp.int32, halo.shape, 0)
        x0 = x[0:SUBLANES, :]
        acc = x * w_ref[CONV_K - 1:CONV_K, :] + b_ref[...]
        acc0 = x0 * w_ref[CONV_K - 1:CONV_K, :] + b_ref[...]
        for k in range(1, CONV_K):
            wk = w_ref[CONV_K - 1 - k:CONV_K - k, :]
            acc = acc + pltpu.roll(x, k, axis=0) * wk
            acc0 = acc0 + jnp.where(row8 < k, pltpu.roll(halo, k, axis=0), pltpu.roll(x0, k, axis=0)) * wk
        c_ref[...] = acc
        o_ref[...] = _silu(acc)
        c_ref[0:SUBLANES, :] = acc0
        o_ref[0:SUBLANES, :] = _silu(acc0)

    return pl.pallas_call(
        body, name=name, grid=(C // tc, S // tm),
        in_specs=[pl.BlockSpec((tm, tc), lambda j, i: (i, j)),
                  pl.BlockSpec((SUBLANES, tc), lambda j, i: (jnp.maximum(i * hb - 1, 0), j)),
                  pl.BlockSpec((CONV_K, tc), lambda j, i: (0, j)),
                  pl.BlockSpec((1, tc), lambda j, i: (0, j))],
        out_specs=[pl.BlockSpec((tm, tc), lambda j, i: (i, j))] * 2,
        out_shape=[jax.ShapeDtypeStruct((S, C), f32)] * 2,
        compiler_params=_cparams(("parallel", "parallel")),
    )(xpre, xpre, w, _row2(b))


def _conv_bwd_dc(dxbc, c, xpre, name):
    S, C = xpre.shape
    tm, tc = _tile(S, 512), _tile(C, 1024)
    hb = tm // SUBLANES

    def body(d_ref, c_ref, x_ref, halo_ref, dc_ref, dw_ref, db_ref):
        i = pl.program_id(1)
        x = x_ref[...]
        dc = d_ref[...] * _dsilu(c_ref[...])
        dc_ref[...] = dc
        halo = jnp.where(i > 0, halo_ref[...], 0.0)
        row = lax.broadcasted_iota(jnp.int32, x.shape, 0)
        row8 = lax.broadcasted_iota(jnp.int32, halo.shape, 0)
        x0 = x[0:SUBLANES, :]
        dc0 = dc[0:SUBLANES, :]
        parts = [jnp.sum(dc * x, axis=0, keepdims=True)]
        for k in range(1, CONV_K):
            xs_big = jnp.where(row < SUBLANES, 0.0, pltpu.roll(x, k, axis=0))
            xs0 = jnp.where(row8 < k, pltpu.roll(halo, k, axis=0), pltpu.roll(x0, k, axis=0))
            parts.append(jnp.sum(dc * xs_big, axis=0, keepdims=True) + jnp.sum(dc0 * xs0, axis=0, keepdims=True))
        dw = jnp.concatenate([parts[CONV_K - 1 - k] for k in range(CONV_K)] + [jnp.zeros((SUBLANES - CONV_K, x.shape[1]), f32)], axis=0)
        db = jnp.sum(dc, axis=0, keepdims=True)

        @pl.when(i == 0)
        def _():
            dw_ref[...] = dw
            db_ref[...] = db

        @pl.when(i > 0)
        def _():
            dw_ref[...] += dw
            db_ref[...] += db

    return pl.pallas_call(
        body, name=name, grid=(C // tc, S // tm),
        in_specs=[pl.BlockSpec((tm, tc), lambda j, i: (i, j))] * 3 +
                 [pl.BlockSpec((SUBLANES, tc), lambda j, i: (jnp.maximum(i * hb - 1, 0), j))],
        out_specs=[pl.BlockSpec((tm, tc), lambda j, i: (i, j)),
                   pl.BlockSpec((SUBLANES, tc), lambda j, i: (0, j)),
                   pl.BlockSpec((1, tc), lambda j, i: (0, j))],
        out_shape=[jax.ShapeDtypeStruct((S, C), f32), jax.ShapeDtypeStruct((SUBLANES, C), f32), jax.ShapeDtypeStruct((1, C), f32)],
        compiler_params=_cparams(("parallel", "arbitrary")),
    )(dxbc, c, xpre, xpre)


def _conv_bwd_dx(dc, w, name):
    S, C = dc.shape
    tm, tc = _tile(S, 512), _tile(C, 1024)
    hb = tm // SUBLANES
    nrow = S // tm
    last8 = S // SUBLANES - 1

    def body(d_ref, nxt_ref, w_ref, o_ref):
        i = pl.program_id(1)
        d = d_ref[...]
        nxt = jnp.where(i < nrow - 1, nxt_ref[...], 0.0)
        row8 = lax.broadcasted_iota(jnp.int32, nxt.shape, 0)
        dl = d[tm - SUBLANES:tm, :]
        acc = d * w_ref[CONV_K - 1:CONV_K, :]
        accl = dl * w_ref[CONV_K - 1:CONV_K, :]
        for j in range(1, CONV_K):
            wk = w_ref[CONV_K - 1 - j:CONV_K - j, :]
            acc = acc + pltpu.roll(d, tm - j, axis=0) * wk
            accl = accl + jnp.where(row8 >= SUBLANES - j, pltpu.roll(nxt, SUBLANES - j, axis=0), pltpu.roll(dl, SUBLANES - j, axis=0)) * wk
        o_ref[...] = acc.astype(o_ref.dtype)
        o_ref[tm - SUBLANES:tm, :] = accl.astype(o_ref.dtype)

    return pl.pallas_call(
        body, name=name, grid=(C // tc, nrow),
        in_specs=[pl.BlockSpec((tm, tc), lambda j, i: (i, j)),
                  pl.BlockSpec((SUBLANES, tc), lambda j, i: (jnp.minimum((i + 1) * hb, last8), j)),
                  pl.BlockSpec((CONV_K, tc), lambda j, i: (0, j))],
        out_specs=pl.BlockSpec((tm, tc), lambda j, i: (i, j)),
        out_shape=jax.ShapeDtypeStruct((S, C), f32),
        compiler_params=_cparams(("parallel", "parallel")),
    )(dc, dc, w)


def _halfsum(v, lane_lo):
    s0 = jnp.sum(jnp.where(lane_lo, v, 0.0), axis=1, keepdims=True)
    s1 = jnp.sum(jnp.where(lane_lo, 0.0, v), axis=1, keepdims=True)
    return jnp.where(lane_lo, s0, s1)


def _ssd_specs(S, inner, GN, nchunks, rev):
    L = CHUNK
    cm = (lambda c: nchunks - 1 - c) if rev else (lambda c: c)
    xs = pl.BlockSpec((L, inner), lambda c: (cm(c), 0))
    bb = pl.BlockSpec((L, GN), lambda c: (cm(c), inner // GN))
    cc = pl.BlockSpec((L, GN), lambda c: (cm(c), inner // GN + 1))
    row = pl.BlockSpec((L, inner), lambda c: (cm(c), 0))
    vec = pl.BlockSpec((1, inner), lambda c: (0, 0))
    st = pl.BlockSpec((None, inner, STATE), lambda c: (cm(c), 0, 0))
    return xs, bb, cc, row, vec, st


def _ssd_fwd(xbc, dtx, ax, dx, G, name):
    S, inner = dtx.shape
    GN = G * STATE
    L = CHUNK
    nchunks = S // L
    npairs = inner // LANES
    ppg = npairs // G
    assert inner % GN == 0 and L == LANES and STATE == LANES

    def body(xs_ref, b_ref, c_ref, dtx_ref, ax_ref, dx_ref, y_ref, so_ref, st_ref):
        ci = pl.program_id(0)

        @pl.when(ci == 0)
        def _():
            st_ref[...] = jnp.zeros_like(st_ref)

        r = lax.broadcasted_iota(jnp.int32, (L, L), 0)
        cidx = lax.broadcasted_iota(jnp.int32, (L, L), 1)
        tril = cidx <= r
        lane_lo = cidx < HEADDIM
        xs = xs_ref[...]
        dtv = dtx_ref[...]
        X = xs * dtv
        da = dtv * ax_ref[...]
        cs = _dot(tril.astype(f32), da, NN, HI)
        cs_last = jnp.sum(da, axis=0, keepdims=True)
        so_ref[...] = st_ref[...]
        for g in range(G):
            Bg = b_ref[:, g * STATE:(g + 1) * STATE].astype(bf16)
            Cg = c_ref[:, g * STATE:(g + 1) * STATE].astype(bf16)
            CB = _dot(Cg, Bg, NT)
            for j in range(ppg):
                lo = (g * ppg + j) * LANES
                tile = cs[:, lo:lo + LANES]
                rl = pltpu.roll(tile, HEADDIM, axis=1)
                Xp = X[:, lo:lo + LANES]
                prev = st_ref[lo:lo + LANES, :]
                ypair = _dot(Cg, prev.astype(bf16), NT) * jnp.exp(tile)
                for half in (0, 1):
                    hm = lane_lo if half == 0 else jnp.logical_not(lane_lo)
                    colb = jnp.where(hm, tile, rl)
                    Lm = jnp.exp(jnp.where(tril, colb - colb.T, -1e30))
                    W = (CB * Lm).astype(bf16)
                    ypair = ypair + _dot(W, jnp.where(hm, Xp, 0.0).astype(bf16), NN)
                y_ref[:, lo:lo + LANES] = ypair + xs[:, lo:lo + LANES] * dx_ref[:, lo:lo + LANES]
                last = cs_last[:, lo:lo + LANES]
                snew = _dot((Xp * jnp.exp(last - tile)).astype(bf16), Bg, TN)
                dec_rows = jnp.broadcast_to(jnp.exp(last), (L, LANES)).T
                st_ref[lo:lo + LANES, :] = dec_rows * prev + snew

    xs_s, b_s, c_s, row_s, vec_s, st_s = _ssd_specs(S, inner, GN, nchunks, False)
    return pl.pallas_call(
        body, name=name, grid=(nchunks,),
        in_specs=[xs_s, b_s, c_s, row_s, vec_s, vec_s],
        out_specs=[row_s, st_s],
        out_shape=[jax.ShapeDtypeStruct((S, inner), f32), jax.ShapeDtypeStruct((nchunks, inner, STATE), f32)],
        scratch_shapes=[pltpu.VMEM((inner, STATE), f32)],
        compiler_params=_cparams(("arbitrary",)),
    )(xbc, xbc, xbc, dtx, ax, dx)


def _ssd_bwd(dy, xbc, dtx, ax, dx, states, et, G, name):
    S, inner = dtx.shape
    H = et.shape[1]
    GN = G * STATE
    Cc = inner + 2 * GN
    L = CHUNK
    nchunks = S // L
    npairs = inner // LANES
    ppg = npairs // G

    def body(dy_ref, xs_ref, b_ref, c_ref, dtx_ref, ax_ref, dx_ref, si_ref, et_ref,
             dxbc_ref, ddt_ref, dax_ref, ddx_ref, dst_ref, dA_ref, dAl_ref, ddtp_ref):
        ci = pl.program_id(0)

        @pl.when(ci == 0)
        def _():
            dst_ref[...] = jnp.zeros_like(dst_ref)
            dax_ref[...] = jnp.zeros_like(dax_ref)
            ddx_ref[...] = jnp.zeros_like(ddx_ref)

        r = lax.broadcasted_iota(jnp.int32, (L, L), 0)
        cidx = lax.broadcasted_iota(jnp.int32, (L, L), 1)
        tril = cidx <= r
        lane_lo = cidx < HEADDIM
        lane_lo1 = lax.broadcasted_iota(jnp.int32, (1, LANES), 1) < HEADDIM
        xs = xs_ref[...]
        dtv = dtx_ref[...]
        dyv = dy_ref[...]
        X = xs * dtv
        da = dtv * ax_ref[...]
        cs = _dot(tril.astype(f32), da, NN, HI)
        cs_last = jnp.sum(da, axis=0, keepdims=True)
        ones8 = jnp.ones((SUBLANES, STATE), f32)
        for g in range(G):
            Bg = b_ref[:, g * STATE:(g + 1) * STATE].astype(bf16)
            Cg = c_ref[:, g * STATE:(g + 1) * STATE].astype(bf16)
            CB = _dot(Cg, Bg, NT)
            dCB = jnp.zeros((L, L), f32)
            dBg = jnp.zeros((L, STATE), f32)
            dCg = jnp.zeros((L, STATE), f32)
            for j in range(ppg):
                lo = (g * ppg + j) * LANES
                tile = cs[:, lo:lo + LANES]
                rl = pltpu.roll(tile, HEADDIM, axis=1)
                eA = jnp.exp(tile)
                Xp = X[:, lo:lo + LANES]
                dYp = dyv[:, lo:lo + LANES]
                xsp = xs[:, lo:lo + LANES]
                prev = si_ref[lo:lo + LANES, :]
                dSn = dst_ref[lo:lo + LANES, :]
                prev_b = prev.astype(bf16)
                dSn_b = dSn.astype(bf16)
                yoff = _dot(Cg, prev_b, NT) * eA
                dA_t = _halfsum(dYp * yoff, lane_lo)
                dYe = (dYp * eA).astype(bf16)
                dCg = dCg + _dot(dYe, prev_b, NN)
                dprev = _dot(dYe, Cg, TN)
                last = cs_last[:, lo:lo + LANES]
                w = jnp.exp(last - tile)
                BdS = _dot(Bg, dSn_b, NT)
                Xw = Xp * w
                dXp = w * BdS
                Gt = _halfsum(Xw * BdS, lane_lo)
                dA_t = dA_t - Gt
                dAl_t = jnp.sum(Gt, axis=0, keepdims=True)
                dBg = dBg + _dot(Xw.astype(bf16), dSn_b, NN)
                dec_rows = jnp.broadcast_to(jnp.exp(last), (L, LANES)).T
                dprev = dprev + dec_rows * dSn
                v = _dot(ones8, dSn * prev * dec_rows, NT, HI)
                dAl_t = dAl_t + _halfsum(jnp.sum(v, axis=0, keepdims=True) * (1.0 / SUBLANES), lane_lo1)
                for half in (0, 1):
                    hm = lane_lo if half == 0 else jnp.logical_not(lane_lo)
                    colb = jnp.where(hm, tile, rl)
                    Lm = jnp.exp(jnp.where(tril, colb - colb.T, -1e30))
                    W = CB * Lm
                    Xh = jnp.where(hm, Xp, 0.0).astype(bf16)
                    dYh = jnp.where(hm, dYp, 0.0).astype(bf16)
                    dW = _dot(dYh, Xh, NT)
                    dXp = dXp + _dot(W.astype(bf16), dYh, TN)
                    E = dW * W
                    dCB = dCB + dW * Lm
                    rs = jnp.sum(E, axis=1, keepdims=True)
                    ccol = jnp.broadcast_to(jnp.sum(E, axis=0, keepdims=True), (L, L)).T
                    dA_t = dA_t + jnp.where(hm, rs - ccol, 0.0)
                dxbc_ref[:, lo:lo + LANES] = dXp * dtv[:, lo:lo + LANES] + dYp * dx_ref[:, lo:lo + LANES]
                ddtp_ref[:, lo:lo + LANES] = dXp * xsp
                ddx_ref[:, lo:lo + LANES] += jnp.sum(dYp * xsp, axis=0, keepdims=True)
                dA_ref[:, lo:lo + LANES] = dA_t
                dAl_ref[:, lo:lo + LANES] = dAl_t
                dst_ref[lo:lo + LANES, :] = dprev
            dCBb = dCB.astype(bf16)
            dxbc_ref[:, inner + g * STATE:inner + (g + 1) * STATE] = dBg + _dot(dCBb, Cg, TN)
            dxbc_ref[:, inner + GN + g * STATE:inner + GN + (g + 1) * STATE] = dCg + _dot(dCBb, Bg, NN)
        triu = (cidx >= r).astype(f32)
        dda = _dot(triu, dA_ref[...], NN, HI) + dAl_ref[...]
        ddt_full = ddtp_ref[...] + dda * ax_ref[...] * (1.0 / HEADDIM)
        ddt_ref[...] = _dot(ddt_full, et_ref[...], NN, HI)
        dax_ref[...] += jnp.sum(dda * dtv, axis=0, keepdims=True)

    xs_s, b_s, c_s, row_s, vec_s, st_s = _ssd_specs(S, inner, GN, nchunks, True)
    return pl.pallas_call(
        body, name=name, grid=(nchunks,),
        in_specs=[row_s, xs_s, b_s, c_s, row_s, vec_s, vec_s, st_s, _full_spec(et.shape)],
        out_specs=[pl.BlockSpec((L, Cc), lambda c: (nchunks - 1 - c, 0)),
                   pl.BlockSpec((L, H), lambda c: (nchunks - 1 - c, 0)), vec_s, vec_s],
        out_shape=[jax.ShapeDtypeStruct((S, Cc), f32), jax.ShapeDtypeStruct((S, H), f32),
                   jax.ShapeDtypeStruct((1, inner), f32), jax.ShapeDtypeStruct((1, inner), f32)],
        scratch_shapes=[pltpu.VMEM((inner, STATE), f32), pltpu.VMEM((L, inner), f32),
                        pltpu.VMEM((1, inner), f32), pltpu.VMEM((L, inner), f32)],
        compiler_params=_cparams(("arbitrary",)),
    )(dy, xbc, xbc, xbc, dtx, ax, dx, states, et)


def _dt_fwd(dt_pre, bias, e, name):
    H, inner = e.shape

    def fn(dp, b, e_):
        dt = _softplus(dp + b)
        return dt, _dot(dt, e_, NN, HI)

    return _rowcall(fn, name=name, rows=[dt_pre], fulls=[_row2(bias), e], out_rows=[(H, f32), (inner, f32)])


def _dt_bwd(ddt, dt_pre, bias, name):
    H = ddt.shape[1]

    def fn(dd, dp, b):
        g = dd * _sigmoid(dp + b)
        return g, jnp.sum(g, axis=0, keepdims=True)

    return _rowcall(fn, name=name, rows=[ddt, dt_pre], fulls=[_row2(bias)], out_rows=[(H, f32)], out_accs=[(1, H)])


def _gnorm_fwd(y, z, w, G, name):
    inner = y.shape[1]
    gs = inner // G

    def fn(y_, z_, w_):
        gg = y_ * _silu(z_)
        outs = []
        for g in range(G):
            sl = slice(g * gs, (g + 1) * gs)
            outs.append(_rms(gg[:, sl], w_[:, sl], LN_EPS))
        return jnp.concatenate(outs, axis=1)

    return _rowcall(fn, name=name, rows=[y, z], fulls=[_row2(w)], out_rows=[(inner, bf16)], tm=256)[0]


def _gnorm_bwd(dyn, y, z, w, G, name):
    inner = y.shape[1]
    gs = inner // G

    def fn(d_, y_, z_, w_):
        sz = _silu(z_)
        gg = y_ * sz
        dgs, dws = [], []
        for g in range(G):
            sl = slice(g * gs, (g + 1) * gs)
            dg, dw = _rms_bwd(d_[:, sl], gg[:, sl], w_[:, sl], LN_EPS)
            dgs.append(dg)
            dws.append(dw)
        dgg = jnp.concatenate(dgs, axis=1)
        return dgg * sz, dgg * y_ * _dsilu(z_), jnp.concatenate(dws, axis=1)

    return _rowcall(fn, name=name, rows=[dyn, y, z], fulls=[_row2(w)], out_rows=[(inner, f32), (inner, f32)],
                    out_accs=[(1, inner)], tm=256)


def _gmlp_parts(pre, lw, lb, I):
    hp = _gelu(pre)
    uu = hp[:, :I]
    vp = hp[:, I:]
    xc = vp - jnp.mean(vp, axis=-1, keepdims=True)
    rstd = lax.rsqrt(jnp.mean(xc * xc, axis=-1, keepdims=True) + LN_EPS)
    vhat = xc * rstd
    return uu, vhat, rstd, vhat * lw + lb


def _gmlp_mid_fwd(pre, b_in, ln_w, ln_b, w_s, bsx, name):
    S, two_i = pre.shape
    I = two_i // 2
    NG = w_s.shape[0]
    gd = I // NG
    L = CHUNK

    def body(pre_ref, bi_ref, lw_ref, lb_ref, ws_ref, bsx_ref, o_ref):
        uu, _, _, vv = _gmlp_parts(pre_ref[...] + bi_ref[...], lw_ref[...], lb_ref[...], I)
        r = lax.broadcasted_iota(jnp.int32, (L, L), 0)
        cidx = lax.broadcasted_iota(jnp.int32, (L, L), 1)
        tril = cidx <= r
        for g in range(NG):
            sl = slice(g * gd, (g + 1) * gd)
            wg = jnp.where(tril, ws_ref[g], 0.0).astype(bf16)
            mixed = _dot(wg, vv[:, sl].astype(bf16), NN) + bsx_ref[:, sl]
            o_ref[:, sl] = (uu[:, sl] * mixed).astype(o_ref.dtype)

    return pl.pallas_call(
        body, name=name, grid=(S // L,),
        in_specs=[_row_spec(L, two_i), _full_spec((1, two_i)), _full_spec((1, I)), _full_spec((1, I)), _full_spec(w_s.shape), _full_spec(bsx.shape)],
        out_specs=_row_spec(L, I), out_shape=jax.ShapeDtypeStruct((S, I), bf16),
        compiler_params=_cparams(("parallel",)),
    )(pre, _row2(b_in), _row2(ln_w), _row2(ln_b), w_s, bsx)


def _gmlp_mid_bwd(do, pre, b_in, ln_w, ln_b, w_s, bsx, name):
    S, two_i = pre.shape
    I = two_i // 2
    NG = w_s.shape[0]
    gd = I // NG
    L = CHUNK

    def body(do_ref, pre_ref, bi_ref, lw_ref, lb_ref, ws_ref, bsx_ref, dpre_ref, dbi_ref, dlw_ref, dlb_ref, dws_ref, dbs_ref, dvv_ref):
        ci = pl.program_id(0)

        @pl.when(ci == 0)
        def _():
            for ref in (dbi_ref, dlw_ref, dlb_ref, dws_ref, dbs_ref):
                ref[...] = jnp.zeros_like(ref)

        pre = pre_ref[...] + bi_ref[...]
        lw = lw_ref[...]
        uu, vhat, rstd, vv = _gmlp_parts(pre, lw, lb_ref[...], I)
        dov = do_ref[...]
        r = lax.broadcasted_iota(jnp.int32, (L, L), 0)
        cidx = lax.broadcasted_iota(jnp.int32, (L, L), 1)
        tril = cidx <= r
        duus = []
        for g in range(NG):
            sl = slice(g * gd, (g + 1) * gd)
            wg = jnp.where(tril, ws_ref[g], 0.0).astype(bf16)
            vg = vv[:, sl].astype(bf16)
            mixed = _dot(wg, vg, NN) + bsx_ref[:, sl]
            duus.append(dov[:, sl] * mixed)
            dmixed = dov[:, sl] * uu[:, sl]
            dbs_ref[:, sl] += dmixed
            dmb = dmixed.astype(bf16)
            dvv_ref[:, sl] = _dot(wg, dmb, TN)
            dws_ref[g] += jnp.where(tril, _dot(dmb, vg, NT), 0.0)
        duu = jnp.concatenate(duus, axis=1)
        dvv = dvv_ref[...]
        dlw_ref[...] += jnp.sum(dvv * vhat, axis=0, keepdims=True)
        dlb_ref[...] += jnp.sum(dvv, axis=0, keepdims=True)
        dvh = dvv * lw
        dvp = rstd * (dvh - jnp.mean(dvh, axis=-1, keepdims=True) - vhat * jnp.mean(dvh * vhat, axis=-1, keepdims=True))
        dpre = jnp.concatenate([duu, dvp], axis=1) * _dgelu(pre)
        dbi_ref[...] += jnp.sum(dpre, axis=0, keepdims=True)
        dpre_ref[...] = dpre.astype(dpre_ref.dtype)

    return pl.pallas_call(
        body, name=name, grid=(S // L,),
        in_specs=[_row_spec(L, I), _row_spec(L, two_i), _full_spec((1, two_i)), _full_spec((1, I)), _full_spec((1, I)),
                  _full_spec(w_s.shape), _full_spec(bsx.shape)],
        out_specs=[_row_spec(L, two_i), _full_spec((1, two_i)), _full_spec((1, I)), _full_spec((1, I)), _full_spec(w_s.shape), _full_spec((L, I))],
        out_shape=[jax.ShapeDtypeStruct((S, two_i), bf16), jax.ShapeDtypeStruct((1, two_i), f32), jax.ShapeDtypeStruct((1, I), f32),
                   jax.ShapeDtypeStruct((1, I), f32), jax.ShapeDtypeStruct(w_s.shape, f32), jax.ShapeDtypeStruct((L, I), f32)],
        scratch_shapes=[pltpu.VMEM((L, I), f32)],
        compiler_params=_cparams(("arbitrary",)),
    )(do, pre, _row2(b_in), _row2(ln_w), _row2(ln_b), w_s, bsx)


def _lane_group_sum(acc, eg, name):
    NG = eg.shape[1]
    return _rowcall(lambda a, e: _dot(a, e, NN, HI), name=name, rows=[acc], fulls=[eg], out_rows=[(NG, f32)])[0]


def _ffn_up(u, wg, wu, name):
    S, D = u.shape
    nb, _, F4 = wg.shape
    tm = _tile(S, 1024)

    def body(u_ref, wg_ref, wu_ref, g_ref, up_ref, a_ref):
        uv = u_ref[...]
        g = _dot(uv, wg_ref[...], NN)
        up = _dot(uv, wu_ref[...], NN)
        g_ref[...] = g
        up_ref[...] = up
        a_ref[...] = (_silu(g) * up).astype(a_ref.dtype)

    wspec = pl.BlockSpec((None, D, F4), lambda k, i: (k, 0, 0))
    ospec = pl.BlockSpec((None, tm, F4), lambda k, i: (k, i, 0))
    return pl.pallas_call(
        body, name=name, grid=(nb, S // tm),
        in_specs=[pl.BlockSpec((tm, D), lambda k, i: (i, 0)), wspec, wspec],
        out_specs=[ospec, ospec, ospec],
        out_shape=[jax.ShapeDtypeStruct((nb, S, F4), f32), jax.ShapeDtypeStruct((nb, S, F4), f32), jax.ShapeDtypeStruct((nb, S, F4), bf16)],
        compiler_params=_cparams(("parallel", "parallel")),
    )(u, wg, wu)


def _ffn_bwd_act(dh, wd, G, U, name):
    S, D = dh.shape
    nb, F4, _ = wd.shape
    tm = _tile(S, 1024)

    def body(dh_ref, wd_ref, g_ref, up_ref, dg_ref, du_ref):
        dA = _dot(dh_ref[...].astype(bf16), wd_ref[...], NT)
        g = g_ref[...]
        dg_ref[...] = (dA * up_ref[...] * _dsilu(g)).astype(dg_ref.dtype)
        du_ref[...] = (dA * _silu(g)).astype(du_ref.dtype)

    ospec = pl.BlockSpec((None, tm, F4), lambda k, i: (k, i, 0))
    return pl.pallas_call(
        body, name=name, grid=(nb, S // tm),
        in_specs=[pl.BlockSpec((tm, D), lambda k, i: (i, 0)), pl.BlockSpec((None, F4, D), lambda k, i: (k, 0, 0)), ospec, ospec],
        out_specs=[ospec, ospec],
        out_shape=[jax.ShapeDtypeStruct((nb, S, F4), bf16)] * 2,
        compiler_params=_cparams(("parallel", "parallel")),
    )(dh, wd, G, U)


def _rms_bwd_add(dres, du, h, w, name):
    D = h.shape[1]

    def fn(dr, du_, h_, w_):
        dx, dw = _rms_bwd(du_, h_, w_, RMS_EPS)
        return dr + dx, dw

    return _rowcall(fn, name=name, rows=[dres, du, h], fulls=[_row2(w)], out_rows=[(D, f32)], out_accs=[(1, D)])


def _ple_fwd(h, p_i, wp, pn, gn, wgate, name):
    D = h.shape[1]

    def fn(h_, p_, wp_, pn_, gn_, wg_):
        pe = _dot(p_.astype(bf16), wp_, NN)
        e = _rms(pe, pn_, RMS_EPS)
        q = _rms(h_, gn_, RMS_EPS)
        gate = _sigmoid(_dot(q.astype(bf16), wg_, NN))
        return h_ + gate * e, pe, gate

    return _rowcall(fn, name=name, rows=[h, p_i], fulls=[wp, _row2(pn), _row2(gn), wgate],
                    out_rows=[(D, f32), (D, f32), (D, f32)], tm=256)


def _ple_bwd(dh3, h, pe, gate, pn, gn, wgate, name):
    D = h.shape[1]

    def fn(d_, h_, pe_, gate_, pn_, gn_, wg_):
        e = _rms(pe_, pn_, RMS_EPS)
        dzg = d_ * e * gate_ * (1.0 - gate_)
        dq = _dot(dzg.astype(bf16), wg_, NT)
        dxq, dgn = _rms_bwd(dq, h_, gn_, RMS_EPS)
        dpe, dpn = _rms_bwd(d_ * gate_, pe_, pn_, RMS_EPS)
        return d_ + dxq, dzg, dpe, _rms(h_, gn_, RMS_EPS), dpn, dgn

    return _rowcall(fn, name=name, rows=[dh3, h, pe, gate], fulls=[_row2(pn), _row2(gn), wgate],
                    out_rows=[(D, f32), (D, bf16), (D, bf16), (D, bf16)], out_accs=[(1, D), (1, D)], tm=256)


def _loss_head(h, target, fn_w, name):
    D = h.shape[1]

    def fn(h_, t_, w_):
        diff = _rms(h_, w_, RMS_EPS) - t_
        loss = 0.5 * jnp.sum(jnp.mean(diff * diff, axis=-1, keepdims=True), axis=0, keepdims=True)
        dh, dw = _rms_bwd(diff * (1.0 / D), h_, w_, RMS_EPS)
        return dh, jnp.broadcast_to(loss, (1, LANES)), dw

    return _rowcall(fn, name=name, rows=[h, target], fulls=[_row2(fn_w)], out_rows=[(D, f32)], out_accs=[(1, LANES), (1, D)])


def _adamw(w, m, v, g, name):
    R, C = w.shape
    tr = R
    while tr * C > 256 * 1024 and tr % (2 * SUBLANES) == 0:
        tr //= 2

    def body(w_ref, m_ref, v_ref, g_ref, d_ref, mo_ref, vo_ref):
        g = g_ref[...]
        mn = ADAM_B1 * m_ref[...] + (1.0 - ADAM_B1) * g
        vn = ADAM_B2 * v_ref[...] + (1.0 - ADAM_B2) * (g * g)
        m_hat = mn / (1.0 - ADAM_B1 ** ADAM_STEP)
        v_hat = vn / (1.0 - ADAM_B2 ** ADAM_STEP)
        d_ref[...] = -ADAM_LR * (m_hat / (jnp.sqrt(v_hat) + ADAM_EPS) + ADAM_WD * w_ref[...])
        mo_ref[...] = mn
        vo_ref[...] = vn

    spec = pl.BlockSpec((tr, C), lambda i: (i, 0))
    return pl.pallas_call(
        body, name=name, grid=(R // tr,), in_specs=[spec] * 4,
        out_specs=[spec] * 3, out_shape=[jax.ShapeDtypeStruct((R, C), f32)] * 3,
        compiler_params=_cparams(("parallel",)),
    )(w, m, v, g)


def _expand_onehot(n, per):
    lane = lax.broadcasted_iota(jnp.int32, (n, n * per), 1)
    row = lax.broadcasted_iota(jnp.int32, (n, n * per), 0)
    return (lane // per == row).astype(f32)


def _ssd_layer_fwd(h, nm_w, W, t):
    H = W["dt_bias"].shape[0]
    inner = H * HEADDIM
    G = (W["conv_b"].shape[0] - inner) // (2 * STATE)
    hn = _rms_fwd(h, nm_w, f"rms_mix_{t}")
    z = _mm(hn, W["wz"], name=f"ssd_z_{t}")
    xpre = _mm(hn, W["wxbc"], name=f"ssd_xbc_{t}")
    dt_pre = _mm(hn, W["wdt"], name=f"ssd_dt_{t}")
    c, xbc = _conv_fwd(xpre, W["conv_w"], W["conv_b"], f"ssd_conv_{t}")
    _, dtx = _dt_fwd(dt_pre, W["dt_bias"], _expand_onehot(H, HEADDIM), f"ssd_dtx_{t}")
    a = -jnp.exp(W["a_log"])
    ax = _row2(jnp.repeat(a, HEADDIM))
    dx = _row2(jnp.repeat(W["d"], HEADDIM))
    y, states = _ssd_fwd(xbc, dtx, ax, dx, G, f"ssd_scan_{t}")
    yn = _gnorm_fwd(y, z, W["norm_w"], G, f"ssd_gnorm_{t}")
    h1 = _mm(yn, W["wout"], res=h, name=f"ssd_out_{t}")
    return h1, (h, hn, z, xpre, dt_pre, c, xbc, dtx, a, ax, dx, y, states, yn)


def _ssd_layer_bwd(dh1, saved, nm_w, W, t):
    h, hn, z, xpre, dt_pre, c, xbc, dtx, a, ax, dx, y, states, yn = saved
    H = W["dt_bias"].shape[0]
    inner = H * HEADDIM
    G = (W["conv_b"].shape[0] - inner) // (2 * STATE)
    dyn = _mm(dh1, W["wout"], mode="nt", name=f"ssd_out_dx_{t}")
    g_wout = _mm(yn, dh1, mode="tn", out_dtype=bf16, name=f"ssd_out_dw_{t}")
    dy, dz, g_normw = _gnorm_bwd(dyn, y, z, W["norm_w"], G, f"ssd_gnorm_bwd_{t}")
    dxbc, ddt, dax, ddx = _ssd_bwd(dy, xbc, dtx, ax, dx, states, _expand_onehot(H, HEADDIM).T, G, f"ssd_scan_bwd_{t}")
    dc, g_convw8, g_convb = _conv_bwd_dc(dxbc, c, xpre, f"ssd_conv_bwd_dc_{t}")
    dxpre = _conv_bwd_dx(dc, W["conv_w"], f"ssd_conv_bwd_dx_{t}")
    ddt_pre, g_dtb = _dt_bwd(ddt, dt_pre, W["dt_bias"], f"ssd_dt_bwd_{t}")
    g_wz = _mm(hn, dz, mode="tn", out_dtype=bf16, name=f"ssd_z_dw_{t}")
    g_wxbc = _mm(hn, dxpre, mode="tn", out_dtype=bf16, name=f"ssd_xbc_dw_{t}")
    g_wdt = _mm(hn, ddt_pre, mode="tn", out_dtype=bf16, name=f"ssd_dt_dw_{t}")
    dhn = _mm(dz, W["wz"], mode="nt", name=f"ssd_z_dx_{t}")
    dhn = _mm(dxpre, W["wxbc"], mode="nt", res=dhn, name=f"ssd_xbc_dx_{t}")
    dhn = _mm(ddt_pre, W["wdt"], mode="nt", res=dhn, name=f"ssd_dt_dx_{t}")
    dh, g_nm = _rms_bwd_add(dh1, dhn, h, nm_w, f"rms_mix_bwd_{t}")
    grads = dict(
        w_in=jnp.concatenate([g_wz, g_wxbc, g_wdt], axis=1), wout=g_wout,
        conv_w=g_convw8[:CONV_K], conv_b=g_convb[0], dt_bias=g_dtb[0],
        a_log=dax[0].reshape(H, HEADDIM)[:, 0] * a, d=jnp.sum(ddx[0].reshape(H, HEADDIM), axis=1),
        norm_w=g_normw[0], norm_mix=g_nm[0])
    return dh, grads


def _gmlp_layer_fwd(h, nm_w, W, t):
    NG, L, _ = W["w_s"].shape
    I = W["ln_w"].shape[0]
    hn = _rms_fwd(h, nm_w, f"rms_mix_{t}")
    pre = _mm(hn, W["win"], name=f"gmlp_in_{t}")
    bsx = jnp.repeat(W["b_s"].T, I // NG, axis=1)
    o = _gmlp_mid_fwd(pre, W["b_in"], W["ln_w"], W["ln_b"], W["w_s"], bsx, f"gmlp_mid_{t}")
    h1 = _mm(o, W["wout"], res=h, name=f"gmlp_out_{t}")
    return h1, (h, hn, pre, bsx, o)


def _gmlp_layer_bwd(dh1, saved, nm_w, W, t):
    h, hn, pre, bsx, o = saved
    NG = W["w_s"].shape[0]
    I = W["ln_w"].shape[0]
    do = _mm(dh1, W["wout"], mode="nt", name=f"gmlp_out_dx_{t}")
    g_wout = _mm(o, dh1, mode="tn", out_dtype=bf16, name=f"gmlp_out_dw_{t}")
    dpre, g_bin, g_lnw, g_lnb, g_ws, dbs = _gmlp_mid_bwd(do, pre, W["b_in"], W["ln_w"], W["ln_b"], W["w_s"], bsx, f"gmlp_mid_bwd_{t}")
    g_bs = _lane_group_sum(dbs, _expand_onehot(NG, I // NG).T, f"gmlp_bs_{t}").T
    g_win = _mm(hn, dpre, mode="tn", out_dtype=bf16, name=f"gmlp_in_dw_{t}")
    dhn = _mm(dpre, W["win"], mode="nt", name=f"gmlp_in_dx_{t}")
    dh, g_nm = _rms_bwd_add(dh1, dhn, h, nm_w, f"rms_mix_bwd_{t}")
    grads = dict(win=g_win, wout=g_wout, b_in=g_bin[0], ln_w=g_lnw[0], ln_b=g_lnb[0], w_s=g_ws, b_s=g_bs, norm_mix=g_nm[0])
    return dh, grads


def _ffn_fwd(h1, nf_w, W, t):
    u = _rms_fwd(h1, nf_w, f"rms_ffn_{t}")
    Gm, Um, A = _ffn_up(u, W["wg"], W["wu"], f"ffn_up_{t}")
    h2 = _mm(A, W["wd"], kbatch=True, res=h1, name=f"ffn_down_{t}")
    return h2, (h1, u, Gm, Um, A)


def _ffn_bwd(dh2, saved, nf_w, W, t):
    h1, u, Gm, Um, A = saved
    dG, dU = _ffn_bwd_act(dh2, W["wd"], Gm, Um, f"ffn_act_bwd_{t}")
    g_wd = _mm(A, dh2, mode="tn", out_dtype=bf16, name=f"ffn_down_dw_{t}")
    g_wg = _mm(u, dG, mode="tn", out_dtype=bf16, name=f"ffn_gate_dw_{t}")
    g_wu = _mm(u, dU, mode="tn", out_dtype=bf16, name=f"ffn_up_dw_{t}")
    du = _mm(dG, W["wg"], mode="nt", kbatch=True, name=f"ffn_gate_dx_{t}")
    du = _mm(dU, W["wu"], mode="nt", kbatch=True, res=du, name=f"ffn_up_dx_{t}")
    dh1, g_nf = _rms_bwd_add(dh2, du, h1, nf_w, f"rms_ffn_bwd_{t}")
    return dh1, dict(wg=g_wg, wu=g_wu, wd=g_wd, norm_ffn=g_nf[0])


def _local_step(x, p, target, W):
    depth = p.shape[0]
    h = x
    saved = []
    for i in range(depth):
        j = i // 2
        if i % 2 == 0:
            h1, s_mix = _ssd_layer_fwd(h, W["norm_mix"][i], W["ssd"][j], i)
        else:
            h1, s_mix = _gmlp_layer_fwd(h, W["norm_mix"][i], W["gmlp"][j], i)
        h2, s_ffn = _ffn_fwd(h1, W["norm_ffn"][i], W["ffn"][i], i)
        P = W["ple"][i]
        h3, pe, gate = _ple_fwd(h2, p[i], P["wp"], P["pn"], P["gn"], P["wgate"], f"ple_{i}")
        saved.append((s_mix, s_ffn, (h2, pe, gate)))
        h = h3
    dh, loss, g_fn = _loss_head(h, target, W["final_norm"], "loss_head")
    grads = dict(final_norm=g_fn[0], ssd=[None] * len(W["ssd"]), gmlp=[None] * len(W["gmlp"]), ffn=[None] * depth, ple=[None] * depth)
    for i in reversed(range(depth)):
        j = i // 2
        s_mix, s_ffn, (h2, pe, gate) = saved[i]
        P = W["ple"][i]
        dh, dzg, dpe, q, g_pn, g_gn = _ple_bwd(dh, h2, pe, gate, P["pn"], P["gn"], P["wgate"], f"ple_bwd_{i}")
        grads["ple"][i] = dict(
            wgate=_mm(q, dzg, mode="tn", out_dtype=bf16, name=f"ple_gate_dw_{i}"),
            wp=_mm(p[i], dpe, mode="tn", out_dtype=bf16, name=f"ple_proj_dw_{i}"), pn=g_pn[0], gn=g_gn[0])
        dh, grads["ffn"][i] = _ffn_bwd(dh, s_ffn, W["norm_ffn"][i], W["ffn"][i], i)
        if i % 2 == 0:
            dh, grads["ssd"][j] = _ssd_layer_bwd(dh, s_mix, W["norm_mix"][i], W["ssd"][j], i)
        else:
            dh, grads["gmlp"][j] = _gmlp_layer_bwd(dh, s_mix, W["norm_mix"][i], W["gmlp"][j], i)
    return loss[0, 0], dh, grads


def _flip(v, f):
    return 1 - v if f else v


_ANY = pl.BlockSpec(memory_space=pl.ANY)


def _gather_chips(arrs):
    flat = [(t, l) for t, a in enumerate(arrs) for l in range(a.shape[0])]
    n_in, n_out = len(arrs), len(flat)
    flips = [(1, 0), (0, 1), (1, 1)]
    nf = len(flips)

    def body(*refs):
        ins, outs = refs[:n_in], refs[n_in:n_in + n_out]
        send_sems, recv_sems, fsend_sems, frecv_sems, loc_sems = refs[n_in + n_out:]
        x, y, c = lax.axis_index("x"), lax.axis_index("y"), lax.axis_index("c")
        mychip = 2 * x + y

        def rows(o, h):
            r2 = arrs[flat[o][0]].shape[1] // 2
            return pl.ds(h * r2, r2)

        def ici(o, j, slot):
            t, l = flat[o]
            fx, fy = flips[j]
            return pltpu.make_async_remote_copy(
                src_ref=ins[t].at[l, rows(o, c)], dst_ref=outs[o].at[slot, rows(o, c)], send_sem=send_sems.at[nf * o + j],
                recv_sem=recv_sems.at[nf * o + j], device_id=(_flip(x, fx), _flip(y, fy), c), device_id_type=MESH)

        def forward(o, j, h):
            fx, fy = flips[j]
            part = outs[o].at[2 * _flip(x, fx) + _flip(y, fy), rows(o, h)]
            return pltpu.make_async_remote_copy(
                src_ref=part, dst_ref=part, send_sem=fsend_sems.at[nf * o + j], recv_sem=frecv_sems.at[nf * o + j],
                device_id=(x, y, 1 - c), device_id_type=MESH)

        local = [pltpu.make_async_copy(ins[t].at[l], outs[o].at[mychip], loc_sems.at[o]) for o, (t, l) in enumerate(flat)]
        sends = [ici(o, j, mychip) for o in range(n_out) for j in range(nf)]
        for cp in local + sends:
            cp.start()
        forwards = []
        for o in range(n_out):
            for j, (fx, fy) in enumerate(flips):
                ici(o, j, 2 * _flip(x, fx) + _flip(y, fy)).wait_recv()
                forwards.append(forward(o, j, c))
                forwards[-1].start()
        for o in range(n_out):
            for j in range(nf):
                forward(o, j, 1 - c).wait_recv()
        for cp in sends + forwards:
            cp.wait_send()
        for cp in local:
            cp.wait()

    dma = pltpu.SemaphoreType.DMA
    outs = pl.pallas_call(
        body, name="gather_weights", in_specs=[_ANY] * n_in, out_specs=[_ANY] * n_out,
        out_shape=[jax.ShapeDtypeStruct((N_CHIPS,) + arrs[t].shape[1:], arrs[t].dtype) for t, _ in flat],
        scratch_shapes=[dma((nf * n_out,)), dma((nf * n_out,)), dma((nf * n_out,)), dma((nf * n_out,)), dma((n_out,))],
    )(*arrs)
    res = [[] for _ in arrs]
    for o, (t, _) in enumerate(flat):
        res[t].append(outs[o])
    return res


def _half_struct(a, lead):
    return jax.ShapeDtypeStruct(lead + (a.shape[-2] // 2, a.shape[-1]), a.dtype)


def _sibling_split(tensors, rep):
    n = len(tensors)

    def body(*refs):
        ins, rep_ref = refs[:n], refs[n]
        own, got, rep_got = refs[n + 1:2 * n + 1], refs[2 * n + 1:3 * n + 1], refs[3 * n + 1]
        send_sems, recv_sems, loc_sems = refs[3 * n + 2:]
        x, y, c = lax.axis_index("x"), lax.axis_index("y"), lax.axis_index("c")

        def half(i, h):
            r2 = tensors[i].shape[1] // 2
            return ins[i].at[:, pl.ds(h * r2, r2)]

        def swap(i):
            src, dst = (half(i, 1 - c), got[i]) if i < n else (rep_ref, rep_got)
            return pltpu.make_async_remote_copy(src_ref=src, dst_ref=dst, send_sem=send_sems.at[i], recv_sem=recv_sems.at[i],
                                                device_id=(x, y, 1 - c), device_id_type=MESH)

        copies = [swap(i) for i in range(n + 1)]
        local = [pltpu.make_async_copy(half(i, c), own[i], loc_sems.at[i]) for i in range(n)]
        for cp in copies + local:
            cp.start()
        for cp in copies + local:
            cp.wait()

    dma = pltpu.SemaphoreType.DMA
    halves = [_half_struct(t, (N_CHIPS,)) for t in tensors]
    outs = pl.pallas_call(
        body, name="grads_sibling_split", in_specs=[_ANY] * (n + 1), out_specs=[_ANY] * (2 * n + 1),
        out_shape=halves + halves + [jax.ShapeDtypeStruct(rep.shape, rep.dtype)],
        scratch_shapes=[dma((n + 1,)), dma((n + 1,)), dma((n,))],
    )(*tensors, rep)
    return outs[:n], outs[n:2 * n], outs[2 * n]


def _chip_exchange(parts, rep):
    n = len(parts)
    flips = [(1, 0), (0, 1), (1, 1)]
    nf = len(flips)

    def body(*refs):
        ins = refs[:n + 1]
        outs = refs[n + 1:2 * n + 2]
        send_sems, recv_sems, loc_sems = refs[2 * n + 2:]
        x, y, c = lax.axis_index("x"), lax.axis_index("y"), lax.axis_index("c")
        mychip = 2 * x + y

        def mine(i, chip):
            return ins[i].at[chip] if i < n else ins[i]

        def copy(i, j, slot):
            px, py = _flip(x, flips[j][0]), _flip(y, flips[j][1])
            return pltpu.make_async_remote_copy(
                src_ref=mine(i, 2 * px + py), dst_ref=outs[i].at[slot], send_sem=send_sems.at[nf * i + j],
                recv_sem=recv_sems.at[nf * i + j], device_id=(px, py, c), device_id_type=MESH)

        local = [pltpu.make_async_copy(mine(i, mychip), outs[i].at[mychip], loc_sems.at[i]) for i in range(n + 1)]
        sends = [copy(i, j, mychip) for i in range(n + 1) for j in range(nf)]
        for cp in local + sends:
            cp.start()
        for i in range(n + 1):
            for j, (fx, fy) in enumerate(flips):
                copy(i, j, 2 * _flip(x, fx) + _flip(y, fy)).wait_recv()
        for cp in sends:
            cp.wait_send()
        for cp in local:
            cp.wait()

    dma = pltpu.SemaphoreType.DMA
    outs = pl.pallas_call(
        body, name="grads_chip_exchange", in_specs=[_ANY] * (n + 1), out_specs=[_ANY] * (n + 1),
        out_shape=[jax.ShapeDtypeStruct(t.shape, t.dtype) for t in parts] + [jax.ShapeDtypeStruct((N_CHIPS,) + rep.shape, rep.dtype)],
        scratch_shapes=[dma((nf * (n + 1),)), dma((nf * (n + 1),)), dma((n + 1,))],
    )(*parts, rep)
    return outs[:n], outs[n]


def _sibling_join(groups):
    flat = [(gi, l) for gi, arrs in enumerate(groups) for l in range(len(arrs))]
    n, n_out = len(flat), len(groups)

    def body(*refs):
        ins, outs = refs[:n], refs[n:n + n_out]
        send_sems, recv_sems, loc_sems = refs[n + n_out:]
        x, y, c = lax.axis_index("x"), lax.axis_index("y"), lax.axis_index("c")

        def place(i, h):
            gi, l = flat[i]
            r2 = groups[gi][l].shape[0]
            return outs[gi].at[l, pl.ds(h * r2, r2)]

        def push(i, h):
            return pltpu.make_async_remote_copy(src_ref=ins[i], dst_ref=place(i, h), send_sem=send_sems.at[i], recv_sem=recv_sems.at[i],
                                                device_id=(x, y, 1 - c), device_id_type=MESH)

        local = [pltpu.make_async_copy(ins[i], place(i, c), loc_sems.at[i]) for i in range(n)]
        sends = [push(i, c) for i in range(n)]
        for cp in local + sends:
            cp.start()
        for i in range(n):
            push(i, 1 - c).wait_recv()
        for cp in sends:
            cp.wait_send()
        for cp in local:
            cp.wait()

    dma = pltpu.SemaphoreType.DMA
    return pl.pallas_call(
        body, name="grads_sibling_join", in_specs=[_ANY] * n, out_specs=[_ANY] * n_out,
        out_shape=[jax.ShapeDtypeStruct((len(arrs), 2 * arrs[0].shape[0], arrs[0].shape[1]), arrs[0].dtype) for arrs in groups],
        scratch_shapes=[dma((n,)), dma((n,)), dma((n,))],
    )(*[a for arrs in groups for a in arrs])


def _sum_tile_rows(rows, cols):
    tr = rows
    while tr * cols > 256 * 1024 and tr % (4 * SUBLANES) == 0:
        tr //= 2
    return tr


def _add2(a, b, name):
    shp = a.shape
    a2, b2 = a.reshape(-1, shp[-1]), b.reshape(-1, shp[-1])
    R, C = a2.shape
    tr = _sum_tile_rows(R, C)

    def body(a_ref, b_ref, o_ref):
        o_ref[...] = (a_ref[...].astype(f32) + b_ref[...].astype(f32)).astype(o_ref.dtype)

    spec = pl.BlockSpec((tr, C), lambda i: (i, 0))
    return pl.pallas_call(body, name=name, grid=(R // tr,), in_specs=[spec, spec], out_specs=spec,
                          out_shape=jax.ShapeDtypeStruct((R, C), a.dtype), compiler_params=_cparams(("parallel",)))(a2, b2).reshape(shp)


def _sum_leading(parts, name):
    n, R, C = parts.shape
    tr = _sum_tile_rows(R, C)

    def body(p_ref, o_ref):
        s = p_ref[0].astype(f32)
        for k in range(1, n):
            s = s + p_ref[k].astype(f32)
        o_ref[...] = s

    return pl.pallas_call(body, name=name, grid=(R // tr,), in_specs=[pl.BlockSpec((n, tr, C), lambda i: (0, i, 0))],
                          out_specs=pl.BlockSpec((tr, C), lambda i: (i, 0)), out_shape=jax.ShapeDtypeStruct((R, C), f32),
                          compiler_params=_cparams(("parallel",)))(parts)


PACK_COLS = 1024
PACK_ROW_MULTIPLE = 64

BIG = ("ssd_w_in", "ssd_w_out", "gmlp_w_in", "gmlp_w_out", "ffn_w_gate", "ffn_w_up", "ffn_w_down", "ple_w_proj", "ple_w_gate")
SMALL_SHARDED = ("ssd_conv_w", "gmlp_b_in", "gmlp_ln_w", "gmlp_ln_b")
REPLICATED = ("norm_mix", "norm_ffn", "ssd_conv_b", "ssd_dt_bias", "ssd_a_log", "ssd_d", "ssd_norm_w", "gmlp_w_s", "gmlp_b_s",
              "ple_norm", "ple_gate_norm", "final_norm")
WEIGHTS = ("norm_mix", "norm_ffn", "ssd_w_in", "ssd_conv_w", "ssd_conv_b", "ssd_dt_bias", "ssd_a_log", "ssd_d", "ssd_norm_w", "ssd_w_out",
           "gmlp_w_in", "gmlp_b_in", "gmlp_ln_w", "gmlp_ln_b", "gmlp_w_s", "gmlp_b_s", "gmlp_w_out", "ffn_w_gate", "ffn_w_up",
           "ffn_w_down", "ple_w_proj", "ple_norm", "ple_gate_norm", "ple_w_gate", "final_norm")
COLUMN_SHARDED = ("ssd_w_in", "gmlp_w_in", "ple_w_proj")


def _pack(arrs):
    flat = jnp.concatenate([a.reshape(-1).astype(f32) for a in arrs])
    per = PACK_COLS * PACK_ROW_MULTIPLE
    n = -(-flat.shape[0] // per) * per
    return jnp.pad(flat, (0, n - flat.shape[0])).reshape(-1, PACK_COLS)


def _unpack(buf, shapes):
    flat = buf.reshape(-1)
    out, o = [], 0
    for s in shapes:
        n = math.prod(s)
        out.append(flat[o:o + n].reshape(s))
        o += n
    return out


def _chip_major(g):
    r, c4 = g.shape
    return g.reshape(r, N_CHIPS, c4 // N_CHIPS).transpose(1, 0, 2)


def _from_chip_major(g):
    k, r, c = g.shape
    return g.transpose(1, 0, 2).reshape(r, k * c)


def _adamw_nd(w, m, v, g, name):
    shp = w.shape
    two = lambda a: a.reshape(-1, shp[-1])
    return [o.reshape(shp) for o in _adamw(two(w), two(m), two(v), two(g), name)]


def kernel(x, p, norm_mix, norm_ffn, ssd_w_in, ssd_conv_w, ssd_conv_b, ssd_dt_bias, ssd_a_log, ssd_d, ssd_norm_w, ssd_w_out, gmlp_w_in, gmlp_b_in, gmlp_ln_w, gmlp_ln_b, gmlp_w_s, gmlp_b_s, gmlp_w_out, ffn_w_gate, ffn_w_up, ffn_w_down, ple_w_proj, ple_norm, ple_gate_norm, ple_w_gate, final_norm, loss_target, m_norm_mix, m_norm_ffn, m_ssd_w_in, m_ssd_conv_w, m_ssd_conv_b, m_ssd_dt_bias, m_ssd_a_log, m_ssd_d, m_ssd_norm_w, m_ssd_w_out, m_gmlp_w_in, m_gmlp_b_in, m_gmlp_ln_w, m_gmlp_ln_b, m_gmlp_w_s, m_gmlp_b_s, m_gmlp_w_out, m_ffn_w_gate, m_ffn_w_up, m_ffn_w_down, m_ple_w_proj, m_ple_norm, m_ple_gate_norm, m_ple_w_gate, m_final_norm, v_norm_mix, v_norm_ffn, v_ssd_w_in, v_ssd_conv_w, v_ssd_conv_b, v_ssd_dt_bias, v_ssd_a_log, v_ssd_d, v_ssd_norm_w, v_ssd_w_out, v_gmlp_w_in, v_gmlp_b_in, v_gmlp_ln_w, v_gmlp_ln_b, v_gmlp_w_s, v_gmlp_b_s, v_gmlp_w_out, v_ffn_w_gate, v_ffn_w_up, v_ffn_w_down, v_ple_w_proj, v_ple_norm, v_ple_gate_norm, v_ple_w_gate, v_final_norm):
    given = dict(locals())
    w = {n: given[n] for n in WEIGHTS}
    mom = {n: given["m_" + n] for n in WEIGHTS}
    var = {n: given["v_" + n] for n in WEIGHTS}
    depth = p.shape[0]
    n_ssd, n_gmlp = ssd_w_in.shape[0], gmlp_w_in.shape[0]
    inner = ssd_dt_bias.shape[1] * HEADDIM
    conv_dim = ssd_conv_b.shape[1]

    small_shapes = [w[n].shape for n in SMALL_SHARDED]
    gathered = _gather_chips([w[n].astype(bf16) for n in BIG] + [_pack([w[n] for n in SMALL_SHARDED])[None]])
    gw = dict(zip(BIG, gathered[:-1]))
    small_by_chip = [_unpack(gathered[-1][0][k], small_shapes) for k in range(N_CHIPS)]
    small_full = {n: jnp.concatenate([small_by_chip[k][i] for k in range(N_CHIPS)], axis=-1) for i, n in enumerate(SMALL_SHARDED)}

    W = dict(norm_mix=norm_mix, norm_ffn=norm_ffn, final_norm=final_norm, ssd=[], gmlp=[], ffn=[], ple=[])
    for j in range(n_ssd):
        w_in = _from_chip_major(gw["ssd_w_in"][j])
        W["ssd"].append(dict(
            wz=w_in[:, :inner], wxbc=w_in[:, inner:inner + conv_dim], wdt=w_in[:, inner + conv_dim:],
            conv_w=small_full["ssd_conv_w"][j], conv_b=ssd_conv_b[j], dt_bias=ssd_dt_bias[j], a_log=ssd_a_log[j], d=ssd_d[j],
            norm_w=ssd_norm_w[j], wout=gw["ssd_w_out"][j].reshape(-1, gw["ssd_w_out"][j].shape[-1])))
    for j in range(n_gmlp):
        W["gmlp"].append(dict(
            win=_from_chip_major(gw["gmlp_w_in"][j]), b_in=small_full["gmlp_b_in"][j], ln_w=small_full["gmlp_ln_w"][j],
            ln_b=small_full["gmlp_ln_b"][j], w_s=gmlp_w_s[j], b_s=gmlp_b_s[j],
            wout=gw["gmlp_w_out"][j].reshape(-1, gw["gmlp_w_out"][j].shape[-1])))
    for i in range(depth):
        W["ffn"].append(dict(wg=gw["ffn_w_gate"][i], wu=gw["ffn_w_up"][i], wd=gw["ffn_w_down"][i]))
        W["ple"].append(dict(wp=_from_chip_major(gw["ple_w_proj"][i]), pn=ple_norm[i], gn=ple_gate_norm[i],
                             wgate=gw["ple_w_gate"][i].reshape(-1, gw["ple_w_gate"][i].shape[-1])))

    loss_part, grad_x, g = _local_step(x[0], p[:, 0], loss_target[0], W)
    loss = lax.psum(loss_part, ("x", "y", "c"))

    rows4 = lambda a: a.reshape((N_CHIPS, a.shape[0] // N_CHIPS) + a.shape[1:])
    big = dict(
        ssd_w_in=[_chip_major(s["w_in"]) for s in g["ssd"]], ssd_w_out=[rows4(s["wout"]) for s in g["ssd"]],
        gmlp_w_in=[_chip_major(s["win"]) for s in g["gmlp"]], gmlp_w_out=[rows4(s["wout"]) for s in g["gmlp"]],
        ffn_w_gate=[s["wg"] for s in g["ffn"]], ffn_w_up=[s["wu"] for s in g["ffn"]], ffn_w_down=[s["wd"] for s in g["ffn"]],
        ple_w_proj=[_chip_major(s["wp"]) for s in g["ple"]], ple_w_gate=[rows4(s["wgate"]) for s in g["ple"]])
    small_g = dict(ssd_conv_w=jnp.stack([s["conv_w"] for s in g["ssd"]]), gmlp_b_in=jnp.stack([s["b_in"] for s in g["gmlp"]]),
                   gmlp_ln_w=jnp.stack([s["ln_w"] for s in g["gmlp"]]), gmlp_ln_b=jnp.stack([s["ln_b"] for s in g["gmlp"]]))
    cut = lambda a, k: a[..., k * (a.shape[-1] // N_CHIPS):(k + 1) * (a.shape[-1] // N_CHIPS)]
    small_packed = jnp.stack([_pack([cut(small_g[n], k) for n in SMALL_SHARDED]) for k in range(N_CHIPS)])
    mix = [g["ssd"][i // 2]["norm_mix"] if i % 2 == 0 else g["gmlp"][i // 2]["norm_mix"] for i in range(depth)]
    rep_g = dict(
        norm_mix=jnp.stack(mix), norm_ffn=jnp.stack([s["norm_ffn"] for s in g["ffn"]]),
        ssd_conv_b=jnp.stack([s["conv_b"] for s in g["ssd"]]), ssd_dt_bias=jnp.stack([s["dt_bias"] for s in g["ssd"]]),
        ssd_a_log=jnp.stack([s["a_log"] for s in g["ssd"]]), ssd_d=jnp.stack([s["d"] for s in g["ssd"]]),
        ssd_norm_w=jnp.stack([s["norm_w"] for s in g["ssd"]]), gmlp_w_s=jnp.stack([s["w_s"] for s in g["gmlp"]]),
        gmlp_b_s=jnp.stack([s["b_s"] for s in g["gmlp"]]), ple_norm=jnp.stack([s["pn"] for s in g["ple"]]),
        ple_gate_norm=jnp.stack([s["gn"] for s in g["ple"]]), final_norm=g["final_norm"])
    rep_packed = _pack([rep_g[n] for n in REPLICATED])
    tensors = [a for n in BIG for a in big[n]] + [small_packed]
    own, got, rep_got = _sibling_split(tensors, rep_packed)
    pair_sums = [_add2(a, b, f"grads_pair_sum_{i}") for i, (a, b) in enumerate(zip(own, got))]
    chip_parts, rep_parts = _chip_exchange(pair_sums, _add2(rep_packed, rep_got, "grads_pair_sum_rep"))
    halves = [_sum_leading(t, f"grads_chip_sum_{i}") for i, t in enumerate(chip_parts)]
    rep_total = _sum_leading(rep_parts, "grads_chip_sum_rep")
    groups, o = [], 0
    for n in BIG:
        groups.append(halves[o:o + len(big[n])])
        o += len(big[n])
    reduced = _sibling_join(groups + [halves[o:]])

    res = {}
    for n, gsum in zip(BIG, reduced):
        res[n] = [gsum] + _adamw_nd(w[n], mom[n], var[n], gsum, "adamw_" + n)
    for names, gsum, tag in ((SMALL_SHARDED, reduced[-1][0], "adamw_small_sharded"), (REPLICATED, rep_total, "adamw_replicated")):
        packs = [gsum] + list(_adamw(_pack([w[n] for n in names]), _pack([mom[n] for n in names]), _pack([var[n] for n in names]), gsum, tag))
        per_kind = [_unpack(pk, [w[n].shape for n in names]) for pk in packs]
        for i, n in enumerate(names):
            res[n] = [per_kind[k][i] for k in range(4)]
    return (loss, grad_x[None], *[res[n][0] for n in WEIGHTS], *[res[n][1] for n in WEIGHTS],
            *[res[n][2] for n in WEIGHTS], *[res[n][3] for n in WEIGHTS])
```

```python
import functools
import math

import jax
import jax.numpy as jnp
from jax import lax
from jax.experimental import pallas as pl
from jax.experimental.pallas import tpu as pltpu

f32 = jnp.float32
bf16 = jnp.bfloat16
HI = lax.Precision.HIGHEST

LANES = 128
SUBLANES = 8
VMEM_LIMIT_BYTES = 56 * 1024 * 1024

HEADDIM = 64
STATE = 128
CHUNK = 128
CONV_K = 4
RMS_EPS = 1e-6
LN_EPS = 1e-5
ADAM_LR = 0.001
ADAM_B1 = 0.9
ADAM_B2 = 0.999
ADAM_EPS = 1e-08
ADAM_WD = 0.01
ADAM_STEP = 10

N_CHIPS = 4
N_DEV = 8
MESH = pl.DeviceIdType.MESH


def _cparams(sem):
    return pltpu.CompilerParams(dimension_semantics=sem, vmem_limit_bytes=VMEM_LIMIT_BYTES)


def _tile(n, want):
    if n <= want:
        return n
    t = want
    while n % t:
        t //= 2
    return t


def _row_spec(tm, c):
    return pl.BlockSpec((tm, c), lambda i: (i, 0))


def _full_spec(shape):
    nd = len(shape)
    return pl.BlockSpec(tuple(shape), lambda *_: (0,) * nd)


def _sigmoid(x):
    return 1.0 / (1.0 + jnp.exp(-x))


def _silu(x):
    return x * _sigmoid(x)


def _dsilu(x):
    s = _sigmoid(x)
    return s * (1.0 + x * (1.0 - s))


def _gelu(x):
    return 0.5 * x * (1.0 + lax.erf(x * (1.0 / math.sqrt(2.0))))


def _dgelu(x):
    return 0.5 * (1.0 + lax.erf(x * (1.0 / math.sqrt(2.0)))) + x * jnp.exp(-0.5 * x * x) * (1.0 / math.sqrt(2.0 * math.pi))


def _softplus(x):
    return jnp.maximum(x, 0.0) + jnp.log(1.0 + jnp.exp(-jnp.abs(x)))


def _rms(x, w, eps):
    r = lax.rsqrt(jnp.mean(x * x, axis=-1, keepdims=True) + eps)
    return x * r * w


def _rms_bwd(dy, x, w, eps):
    r = lax.rsqrt(jnp.mean(x * x, axis=-1, keepdims=True) + eps)
    xh = x * r
    g = dy * w
    dx = r * (g - xh * jnp.mean(g * xh, axis=-1, keepdims=True))
    dw = jnp.sum(dy * xh, axis=0, keepdims=True)
    return dx, dw


def _dot(a, b, dims=(((1,), (0,)), ((), ())), precision=None):
    return lax.dot_general(a, b, dims, precision=precision, preferred_element_type=f32)


NN = (((1,), (0,)), ((), ()))
NT = (((1,), (1,)), ((), ()))
TN = (((0,), (0,)), ((), ()))


def _mm(a, b, *, mode="nn", out_dtype=f32, res=None, kbatch=False, tm=1024, tn=512, tk=2048, name):
    a3, b3 = a.ndim == 3, b.ndim == 3
    nb = a.shape[0] if a3 else (b.shape[0] if b3 else 1)
    ash, bsh = a.shape[-2:], b.shape[-2:]
    if mode == "nn":
        M, K, N = ash[0], ash[1], bsh[1]
    elif mode == "nt":
        M, K, N = ash[0], ash[1], bsh[0]
    else:
        K, M, N = ash[0], ash[1], bsh[1]
    tm, tn, tk = _tile(M, tm), (N if N % LANES else _tile(N, tn)), (K if K % LANES else _tile(K, tk))
    nk = K // tk
    if kbatch:
        assert a3 and b3
        grid = (1, M // tm, N // tn, nb * nk)
        bi = lambda g, k: k // nk
        ki = lambda g, k: k % nk
    else:
        grid = (nb, M // tm, N // tn, nk)
        bi = lambda g, k: g
        ki = lambda g, k: k
    nsteps = grid[3]

    def spec(is3, blk, imap):
        if is3:
            return pl.BlockSpec((None,) + blk, lambda g, i, j, k: (bi(g, k),) + imap(i, j, ki(g, k)))
        return pl.BlockSpec(blk, lambda g, i, j, k: imap(i, j, ki(g, k)))

    if mode == "nn":
        a_spec = spec(a3, (tm, tk), lambda i, j, k: (i, k))
        b_spec = spec(b3, (tk, tn), lambda i, j, k: (k, j))
        dims = NN
    elif mode == "nt":
        a_spec = spec(a3, (tm, tk), lambda i, j, k: (i, k))
        b_spec = spec(b3, (tn, tk), lambda i, j, k: (j, k))
        dims = NT
    else:
        a_spec = spec(a3, (tk, tm), lambda i, j, k: (k, i))
        b_spec = spec(b3, (tk, tn), lambda i, j, k: (k, j))
        dims = TN
    out3 = (a3 or b3) and not kbatch
    if out3:
        o_spec = pl.BlockSpec((None, tm, tn), lambda g, i, j, k: (g, i, j))
        o_shape = jax.ShapeDtypeStruct((nb, M, N), out_dtype)
    else:
        o_spec = pl.BlockSpec((tm, tn), lambda g, i, j, k: (i, j))
        o_shape = jax.ShapeDtypeStruct((M, N), out_dtype)
    in_specs = [a_spec, b_spec]
    args = [a, b]
    if res is not None:
        in_specs.append(pl.BlockSpec((tm, tn), lambda g, i, j, k: (i, j)))
        args.append(res)

    def body(*refs):
        if res is not None:
            a_ref, b_ref, r_ref, o_ref, acc_ref = refs
        else:
            a_ref, b_ref, o_ref, acc_ref = refs
        k = pl.program_id(3)

        @pl.when(k == 0)
        def _():
            acc_ref[...] = jnp.zeros_like(acc_ref)

        acc_ref[...] += _dot(a_ref[...].astype(bf16), b_ref[...].astype(bf16), dims)

        @pl.when(k == nsteps - 1)
        def _():
            r = acc_ref[...]
            if res is not None:
                r = r + r_ref[...]
            o_ref[...] = r.astype(o_ref.dtype)

    return pl.pallas_call(
        body, name=name, grid=grid, in_specs=in_specs, out_specs=o_spec, out_shape=o_shape,
        scratch_shapes=[pltpu.VMEM((tm, tn), f32)],
        compiler_params=_cparams(("parallel", "parallel", "parallel", "arbitrary")),
    )(*args)


def _rowcall(fn, *, name, rows, fulls, out_rows, out_accs=(), tm=512):
    S = rows[0].shape[0]
    tm = _tile(S, tm)
    n_r, n_f, n_or, n_oa = len(rows), len(fulls), len(out_rows), len(out_accs)

    def body(*refs):
        ins = [r[...] for r in refs[:n_r + n_f]]
        outs = fn(*ins)
        if not isinstance(outs, (tuple, list)):
            outs = (outs,)
        o_refs = refs[n_r + n_f:]
        for o_ref, v in zip(o_refs[:n_or], outs[:n_or]):
            o_ref[...] = v.astype(o_ref.dtype)
        if n_oa:
            first = pl.program_id(0) == 0

            @pl.when(first)
            def _():
                for o_ref, v in zip(o_refs[n_or:], outs[n_or:]):
                    o_ref[...] = v

            @pl.when(jnp.logical_not(first))
            def _():
                for o_ref, v in zip(o_refs[n_or:], outs[n_or:]):
                    o_ref[...] += v

    in_specs = [_row_spec(tm, r.shape[1]) for r in rows] + [_full_spec(f.shape) for f in fulls]
    out_specs = [_row_spec(tm, c) for c, _ in out_rows] + [_full_spec(s) for s in out_accs]
    out_shape = [jax.ShapeDtypeStruct((S, c), d) for c, d in out_rows] + [jax.ShapeDtypeStruct(s, f32) for s in out_accs]
    res = pl.pallas_call(
        body, name=name, grid=(S // tm,), in_specs=in_specs, out_specs=out_specs, out_shape=out_shape,
        compiler_params=_cparams(("arbitrary",) if n_oa else ("parallel",)),
    )(*rows, *fulls)
    return res


def _row2(v):
    return v.reshape(1, -1)


def _rms_fwd(h, w, name):
    D = h.shape[1]
    return _rowcall(lambda x, w_: _rms(x, w_, RMS_EPS), name=name, rows=[h], fulls=[_row2(w)], out_rows=[(D, bf16)])[0]


def _conv_fwd(xpre, w, b, name):
    S, C = xpre.shape
    tm, tc = _tile(S, 512), _tile(C, 1024)
    hb = tm // SUBLANES

    def body(x_ref, halo_ref, w_ref, b_ref, c_ref, o_ref):
        i = pl.program_id(1)
        x = x_ref[...]
        halo = jnp.where(i > 0, halo_ref[...], 0.0)
        row = lax.broadcasted_iota(jnp.int32, x.shape, 0)
        row8 = lax.broadcasted_iota(jnp.int32, halo.shape, 0)
        x0 = x[0:SUBLANES, :]
        acc = x * w_ref[CONV_K - 1:CONV_K, :] + b_ref[...]
        acc0 = x0 * w_ref[CONV_K - 1:CONV_K, :] + b_ref[...]
        for k in range(1, CONV_K):
            wk = w_ref[CONV_K - 1 - k:CONV_K - k, :]
            acc = acc + pltpu.roll(x, k, axis=0) * wk
            acc0 = acc0 + jnp.where(row8 < k, pltpu.roll(halo, k, axis=0), pltpu.roll(x0, k, axis=0)) * wk
        c_ref[...] = acc
        o_ref[...] = _silu(acc)
        c_ref[0:SUBLANES, :] = acc0
        o_ref[0:SUBLANES, :] = _silu(acc0)

    return pl.pallas_call(
        body, name=name, grid=(C // tc, S // tm),
        in_specs=[pl.BlockSpec((tm, tc), lambda j, i: (i, j)),
                  pl.BlockSpec((SUBLANES, tc), lambda j, i: (jnp.maximum(i * hb - 1, 0), j)),
                  pl.BlockSpec((CONV_K, tc), lambda j, i: (0, j)),
                  pl.BlockSpec((1, tc), lambda j, i: (0, j))],
        out_specs=[pl.BlockSpec((tm, tc), lambda j, i: (i, j))] * 2,
        out_shape=[jax.ShapeDtypeStruct((S, C), f32)] * 2,
        compiler_params=_cparams(("parallel", "parallel")),
    )(xpre, xpre, w, _row2(b))


def _conv_bwd_dc(dxbc, c, xpre, name):
    S, C = xpre.shape
    tm, tc = _tile(S, 512), _tile(C, 1024)
    hb = tm // SUBLANES

    def body(d_ref, c_ref, x_ref, halo_ref, dc_ref, dw_ref, db_ref):
        i = pl.program_id(1)
        x = x_ref[...]
        dc = d_ref[...] * _dsilu(c_ref[...])
        dc_ref[...] = dc
        halo = jnp.where(i > 0, halo_ref[...], 0.0)
        row = lax.broadcasted_iota(jnp.int32, x.shape, 0)
        row8 = lax.broadcasted_iota(jnp.int32, halo.shape, 0)
        x0 = x[0:SUBLANES, :]
        dc0 = dc[0:SUBLANES, :]
        parts = [jnp.sum(dc * x, axis=0, keepdims=True)]
        for k in range(1, CONV_K):
            xs_big = jnp.where(row < SUBLANES, 0.0, pltpu.roll(x, k, axis=0))
            xs0 = jnp.where(row8 < k, pltpu.roll(halo, k, axis=0), pltpu.roll(x0, k, axis=0))
            parts.append(jnp.sum(dc * xs_big, axis=0, keepdims=True) + jnp.sum(dc0 * xs0, axis=0, keepdims=True))
        dw = jnp.concatenate([parts[CONV_K - 1 - k] for k in range(CONV_K)] + [jnp.zeros((SUBLANES - CONV_K, x.shape[1]), f32)], axis=0)
        db = jnp.sum(dc, axis=0, keepdims=True)

        @pl.when(i == 0)
        def _():
            dw_ref[...] = dw
            db_ref[...] = db

        @pl.when(i > 0)
        def _():
            dw_ref[...] += dw
            db_ref[...] += db

    return pl.pallas_call(
        body, name=name, grid=(C // tc, S // tm),
        in_specs=[pl.BlockSpec((tm, tc), lambda j, i: (i, j))] * 3 +
                 [pl.BlockSpec((SUBLANES, tc), lambda j, i: (jnp.maximum(i * hb - 1, 0), j))],
        out_specs=[pl.BlockSpec((tm, tc), lambda j, i: (i, j)),
                   pl.BlockSpec((SUBLANES, tc), lambda j, i: (0, j)),
                   pl.BlockSpec((1, tc), lambda j, i: (0, j))],
        out_shape=[jax.ShapeDtypeStruct((S, C), f32), jax.ShapeDtypeStruct((SUBLANES, C), f32), jax.ShapeDtypeStruct((1, C), f32)],
        compiler_params=_cparams(("parallel", "arbitrary")),
    )(dxbc, c, xpre, xpre)


def _conv_bwd_dx(dc, w, name):
    S, C = dc.shape
    tm, tc = _tile(S, 512), _tile(C, 1024)
    hb = tm // SUBLANES
    nrow = S // tm
    last8 = S // SUBLANES - 1

    def body(d_ref, nxt_ref, w_ref, o_ref):
        i = pl.program_id(1)
        d = d_ref[...]
        nxt = jnp.where(i < nrow - 1, nxt_ref[...], 0.0)
        row8 = lax.broadcasted_iota(jnp.int32, nxt.shape, 0)
        dl = d[tm - SUBLANES:tm, :]
        acc = d * w_ref[CONV_K - 1:CONV_K, :]
        accl = dl * w_ref[CONV_K - 1:CONV_K, :]
        for j in range(1, CONV_K):
            wk = w_ref[CONV_K - 1 - j:CONV_K - j, :]
            acc = acc + pltpu.roll(d, tm - j, axis=0) * wk
            accl = accl + jnp.where(row8 >= SUBLANES - j, pltpu.roll(nxt, SUBLANES - j, axis=0), pltpu.roll(dl, SUBLANES - j, axis=0)) * wk
        o_ref[...] = acc.astype(o_ref.dtype)
        o_ref[tm - SUBLANES:tm, :] = accl.astype(o_ref.dtype)

    return pl.pallas_call(
        body, name=name, grid=(C // tc, nrow),
        in_specs=[pl.BlockSpec((tm, tc), lambda j, i: (i, j)),
                  pl.BlockSpec((SUBLANES, tc), lambda j, i: (jnp.minimum((i + 1) * hb, last8), j)),
                  pl.BlockSpec((CONV_K, tc), lambda j, i: (0, j))],
        out_specs=pl.BlockSpec((tm, tc), lambda j, i: (i, j)),
        out_shape=jax.ShapeDtypeStruct((S, C), f32),
        compiler_params=_cparams(("parallel", "parallel")),
    )(dc, dc, w)


def _halfsum(v, lane_lo):
    s0 = jnp.sum(jnp.where(lane_lo, v, 0.0), axis=1, keepdims=True)
    s1 = jnp.sum(jnp.where(lane_lo, 0.0, v), axis=1, keepdims=True)
    return jnp.where(lane_lo, s0, s1)


def _ssd_specs(S, inner, GN, nchunks, rev):
    L = CHUNK
    cm = (lambda c: nchunks - 1 - c) if rev else (lambda c: c)
    xs = pl.BlockSpec((L, inner), lambda c: (cm(c), 0))
    bb = pl.BlockSpec((L, GN), lambda c: (cm(c), inner // GN))
    cc = pl.BlockSpec((L, GN), lambda c: (cm(c), inner // GN + 1))
    row = pl.BlockSpec((L, inner), lambda c: (cm(c), 0))
    vec = pl.BlockSpec((1, inner), lambda c: (0, 0))
    st = pl.BlockSpec((None, inner, STATE), lambda c: (cm(c), 0, 0))
    return xs, bb, cc, row, vec, st


def _ssd_fwd(xbc, dtx, ax, dx, G, name):
    S, inner = dtx.shape
    GN = G * STATE
    L = CHUNK
    nchunks = S // L
    npairs = inner // LANES
    ppg = npairs // G
    assert inner % GN == 0 and L == LANES and STATE == LANES

    def body(xs_ref, b_ref, c_ref, dtx_ref, ax_ref, dx_ref, y_ref, so_ref, st_ref):
        ci = pl.program_id(0)

        @pl.when(ci == 0)
        def _():
            st_ref[...] = jnp.zeros_like(st_ref)

        r = lax.broadcasted_iota(jnp.int32, (L, L), 0)
        cidx = lax.broadcasted_iota(jnp.int32, (L, L), 1)
        tril = cidx <= r
        lane_lo = cidx < HEADDIM
        xs = xs_ref[...]
        dtv = dtx_ref[...]
        X = xs * dtv
        da = dtv * ax_ref[...]
        cs = _dot(tril.astype(f32), da, NN, HI)
        cs_last = jnp.sum(da, axis=0, keepdims=True)
        so_ref[...] = st_ref[...]
        for g in range(G):
            Bg = b_ref[:, g * STATE:(g + 1) * STATE].astype(bf16)
            Cg = c_ref[:, g * STATE:(g + 1) * STATE].astype(bf16)
            CB = _dot(Cg, Bg, NT)
            for j in range(ppg):
                lo = (g * ppg + j) * LANES
                tile = cs[:, lo:lo + LANES]
                rl = pltpu.roll(tile, HEADDIM, axis=1)
                Xp = X[:, lo:lo + LANES]
                prev = st_ref[lo:lo + LANES, :]
                ypair = _dot(Cg, prev.astype(bf16), NT) * jnp.exp(tile)
                for half in (0, 1):
                    hm = lane_lo if half == 0 else jnp.logical_not(lane_lo)
                    colb = jnp.where(hm, tile, rl)
                    Lm = jnp.exp(jnp.where(tril, colb - colb.T, -1e30))
                    W = (CB * Lm).astype(bf16)
                    ypair = ypair + _dot(W, jnp.where(hm, Xp, 0.0).astype(bf16), NN)
                y_ref[:, lo:lo + LANES] = ypair + xs[:, lo:lo + LANES] * dx_ref[:, lo:lo + LANES]
                last = cs_last[:, lo:lo + LANES]
                snew = _dot((Xp * jnp.exp(last - tile)).astype(bf16), Bg, TN)
                dec_rows = jnp.broadcast_to(jnp.exp(last), (L, LANES)).T
                st_ref[lo:lo + LANES, :] = dec_rows * prev + snew

    xs_s, b_s, c_s, row_s, vec_s, st_s = _ssd_specs(S, inner, GN, nchunks, False)
    return pl.pallas_call(
        body, name=name, grid=(nchunks,),
        in_specs=[xs_s, b_s, c_s, row_s, vec_s, vec_s],
        out_specs=[row_s, st_s],
        out_shape=[jax.ShapeDtypeStruct((S, inner), f32), jax.ShapeDtypeStruct((nchunks, inner, STATE), f32)],
        scratch_shapes=[pltpu.VMEM((inner, STATE), f32)],
        compiler_params=_cparams(("arbitrary",)),
    )(xbc, xbc, xbc, dtx, ax, dx)


def _ssd_bwd(dy, xbc, dtx, ax, dx, states, et, G, name):
    S, inner = dtx.shape
    H = et.shape[1]
    GN = G * STATE
    Cc = inner + 2 * GN
    L = CHUNK
    nchunks = S // L
    npairs = inner // LANES
    ppg = npairs // G

    def body(dy_ref, xs_ref, b_ref, c_ref, dtx_ref, ax_ref, dx_ref, si_ref, et_ref,
             dxbc_ref, ddt_ref, dax_ref, ddx_ref, dst_ref, dA_ref, dAl_ref, ddtp_ref):
        ci = pl.program_id(0)

        @pl.when(ci == 0)
        def _():
            dst_ref[...] = jnp.zeros_like(dst_ref)
            dax_ref[...] = jnp.zeros_like(dax_ref)
            ddx_ref[...] = jnp.zeros_like(ddx_ref)

        r = lax.broadcasted_iota(jnp.int32, (L, L), 0)
        cidx = lax.broadcasted_iota(jnp.int32, (L, L), 1)
        tril = cidx <= r
        lane_lo = cidx < HEADDIM
        lane_lo1 = lax.broadcasted_iota(jnp.int32, (1, LANES), 1) < HEADDIM
        xs = xs_ref[...]
        dtv = dtx_ref[...]
        dyv = dy_ref[...]
        X = xs * dtv
        da = dtv * ax_ref[...]
        cs = _dot(tril.astype(f32), da, NN, HI)
        cs_last = jnp.sum(da, axis=0, keepdims=True)
        ones8 = jnp.ones((SUBLANES, STATE), f32)
        for g in range(G):
            Bg = b_ref[:, g * STATE:(g + 1) * STATE].astype(bf16)
            Cg = c_ref[:, g * STATE:(g + 1) * STATE].astype(bf16)
            CB = _dot(Cg, Bg, NT)
            dCB = jnp.zeros((L, L), f32)
            dBg = jnp.zeros((L, STATE), f32)
            dCg = jnp.zeros((L, STATE), f32)
            for j in range(ppg):
                lo = (g * ppg + j) * LANES
                tile = cs[:, lo:lo + LANES]
                rl = pltpu.roll(tile, HEADDIM, axis=1)
                eA = jnp.exp(tile)
                Xp = X[:, lo:lo + LANES]
                dYp = dyv[:, lo:lo + LANES]
                xsp = xs[:, lo:lo + LANES]
                prev = si_ref[lo:lo + LANES, :]
                dSn = dst_ref[lo:lo + LANES, :]
                prev_b = prev.astype(bf16)
                dSn_b = dSn.astype(bf16)
                yoff = _dot(Cg, prev_b, NT) * eA
                dA_t = _halfsum(dYp * yoff, lane_lo)
                dYe = (dYp * eA).astype(bf16)
                dCg = dCg + _dot(dYe, prev_b, NN)
                dprev = _dot(dYe, Cg, TN)
                last = cs_last[:, lo:lo + LANES]
                w = jnp.exp(last - tile)
                BdS = _dot(Bg, dSn_b, NT)
                Xw = Xp * w
                dXp = w * BdS
                Gt = _halfsum(Xw * BdS, lane_lo)
                dA_t = dA_t - Gt
                dAl_t = jnp.sum(Gt, axis=0, keepdims=True)
                dBg = dBg + _dot(Xw.astype(bf16), dSn_b, NN)
                dec_rows = jnp.broadcast_to(jnp.exp(last), (L, LANES)).T
                dprev = dprev + dec_rows * dSn
                v = _dot(ones8, dSn * prev * dec_rows, NT, HI)
                dAl_t = dAl_t + _halfsum(jnp.sum(v, axis=0, keepdims=True) * (1.0 / SUBLANES), lane_lo1)
                for half in (0, 1):
                    hm = lane_lo if half == 0 else jnp.logical_not(lane_lo)
                    colb = jnp.where(hm, tile, rl)
                    Lm = jnp.exp(jnp.where(tril, colb - colb.T, -1e30))
                    W = CB * Lm
                    Xh = jnp.where(hm, Xp, 0.0).astype(bf16)
                    dYh = jnp.where(hm, dYp, 0.0).astype(bf16)
                    dW = _dot(dYh, Xh, NT)
                    dXp = dXp + _dot(W.astype(bf16), dYh, TN)
                    E = dW * W
                    dCB = dCB + dW * Lm
                    rs = jnp.sum(E, axis=1, keepdims=True)
                    ccol = jnp.broadcast_to(jnp.sum(E, axis=0, keepdims=True), (L, L)).T
                    dA_t = dA_t + jnp.where(hm, rs - ccol, 0.0)
                dxbc_ref[:, lo:lo + LANES] = dXp * dtv[:, lo:lo + LANES] + dYp * dx_ref[:, lo:lo + LANES]
                ddtp_ref[:, lo:lo + LANES] = dXp * xsp
                ddx_ref[:, lo:lo + LANES] += jnp.sum(dYp * xsp, axis=0, keepdims=True)
                dA_ref[:, lo:lo + LANES] = dA_t
                dAl_ref[:, lo:lo + LANES] = dAl_t
                dst_ref[lo:lo + LANES, :] = dprev
            dCBb = dCB.astype(bf16)
            dxbc_ref[:, inner + g * STATE:inner + (g + 1) * STATE] = dBg + _dot(dCBb, Cg, TN)
            dxbc_ref[:, inner + GN + g * STATE:inner + GN + (g + 1) * STATE] = dCg + _dot(dCBb, Bg, NN)
        triu = (cidx >= r).astype(f32)
        dda = _dot(triu, dA_ref[...], NN, HI) + dAl_ref[...]
        ddt_full = ddtp_ref[...] + dda * ax_ref[...] * (1.0 / HEADDIM)
        ddt_ref[...] = _dot(ddt_full, et_ref[...], NN, HI)
        dax_ref[...] += jnp.sum(dda * dtv, axis=0, keepdims=True)

    xs_s, b_s, c_s, row_s, vec_s, st_s = _ssd_specs(S, inner, GN, nchunks, True)
    return pl.pallas_call(
        body, name=name, grid=(nchunks,),
        in_specs=[row_s, xs_s, b_s, c_s, row_s, vec_s, vec_s, st_s, _full_spec(et.shape)],
        out_specs=[pl.BlockSpec((L, Cc), lambda c: (nchunks - 1 - c, 0)),
                   pl.BlockSpec((L, H), lambda c: (nchunks - 1 - c, 0)), vec_s, vec_s],
        out_shape=[jax.ShapeDtypeStruct((S, Cc), f32), jax.ShapeDtypeStruct((S, H), f32),
                   jax.ShapeDtypeStruct((1, inner), f32), jax.ShapeDtypeStruct((1, inner), f32)],
        scratch_shapes=[pltpu.VMEM((inner, STATE), f32), pltpu.VMEM((L, inner), f32),
                        pltpu.VMEM((1, inner), f32), pltpu.VMEM((L, inner), f32)],
        compiler_params=_cparams(("arbitrary",)),
    )(dy, xbc, xbc, xbc, dtx, ax, dx, states, et)


def _dt_fwd(dt_pre, bias, e, name):
    H, inner = e.shape

    def fn(dp, b, e_):
        dt = _softplus(dp + b)
        return dt, _dot(dt, e_, NN, HI)

    return _rowcall(fn, name=name, rows=[dt_pre], fulls=[_row2(bias), e], out_rows=[(H, f32), (inner, f32)])


def _dt_bwd(ddt, dt_pre, bias, name):
    H = ddt.shape[1]

    def fn(dd, dp, b):
        g = dd * _sigmoid(dp + b)
        return g, jnp.sum(g, axis=0, keepdims=True)

    return _rowcall(fn, name=name, rows=[ddt, dt_pre], fulls=[_row2(bias)], out_rows=[(H, f32)], out_accs=[(1, H)])


def _gnorm_fwd(y, z, w, G, name):
    inner = y.shape[1]
    gs = inner // G

    def fn(y_, z_, w_):
        gg = y_ * _silu(z_)
        outs = []
        for g in range(G):
            sl = slice(g * gs, (g + 1) * gs)
            outs.append(_rms(gg[:, sl], w_[:, sl], LN_EPS))
        return jnp.concatenate(outs, axis=1)

    return _rowcall(fn, name=name, rows=[y, z], fulls=[_row2(w)], out_rows=[(inner, bf16)], tm=256)[0]


def _gnorm_bwd(dyn, y, z, w, G, name):
    inner = y.shape[1]
    gs = inner // G

    def fn(d_, y_, z_, w_):
        sz = _silu(z_)
        gg = y_ * sz
        dgs, dws = [], []
        for g in range(G):
            sl = slice(g * gs, (g + 1) * gs)
            dg, dw = _rms_bwd(d_[:, sl], gg[:, sl], w_[:, sl], LN_EPS)
            dgs.append(dg)
            dws.append(dw)
        dgg = jnp.concatenate(dgs, axis=1)
        return dgg * sz, dgg * y_ * _dsilu(z_), jnp.concatenate(dws, axis=1)

    return _rowcall(fn, name=name, rows=[dyn, y, z], fulls=[_row2(w)], out_rows=[(inner, f32), (inner, f32)],
                    out_accs=[(1, inner)], tm=256)


def _gmlp_parts(pre, lw, lb, I):
    hp = _gelu(pre)
    uu = hp[:, :I]
    vp = hp[:, I:]
    xc = vp - jnp.mean(vp, axis=-1, keepdims=True)
    rstd = lax.rsqrt(jnp.mean(xc * xc, axis=-1, keepdims=True) + LN_EPS)
    vhat = xc * rstd
    return uu, vhat, rstd, vhat * lw + lb


def _gmlp_mid_fwd(pre, b_in, ln_w, ln_b, w_s, bsx, name):
    S, two_i = pre.shape
    I = two_i // 2
    NG = w_s.shape[0]
    gd = I // NG
    L = CHUNK

    def body(pre_ref, bi_ref, lw_ref, lb_ref, ws_ref, bsx_ref, o_ref):
        uu, _, _, vv = _gmlp_parts(pre_ref[...] + bi_ref[...], lw_ref[...], lb_ref[...], I)
        r = lax.broadcasted_iota(jnp.int32, (L, L), 0)
        cidx = lax.broadcasted_iota(jnp.int32, (L, L), 1)
        tril = cidx <= r
        for g in range(NG):
            sl = slice(g * gd, (g + 1) * gd)
            wg = jnp.where(tril, ws_ref[g], 0.0).astype(bf16)
            mixed = _dot(wg, vv[:, sl].astype(bf16), NN) + bsx_ref[:, sl]
            o_ref[:, sl] = (uu[:, sl] * mixed).astype(o_ref.dtype)

    return pl.pallas_call(
        body, name=name, grid=(S // L,),
        in_specs=[_row_spec(L, two_i), _full_spec((1, two_i)), _full_spec((1, I)), _full_spec((1, I)), _full_spec(w_s.shape), _full_spec(bsx.shape)],
        out_specs=_row_spec(L, I), out_shape=jax.ShapeDtypeStruct((S, I), bf16),
        compiler_params=_cparams(("parallel",)),
    )(pre, _row2(b_in), _row2(ln_w), _row2(ln_b), w_s, bsx)


def _gmlp_mid_bwd(do, pre, b_in, ln_w, ln_b, w_s, bsx, name):
    S, two_i = pre.shape
    I = two_i // 2
    NG = w_s.shape[0]
    gd = I // NG
    L = CHUNK

    def body(do_ref, pre_ref, bi_ref, lw_ref, lb_ref, ws_ref, bsx_ref, dpre_ref, dbi_ref, dlw_ref, dlb_ref, dws_ref, dbs_ref, dvv_ref):
        ci = pl.program_id(0)

        @pl.when(ci == 0)
        def _():
            for ref in (dbi_ref, dlw_ref, dlb_ref, dws_ref, dbs_ref):
                ref[...] = jnp.zeros_like(ref)

        pre = pre_ref[...] + bi_ref[...]
        lw = lw_ref[...]
        uu, vhat, rstd, vv = _gmlp_parts(pre, lw, lb_ref[...], I)
        dov = do_ref[...]
        r = lax.broadcasted_iota(jnp.int32, (L, L), 0)
        cidx = lax.broadcasted_iota(jnp.int32, (L, L), 1)
        tril = cidx <= r
        duus = []
        for g in range(NG):
            sl = slice(g * gd, (g + 1) * gd)
            wg = jnp.where(tril, ws_ref[g], 0.0).astype(bf16)
            vg = vv[:, sl].astype(bf16)
            mixed = _dot(wg, vg, NN) + bsx_ref[:, sl]
            duus.append(dov[:, sl] * mixed)
            dmixed = dov[:, sl] * uu[:, sl]
            dbs_ref[:, sl] += dmixed
            dmb = dmixed.astype(bf16)
            dvv_ref[:, sl] = _dot(wg, dmb, TN)
            dws_ref[g] += jnp.where(tril, _dot(dmb, vg, NT), 0.0)
        duu = jnp.concatenate(duus, axis=1)
        dvv = dvv_ref[...]
        dlw_ref[...] += jnp.sum(dvv * vhat, axis=0, keepdims=True)
        dlb_ref[...] += jnp.sum(dvv, axis=0, keepdims=True)
        dvh = dvv * lw
        dvp = rstd * (dvh - jnp.mean(dvh, axis=-1, keepdims=True) - vhat * jnp.mean(dvh * vhat, axis=-1, keepdims=True))
        dpre = jnp.concatenate([duu, dvp], axis=1) * _dgelu(pre)
        dbi_ref[...] += jnp.sum(dpre, axis=0, keepdims=True)
        dpre_ref[...] = dpre.astype(dpre_ref.dtype)

    return pl.pallas_call(
        body, name=name, grid=(S // L,),
        in_specs=[_row_spec(L, I), _row_spec(L, two_i), _full_spec((1, two_i)), _full_spec((1, I)), _full_spec((1, I)),
                  _full_spec(w_s.shape), _full_spec(bsx.shape)],
        out_specs=[_row_spec(L, two_i), _full_spec((1, two_i)), _full_spec((1, I)), _full_spec((1, I)), _full_spec(w_s.shape), _full_spec((L, I))],
        out_shape=[jax.ShapeDtypeStruct((S, two_i), bf16), jax.ShapeDtypeStruct((1, two_i), f32), jax.ShapeDtypeStruct((1, I), f32),
                   jax.ShapeDtypeStruct((1, I), f32), jax.ShapeDtypeStruct(w_s.shape, f32), jax.ShapeDtypeStruct((L, I), f32)],
        scratch_shapes=[pltpu.VMEM((L, I), f32)],
        compiler_params=_cparams(("arbitrary",)),
    )(do, pre, _row2(b_in), _row2(ln_w), _row2(ln_b), w_s, bsx)


def _lane_group_sum(acc, eg, name):
    NG = eg.shape[1]
    return _rowcall(lambda a, e: _dot(a, e, NN, HI), name=name, rows=[acc], fulls=[eg], out_rows=[(NG, f32)])[0]


def _ffn_up(u, wg, wu, name):
    S, D = u.shape
    nb, _, F4 = wg.shape
    tm = _tile(S, 1024)

    def body(u_ref, wg_ref, wu_ref, g_ref, up_ref, a_ref):
        uv = u_ref[...]
        g = _dot(uv, wg_ref[...], NN)
        up = _dot(uv, wu_ref[...], NN)
        g_ref[...] = g
        up_ref[...] = up
        a_ref[...] = (_silu(g) * up).astype(a_ref.dtype)

    wspec = pl.BlockSpec((None, D, F4), lambda k, i: (k, 0, 0))
    ospec = pl.BlockSpec((None, tm, F4), lambda k, i: (k, i, 0))
    return pl.pallas_call(
        body, name=name, grid=(nb, S // tm),
        in_specs=[pl.BlockSpec((tm, D), lambda k, i: (i, 0)), wspec, wspec],
        out_specs=[ospec, ospec, ospec],
        out_shape=[jax.ShapeDtypeStruct((nb, S, F4), f32), jax.ShapeDtypeStruct((nb, S, F4), f32), jax.ShapeDtypeStruct((nb, S, F4), bf16)],
        compiler_params=_cparams(("parallel", "parallel")),
    )(u, wg, wu)


def _ffn_bwd_act(dh, wd, G, U, name):
    S, D = dh.shape
    nb, F4, _ = wd.shape
    tm = _tile(S, 1024)

    def body(dh_ref, wd_ref, g_ref, up_ref, dg_ref, du_ref):
        dA = _dot(dh_ref[...].astype(bf16), wd_ref[...], NT)
        g = g_ref[...]
        dg_ref[...] = (dA * up_ref[...] * _dsilu(g)).astype(dg_ref.dtype)
        du_ref[...] = (dA * _silu(g)).astype(du_ref.dtype)

    ospec = pl.BlockSpec((None, tm, F4), lambda k, i: (k, i, 0))
    return pl.pallas_call(
        body, name=name, grid=(nb, S // tm),
        in_specs=[pl.BlockSpec((tm, D), lambda k, i: (i, 0)), pl.BlockSpec((None, F4, D), lambda k, i: (k, 0, 0)), ospec, ospec],
        out_specs=[ospec, ospec],
        out_shape=[jax.ShapeDtypeStruct((nb, S, F4), bf16)] * 2,
        compiler_params=_cparams(("parallel", "parallel")),
    )(dh, wd, G, U)


def _rms_bwd_add(dres, du, h, w, name):
    D = h.shape[1]

    def fn(dr, du_, h_, w_):
        dx, dw = _rms_bwd(du_, h_, w_, RMS_EPS)
        return dr + dx, dw

    return _rowcall(fn, name=name, rows=[dres, du, h], fulls=[_row2(w)], out_rows=[(D, f32)], out_accs=[(1, D)])


def _ple_fwd(h, p_i, wp, pn, gn, wgate, name):
    D = h.shape[1]

    def fn(h_, p_, wp_, pn_, gn_, wg_):
        pe = _dot(p_.astype(bf16), wp_, NN)
        e = _rms(pe, pn_, RMS_EPS)
        q = _rms(h_, gn_, RMS_EPS)
        gate = _sigmoid(_dot(q.astype(bf16), wg_, NN))
        return h_ + gate * e, pe, gate

    return _rowcall(fn, name=name, rows=[h, p_i], fulls=[wp, _row2(pn), _row2(gn), wgate],
                    out_rows=[(D, f32), (D, f32), (D, f32)], tm=256)


def _ple_bwd(dh3, h, pe, gate, pn, gn, wgate, name):
    D = h.shape[1]

    def fn(d_, h_, pe_, gate_, pn_, gn_, wg_):
        e = _rms(pe_, pn_, RMS_EPS)
        dzg = d_ * e * gate_ * (1.0 - gate_)
        dq = _dot(dzg.astype(bf16), wg_, NT)
        dxq, dgn = _rms_bwd(dq, h_, gn_, RMS_EPS)
        dpe, dpn = _rms_bwd(d_ * gate_, pe_, pn_, RMS_EPS)
        return d_ + dxq, dzg, dpe, _rms(h_, gn_, RMS_EPS), dpn, dgn

    return _rowcall(fn, name=name, rows=[dh3, h, pe, gate], fulls=[_row2(pn), _row2(gn), wgate],
                    out_rows=[(D, f32), (D, bf16), (D, bf16), (D, bf16)], out_accs=[(1, D), (1, D)], tm=256)


def _loss_head(h, target, fn_w, name):
    D = h.shape[1]

    def fn(h_, t_, w_):
        diff = _rms(h_, w_, RMS_EPS) - t_
        loss = 0.5 * jnp.sum(jnp.mean(diff * diff, axis=-1, keepdims=True), axis=0, keepdims=True)
        dh, dw = _rms_bwd(diff * (1.0 / D), h_, w_, RMS_EPS)
        return dh, jnp.broadcast_to(loss, (1, LANES)), dw

    return _rowcall(fn, name=name, rows=[h, target], fulls=[_row2(fn_w)], out_rows=[(D, f32)], out_accs=[(1, LANES), (1, D)])


def _adamw(w, m, v, g, name):
    R, C = w.shape
    tr = R
    while tr * C > 256 * 1024 and tr % (2 * SUBLANES) == 0:
        tr //= 2

    def body(w_ref, m_ref, v_ref, g_ref, d_ref, mo_ref, vo_ref):
        g = g_ref[...]
        mn = ADAM_B1 * m_ref[...] + (1.0 - ADAM_B1) * g
        vn = ADAM_B2 * v_ref[...] + (1.0 - ADAM_B2) * (g * g)
        m_hat = mn / (1.0 - ADAM_B1 ** ADAM_STEP)
        v_hat = vn / (1.0 - ADAM_B2 ** ADAM_STEP)
        d_ref[...] = -ADAM_LR * (m_hat / (jnp.sqrt(v_hat) + ADAM_EPS) + ADAM_WD * w_ref[...])
        mo_ref[...] = mn
        vo_ref[...] = vn

    spec = pl.BlockSpec((tr, C), lambda i: (i, 0))
    return pl.pallas_call(
        body, name=name, grid=(R // tr,), in_specs=[spec] * 4,
        out_specs=[spec] * 3, out_shape=[jax.ShapeDtypeStruct((R, C), f32)] * 3,
        compiler_params=_cparams(("parallel",)),
    )(w, m, v, g)


def _expand_onehot(n, per):
    lane = lax.broadcasted_iota(jnp.int32, (n, n * per), 1)
    row = lax.broadcasted_iota(jnp.int32, (n, n * per), 0)
    return (lane // per == row).astype(f32)


def _ssd_layer_fwd(h, nm_w, W, t):
    H = W["dt_bias"].shape[0]
    inner = H * HEADDIM
    G = (W["conv_b"].shape[0] - inner) // (2 * STATE)
    hn = _rms_fwd(h, nm_w, f"rms_mix_{t}")
    z = _mm(hn, W["wz"], name=f"ssd_z_{t}")
    xpre = _mm(hn, W["wxbc"], name=f"ssd_xbc_{t}")
    dt_pre = _mm(hn, W["wdt"], name=f"ssd_dt_{t}")
    c, xbc = _conv_fwd(xpre, W["conv_w"], W["conv_b"], f"ssd_conv_{t}")
    _, dtx = _dt_fwd(dt_pre, W["dt_bias"], _expand_onehot(H, HEADDIM), f"ssd_dtx_{t}")
    a = -jnp.exp(W["a_log"])
    ax = _row2(jnp.repeat(a, HEADDIM))
    dx = _row2(jnp.repeat(W["d"], HEADDIM))
    y, states = _ssd_fwd(xbc, dtx, ax, dx, G, f"ssd_scan_{t}")
    yn = _gnorm_fwd(y, z, W["norm_w"], G, f"ssd_gnorm_{t}")
    h1 = _mm(yn, W["wout"], res=h, name=f"ssd_out_{t}")
    return h1, (h, hn, z, xpre, dt_pre, c, xbc, dtx, a, ax, dx, y, states, yn)


def _ssd_layer_bwd(dh1, saved, nm_w, W, t):
    h, hn, z, xpre, dt_pre, c, xbc, dtx, a, ax, dx, y, states, yn = saved
    H = W["dt_bias"].shape[0]
    inner = H * HEADDIM
    G = (W["conv_b"].shape[0] - inner) // (2 * STATE)
    dyn = _mm(dh1, W["wout"], mode="nt", name=f"ssd_out_dx_{t}")
    g_wout = _mm(yn, dh1, mode="tn", out_dtype=bf16, name=f"ssd_out_dw_{t}")
    dy, dz, g_normw = _gnorm_bwd(dyn, y, z, W["norm_w"], G, f"ssd_gnorm_bwd_{t}")
    dxbc, ddt, dax, ddx = _ssd_bwd(dy, xbc, dtx, ax, dx, states, _expand_onehot(H, HEADDIM).T, G, f"ssd_scan_bwd_{t}")
    dc, g_convw8, g_convb = _conv_bwd_dc(dxbc, c, xpre, f"ssd_conv_bwd_dc_{t}")
    dxpre = _conv_bwd_dx(dc, W["conv_w"], f"ssd_conv_bwd_dx_{t}")
    ddt_pre, g_dtb = _dt_bwd(ddt, dt_pre, W["dt_bias"], f"ssd_dt_bwd_{t}")
    g_wz = _mm(hn, dz, mode="tn", out_dtype=bf16, name=f"ssd_z_dw_{t}")
    g_wxbc = _mm(hn, dxpre, mode="tn", out_dtype=bf16, name=f"ssd_xbc_dw_{t}")
    g_wdt = _mm(hn, ddt_pre, mode="tn", out_dtype=bf16, name=f"ssd_dt_dw_{t}")
    dhn = _mm(dz, W["wz"], mode="nt", name=f"ssd_z_dx_{t}")
    dhn = _mm(dxpre, W["wxbc"], mode="nt", res=dhn, name=f"ssd_xbc_dx_{t}")
    dhn = _mm(ddt_pre, W["wdt"], mode="nt", res=dhn, name=f"ssd_dt_dx_{t}")
    dh, g_nm = _rms_bwd_add(dh1, dhn, h, nm_w, f"rms_mix_bwd_{t}")
    grads = dict(
        w_in=jnp.concatenate([g_wz, g_wxbc, g_wdt], axis=1), wout=g_wout,
        conv_w=g_convw8[:CONV_K], conv_b=g_convb[0], dt_bias=g_dtb[0],
        a_log=dax[0].reshape(H, HEADDIM)[:, 0] * a, d=jnp.sum(ddx[0].reshape(H, HEADDIM), axis=1),
        norm_w=g_normw[0], norm_mix=g_nm[0])
    return dh, grads


def _gmlp_layer_fwd(h, nm_w, W, t):
    NG, L, _ = W["w_s"].shape
    I = W["ln_w"].shape[0]
    hn = _rms_fwd(h, nm_w, f"rms_mix_{t}")
    pre = _mm(hn, W["win"], name=f"gmlp_in_{t}")
    bsx = jnp.repeat(W["b_s"].T, I // NG, axis=1)
    o = _gmlp_mid_fwd(pre, W["b_in"], W["ln_w"], W["ln_b"], W["w_s"], bsx, f"gmlp_mid_{t}")
    h1 = _mm(o, W["wout"], res=h, name=f"gmlp_out_{t}")
    return h1, (h, hn, pre, bsx, o)


def _gmlp_layer_bwd(dh1, saved, nm_w, W, t):
    h, hn, pre, bsx, o = saved
    NG = W["w_s"].shape[0]
    I = W["ln_w"].shape[0]
    do = _mm(dh1, W["wout"], mode="nt", name=f"gmlp_out_dx_{t}")
    g_wout = _mm(o, dh1, mode="tn", out_dtype=bf16, name=f"gmlp_out_dw_{t}")
    dpre, g_bin, g_lnw, g_lnb, g_ws, dbs = _gmlp_mid_bwd(do, pre, W["b_in"], W["ln_w"], W["ln_b"], W["w_s"], bsx, f"gmlp_mid_bwd_{t}")
    g_bs = _lane_group_sum(dbs, _expand_onehot(NG, I // NG).T, f"gmlp_bs_{t}").T
    g_win = _mm(hn, dpre, mode="tn", out_dtype=bf16, name=f"gmlp_in_dw_{t}")
    dhn = _mm(dpre, W["win"], mode="nt", name=f"gmlp_in_dx_{t}")
    dh, g_nm = _rms_bwd_add(dh1, dhn, h, nm_w, f"rms_mix_bwd_{t}")
    grads = dict(win=g_win, wout=g_wout, b_in=g_bin[0], ln_w=g_lnw[0], ln_b=g_lnb[0], w_s=g_ws, b_s=g_bs, norm_mix=g_nm[0])
    return dh, grads


def _ffn_fwd(h1, nf_w, W, t):
    u = _rms_fwd(h1, nf_w, f"rms_ffn_{t}")
    Gm, Um, A = _ffn_up(u, W["wg"], W["wu"], f"ffn_up_{t}")
    h2 = _mm(A, W["wd"], kbatch=True, res=h1, name=f"ffn_down_{t}")
    return h2, (h1, u, Gm, Um, A)


def _ffn_bwd(dh2, saved, nf_w, W, t):
    h1, u, Gm, Um, A = saved
    dG, dU = _ffn_bwd_act(dh2, W["wd"], Gm, Um, f"ffn_act_bwd_{t}")
    g_wd = _mm(A, dh2, mode="tn", out_dtype=bf16, name=f"ffn_down_dw_{t}")
    g_wg = _mm(u, dG, mode="tn", out_dtype=bf16, name=f"ffn_gate_dw_{t}")
    g_wu = _mm(u, dU, mode="tn", out_dtype=bf16, name=f"ffn_up_dw_{t}")
    du = _mm(dG, W["wg"], mode="nt", kbatch=True, name=f"ffn_gate_dx_{t}")
    du = _mm(dU, W["wu"], mode="nt", kbatch=True, res=du, name=f"ffn_up_dx_{t}")
    dh1, g_nf = _rms_bwd_add(dh2, du, h1, nf_w, f"rms_ffn_bwd_{t}")
    return dh1, dict(wg=g_wg, wu=g_wu, wd=g_wd, norm_ffn=g_nf[0])


def _local_step(x, p, target, W):
    depth = p.shape[0]
    h = x
    saved = []
    for i in range(depth):
        j = i // 2
        if i % 2 == 0:
            h1, s_mix = _ssd_layer_fwd(h, W["norm_mix"][i], W["ssd"][j], i)
        else:
            h1, s_mix = _gmlp_layer_fwd(h, W["norm_mix"][i], W["gmlp"][j], i)
        h2, s_ffn = _ffn_fwd(h1, W["norm_ffn"][i], W["ffn"][i], i)
        P = W["ple"][i]
        h3, pe, gate = _ple_fwd(h2, p[i], P["wp"], P["pn"], P["gn"], P["wgate"], f"ple_{i}")
        saved.append((s_mix, s_ffn, (h2, pe, gate)))
        h = h3
    dh, loss, g_fn = _loss_head(h, target, W["final_norm"], "loss_head")
    grads = dict(final_norm=g_fn[0], ssd=[None] * len(W["ssd"]), gmlp=[None] * len(W["gmlp"]), ffn=[None] * depth, ple=[None] * depth)
    for i in reversed(range(depth)):
        j = i // 2
        s_mix, s_ffn, (h2, pe, gate) = saved[i]
        P = W["ple"][i]
        dh, dzg, dpe, q, g_pn, g_gn = _ple_bwd(dh, h2, pe, gate, P["pn"], P["gn"], P["wgate"], f"ple_bwd_{i}")
        grads["ple"][i] = dict(
            wgate=_mm(q, dzg, mode="tn", out_dtype=bf16, name=f"ple_gate_dw_{i}"),
            wp=_mm(p[i], dpe, mode="tn", out_dtype=bf16, name=f"ple_proj_dw_{i}"), pn=g_pn[0], gn=g_gn[0])
        dh, grads["ffn"][i] = _ffn_bwd(dh, s_ffn, W["norm_ffn"][i], W["ffn"][i], i)
        if i % 2 == 0:
            dh, grads["ssd"][j] = _ssd_layer_bwd(dh, s_mix, W["norm_mix"][i], W["ssd"][j], i)
        else:
            dh, grads["gmlp"][j] = _gmlp_layer_bwd(dh, s_mix, W["norm_mix"][i], W["gmlp"][j], i)
    return loss[0, 0], dh, grads


def _flip(v, f):
    return 1 - v if f else v


_ANY = pl.BlockSpec(memory_space=pl.ANY)


def _gather_chips(arrs):
    flat = [(t, l) for t, a in enumerate(arrs) for l in range(a.shape[0])]
    n_in, n_out = len(arrs), len(flat)
    flips = [(1, 0), (0, 1), (1, 1)]
    nf = len(flips)

    def body(*refs):
        ins, outs = refs[:n_in], refs[n_in + n_out:n_in + 2 * n_out]
        send_sems, recv_sems, fsend_sems, frecv_sems = refs[n_in + 2 * n_out:]
        x, y, c = lax.axis_index("x"), lax.axis_index("y"), lax.axis_index("c")
        mychip = 2 * x + y

        def rows(o, h):
            r2 = arrs[flat[o][0]].shape[1] // 2
            return pl.ds(h * r2, r2)

        def ici(o, j, slot):
            t, l = flat[o]
            fx, fy = flips[j]
            return pltpu.make_async_remote_copy(
                src_ref=ins[t].at[l, rows(o, c)], dst_ref=outs[o].at[slot, rows(o, c)], send_sem=send_sems.at[nf * o + j],
                recv_sem=recv_sems.at[nf * o + j], device_id=(_flip(x, fx), _flip(y, fy), c), device_id_type=MESH)

        def forward(o, j, h):
            fx, fy = flips[j]
            part = outs[o].at[2 * _flip(x, fx) + _flip(y, fy), rows(o, h)]
            return pltpu.make_async_remote_copy(
                src_ref=part, dst_ref=part, send_sem=fsend_sems.at[nf * o + j], recv_sem=frecv_sems.at[nf * o + j],
                device_id=(x, y, 1 - c), device_id_type=MESH)

        sends = [ici(o, j, mychip) for o in range(n_out) for j in range(nf)]
        for cp in sends:
            cp.start()
        forwards = []
        for o in range(n_out):
            for j, (fx, fy) in enumerate(flips):
                ici(o, j, 2 * _flip(x, fx) + _flip(y, fy)).wait_recv()
                forwards.append(forward(o, j, c))
                forwards[-1].start()
        for o in range(n_out):
            for j in range(nf):
                forward(o, j, 1 - c).wait_recv()
        for cp in sends + forwards:
            cp.wait_send()

    inits = [jnp.broadcast_to(arrs[t][l][None], (N_CHIPS,) + arrs[t].shape[1:]) for t, l in flat]
    dma = pltpu.SemaphoreType.DMA
    outs = pl.pallas_call(
        body, name="gather_weights", in_specs=[_ANY] * (n_in + n_out), out_specs=[_ANY] * n_out,
        out_shape=[jax.ShapeDtypeStruct(b.shape, b.dtype) for b in inits],
        input_output_aliases={n_in + o: o for o in range(n_out)},
        scratch_shapes=[dma((nf * n_out,)), dma((nf * n_out,)), dma((nf * n_out,)), dma((nf * n_out,))],
    )(*arrs, *inits)
    res = [[] for _ in arrs]
    for o, (t, _) in enumerate(flat):
        res[t].append(outs[o])
    return res


def _half_struct(a, lead):
    return jax.ShapeDtypeStruct(lead + (a.shape[-2] // 2, a.shape[-1]), a.dtype)


def _sibling_split(tensors, rep):
    n = len(tensors)

    def body(*refs):
        ins, rep_ref = refs[:n], refs[n]
        got, rep_got = refs[n + 1:2 * n + 1], refs[2 * n + 1]
        send_sems, recv_sems = refs[2 * n + 2:]
        x, y, c = lax.axis_index("x"), lax.axis_index("y"), lax.axis_index("c")

        def swap(i):
            if i < n:
                r2 = tensors[i].shape[1] // 2
                src, dst = ins[i].at[:, pl.ds((1 - c) * r2, r2)], got[i]
            else:
                src, dst = rep_ref, rep_got
            return pltpu.make_async_remote_copy(src_ref=src, dst_ref=dst, send_sem=send_sems.at[i], recv_sem=recv_sems.at[i],
                                                device_id=(x, y, 1 - c), device_id_type=MESH)

        copies = [swap(i) for i in range(n + 1)]
        for cp in copies:
            cp.start()
        for cp in copies:
            cp.wait()

    dma = pltpu.SemaphoreType.DMA
    outs = pl.pallas_call(
        body, name="grads_sibling_split", in_specs=[_ANY] * (n + 1), out_specs=[_ANY] * (n + 1),
        out_shape=[_half_struct(t, (N_CHIPS,)) for t in tensors] + [jax.ShapeDtypeStruct(rep.shape, rep.dtype)],
        scratch_shapes=[dma((n + 1,)), dma((n + 1,))],
    )(*tensors, rep)
    return outs[:n], outs[n]


def _chip_exchange(parts, rep):
    n = len(parts)
    flips = [(1, 0), (0, 1), (1, 1)]
    nf = len(flips)

    def body(*refs):
        ins = refs[:n + 1]
        outs = refs[n + 1:2 * n + 2]
        send_sems, recv_sems = refs[2 * n + 2:]
        x, y, c = lax.axis_index("x"), lax.axis_index("y"), lax.axis_index("c")
        mychip = 2 * x + y

        def copy(i, j, slot):
            px, py = _flip(x, flips[j][0]), _flip(y, flips[j][1])
            return pltpu.make_async_remote_copy(
                src_ref=ins[i].at[2 * px + py] if i < n else ins[i], dst_ref=outs[i].at[slot], send_sem=send_sems.at[nf * i + j],
                recv_sem=recv_sems.at[nf * i + j], device_id=(px, py, c), device_id_type=MESH)

        sends = [copy(i, j, mychip) for i in range(n + 1) for j in range(nf)]
        for cp in sends:
            cp.start()
        for i in range(n + 1):
            for j, (fx, fy) in enumerate(flips):
                copy(i, j, 2 * _flip(x, fx) + _flip(y, fy)).wait_recv()
        for cp in sends:
            cp.wait_send()

    dma = pltpu.SemaphoreType.DMA
    outs = pl.pallas_call(
        body, name="grads_chip_exchange", in_specs=[_ANY] * (n + 1), out_specs=[_ANY] * (n + 1),
        out_shape=[jax.ShapeDtypeStruct(t.shape, t.dtype) for t in parts] + [jax.ShapeDtypeStruct((N_CHIPS,) + rep.shape, rep.dtype)],
        scratch_shapes=[dma((nf * (n + 1),)), dma((nf * (n + 1),))],
    )(*parts, rep)
    return outs[:n], outs[n]


def _sibling_join(bufs):
    flat = [(gi, l) for gi, b in enumerate(bufs) for l in range(b.shape[0])]
    n, n_buf = len(flat), len(bufs)

    def body(*refs):
        outs = refs[n_buf:2 * n_buf]
        send_sems, recv_sems = refs[2 * n_buf:]
        x, y, c = lax.axis_index("x"), lax.axis_index("y"), lax.axis_index("c")

        def push(i, h):
            gi, l = flat[i]
            r2 = bufs[gi].shape[1] // 2
            part = outs[gi].at[l, pl.ds(h * r2, r2)]
            return pltpu.make_async_remote_copy(src_ref=part, dst_ref=part, send_sem=send_sems.at[i], recv_sem=recv_sems.at[i],
                                                device_id=(x, y, 1 - c), device_id_type=MESH)

        sends = [push(i, c) for i in range(n)]
        for cp in sends:
            cp.start()
        for i in range(n):
            push(i, 1 - c).wait_recv()
        for cp in sends:
            cp.wait_send()

    dma = pltpu.SemaphoreType.DMA
    return pl.pallas_call(
        body, name="grads_sibling_join", in_specs=[_ANY] * n_buf, out_specs=[_ANY] * n_buf,
        out_shape=[jax.ShapeDtypeStruct(b.shape, b.dtype) for b in bufs],
        input_output_aliases={i: i for i in range(n_buf)},
        scratch_shapes=[dma((n,)), dma((n,))],
    )(*bufs)


def _sum_tile_rows(rows, cols):
    tr = rows
    while tr * cols > 256 * 1024 and tr % (4 * SUBLANES) == 0:
        tr //= 2
    return tr


def _pair_sum(full, got, place, name):
    nch, R, C = full.shape
    R2 = R // 2
    tr = _sum_tile_rows(R2, C)
    nb = R2 // tr

    def body(place_ref, a_ref, b_ref, o_ref):
        o_ref[...] = (a_ref[...].astype(f32) + b_ref[...].astype(f32)).astype(o_ref.dtype)

    return pl.pallas_call(
        body, name=name, out_shape=jax.ShapeDtypeStruct((nch, R2, C), full.dtype),
        grid_spec=pltpu.PrefetchScalarGridSpec(
            num_scalar_prefetch=1, grid=(nch, nb),
            in_specs=[pl.BlockSpec((None, tr, C), lambda k, i, pr: (k, pr[1] * nb + i, 0)),
                      pl.BlockSpec((None, tr, C), lambda k, i, pr: (k, i, 0))],
            out_specs=pl.BlockSpec((None, tr, C), lambda k, i, pr: (k, i, 0))),
        compiler_params=_cparams(("parallel", "parallel")),
    )(place, full, got)


def _add2(a, b, name):
    R, C = a.shape
    tr = _sum_tile_rows(R, C)

    def body(a_ref, b_ref, o_ref):
        o_ref[...] = a_ref[...] + b_ref[...]

    spec = pl.BlockSpec((tr, C), lambda i: (i, 0))
    return pl.pallas_call(body, name=name, grid=(R // tr,), in_specs=[spec, spec], out_specs=spec,
                          out_shape=jax.ShapeDtypeStruct((R, C), f32), compiler_params=_cparams(("parallel",)))(a, b)


def _chip_sum(landed, mine, place, name, into=None, layer=0, layers=1):
    nch, R2, C = landed.shape
    tr = _sum_tile_rows(R2, C)
    nb = R2 // tr

    def body(*refs):
        place_ref, l_ref, m_ref, o_ref = refs[0], refs[1], refs[2], refs[-1]
        s = jnp.where(place_ref[0] == 0, m_ref[...].astype(f32), l_ref[0].astype(f32))
        for k in range(1, nch):
            s = s + jnp.where(place_ref[0] == k, m_ref[...].astype(f32), l_ref[k].astype(f32))
        o_ref[...] = s

    in_specs = [pl.BlockSpec((nch, tr, C), lambda i, pr: (0, i, 0)),
                pl.BlockSpec((None, tr, C), lambda i, pr: (pr[0], i, 0))]
    args = [place, landed, mine]
    if into is not None:
        in_specs.append(_ANY)
        args.append(into)
    return pl.pallas_call(
        body, name=name, out_shape=jax.ShapeDtypeStruct((layers, 2 * R2, C), f32),
        grid_spec=pltpu.PrefetchScalarGridSpec(
            num_scalar_prefetch=1, grid=(nb,), in_specs=in_specs,
            out_specs=pl.BlockSpec((None, tr, C), lambda i, pr: (layer, pr[1] * nb + i, 0))),
        input_output_aliases={3: 0} if into is not None else {},
        compiler_params=_cparams(("parallel",)),
    )(*args)


def _chip_sum_rep(landed, mine, place, name):
    nch, R, C = landed.shape
    tr = _sum_tile_rows(R, C)

    def body(place_ref, l_ref, m_ref, o_ref):
        s = jnp.where(place_ref[0] == 0, m_ref[...], l_ref[0])
        for k in range(1, nch):
            s = s + jnp.where(place_ref[0] == k, m_ref[...], l_ref[k])
        o_ref[...] = s

    return pl.pallas_call(
        body, name=name, out_shape=jax.ShapeDtypeStruct((R, C), f32),
        grid_spec=pltpu.PrefetchScalarGridSpec(
            num_scalar_prefetch=1, grid=(R // tr,),
            in_specs=[pl.BlockSpec((nch, tr, C), lambda i, pr: (0, i, 0)), pl.BlockSpec((tr, C), lambda i, pr: (i, 0))],
            out_specs=pl.BlockSpec((tr, C), lambda i, pr: (i, 0))),
        compiler_params=_cparams(("parallel",)),
    )(place, landed, mine)


PACK_COLS = 1024
PACK_ROW_MULTIPLE = 64

BIG = ("ssd_w_in", "ssd_w_out", "gmlp_w_in", "gmlp_w_out", "ffn_w_gate", "ffn_w_up", "ffn_w_down", "ple_w_proj", "ple_w_gate")
SMALL_SHARDED = ("ssd_conv_w", "gmlp_b_in", "gmlp_ln_w", "gmlp_ln_b")
REPLICATED = ("norm_mix", "norm_ffn", "ssd_conv_b", "ssd_dt_bias", "ssd_a_log", "ssd_d", "ssd_norm_w", "gmlp_w_s", "gmlp_b_s",
              "ple_norm", "ple_gate_norm", "final_norm")
WEIGHTS = ("norm_mix", "norm_ffn", "ssd_w_in", "ssd_conv_w", "ssd_conv_b", "ssd_dt_bias", "ssd_a_log", "ssd_d", "ssd_norm_w", "ssd_w_out",
           "gmlp_w_in", "gmlp_b_in", "gmlp_ln_w", "gmlp_ln_b", "gmlp_w_s", "gmlp_b_s", "gmlp_w_out", "ffn_w_gate", "ffn_w_up",
           "ffn_w_down", "ple_w_proj", "ple_norm", "ple_gate_norm", "ple_w_gate", "final_norm")
COLUMN_SHARDED = ("ssd_w_in", "gmlp_w_in", "ple_w_proj")


def _pack(arrs):
    flat = jnp.concatenate([a.reshape(-1).astype(f32) for a in arrs])
    per = PACK_COLS * PACK_ROW_MULTIPLE
    n = -(-flat.shape[0] // per) * per
    return jnp.pad(flat, (0, n - flat.shape[0])).reshape(-1, PACK_COLS)


def _unpack(buf, shapes):
    flat = buf.reshape(-1)
    out, o = [], 0
    for s in shapes:
        n = math.prod(s)
        out.append(flat[o:o + n].reshape(s))
        o += n
    return out


def _chip_major(g):
    r, c4 = g.shape
    return g.reshape(r, N_CHIPS, c4 // N_CHIPS).transpose(1, 0, 2)


def _from_chip_major(g):
    k, r, c = g.shape
    return g.transpose(1, 0, 2).reshape(r, k * c)


def _adamw_nd(w, m, v, g, name):
    shp = w.shape
    two = lambda a: a.reshape(-1, shp[-1])
    return [o.reshape(shp) for o in _adamw(two(w), two(m), two(v), two(g), name)]


def kernel(x, p, norm_mix, norm_ffn, ssd_w_in, ssd_conv_w, ssd_conv_b, ssd_dt_bias, ssd_a_log, ssd_d, ssd_norm_w, ssd_w_out, gmlp_w_in, gmlp_b_in, gmlp_ln_w, gmlp_ln_b, gmlp_w_s, gmlp_b_s, gmlp_w_out, ffn_w_gate, ffn_w_up, ffn_w_down, ple_w_proj, ple_norm, ple_gate_norm, ple_w_gate, final_norm, loss_target, m_norm_mix, m_norm_ffn, m_ssd_w_in, m_ssd_conv_w, m_ssd_conv_b, m_ssd_dt_bias, m_ssd_a_log, m_ssd_d, m_ssd_norm_w, m_ssd_w_out, m_gmlp_w_in, m_gmlp_b_in, m_gmlp_ln_w, m_gmlp_ln_b, m_gmlp_w_s, m_gmlp_b_s, m_gmlp_w_out, m_ffn_w_gate, m_ffn_w_up, m_ffn_w_down, m_ple_w_proj, m_ple_norm, m_ple_gate_norm, m_ple_w_gate, m_final_norm, v_norm_mix, v_norm_ffn, v_ssd_w_in, v_ssd_conv_w, v_ssd_conv_b, v_ssd_dt_bias, v_ssd_a_log, v_ssd_d, v_ssd_norm_w, v_ssd_w_out, v_gmlp_w_in, v_gmlp_b_in, v_gmlp_ln_w, v_gmlp_ln_b, v_gmlp_w_s, v_gmlp_b_s, v_gmlp_w_out, v_ffn_w_gate, v_ffn_w_up, v_ffn_w_down, v_ple_w_proj, v_ple_norm, v_ple_gate_norm, v_ple_w_gate, v_final_norm):
    given = dict(locals())
    w = {n: given[n] for n in WEIGHTS}
    mom = {n: given["m_" + n] for n in WEIGHTS}
    var = {n: given["v_" + n] for n in WEIGHTS}
    depth = p.shape[0]
    n_ssd, n_gmlp = ssd_w_in.shape[0], gmlp_w_in.shape[0]
    inner = ssd_dt_bias.shape[1] * HEADDIM
    conv_dim = ssd_conv_b.shape[1]

    small_shapes = [w[n].shape for n in SMALL_SHARDED]
    gathered = _gather_chips([w[n].astype(bf16) for n in BIG] + [_pack([w[n] for n in SMALL_SHARDED])[None]])
    gw = dict(zip(BIG, gathered[:-1]))
    small_by_chip = [_unpack(gathered[-1][0][k], small_shapes) for k in range(N_CHIPS)]
    small_full = {n: jnp.concatenate([small_by_chip[k][i] for k in range(N_CHIPS)], axis=-1) for i, n in enumerate(SMALL_SHARDED)}

    W = dict(norm_mix=norm_mix, norm_ffn=norm_ffn, final_norm=final_norm, ssd=[], gmlp=[], ffn=[], ple=[])
    for j in range(n_ssd):
        w_in = _from_chip_major(gw["ssd_w_in"][j])
        W["ssd"].append(dict(
            wz=w_in[:, :inner], wxbc=w_in[:, inner:inner + conv_dim], wdt=w_in[:, inner + conv_dim:],
            conv_w=small_full["ssd_conv_w"][j], conv_b=ssd_conv_b[j], dt_bias=ssd_dt_bias[j], a_log=ssd_a_log[j], d=ssd_d[j],
            norm_w=ssd_norm_w[j], wout=gw["ssd_w_out"][j].reshape(-1, gw["ssd_w_out"][j].shape[-1])))
    for j in range(n_gmlp):
        W["gmlp"].append(dict(
            win=_from_chip_major(gw["gmlp_w_in"][j]), b_in=small_full["gmlp_b_in"][j], ln_w=small_full["gmlp_ln_w"][j],
            ln_b=small_full["gmlp_ln_b"][j], w_s=gmlp_w_s[j], b_s=gmlp_b_s[j],
            wout=gw["gmlp_w_out"][j].reshape(-1, gw["gmlp_w_out"][j].shape[-1])))
    for i in range(depth):
        W["ffn"].append(dict(wg=gw["ffn_w_gate"][i], wu=gw["ffn_w_up"][i], wd=gw["ffn_w_down"][i]))
        W["ple"].append(dict(wp=_from_chip_major(gw["ple_w_proj"][i]), pn=ple_norm[i], gn=ple_gate_norm[i],
                             wgate=gw["ple_w_gate"][i].reshape(-1, gw["ple_w_gate"][i].shape[-1])))

    loss_part, grad_x, g = _local_step(x[0], p[:, 0], loss_target[0], W)
    loss = lax.psum(loss_part, ("x", "y", "c"))

    rows4 = lambda a: a.reshape((N_CHIPS, a.shape[0] // N_CHIPS) + a.shape[1:])
    big = dict(
        ssd_w_in=[_chip_major(s["w_in"]) for s in g["ssd"]], ssd_w_out=[rows4(s["wout"]) for s in g["ssd"]],
        gmlp_w_in=[_chip_major(s["win"]) for s in g["gmlp"]], gmlp_w_out=[rows4(s["wout"]) for s in g["gmlp"]],
        ffn_w_gate=[s["wg"] for s in g["ffn"]], ffn_w_up=[s["wu"] for s in g["ffn"]], ffn_w_down=[s["wd"] for s in g["ffn"]],
        ple_w_proj=[_chip_major(s["wp"]) for s in g["ple"]], ple_w_gate=[rows4(s["wgate"]) for s in g["ple"]])
    small_g = dict(ssd_conv_w=jnp.stack([s["conv_w"] for s in g["ssd"]]), gmlp_b_in=jnp.stack([s["b_in"] for s in g["gmlp"]]),
                   gmlp_ln_w=jnp.stack([s["ln_w"] for s in g["gmlp"]]), gmlp_ln_b=jnp.stack([s["ln_b"] for s in g["gmlp"]]))
    cut = lambda a, k: a[..., k * (a.shape[-1] // N_CHIPS):(k + 1) * (a.shape[-1] // N_CHIPS)]
    small_packed = jnp.stack([_pack([cut(small_g[n], k) for n in SMALL_SHARDED]) for k in range(N_CHIPS)])
    mix = [g["ssd"][i // 2]["norm_mix"] if i % 2 == 0 else g["gmlp"][i // 2]["norm_mix"] for i in range(depth)]
    rep_g = dict(
        norm_mix=jnp.stack(mix), norm_ffn=jnp.stack([s["norm_ffn"] for s in g["ffn"]]),
        ssd_conv_b=jnp.stack([s["conv_b"] for s in g["ssd"]]), ssd_dt_bias=jnp.stack([s["dt_bias"] for s in g["ssd"]]),
        ssd_a_log=jnp.stack([s["a_log"] for s in g["ssd"]]), ssd_d=jnp.stack([s["d"] for s in g["ssd"]]),
        ssd_norm_w=jnp.stack([s["norm_w"] for s in g["ssd"]]), gmlp_w_s=jnp.stack([s["w_s"] for s in g["gmlp"]]),
        gmlp_b_s=jnp.stack([s["b_s"] for s in g["gmlp"]]), ple_norm=jnp.stack([s["pn"] for s in g["ple"]]),
        ple_gate_norm=jnp.stack([s["gn"] for s in g["ple"]]), final_norm=g["final_norm"])
    rep_packed = _pack([rep_g[n] for n in REPLICATED])
    place = jnp.stack([2 * lax.axis_index("x") + lax.axis_index("y"), lax.axis_index("c")]).astype(jnp.int32)
    tensors = [a for n in BIG for a in big[n]] + [small_packed]
    got, rep_got = _sibling_split(tensors, rep_packed)
    pair_sums = [_pair_sum(a, b, place, f"grads_pair_sum_{i}") for i, (a, b) in enumerate(zip(tensors, got))]
    rep_pair = _add2(rep_packed, rep_got, "grads_pair_sum_rep")
    chip_parts, rep_parts = _chip_exchange(pair_sums, rep_pair)
    rep_total = _chip_sum_rep(rep_parts, rep_pair, place, "grads_chip_sum_rep")
    bufs, o = [], 0
    for n in BIG + ("small",):
        layers = len(big[n]) if n != "small" else 1
        buf = None
        for l in range(layers):
            buf = _chip_sum(chip_parts[o], pair_sums[o], place, f"grads_chip_sum_{o}", into=buf, layer=l, layers=layers)
            o += 1
        bufs.append(buf)
    reduced = _sibling_join(bufs)

    res = {}
    for n, gsum in zip(BIG, reduced):
        res[n] = [gsum] + _adamw_nd(w[n], mom[n], var[n], gsum, "adamw_" + n)
    for names, gsum, tag in ((SMALL_SHARDED, reduced[-1][0], "adamw_small_sharded"), (REPLICATED, rep_total, "adamw_replicated")):
        packs = [gsum] + list(_adamw(_pack([w[n] for n in names]), _pack([mom[n] for n in names]), _pack([var[n] for n in names]), gsum, tag))
        per_kind = [_unpack(pk, [w[n].shape for n in names]) for pk in packs]
        for i, n in enumerate(names):
            res[n] = [per_kind[k][i] for k in range(4)]
    return (loss, grad_x[None], *[res[n][0] for n in WEIGHTS], *[res[n][1] for n in WEIGHTS],
            *[res[n][2] for n in WEIGHTS], *[res[n][3] for n in WEIGHTS])
```

```python
import functools
import math

import jax
import jax.numpy as jnp
from jax import lax
from jax.experimental import pallas as pl
from jax.experimental.pallas import tpu as pltpu

f32 = jnp.float32
bf16 = jnp.bfloat16
HI = lax.Precision.HIGHEST

LANES = 128
SUBLANES = 8
VMEM_LIMIT_BYTES = 56 * 1024 * 1024

HEADDIM = 64
STATE = 128
CHUNK = 128
CONV_K = 4
RMS_EPS = 1e-6
LN_EPS = 1e-5
ADAM_LR = 0.001
ADAM_B1 = 0.9
ADAM_B2 = 0.999
ADAM_EPS = 1e-08
ADAM_WD = 0.01
ADAM_STEP = 10

N_CHIPS = 4
N_DEV = 8
MESH = pl.DeviceIdType.MESH


def _cparams(sem):
    return pltpu.CompilerParams(dimension_semantics=sem, vmem_limit_bytes=VMEM_LIMIT_BYTES)


def _tile(n, want):
    if n <= want:
        return n
    t = want
    while n % t:
        t //= 2
    return t


def _row_spec(tm, c):
    return pl.BlockSpec((tm, c), lambda i: (i, 0))


def _full_spec(shape):
    nd = len(shape)
    return pl.BlockSpec(tuple(shape), lambda *_: (0,) * nd)


def _sigmoid(x):
    return 1.0 / (1.0 + jnp.exp(-x))


def _silu(x):
    return x * _sigmoid(x)


def _dsilu(x):
    s = _sigmoid(x)
    return s * (1.0 + x * (1.0 - s))


def _gelu(x):
    return 0.5 * x * (1.0 + lax.erf(x * (1.0 / math.sqrt(2.0))))


def _dgelu(x):
    return 0.5 * (1.0 + lax.erf(x * (1.0 / math.sqrt(2.0)))) + x * jnp.exp(-0.5 * x * x) * (1.0 / math.sqrt(2.0 * math.pi))


def _softplus(x):
    return jnp.maximum(x, 0.0) + jnp.log(1.0 + jnp.exp(-jnp.abs(x)))


def _rms(x, w, eps):
    r = lax.rsqrt(jnp.mean(x * x, axis=-1, keepdims=True) + eps)
    return x * r * w


def _rms_bwd(dy, x, w, eps):
    r = lax.rsqrt(jnp.mean(x * x, axis=-1, keepdims=True) + eps)
    xh = x * r
    g = dy * w
    dx = r * (g - xh * jnp.mean(g * xh, axis=-1, keepdims=True))
    dw = jnp.sum(dy * xh, axis=0, keepdims=True)
    return dx, dw


def _dot(a, b, dims=(((1,), (0,)), ((), ())), precision=None):
    return lax.dot_general(a, b, dims, precision=precision, preferred_element_type=f32)


NN = (((1,), (0,)), ((), ()))
NT = (((1,), (1,)), ((), ()))
TN = (((0,), (0,)), ((), ()))


def _split3(x):
    hi = x.astype(bf16)
    r1 = x - hi.astype(f32)
    mid = r1.astype(bf16)
    return hi, mid, (r1 - mid.astype(f32)).astype(bf16)


def _dot01_left(m01, x):
    mb = m01.astype(bf16)
    hi, mid, lo = _split3(x)
    return _dot(mb, hi, NN) + _dot(mb, mid, NN) + _dot(mb, lo, NN)


def _dot01_right(x, m01):
    mb = m01.astype(bf16)
    hi, mid, lo = _split3(x)
    return _dot(hi, mb, NN) + _dot(mid, mb, NN) + _dot(lo, mb, NN)


def _mm(a, b, *, mode="nn", out_dtype=f32, res=None, kbatch=False, tm=1024, tn=1024, tk=1024, name):
    a3, b3 = a.ndim == 3, b.ndim == 3
    nb = a.shape[0] if a3 else (b.shape[0] if b3 else 1)
    ash, bsh = a.shape[-2:], b.shape[-2:]
    if mode == "nn":
        M, K, N = ash[0], ash[1], bsh[1]
    elif mode == "nt":
        M, K, N = ash[0], ash[1], bsh[0]
    else:
        K, M, N = ash[0], ash[1], bsh[1]
    tm, tn, tk = _tile(M, tm), (N if N % LANES else _tile(N, tn)), (K if K % LANES else _tile(K, tk))
    nk = K // tk
    if kbatch:
        assert a3 and b3
        grid = (1, M // tm, N // tn, nb * nk)
        bi = lambda g, k: k // nk
        ki = lambda g, k: k % nk
    else:
        grid = (nb, M // tm, N // tn, nk)
        bi = lambda g, k: g
        ki = lambda g, k: k
    nsteps = grid[3]

    def spec(is3, blk, imap):
        if is3:
            return pl.BlockSpec((None,) + blk, lambda g, i, j, k: (bi(g, k),) + imap(i, j, ki(g, k)))
        return pl.BlockSpec(blk, lambda g, i, j, k: imap(i, j, ki(g, k)))

    if mode == "nn":
        a_spec = spec(a3, (tm, tk), lambda i, j, k: (i, k))
        b_spec = spec(b3, (tk, tn), lambda i, j, k: (k, j))
        dims = NN
    elif mode == "nt":
        a_spec = spec(a3, (tm, tk), lambda i, j, k: (i, k))
        b_spec = spec(b3, (tn, tk), lambda i, j, k: (j, k))
        dims = NT
    else:
        a_spec = spec(a3, (tk, tm), lambda i, j, k: (k, i))
        b_spec = spec(b3, (tk, tn), lambda i, j, k: (k, j))
        dims = TN
    out3 = (a3 or b3) and not kbatch
    if out3:
        o_spec = pl.BlockSpec((None, tm, tn), lambda g, i, j, k: (g, i, j))
        o_shape = jax.ShapeDtypeStruct((nb, M, N), out_dtype)
    else:
        o_spec = pl.BlockSpec((tm, tn), lambda g, i, j, k: (i, j))
        o_shape = jax.ShapeDtypeStruct((M, N), out_dtype)
    in_specs = [a_spec, b_spec]
    args = [a, b]
    if res is not None:
        in_specs.append(pl.BlockSpec((tm, tn), lambda g, i, j, k: (i, j)))
        args.append(res)

    def body(*refs):
        if res is not None:
            a_ref, b_ref, r_ref, o_ref, acc_ref = refs
        else:
            a_ref, b_ref, o_ref, acc_ref = refs
        k = pl.program_id(3)

        @pl.when(k == 0)
        def _():
            acc_ref[...] = jnp.zeros_like(acc_ref)

        acc_ref[...] += _dot(a_ref[...].astype(bf16), b_ref[...].astype(bf16), dims)

        @pl.when(k == nsteps - 1)
        def _():
            r = acc_ref[...]
            if res is not None:
                r = r + r_ref[...]
            o_ref[...] = r.astype(o_ref.dtype)

    return pl.pallas_call(
        body, name=name, grid=grid, in_specs=in_specs, out_specs=o_spec, out_shape=o_shape,
        scratch_shapes=[pltpu.VMEM((tm, tn), f32)],
        compiler_params=_cparams(("parallel", "parallel", "parallel", "arbitrary")),
    )(*args)


def _rowcall(fn, *, name, rows, fulls, out_rows, out_accs=(), tm=512):
    S = rows[0].shape[0]
    tm = _tile(S, tm)
    n_r, n_f, n_or, n_oa = len(rows), len(fulls), len(out_rows), len(out_accs)

    def body(*refs):
        ins = [r[...] for r in refs[:n_r + n_f]]
        outs = fn(*ins)
        if not isinstance(outs, (tuple, list)):
            outs = (outs,)
        o_refs = refs[n_r + n_f:]
        for o_ref, v in zip(o_refs[:n_or], outs[:n_or]):
            o_ref[...] = v.astype(o_ref.dtype)
        if n_oa:
            first = pl.program_id(0) == 0

            @pl.when(first)
            def _():
                for o_ref, v in zip(o_refs[n_or:], outs[n_or:]):
                    o_ref[...] = v

            @pl.when(jnp.logical_not(first))
            def _():
                for o_ref, v in zip(o_refs[n_or:], outs[n_or:]):
                    o_ref[...] += v

    in_specs = [_row_spec(tm, r.shape[1]) for r in rows] + [_full_spec(f.shape) for f in fulls]
    out_specs = [_row_spec(tm, c) for c, _ in out_rows] + [_full_spec(s) for s in out_accs]
    out_shape = [jax.ShapeDtypeStruct((S, c), d) for c, d in out_rows] + [jax.ShapeDtypeStruct(s, f32) for s in out_accs]
    res = pl.pallas_call(
        body, name=name, grid=(S // tm,), in_specs=in_specs, out_specs=out_specs, out_shape=out_shape,
        compiler_params=_cparams(("arbitrary",) if n_oa else ("parallel",)),
    )(*rows, *fulls)
    return res


def _row2(v):
    return v.reshape(1, -1)


def _rms_fwd(h, w, name):
    D = h.shape[1]
    return _rowcall(lambda x, w_: _rms(x, w_, RMS_EPS), name=name, rows=[h], fulls=[_row2(w)], out_rows=[(D, bf16)])[0]


def _conv_fwd(xpre, w, b, name):
    S, C = xpre.shape
    tm, tc = _tile(S, 512), _tile(C, 1024)
    hb = tm // SUBLANES

    def body(x_ref, halo_ref, w_ref, b_ref, c_ref, o_ref):
        i = pl.program_id(1)
        x = x_ref[...]
        halo = jnp.where(i > 0, halo_ref[...], 0.0)
        row = lax.broadcasted_iota(jnp.int32, x.shape, 0)
        row8 = lax.broadcasted_iota(jnp.int32, halo.shape, 0)
        x0 = x[0:SUBLANES, :]
        acc = x * w_ref[CONV_K - 1:CONV_K, :] + b_ref[...]
        acc0 = x0 * w_ref[CONV_K - 1:CONV_K, :] + b_ref[...]
        for k in range(1, CONV_K):
            wk = w_ref[CONV_K - 1 - k:CONV_K - k, :]
            acc = acc + pltpu.roll(x, k, axis=0) * wk
            acc0 = acc0 + jnp.where(row8 < k, pltpu.roll(halo, k, axis=0), pltpu.roll(x0, k, axis=0)) * wk
        c_ref[...] = acc
        o_ref[...] = _silu(acc)
        c_ref[0:SUBLANES, :] = acc0
        o_ref[0:SUBLANES, :] = _silu(acc0)

    return pl.pallas_call(
        body, name=name, grid=(C // tc, S // tm),
        in_specs=[pl.BlockSpec((tm, tc), lambda j, i: (i, j)),
                  pl.BlockSpec((SUBLANES, tc), lambda j, i: (jnp.maximum(i * hb - 1, 0), j)),
                  pl.BlockSpec((CONV_K, tc), lambda j, i: (0, j)),
                  pl.BlockSpec((1, tc), lambda j, i: (0, j))],
        out_specs=[pl.BlockSpec((tm, tc), lambda j, i: (i, j))] * 2,
        out_shape=[jax.ShapeDtypeStruct((S, C), f32)] * 2,
        compiler_params=_cparams(("parallel", "parallel")),
    )(xpre, xpre, w, _row2(b))


def _conv_bwd_dc(dxbc, c, xpre, name):
    S, C = xpre.shape
    tm, tc = _tile(S, 512), _tile(C, 1024)
    hb = tm // SUBLANES

    def body(d_ref, c_ref, x_ref, halo_ref, dc_ref, dw_ref, db_ref):
        i = pl.program_id(1)
        x = x_ref[...]
        dc = d_ref[...] * _dsilu(c_ref[...])
        dc_ref[...] = dc
        halo = jnp.where(i > 0, halo_ref[...], 0.0)
        row = lax.broadcasted_iota(jnp.int32, x.shape, 0)
        row8 = lax.broadcasted_iota(jnp.int32, halo.shape, 0)
        x0 = x[0:SUBLANES, :]
        dc0 = dc[0:SUBLANES, :]
        parts = [jnp.sum(dc * x, axis=0, keepdims=True)]
        for k in range(1, CONV_K):
            xs_big = jnp.where(row < SUBLANES, 0.0, pltpu.roll(x, k, axis=0))
            xs0 = jnp.where(row8 < k, pltpu.roll(halo, k, axis=0), pltpu.roll(x0, k, axis=0))
            parts.append(jnp.sum(dc * xs_big, axis=0, keepdims=True) + jnp.sum(dc0 * xs0, axis=0, keepdims=True))
        dw = jnp.concatenate([parts[CONV_K - 1 - k] for k in range(CONV_K)] + [jnp.zeros((SUBLANES - CONV_K, x.shape[1]), f32)], axis=0)
        db = jnp.sum(dc, axis=0, keepdims=True)

        @pl.when(i == 0)
        def _():
            dw_ref[...] = dw
            db_ref[...] = db

        @pl.when(i > 0)
        def _():
            dw_ref[...] += dw
            db_ref[...] += db

    return pl.pallas_call(
        body, name=name, grid=(C // tc, S // tm),
        in_specs=[pl.BlockSpec((tm, tc), lambda j, i: (i, j))] * 3 +
                 [pl.BlockSpec((SUBLANES, tc), lambda j, i: (jnp.maximum(i * hb - 1, 0), j))],
        out_specs=[pl.BlockSpec((tm, tc), lambda j, i: (i, j)),
                   pl.BlockSpec((SUBLANES, tc), lambda j, i: (0, j)),
                   pl.BlockSpec((1, tc), lambda j, i: (0, j))],
        out_shape=[jax.ShapeDtypeStruct((S, C), f32), jax.ShapeDtypeStruct((SUBLANES, C), f32), jax.ShapeDtypeStruct((1, C), f32)],
        compiler_params=_cparams(("parallel", "arbitrary")),
    )(dxbc, c, xpre, xpre)


def _conv_bwd_dx(dc, w, name):
    S, C = dc.shape
    tm, tc = _tile(S, 512), _tile(C, 1024)
    hb = tm // SUBLANES
    nrow = S // tm
    last8 = S // SUBLANES - 1

    def body(d_ref, nxt_ref, w_ref, o_ref):
        i = pl.program_id(1)
        d = d_ref[...]
        nxt = jnp.where(i < nrow - 1, nxt_ref[...], 0.0)
        row8 = lax.broadcasted_iota(jnp.int32, nxt.shape, 0)
        dl = d[tm - SUBLANES:tm, :]
        acc = d * w_ref[CONV_K - 1:CONV_K, :]
        accl = dl * w_ref[CONV_K - 1:CONV_K, :]
        for j in range(1, CONV_K):
            wk = w_ref[CONV_K - 1 - j:CONV_K - j, :]
            acc = acc + pltpu.roll(d, tm - j, axis=0) * wk
            accl = accl + jnp.where(row8 >= SUBLANES - j, pltpu.roll(nxt, SUBLANES - j, axis=0), pltpu.roll(dl, SUBLANES - j, axis=0)) * wk
        o_ref[...] = acc.astype(o_ref.dtype)
        o_ref[tm - SUBLANES:tm, :] = accl.astype(o_ref.dtype)

    return pl.pallas_call(
        body, name=name, grid=(C // tc, nrow),
        in_specs=[pl.BlockSpec((tm, tc), lambda j, i: (i, j)),
                  pl.BlockSpec((SUBLANES, tc), lambda j, i: (jnp.minimum((i + 1) * hb, last8), j)),
                  pl.BlockSpec((CONV_K, tc), lambda j, i: (0, j))],
        out_specs=pl.BlockSpec((tm, tc), lambda j, i: (i, j)),
        out_shape=jax.ShapeDtypeStruct((S, C), f32),
        compiler_params=_cparams(("parallel", "parallel")),
    )(dc, dc, w)


def _halfsum(v, lane_lo):
    s0 = jnp.sum(jnp.where(lane_lo, v, 0.0), axis=1, keepdims=True)
    s1 = jnp.sum(jnp.where(lane_lo, 0.0, v), axis=1, keepdims=True)
    return jnp.where(lane_lo, s0, s1)


def _ssd_specs(S, inner, GN, nchunks, rev):
    L = CHUNK
    cm = (lambda c: nchunks - 1 - c) if rev else (lambda c: c)
    xs = pl.BlockSpec((L, inner), lambda c: (cm(c), 0))
    bb = pl.BlockSpec((L, GN), lambda c: (cm(c), inner // GN))
    cc = pl.BlockSpec((L, GN), lambda c: (cm(c), inner // GN + 1))
    row = pl.BlockSpec((L, inner), lambda c: (cm(c), 0))
    vec = pl.BlockSpec((1, inner), lambda c: (0, 0))
    st = pl.BlockSpec((None, inner, STATE), lambda c: (cm(c), 0, 0))
    return xs, bb, cc, row, vec, st


def _ssd_fwd(xbc, dtx, ax, dx, G, name):
    S, inner = dtx.shape
    GN = G * STATE
    L = CHUNK
    nchunks = S // L
    npairs = inner // LANES
    ppg = npairs // G
    assert inner % GN == 0 and L == LANES and STATE == LANES

    def body(xs_ref, b_ref, c_ref, dtx_ref, ax_ref, dx_ref, y_ref, so_ref, st_ref):
        ci = pl.program_id(0)

        @pl.when(ci == 0)
        def _():
            st_ref[...] = jnp.zeros_like(st_ref)

        r = lax.broadcasted_iota(jnp.int32, (L, L), 0)
        cidx = lax.broadcasted_iota(jnp.int32, (L, L), 1)
        tril = cidx <= r
        lane_lo = cidx < HEADDIM
        xs = xs_ref[...]
        dtv = dtx_ref[...]
        X = xs * dtv
        da = dtv * ax_ref[...]
        cs = _dot01_left(tril, da)
        cs_last = jnp.sum(da, axis=0, keepdims=True)
        so_ref[...] = st_ref[...]
        for g in range(G):
            Bg = b_ref[:, g * STATE:(g + 1) * STATE].astype(bf16)
            Cg = c_ref[:, g * STATE:(g + 1) * STATE].astype(bf16)
            CB = _dot(Cg, Bg, NT)
            for j in range(ppg):
                lo = (g * ppg + j) * LANES
                tile = cs[:, lo:lo + LANES]
                rl = pltpu.roll(tile, HEADDIM, axis=1)
                Xp = X[:, lo:lo + LANES]
                prev = st_ref[lo:lo + LANES, :]
                ypair = _dot(Cg, prev.astype(bf16), NT) * jnp.exp(tile)
                for half in (0, 1):
                    hm = lane_lo if half == 0 else jnp.logical_not(lane_lo)
                    colb = jnp.where(hm, tile, rl)
                    Lm = jnp.exp(jnp.where(tril, colb - colb.T, -1e30))
                    W = (CB * Lm).astype(bf16)
                    ypair = ypair + _dot(W, jnp.where(hm, Xp, 0.0).astype(bf16), NN)
                y_ref[:, lo:lo + LANES] = ypair + xs[:, lo:lo + LANES] * dx_ref[:, lo:lo + LANES]
                last = cs_last[:, lo:lo + LANES]
                snew = _dot((Xp * jnp.exp(last - tile)).astype(bf16), Bg, TN)
                dec_rows = jnp.broadcast_to(jnp.exp(last), (L, LANES)).T
                st_ref[lo:lo + LANES, :] = dec_rows * prev + snew

    xs_s, b_s, c_s, row_s, vec_s, st_s = _ssd_specs(S, inner, GN, nchunks, False)
    return pl.pallas_call(
        body, name=name, grid=(nchunks,),
        in_specs=[xs_s, b_s, c_s, row_s, vec_s, vec_s],
        out_specs=[row_s, st_s],
        out_shape=[jax.ShapeDtypeStruct((S, inner), f32), jax.ShapeDtypeStruct((nchunks, inner, STATE), f32)],
        scratch_shapes=[pltpu.VMEM((inner, STATE), f32)],
        compiler_params=_cparams(("arbitrary",)),
    )(xbc, xbc, xbc, dtx, ax, dx)


def _ssd_bwd(dy, y, xbc, dtx, ax, dx, states, et, G, name):
    S, inner = dtx.shape
    H = et.shape[1]
    GN = G * STATE
    Cc = inner + 2 * GN
    L = CHUNK
    nchunks = S // L
    npairs = inner // LANES
    ppg = npairs // G

    def body(dy_ref, y_ref, xs_ref, b_ref, c_ref, dtx_ref, ax_ref, dx_ref, si_ref, et_ref,
             dxbc_ref, ddt_ref, dax_ref, ddx_ref, dst_ref, dA_ref, dAl_ref, ddtp_ref):
        ci = pl.program_id(0)

        @pl.when(ci == 0)
        def _():
            dst_ref[...] = jnp.zeros_like(dst_ref)
            dax_ref[...] = jnp.zeros_like(dax_ref)
            ddx_ref[...] = jnp.zeros_like(ddx_ref)

        r = lax.broadcasted_iota(jnp.int32, (L, L), 0)
        cidx = lax.broadcasted_iota(jnp.int32, (L, L), 1)
        tril = cidx <= r
        lane_lo = cidx < HEADDIM
        lane_lo1 = lax.broadcasted_iota(jnp.int32, (1, LANES), 1) < HEADDIM
        xs = xs_ref[...]
        dtv = dtx_ref[...]
        dyv = dy_ref[...]
        X = xs * dtv
        da = dtv * ax_ref[...]
        cs = _dot01_left(tril, da)
        cs_last = jnp.sum(da, axis=0, keepdims=True)
        for g in range(G):
            Bg = b_ref[:, g * STATE:(g + 1) * STATE].astype(bf16)
            Cg = c_ref[:, g * STATE:(g + 1) * STATE].astype(bf16)
            CB = _dot(Cg, Bg, NT)
            dCB = jnp.zeros((L, L), f32)
            dBg = jnp.zeros((L, STATE), f32)
            dCg = jnp.zeros((L, STATE), f32)
            for j in range(ppg):
                lo = (g * ppg + j) * LANES
                tile = cs[:, lo:lo + LANES]
                rl = pltpu.roll(tile, HEADDIM, axis=1)
                eA = jnp.exp(tile)
                Xp = X[:, lo:lo + LANES]
                dYp = dyv[:, lo:lo + LANES]
                xsp = xs[:, lo:lo + LANES]
                prev = si_ref[lo:lo + LANES, :]
                dSn = dst_ref[lo:lo + LANES, :]
                prev_b = prev.astype(bf16)
                dSn_b = dSn.astype(bf16)
                dYe = (dYp * eA).astype(bf16)
                dCg = dCg + _dot(dYe, prev_b, NN)
                dprev = _dot(dYe, Cg, TN)
                last = cs_last[:, lo:lo + LANES]
                w = jnp.exp(last - tile)
                BdS = _dot(Bg, dSn_b, NT)
                Xw = Xp * w
                XwB = Xw * BdS
                dAl_t = _halfsum(jnp.sum(XwB, axis=0, keepdims=True), lane_lo1)
                dBg = dBg + _dot(Xw.astype(bf16), dSn_b, NN)
                dec_rows = jnp.broadcast_to(jnp.exp(last), (L, LANES)).T
                dprev = dprev + dec_rows * dSn
                rsum = jnp.sum(dSn * prev * dec_rows, axis=1, keepdims=True)
                s0 = jnp.sum(rsum[0:HEADDIM], axis=0, keepdims=True)
                s1 = jnp.sum(rsum[HEADDIM:LANES], axis=0, keepdims=True)
                dAl_t = dAl_t + jnp.where(lane_lo1, s0, s1)
                dXd = jnp.zeros((L, LANES), f32)
                for half in (0, 1):
                    hm = lane_lo if half == 0 else jnp.logical_not(lane_lo)
                    colb = jnp.where(hm, tile, rl)
                    Lm = jnp.exp(jnp.where(tril, colb - colb.T, -1e30))
                    dYh = jnp.where(hm, dYp, 0.0).astype(bf16)
                    dW = _dot(dYh, jnp.where(hm, Xp, 0.0).astype(bf16), NT)
                    dXd = dXd + _dot((CB * Lm).astype(bf16), dYh, TN)
                    dCB = dCB + dW * Lm
                yoff = _dot(Cg, prev_b, NT) * eA
                ydiag = y_ref[:, lo:lo + LANES] - xsp * dx_ref[:, lo:lo + LANES] - yoff
                dYb = dYp.astype(bf16).astype(f32)
                Xb = Xp.astype(bf16).astype(f32)
                dA_t = _halfsum(dYb * ydiag - Xb * dXd + dYp * yoff - XwB, lane_lo)
                dXp = w * BdS + dXd
                dxbc_ref[:, lo:lo + LANES] = dXp * dtv[:, lo:lo + LANES] + dYp * dx_ref[:, lo:lo + LANES]
                ddtp_ref[:, lo:lo + LANES] = dXp * xsp
                ddx_ref[:, lo:lo + LANES] += jnp.sum(dYp * xsp, axis=0, keepdims=True)
                dA_ref[:, lo:lo + LANES] = dA_t
                dAl_ref[:, lo:lo + LANES] = dAl_t
                dst_ref[lo:lo + LANES, :] = dprev
            dCBb = dCB.astype(bf16)
            dxbc_ref[:, inner + g * STATE:inner + (g + 1) * STATE] = dBg + _dot(dCBb, Cg, TN)
            dxbc_ref[:, inner + GN + g * STATE:inner + GN + (g + 1) * STATE] = dCg + _dot(dCBb, Bg, NN)
        dda = _dot01_left(cidx >= r, dA_ref[...]) + dAl_ref[...]
        ddt_full = ddtp_ref[...] + dda * ax_ref[...] * (1.0 / HEADDIM)
        ddt_ref[...] = _dot01_right(ddt_full, et_ref[...])
        dax_ref[...] += jnp.sum(dda * dtv, axis=0, keepdims=True)

    xs_s, b_s, c_s, row_s, vec_s, st_s = _ssd_specs(S, inner, GN, nchunks, True)
    return pl.pallas_call(
        body, name=name, grid=(nchunks,),
        in_specs=[row_s, row_s, xs_s, b_s, c_s, row_s, vec_s, vec_s, st_s, _full_spec(et.shape)],
        out_specs=[pl.BlockSpec((L, Cc), lambda c: (nchunks - 1 - c, 0)),
                   pl.BlockSpec((L, H), lambda c: (nchunks - 1 - c, 0)), vec_s, vec_s],
        out_shape=[jax.ShapeDtypeStruct((S, Cc), f32), jax.ShapeDtypeStruct((S, H), f32),
                   jax.ShapeDtypeStruct((1, inner), f32), jax.ShapeDtypeStruct((1, inner), f32)],
        scratch_shapes=[pltpu.VMEM((inner, STATE), f32), pltpu.VMEM((L, inner), f32),
                        pltpu.VMEM((1, inner), f32), pltpu.VMEM((L, inner), f32)],
        compiler_params=_cparams(("arbitrary",)),
    )(dy, y, xbc, xbc, xbc, dtx, ax, dx, states, et)


def _dt_fwd(dt_pre, bias, e, name):
    H, inner = e.shape

    def fn(dp, b, e_):
        dt = _softplus(dp + b)
        return dt, _dot01_right(dt, e_)

    return _rowcall(fn, name=name, rows=[dt_pre], fulls=[_row2(bias), e], out_rows=[(H, f32), (inner, f32)])


def _dt_bwd(ddt, dt_pre, bias, name):
    H = ddt.shape[1]

    def fn(dd, dp, b):
        g = dd * _sigmoid(dp + b)
        return g, jnp.sum(g, axis=0, keepdims=True)

    return _rowcall(fn, name=name, rows=[ddt, dt_pre], fulls=[_row2(bias)], out_rows=[(H, f32)], out_accs=[(1, H)])


def _gnorm_fwd(y, z, w, G, name):
    inner = y.shape[1]
    gs = inner // G

    def fn(y_, z_, w_):
        gg = y_ * _silu(z_)
        outs = []
        for g in range(G):
            sl = slice(g * gs, (g + 1) * gs)
            outs.append(_rms(gg[:, sl], w_[:, sl], LN_EPS))
        return jnp.concatenate(outs, axis=1)

    return _rowcall(fn, name=name, rows=[y, z], fulls=[_row2(w)], out_rows=[(inner, bf16)], tm=256)[0]


def _gnorm_bwd(dyn, y, z, w, G, name):
    inner = y.shape[1]
    gs = inner // G

    def fn(d_, y_, z_, w_):
        sz = _silu(z_)
        gg = y_ * sz
        dgs, dws = [], []
        for g in range(G):
            sl = slice(g * gs, (g + 1) * gs)
            dg, dw = _rms_bwd(d_[:, sl], gg[:, sl], w_[:, sl], LN_EPS)
            dgs.append(dg)
            dws.append(dw)
        dgg = jnp.concatenate(dgs, axis=1)
        return dgg * sz, dgg * y_ * _dsilu(z_), jnp.concatenate(dws, axis=1)

    return _rowcall(fn, name=name, rows=[dyn, y, z], fulls=[_row2(w)], out_rows=[(inner, f32), (inner, f32)],
                    out_accs=[(1, inner)], tm=256)


def _gmlp_parts(pre, lw, lb, I):
    hp = _gelu(pre)
    uu = hp[:, :I]
    vp = hp[:, I:]
    xc = vp - jnp.mean(vp, axis=-1, keepdims=True)
    rstd = lax.rsqrt(jnp.mean(xc * xc, axis=-1, keepdims=True) + LN_EPS)
    vhat = xc * rstd
    return uu, vhat, rstd, vhat * lw + lb


def _gmlp_mid_fwd(pre, b_in, ln_w, ln_b, w_s, bsx, name):
    S, two_i = pre.shape
    I = two_i // 2
    NG = w_s.shape[0]
    gd = I // NG
    L = CHUNK

    def body(pre_ref, bi_ref, lw_ref, lb_ref, ws_ref, bsx_ref, o_ref):
        uu, _, _, vv = _gmlp_parts(pre_ref[...] + bi_ref[...], lw_ref[...], lb_ref[...], I)
        r = lax.broadcasted_iota(jnp.int32, (L, L), 0)
        cidx = lax.broadcasted_iota(jnp.int32, (L, L), 1)
        tril = cidx <= r
        for g in range(NG):
            sl = slice(g * gd, (g + 1) * gd)
            wg = jnp.where(tril, ws_ref[g], 0.0).astype(bf16)
            mixed = _dot(wg, vv[:, sl].astype(bf16), NN) + bsx_ref[:, sl]
            o_ref[:, sl] = (uu[:, sl] * mixed).astype(o_ref.dtype)

    return pl.pallas_call(
        body, name=name, grid=(S // L,),
        in_specs=[_row_spec(L, two_i), _full_spec((1, two_i)), _full_spec((1, I)), _full_spec((1, I)), _full_spec(w_s.shape), _full_spec(bsx.shape)],
        out_specs=_row_spec(L, I), out_shape=jax.ShapeDtypeStruct((S, I), bf16),
        compiler_params=_cparams(("parallel",)),
    )(pre, _row2(b_in), _row2(ln_w), _row2(ln_b), w_s, bsx)


def _gmlp_mid_bwd(do, pre, b_in, ln_w, ln_b, w_s, bsx, name):
    S, two_i = pre.shape
    I = two_i // 2
    NG = w_s.shape[0]
    gd = I // NG
    L = CHUNK

    def body(do_ref, pre_ref, bi_ref, lw_ref, lb_ref, ws_ref, bsx_ref, dpre_ref, dbi_ref, dlw_ref, dlb_ref, dws_ref, dbs_ref, dvv_ref):
        ci = pl.program_id(0)

        @pl.when(ci == 0)
        def _():
            for ref in (dbi_ref, dlw_ref, dlb_ref, dws_ref, dbs_ref):
                ref[...] = jnp.zeros_like(ref)

        pre = pre_ref[...] + bi_ref[...]
        lw = lw_ref[...]
        uu, vhat, rstd, vv = _gmlp_parts(pre, lw, lb_ref[...], I)
        dov = do_ref[...]
        r = lax.broadcasted_iota(jnp.int32, (L, L), 0)
        cidx = lax.broadcasted_iota(jnp.int32, (L, L), 1)
        tril = cidx <= r
        duus = []
        for g in range(NG):
            sl = slice(g * gd, (g + 1) * gd)
            wg = jnp.where(tril, ws_ref[g], 0.0).astype(bf16)
            vg = vv[:, sl].astype(bf16)
            mixed = _dot(wg, vg, NN) + bsx_ref[:, sl]
            duus.append(dov[:, sl] * mixed)
            dmixed = dov[:, sl] * uu[:, sl]
            dbs_ref[:, sl] += dmixed
            dmb = dmixed.astype(bf16)
            dvv_ref[:, sl] = _dot(wg, dmb, TN)
            dws_ref[g] += jnp.where(tril, _dot(dmb, vg, NT), 0.0)
        duu = jnp.concatenate(duus, axis=1)
        dvv = dvv_ref[...]
        dlw_ref[...] += jnp.sum(dvv * vhat, axis=0, keepdims=True)
        dlb_ref[...] += jnp.sum(dvv, axis=0, keepdims=True)
        dvh = dvv * lw
        dvp = rstd * (dvh - jnp.mean(dvh, axis=-1, keepdims=True) - vhat * jnp.mean(dvh * vhat, axis=-1, keepdims=True))
        dpre = jnp.concatenate([duu, dvp], axis=1) * _dgelu(pre)
        dbi_ref[...] += jnp.sum(dpre, axis=0, keepdims=True)
        dpre_ref[...] = dpre.astype(dpre_ref.dtype)

    return pl.pallas_call(
        body, name=name, grid=(S // L,),
        in_specs=[_row_spec(L, I), _row_spec(L, two_i), _full_spec((1, two_i)), _full_spec((1, I)), _full_spec((1, I)),
                  _full_spec(w_s.shape), _full_spec(bsx.shape)],
        out_specs=[_row_spec(L, two_i), _full_spec((1, two_i)), _full_spec((1, I)), _full_spec((1, I)), _full_spec(w_s.shape), _full_spec((L, I))],
        out_shape=[jax.ShapeDtypeStruct((S, two_i), bf16), jax.ShapeDtypeStruct((1, two_i), f32), jax.ShapeDtypeStruct((1, I), f32),
                   jax.ShapeDtypeStruct((1, I), f32), jax.ShapeDtypeStruct(w_s.shape, f32), jax.ShapeDtypeStruct((L, I), f32)],
        scratch_shapes=[pltpu.VMEM((L, I), f32)],
        compiler_params=_cparams(("arbitrary",)),
    )(do, pre, _row2(b_in), _row2(ln_w), _row2(ln_b), w_s, bsx)


def _lane_group_sum(acc, eg, name):
    NG = eg.shape[1]
    return _rowcall(lambda a, e: _dot(a, e, NN, HI), name=name, rows=[acc], fulls=[eg], out_rows=[(NG, f32)])[0]


def _ffn_fwd_fused(h1, nf_w, wg, wu, wd, name):
    S, D = h1.shape
    nb, _, F4 = wg.shape
    tm = _tile(S, 512)

    def body(h_ref, nf_ref, wg_ref, wu_ref, wd_ref, h2_ref, u_ref, g_ref, up_ref, a_ref, acc_ref):
        k = pl.program_id(1)

        @pl.when(k == 0)
        def _():
            u_ref[...] = _rms(h_ref[...], nf_ref[...], RMS_EPS).astype(u_ref.dtype)

        uv = u_ref[...]
        g = _dot(uv, wg_ref[...], NN)
        up = _dot(uv, wu_ref[...], NN)
        a = (_silu(g) * up).astype(bf16)
        g_ref[...] = g.astype(g_ref.dtype)
        up_ref[...] = up.astype(up_ref.dtype)
        a_ref[...] = a
        part = _dot(a, wd_ref[...], NN)

        @pl.when(k == 0)
        def _():
            acc_ref[...] = part

        @pl.when(k > 0)
        def _():
            acc_ref[...] += part

        @pl.when(k == nb - 1)
        def _():
            h2_ref[...] = h_ref[...] + acc_ref[...]

    row = pl.BlockSpec((tm, D), lambda i, k: (i, 0))
    wspec = pl.BlockSpec((None, D, F4), lambda i, k: (k, 0, 0))
    cspec = pl.BlockSpec((None, tm, F4), lambda i, k: (k, i, 0))
    chunk = jax.ShapeDtypeStruct((nb, S, F4), bf16)
    return pl.pallas_call(
        body, name=name, grid=(S // tm, nb),
        in_specs=[row, _full_spec((1, D)), wspec, wspec, pl.BlockSpec((None, F4, D), lambda i, k: (k, 0, 0))],
        out_specs=[row, row, cspec, cspec, cspec],
        out_shape=[jax.ShapeDtypeStruct((S, D), f32), jax.ShapeDtypeStruct((S, D), bf16), chunk, chunk, chunk],
        scratch_shapes=[pltpu.VMEM((tm, D), f32)],
        compiler_params=_cparams(("parallel", "arbitrary")),
    )(h1, _row2(nf_w), wg, wu, wd)


def _ffn_bwd_fused(dh, h1, nf_w, wd, wg, wu, G, U, name):
    S, D = dh.shape
    nb, F4, _ = wd.shape
    tm = _tile(S, 512)

    def body(dh_ref, h_ref, nf_ref, wd_ref, wg_ref, wu_ref, g_ref, up_ref, dg_ref, du_ref, dh1_ref, dnf_ref, acc_ref):
        i, k = pl.program_id(0), pl.program_id(1)
        dA = _dot(dh_ref[...].astype(bf16), wd_ref[...], NT)
        g = g_ref[...].astype(f32)
        dg = (dA * up_ref[...].astype(f32) * _dsilu(g)).astype(bf16)
        du = (dA * _silu(g)).astype(bf16)
        dg_ref[...] = dg
        du_ref[...] = du
        part = _dot(dg, wg_ref[...], NT) + _dot(du, wu_ref[...], NT)

        @pl.when(k == 0)
        def _():
            acc_ref[...] = part

        @pl.when(k > 0)
        def _():
            acc_ref[...] += part

        @pl.when(k == nb - 1)
        def _():
            dx, dw = _rms_bwd(acc_ref[...], h_ref[...], nf_ref[...], RMS_EPS)
            dh1_ref[...] = dh_ref[...] + dx

            @pl.when(i == 0)
            def _():
                dnf_ref[...] = dw

            @pl.when(i > 0)
            def _():
                dnf_ref[...] += dw

    row = pl.BlockSpec((tm, D), lambda i, k: (i, 0))
    wspec = pl.BlockSpec((None, D, F4), lambda i, k: (k, 0, 0))
    cspec = pl.BlockSpec((None, tm, F4), lambda i, k: (k, i, 0))
    chunk = jax.ShapeDtypeStruct((nb, S, F4), bf16)
    return pl.pallas_call(
        body, name=name, grid=(S // tm, nb),
        in_specs=[row, row, _full_spec((1, D)), pl.BlockSpec((None, F4, D), lambda i, k: (k, 0, 0)), wspec, wspec, cspec, cspec],
        out_specs=[cspec, cspec, row, _full_spec((1, D))],
        out_shape=[chunk, chunk, jax.ShapeDtypeStruct((S, D), f32), jax.ShapeDtypeStruct((1, D), f32)],
        scratch_shapes=[pltpu.VMEM((tm, D), f32)],
        compiler_params=_cparams(("arbitrary", "arbitrary")),
    )(dh, h1, _row2(nf_w), wd, wg, wu, G, U)


def _rms_bwd_add(dres, du, h, w, name):
    D = h.shape[1]

    def fn(dr, du_, h_, w_):
        dx, dw = _rms_bwd(du_, h_, w_, RMS_EPS)
        return dr + dx, dw

    return _rowcall(fn, name=name, rows=[dres, du, h], fulls=[_row2(w)], out_rows=[(D, f32)], out_accs=[(1, D)])


def _ple_fwd(h, p_i, wp, pn, gn, wgate, name):
    D = h.shape[1]

    def fn(h_, p_, wp_, pn_, gn_, wg_):
        pe = _dot(p_.astype(bf16), wp_, NN)
        e = _rms(pe, pn_, RMS_EPS)
        q = _rms(h_, gn_, RMS_EPS)
        gate = _sigmoid(_dot(q.astype(bf16), wg_, NN))
        return h_ + gate * e, pe, gate

    return _rowcall(fn, name=name, rows=[h, p_i], fulls=[wp, _row2(pn), _row2(gn), wgate],
                    out_rows=[(D, f32), (D, f32), (D, f32)], tm=256)


def _ple_bwd(dh3, h, pe, gate, pn, gn, wgate, name):
    D = h.shape[1]

    def fn(d_, h_, pe_, gate_, pn_, gn_, wg_):
        e = _rms(pe_, pn_, RMS_EPS)
        dzg = d_ * e * gate_ * (1.0 - gate_)
        dq = _dot(dzg.astype(bf16), wg_, NT)
        dxq, dgn = _rms_bwd(dq, h_, gn_, RMS_EPS)
        dpe, dpn = _rms_bwd(d_ * gate_, pe_, pn_, RMS_EPS)
        return d_ + dxq, dzg, dpe, _rms(h_, gn_, RMS_EPS), dpn, dgn

    return _rowcall(fn, name=name, rows=[dh3, h, pe, gate], fulls=[_row2(pn), _row2(gn), wgate],
                    out_rows=[(D, f32), (D, bf16), (D, bf16), (D, bf16)], out_accs=[(1, D), (1, D)], tm=256)


def _loss_head(h, target, fn_w, name):
    D = h.shape[1]

    def fn(h_, t_, w_):
        diff = _rms(h_, w_, RMS_EPS) - t_
        loss = 0.5 * jnp.sum(jnp.mean(diff * diff, axis=-1, keepdims=True), axis=0, keepdims=True)
        dh, dw = _rms_bwd(diff * (1.0 / D), h_, w_, RMS_EPS)
        return dh, jnp.broadcast_to(loss, (1, LANES)), dw

    return _rowcall(fn, name=name, rows=[h, target], fulls=[_row2(fn_w)], out_rows=[(D, f32)], out_accs=[(1, LANES), (1, D)])


def _adamw(w, m, v, g, name):
    R, C = w.shape
    tr = R
    while tr * C > 256 * 1024 and tr % (2 * SUBLANES) == 0:
        tr //= 2

    def body(w_ref, m_ref, v_ref, g_ref, d_ref, mo_ref, vo_ref):
        g = g_ref[...]
        mn = ADAM_B1 * m_ref[...] + (1.0 - ADAM_B1) * g
        vn = ADAM_B2 * v_ref[...] + (1.0 - ADAM_B2) * (g * g)
        m_hat = mn / (1.0 - ADAM_B1 ** ADAM_STEP)
        v_hat = vn / (1.0 - ADAM_B2 ** ADAM_STEP)
        d_ref[...] = -ADAM_LR * (m_hat / (jnp.sqrt(v_hat) + ADAM_EPS) + ADAM_WD * w_ref[...])
        mo_ref[...] = mn
        vo_ref[...] = vn

    spec = pl.BlockSpec((tr, C), lambda i: (i, 0))
    return pl.pallas_call(
        body, name=name, grid=(R // tr,), in_specs=[spec] * 4,
        out_specs=[spec] * 3, out_shape=[jax.ShapeDtypeStruct((R, C), f32)] * 3,
        compiler_params=_cparams(("parallel",)),
    )(w, m, v, g)


def _expand_onehot(n, per):
    lane = lax.broadcasted_iota(jnp.int32, (n, n * per), 1)
    row = lax.broadcasted_iota(jnp.int32, (n, n * per), 0)
    return (lane // per == row).astype(f32)


def _ssd_layer_fwd(h, nm_w, W, t):
    H = W["dt_bias"].shape[0]
    inner = H * HEADDIM
    G = (W["conv_b"].shape[0] - inner) // (2 * STATE)
    hn = _rms_fwd(h, nm_w, f"rms_mix_{t}")
    z = _mm(hn, W["wz"], name=f"ssd_z_{t}")
    xpre = _mm(hn, W["wxbc"], name=f"ssd_xbc_{t}")
    dt_pre = _mm(hn, W["wdt"], name=f"ssd_dt_{t}")
    c, xbc = _conv_fwd(xpre, W["conv_w"], W["conv_b"], f"ssd_conv_{t}")
    _, dtx = _dt_fwd(dt_pre, W["dt_bias"], _expand_onehot(H, HEADDIM), f"ssd_dtx_{t}")
    a = -jnp.exp(W["a_log"])
    ax = _row2(jnp.repeat(a, HEADDIM))
    dx = _row2(jnp.repeat(W["d"], HEADDIM))
    y, states = _ssd_fwd(xbc, dtx, ax, dx, G, f"ssd_scan_{t}")
    yn = _gnorm_fwd(y, z, W["norm_w"], G, f"ssd_gnorm_{t}")
    h1 = _mm(yn, W["wout"], res=h, name=f"ssd_out_{t}")
    return h1, (h, hn, z, xpre, dt_pre, c, xbc, dtx, a, ax, dx, y, states, yn)


def _ssd_layer_bwd(dh1, saved, nm_w, W, t):
    h, hn, z, xpre, dt_pre, c, xbc, dtx, a, ax, dx, y, states, yn = saved
    H = W["dt_bias"].shape[0]
    inner = H * HEADDIM
    G = (W["conv_b"].shape[0] - inner) // (2 * STATE)
    dyn = _mm(dh1, W["wout"], mode="nt", name=f"ssd_out_dx_{t}")
    g_wout = _mm(yn, dh1, mode="tn", out_dtype=bf16, name=f"ssd_out_dw_{t}")
    dy, dz, g_normw = _gnorm_bwd(dyn, y, z, W["norm_w"], G, f"ssd_gnorm_bwd_{t}")
    dxbc, ddt, dax, ddx = _ssd_bwd(dy, y, xbc, dtx, ax, dx, states, _expand_onehot(H, HEADDIM).T, G, f"ssd_scan_bwd_{t}")
    dc, g_convw8, g_convb = _conv_bwd_dc(dxbc, c, xpre, f"ssd_conv_bwd_dc_{t}")
    dxpre = _conv_bwd_dx(dc, W["conv_w"], f"ssd_conv_bwd_dx_{t}")
    ddt_pre, g_dtb = _dt_bwd(ddt, dt_pre, W["dt_bias"], f"ssd_dt_bwd_{t}")
    g_wz = _mm(hn, dz, mode="tn", out_dtype=bf16, name=f"ssd_z_dw_{t}")
    g_wxbc = _mm(hn, dxpre, mode="tn", out_dtype=bf16, name=f"ssd_xbc_dw_{t}")
    g_wdt = _mm(hn, ddt_pre, mode="tn", out_dtype=bf16, name=f"ssd_dt_dw_{t}")
    dhn = _mm(dz, W["wz"], mode="nt", name=f"ssd_z_dx_{t}")
    dhn = _mm(dxpre, W["wxbc"], mode="nt", res=dhn, name=f"ssd_xbc_dx_{t}")
    dhn = _mm(ddt_pre, W["wdt"], mode="nt", res=dhn, name=f"ssd_dt_dx_{t}")
    dh, g_nm = _rms_bwd_add(dh1, dhn, h, nm_w, f"rms_mix_bwd_{t}")
    grads = dict(
        w_in=jnp.concatenate([g_wz, g_wxbc, g_wdt], axis=1), wout=g_wout,
        conv_w=g_convw8[:CONV_K], conv_b=g_convb[0], dt_bias=g_dtb[0],
        a_log=dax[0].reshape(H, HEADDIM)[:, 0] * a, d=jnp.sum(ddx[0].reshape(H, HEADDIM), axis=1),
        norm_w=g_normw[0], norm_mix=g_nm[0])
    return dh, grads


def _gmlp_layer_fwd(h, nm_w, W, t):
    NG, L, _ = W["w_s"].shape
    I = W["ln_w"].shape[0]
    hn = _rms_fwd(h, nm_w, f"rms_mix_{t}")
    pre = _mm(hn, W["win"], name=f"gmlp_in_{t}")
    bsx = jnp.repeat(W["b_s"].T, I // NG, axis=1)
    o = _gmlp_mid_fwd(pre, W["b_in"], W["ln_w"], W["ln_b"], W["w_s"], bsx, f"gmlp_mid_{t}")
    h1 = _mm(o, W["wout"], res=h, name=f"gmlp_out_{t}")
    return h1, (h, hn, pre, bsx, o)


def _gmlp_layer_bwd(dh1, saved, nm_w, W, t):
    h, hn, pre, bsx, o = saved
    NG = W["w_s"].shape[0]
    I = W["ln_w"].shape[0]
    do = _mm(dh1, W["wout"], mode="nt", name=f"gmlp_out_dx_{t}")
    g_wout = _mm(o, dh1, mode="tn", out_dtype=bf16, name=f"gmlp_out_dw_{t}")
    dpre, g_bin, g_lnw, g_lnb, g_ws, dbs = _gmlp_mid_bwd(do, pre, W["b_in"], W["ln_w"], W["ln_b"], W["w_s"], bsx, f"gmlp_mid_bwd_{t}")
    g_bs = _lane_group_sum(dbs, _expand_onehot(NG, I // NG).T, f"gmlp_bs_{t}").T
    g_win = _mm(hn, dpre, mode="tn", out_dtype=bf16, name=f"gmlp_in_dw_{t}")
    dhn = _mm(dpre, W["win"], mode="nt", name=f"gmlp_in_dx_{t}")
    dh, g_nm = _rms_bwd_add(dh1, dhn, h, nm_w, f"rms_mix_bwd_{t}")
    grads = dict(win=g_win, wout=g_wout, b_in=g_bin[0], ln_w=g_lnw[0], ln_b=g_lnb[0], w_s=g_ws, b_s=g_bs, norm_mix=g_nm[0])
    return dh, grads


def _ffn_fwd(h1, nf_w, W, t):
    h2, u, Gm, Um, A = _ffn_fwd_fused(h1, nf_w, W["wg"], W["wu"], W["wd"], f"ffn_fwd_{t}")
    return h2, (h1, u, Gm, Um, A)


def _ffn_bwd(dh2, saved, nf_w, W, t):
    h1, u, Gm, Um, A = saved
    dG, dU, dh1, g_nf = _ffn_bwd_fused(dh2, h1, nf_w, W["wd"], W["wg"], W["wu"], Gm, Um, f"ffn_bwd_{t}")
    g_wd = _mm(A, dh2, mode="tn", out_dtype=bf16, name=f"ffn_down_dw_{t}")
    g_wg = _mm(u, dG, mode="tn", out_dtype=bf16, name=f"ffn_gate_dw_{t}")
    g_wu = _mm(u, dU, mode="tn", out_dtype=bf16, name=f"ffn_up_dw_{t}")
    return dh1, dict(wg=g_wg, wu=g_wu, wd=g_wd, norm_ffn=g_nf[0])


def _local_step(x, p, target, W):
    depth = p.shape[0]
    h = x
    saved = []
    for i in range(depth):
        j = i // 2
        if i % 2 == 0:
            h1, s_mix = _ssd_layer_fwd(h, W["norm_mix"][i], W["ssd"][j], i)
        else:
            h1, s_mix = _gmlp_layer_fwd(h, W["norm_mix"][i], W["gmlp"][j], i)
        h2, s_ffn = _ffn_fwd(h1, W["norm_ffn"][i], W["ffn"][i], i)
        P = W["ple"][i]
        h3, pe, gate = _ple_fwd(h2, p[i], P["wp"], P["pn"], P["gn"], P["wgate"], f"ple_{i}")
        saved.append((s_mix, s_ffn, (h2, pe, gate)))
        h = h3
    dh, loss, g_fn = _loss_head(h, target, W["final_norm"], "loss_head")
    grads = dict(final_norm=g_fn[0], ssd=[None] * len(W["ssd"]), gmlp=[None] * len(W["gmlp"]), ffn=[None] * depth, ple=[None] * depth)
    for i in reversed(range(depth)):
        j = i // 2
        s_mix, s_ffn, (h2, pe, gate) = saved[i]
        P = W["ple"][i]
        dh, dzg, dpe, q, g_pn, g_gn = _ple_bwd(dh, h2, pe, gate, P["pn"], P["gn"], P["wgate"], f"ple_bwd_{i}")
        grads["ple"][i] = dict(
            wgate=_mm(q, dzg, mode="tn", out_dtype=bf16, name=f"ple_gate_dw_{i}"),
            wp=_mm(p[i], dpe, mode="tn", out_dtype=bf16, name=f"ple_proj_dw_{i}"), pn=g_pn[0], gn=g_gn[0])
        dh, grads["ffn"][i] = _ffn_bwd(dh, s_ffn, W["norm_ffn"][i], W["ffn"][i], i)
        if i % 2 == 0:
            dh, grads["ssd"][j] = _ssd_layer_bwd(dh, s_mix, W["norm_mix"][i], W["ssd"][j], i)
        else:
            dh, grads["gmlp"][j] = _gmlp_layer_bwd(dh, s_mix, W["norm_mix"][i], W["gmlp"][j], i)
    return loss[0, 0], dh, grads


def _flip(v, f):
    return 1 - v if f else v


_ANY = pl.BlockSpec(memory_space=pl.ANY)


def _gather_chips(arrs):
    flat = [(t, l) for t, a in enumerate(arrs) for l in range(a.shape[0])]
    n_in, n_out = len(arrs), len(flat)
    flips = [(1, 0), (0, 1), (1, 1)]
    nf = len(flips)

    def body(*refs):
        ins, outs = refs[:n_in], refs[n_in + n_out:n_in + 2 * n_out]
        send_sems, recv_sems, fsend_sems, frecv_sems = refs[n_in + 2 * n_out:]
        x, y, c = lax.axis_index("x"), lax.axis_index("y"), lax.axis_index("c")
        mychip = 2 * x + y

        def rows(o, h):
            r2 = arrs[flat[o][0]].shape[1] // 2
            return pl.ds(h * r2, r2)

        def ici(o, j, slot):
            t, l = flat[o]
            fx, fy = flips[j]
            return pltpu.make_async_remote_copy(
                src_ref=ins[t].at[l, rows(o, c)], dst_ref=outs[o].at[slot, rows(o, c)], send_sem=send_sems.at[nf * o + j],
                recv_sem=recv_sems.at[nf * o + j], device_id=(_flip(x, fx), _flip(y, fy), c), device_id_type=MESH)

        def forward(o, j, h):
            fx, fy = flips[j]
            part = outs[o].at[2 * _flip(x, fx) + _flip(y, fy), rows(o, h)]
            return pltpu.make_async_remote_copy(
                src_ref=part, dst_ref=part, send_sem=fsend_sems.at[nf * o + j], recv_sem=frecv_sems.at[nf * o + j],
                device_id=(x, y, 1 - c), device_id_type=MESH)

        sends = [ici(o, j, mychip) for o in range(n_out) for j in range(nf)]
        for cp in sends:
            cp.start()
        forwards = []
        for o in range(n_out):
            for j, (fx, fy) in enumerate(flips):
                ici(o, j, 2 * _flip(x, fx) + _flip(y, fy)).wait_recv()
                forwards.append(forward(o, j, c))
                forwards[-1].start()
        for o in range(n_out):
            for j in range(nf):
                forward(o, j, 1 - c).wait_recv()
        for cp in sends + forwards:
            cp.wait_send()

    inits = [jnp.broadcast_to(arrs[t][l][None], (N_CHIPS,) + arrs[t].shape[1:]) for t, l in flat]
    dma = pltpu.SemaphoreType.DMA
    outs = pl.pallas_call(
        body, name="gather_weights", in_specs=[_ANY] * (n_in + n_out), out_specs=[_ANY] * n_out,
        out_shape=[jax.ShapeDtypeStruct(b.shape, b.dtype) for b in inits],
        input_output_aliases={n_in + o: o for o in range(n_out)},
        scratch_shapes=[dma((nf * n_out,)), dma((nf * n_out,)), dma((nf * n_out,)), dma((nf * n_out,))],
    )(*arrs, *inits)
    res = [[] for _ in arrs]
    for o, (t, _) in enumerate(flat):
        res[t].append(outs[o])
    return res


def _half_struct(a, lead):
    return jax.ShapeDtypeStruct(lead + (a.shape[-2] // 2, a.shape[-1]), a.dtype)


def _sibling_split(tensors, rep):
    n = len(tensors)

    def body(*refs):
        ins, rep_ref = refs[:n], refs[n]
        got, rep_got = refs[n + 1:2 * n + 1], refs[2 * n + 1]
        send_sems, recv_sems = refs[2 * n + 2:]
        x, y, c = lax.axis_index("x"), lax.axis_index("y"), lax.axis_index("c")

        def swap(i):
            if i < n:
                r2 = tensors[i].shape[1] // 2
                src, dst = ins[i].at[:, pl.ds((1 - c) * r2, r2)], got[i]
            else:
                src, dst = rep_ref, rep_got
            return pltpu.make_async_remote_copy(src_ref=src, dst_ref=dst, send_sem=send_sems.at[i], recv_sem=recv_sems.at[i],
                                                device_id=(x, y, 1 - c), device_id_type=MESH)

        copies = [swap(i) for i in range(n + 1)]
        for cp in copies:
            cp.start()
        for cp in copies:
            cp.wait()

    dma = pltpu.SemaphoreType.DMA
    outs = pl.pallas_call(
        body, name="grads_sibling_split", in_specs=[_ANY] * (n + 1), out_specs=[_ANY] * (n + 1),
        out_shape=[_half_struct(t, (N_CHIPS,)) for t in tensors] + [jax.ShapeDtypeStruct(rep.shape, rep.dtype)],
        scratch_shapes=[dma((n + 1,)), dma((n + 1,))],
    )(*tensors, rep)
    return outs[:n], outs[n]


def _chip_exchange(parts, rep):
    n = len(parts)
    flips = [(1, 0), (0, 1), (1, 1)]
    nf = len(flips)

    def body(*refs):
        ins = refs[:n + 1]
        outs = refs[n + 1:2 * n + 2]
        send_sems, recv_sems = refs[2 * n + 2:]
        x, y, c = lax.axis_index("x"), lax.axis_index("y"), lax.axis_index("c")
        mychip = 2 * x + y

        def copy(i, j, slot):
            px, py = _flip(x, flips[j][0]), _flip(y, flips[j][1])
            return pltpu.make_async_remote_copy(
                src_ref=ins[i].at[2 * px + py] if i < n else ins[i], dst_ref=outs[i].at[slot], send_sem=send_sems.at[nf * i + j],
                recv_sem=recv_sems.at[nf * i + j], device_id=(px, py, c), device_id_type=MESH)

        sends = [copy(i, j, mychip) for i in range(n + 1) for j in range(nf)]
        for cp in sends:
            cp.start()
        for i in range(n + 1):
            for j, (fx, fy) in enumerate(flips):
                copy(i, j, 2 * _flip(x, fx) + _flip(y, fy)).wait_recv()
        for cp in sends:
            cp.wait_send()

    dma = pltpu.SemaphoreType.DMA
    outs = pl.pallas_call(
        body, name="grads_chip_exchange", in_specs=[_ANY] * (n + 1), out_specs=[_ANY] * (n + 1),
        out_shape=[jax.ShapeDtypeStruct(t.shape, t.dtype) for t in parts] + [jax.ShapeDtypeStruct((N_CHIPS,) + rep.shape, rep.dtype)],
        scratch_shapes=[dma((nf * (n + 1),)), dma((nf * (n + 1),))],
    )(*parts, rep)
    return outs[:n], outs[n]


def _sibling_join(bufs):
    flat = [(gi, l) for gi, b in enumerate(bufs) for l in range(b.shape[0])]
    n, n_buf = len(flat), len(bufs)

    def body(*refs):
        outs = refs[n_buf:2 * n_buf]
        send_sems, recv_sems = refs[2 * n_buf:]
        x, y, c = lax.axis_index("x"), lax.axis_index("y"), lax.axis_index("c")

        def push(i, h):
            gi, l = flat[i]
            r2 = bufs[gi].shape[1] // 2
            part = outs[gi].at[l, pl.ds(h * r2, r2)]
            return pltpu.make_async_remote_copy(src_ref=part, dst_ref=part, send_sem=send_sems.at[i], recv_sem=recv_sems.at[i],
                                                device_id=(x, y, 1 - c), device_id_type=MESH)

        sends = [push(i, c) for i in range(n)]
        for cp in sends:
            cp.start()
        for i in range(n):
            push(i, 1 - c).wait_recv()
        for cp in sends:
            cp.wait_send()

    dma = pltpu.SemaphoreType.DMA
    return pl.pallas_call(
        body, name="grads_sibling_join", in_specs=[_ANY] * n_buf, out_specs=[_ANY] * n_buf,
        out_shape=[jax.ShapeDtypeStruct(b.shape, b.dtype) for b in bufs],
        input_output_aliases={i: i for i in range(n_buf)},
        scratch_shapes=[dma((n,)), dma((n,))],
    )(*bufs)


def _sum_tile_rows(rows, cols):
    tr = rows
    while tr * cols > 256 * 1024 and tr % (4 * SUBLANES) == 0:
        tr //= 2
    return tr


def _pair_sum(full, got, place, name):
    nch, R, C = full.shape
    R2 = R // 2
    tr = _sum_tile_rows(R2, C)
    nb = R2 // tr

    def body(place_ref, a_ref, b_ref, o_ref):
        o_ref[...] = (a_ref[...].astype(f32) + b_ref[...].astype(f32)).astype(o_ref.dtype)

    return pl.pallas_call(
        body, name=name, out_shape=jax.ShapeDtypeStruct((nch, R2, C), full.dtype),
        grid_spec=pltpu.PrefetchScalarGridSpec(
            num_scalar_prefetch=1, grid=(nch, nb),
            in_specs=[pl.BlockSpec((None, tr, C), lambda k, i, pr: (k, pr[1] * nb + i, 0)),
                      pl.BlockSpec((None, tr, C), lambda k, i, pr: (k, i, 0))],
            out_specs=pl.BlockSpec((None, tr, C), lambda k, i, pr: (k, i, 0))),
        compiler_params=_cparams(("parallel", "parallel")),
    )(place, full, got)


def _add2(a, b, name):
    R, C = a.shape
    tr = _sum_tile_rows(R, C)

    def body(a_ref, b_ref, o_ref):
        o_ref[...] = a_ref[...] + b_ref[...]

    spec = pl.BlockSpec((tr, C), lambda i: (i, 0))
    return pl.pallas_call(body, name=name, grid=(R // tr,), in_specs=[spec, spec], out_specs=spec,
                          out_shape=jax.ShapeDtypeStruct((R, C), f32), compiler_params=_cparams(("parallel",)))(a, b)


def _chip_sum(landed, mine, place, name, into=None, layer=0, layers=1):
    nch, R2, C = landed.shape
    tr = _sum_tile_rows(R2, C)
    nb = R2 // tr

    def body(*refs):
        place_ref, l_ref, m_ref, o_ref = refs[0], refs[1], refs[2], refs[-1]
        s = jnp.where(place_ref[0] == 0, m_ref[...].astype(f32), l_ref[0].astype(f32))
        for k in range(1, nch):
            s = s + jnp.where(place_ref[0] == k, m_ref[...].astype(f32), l_ref[k].astype(f32))
        o_ref[...] = s

    in_specs = [pl.BlockSpec((nch, tr, C), lambda i, pr: (0, i, 0)),
                pl.BlockSpec((None, tr, C), lambda i, pr: (pr[0], i, 0))]
    args = [place, landed, mine]
    if into is not None:
        in_specs.append(_ANY)
        args.append(into)
    return pl.pallas_call(
        body, name=name, out_shape=jax.ShapeDtypeStruct((layers, 2 * R2, C), f32),
        grid_spec=pltpu.PrefetchScalarGridSpec(
            num_scalar_prefetch=1, grid=(nb,), in_specs=in_specs,
            out_specs=pl.BlockSpec((None, tr, C), lambda i, pr: (layer, pr[1] * nb + i, 0))),
        input_output_aliases={3: 0} if into is not None else {},
        compiler_params=_cparams(("parallel",)),
    )(*args)


def _chip_sum_rep(landed, mine, place, name):
    nch, R, C = landed.shape
    tr = _sum_tile_rows(R, C)

    def body(place_ref, l_ref, m_ref, o_ref):
        s = jnp.where(place_ref[0] == 0, m_ref[...], l_ref[0])
        for k in range(1, nch):
            s = s + jnp.where(place_ref[0] == k, m_ref[...], l_ref[k])
        o_ref[...] = s

    return pl.pallas_call(
        body, name=name, out_shape=jax.ShapeDtypeStruct((R, C), f32),
        grid_spec=pltpu.PrefetchScalarGridSpec(
            num_scalar_prefetch=1, grid=(R // tr,),
            in_specs=[pl.BlockSpec((nch, tr, C), lambda i, pr: (0, i, 0)), pl.BlockSpec((tr, C), lambda i, pr: (i, 0))],
            out_specs=pl.BlockSpec((tr, C), lambda i, pr: (i, 0))),
        compiler_params=_cparams(("parallel",)),
    )(place, landed, mine)


PACK_COLS = 1024
PACK_ROW_MULTIPLE = 64

BIG = ("ssd_w_in", "ssd_w_out", "gmlp_w_in", "gmlp_w_out", "ffn_w_gate", "ffn_w_up", "ffn_w_down", "ple_w_proj", "ple_w_gate")
SMALL_SHARDED = ("ssd_conv_w", "gmlp_b_in", "gmlp_ln_w", "gmlp_ln_b")
REPLICATED = ("norm_mix", "norm_ffn", "ssd_conv_b", "ssd_dt_bias", "ssd_a_log", "ssd_d", "ssd_norm_w", "gmlp_w_s", "gmlp_b_s",
              "ple_norm", "ple_gate_norm", "final_norm")
WEIGHTS = ("norm_mix", "norm_ffn", "ssd_w_in", "ssd_conv_w", "ssd_conv_b", "ssd_dt_bias", "ssd_a_log", "ssd_d", "ssd_norm_w", "ssd_w_out",
           "gmlp_w_in", "gmlp_b_in", "gmlp_ln_w", "gmlp_ln_b", "gmlp_w_s", "gmlp_b_s", "gmlp_w_out", "ffn_w_gate", "ffn_w_up",
           "ffn_w_down", "ple_w_proj", "ple_norm", "ple_gate_norm", "ple_w_gate", "final_norm")
COLUMN_SHARDED = ("ssd_w_in", "gmlp_w_in", "ple_w_proj")


def _pack(arrs):
    flat = jnp.concatenate([a.reshape(-1).astype(f32) for a in arrs])
    per = PACK_COLS * PACK_ROW_MULTIPLE
    n = -(-flat.shape[0] // per) * per
    return jnp.pad(flat, (0, n - flat.shape[0])).reshape(-1, PACK_COLS)


def _unpack(buf, shapes):
    flat = buf.reshape(-1)
    out, o = [], 0
    for s in shapes:
        n = math.prod(s)
        out.append(flat[o:o + n].reshape(s))
        o += n
    return out


def _chip_major(g):
    r, c4 = g.shape
    return g.reshape(r, N_CHIPS, c4 // N_CHIPS).transpose(1, 0, 2)


def _from_chip_major(g):
    k, r, c = g.shape
    return g.transpose(1, 0, 2).reshape(r, k * c)


def _adamw_nd(w, m, v, g, name):
    shp = w.shape
    two = lambda a: a.reshape(-1, shp[-1])
    return [o.reshape(shp) for o in _adamw(two(w), two(m), two(v), two(g), name)]


def kernel(x, p, norm_mix, norm_ffn, ssd_w_in, ssd_conv_w, ssd_conv_b, ssd_dt_bias, ssd_a_log, ssd_d, ssd_norm_w, ssd_w_out, gmlp_w_in, gmlp_b_in, gmlp_ln_w, gmlp_ln_b, gmlp_w_s, gmlp_b_s, gmlp_w_out, ffn_w_gate, ffn_w_up, ffn_w_down, ple_w_proj, ple_norm, ple_gate_norm, ple_w_gate, final_norm, loss_target, m_norm_mix, m_norm_ffn, m_ssd_w_in, m_ssd_conv_w, m_ssd_conv_b, m_ssd_dt_bias, m_ssd_a_log, m_ssd_d, m_ssd_norm_w, m_ssd_w_out, m_gmlp_w_in, m_gmlp_b_in, m_gmlp_ln_w, m_gmlp_ln_b, m_gmlp_w_s, m_gmlp_b_s, m_gmlp_w_out, m_ffn_w_gate, m_ffn_w_up, m_ffn_w_down, m_ple_w_proj, m_ple_norm, m_ple_gate_norm, m_ple_w_gate, m_final_norm, v_norm_mix, v_norm_ffn, v_ssd_w_in, v_ssd_conv_w, v_ssd_conv_b, v_ssd_dt_bias, v_ssd_a_log, v_ssd_d, v_ssd_norm_w, v_ssd_w_out, v_gmlp_w_in, v_gmlp_b_in, v_gmlp_ln_w, v_gmlp_ln_b, v_gmlp_w_s, v_gmlp_b_s, v_gmlp_w_out, v_ffn_w_gate, v_ffn_w_up, v_ffn_w_down, v_ple_w_proj, v_ple_norm, v_ple_gate_norm, v_ple_w_gate, v_final_norm):
    given = dict(locals())
    w = {n: given[n] for n in WEIGHTS}
    mom = {n: given["m_" + n] for n in WEIGHTS}
    var = {n: given["v_" + n] for n in WEIGHTS}
    depth = p.shape[0]
    n_ssd, n_gmlp = ssd_w_in.shape[0], gmlp_w_in.shape[0]
    inner = ssd_dt_bias.shape[1] * HEADDIM
    conv_dim = ssd_conv_b.shape[1]

    small_shapes = [w[n].shape for n in SMALL_SHARDED]
    gathered = _gather_chips([w[n].astype(bf16) for n in BIG] + [_pack([w[n] for n in SMALL_SHARDED])[None]])
    gw = dict(zip(BIG, gathered[:-1]))
    small_by_chip = [_unpack(gathered[-1][0][k], small_shapes) for k in range(N_CHIPS)]
    small_full = {n: jnp.concatenate([small_by_chip[k][i] for k in range(N_CHIPS)], axis=-1) for i, n in enumerate(SMALL_SHARDED)}

    W = dict(norm_mix=norm_mix, norm_ffn=norm_ffn, final_norm=final_norm, ssd=[], gmlp=[], ffn=[], ple=[])
    for j in range(n_ssd):
        w_in = _from_chip_major(gw["ssd_w_in"][j])
        W["ssd"].append(dict(
            wz=w_in[:, :inner], wxbc=w_in[:, inner:inner + conv_dim], wdt=w_in[:, inner + conv_dim:],
            conv_w=small_full["ssd_conv_w"][j], conv_b=ssd_conv_b[j], dt_bias=ssd_dt_bias[j], a_log=ssd_a_log[j], d=ssd_d[j],
            norm_w=ssd_norm_w[j], wout=gw["ssd_w_out"][j].reshape(-1, gw["ssd_w_out"][j].shape[-1])))
    for j in range(n_gmlp):
        W["gmlp"].append(dict(
            win=_from_chip_major(gw["gmlp_w_in"][j]), b_in=small_full["gmlp_b_in"][j], ln_w=small_full["gmlp_ln_w"][j],
            ln_b=small_full["gmlp_ln_b"][j], w_s=gmlp_w_s[j], b_s=gmlp_b_s[j],
            wout=gw["gmlp_w_out"][j].reshape(-1, gw["gmlp_w_out"][j].shape[-1])))
    for i in range(depth):
        W["ffn"].append(dict(wg=gw["ffn_w_gate"][i], wu=gw["ffn_w_up"][i], wd=gw["ffn_w_down"][i]))
        W["ple"].append(dict(wp=_from_chip_major(gw["ple_w_proj"][i]), pn=ple_norm[i], gn=ple_gate_norm[i],
                             wgate=gw["ple_w_gate"][i].reshape(-1, gw["ple_w_gate"][i].shape[-1])))

    loss_part, grad_x, g = _local_step(x[0], p[:, 0], loss_target[0], W)
    loss = lax.psum(loss_part, ("x", "y", "c"))

    rows4 = lambda a: a.reshape((N_CHIPS, a.shape[0] // N_CHIPS) + a.shape[1:])
    big = dict(
        ssd_w_in=[_chip_major(s["w_in"]) for s in g["ssd"]], ssd_w_out=[rows4(s["wout"]) for s in g["ssd"]],
        gmlp_w_in=[_chip_major(s["win"]) for s in g["gmlp"]], gmlp_w_out=[rows4(s["wout"]) for s in g["gmlp"]],
        ffn_w_gate=[s["wg"] for s in g["ffn"]], ffn_w_up=[s["wu"] for s in g["ffn"]], ffn_w_down=[s["wd"] for s in g["ffn"]],
        ple_w_proj=[_chip_major(s["wp"]) for s in g["ple"]], ple_w_gate=[rows4(s["wgate"]) for s in g["ple"]])
    small_g = dict(ssd_conv_w=jnp.stack([s["conv_w"] for s in g["ssd"]]), gmlp_b_in=jnp.stack([s["b_in"] for s in g["gmlp"]]),
                   gmlp_ln_w=jnp.stack([s["ln_w"] for s in g["gmlp"]]), gmlp_ln_b=jnp.stack([s["ln_b"] for s in g["gmlp"]]))
    cut = lambda a, k: a[..., k * (a.shape[-1] // N_CHIPS):(k + 1) * (a.shape[-1] // N_CHIPS)]
    small_packed = jnp.stack([_pack([cut(small_g[n], k) for n in SMALL_SHARDED]) for k in range(N_CHIPS)])
    mix = [g["ssd"][i // 2]["norm_mix"] if i % 2 == 0 else g["gmlp"][i // 2]["norm_mix"] for i in range(depth)]
    rep_g = dict(
        norm_mix=jnp.stack(mix), norm_ffn=jnp.stack([s["norm_ffn"] for s in g["ffn"]]),
        ssd_conv_b=jnp.stack([s["conv_b"] for s in g["ssd"]]), ssd_dt_bias=jnp.stack([s["dt_bias"] for s in g["ssd"]]),
        ssd_a_log=jnp.stack([s["a_log"] for s in g["ssd"]]), ssd_d=jnp.stack([s["d"] for s in g["ssd"]]),
        ssd_norm_w=jnp.stack([s["norm_w"] for s in g["ssd"]]), gmlp_w_s=jnp.stack([s["w_s"] for s in g["gmlp"]]),
        gmlp_b_s=jnp.stack([s["b_s"] for s in g["gmlp"]]), ple_norm=jnp.stack([s["pn"] for s in g["ple"]]),
        ple_gate_norm=jnp.stack([s["gn"] for s in g["ple"]]), final_norm=g["final_norm"])
    rep_packed = _pack([rep_g[n] for n in REPLICATED])
    place = jnp.stack([2 * lax.axis_index("x") + lax.axis_index("y"), lax.axis_index("c")]).astype(jnp.int32)
    tensors = [a for n in BIG for a in big[n]] + [small_packed]
    got, rep_got = _sibling_split(tensors, rep_packed)
    pair_sums = [_pair_sum(a, b, place, f"grads_pair_sum_{i}") for i, (a, b) in enumerate(zip(tensors, got))]
    rep_pair = _add2(rep_packed, rep_got, "grads_pair_sum_rep")
    chip_parts, rep_parts = _chip_exchange(pair_sums, rep_pair)
    rep_total = _chip_sum_rep(rep_parts, rep_pair, place, "grads_chip_sum_rep")
    bufs, o = [], 0
    for n in BIG + ("small",):
        layers = len(big[n]) if n != "small" else 1
        buf = None
        for l in range(layers):
            buf = _chip_sum(chip_parts[o], pair_sums[o], place, f"grads_chip_sum_{o}", into=buf, layer=l, layers=layers)
            o += 1
        bufs.append(buf)
    reduced = _sibling_join(bufs)

    res = {}
    for n, gsum in zip(BIG, reduced):
        res[n] = [gsum] + _adamw_nd(w[n], mom[n], var[n], gsum, "adamw_" + n)
    for names, gsum, tag in ((SMALL_SHARDED, reduced[-1][0], "adamw_small_sharded"), (REPLICATED, rep_total, "adamw_replicated")):
        packs = [gsum] + list(_adamw(_pack([w[n] for n in names]), _pack([mom[n] for n in names]), _pack([var[n] for n in names]), gsum, tag))
        per_kind = [_unpack(pk, [w[n].shape for n in names]) for pk in packs]
        for i, n in enumerate(names):
            res[n] = [per_kind[k][i] for k in range(4)]
    return (loss, grad_x[None], *[res[n][0] for n in WEIGHTS], *[res[n][1] for n in WEIGHTS],
            *[res[n][2] for n in WEIGHTS], *[res[n][3] for n in WEIGHTS])
```

```python
import functools
import math

import jax
import jax.numpy as jnp
from jax import lax
from jax.experimental import pallas as pl
from jax.experimental.pallas import tpu as pltpu

f32 = jnp.float32
bf16 = jnp.bfloat16
HI = lax.Precision.HIGHEST

LANES = 128
SUBLANES = 8
VMEM_LIMIT_BYTES = 56 * 1024 * 1024

HEADDIM = 64
STATE = 128
CHUNK = 128
CONV_K = 4
RMS_EPS = 1e-6
LN_EPS = 1e-5
ADAM_LR = 0.001
ADAM_B1 = 0.9
ADAM_B2 = 0.999
ADAM_EPS = 1e-08
ADAM_WD = 0.01
ADAM_STEP = 10

N_CHIPS = 4
N_DEV = 8
MESH = pl.DeviceIdType.MESH


def _cparams(sem):
    return pltpu.CompilerParams(dimension_semantics=sem, vmem_limit_bytes=VMEM_LIMIT_BYTES)


def _tile(n, want):
    if n <= want:
        return n
    t = want
    while n % t:
        t //= 2
    return t


def _row_spec(tm, c):
    return pl.BlockSpec((tm, c), lambda i: (i, 0))


def _full_spec(shape):
    nd = len(shape)
    return pl.BlockSpec(tuple(shape), lambda *_: (0,) * nd)


def _sigmoid(x):
    return 1.0 / (1.0 + jnp.exp(-x))


def _silu(x):
    return x * _sigmoid(x)


def _dsilu(x):
    s = _sigmoid(x)
    return s * (1.0 + x * (1.0 - s))


def _gelu(x):
    return 0.5 * x * (1.0 + lax.erf(x * (1.0 / math.sqrt(2.0))))


def _dgelu(x):
    return 0.5 * (1.0 + lax.erf(x * (1.0 / math.sqrt(2.0)))) + x * jnp.exp(-0.5 * x * x) * (1.0 / math.sqrt(2.0 * math.pi))


def _softplus(x):
    return jnp.maximum(x, 0.0) + jnp.log(1.0 + jnp.exp(-jnp.abs(x)))


def _rms(x, w, eps):
    r = lax.rsqrt(jnp.mean(x * x, axis=-1, keepdims=True) + eps)
    return x * r * w


def _rms_bwd(dy, x, w, eps):
    r = lax.rsqrt(jnp.mean(x * x, axis=-1, keepdims=True) + eps)
    xh = x * r
    g = dy * w
    dx = r * (g - xh * jnp.mean(g * xh, axis=-1, keepdims=True))
    dw = jnp.sum(dy * xh, axis=0, keepdims=True)
    return dx, dw


def _dot(a, b, dims=(((1,), (0,)), ((), ())), precision=None):
    return lax.dot_general(a, b, dims, precision=precision, preferred_element_type=f32)


NN = (((1,), (0,)), ((), ()))
NT = (((1,), (1,)), ((), ()))
TN = (((0,), (0,)), ((), ()))


def _split3(x):
    hi = x.astype(bf16)
    r1 = x - hi.astype(f32)
    mid = r1.astype(bf16)
    return hi, mid, (r1 - mid.astype(f32)).astype(bf16)


def _dot01_left(m01, x):
    mb = m01.astype(bf16)
    hi, mid, lo = _split3(x)
    return _dot(mb, hi, NN) + _dot(mb, mid, NN) + _dot(mb, lo, NN)


def _dot01_right(x, m01):
    mb = m01.astype(bf16)
    hi, mid, lo = _split3(x)
    return _dot(hi, mb, NN) + _dot(mid, mb, NN) + _dot(lo, mb, NN)


def _mm(a, b, *, mode="nn", out_dtype=f32, res=None, kbatch=False, tm=1024, tn=1024, tk=1024, name):
    a3, b3 = a.ndim == 3, b.ndim == 3
    nb = a.shape[0] if a3 else (b.shape[0] if b3 else 1)
    ash, bsh = a.shape[-2:], b.shape[-2:]
    if mode == "nn":
        M, K, N = ash[0], ash[1], bsh[1]
    elif mode == "nt":
        M, K, N = ash[0], ash[1], bsh[0]
    else:
        K, M, N = ash[0], ash[1], bsh[1]
    tm, tn, tk = _tile(M, tm), (N if N % LANES else _tile(N, tn)), (K if K % LANES else _tile(K, tk))
    nk = K // tk
    if kbatch:
        assert a3 and b3
        grid = (1, M // tm, N // tn, nb * nk)
        bi = lambda g, k: k // nk
        ki = lambda g, k: k % nk
    else:
        grid = (nb, M // tm, N // tn, nk)
        bi = lambda g, k: g
        ki = lambda g, k: k
    nsteps = grid[3]

    def spec(is3, blk, imap):
        if is3:
            return pl.BlockSpec((None,) + blk, lambda g, i, j, k: (bi(g, k),) + imap(i, j, ki(g, k)))
        return pl.BlockSpec(blk, lambda g, i, j, k: imap(i, j, ki(g, k)))

    if mode == "nn":
        a_spec = spec(a3, (tm, tk), lambda i, j, k: (i, k))
        b_spec = spec(b3, (tk, tn), lambda i, j, k: (k, j))
        dims = NN
    elif mode == "nt":
        a_spec = spec(a3, (tm, tk), lambda i, j, k: (i, k))
        b_spec = spec(b3, (tn, tk), lambda i, j, k: (j, k))
        dims = NT
    else:
        a_spec = spec(a3, (tk, tm), lambda i, j, k: (k, i))
        b_spec = spec(b3, (tk, tn), lambda i, j, k: (k, j))
        dims = TN
    out3 = (a3 or b3) and not kbatch
    if out3:
        o_spec = pl.BlockSpec((None, tm, tn), lambda g, i, j, k: (g, i, j))
        o_shape = jax.ShapeDtypeStruct((nb, M, N), out_dtype)
    else:
        o_spec = pl.BlockSpec((tm, tn), lambda g, i, j, k: (i, j))
        o_shape = jax.ShapeDtypeStruct((M, N), out_dtype)
    in_specs = [a_spec, b_spec]
    args = [a, b]
    if res is not None:
        in_specs.append(pl.BlockSpec((tm, tn), lambda g, i, j, k: (i, j)))
        args.append(res)

    def body(*refs):
        if res is not None:
            a_ref, b_ref, r_ref, o_ref, acc_ref = refs
        else:
            a_ref, b_ref, o_ref, acc_ref = refs
        k = pl.program_id(3)

        @pl.when(k == 0)
        def _():
            acc_ref[...] = jnp.zeros_like(acc_ref)

        acc_ref[...] += _dot(a_ref[...].astype(bf16), b_ref[...].astype(bf16), dims)

        @pl.when(k == nsteps - 1)
        def _():
            r = acc_ref[...]
            if res is not None:
                r = r + r_ref[...]
            o_ref[...] = r.astype(o_ref.dtype)

    return pl.pallas_call(
        body, name=name, grid=grid, in_specs=in_specs, out_specs=o_spec, out_shape=o_shape,
        scratch_shapes=[pltpu.VMEM((tm, tn), f32)],
        compiler_params=_cparams(("parallel", "parallel", "parallel", "arbitrary")),
    )(*args)


def _rowcall(fn, *, name, rows, fulls, out_rows, out_accs=(), tm=512):
    S = rows[0].shape[0]
    tm = _tile(S, tm)
    n_r, n_f, n_or, n_oa = len(rows), len(fulls), len(out_rows), len(out_accs)

    def body(*refs):
        ins = [r[...] for r in refs[:n_r + n_f]]
        outs = fn(*ins)
        if not isinstance(outs, (tuple, list)):
            outs = (outs,)
        o_refs = refs[n_r + n_f:]
        for o_ref, v in zip(o_refs[:n_or], outs[:n_or]):
            o_ref[...] = v.astype(o_ref.dtype)
        if n_oa:
            first = pl.program_id(0) == 0

            @pl.when(first)
            def _():
                for o_ref, v in zip(o_refs[n_or:], outs[n_or:]):
                    o_ref[...] = v

            @pl.when(jnp.logical_not(first))
            def _():
                for o_ref, v in zip(o_refs[n_or:], outs[n_or:]):
                    o_ref[...] += v

    in_specs = [_row_spec(tm, r.shape[1]) for r in rows] + [_full_spec(f.shape) for f in fulls]
    out_specs = [_row_spec(tm, c) for c, _ in out_rows] + [_full_spec(s) for s in out_accs]
    out_shape = [jax.ShapeDtypeStruct((S, c), d) for c, d in out_rows] + [jax.ShapeDtypeStruct(s, f32) for s in out_accs]
    res = pl.pallas_call(
        body, name=name, grid=(S // tm,), in_specs=in_specs, out_specs=out_specs, out_shape=out_shape,
        compiler_params=_cparams(("arbitrary",) if n_oa else ("parallel",)),
    )(*rows, *fulls)
    return res


def _row2(v):
    return v.reshape(1, -1)


def _rms_fwd(h, w, name):
    D = h.shape[1]
    return _rowcall(lambda x, w_: _rms(x, w_, RMS_EPS), name=name, rows=[h], fulls=[_row2(w)], out_rows=[(D, bf16)])[0]


def _conv_fwd(xpre, w, b, name):
    S, C = xpre.shape
    tm, tc = _tile(S, 512), _tile(C, 1024)
    hb = tm // SUBLANES

    def body(x_ref, halo_ref, w_ref, b_ref, c_ref, o_ref):
        i = pl.program_id(1)
        x = x_ref[...]
        halo = jnp.where(i > 0, halo_ref[...], 0.0)
        row = lax.broadcasted_iota(jnp.int32, x.shape, 0)
        row8 = lax.broadcasted_iota(jnp.int32, halo.shape, 0)
        x0 = x[0:SUBLANES, :]
        acc = x * w_ref[CONV_K - 1:CONV_K, :] + b_ref[...]
        acc0 = x0 * w_ref[CONV_K - 1:CONV_K, :] + b_ref[...]
        for k in range(1, CONV_K):
            wk = w_ref[CONV_K - 1 - k:CONV_K - k, :]
            acc = acc + pltpu.roll(x, k, axis=0) * wk
            acc0 = acc0 + jnp.where(row8 < k, pltpu.roll(halo, k, axis=0), pltpu.roll(x0, k, axis=0)) * wk
        c_ref[...] = acc
        o_ref[...] = _silu(acc)
        c_ref[0:SUBLANES, :] = acc0
        o_ref[0:SUBLANES, :] = _silu(acc0)

    return pl.pallas_call(
        body, name=name, grid=(C // tc, S // tm),
        in_specs=[pl.BlockSpec((tm, tc), lambda j, i: (i, j)),
                  pl.BlockSpec((SUBLANES, tc), lambda j, i: (jnp.maximum(i * hb - 1, 0), j)),
                  pl.BlockSpec((CONV_K, tc), lambda j, i: (0, j)),
                  pl.BlockSpec((1, tc), lambda j, i: (0, j))],
        out_specs=[pl.BlockSpec((tm, tc), lambda j, i: (i, j))] * 2,
        out_shape=[jax.ShapeDtypeStruct((S, C), f32)] * 2,
        compiler_params=_cparams(("parallel", "parallel")),
    )(xpre, xpre, w, _row2(b))


def _conv_bwd_dc(dxbc, c, xpre, name):
    S, C = xpre.shape
    tm, tc = _tile(S, 512), _tile(C, 1024)
    hb = tm // SUBLANES

    def body(d_ref, c_ref, x_ref, halo_ref, dc_ref, dw_ref, db_ref):
        i = pl.program_id(1)
        x = x_ref[...]
        dc = d_ref[...] * _dsilu(c_ref[...])
        dc_ref[...] = dc
        halo = jnp.where(i > 0, halo_ref[...], 0.0)
        row = lax.broadcasted_iota(jnp.int32, x.shape, 0)
        row8 = lax.broadcasted_iota(jnp.int32, halo.shape, 0)
        x0 = x[0:SUBLANES, :]
        dc0 = dc[0:SUBLANES, :]
        parts = [jnp.sum(dc * x, axis=0, keepdims=True)]
        for k in range(1, CONV_K):
            xs_big = jnp.where(row < SUBLANES, 0.0, pltpu.roll(x, k, axis=0))
            xs0 = jnp.where(row8 < k, pltpu.roll(halo, k, axis=0), pltpu.roll(x0, k, axis=0))
            parts.append(jnp.sum(dc * xs_big, axis=0, keepdims=True) + jnp.sum(dc0 * xs0, axis=0, keepdims=True))
        dw = jnp.concatenate([parts[CONV_K - 1 - k] for k in range(CONV_K)] + [jnp.zeros((SUBLANES - CONV_K, x.shape[1]), f32)], axis=0)
        db = jnp.sum(dc, axis=0, keepdims=True)

        @pl.when(i == 0)
        def _():
            dw_ref[...] = dw
            db_ref[...] = db

        @pl.when(i > 0)
        def _():
            dw_ref[...] += dw
            db_ref[...] += db

    return pl.pallas_call(
        body, name=name, grid=(C // tc, S // tm),
        in_specs=[pl.BlockSpec((tm, tc), lambda j, i: (i, j))] * 3 +
                 [pl.BlockSpec((SUBLANES, tc), lambda j, i: (jnp.maximum(i * hb - 1, 0), j))],
        out_specs=[pl.BlockSpec((tm, tc), lambda j, i: (i, j)),
                   pl.BlockSpec((SUBLANES, tc), lambda j, i: (0, j)),
                   pl.BlockSpec((1, tc), lambda j, i: (0, j))],
        out_shape=[jax.ShapeDtypeStruct((S, C), f32), jax.ShapeDtypeStruct((SUBLANES, C), f32), jax.ShapeDtypeStruct((1, C), f32)],
        compiler_params=_cparams(("parallel", "arbitrary")),
    )(dxbc, c, xpre, xpre)


def _conv_bwd_dx(dc, w, name):
    S, C = dc.shape
    tm, tc = _tile(S, 512), _tile(C, 1024)
    hb = tm // SUBLANES
    nrow = S // tm
    last8 = S // SUBLANES - 1

    def body(d_ref, nxt_ref, w_ref, o_ref):
        i = pl.program_id(1)
        d = d_ref[...]
        nxt = jnp.where(i < nrow - 1, nxt_ref[...], 0.0)
        row8 = lax.broadcasted_iota(jnp.int32, nxt.shape, 0)
        dl = d[tm - SUBLANES:tm, :]
        acc = d * w_ref[CONV_K - 1:CONV_K, :]
        accl = dl * w_ref[CONV_K - 1:CONV_K, :]
        for j in range(1, CONV_K):
            wk = w_ref[CONV_K - 1 - j:CONV_K - j, :]
            acc = acc + pltpu.roll(d, tm - j, axis=0) * wk
            accl = accl + jnp.where(row8 >= SUBLANES - j, pltpu.roll(nxt, SUBLANES - j, axis=0), pltpu.roll(dl, SUBLANES - j, axis=0)) * wk
        o_ref[...] = acc.astype(o_ref.dtype)
        o_ref[tm - SUBLANES:tm, :] = accl.astype(o_ref.dtype)

    return pl.pallas_call(
        body, name=name, grid=(C // tc, nrow),
        in_specs=[pl.BlockSpec((tm, tc), lambda j, i: (i, j)),
                  pl.BlockSpec((SUBLANES, tc), lambda j, i: (jnp.minimum((i + 1) * hb, last8), j)),
                  pl.BlockSpec((CONV_K, tc), lambda j, i: (0, j))],
        out_specs=pl.BlockSpec((tm, tc), lambda j, i: (i, j)),
        out_shape=jax.ShapeDtypeStruct((S, C), f32),
        compiler_params=_cparams(("parallel", "parallel")),
    )(dc, dc, w)


def _halfsum(v, lane_lo):
    s0 = jnp.sum(jnp.where(lane_lo, v, 0.0), axis=1, keepdims=True)
    s1 = jnp.sum(jnp.where(lane_lo, 0.0, v), axis=1, keepdims=True)
    return jnp.where(lane_lo, s0, s1)


def _ssd_specs(S, inner, GN, nchunks, rev):
    L = CHUNK
    cm = (lambda c: nchunks - 1 - c) if rev else (lambda c: c)
    xs = pl.BlockSpec((L, inner), lambda c: (cm(c), 0))
    bb = pl.BlockSpec((L, GN), lambda c: (cm(c), inner // GN))
    cc = pl.BlockSpec((L, GN), lambda c: (cm(c), inner // GN + 1))
    row = pl.BlockSpec((L, inner), lambda c: (cm(c), 0))
    vec = pl.BlockSpec((1, inner), lambda c: (0, 0))
    st = pl.BlockSpec((None, inner, STATE), lambda c: (cm(c), 0, 0))
    return xs, bb, cc, row, vec, st


def _ssd_fwd(xbc, dtx, ax, dx, G, name):
    S, inner = dtx.shape
    GN = G * STATE
    L = CHUNK
    nchunks = S // L
    npairs = inner // LANES
    ppg = npairs // G
    assert inner % GN == 0 and L == LANES and STATE == LANES

    def body(xs_ref, b_ref, c_ref, dtx_ref, ax_ref, dx_ref, y_ref, so_ref, st_ref):
        ci = pl.program_id(0)

        @pl.when(ci == 0)
        def _():
            st_ref[...] = jnp.zeros_like(st_ref)

        r = lax.broadcasted_iota(jnp.int32, (L, L), 0)
        cidx = lax.broadcasted_iota(jnp.int32, (L, L), 1)
        tril = cidx <= r
        lane_lo = cidx < HEADDIM
        xs = xs_ref[...]
        dtv = dtx_ref[...]
        X = xs * dtv
        da = dtv * ax_ref[...]
        cs = _dot01_left(tril, da)
        cs_last = jnp.sum(da, axis=0, keepdims=True)
        so_ref[...] = st_ref[...]
        for g in range(G):
            Bg = b_ref[:, g * STATE:(g + 1) * STATE].astype(bf16)
            Cg = c_ref[:, g * STATE:(g + 1) * STATE].astype(bf16)
            CB = _dot(Cg, Bg, NT)
            for j in range(ppg):
                lo = (g * ppg + j) * LANES
                tile = cs[:, lo:lo + LANES]
                rl = pltpu.roll(tile, HEADDIM, axis=1)
                Xp = X[:, lo:lo + LANES]
                prev = st_ref[lo:lo + LANES, :]
                ypair = _dot(Cg, prev.astype(bf16), NT) * jnp.exp(tile)
                for half in (0, 1):
                    hm = lane_lo if half == 0 else jnp.logical_not(lane_lo)
                    colb = jnp.where(hm, tile, rl)
                    Lm = jnp.exp(jnp.where(tril, colb - colb.T, -1e30))
                    W = (CB * Lm).astype(bf16)
                    ypair = ypair + _dot(W, jnp.where(hm, Xp, 0.0).astype(bf16), NN)
                y_ref[:, lo:lo + LANES] = ypair + xs[:, lo:lo + LANES] * dx_ref[:, lo:lo + LANES]
                last = cs_last[:, lo:lo + LANES]
                snew = _dot((Xp * jnp.exp(last - tile)).astype(bf16), Bg, TN)
                dec_rows = jnp.broadcast_to(jnp.exp(last), (L, LANES)).T
                st_ref[lo:lo + LANES, :] = dec_rows * prev + snew

    xs_s, b_s, c_s, row_s, vec_s, st_s = _ssd_specs(S, inner, GN, nchunks, False)
    return pl.pallas_call(
        body, name=name, grid=(nchunks,),
        in_specs=[xs_s, b_s, c_s, row_s, vec_s, vec_s],
        out_specs=[row_s, st_s],
        out_shape=[jax.ShapeDtypeStruct((S, inner), f32), jax.ShapeDtypeStruct((nchunks, inner, STATE), f32)],
        scratch_shapes=[pltpu.VMEM((inner, STATE), f32)],
        compiler_params=_cparams(("arbitrary",)),
    )(xbc, xbc, xbc, dtx, ax, dx)


def _ssd_bwd(dy, y, xbc, dtx, ax, dx, states, et, G, name):
    S, inner = dtx.shape
    H = et.shape[1]
    GN = G * STATE
    Cc = inner + 2 * GN
    L = CHUNK
    nchunks = S // L
    npairs = inner // LANES
    ppg = npairs // G

    def body(dy_ref, y_ref, xs_ref, b_ref, c_ref, dtx_ref, ax_ref, dx_ref, si_ref, et_ref,
             dxbc_ref, ddt_ref, dax_ref, ddx_ref, dst_ref, dA_ref, dAl_ref, ddtp_ref):
        ci = pl.program_id(0)

        @pl.when(ci == 0)
        def _():
            dst_ref[...] = jnp.zeros_like(dst_ref)
            dax_ref[...] = jnp.zeros_like(dax_ref)
            ddx_ref[...] = jnp.zeros_like(ddx_ref)

        r = lax.broadcasted_iota(jnp.int32, (L, L), 0)
        cidx = lax.broadcasted_iota(jnp.int32, (L, L), 1)
        tril = cidx <= r
        lane_lo = cidx < HEADDIM
        lane_lo1 = lax.broadcasted_iota(jnp.int32, (1, LANES), 1) < HEADDIM
        xs = xs_ref[...]
        dtv = dtx_ref[...]
        dyv = dy_ref[...]
        X = xs * dtv
        da = dtv * ax_ref[...]
        cs = _dot01_left(tril, da)
        cs_last = jnp.sum(da, axis=0, keepdims=True)
        for g in range(G):
            Bg = b_ref[:, g * STATE:(g + 1) * STATE].astype(bf16)
            Cg = c_ref[:, g * STATE:(g + 1) * STATE].astype(bf16)
            CB = _dot(Cg, Bg, NT)
            dCB = jnp.zeros((L, L), f32)
            dBg = jnp.zeros((L, STATE), f32)
            dCg = jnp.zeros((L, STATE), f32)
            for j in range(ppg):
                lo = (g * ppg + j) * LANES
                tile = cs[:, lo:lo + LANES]
                rl = pltpu.roll(tile, HEADDIM, axis=1)
                eA = jnp.exp(tile)
                Xp = X[:, lo:lo + LANES]
                dYp = dyv[:, lo:lo + LANES]
                xsp = xs[:, lo:lo + LANES]
                prev = si_ref[lo:lo + LANES, :]
                dSn = dst_ref[lo:lo + LANES, :]
                prev_b = prev.astype(bf16)
                dSn_b = dSn.astype(bf16)
                dYe = (dYp * eA).astype(bf16)
                dCg = dCg + _dot(dYe, prev_b, NN)
                dprev = _dot(dYe, Cg, TN)
                last = cs_last[:, lo:lo + LANES]
                w = jnp.exp(last - tile)
                BdS = _dot(Bg, dSn_b, NT)
                Xw = Xp * w
                XwB = Xw * BdS
                dAl_t = _halfsum(jnp.sum(XwB, axis=0, keepdims=True), lane_lo1)
                dBg = dBg + _dot(Xw.astype(bf16), dSn_b, NN)
                dec_rows = jnp.broadcast_to(jnp.exp(last), (L, LANES)).T
                dprev = dprev + dec_rows * dSn
                rsum = jnp.sum(dSn * prev * dec_rows, axis=1, keepdims=True)
                s0 = jnp.sum(rsum[0:HEADDIM], axis=0, keepdims=True)
                s1 = jnp.sum(rsum[HEADDIM:LANES], axis=0, keepdims=True)
                dAl_t = dAl_t + jnp.where(lane_lo1, s0, s1)
                dXd = jnp.zeros((L, LANES), f32)
                for half in (0, 1):
                    hm = lane_lo if half == 0 else jnp.logical_not(lane_lo)
                    colb = jnp.where(hm, tile, rl)
                    Lm = jnp.exp(jnp.where(tril, colb - colb.T, -1e30))
                    dYh = jnp.where(hm, dYp, 0.0).astype(bf16)
                    dW = _dot(dYh, jnp.where(hm, Xp, 0.0).astype(bf16), NT)
                    dXd = dXd + _dot((CB * Lm).astype(bf16), dYh, TN)
                    dCB = dCB + dW * Lm
                yoff = _dot(Cg, prev_b, NT) * eA
                ydiag = y_ref[:, lo:lo + LANES] - xsp * dx_ref[:, lo:lo + LANES] - yoff
                dYb = dYp.astype(bf16).astype(f32)
                Xb = Xp.astype(bf16).astype(f32)
                dA_t = _halfsum(dYb * ydiag - Xb * dXd + dYp * yoff - XwB, lane_lo)
                dXp = w * BdS + dXd
                dxbc_ref[:, lo:lo + LANES] = dXp * dtv[:, lo:lo + LANES] + dYp * dx_ref[:, lo:lo + LANES]
                ddtp_ref[:, lo:lo + LANES] = dXp * xsp
                ddx_ref[:, lo:lo + LANES] += jnp.sum(dYp * xsp, axis=0, keepdims=True)
                dA_ref[:, lo:lo + LANES] = dA_t
                dAl_ref[:, lo:lo + LANES] = dAl_t
                dst_ref[lo:lo + LANES, :] = dprev
            dCBb = dCB.astype(bf16)
            dxbc_ref[:, inner + g * STATE:inner + (g + 1) * STATE] = dBg + _dot(dCBb, Cg, TN)
            dxbc_ref[:, inner + GN + g * STATE:inner + GN + (g + 1) * STATE] = dCg + _dot(dCBb, Bg, NN)
        dda = _dot01_left(cidx >= r, dA_ref[...]) + dAl_ref[...]
        ddt_full = ddtp_ref[...] + dda * ax_ref[...] * (1.0 / HEADDIM)
        ddt_ref[...] = _dot01_right(ddt_full, et_ref[...])
        dax_ref[...] += jnp.sum(dda * dtv, axis=0, keepdims=True)

    xs_s, b_s, c_s, row_s, vec_s, st_s = _ssd_specs(S, inner, GN, nchunks, True)
    return pl.pallas_call(
        body, name=name, grid=(nchunks,),
        in_specs=[row_s, row_s, xs_s, b_s, c_s, row_s, vec_s, vec_s, st_s, _full_spec(et.shape)],
        out_specs=[pl.BlockSpec((L, Cc), lambda c: (nchunks - 1 - c, 0)),
                   pl.BlockSpec((L, H), lambda c: (nchunks - 1 - c, 0)), vec_s, vec_s],
        out_shape=[jax.ShapeDtypeStruct((S, Cc), f32), jax.ShapeDtypeStruct((S, H), f32),
                   jax.ShapeDtypeStruct((1, inner), f32), jax.ShapeDtypeStruct((1, inner), f32)],
        scratch_shapes=[pltpu.VMEM((inner, STATE), f32), pltpu.VMEM((L, inner), f32),
                        pltpu.VMEM((1, inner), f32), pltpu.VMEM((L, inner), f32)],
        compiler_params=_cparams(("arbitrary",)),
    )(dy, y, xbc, xbc, xbc, dtx, ax, dx, states, et)


def _dt_fwd(dt_pre, bias, e, name):
    H, inner = e.shape

    def fn(dp, b, e_):
        dt = _softplus(dp + b)
        return dt, _dot01_right(dt, e_)

    return _rowcall(fn, name=name, rows=[dt_pre], fulls=[_row2(bias), e], out_rows=[(H, f32), (inner, f32)])


def _dt_bwd(ddt, dt_pre, bias, name):
    H = ddt.shape[1]

    def fn(dd, dp, b):
        g = dd * _sigmoid(dp + b)
        return g, jnp.sum(g, axis=0, keepdims=True)

    return _rowcall(fn, name=name, rows=[ddt, dt_pre], fulls=[_row2(bias)], out_rows=[(H, f32)], out_accs=[(1, H)])


def _gnorm_fwd(y, z, w, G, name):
    inner = y.shape[1]
    gs = inner // G

    def fn(y_, z_, w_):
        gg = y_ * _silu(z_)
        outs = []
        for g in range(G):
            sl = slice(g * gs, (g + 1) * gs)
            outs.append(_rms(gg[:, sl], w_[:, sl], LN_EPS))
        return jnp.concatenate(outs, axis=1)

    return _rowcall(fn, name=name, rows=[y, z], fulls=[_row2(w)], out_rows=[(inner, bf16)], tm=256)[0]


def _gnorm_bwd(dyn, y, z, w, G, name):
    inner = y.shape[1]
    gs = inner // G

    def fn(d_, y_, z_, w_):
        sz = _silu(z_)
        gg = y_ * sz
        dgs, dws = [], []
        for g in range(G):
            sl = slice(g * gs, (g + 1) * gs)
            dg, dw = _rms_bwd(d_[:, sl], gg[:, sl], w_[:, sl], LN_EPS)
            dgs.append(dg)
            dws.append(dw)
        dgg = jnp.concatenate(dgs, axis=1)
        return dgg * sz, dgg * y_ * _dsilu(z_), jnp.concatenate(dws, axis=1)

    return _rowcall(fn, name=name, rows=[dyn, y, z], fulls=[_row2(w)], out_rows=[(inner, f32), (inner, f32)],
                    out_accs=[(1, inner)], tm=256)


def _gmlp_parts(pre, lw, lb, I):
    hp = _gelu(pre)
    uu = hp[:, :I]
    vp = hp[:, I:]
    xc = vp - jnp.mean(vp, axis=-1, keepdims=True)
    rstd = lax.rsqrt(jnp.mean(xc * xc, axis=-1, keepdims=True) + LN_EPS)
    vhat = xc * rstd
    return uu, vhat, rstd, vhat * lw + lb


def _gmlp_mid_fwd(pre, b_in, ln_w, ln_b, w_s, bsx, name):
    S, two_i = pre.shape
    I = two_i // 2
    NG = w_s.shape[0]
    gd = I // NG
    L = CHUNK

    def body(pre_ref, bi_ref, lw_ref, lb_ref, ws_ref, bsx_ref, o_ref):
        uu, _, _, vv = _gmlp_parts(pre_ref[...] + bi_ref[...], lw_ref[...], lb_ref[...], I)
        r = lax.broadcasted_iota(jnp.int32, (L, L), 0)
        cidx = lax.broadcasted_iota(jnp.int32, (L, L), 1)
        tril = cidx <= r
        for g in range(NG):
            sl = slice(g * gd, (g + 1) * gd)
            wg = jnp.where(tril, ws_ref[g], 0.0).astype(bf16)
            mixed = _dot(wg, vv[:, sl].astype(bf16), NN) + bsx_ref[:, sl]
            o_ref[:, sl] = (uu[:, sl] * mixed).astype(o_ref.dtype)

    return pl.pallas_call(
        body, name=name, grid=(S // L,),
        in_specs=[_row_spec(L, two_i), _full_spec((1, two_i)), _full_spec((1, I)), _full_spec((1, I)), _full_spec(w_s.shape), _full_spec(bsx.shape)],
        out_specs=_row_spec(L, I), out_shape=jax.ShapeDtypeStruct((S, I), bf16),
        compiler_params=_cparams(("parallel",)),
    )(pre, _row2(b_in), _row2(ln_w), _row2(ln_b), w_s, bsx)


def _gmlp_mid_bwd(do, pre, b_in, ln_w, ln_b, w_s, bsx, name):
    S, two_i = pre.shape
    I = two_i // 2
    NG = w_s.shape[0]
    gd = I // NG
    L = CHUNK

    def body(do_ref, pre_ref, bi_ref, lw_ref, lb_ref, ws_ref, bsx_ref, dpre_ref, dbi_ref, dlw_ref, dlb_ref, dws_ref, dbs_ref, dvv_ref):
        ci = pl.program_id(0)

        @pl.when(ci == 0)
        def _():
            for ref in (dbi_ref, dlw_ref, dlb_ref, dws_ref, dbs_ref):
                ref[...] = jnp.zeros_like(ref)

        pre = pre_ref[...] + bi_ref[...]
        lw = lw_ref[...]
        uu, vhat, rstd, vv = _gmlp_parts(pre, lw, lb_ref[...], I)
        dov = do_ref[...]
        r = lax.broadcasted_iota(jnp.int32, (L, L), 0)
        cidx = lax.broadcasted_iota(jnp.int32, (L, L), 1)
        tril = cidx <= r
        duus = []
        for g in range(NG):
            sl = slice(g * gd, (g + 1) * gd)
            wg = jnp.where(tril, ws_ref[g], 0.0).astype(bf16)
            vg = vv[:, sl].astype(bf16)
            mixed = _dot(wg, vg, NN) + bsx_ref[:, sl]
            duus.append(dov[:, sl] * mixed)
            dmixed = dov[:, sl] * uu[:, sl]
            dbs_ref[:, sl] += dmixed
            dmb = dmixed.astype(bf16)
            dvv_ref[:, sl] = _dot(wg, dmb, TN)
            dws_ref[g] += jnp.where(tril, _dot(dmb, vg, NT), 0.0)
        duu = jnp.concatenate(duus, axis=1)
        dvv = dvv_ref[...]
        dlw_ref[...] += jnp.sum(dvv * vhat, axis=0, keepdims=True)
        dlb_ref[...] += jnp.sum(dvv, axis=0, keepdims=True)
        dvh = dvv * lw
        dvp = rstd * (dvh - jnp.mean(dvh, axis=-1, keepdims=True) - vhat * jnp.mean(dvh * vhat, axis=-1, keepdims=True))
        dpre = jnp.concatenate([duu, dvp], axis=1) * _dgelu(pre)
        dbi_ref[...] += jnp.sum(dpre, axis=0, keepdims=True)
        dpre_ref[...] = dpre.astype(dpre_ref.dtype)

    return pl.pallas_call(
        body, name=name, grid=(S // L,),
        in_specs=[_row_spec(L, I), _row_spec(L, two_i), _full_spec((1, two_i)), _full_spec((1, I)), _full_spec((1, I)),
                  _full_spec(w_s.shape), _full_spec(bsx.shape)],
        out_specs=[_row_spec(L, two_i), _full_spec((1, two_i)), _full_spec((1, I)), _full_spec((1, I)), _full_spec(w_s.shape), _full_spec((L, I))],
        out_shape=[jax.ShapeDtypeStruct((S, two_i), bf16), jax.ShapeDtypeStruct((1, two_i), f32), jax.ShapeDtypeStruct((1, I), f32),
                   jax.ShapeDtypeStruct((1, I), f32), jax.ShapeDtypeStruct(w_s.shape, f32), jax.ShapeDtypeStruct((L, I), f32)],
        scratch_shapes=[pltpu.VMEM((L, I), f32)],
        compiler_params=_cparams(("arbitrary",)),
    )(do, pre, _row2(b_in), _row2(ln_w), _row2(ln_b), w_s, bsx)


def _lane_group_sum(acc, eg, name):
    NG = eg.shape[1]
    return _rowcall(lambda a, e: _dot(a, e, NN, HI), name=name, rows=[acc], fulls=[eg], out_rows=[(NG, f32)])[0]


def _ffn_fwd_fused(h1, nf_w, wg, wu, wd, name):
    S, D = h1.shape
    nb, _, F4 = wg.shape
    tm = _tile(S, 512)

    def body(h_ref, nf_ref, wg_ref, wu_ref, wd_ref, h2_ref, u_ref, g_ref, up_ref, a_ref, acc_ref):
        k = pl.program_id(1)

        @pl.when(k == 0)
        def _():
            u_ref[...] = _rms(h_ref[...], nf_ref[...], RMS_EPS).astype(u_ref.dtype)

        uv = u_ref[...]
        g = _dot(uv, wg_ref[...], NN)
        up = _dot(uv, wu_ref[...], NN)
        a = (_silu(g) * up).astype(bf16)
        g_ref[...] = g.astype(g_ref.dtype)
        up_ref[...] = up.astype(up_ref.dtype)
        a_ref[...] = a
        part = _dot(a, wd_ref[...], NN)

        @pl.when(k == 0)
        def _():
            acc_ref[...] = part

        @pl.when(k > 0)
        def _():
            acc_ref[...] += part

        @pl.when(k == nb - 1)
        def _():
            h2_ref[...] = h_ref[...] + acc_ref[...]

    row = pl.BlockSpec((tm, D), lambda i, k: (i, 0))
    wspec = pl.BlockSpec((None, D, F4), lambda i, k: (k, 0, 0))
    cspec = pl.BlockSpec((None, tm, F4), lambda i, k: (k, i, 0))
    chunk = jax.ShapeDtypeStruct((nb, S, F4), bf16)
    return pl.pallas_call(
        body, name=name, grid=(S // tm, nb),
        in_specs=[row, _full_spec((1, D)), wspec, wspec, pl.BlockSpec((None, F4, D), lambda i, k: (k, 0, 0))],
        out_specs=[row, row, cspec, cspec, cspec],
        out_shape=[jax.ShapeDtypeStruct((S, D), f32), jax.ShapeDtypeStruct((S, D), bf16), chunk, chunk, chunk],
        scratch_shapes=[pltpu.VMEM((tm, D), f32)],
        compiler_params=_cparams(("parallel", "arbitrary")),
    )(h1, _row2(nf_w), wg, wu, wd)


def _ffn_bwd_fused(dh, h1, nf_w, wd, wg, wu, G, U, name):
    S, D = dh.shape
    nb, F4, _ = wd.shape
    tm = _tile(S, 512)

    def body(dh_ref, h_ref, nf_ref, wd_ref, wg_ref, wu_ref, g_ref, up_ref, dg_ref, du_ref, dh1_ref, dnf_ref, acc_ref):
        i, k = pl.program_id(0), pl.program_id(1)
        dA = _dot(dh_ref[...].astype(bf16), wd_ref[...], NT)
        g = g_ref[...].astype(f32)
        dg = (dA * up_ref[...].astype(f32) * _dsilu(g)).astype(bf16)
        du = (dA * _silu(g)).astype(bf16)
        dg_ref[...] = dg
        du_ref[...] = du
        part = _dot(dg, wg_ref[...], NT) + _dot(du, wu_ref[...], NT)

        @pl.when(k == 0)
        def _():
            acc_ref[...] = part

        @pl.when(k > 0)
        def _():
            acc_ref[...] += part

        @pl.when(k == nb - 1)
        def _():
            dx, dw = _rms_bwd(acc_ref[...], h_ref[...], nf_ref[...], RMS_EPS)
            dh1_ref[...] = dh_ref[...] + dx

            @pl.when(i == 0)
            def _():
                dnf_ref[...] = dw

            @pl.when(i > 0)
            def _():
                dnf_ref[...] += dw

    row = pl.BlockSpec((tm, D), lambda i, k: (i, 0))
    wspec = pl.BlockSpec((None, D, F4), lambda i, k: (k, 0, 0))
    cspec = pl.BlockSpec((None, tm, F4), lambda i, k: (k, i, 0))
    chunk = jax.ShapeDtypeStruct((nb, S, F4), bf16)
    return pl.pallas_call(
        body, name=name, grid=(S // tm, nb),
        in_specs=[row, row, _full_spec((1, D)), pl.BlockSpec((None, F4, D), lambda i, k: (k, 0, 0)), wspec, wspec, cspec, cspec],
        out_specs=[cspec, cspec, row, _full_spec((1, D))],
        out_shape=[chunk, chunk, jax.ShapeDtypeStruct((S, D), f32), jax.ShapeDtypeStruct((1, D), f32)],
        scratch_shapes=[pltpu.VMEM((tm, D), f32)],
        compiler_params=_cparams(("arbitrary", "arbitrary")),
    )(dh, h1, _row2(nf_w), wd, wg, wu, G, U)


def _rms_bwd_add(dres, du, h, w, name):
    D = h.shape[1]

    def fn(dr, du_, h_, w_):
        dx, dw = _rms_bwd(du_, h_, w_, RMS_EPS)
        return dr + dx, dw

    return _rowcall(fn, name=name, rows=[dres, du, h], fulls=[_row2(w)], out_rows=[(D, f32)], out_accs=[(1, D)])


def _ple_fwd(h, p_i, wp, pn, gn, wgate, name):
    D = h.shape[1]

    def fn(h_, p_, wp_, pn_, gn_, wg_):
        pe = _dot(p_.astype(bf16), wp_, NN)
        e = _rms(pe, pn_, RMS_EPS)
        q = _rms(h_, gn_, RMS_EPS)
        gate = _sigmoid(_dot(q.astype(bf16), wg_, NN))
        return h_ + gate * e, pe, gate

    return _rowcall(fn, name=name, rows=[h, p_i], fulls=[wp, _row2(pn), _row2(gn), wgate],
                    out_rows=[(D, f32), (D, f32), (D, f32)], tm=256)


def _ple_bwd(dh3, h, pe, gate, pn, gn, wgate, name, after=None):
    D = h.shape[1]

    def fn(d_, h_, pe_, gate_, pn_, gn_, wg_, *_):
        e = _rms(pe_, pn_, RMS_EPS)
        dzg = d_ * e * gate_ * (1.0 - gate_)
        dq = _dot(dzg.astype(bf16), wg_, NT)
        dxq, dgn = _rms_bwd(dq, h_, gn_, RMS_EPS)
        dpe, dpn = _rms_bwd(d_ * gate_, pe_, pn_, RMS_EPS)
        return d_ + dxq, dzg, dpe, _rms(h_, gn_, RMS_EPS), dpn, dgn

    return _rowcall(fn, name=name, rows=[dh3, h, pe, gate], fulls=[_row2(pn), _row2(gn), wgate] + ([] if after is None else [after]),
                    out_rows=[(D, f32), (D, bf16), (D, bf16), (D, bf16)], out_accs=[(1, D), (1, D)], tm=256)


def _loss_head(h, target, fn_w, name):
    D = h.shape[1]

    def fn(h_, t_, w_):
        diff = _rms(h_, w_, RMS_EPS) - t_
        loss = 0.5 * jnp.sum(jnp.mean(diff * diff, axis=-1, keepdims=True), axis=0, keepdims=True)
        dh, dw = _rms_bwd(diff * (1.0 / D), h_, w_, RMS_EPS)
        return dh, jnp.broadcast_to(loss, (1, LANES)), dw

    return _rowcall(fn, name=name, rows=[h, target], fulls=[_row2(fn_w)], out_rows=[(D, f32)], out_accs=[(1, LANES), (1, D)])


def _adamw(w, m, v, g, name):
    R, C = w.shape
    tr = R
    while tr * C > 256 * 1024 and tr % (2 * SUBLANES) == 0:
        tr //= 2

    def body(w_ref, m_ref, v_ref, g_ref, d_ref, mo_ref, vo_ref):
        g = g_ref[...]
        mn = ADAM_B1 * m_ref[...] + (1.0 - ADAM_B1) * g
        vn = ADAM_B2 * v_ref[...] + (1.0 - ADAM_B2) * (g * g)
        m_hat = mn / (1.0 - ADAM_B1 ** ADAM_STEP)
        v_hat = vn / (1.0 - ADAM_B2 ** ADAM_STEP)
        d_ref[...] = -ADAM_LR * (m_hat / (jnp.sqrt(v_hat) + ADAM_EPS) + ADAM_WD * w_ref[...])
        mo_ref[...] = mn
        vo_ref[...] = vn

    spec = pl.BlockSpec((tr, C), lambda i: (i, 0))
    return pl.pallas_call(
        body, name=name, grid=(R // tr,), in_specs=[spec] * 4,
        out_specs=[spec] * 3, out_shape=[jax.ShapeDtypeStruct((R, C), f32)] * 3,
        compiler_params=_cparams(("parallel",)),
    )(w, m, v, g)


def _expand_onehot(n, per):
    lane = lax.broadcasted_iota(jnp.int32, (n, n * per), 1)
    row = lax.broadcasted_iota(jnp.int32, (n, n * per), 0)
    return (lane // per == row).astype(f32)


def _ssd_layer_fwd(h, nm_w, W, t):
    H = W["dt_bias"].shape[0]
    inner = H * HEADDIM
    G = (W["conv_b"].shape[0] - inner) // (2 * STATE)
    hn = _rms_fwd(h, nm_w, f"rms_mix_{t}")
    z = _mm(hn, W["wz"], name=f"ssd_z_{t}")
    xpre = _mm(hn, W["wxbc"], name=f"ssd_xbc_{t}")
    dt_pre = _mm(hn, W["wdt"], name=f"ssd_dt_{t}")
    c, xbc = _conv_fwd(xpre, W["conv_w"], W["conv_b"], f"ssd_conv_{t}")
    _, dtx = _dt_fwd(dt_pre, W["dt_bias"], _expand_onehot(H, HEADDIM), f"ssd_dtx_{t}")
    a = -jnp.exp(W["a_log"])
    ax = _row2(jnp.repeat(a, HEADDIM))
    dx = _row2(jnp.repeat(W["d"], HEADDIM))
    y, states = _ssd_fwd(xbc, dtx, ax, dx, G, f"ssd_scan_{t}")
    yn = _gnorm_fwd(y, z, W["norm_w"], G, f"ssd_gnorm_{t}")
    h1 = _mm(yn, W["wout"], res=h, name=f"ssd_out_{t}")
    return h1, (h, hn, z, xpre, dt_pre, c, xbc, dtx, a, ax, dx, y, states, yn)


def _ssd_layer_bwd(dh1, saved, nm_w, W, t):
    h, hn, z, xpre, dt_pre, c, xbc, dtx, a, ax, dx, y, states, yn = saved
    H = W["dt_bias"].shape[0]
    inner = H * HEADDIM
    G = (W["conv_b"].shape[0] - inner) // (2 * STATE)
    dyn = _mm(dh1, W["wout"], mode="nt", name=f"ssd_out_dx_{t}")
    g_wout = _mm(yn, dh1, mode="tn", out_dtype=bf16, name=f"ssd_out_dw_{t}")
    dy, dz, g_normw = _gnorm_bwd(dyn, y, z, W["norm_w"], G, f"ssd_gnorm_bwd_{t}")
    dxbc, ddt, dax, ddx = _ssd_bwd(dy, y, xbc, dtx, ax, dx, states, _expand_onehot(H, HEADDIM).T, G, f"ssd_scan_bwd_{t}")
    dc, g_convw8, g_convb = _conv_bwd_dc(dxbc, c, xpre, f"ssd_conv_bwd_dc_{t}")
    dxpre = _conv_bwd_dx(dc, W["conv_w"], f"ssd_conv_bwd_dx_{t}")
    ddt_pre, g_dtb = _dt_bwd(ddt, dt_pre, W["dt_bias"], f"ssd_dt_bwd_{t}")
    g_wz = _mm(hn, dz, mode="tn", out_dtype=bf16, name=f"ssd_z_dw_{t}")
    g_wxbc = _mm(hn, dxpre, mode="tn", out_dtype=bf16, name=f"ssd_xbc_dw_{t}")
    g_wdt = _mm(hn, ddt_pre, mode="tn", out_dtype=bf16, name=f"ssd_dt_dw_{t}")
    dhn = _mm(dz, W["wz"], mode="nt", name=f"ssd_z_dx_{t}")
    dhn = _mm(dxpre, W["wxbc"], mode="nt", res=dhn, name=f"ssd_xbc_dx_{t}")
    dhn = _mm(ddt_pre, W["wdt"], mode="nt", res=dhn, name=f"ssd_dt_dx_{t}")
    dh, g_nm = _rms_bwd_add(dh1, dhn, h, nm_w, f"rms_mix_bwd_{t}")
    grads = dict(
        w_in=jnp.concatenate([g_wz, g_wxbc, g_wdt], axis=1), wout=g_wout,
        conv_w=g_convw8[:CONV_K], conv_b=g_convb[0], dt_bias=g_dtb[0],
        a_log=dax[0].reshape(H, HEADDIM)[:, 0] * a, d=jnp.sum(ddx[0].reshape(H, HEADDIM), axis=1),
        norm_w=g_normw[0], norm_mix=g_nm[0])
    return dh, grads


def _gmlp_layer_fwd(h, nm_w, W, t):
    NG, L, _ = W["w_s"].shape
    I = W["ln_w"].shape[0]
    hn = _rms_fwd(h, nm_w, f"rms_mix_{t}")
    pre = _mm(hn, W["win"], name=f"gmlp_in_{t}")
    bsx = jnp.repeat(W["b_s"].T, I // NG, axis=1)
    o = _gmlp_mid_fwd(pre, W["b_in"], W["ln_w"], W["ln_b"], W["w_s"], bsx, f"gmlp_mid_{t}")
    h1 = _mm(o, W["wout"], res=h, name=f"gmlp_out_{t}")
    return h1, (h, hn, pre, bsx, o)


def _gmlp_layer_bwd(dh1, saved, nm_w, W, t):
    h, hn, pre, bsx, o = saved
    NG = W["w_s"].shape[0]
    I = W["ln_w"].shape[0]
    do = _mm(dh1, W["wout"], mode="nt", name=f"gmlp_out_dx_{t}")
    g_wout = _mm(o, dh1, mode="tn", out_dtype=bf16, name=f"gmlp_out_dw_{t}")
    dpre, g_bin, g_lnw, g_lnb, g_ws, dbs = _gmlp_mid_bwd(do, pre, W["b_in"], W["ln_w"], W["ln_b"], W["w_s"], bsx, f"gmlp_mid_bwd_{t}")
    g_bs = _lane_group_sum(dbs, _expand_onehot(NG, I // NG).T, f"gmlp_bs_{t}").T
    g_win = _mm(hn, dpre, mode="tn", out_dtype=bf16, name=f"gmlp_in_dw_{t}")
    dhn = _mm(dpre, W["win"], mode="nt", name=f"gmlp_in_dx_{t}")
    dh, g_nm = _rms_bwd_add(dh1, dhn, h, nm_w, f"rms_mix_bwd_{t}")
    grads = dict(win=g_win, wout=g_wout, b_in=g_bin[0], ln_w=g_lnw[0], ln_b=g_lnb[0], w_s=g_ws, b_s=g_bs, norm_mix=g_nm[0])
    return dh, grads


def _ffn_fwd(h1, nf_w, W, t):
    h2, u, Gm, Um, A = _ffn_fwd_fused(h1, nf_w, W["wg"], W["wu"], W["wd"], f"ffn_fwd_{t}")
    return h2, (h1, u, Gm, Um, A)


def _ffn_bwd(dh2, saved, nf_w, W, t):
    h1, u, Gm, Um, A = saved
    dG, dU, dh1, g_nf = _ffn_bwd_fused(dh2, h1, nf_w, W["wd"], W["wg"], W["wu"], Gm, Um, f"ffn_bwd_{t}")
    g_wd = _mm(A, dh2, mode="tn", out_dtype=bf16, name=f"ffn_down_dw_{t}")
    g_wg = _mm(u, dG, mode="tn", out_dtype=bf16, name=f"ffn_gate_dw_{t}")
    g_wu = _mm(u, dU, mode="tn", out_dtype=bf16, name=f"ffn_up_dw_{t}")
    return dh1, dict(wg=g_wg, wu=g_wu, wd=g_wd, norm_ffn=g_nf[0])


def _local_step(x, p, target, norms, layer_weights, on_layer_grads=None, final_norm_grad=None):
    depth = p.shape[0]
    h = x
    saved = []
    for i in range(depth):
        Wi = layer_weights(i, h)
        if i % 2 == 0:
            h1, s_mix = _ssd_layer_fwd(h, norms["norm_mix"][i], Wi["mix"], i)
        else:
            h1, s_mix = _gmlp_layer_fwd(h, norms["norm_mix"][i], Wi["mix"], i)
        h2, s_ffn = _ffn_fwd(h1, norms["norm_ffn"][i], Wi["ffn"], i)
        P = Wi["ple"]
        h3, pe, gate = _ple_fwd(h2, p[i], P["wp"], P["pn"], P["gn"], P["wgate"], f"ple_{i}")
        saved.append((Wi, s_mix, s_ffn, (h2, pe, gate)))
        h = h3
    dh, loss, g_fn = _loss_head(h, target, norms["final_norm"], "loss_head")
    if final_norm_grad is not None:
        final_norm_grad[0] = g_fn[0]
    grads = [None] * depth
    after = None
    for i in reversed(range(depth)):
        Wi, s_mix, s_ffn, (h2, pe, gate) = saved[i]
        P = Wi["ple"]
        dh, dzg, dpe, q, g_pn, g_gn = _ple_bwd(dh, h2, pe, gate, P["pn"], P["gn"], P["wgate"], f"ple_bwd_{i}", after=after)
        g_ple = dict(wgate=_mm(q, dzg, mode="tn", out_dtype=bf16, name=f"ple_gate_dw_{i}"),
                     wp=_mm(p[i], dpe, mode="tn", out_dtype=bf16, name=f"ple_proj_dw_{i}"), pn=g_pn[0], gn=g_gn[0])
        dh, g_ffn = _ffn_bwd(dh, s_ffn, norms["norm_ffn"][i], Wi["ffn"], i)
        if i % 2 == 0:
            dh, g_mix = _ssd_layer_bwd(dh, s_mix, norms["norm_mix"][i], Wi["mix"], i)
        else:
            dh, g_mix = _gmlp_layer_bwd(dh, s_mix, norms["norm_mix"][i], Wi["mix"], i)
        grads[i] = dict(mix=g_mix, ffn=g_ffn, ple=g_ple)
        if on_layer_grads is not None:
            after = on_layer_grads(i, grads[i])
    return loss[0, 0], dh, g_fn[0], grads


def _flip(v, f):
    return 1 - v if f else v


_ANY = pl.BlockSpec(memory_space=pl.ANY)


_SEM = pl.BlockSpec(memory_space=pltpu.SEMAPHORE)
_DATAFLOW = pltpu.SideEffectType.DATAFLOW_SIDE_EFFECTING
_CHIP_FLIPS = ((1, 0), (0, 1), (1, 1))
_DMA = pltpu.SemaphoreType.DMA


def _structs(arrs):
    return [jax.ShapeDtypeStruct(a.shape, a.dtype) for a in arrs]


def _gather_copy(src, buf, send_sems, recv_sems, k, j, slot, x, y, c):
    r2 = src.shape[0] // 2
    fx, fy = _CHIP_FLIPS[j]
    nf = len(_CHIP_FLIPS)
    return pltpu.make_async_remote_copy(
        src_ref=src.at[pl.ds(c * r2, r2)], dst_ref=buf.at[slot, pl.ds(c * r2, r2)], send_sem=send_sems.at[nf * k + j],
        recv_sem=recv_sems.at[nf * k + j], device_id=(_flip(x, fx), _flip(y, fy), c), device_id_type=MESH)


def _gather_start(srcs, groups):
    n = len(srcs)
    ng = len(groups)
    nf = len(_CHIP_FLIPS)

    def body(*refs):
        src_refs, buf_refs, sems = refs[:n], refs[n:2 * n], refs[4 * n:]
        x, y, c = lax.axis_index("x"), lax.axis_index("y"), lax.axis_index("c")
        for gi, group in enumerate(groups):
            for k, o in enumerate(group):
                for j in range(nf):
                    _gather_copy(src_refs[o], buf_refs[o], sems[2 * gi], sems[2 * gi + 1], k, j, 2 * x + y, x, y, c).start()

    inits = [jnp.broadcast_to(s[None], (N_CHIPS,) + s.shape) for s in srcs]
    sem_shapes = [_DMA((nf * len(g),)) for g in groups for _ in range(2)]
    outs = pl.pallas_call(
        body, name="gather_start", in_specs=[_ANY] * (2 * n), out_specs=[_ANY] * (2 * n) + [_SEM] * (2 * ng),
        out_shape=_structs(srcs) + _structs(inits) + sem_shapes, input_output_aliases={i: i for i in range(2 * n)},
        compiler_params=pltpu.CompilerParams(has_side_effects=_DATAFLOW),
    )(*srcs, *inits)
    return outs[:n], outs[n:2 * n], [(outs[2 * n + 2 * gi], outs[2 * n + 2 * gi + 1]) for gi in range(ng)]


def _gather_wait(srcs, bufs, sems, after, name):
    n = len(srcs)
    nf = len(_CHIP_FLIPS)

    def body(*refs):
        src_refs, buf_refs, send_sems, recv_sems = refs[:n], refs[n:2 * n], refs[2 * n], refs[2 * n + 1]
        x, y, c = lax.axis_index("x"), lax.axis_index("y"), lax.axis_index("c")
        for k in range(n):
            for j, (fx, fy) in enumerate(_CHIP_FLIPS):
                cp = _gather_copy(src_refs[k], buf_refs[k], send_sems, recv_sems, k, j, 2 * _flip(x, fx) + _flip(y, fy), x, y, c)
                cp.wait_send()
                cp.wait_recv()

    outs = pl.pallas_call(
        body, name=name, in_specs=[_ANY] * (2 * n) + [_SEM, _SEM, _ANY], out_specs=[_ANY] * (2 * n),
        out_shape=_structs(srcs) + _structs(bufs), input_output_aliases={i: i for i in range(2 * n)},
        compiler_params=pltpu.CompilerParams(has_side_effects=_DATAFLOW),
    )(*srcs, *bufs, *sems, after)
    return outs[n:]


def _gather_forward(bufs, name):
    n = len(bufs)
    nf = len(_CHIP_FLIPS)

    def body(*refs):
        outs = refs[n:2 * n]
        send_sems, recv_sems = refs[2 * n:]
        x, y, c = lax.axis_index("x"), lax.axis_index("y"), lax.axis_index("c")

        def forward(k, j, h):
            r2 = bufs[k].shape[1] // 2
            fx, fy = _CHIP_FLIPS[j]
            part = outs[k].at[2 * _flip(x, fx) + _flip(y, fy), pl.ds(h * r2, r2)]
            return pltpu.make_async_remote_copy(src_ref=part, dst_ref=part, send_sem=send_sems.at[nf * k + j],
                                                recv_sem=recv_sems.at[nf * k + j], device_id=(x, y, 1 - c), device_id_type=MESH)

        sends = [forward(k, j, c) for k in range(n) for j in range(nf)]
        for cp in sends:
            cp.start()
        for k in range(n):
            for j in range(nf):
                forward(k, j, 1 - c).wait_recv()
        for cp in sends:
            cp.wait_send()

    return pl.pallas_call(
        body, name=name, in_specs=[_ANY] * n, out_specs=[_ANY] * n, out_shape=_structs(bufs),
        input_output_aliases={i: i for i in range(n)}, scratch_shapes=[_DMA((nf * n,)), _DMA((nf * n,))],
    )(*bufs)


def _half_struct(a, lead):
    return jax.ShapeDtypeStruct(lead + (a.shape[-2] // 2, a.shape[-1]), a.dtype)


def _sibling_split(tensors, wholes, name):
    n, m = len(tensors), len(wholes)

    def body(*refs):
        ins, got = refs[:n + m], refs[n + m:2 * (n + m)]
        send_sems, recv_sems = refs[2 * (n + m):]
        x, y, c = lax.axis_index("x"), lax.axis_index("y"), lax.axis_index("c")

        def swap(i):
            src = ins[i]
            if i < n:
                r2 = tensors[i].shape[1] // 2
                src = src.at[:, pl.ds((1 - c) * r2, r2)]
            return pltpu.make_async_remote_copy(src_ref=src, dst_ref=got[i], send_sem=send_sems.at[i], recv_sem=recv_sems.at[i],
                                                device_id=(x, y, 1 - c), device_id_type=MESH)

        copies = [swap(i) for i in range(n + m)]
        for cp in copies:
            cp.start()
        for cp in copies:
            cp.wait()

    outs = pl.pallas_call(
        body, name=name, in_specs=[_ANY] * (n + m), out_specs=[_ANY] * (n + m),
        out_shape=[_half_struct(t, (N_CHIPS,)) for t in tensors] + _structs(wholes),
        scratch_shapes=[_DMA((n + m,)), _DMA((n + m,))],
    )(*tensors, *wholes)
    return outs[:n], outs[n:]


def _exchange_copy(srcs, lands, n, send_sems, recv_sems, i, j, slot, x, y, c):
    nf = len(_CHIP_FLIPS)
    px, py = _flip(x, _CHIP_FLIPS[j][0]), _flip(y, _CHIP_FLIPS[j][1])
    return pltpu.make_async_remote_copy(
        src_ref=srcs[i].at[2 * px + py] if i < n else srcs[i], dst_ref=lands[i].at[slot], send_sem=send_sems.at[nf * i + j],
        recv_sem=recv_sems.at[nf * i + j], device_id=(px, py, c), device_id_type=MESH)


def _exchange_start(parts, wholes, name):
    n, m = len(parts), len(wholes)
    nf = len(_CHIP_FLIPS)
    t = n + m
    land_structs = _structs(parts) + [jax.ShapeDtypeStruct((N_CHIPS,) + w.shape, w.dtype) for w in wholes]

    def body(*refs):
        srcs, lands, send_sems, recv_sems, token = refs[:t], refs[2 * t:3 * t], refs[3 * t], refs[3 * t + 1], refs[3 * t + 2]
        x, y, c = lax.axis_index("x"), lax.axis_index("y"), lax.axis_index("c")
        for i in range(t):
            for j in range(nf):
                _exchange_copy(srcs, lands, n, send_sems, recv_sems, i, j, 2 * x + y, x, y, c).start()
        token[...] = jnp.zeros_like(token)

    outs = pl.pallas_call(
        body, name=name, in_specs=[_ANY] * t,
        out_specs=[_ANY] * (2 * t) + [_SEM, _SEM, pl.BlockSpec(memory_space=pltpu.VMEM)],
        out_shape=_structs(parts) + _structs(wholes) + land_structs + [_DMA((nf * t,)), _DMA((nf * t,)),
                                                                        jax.ShapeDtypeStruct((SUBLANES, LANES), f32)],
        input_output_aliases={i: i for i in range(t)},
        compiler_params=pltpu.CompilerParams(has_side_effects=_DATAFLOW),
    )(*parts, *wholes)
    return outs[:t], outs[t:2 * t], (outs[2 * t], outs[2 * t + 1]), outs[2 * t + 2]


def _exchange_wait(srcs, lands, n, sems, after, name):
    t = len(srcs)
    nf = len(_CHIP_FLIPS)

    def body(*refs):
        src_refs, land_refs, send_sems, recv_sems = refs[:t], refs[t:2 * t], refs[2 * t], refs[2 * t + 1]
        x, y, c = lax.axis_index("x"), lax.axis_index("y"), lax.axis_index("c")
        for i in range(t):
            for j, (fx, fy) in enumerate(_CHIP_FLIPS):
                cp = _exchange_copy(src_refs, land_refs, n, send_sems, recv_sems, i, j, 2 * _flip(x, fx) + _flip(y, fy), x, y, c)
                cp.wait_send()
                cp.wait_recv()

    outs = pl.pallas_call(
        body, name=name, in_specs=[_ANY] * (2 * t) + [_SEM, _SEM, _ANY], out_specs=[_ANY] * (2 * t),
        out_shape=_structs(srcs) + _structs(lands), input_output_aliases={i: i for i in range(2 * t)},
        compiler_params=pltpu.CompilerParams(has_side_effects=_DATAFLOW),
    )(*srcs, *lands, *sems, after)
    return outs[:t], outs[t:]


def _sibling_join(bufs):
    flat = [(gi, l) for gi, b in enumerate(bufs) for l in range(b.shape[0])]
    n, n_buf = len(flat), len(bufs)

    def body(*refs):
        outs = refs[n_buf:2 * n_buf]
        send_sems, recv_sems = refs[2 * n_buf:]
        x, y, c = lax.axis_index("x"), lax.axis_index("y"), lax.axis_index("c")

        def push(i, h):
            gi, l = flat[i]
            r2 = bufs[gi].shape[1] // 2
            part = outs[gi].at[l, pl.ds(h * r2, r2)]
            return pltpu.make_async_remote_copy(src_ref=part, dst_ref=part, send_sem=send_sems.at[i], recv_sem=recv_sems.at[i],
                                                device_id=(x, y, 1 - c), device_id_type=MESH)

        sends = [push(i, c) for i in range(n)]
        for cp in sends:
            cp.start()
        for i in range(n):
            push(i, 1 - c).wait_recv()
        for cp in sends:
            cp.wait_send()

    dma = pltpu.SemaphoreType.DMA
    return pl.pallas_call(
        body, name="grads_sibling_join", in_specs=[_ANY] * n_buf, out_specs=[_ANY] * n_buf,
        out_shape=[jax.ShapeDtypeStruct(b.shape, b.dtype) for b in bufs],
        input_output_aliases={i: i for i in range(n_buf)},
        scratch_shapes=[dma((n,)), dma((n,))],
    )(*bufs)


def _sum_tile_rows(rows, cols):
    tr = rows
    while tr * cols > 256 * 1024 and tr % (4 * SUBLANES) == 0:
        tr //= 2
    return tr


def _pair_sum(full, got, place, name):
    nch, R, C = full.shape
    R2 = R // 2
    tr = _sum_tile_rows(R2, C)
    nb = R2 // tr

    def body(place_ref, a_ref, b_ref, o_ref):
        o_ref[...] = (a_ref[...].astype(f32) + b_ref[...].astype(f32)).astype(o_ref.dtype)

    return pl.pallas_call(
        body, name=name, out_shape=jax.ShapeDtypeStruct((nch, R2, C), full.dtype),
        grid_spec=pltpu.PrefetchScalarGridSpec(
            num_scalar_prefetch=1, grid=(nch, nb),
            in_specs=[pl.BlockSpec((None, tr, C), lambda k, i, pr: (k, pr[1] * nb + i, 0)),
                      pl.BlockSpec((None, tr, C), lambda k, i, pr: (k, i, 0))],
            out_specs=pl.BlockSpec((None, tr, C), lambda k, i, pr: (k, i, 0))),
        compiler_params=_cparams(("parallel", "parallel")),
    )(place, full, got)


def _add2(a, b, name):
    R, C = a.shape
    tr = _sum_tile_rows(R, C)

    def body(a_ref, b_ref, o_ref):
        o_ref[...] = a_ref[...] + b_ref[...]

    spec = pl.BlockSpec((tr, C), lambda i: (i, 0))
    return pl.pallas_call(body, name=name, grid=(R // tr,), in_specs=[spec, spec], out_specs=spec,
                          out_shape=jax.ShapeDtypeStruct((R, C), f32), compiler_params=_cparams(("parallel",)))(a, b)


def _chip_sum(landed, mine, place, name, into=None, layer=0, layers=1):
    nch, R2, C = landed.shape
    tr = _sum_tile_rows(R2, C)
    nb = R2 // tr

    def body(*refs):
        place_ref, l_ref, m_ref, o_ref = refs[0], refs[1], refs[2], refs[-1]
        s = jnp.where(place_ref[0] == 0, m_ref[...].astype(f32), l_ref[0].astype(f32))
        for k in range(1, nch):
            s = s + jnp.where(place_ref[0] == k, m_ref[...].astype(f32), l_ref[k].astype(f32))
        o_ref[...] = s

    in_specs = [pl.BlockSpec((nch, tr, C), lambda i, pr: (0, i, 0)),
                pl.BlockSpec((None, tr, C), lambda i, pr: (pr[0], i, 0))]
    args = [place, landed, mine]
    if into is not None:
        in_specs.append(_ANY)
        args.append(into)
    return pl.pallas_call(
        body, name=name, out_shape=jax.ShapeDtypeStruct((layers, 2 * R2, C), f32),
        grid_spec=pltpu.PrefetchScalarGridSpec(
            num_scalar_prefetch=1, grid=(nb,), in_specs=in_specs,
            out_specs=pl.BlockSpec((None, tr, C), lambda i, pr: (layer, pr[1] * nb + i, 0))),
        input_output_aliases={3: 0} if into is not None else {},
        compiler_params=_cparams(("parallel",)),
    )(*args)


def _chip_sum_rep(landed, mine, place, name):
    nch, R, C = landed.shape
    tr = _sum_tile_rows(R, C)

    def body(place_ref, l_ref, m_ref, o_ref):
        s = jnp.where(place_ref[0] == 0, m_ref[...], l_ref[0])
        for k in range(1, nch):
            s = s + jnp.where(place_ref[0] == k, m_ref[...], l_ref[k])
        o_ref[...] = s

    return pl.pallas_call(
        body, name=name, out_shape=jax.ShapeDtypeStruct((R, C), f32),
        grid_spec=pltpu.PrefetchScalarGridSpec(
            num_scalar_prefetch=1, grid=(R // tr,),
            in_specs=[pl.BlockSpec((nch, tr, C), lambda i, pr: (0, i, 0)), pl.BlockSpec((tr, C), lambda i, pr: (i, 0))],
            out_specs=pl.BlockSpec((tr, C), lambda i, pr: (i, 0))),
        compiler_params=_cparams(("parallel",)),
    )(place, landed, mine)


PACK_COLS = 1024
PACK_ROW_MULTIPLE = 64

BIG = ("ssd_w_in", "ssd_w_out", "gmlp_w_in", "gmlp_w_out", "ffn_w_gate", "ffn_w_up", "ffn_w_down", "ple_w_proj", "ple_w_gate")
SMALL_SHARDED = ("ssd_conv_w", "gmlp_b_in", "gmlp_ln_w", "gmlp_ln_b")
REPLICATED = ("norm_mix", "norm_ffn", "ssd_conv_b", "ssd_dt_bias", "ssd_a_log", "ssd_d", "ssd_norm_w", "gmlp_w_s", "gmlp_b_s",
              "ple_norm", "ple_gate_norm", "final_norm")
WEIGHTS = ("norm_mix", "norm_ffn", "ssd_w_in", "ssd_conv_w", "ssd_conv_b", "ssd_dt_bias", "ssd_a_log", "ssd_d", "ssd_norm_w", "ssd_w_out",
           "gmlp_w_in", "gmlp_b_in", "gmlp_ln_w", "gmlp_ln_b", "gmlp_w_s", "gmlp_b_s", "gmlp_w_out", "ffn_w_gate", "ffn_w_up",
           "ffn_w_down", "ple_w_proj", "ple_norm", "ple_gate_norm", "ple_w_gate", "final_norm")
COLUMN_SHARDED = ("ssd_w_in", "gmlp_w_in", "ple_w_proj")


def _pack(arrs):
    flat = jnp.concatenate([a.reshape(-1).astype(f32) for a in arrs])
    per = PACK_COLS * PACK_ROW_MULTIPLE
    n = -(-flat.shape[0] // per) * per
    return jnp.pad(flat, (0, n - flat.shape[0])).reshape(-1, PACK_COLS)


def _unpack(buf, shapes):
    flat = buf.reshape(-1)
    out, o = [], 0
    for s in shapes:
        n = math.prod(s)
        out.append(flat[o:o + n].reshape(s))
        o += n
    return out


def _chip_major(g):
    r, c4 = g.shape
    return g.reshape(r, N_CHIPS, c4 // N_CHIPS).transpose(1, 0, 2)


def _from_chip_major(g):
    k, r, c = g.shape
    return g.transpose(1, 0, 2).reshape(r, k * c)


def _adamw_nd(w, m, v, g, name):
    shp = w.shape
    two = lambda a: a.reshape(-1, shp[-1])
    return [o.reshape(shp) for o in _adamw(two(w), two(m), two(v), two(g), name)]


def kernel(x, p, norm_mix, norm_ffn, ssd_w_in, ssd_conv_w, ssd_conv_b, ssd_dt_bias, ssd_a_log, ssd_d, ssd_norm_w, ssd_w_out, gmlp_w_in, gmlp_b_in, gmlp_ln_w, gmlp_ln_b, gmlp_w_s, gmlp_b_s, gmlp_w_out, ffn_w_gate, ffn_w_up, ffn_w_down, ple_w_proj, ple_norm, ple_gate_norm, ple_w_gate, final_norm, loss_target, m_norm_mix, m_norm_ffn, m_ssd_w_in, m_ssd_conv_w, m_ssd_conv_b, m_ssd_dt_bias, m_ssd_a_log, m_ssd_d, m_ssd_norm_w, m_ssd_w_out, m_gmlp_w_in, m_gmlp_b_in, m_gmlp_ln_w, m_gmlp_ln_b, m_gmlp_w_s, m_gmlp_b_s, m_gmlp_w_out, m_ffn_w_gate, m_ffn_w_up, m_ffn_w_down, m_ple_w_proj, m_ple_norm, m_ple_gate_norm, m_ple_w_gate, m_final_norm, v_norm_mix, v_norm_ffn, v_ssd_w_in, v_ssd_conv_w, v_ssd_conv_b, v_ssd_dt_bias, v_ssd_a_log, v_ssd_d, v_ssd_norm_w, v_ssd_w_out, v_gmlp_w_in, v_gmlp_b_in, v_gmlp_ln_w, v_gmlp_ln_b, v_gmlp_w_s, v_gmlp_b_s, v_gmlp_w_out, v_ffn_w_gate, v_ffn_w_up, v_ffn_w_down, v_ple_w_proj, v_ple_norm, v_ple_gate_norm, v_ple_w_gate, v_final_norm):
    given = dict(locals())
    w = {n: given[n] for n in WEIGHTS}
    mom = {n: given["m_" + n] for n in WEIGHTS}
    var = {n: given["v_" + n] for n in WEIGHTS}
    depth = p.shape[0]
    n_ssd, n_gmlp = ssd_w_in.shape[0], gmlp_w_in.shape[0]
    inner = ssd_dt_bias.shape[1] * HEADDIM
    conv_dim = ssd_conv_b.shape[1]

    place = jnp.stack([2 * lax.axis_index("x") + lax.axis_index("y"), lax.axis_index("c")]).astype(jnp.int32)

    keys, groups = [], []
    for i in range(depth):
        j = i // 2
        names = (("ssd_w_in", j), ("ssd_w_out", j)) if i % 2 == 0 else (("gmlp_w_in", j), ("gmlp_w_out", j))
        names += (("ffn_w_gate", i), ("ffn_w_up", i), ("ffn_w_down", i), ("ple_w_proj", i), ("ple_w_gate", i))
        if i == 0:
            names = (("small", 0),) + names
        groups.append(list(range(len(keys), len(keys) + len(names))))
        keys += names
    small_shapes = [w[n].shape for n in SMALL_SHARDED]
    srcs = [_pack([w[n] for n in SMALL_SHARDED]) if n == "small" else w[n][l].astype(bf16) for n, l in keys]
    srcs, landing, gather_sems = _gather_start(srcs, groups)
    small_full = {}

    def layer_weights(i, h):
        idx = groups[i]
        got = _gather_wait([srcs[o] for o in idx], [landing[o] for o in idx], gather_sems[i], h, f"gather_wait_{i}")
        gw = dict(zip([keys[o] for o in idx], _gather_forward(got, f"gather_forward_{i}")))
        rows = lambda a: a.reshape(-1, a.shape[-1])
        j = i // 2
        if i == 0:
            by_chip = [_unpack(gw[("small", 0)][k], small_shapes) for k in range(N_CHIPS)]
            small_full.update({n: jnp.concatenate([by_chip[k][t] for k in range(N_CHIPS)], axis=-1) for t, n in enumerate(SMALL_SHARDED)})
        if i % 2 == 0:
            w_in = _from_chip_major(gw[("ssd_w_in", j)])
            mix = dict(wz=w_in[:, :inner], wxbc=w_in[:, inner:inner + conv_dim], wdt=w_in[:, inner + conv_dim:],
                       conv_w=small_full["ssd_conv_w"][j], conv_b=ssd_conv_b[j], dt_bias=ssd_dt_bias[j], a_log=ssd_a_log[j],
                       d=ssd_d[j], norm_w=ssd_norm_w[j], wout=rows(gw[("ssd_w_out", j)]))
        else:
            mix = dict(win=_from_chip_major(gw[("gmlp_w_in", j)]), b_in=small_full["gmlp_b_in"][j], ln_w=small_full["gmlp_ln_w"][j],
                       ln_b=small_full["gmlp_ln_b"][j], w_s=gmlp_w_s[j], b_s=gmlp_b_s[j], wout=rows(gw[("gmlp_w_out", j)]))
        return dict(mix=mix, ffn=dict(wg=gw[("ffn_w_gate", i)], wu=gw[("ffn_w_up", i)], wd=gw[("ffn_w_down", i)]),
                    ple=dict(wp=_from_chip_major(gw[("ple_w_proj", i)]), pn=ple_norm[i], gn=ple_gate_norm[i],
                             wgate=rows(gw[("ple_w_gate", i)])))

    rows4 = lambda a: a.reshape((N_CHIPS, a.shape[0] // N_CHIPS) + a.shape[1:])
    cut = lambda a, k: a[..., k * (a.shape[-1] // N_CHIPS):(k + 1) * (a.shape[-1] // N_CHIPS)]
    layer_grads = [None] * depth
    in_flight = [None] * depth
    pair_sums = {}

    def on_layer_grads(i, gi):
        layer_grads[i] = gi
        j = i // 2
        if i % 2 == 0:
            chunks = {("ssd_w_in", j): _chip_major(gi["mix"]["w_in"]), ("ssd_w_out", j): rows4(gi["mix"]["wout"])}
        else:
            chunks = {("gmlp_w_in", j): _chip_major(gi["mix"]["win"]), ("gmlp_w_out", j): rows4(gi["mix"]["wout"])}
        chunks.update({("ffn_w_gate", i): gi["ffn"]["wg"], ("ffn_w_up", i): gi["ffn"]["wu"], ("ffn_w_down", i): gi["ffn"]["wd"],
                       ("ple_w_proj", i): _chip_major(gi["ple"]["wp"]), ("ple_w_gate", i): rows4(gi["ple"]["wgate"])})
        wholes = []
        if i == 0:
            ssd, gml = layer_grads[0::2], layer_grads[1::2]
            small_g = dict(ssd_conv_w=jnp.stack([s["mix"]["conv_w"] for s in ssd]), gmlp_b_in=jnp.stack([s["mix"]["b_in"] for s in gml]),
                           gmlp_ln_w=jnp.stack([s["mix"]["ln_w"] for s in gml]), gmlp_ln_b=jnp.stack([s["mix"]["ln_b"] for s in gml]))
            chunks[("small", 0)] = jnp.stack([_pack([cut(small_g[n], k) for n in SMALL_SHARDED]) for k in range(N_CHIPS)])
            rep_g = dict(
                norm_mix=jnp.stack([s["mix"]["norm_mix"] for s in layer_grads]), norm_ffn=jnp.stack([s["ffn"]["norm_ffn"] for s in layer_grads]),
                ssd_conv_b=jnp.stack([s["mix"]["conv_b"] for s in ssd]), ssd_dt_bias=jnp.stack([s["mix"]["dt_bias"] for s in ssd]),
                ssd_a_log=jnp.stack([s["mix"]["a_log"] for s in ssd]), ssd_d=jnp.stack([s["mix"]["d"] for s in ssd]),
                ssd_norm_w=jnp.stack([s["mix"]["norm_w"] for s in ssd]), gmlp_w_s=jnp.stack([s["mix"]["w_s"] for s in gml]),
                gmlp_b_s=jnp.stack([s["mix"]["b_s"] for s in gml]), ple_norm=jnp.stack([s["ple"]["pn"] for s in layer_grads]),
                ple_gate_norm=jnp.stack([s["ple"]["gn"] for s in layer_grads]), final_norm=final_norm_grad[0])
            wholes = [_pack([rep_g[n] for n in REPLICATED])]
        ks = list(chunks)
        got, wgot = _sibling_split([chunks[k] for k in ks], wholes, f"grads_sibling_split_{i}")
        sums = [_pair_sum(chunks[k], b, place, f"grads_pair_sum_{i}_{t}") for t, (k, b) in enumerate(zip(ks, got))]
        wsums = [_add2(a, b, "grads_pair_sum_rep") for a, b in zip(wholes, wgot)]
        thru, lands, sems, token = _exchange_start(sums, wsums, f"grads_exchange_start_{i}")
        in_flight[i] = (ks, thru, lands, sems)
        return token

    final_norm_grad = [None]
    norms = dict(norm_mix=norm_mix, norm_ffn=norm_ffn, final_norm=final_norm)
    loss_part, grad_x, g_fn, _ = _local_step(x[0], p[:, 0], loss_target[0], norms, layer_weights, on_layer_grads, final_norm_grad)
    loss = lax.psum(loss_part, ("x", "y", "c"))

    landed = {}
    for i in reversed(range(depth)):
        ks, thru, lands, sems = in_flight[i]
        thru, lands = _exchange_wait(thru, lands, len(ks), sems, grad_x, f"grads_exchange_wait_{i}")
        landed.update(dict(zip(ks, lands[:len(ks)])))
        pair_sums.update(dict(zip(ks, thru[:len(ks)])))
        if len(lands) > len(ks):
            landed["rep"], pair_sums["rep"] = lands[-1], thru[-1]
    rep_total = _chip_sum_rep(landed["rep"], pair_sums["rep"], place, "grads_chip_sum_rep")
    bufs = []
    for n in BIG + ("small",):
        layers = w[n].shape[0] if n != "small" else 1
        buf = None
        for l in range(layers):
            buf = _chip_sum(landed[(n, l)], pair_sums[(n, l)], place, f"grads_chip_sum_{n}_{l}", into=buf, layer=l, layers=layers)
        bufs.append(buf)
    reduced = _sibling_join(bufs)

    res = {}
    for n, gsum in zip(BIG, reduced):
        res[n] = [gsum] + _adamw_nd(w[n], mom[n], var[n], gsum, "adamw_" + n)
    for names, gsum, tag in ((SMALL_SHARDED, reduced[-1][0], "adamw_small_sharded"), (REPLICATED, rep_total, "adamw_replicated")):
        packs = [gsum] + list(_adamw(_pack([w[n] for n in names]), _pack([mom[n] for n in names]), _pack([var[n] for n in names]), gsum, tag))
        per_kind = [_unpack(pk, [w[n].shape for n in names]) for pk in packs]
        for i, n in enumerate(names):
            res[n] = [per_kind[k][i] for k in range(4)]
    return (loss, grad_x[None], *[res[n][0] for n in WEIGHTS], *[res[n][1] for n in WEIGHTS],
            *[res[n][2] for n in WEIGHTS], *[res[n][3] for n in WEIGHTS])
```

```python
import functools
import math

import jax
import jax.numpy as jnp
from jax import lax
from jax.experimental import pallas as pl
from jax.experimental.pallas import tpu as pltpu

f32 = jnp.float32
bf16 = jnp.bfloat16
HI = lax.Precision.HIGHEST

LANES = 128
SUBLANES = 8
VMEM_LIMIT_BYTES = 56 * 1024 * 1024

HEADDIM = 64
STATE = 128
CHUNK = 128
CONV_K = 4
RMS_EPS = 1e-6
LN_EPS = 1e-5
ADAM_LR = 0.001
ADAM_B1 = 0.9
ADAM_B2 = 0.999
ADAM_EPS = 1e-08
ADAM_WD = 0.01
ADAM_STEP = 10

N_CHIPS = 4
N_DEV = 8
MESH = pl.DeviceIdType.MESH


def _cparams(sem):
    return pltpu.CompilerParams(dimension_semantics=sem, vmem_limit_bytes=VMEM_LIMIT_BYTES)


def _tile(n, want):
    if n <= want:
        return n
    t = want
    while n % t:
        t //= 2
    return t


def _row_spec(tm, c):
    return pl.BlockSpec((tm, c), lambda i: (i, 0))


def _full_spec(shape):
    nd = len(shape)
    return pl.BlockSpec(tuple(shape), lambda *_: (0,) * nd)


def _sigmoid(x):
    return 1.0 / (1.0 + jnp.exp(-x))


def _silu(x):
    return x * _sigmoid(x)


def _dsilu(x):
    s = _sigmoid(x)
    return s * (1.0 + x * (1.0 - s))


def _gelu(x):
    return 0.5 * x * (1.0 + lax.erf(x * (1.0 / math.sqrt(2.0))))


def _dgelu(x):
    return 0.5 * (1.0 + lax.erf(x * (1.0 / math.sqrt(2.0)))) + x * jnp.exp(-0.5 * x * x) * (1.0 / math.sqrt(2.0 * math.pi))


def _softplus(x):
    return jnp.maximum(x, 0.0) + jnp.log(1.0 + jnp.exp(-jnp.abs(x)))


def _rms(x, w, eps):
    r = lax.rsqrt(jnp.mean(x * x, axis=-1, keepdims=True) + eps)
    return x * r * w


def _rms_bwd(dy, x, w, eps):
    r = lax.rsqrt(jnp.mean(x * x, axis=-1, keepdims=True) + eps)
    xh = x * r
    g = dy * w
    dx = r * (g - xh * jnp.mean(g * xh, axis=-1, keepdims=True))
    dw = jnp.sum(dy * xh, axis=0, keepdims=True)
    return dx, dw


def _dot(a, b, dims=(((1,), (0,)), ((), ())), precision=None):
    return lax.dot_general(a, b, dims, precision=precision, preferred_element_type=f32)


NN = (((1,), (0,)), ((), ()))
NT = (((1,), (1,)), ((), ()))
TN = (((0,), (0,)), ((), ()))


def _split3(x):
    hi = x.astype(bf16)
    r1 = x - hi.astype(f32)
    mid = r1.astype(bf16)
    return hi, mid, (r1 - mid.astype(f32)).astype(bf16)


def _dot01_left(m01, x):
    mb = m01.astype(bf16)
    hi, mid, lo = _split3(x)
    return _dot(mb, hi, NN) + _dot(mb, mid, NN) + _dot(mb, lo, NN)


def _dot01_right(x, m01):
    mb = m01.astype(bf16)
    hi, mid, lo = _split3(x)
    return _dot(hi, mb, NN) + _dot(mid, mb, NN) + _dot(lo, mb, NN)


def _mm(a, b, *, mode="nn", out_dtype=f32, res=None, kbatch=False, brows=None, after=None, tm=1024, tn=1024, tk=1024, name):
    a3, b3 = a.ndim == 3, b.ndim == 3
    nb = a.shape[0] if a3 else (b.shape[0] if b3 else 1)
    ash, bsh = a.shape[-2:], b.shape[-2:]
    if brows is not None:
        bsh = (brows[1], bsh[1])
    if mode == "nn":
        M, K, N = ash[0], ash[1], bsh[1]
    elif mode == "nt":
        M, K, N = ash[0], ash[1], bsh[0]
    else:
        K, M, N = ash[0], ash[1], bsh[1]
    tm, tn, tk = _tile(M, tm), (N if N % LANES else _tile(N, tn)), (K if K % LANES else _tile(K, tk))
    b0 = 0
    if brows is not None:
        assert mode in ("nn", "nt") and bsh[0] == (K if mode == "nn" else N)
        blk = tk if mode == "nn" else tn
        while brows[0] % blk:
            blk //= 2
        assert blk % LANES == 0 or blk == brows[1]
        b0 = brows[0] // blk
        tn, tk = (tn, blk) if mode == "nn" else (blk, tk)
    nk = K // tk
    if kbatch:
        assert a3 and b3
        grid = (1, M // tm, N // tn, nb * nk)
        bi = lambda g, k: k // nk
        ki = lambda g, k: k % nk
    else:
        grid = (nb, M // tm, N // tn, nk)
        bi = lambda g, k: g
        ki = lambda g, k: k
    nsteps = grid[3]

    def spec(is3, blk, imap):
        if is3:
            return pl.BlockSpec((None,) + blk, lambda g, i, j, k: (bi(g, k),) + imap(i, j, ki(g, k)))
        return pl.BlockSpec(blk, lambda g, i, j, k: imap(i, j, ki(g, k)))

    if mode == "nn":
        a_spec = spec(a3, (tm, tk), lambda i, j, k: (i, k))
        b_spec = spec(b3, (tk, tn), lambda i, j, k: (k + b0, j))
        dims = NN
    elif mode == "nt":
        a_spec = spec(a3, (tm, tk), lambda i, j, k: (i, k))
        b_spec = spec(b3, (tn, tk), lambda i, j, k: (j + b0, k))
        dims = NT
    else:
        a_spec = spec(a3, (tk, tm), lambda i, j, k: (k, i))
        b_spec = spec(b3, (tk, tn), lambda i, j, k: (k, j))
        dims = TN
    out3 = (a3 or b3) and not kbatch
    if out3:
        o_spec = pl.BlockSpec((None, tm, tn), lambda g, i, j, k: (g, i, j))
        o_shape = jax.ShapeDtypeStruct((nb, M, N), out_dtype)
    else:
        o_spec = pl.BlockSpec((tm, tn), lambda g, i, j, k: (i, j))
        o_shape = jax.ShapeDtypeStruct((M, N), out_dtype)
    in_specs = [a_spec, b_spec]
    args = [a, b]
    if res is not None:
        in_specs.append(pl.BlockSpec((tm, tn), lambda g, i, j, k: (i, j)))
        args.append(res)
    if after is not None:
        in_specs.append(pl.BlockSpec(memory_space=pl.ANY))
        args.append(after)

    def body(*refs):
        a_ref, b_ref = refs[:2]
        r_ref = refs[2] if res is not None else None
        o_ref, acc_ref = refs[-2:]
        k = pl.program_id(3)

        @pl.when(k == 0)
        def _():
            acc_ref[...] = jnp.zeros_like(acc_ref)

        acc_ref[...] += _dot(a_ref[...].astype(bf16), b_ref[...].astype(bf16), dims)

        @pl.when(k == nsteps - 1)
        def _():
            r = acc_ref[...]
            if res is not None:
                r = r + r_ref[...]
            o_ref[...] = r.astype(o_ref.dtype)

    return pl.pallas_call(
        body, name=name, grid=grid, in_specs=in_specs, out_specs=o_spec, out_shape=o_shape,
        scratch_shapes=[pltpu.VMEM((tm, tn), f32)],
        compiler_params=_cparams(("parallel", "parallel", "parallel", "arbitrary")),
    )(*args)


def _rowcall(fn, *, name, rows, fulls, out_rows, out_accs=(), tm=512):
    S = rows[0].shape[0]
    tm = _tile(S, tm)
    n_r, n_f, n_or, n_oa = len(rows), len(fulls), len(out_rows), len(out_accs)

    def body(*refs):
        ins = [r[...] for r in refs[:n_r + n_f]]
        outs = fn(*ins)
        if not isinstance(outs, (tuple, list)):
            outs = (outs,)
        o_refs = refs[n_r + n_f:]
        for o_ref, v in zip(o_refs[:n_or], outs[:n_or]):
            o_ref[...] = v.astype(o_ref.dtype)
        if n_oa:
            first = pl.program_id(0) == 0

            @pl.when(first)
            def _():
                for o_ref, v in zip(o_refs[n_or:], outs[n_or:]):
                    o_ref[...] = v

            @pl.when(jnp.logical_not(first))
            def _():
                for o_ref, v in zip(o_refs[n_or:], outs[n_or:]):
                    o_ref[...] += v

    in_specs = [_row_spec(tm, r.shape[1]) for r in rows] + [_full_spec(f.shape) for f in fulls]
    out_specs = [_row_spec(tm, c) for c, _ in out_rows] + [_full_spec(s) for s in out_accs]
    out_shape = [jax.ShapeDtypeStruct((S, c), d) for c, d in out_rows] + [jax.ShapeDtypeStruct(s, f32) for s in out_accs]
    res = pl.pallas_call(
        body, name=name, grid=(S // tm,), in_specs=in_specs, out_specs=out_specs, out_shape=out_shape,
        compiler_params=_cparams(("arbitrary",) if n_oa else ("parallel",)),
    )(*rows, *fulls)
    return res


def _row2(v):
    return v.reshape(1, -1)


def _rms_fwd(h, w, name):
    D = h.shape[1]
    return _rowcall(lambda x, w_: _rms(x, w_, RMS_EPS), name=name, rows=[h], fulls=[_row2(w)], out_rows=[(D, bf16)])[0]


def _conv_fwd(xpre, w, b, name):
    S, C = xpre.shape
    tm, tc = _tile(S, 512), _tile(C, 1024)
    hb = tm // SUBLANES

    def body(x_ref, halo_ref, w_ref, b_ref, c_ref, o_ref):
        i = pl.program_id(1)
        x = x_ref[...]
        halo = jnp.where(i > 0, halo_ref[...], 0.0)
        row = lax.broadcasted_iota(jnp.int32, x.shape, 0)
        row8 = lax.broadcasted_iota(jnp.int32, halo.shape, 0)
        x0 = x[0:SUBLANES, :]
        acc = x * w_ref[CONV_K - 1:CONV_K, :] + b_ref[...]
        acc0 = x0 * w_ref[CONV_K - 1:CONV_K, :] + b_ref[...]
        for k in range(1, CONV_K):
            wk = w_ref[CONV_K - 1 - k:CONV_K - k, :]
            acc = acc + pltpu.roll(x, k, axis=0) * wk
            acc0 = acc0 + jnp.where(row8 < k, pltpu.roll(halo, k, axis=0), pltpu.roll(x0, k, axis=0)) * wk
        c_ref[...] = acc
        o_ref[...] = _silu(acc)
        c_ref[0:SUBLANES, :] = acc0
        o_ref[0:SUBLANES, :] = _silu(acc0)

    return pl.pallas_call(
        body, name=name, grid=(C // tc, S // tm),
        in_specs=[pl.BlockSpec((tm, tc), lambda j, i: (i, j)),
                  pl.BlockSpec((SUBLANES, tc), lambda j, i: (jnp.maximum(i * hb - 1, 0), j)),
                  pl.BlockSpec((CONV_K, tc), lambda j, i: (0, j)),
                  pl.BlockSpec((1, tc), lambda j, i: (0, j))],
        out_specs=[pl.BlockSpec((tm, tc), lambda j, i: (i, j))] * 2,
        out_shape=[jax.ShapeDtypeStruct((S, C), f32)] * 2,
        compiler_params=_cparams(("parallel", "parallel")),
    )(xpre, xpre, w, _row2(b))


def _conv_bwd_dc(dxbc, c, xpre, name):
    S, C = xpre.shape
    tm, tc = _tile(S, 512), _tile(C, 1024)
    hb = tm // SUBLANES

    def body(d_ref, c_ref, x_ref, halo_ref, dc_ref, dw_ref, db_ref):
        i = pl.program_id(1)
        x = x_ref[...]
        dc = d_ref[...] * _dsilu(c_ref[...])
        dc_ref[...] = dc
        halo = jnp.where(i > 0, halo_ref[...], 0.0)
        row = lax.broadcasted_iota(jnp.int32, x.shape, 0)
        row8 = lax.broadcasted_iota(jnp.int32, halo.shape, 0)
        x0 = x[0:SUBLANES, :]
        dc0 = dc[0:SUBLANES, :]
        parts = [jnp.sum(dc * x, axis=0, keepdims=True)]
        for k in range(1, CONV_K):
            xs_big = jnp.where(row < SUBLANES, 0.0, pltpu.roll(x, k, axis=0))
            xs0 = jnp.where(row8 < k, pltpu.roll(halo, k, axis=0), pltpu.roll(x0, k, axis=0))
            parts.append(jnp.sum(dc * xs_big, axis=0, keepdims=True) + jnp.sum(dc0 * xs0, axis=0, keepdims=True))
        dw = jnp.concatenate([parts[CONV_K - 1 - k] for k in range(CONV_K)] + [jnp.zeros((SUBLANES - CONV_K, x.shape[1]), f32)], axis=0)
        db = jnp.sum(dc, axis=0, keepdims=True)

        @pl.when(i == 0)
        def _():
            dw_ref[...] = dw
            db_ref[...] = db

        @pl.when(i > 0)
        def _():
            dw_ref[...] += dw
            db_ref[...] += db

    return pl.pallas_call(
        body, name=name, grid=(C // tc, S // tm),
        in_specs=[pl.BlockSpec((tm, tc), lambda j, i: (i, j))] * 3 +
                 [pl.BlockSpec((SUBLANES, tc), lambda j, i: (jnp.maximum(i * hb - 1, 0), j))],
        out_specs=[pl.BlockSpec((tm, tc), lambda j, i: (i, j)),
                   pl.BlockSpec((SUBLANES, tc), lambda j, i: (0, j)),
                   pl.BlockSpec((1, tc), lambda j, i: (0, j))],
        out_shape=[jax.ShapeDtypeStruct((S, C), f32), jax.ShapeDtypeStruct((SUBLANES, C), f32), jax.ShapeDtypeStruct((1, C), f32)],
        compiler_params=_cparams(("parallel", "arbitrary")),
    )(dxbc, c, xpre, xpre)


def _conv_bwd_dx(dc, w, name):
    S, C = dc.shape
    tm, tc = _tile(S, 512), _tile(C, 1024)
    hb = tm // SUBLANES
    nrow = S // tm
    last8 = S // SUBLANES - 1

    def body(d_ref, nxt_ref, w_ref, o_ref):
        i = pl.program_id(1)
        d = d_ref[...]
        nxt = jnp.where(i < nrow - 1, nxt_ref[...], 0.0)
        row8 = lax.broadcasted_iota(jnp.int32, nxt.shape, 0)
        dl = d[tm - SUBLANES:tm, :]
        acc = d * w_ref[CONV_K - 1:CONV_K, :]
        accl = dl * w_ref[CONV_K - 1:CONV_K, :]
        for j in range(1, CONV_K):
            wk = w_ref[CONV_K - 1 - j:CONV_K - j, :]
            acc = acc + pltpu.roll(d, tm - j, axis=0) * wk
            accl = accl + jnp.where(row8 >= SUBLANES - j, pltpu.roll(nxt, SUBLANES - j, axis=0), pltpu.roll(dl, SUBLANES - j, axis=0)) * wk
        o_ref[...] = acc.astype(o_ref.dtype)
        o_ref[tm - SUBLANES:tm, :] = accl.astype(o_ref.dtype)

    return pl.pallas_call(
        body, name=name, grid=(C // tc, nrow),
        in_specs=[pl.BlockSpec((tm, tc), lambda j, i: (i, j)),
                  pl.BlockSpec((SUBLANES, tc), lambda j, i: (jnp.minimum((i + 1) * hb, last8), j)),
                  pl.BlockSpec((CONV_K, tc), lambda j, i: (0, j))],
        out_specs=pl.BlockSpec((tm, tc), lambda j, i: (i, j)),
        out_shape=jax.ShapeDtypeStruct((S, C), f32),
        compiler_params=_cparams(("parallel", "parallel")),
    )(dc, dc, w)


def _halfsum(v, lane_lo):
    s0 = jnp.sum(jnp.where(lane_lo, v, 0.0), axis=1, keepdims=True)
    s1 = jnp.sum(jnp.where(lane_lo, 0.0, v), axis=1, keepdims=True)
    return jnp.where(lane_lo, s0, s1)


def _ssd_specs(S, inner, GN, nchunks, rev):
    L = CHUNK
    cm = (lambda c: nchunks - 1 - c) if rev else (lambda c: c)
    xs = pl.BlockSpec((L, inner), lambda c: (cm(c), 0))
    bb = pl.BlockSpec((L, GN), lambda c: (cm(c), inner // GN))
    cc = pl.BlockSpec((L, GN), lambda c: (cm(c), inner // GN + 1))
    row = pl.BlockSpec((L, inner), lambda c: (cm(c), 0))
    vec = pl.BlockSpec((1, inner), lambda c: (0, 0))
    st = pl.BlockSpec((None, inner, STATE), lambda c: (cm(c), 0, 0))
    return xs, bb, cc, row, vec, st


def _ssd_fwd(xbc, dtx, ax, dx, G, name):
    S, inner = dtx.shape
    GN = G * STATE
    L = CHUNK
    nchunks = S // L
    npairs = inner // LANES
    ppg = npairs // G
    assert inner % GN == 0 and L == LANES and STATE == LANES

    def body(xs_ref, b_ref, c_ref, dtx_ref, ax_ref, dx_ref, y_ref, so_ref, st_ref):
        ci = pl.program_id(0)

        @pl.when(ci == 0)
        def _():
            st_ref[...] = jnp.zeros_like(st_ref)

        r = lax.broadcasted_iota(jnp.int32, (L, L), 0)
        cidx = lax.broadcasted_iota(jnp.int32, (L, L), 1)
        tril = cidx <= r
        lane_lo = cidx < HEADDIM
        xs = xs_ref[...]
        dtv = dtx_ref[...]
        X = xs * dtv
        da = dtv * ax_ref[...]
        cs = _dot01_left(tril, da)
        cs_last = jnp.sum(da, axis=0, keepdims=True)
        so_ref[...] = st_ref[...]
        for g in range(G):
            Bg = b_ref[:, g * STATE:(g + 1) * STATE].astype(bf16)
            Cg = c_ref[:, g * STATE:(g + 1) * STATE].astype(bf16)
            CB = _dot(Cg, Bg, NT)
            for j in range(ppg):
                lo = (g * ppg + j) * LANES
                tile = cs[:, lo:lo + LANES]
                rl = pltpu.roll(tile, HEADDIM, axis=1)
                Xp = X[:, lo:lo + LANES]
                prev = st_ref[lo:lo + LANES, :]
                ypair = _dot(Cg, prev.astype(bf16), NT) * jnp.exp(tile)
                for half in (0, 1):
                    hm = lane_lo if half == 0 else jnp.logical_not(lane_lo)
                    colb = jnp.where(hm, tile, rl)
                    Lm = jnp.exp(jnp.where(tril, colb - colb.T, -1e30))
                    W = (CB * Lm).astype(bf16)
                    ypair = ypair + _dot(W, jnp.where(hm, Xp, 0.0).astype(bf16), NN)
                y_ref[:, lo:lo + LANES] = ypair + xs[:, lo:lo + LANES] * dx_ref[:, lo:lo + LANES]
                last = cs_last[:, lo:lo + LANES]
                snew = _dot((Xp * jnp.exp(last - tile)).astype(bf16), Bg, TN)
                dec_rows = jnp.broadcast_to(jnp.exp(last), (L, LANES)).T
                st_ref[lo:lo + LANES, :] = dec_rows * prev + snew

    xs_s, b_s, c_s, row_s, vec_s, st_s = _ssd_specs(S, inner, GN, nchunks, False)
    return pl.pallas_call(
        body, name=name, grid=(nchunks,),
        in_specs=[xs_s, b_s, c_s, row_s, vec_s, vec_s],
        out_specs=[row_s, st_s],
        out_shape=[jax.ShapeDtypeStruct((S, inner), f32), jax.ShapeDtypeStruct((nchunks, inner, STATE), f32)],
        scratch_shapes=[pltpu.VMEM((inner, STATE), f32)],
        compiler_params=_cparams(("arbitrary",)),
    )(xbc, xbc, xbc, dtx, ax, dx)


def _ssd_bwd(dy, y, xbc, dtx, ax, dx, states, et, G, name):
    S, inner = dtx.shape
    H = et.shape[1]
    GN = G * STATE
    Cc = inner + 2 * GN
    L = CHUNK
    nchunks = S // L
    npairs = inner // LANES
    ppg = npairs // G

    def body(dy_ref, y_ref, xs_ref, b_ref, c_ref, dtx_ref, ax_ref, dx_ref, si_ref, et_ref,
             dxbc_ref, ddt_ref, dax_ref, ddx_ref, dst_ref, dA_ref, dAl_ref, ddtp_ref):
        ci = pl.program_id(0)

        @pl.when(ci == 0)
        def _():
            dst_ref[...] = jnp.zeros_like(dst_ref)
            dax_ref[...] = jnp.zeros_like(dax_ref)
            ddx_ref[...] = jnp.zeros_like(ddx_ref)

        r = lax.broadcasted_iota(jnp.int32, (L, L), 0)
        cidx = lax.broadcasted_iota(jnp.int32, (L, L), 1)
        tril = cidx <= r
        lane_lo = cidx < HEADDIM
        lane_lo1 = lax.broadcasted_iota(jnp.int32, (1, LANES), 1) < HEADDIM
        xs = xs_ref[...]
        dtv = dtx_ref[...]
        dyv = dy_ref[...]
        X = xs * dtv
        da = dtv * ax_ref[...]
        cs = _dot01_left(tril, da)
        cs_last = jnp.sum(da, axis=0, keepdims=True)
        for g in range(G):
            Bg = b_ref[:, g * STATE:(g + 1) * STATE].astype(bf16)
            Cg = c_ref[:, g * STATE:(g + 1) * STATE].astype(bf16)
            CB = _dot(Cg, Bg, NT)
            dCB = jnp.zeros((L, L), f32)
            dBg = jnp.zeros((L, STATE), f32)
            dCg = jnp.zeros((L, STATE), f32)
            for j in range(ppg):
                lo = (g * ppg + j) * LANES
                tile = cs[:, lo:lo + LANES]
                rl = pltpu.roll(tile, HEADDIM, axis=1)
                eA = jnp.exp(tile)
                Xp = X[:, lo:lo + LANES]
                dYp = dyv[:, lo:lo + LANES]
                xsp = xs[:, lo:lo + LANES]
                prev = si_ref[lo:lo + LANES, :]
                dSn = dst_ref[lo:lo + LANES, :]
                prev_b = prev.astype(bf16)
                dSn_b = dSn.astype(bf16)
                dYe = (dYp * eA).astype(bf16)
                dCg = dCg + _dot(dYe, prev_b, NN)
                dprev = _dot(dYe, Cg, TN)
                last = cs_last[:, lo:lo + LANES]
                w = jnp.exp(last - tile)
                BdS = _dot(Bg, dSn_b, NT)
                Xw = Xp * w
                XwB = Xw * BdS
                dAl_t = _halfsum(jnp.sum(XwB, axis=0, keepdims=True), lane_lo1)
                dBg = dBg + _dot(Xw.astype(bf16), dSn_b, NN)
                dec_rows = jnp.broadcast_to(jnp.exp(last), (L, LANES)).T
                dprev = dprev + dec_rows * dSn
                rsum = jnp.sum(dSn * prev * dec_rows, axis=1, keepdims=True)
                s0 = jnp.sum(rsum[0:HEADDIM], axis=0, keepdims=True)
                s1 = jnp.sum(rsum[HEADDIM:LANES], axis=0, keepdims=True)
                dAl_t = dAl_t + jnp.where(lane_lo1, s0, s1)
                dXd = jnp.zeros((L, LANES), f32)
                for half in (0, 1):
                    hm = lane_lo if half == 0 else jnp.logical_not(lane_lo)
                    colb = jnp.where(hm, tile, rl)
                    Lm = jnp.exp(jnp.where(tril, colb - colb.T, -1e30))
                    dYh = jnp.where(hm, dYp, 0.0).astype(bf16)
                    dW = _dot(dYh, jnp.where(hm, Xp, 0.0).astype(bf16), NT)
                    dXd = dXd + _dot((CB * Lm).astype(bf16), dYh, TN)
                    dCB = dCB + dW * Lm
                yoff = _dot(Cg, prev_b, NT) * eA
                ydiag = y_ref[:, lo:lo + LANES] - xsp * dx_ref[:, lo:lo + LANES] - yoff
                dYb = dYp.astype(bf16).astype(f32)
                Xb = Xp.astype(bf16).astype(f32)
                dA_t = _halfsum(dYb * ydiag - Xb * dXd + dYp * yoff - XwB, lane_lo)
                dXp = w * BdS + dXd
                dxbc_ref[:, lo:lo + LANES] = dXp * dtv[:, lo:lo + LANES] + dYp * dx_ref[:, lo:lo + LANES]
                ddtp_ref[:, lo:lo + LANES] = dXp * xsp
                ddx_ref[:, lo:lo + LANES] += jnp.sum(dYp * xsp, axis=0, keepdims=True)
                dA_ref[:, lo:lo + LANES] = dA_t
                dAl_ref[:, lo:lo + LANES] = dAl_t
                dst_ref[lo:lo + LANES, :] = dprev
            dCBb = dCB.astype(bf16)
            dxbc_ref[:, inner + g * STATE:inner + (g + 1) * STATE] = dBg + _dot(dCBb, Cg, TN)
            dxbc_ref[:, inner + GN + g * STATE:inner + GN + (g + 1) * STATE] = dCg + _dot(dCBb, Bg, NN)
        dda = _dot01_left(cidx >= r, dA_ref[...]) + dAl_ref[...]
        ddt_full = ddtp_ref[...] + dda * ax_ref[...] * (1.0 / HEADDIM)
        ddt_ref[...] = _dot01_right(ddt_full, et_ref[...])
        dax_ref[...] += jnp.sum(dda * dtv, axis=0, keepdims=True)

    xs_s, b_s, c_s, row_s, vec_s, st_s = _ssd_specs(S, inner, GN, nchunks, True)
    return pl.pallas_call(
        body, name=name, grid=(nchunks,),
        in_specs=[row_s, row_s, xs_s, b_s, c_s, row_s, vec_s, vec_s, st_s, _full_spec(et.shape)],
        out_specs=[pl.BlockSpec((L, Cc), lambda c: (nchunks - 1 - c, 0)),
                   pl.BlockSpec((L, H), lambda c: (nchunks - 1 - c, 0)), vec_s, vec_s],
        out_shape=[jax.ShapeDtypeStruct((S, Cc), f32), jax.ShapeDtypeStruct((S, H), f32),
                   jax.ShapeDtypeStruct((1, inner), f32), jax.ShapeDtypeStruct((1, inner), f32)],
        scratch_shapes=[pltpu.VMEM((inner, STATE), f32), pltpu.VMEM((L, inner), f32),
                        pltpu.VMEM((1, inner), f32), pltpu.VMEM((L, inner), f32)],
        compiler_params=_cparams(("arbitrary",)),
    )(dy, y, xbc, xbc, xbc, dtx, ax, dx, states, et)


def _dt_fwd(dt_pre, bias, e, name):
    H, inner = e.shape

    def fn(dp, b, e_):
        dt = _softplus(dp + b)
        return dt, _dot01_right(dt, e_)

    return _rowcall(fn, name=name, rows=[dt_pre], fulls=[_row2(bias), e], out_rows=[(H, f32), (inner, f32)])


def _dt_bwd(ddt, dt_pre, bias, name):
    H = ddt.shape[1]

    def fn(dd, dp, b):
        g = dd * _sigmoid(dp + b)
        return g, jnp.sum(g, axis=0, keepdims=True)

    return _rowcall(fn, name=name, rows=[ddt, dt_pre], fulls=[_row2(bias)], out_rows=[(H, f32)], out_accs=[(1, H)])


def _gnorm_fwd(y, z, w, G, name):
    inner = y.shape[1]
    gs = inner // G

    def fn(y_, z_, w_):
        gg = y_ * _silu(z_)
        outs = []
        for g in range(G):
            sl = slice(g * gs, (g + 1) * gs)
            outs.append(_rms(gg[:, sl], w_[:, sl], LN_EPS))
        return jnp.concatenate(outs, axis=1)

    return _rowcall(fn, name=name, rows=[y, z], fulls=[_row2(w)], out_rows=[(inner, bf16)], tm=256)[0]


def _gnorm_bwd(dyn, y, z, w, G, name):
    inner = y.shape[1]
    gs = inner // G

    def fn(d_, y_, z_, w_):
        sz = _silu(z_)
        gg = y_ * sz
        dgs, dws = [], []
        for g in range(G):
            sl = slice(g * gs, (g + 1) * gs)
            dg, dw = _rms_bwd(d_[:, sl], gg[:, sl], w_[:, sl], LN_EPS)
            dgs.append(dg)
            dws.append(dw)
        dgg = jnp.concatenate(dgs, axis=1)
        return dgg * sz, dgg * y_ * _dsilu(z_), jnp.concatenate(dws, axis=1)

    return _rowcall(fn, name=name, rows=[dyn, y, z], fulls=[_row2(w)], out_rows=[(inner, f32), (inner, f32)],
                    out_accs=[(1, inner)], tm=256)


def _gmlp_parts(pre, lw, lb, I):
    hp = _gelu(pre)
    uu = hp[:, :I]
    vp = hp[:, I:]
    xc = vp - jnp.mean(vp, axis=-1, keepdims=True)
    rstd = lax.rsqrt(jnp.mean(xc * xc, axis=-1, keepdims=True) + LN_EPS)
    vhat = xc * rstd
    return uu, vhat, rstd, vhat * lw + lb


def _gmlp_mid_fwd(pre, b_in, ln_w, ln_b, w_s, bsx, name):
    S, two_i = pre.shape
    I = two_i // 2
    NG = w_s.shape[0]
    gd = I // NG
    L = CHUNK

    def body(pre_ref, bi_ref, lw_ref, lb_ref, ws_ref, bsx_ref, o_ref):
        uu, _, _, vv = _gmlp_parts(pre_ref[...] + bi_ref[...], lw_ref[...], lb_ref[...], I)
        r = lax.broadcasted_iota(jnp.int32, (L, L), 0)
        cidx = lax.broadcasted_iota(jnp.int32, (L, L), 1)
        tril = cidx <= r
        for g in range(NG):
            sl = slice(g * gd, (g + 1) * gd)
            wg = jnp.where(tril, ws_ref[g], 0.0).astype(bf16)
            mixed = _dot(wg, vv[:, sl].astype(bf16), NN) + bsx_ref[:, sl]
            o_ref[:, sl] = (uu[:, sl] * mixed).astype(o_ref.dtype)

    return pl.pallas_call(
        body, name=name, grid=(S // L,),
        in_specs=[_row_spec(L, two_i), _full_spec((1, two_i)), _full_spec((1, I)), _full_spec((1, I)), _full_spec(w_s.shape), _full_spec(bsx.shape)],
        out_specs=_row_spec(L, I), out_shape=jax.ShapeDtypeStruct((S, I), bf16),
        compiler_params=_cparams(("parallel",)),
    )(pre, _row2(b_in), _row2(ln_w), _row2(ln_b), w_s, bsx)


def _gmlp_mid_bwd(do, pre, b_in, ln_w, ln_b, w_s, bsx, name):
    S, two_i = pre.shape
    I = two_i // 2
    NG = w_s.shape[0]
    gd = I // NG
    L = CHUNK

    def body(do_ref, pre_ref, bi_ref, lw_ref, lb_ref, ws_ref, bsx_ref, dpre_ref, dbi_ref, dlw_ref, dlb_ref, dws_ref, dbs_ref, dvv_ref):
        ci = pl.program_id(0)

        @pl.when(ci == 0)
        def _():
            for ref in (dbi_ref, dlw_ref, dlb_ref, dws_ref, dbs_ref):
                ref[...] = jnp.zeros_like(ref)

        pre = pre_ref[...] + bi_ref[...]
        lw = lw_ref[...]
        uu, vhat, rstd, vv = _gmlp_parts(pre, lw, lb_ref[...], I)
        dov = do_ref[...]
        r = lax.broadcasted_iota(jnp.int32, (L, L), 0)
        cidx = lax.broadcasted_iota(jnp.int32, (L, L), 1)
        tril = cidx <= r
        duus = []
        for g in range(NG):
            sl = slice(g * gd, (g + 1) * gd)
            wg = jnp.where(tril, ws_ref[g], 0.0).astype(bf16)
            vg = vv[:, sl].astype(bf16)
            mixed = _dot(wg, vg, NN) + bsx_ref[:, sl]
            duus.append(dov[:, sl] * mixed)
            dmixed = dov[:, sl] * uu[:, sl]
            dbs_ref[:, sl] += dmixed
            dmb = dmixed.astype(bf16)
            dvv_ref[:, sl] = _dot(wg, dmb, TN)
            dws_ref[g] += jnp.where(tril, _dot(dmb, vg, NT), 0.0)
        duu = jnp.concatenate(duus, axis=1)
        dvv = dvv_ref[...]
        dlw_ref[...] += jnp.sum(dvv * vhat, axis=0, keepdims=True)
        dlb_ref[...] += jnp.sum(dvv, axis=0, keepdims=True)
        dvh = dvv * lw
        dvp = rstd * (dvh - jnp.mean(dvh, axis=-1, keepdims=True) - vhat * jnp.mean(dvh * vhat, axis=-1, keepdims=True))
        dpre = jnp.concatenate([duu, dvp], axis=1) * _dgelu(pre)
        dbi_ref[...] += jnp.sum(dpre, axis=0, keepdims=True)
        dpre_ref[...] = dpre.astype(dpre_ref.dtype)

    return pl.pallas_call(
        body, name=name, grid=(S // L,),
        in_specs=[_row_spec(L, I), _row_spec(L, two_i), _full_spec((1, two_i)), _full_spec((1, I)), _full_spec((1, I)),
                  _full_spec(w_s.shape), _full_spec(bsx.shape)],
        out_specs=[_row_spec(L, two_i), _full_spec((1, two_i)), _full_spec((1, I)), _full_spec((1, I)), _full_spec(w_s.shape), _full_spec((L, I))],
        out_shape=[jax.ShapeDtypeStruct((S, two_i), bf16), jax.ShapeDtypeStruct((1, two_i), f32), jax.ShapeDtypeStruct((1, I), f32),
                   jax.ShapeDtypeStruct((1, I), f32), jax.ShapeDtypeStruct(w_s.shape, f32), jax.ShapeDtypeStruct((L, I), f32)],
        scratch_shapes=[pltpu.VMEM((L, I), f32)],
        compiler_params=_cparams(("arbitrary",)),
    )(do, pre, _row2(b_in), _row2(ln_w), _row2(ln_b), w_s, bsx)


def _lane_group_sum(acc, eg, name):
    NG = eg.shape[1]
    return _rowcall(lambda a, e: _dot(a, e, NN, HI), name=name, rows=[acc], fulls=[eg], out_rows=[(NG, f32)])[0]


def _ffn_fwd_fused(h1, nf_w, wg, wu, wd, name):
    S, D = h1.shape
    nb, F4, _ = wg.shape
    tm = _tile(S, 512)

    def body(h_ref, nf_ref, wg_ref, wu_ref, wd_ref, h2_ref, u_ref, g_ref, up_ref, a_ref, acc_ref):
        k = pl.program_id(1)

        @pl.when(k == 0)
        def _():
            u_ref[...] = _rms(h_ref[...], nf_ref[...], RMS_EPS).astype(u_ref.dtype)

        uv = u_ref[...]
        g = _dot(uv, wg_ref[...], NT)
        up = _dot(uv, wu_ref[...], NT)
        a = (_silu(g) * up).astype(bf16)
        g_ref[...] = g.astype(g_ref.dtype)
        up_ref[...] = up.astype(up_ref.dtype)
        a_ref[...] = a
        part = _dot(a, wd_ref[...], NN)

        @pl.when(k == 0)
        def _():
            acc_ref[...] = part

        @pl.when(k > 0)
        def _():
            acc_ref[...] += part

        @pl.when(k == nb - 1)
        def _():
            h2_ref[...] = h_ref[...] + acc_ref[...]

    row = pl.BlockSpec((tm, D), lambda i, k: (i, 0))
    wspec = pl.BlockSpec((None, F4, D), lambda i, k: (k, 0, 0))
    cspec = pl.BlockSpec((None, tm, F4), lambda i, k: (k, i, 0))
    chunk = jax.ShapeDtypeStruct((nb, S, F4), bf16)
    return pl.pallas_call(
        body, name=name, grid=(S // tm, nb),
        in_specs=[row, _full_spec((1, D)), wspec, wspec, pl.BlockSpec((None, F4, D), lambda i, k: (k, 0, 0))],
        out_specs=[row, row, cspec, cspec, cspec],
        out_shape=[jax.ShapeDtypeStruct((S, D), f32), jax.ShapeDtypeStruct((S, D), bf16), chunk, chunk, chunk],
        scratch_shapes=[pltpu.VMEM((tm, D), f32)],
        compiler_params=_cparams(("parallel", "arbitrary")),
    )(h1, _row2(nf_w), wg, wu, wd)


def _ffn_bwd_fused(dh, h1, nf_w, wd, wg, wu, G, U, name, after=None):
    S, D = dh.shape
    nb, F4, _ = wd.shape
    tm = _tile(S, 512)

    def body(dh_ref, h_ref, nf_ref, wd_ref, wg_ref, wu_ref, g_ref, up_ref, *rest):
        dg_ref, du_ref, dh1_ref, dnf_ref, acc_ref = rest[-5:]
        i, k = pl.program_id(0), pl.program_id(1)
        dA = _dot(dh_ref[...].astype(bf16), wd_ref[...], NT)
        g = g_ref[...].astype(f32)
        dg = (dA * up_ref[...].astype(f32) * _dsilu(g)).astype(bf16)
        du = (dA * _silu(g)).astype(bf16)
        dg_ref[...] = dg
        du_ref[...] = du
        part = _dot(dg, wg_ref[...], NN) + _dot(du, wu_ref[...], NN)

        @pl.when(k == 0)
        def _():
            acc_ref[...] = part

        @pl.when(k > 0)
        def _():
            acc_ref[...] += part

        @pl.when(k == nb - 1)
        def _():
            dx, dw = _rms_bwd(acc_ref[...], h_ref[...], nf_ref[...], RMS_EPS)
            dh1_ref[...] = dh_ref[...] + dx

            @pl.when(i == 0)
            def _():
                dnf_ref[...] = dw

            @pl.when(i > 0)
            def _():
                dnf_ref[...] += dw

    row = pl.BlockSpec((tm, D), lambda i, k: (i, 0))
    wspec = pl.BlockSpec((None, F4, D), lambda i, k: (k, 0, 0))
    cspec = pl.BlockSpec((None, tm, F4), lambda i, k: (k, i, 0))
    chunk = jax.ShapeDtypeStruct((nb, S, F4), bf16)
    return pl.pallas_call(
        body, name=name, grid=(S // tm, nb),
        in_specs=[row, row, _full_spec((1, D)), wspec, wspec, wspec, cspec, cspec] + ([] if after is None else [pl.BlockSpec(memory_space=pl.ANY)]),
        out_specs=[cspec, cspec, row, _full_spec((1, D))],
        out_shape=[chunk, chunk, jax.ShapeDtypeStruct((S, D), f32), jax.ShapeDtypeStruct((1, D), f32)],
        scratch_shapes=[pltpu.VMEM((tm, D), f32)],
        compiler_params=_cparams(("arbitrary", "arbitrary")),
    )(dh, h1, _row2(nf_w), wd, wg, wu, G, U, *([] if after is None else [after]))


def _rms_bwd_add(dres, du, h, w, name):
    D = h.shape[1]

    def fn(dr, du_, h_, w_):
        dx, dw = _rms_bwd(du_, h_, w_, RMS_EPS)
        return dr + dx, dw

    return _rowcall(fn, name=name, rows=[dres, du, h], fulls=[_row2(w)], out_rows=[(D, f32)], out_accs=[(1, D)])


def _ple_fwd(h, p_i, wp, pn, gn, wgate, name):
    D = h.shape[1]

    def fn(h_, p_, wp_, pn_, gn_, wg_):
        pe = _dot(p_.astype(bf16), wp_, NN)
        e = _rms(pe, pn_, RMS_EPS)
        q = _rms(h_, gn_, RMS_EPS)
        gate = _sigmoid(_dot(q.astype(bf16), wg_, NN))
        return h_ + gate * e, pe, gate

    return _rowcall(fn, name=name, rows=[h, p_i], fulls=[wp, _row2(pn), _row2(gn), wgate],
                    out_rows=[(D, f32), (D, f32), (D, f32)], tm=256)


def _ple_bwd(dh3, h, pe, gate, pn, gn, wgate, name, after=None):
    D = h.shape[1]

    def fn(d_, h_, pe_, gate_, pn_, gn_, wg_, *_):
        e = _rms(pe_, pn_, RMS_EPS)
        dzg = d_ * e * gate_ * (1.0 - gate_)
        dq = _dot(dzg.astype(bf16), wg_, NT)
        dxq, dgn = _rms_bwd(dq, h_, gn_, RMS_EPS)
        dpe, dpn = _rms_bwd(d_ * gate_, pe_, pn_, RMS_EPS)
        return d_ + dxq, dzg, dpe, _rms(h_, gn_, RMS_EPS), dpn, dgn

    return _rowcall(fn, name=name, rows=[dh3, h, pe, gate], fulls=[_row2(pn), _row2(gn), wgate] + ([] if after is None else [after]),
                    out_rows=[(D, f32), (D, bf16), (D, bf16), (D, bf16)], out_accs=[(1, D), (1, D)], tm=256)


def _loss_head(h, target, fn_w, name):
    D = h.shape[1]

    def fn(h_, t_, w_):
        diff = _rms(h_, w_, RMS_EPS) - t_
        loss = 0.5 * jnp.sum(jnp.mean(diff * diff, axis=-1, keepdims=True), axis=0, keepdims=True)
        dh, dw = _rms_bwd(diff * (1.0 / D), h_, w_, RMS_EPS)
        return dh, jnp.broadcast_to(loss, (1, LANES)), dw

    return _rowcall(fn, name=name, rows=[h, target], fulls=[_row2(fn_w)], out_rows=[(D, f32)], out_accs=[(1, LANES), (1, D)])


def _adamw(w, m, v, g, name):
    R, C = w.shape
    tr, tc = R, C
    while tr * tc > 256 * 1024 and tr % (2 * SUBLANES) == 0:
        tr //= 2
    while tr * tc > 256 * 1024 and tc % (2 * LANES) == 0:
        tc //= 2

    def body(w_ref, m_ref, v_ref, g_ref, d_ref, mo_ref, vo_ref):
        g = g_ref[...]
        mn = ADAM_B1 * m_ref[...] + (1.0 - ADAM_B1) * g
        vn = ADAM_B2 * v_ref[...] + (1.0 - ADAM_B2) * (g * g)
        m_hat = mn / (1.0 - ADAM_B1 ** ADAM_STEP)
        v_hat = vn / (1.0 - ADAM_B2 ** ADAM_STEP)
        d_ref[...] = -ADAM_LR * (m_hat / (jnp.sqrt(v_hat) + ADAM_EPS) + ADAM_WD * w_ref[...])
        mo_ref[...] = mn
        vo_ref[...] = vn

    spec = pl.BlockSpec((tr, tc), lambda i, j: (i, j))
    return pl.pallas_call(
        body, name=name, grid=(R // tr, C // tc), in_specs=[spec] * 4,
        out_specs=[spec] * 3, out_shape=[jax.ShapeDtypeStruct((R, C), f32)] * 3,
        compiler_params=_cparams(("parallel", "parallel")),
    )(w, m, v, g)


def _expand_onehot(n, per):
    lane = lax.broadcasted_iota(jnp.int32, (n, n * per), 1)
    row = lax.broadcasted_iota(jnp.int32, (n, n * per), 0)
    return (lane // per == row).astype(f32)


def _ssd_layer_fwd(h, nm_w, W, t):
    H = W["dt_bias"].shape[0]
    inner = H * HEADDIM
    G = (W["conv_b"].shape[0] - inner) // (2 * STATE)
    hn = _rms_fwd(h, nm_w, f"rms_mix_{t}")
    conv_dim = W["conv_b"].shape[0]
    wT = W["w_inT"]
    z = _mm(hn, wT, mode="nt", brows=(0, inner), name=f"ssd_z_{t}")
    xpre = _mm(hn, wT, mode="nt", brows=(inner, conv_dim), name=f"ssd_xbc_{t}")
    dt_pre = _mm(hn, wT, mode="nt", brows=(inner + conv_dim, H), name=f"ssd_dt_{t}")
    c, xbc = _conv_fwd(xpre, W["conv_w"], W["conv_b"], f"ssd_conv_{t}")
    _, dtx = _dt_fwd(dt_pre, W["dt_bias"], _expand_onehot(H, HEADDIM), f"ssd_dtx_{t}")
    a = -jnp.exp(W["a_log"])
    ax = _row2(jnp.repeat(a, HEADDIM))
    dx = _row2(jnp.repeat(W["d"], HEADDIM))
    y, states = _ssd_fwd(xbc, dtx, ax, dx, G, f"ssd_scan_{t}")
    yn = _gnorm_fwd(y, z, W["norm_w"], G, f"ssd_gnorm_{t}")
    h1 = _mm(yn, W["wout"], res=h, name=f"ssd_out_{t}")
    return h1, (h, hn, z, xpre, dt_pre, c, xbc, dtx, a, ax, dx, y, states, yn)


def _ssd_layer_bwd(dh1, saved, nm_w, W, t, after=None):
    h, hn, z, xpre, dt_pre, c, xbc, dtx, a, ax, dx, y, states, yn = saved
    H = W["dt_bias"].shape[0]
    inner = H * HEADDIM
    G = (W["conv_b"].shape[0] - inner) // (2 * STATE)
    dyn = _mm(dh1, W["wout"], mode="nt", after=after, name=f"ssd_out_dx_{t}")
    g_wout = _mm(yn, dh1, mode="tn", out_dtype=bf16, name=f"ssd_out_dw_{t}")
    dy, dz, g_normw = _gnorm_bwd(dyn, y, z, W["norm_w"], G, f"ssd_gnorm_bwd_{t}")
    dxbc, ddt, dax, ddx = _ssd_bwd(dy, y, xbc, dtx, ax, dx, states, _expand_onehot(H, HEADDIM).T, G, f"ssd_scan_bwd_{t}")
    dc, g_convw8, g_convb = _conv_bwd_dc(dxbc, c, xpre, f"ssd_conv_bwd_dc_{t}")
    dxpre = _conv_bwd_dx(dc, W["conv_w"], f"ssd_conv_bwd_dx_{t}")
    ddt_pre, g_dtb = _dt_bwd(ddt, dt_pre, W["dt_bias"], f"ssd_dt_bwd_{t}")
    conv_dim = W["conv_b"].shape[0]
    wT = W["w_inT"]
    g_wz = _mm(dz, hn, mode="tn", out_dtype=bf16, name=f"ssd_z_dw_{t}")
    g_wxbc = _mm(dxpre, hn, mode="tn", out_dtype=bf16, name=f"ssd_xbc_dw_{t}")
    g_wdt = _mm(ddt_pre, hn, mode="tn", out_dtype=bf16, name=f"ssd_dt_dw_{t}")
    dhn = _mm(dz, wT, brows=(0, inner), name=f"ssd_z_dx_{t}")
    dhn = _mm(dxpre, wT, brows=(inner, conv_dim), res=dhn, name=f"ssd_xbc_dx_{t}")
    dhn = _mm(ddt_pre, wT, brows=(inner + conv_dim, H), res=dhn, name=f"ssd_dt_dx_{t}")
    dh, g_nm = _rms_bwd_add(dh1, dhn, h, nm_w, f"rms_mix_bwd_{t}")
    grads = dict(
        w_inT=jnp.concatenate([g_wz, g_wxbc, g_wdt], axis=0), wout=g_wout,
        conv_w=g_convw8[:CONV_K], conv_b=g_convb[0], dt_bias=g_dtb[0],
        a_log=dax[0].reshape(H, HEADDIM)[:, 0] * a, d=jnp.sum(ddx[0].reshape(H, HEADDIM), axis=1),
        norm_w=g_normw[0], norm_mix=g_nm[0])
    return dh, grads


def _gmlp_layer_fwd(h, nm_w, W, t):
    NG, L, _ = W["w_s"].shape
    I = W["ln_w"].shape[0]
    hn = _rms_fwd(h, nm_w, f"rms_mix_{t}")
    pre = _mm(hn, W["win"], name=f"gmlp_in_{t}")
    bsx = jnp.repeat(W["b_s"].T, I // NG, axis=1)
    o = _gmlp_mid_fwd(pre, W["b_in"], W["ln_w"], W["ln_b"], W["w_s"], bsx, f"gmlp_mid_{t}")
    h1 = _mm(o, W["wout"], res=h, name=f"gmlp_out_{t}")
    return h1, (h, hn, pre, bsx, o)


def _gmlp_layer_bwd(dh1, saved, nm_w, W, t, after=None):
    h, hn, pre, bsx, o = saved
    NG = W["w_s"].shape[0]
    I = W["ln_w"].shape[0]
    do = _mm(dh1, W["wout"], mode="nt", after=after, name=f"gmlp_out_dx_{t}")
    g_wout = _mm(o, dh1, mode="tn", out_dtype=bf16, name=f"gmlp_out_dw_{t}")
    dpre, g_bin, g_lnw, g_lnb, g_ws, dbs = _gmlp_mid_bwd(do, pre, W["b_in"], W["ln_w"], W["ln_b"], W["w_s"], bsx, f"gmlp_mid_bwd_{t}")
    g_bs = _lane_group_sum(dbs, _expand_onehot(NG, I // NG).T, f"gmlp_bs_{t}").T
    g_win = _mm(hn, dpre, mode="tn", out_dtype=bf16, name=f"gmlp_in_dw_{t}")
    dhn = _mm(dpre, W["win"], mode="nt", name=f"gmlp_in_dx_{t}")
    dh, g_nm = _rms_bwd_add(dh1, dhn, h, nm_w, f"rms_mix_bwd_{t}")
    grads = dict(win=g_win, wout=g_wout, b_in=g_bin[0], ln_w=g_lnw[0], ln_b=g_lnb[0], w_s=g_ws, b_s=g_bs, norm_mix=g_nm[0])
    return dh, grads


def _ffn_fwd(h1, nf_w, W, t):
    h2, u, Gm, Um, A = _ffn_fwd_fused(h1, nf_w, W["wg"], W["wu"], W["wd"], f"ffn_fwd_{t}")
    return h2, (h1, u, Gm, Um, A)


def _ffn_bwd(dh2, saved, nf_w, W, t, after=None):
    h1, u, Gm, Um, A = saved
    dG, dU, dh1, g_nf = _ffn_bwd_fused(dh2, h1, nf_w, W["wd"], W["wg"], W["wu"], Gm, Um, f"ffn_bwd_{t}", after=after)
    g_wd = _mm(A, dh2, mode="tn", out_dtype=bf16, name=f"ffn_down_dw_{t}")
    g_wg = _mm(dG, u, mode="tn", out_dtype=bf16, name=f"ffn_gate_dw_{t}")
    g_wu = _mm(dU, u, mode="tn", out_dtype=bf16, name=f"ffn_up_dw_{t}")
    return dh1, dict(wg=g_wg, wu=g_wu, wd=g_wd, norm_ffn=g_nf[0])


def _local_step(x, p, target, norms, layer_weights, on_layer_grads=None, final_norm_grad=None):
    depth = p.shape[0]
    h = x
    saved = []
    for i in range(depth):
        Wm = layer_weights(i, "mix", h)
        if i % 2 == 0:
            h1, s_mix = _ssd_layer_fwd(h, norms["norm_mix"][i], Wm, i)
        else:
            h1, s_mix = _gmlp_layer_fwd(h, norms["norm_mix"][i], Wm, i)
        Wf = layer_weights(i, "ffn", h1)
        h2, s_ffn = _ffn_fwd(h1, norms["norm_ffn"][i], Wf, i)
        P = layer_weights(i, "ple", h2)
        h3, pe, gate = _ple_fwd(h2, p[i], P["wp"], P["pn"], P["gn"], P["wgate"], f"ple_{i}")
        saved.append((Wm, Wf, P, s_mix, s_ffn, (h2, pe, gate)))
        h = h3
    dh, loss, g_fn = _loss_head(h, target, norms["final_norm"], "loss_head")
    if final_norm_grad is not None:
        final_norm_grad[0] = g_fn[0]
    grads = [None] * depth
    tell = on_layer_grads if on_layer_grads is not None else (lambda i, part, g: None)
    after = None
    for i in reversed(range(depth)):
        Wm, Wf, P, s_mix, s_ffn, (h2, pe, gate) = saved[i]
        dh, dzg, dpe, q, g_pn, g_gn = _ple_bwd(dh, h2, pe, gate, P["pn"], P["gn"], P["wgate"], f"ple_bwd_{i}", after=after)
        g_ple = dict(wgate=_mm(q, dzg, mode="tn", out_dtype=bf16, name=f"ple_gate_dw_{i}"),
                     wp=_mm(p[i], dpe, mode="tn", out_dtype=bf16, name=f"ple_proj_dw_{i}"), pn=g_pn[0], gn=g_gn[0])
        after = tell(i, "ple", g_ple)
        dh, g_ffn = _ffn_bwd(dh, s_ffn, norms["norm_ffn"][i], Wf, i, after=after)
        after = tell(i, "ffn", g_ffn)
        if i % 2 == 0:
            dh, g_mix = _ssd_layer_bwd(dh, s_mix, norms["norm_mix"][i], Wm, i, after=after)
        else:
            dh, g_mix = _gmlp_layer_bwd(dh, s_mix, norms["norm_mix"][i], Wm, i, after=after)
        after = tell(i, "mix", g_mix)
        grads[i] = dict(mix=g_mix, ffn=g_ffn, ple=g_ple)
    return loss[0, 0], dh, g_fn[0], grads


def _flip(v, f):
    return 1 - v if f else v


_ANY = pl.BlockSpec(memory_space=pl.ANY)


_SEM = pl.BlockSpec(memory_space=pltpu.SEMAPHORE)
_DATAFLOW = pltpu.SideEffectType.DATAFLOW_SIDE_EFFECTING
_CHIP_FLIPS = ((1, 0), (0, 1), (1, 1))
_DMA = pltpu.SemaphoreType.DMA


def _structs(arrs):
    return [jax.ShapeDtypeStruct(a.shape, a.dtype) for a in arrs]


def _gather_copy(src, buf, send_sems, recv_sems, k, j, slot, x, y, c):
    c2 = src.shape[1] // 2
    fx, fy = _CHIP_FLIPS[j]
    nf = len(_CHIP_FLIPS)
    return pltpu.make_async_remote_copy(
        src_ref=src.at[:, pl.ds(c * c2, c2)], dst_ref=buf.at[slot, :, pl.ds(c * c2, c2)], send_sem=send_sems.at[nf * k + j],
        recv_sem=recv_sems.at[nf * k + j], device_id=(_flip(x, fx), _flip(y, fy), c), device_id_type=MESH)


def _gather_start(srcs, groups):
    n = len(srcs)
    ng = len(groups)
    nf = len(_CHIP_FLIPS)

    def body(*refs):
        src_refs, buf_refs, sems = refs[:n], refs[n:2 * n], refs[4 * n:]
        x, y, c = lax.axis_index("x"), lax.axis_index("y"), lax.axis_index("c")
        for gi, group in enumerate(groups):
            for k, o in enumerate(group):
                for j in range(nf):
                    _gather_copy(src_refs[o], buf_refs[o], sems[2 * gi], sems[2 * gi + 1], k, j, 2 * x + y, x, y, c).start()

    inits = [jnp.broadcast_to(s[None], (N_CHIPS,) + s.shape) for s in srcs]
    sem_shapes = [_DMA((nf * len(g),)) for g in groups for _ in range(2)]
    outs = pl.pallas_call(
        body, name="gather_start", in_specs=[_ANY] * (2 * n), out_specs=[_ANY] * (2 * n) + [_SEM] * (2 * ng),
        out_shape=_structs(srcs) + _structs(inits) + sem_shapes, input_output_aliases={i: i for i in range(2 * n)},
        compiler_params=pltpu.CompilerParams(has_side_effects=_DATAFLOW),
    )(*srcs, *inits)
    return outs[:n], outs[n:2 * n], [(outs[2 * n + 2 * gi], outs[2 * n + 2 * gi + 1]) for gi in range(ng)]


def _gather_wait(srcs, bufs, sems, after, name):
    n = len(srcs)
    nf = len(_CHIP_FLIPS)

    def body(*refs):
        src_refs, buf_refs, send_sems, recv_sems = refs[:n], refs[n:2 * n], refs[2 * n], refs[2 * n + 1]
        x, y, c = lax.axis_index("x"), lax.axis_index("y"), lax.axis_index("c")
        for k in range(n):
            for j, (fx, fy) in enumerate(_CHIP_FLIPS):
                cp = _gather_copy(src_refs[k], buf_refs[k], send_sems, recv_sems, k, j, 2 * _flip(x, fx) + _flip(y, fy), x, y, c)
                cp.wait_send()
                cp.wait_recv()

    outs = pl.pallas_call(
        body, name=name, in_specs=[_ANY] * (2 * n) + [_SEM, _SEM, _ANY], out_specs=[_ANY] * (2 * n),
        out_shape=_structs(srcs) + _structs(bufs), input_output_aliases={i: i for i in range(2 * n)},
        compiler_params=pltpu.CompilerParams(has_side_effects=_DATAFLOW),
    )(*srcs, *bufs, *sems, after)
    return outs[n:]


def _gather_forward(bufs, name):
    n = len(bufs)
    nf = len(_CHIP_FLIPS)

    def body(*refs):
        outs = refs[n:2 * n]
        send_sems, recv_sems = refs[2 * n:]
        x, y, c = lax.axis_index("x"), lax.axis_index("y"), lax.axis_index("c")

        def forward(k, j, h):
            c2 = bufs[k].shape[2] // 2
            fx, fy = _CHIP_FLIPS[j]
            part = outs[k].at[2 * _flip(x, fx) + _flip(y, fy), :, pl.ds(h * c2, c2)]
            return pltpu.make_async_remote_copy(src_ref=part, dst_ref=part, send_sem=send_sems.at[nf * k + j],
                                                recv_sem=recv_sems.at[nf * k + j], device_id=(x, y, 1 - c), device_id_type=MESH)

        sends = [forward(k, j, c) for k in range(n) for j in range(nf)]
        for cp in sends:
            cp.start()
        for k in range(n):
            for j in range(nf):
                forward(k, j, 1 - c).wait_recv()
        for cp in sends:
            cp.wait_send()

    return pl.pallas_call(
        body, name=name, in_specs=[_ANY] * n, out_specs=[_ANY] * n, out_shape=_structs(bufs),
        input_output_aliases={i: i for i in range(n)}, scratch_shapes=[_DMA((nf * n,)), _DMA((nf * n,))],
    )(*bufs)


def _half_struct(a, lead):
    return jax.ShapeDtypeStruct(lead + (a.shape[-2], a.shape[-1] // 2), a.dtype)


def _sibling_split(tensors, wholes, name):
    n, m = len(tensors), len(wholes)

    def body(*refs):
        ins, got = refs[:n + m], refs[n + m:2 * (n + m)]
        send_sems, recv_sems = refs[2 * (n + m):]
        x, y, c = lax.axis_index("x"), lax.axis_index("y"), lax.axis_index("c")

        def swap(i):
            src = ins[i]
            if i < n:
                c2 = tensors[i].shape[2] // 2
                src = src.at[:, :, pl.ds((1 - c) * c2, c2)]
            return pltpu.make_async_remote_copy(src_ref=src, dst_ref=got[i], send_sem=send_sems.at[i], recv_sem=recv_sems.at[i],
                                                device_id=(x, y, 1 - c), device_id_type=MESH)

        copies = [swap(i) for i in range(n + m)]
        for cp in copies:
            cp.start()
        for cp in copies:
            cp.wait()

    outs = pl.pallas_call(
        body, name=name, in_specs=[_ANY] * (n + m), out_specs=[_ANY] * (n + m),
        out_shape=[_half_struct(t, (N_CHIPS,)) for t in tensors] + _structs(wholes),
        scratch_shapes=[_DMA((n + m,)), _DMA((n + m,))],
    )(*tensors, *wholes)
    return outs[:n], outs[n:]


def _exchange_copy(srcs, lands, n, send_sems, recv_sems, i, j, slot, x, y, c):
    nf = len(_CHIP_FLIPS)
    px, py = _flip(x, _CHIP_FLIPS[j][0]), _flip(y, _CHIP_FLIPS[j][1])
    return pltpu.make_async_remote_copy(
        src_ref=srcs[i].at[2 * px + py] if i < n else srcs[i], dst_ref=lands[i].at[slot], send_sem=send_sems.at[nf * i + j],
        recv_sem=recv_sems.at[nf * i + j], device_id=(px, py, c), device_id_type=MESH)


def _exchange_start(parts, wholes, name):
    n, m = len(parts), len(wholes)
    nf = len(_CHIP_FLIPS)
    t = n + m
    land_structs = _structs(parts) + [jax.ShapeDtypeStruct((N_CHIPS,) + w.shape, w.dtype) for w in wholes]

    def body(*refs):
        srcs, lands, send_sems, recv_sems, token = refs[:t], refs[2 * t:3 * t], refs[3 * t], refs[3 * t + 1], refs[3 * t + 2]
        x, y, c = lax.axis_index("x"), lax.axis_index("y"), lax.axis_index("c")
        for i in range(t):
            for j in range(nf):
                _exchange_copy(srcs, lands, n, send_sems, recv_sems, i, j, 2 * x + y, x, y, c).start()
        token[...] = jnp.zeros_like(token)

    outs = pl.pallas_call(
        body, name=name, in_specs=[_ANY] * t,
        out_specs=[_ANY] * (2 * t) + [_SEM, _SEM, pl.BlockSpec(memory_space=pltpu.VMEM)],
        out_shape=_structs(parts) + _structs(wholes) + land_structs + [_DMA((nf * t,)), _DMA((nf * t,)),
                                                                        jax.ShapeDtypeStruct((SUBLANES, LANES), f32)],
        input_output_aliases={i: i for i in range(t)},
        compiler_params=pltpu.CompilerParams(has_side_effects=_DATAFLOW),
    )(*parts, *wholes)
    return outs[:t], outs[t:2 * t], (outs[2 * t], outs[2 * t + 1]), outs[2 * t + 2]


def _exchange_wait(srcs, lands, n, sems, after, name):
    t = len(srcs)
    nf = len(_CHIP_FLIPS)

    def body(*refs):
        src_refs, land_refs, send_sems, recv_sems = refs[:t], refs[t:2 * t], refs[2 * t], refs[2 * t + 1]
        x, y, c = lax.axis_index("x"), lax.axis_index("y"), lax.axis_index("c")
        for i in range(t):
            for j, (fx, fy) in enumerate(_CHIP_FLIPS):
                cp = _exchange_copy(src_refs, land_refs, n, send_sems, recv_sems, i, j, 2 * _flip(x, fx) + _flip(y, fy), x, y, c)
                cp.wait_send()
                cp.wait_recv()

    outs = pl.pallas_call(
        body, name=name, in_specs=[_ANY] * (2 * t) + [_SEM, _SEM, _ANY], out_specs=[_ANY] * (2 * t),
        out_shape=_structs(srcs) + _structs(lands), input_output_aliases={i: i for i in range(2 * t)},
        compiler_params=pltpu.CompilerParams(has_side_effects=_DATAFLOW),
    )(*srcs, *lands, *sems, after)
    return outs[:t], outs[t:]


def _sibling_join(bufs):
    flat = [(gi, l) for gi, b in enumerate(bufs) for l in range(b.shape[0])]
    n, n_buf = len(flat), len(bufs)

    def body(*refs):
        outs = refs[n_buf:2 * n_buf]
        send_sems, recv_sems = refs[2 * n_buf:]
        x, y, c = lax.axis_index("x"), lax.axis_index("y"), lax.axis_index("c")

        def push(i, h):
            gi, l = flat[i]
            c2 = bufs[gi].shape[2] // 2
            part = outs[gi].at[l, :, pl.ds(h * c2, c2)]
            return pltpu.make_async_remote_copy(src_ref=part, dst_ref=part, send_sem=send_sems.at[i], recv_sem=recv_sems.at[i],
                                                device_id=(x, y, 1 - c), device_id_type=MESH)

        sends = [push(i, c) for i in range(n)]
        for cp in sends:
            cp.start()
        for i in range(n):
            push(i, 1 - c).wait_recv()
        for cp in sends:
            cp.wait_send()

    dma = pltpu.SemaphoreType.DMA
    return pl.pallas_call(
        body, name="grads_sibling_join", in_specs=[_ANY] * n_buf, out_specs=[_ANY] * n_buf,
        out_shape=[jax.ShapeDtypeStruct(b.shape, b.dtype) for b in bufs],
        input_output_aliases={i: i for i in range(n_buf)},
        scratch_shapes=[dma((n,)), dma((n,))],
    )(*bufs)


def _sum_tile_rows(rows, cols):
    tr = rows
    while tr * cols > 256 * 1024 and tr % (4 * SUBLANES) == 0:
        tr //= 2
    return tr


def _pair_sum(full, got, place, name):
    nch, R, C = full.shape
    C2 = C // 2
    tr = _sum_tile_rows(R, C2)
    nb = R // tr

    def body(place_ref, a_ref, b_ref, o_ref):
        o_ref[...] = (a_ref[...].astype(f32) + b_ref[...].astype(f32)).astype(o_ref.dtype)

    return pl.pallas_call(
        body, name=name, out_shape=jax.ShapeDtypeStruct((nch, R, C2), full.dtype),
        grid_spec=pltpu.PrefetchScalarGridSpec(
            num_scalar_prefetch=1, grid=(nch, nb),
            in_specs=[pl.BlockSpec((None, tr, C2), lambda k, i, pr: (k, i, pr[1])),
                      pl.BlockSpec((None, tr, C2), lambda k, i, pr: (k, i, 0))],
            out_specs=pl.BlockSpec((None, tr, C2), lambda k, i, pr: (k, i, 0))),
        compiler_params=_cparams(("parallel", "parallel")),
    )(place, full, got)


def _add2(a, b, name):
    R, C = a.shape
    tr = _sum_tile_rows(R, C)

    def body(a_ref, b_ref, o_ref):
        o_ref[...] = a_ref[...] + b_ref[...]

    spec = pl.BlockSpec((tr, C), lambda i: (i, 0))
    return pl.pallas_call(body, name=name, grid=(R // tr,), in_specs=[spec, spec], out_specs=spec,
                          out_shape=jax.ShapeDtypeStruct((R, C), f32), compiler_params=_cparams(("parallel",)))(a, b)


def _chip_sum(landed, mine, place, name, into=None, layer=0, layers=1):
    nch, R, C = landed.shape
    tr = _sum_tile_rows(R, C)
    nb = R // tr

    def body(*refs):
        place_ref, l_ref, m_ref, o_ref = refs[0], refs[1], refs[2], refs[-1]
        s = jnp.where(place_ref[0] == 0, m_ref[...].astype(f32), l_ref[0].astype(f32))
        for k in range(1, nch):
            s = s + jnp.where(place_ref[0] == k, m_ref[...].astype(f32), l_ref[k].astype(f32))
        o_ref[...] = s

    in_specs = [pl.BlockSpec((nch, tr, C), lambda i, pr: (0, i, 0)),
                pl.BlockSpec((None, tr, C), lambda i, pr: (pr[0], i, 0))]
    args = [place, landed, mine]
    if into is not None:
        in_specs.append(_ANY)
        args.append(into)
    return pl.pallas_call(
        body, name=name, out_shape=jax.ShapeDtypeStruct((layers, R, 2 * C), f32),
        grid_spec=pltpu.PrefetchScalarGridSpec(
            num_scalar_prefetch=1, grid=(nb,), in_specs=in_specs,
            out_specs=pl.BlockSpec((None, tr, C), lambda i, pr: (layer, i, pr[1]))),
        input_output_aliases={3: 0} if into is not None else {},
        compiler_params=_cparams(("parallel",)),
    )(*args)


def _chip_sum_rep(landed, mine, place, name):
    nch, R, C = landed.shape
    tr = _sum_tile_rows(R, C)

    def body(place_ref, l_ref, m_ref, o_ref):
        s = jnp.where(place_ref[0] == 0, m_ref[...], l_ref[0])
        for k in range(1, nch):
            s = s + jnp.where(place_ref[0] == k, m_ref[...], l_ref[k])
        o_ref[...] = s

    return pl.pallas_call(
        body, name=name, out_shape=jax.ShapeDtypeStruct((R, C), f32),
        grid_spec=pltpu.PrefetchScalarGridSpec(
            num_scalar_prefetch=1, grid=(R // tr,),
            in_specs=[pl.BlockSpec((nch, tr, C), lambda i, pr: (0, i, 0)), pl.BlockSpec((tr, C), lambda i, pr: (i, 0))],
            out_specs=pl.BlockSpec((tr, C), lambda i, pr: (i, 0))),
        compiler_params=_cparams(("parallel",)),
    )(place, landed, mine)


PACK_COLS = 1024
PACK_ROW_MULTIPLE = 64

BIG = ("ssd_w_in", "ssd_w_out", "gmlp_w_in", "gmlp_w_out", "ffn_w_gate", "ffn_w_up", "ffn_w_down", "ple_w_proj", "ple_w_gate")
SMALL_SHARDED = ("ssd_conv_w", "gmlp_b_in", "gmlp_ln_w", "gmlp_ln_b")
REP_EARLY = ("gmlp_w_s", "gmlp_b_s")
REP_LATE = ("norm_mix", "norm_ffn", "ssd_conv_b", "ssd_dt_bias", "ssd_a_log", "ssd_d", "ssd_norm_w", "ple_norm", "ple_gate_norm",
            "final_norm")
WEIGHTS = ("norm_mix", "norm_ffn", "ssd_w_in", "ssd_conv_w", "ssd_conv_b", "ssd_dt_bias", "ssd_a_log", "ssd_d", "ssd_norm_w", "ssd_w_out",
           "gmlp_w_in", "gmlp_b_in", "gmlp_ln_w", "gmlp_ln_b", "gmlp_w_s", "gmlp_b_s", "gmlp_w_out", "ffn_w_gate", "ffn_w_up",
           "ffn_w_down", "ple_w_proj", "ple_norm", "ple_gate_norm", "ple_w_gate", "final_norm")
TRANSPOSED = ("ssd_w_in", "ffn_w_gate", "ffn_w_up")


def _pack(arrs):
    flat = jnp.concatenate([a.reshape(-1).astype(f32) for a in arrs])
    per = PACK_COLS * PACK_ROW_MULTIPLE
    n = -(-flat.shape[0] // per) * per
    return jnp.pad(flat, (0, n - flat.shape[0])).reshape(-1, PACK_COLS)


def _unpack(buf, shapes):
    flat = buf.reshape(-1)
    out, o = [], 0
    for s in shapes:
        n = math.prod(s)
        out.append(flat[o:o + n].reshape(s))
        o += n
    return out


def _chip_major(g):
    r, c4 = g.shape
    return g.reshape(r, N_CHIPS, c4 // N_CHIPS).transpose(1, 0, 2)


def _from_chip_major(g):
    k, r, c = g.shape
    return g.transpose(1, 0, 2).reshape(r, k * c)


def _adamw_nd(w, m, v, g, name):
    shp = w.shape
    two = lambda a: a.reshape(-1, shp[-1])
    return [o.reshape(shp) for o in _adamw(two(w), two(m), two(v), two(g), name)]


def kernel(x, p, norm_mix, norm_ffn, ssd_w_in, ssd_conv_w, ssd_conv_b, ssd_dt_bias, ssd_a_log, ssd_d, ssd_norm_w, ssd_w_out, gmlp_w_in, gmlp_b_in, gmlp_ln_w, gmlp_ln_b, gmlp_w_s, gmlp_b_s, gmlp_w_out, ffn_w_gate, ffn_w_up, ffn_w_down, ple_w_proj, ple_norm, ple_gate_norm, ple_w_gate, final_norm, loss_target, m_norm_mix, m_norm_ffn, m_ssd_w_in, m_ssd_conv_w, m_ssd_conv_b, m_ssd_dt_bias, m_ssd_a_log, m_ssd_d, m_ssd_norm_w, m_ssd_w_out, m_gmlp_w_in, m_gmlp_b_in, m_gmlp_ln_w, m_gmlp_ln_b, m_gmlp_w_s, m_gmlp_b_s, m_gmlp_w_out, m_ffn_w_gate, m_ffn_w_up, m_ffn_w_down, m_ple_w_proj, m_ple_norm, m_ple_gate_norm, m_ple_w_gate, m_final_norm, v_norm_mix, v_norm_ffn, v_ssd_w_in, v_ssd_conv_w, v_ssd_conv_b, v_ssd_dt_bias, v_ssd_a_log, v_ssd_d, v_ssd_norm_w, v_ssd_w_out, v_gmlp_w_in, v_gmlp_b_in, v_gmlp_ln_w, v_gmlp_ln_b, v_gmlp_w_s, v_gmlp_b_s, v_gmlp_w_out, v_ffn_w_gate, v_ffn_w_up, v_ffn_w_down, v_ple_w_proj, v_ple_norm, v_ple_gate_norm, v_ple_w_gate, v_final_norm):
    given = dict(locals())
    view = lambda n, a: jnp.swapaxes(a, 1, 2) if n in TRANSPOSED else a
    w = {n: view(n, given[n]) for n in WEIGHTS}
    mom = {n: view(n, given["m_" + n]) for n in WEIGHTS}
    var = {n: view(n, given["v_" + n]) for n in WEIGHTS}
    depth = p.shape[0]
    n_ssd, n_gmlp = ssd_w_in.shape[0], gmlp_w_in.shape[0]
    inner = ssd_dt_bias.shape[1] * HEADDIM
    conv_dim = ssd_conv_b.shape[1]

    place = jnp.stack([2 * lax.axis_index("x") + lax.axis_index("y"), lax.axis_index("c")]).astype(jnp.int32)

    def part_keys(i, part):
        j = i // 2
        if part == "mix":
            names = (("ssd_w_in", j), ("ssd_w_out", j)) if i % 2 == 0 else (("gmlp_w_in", j), ("gmlp_w_out", j))
            return ((("small", 0),) if i == 0 else ()) + names
        if part == "ffn":
            return (("ffn_w_gate", i), ("ffn_w_up", i), ("ffn_w_down", i))
        return (("ple_w_proj", i), ("ple_w_gate", i))

    parts = [(i, part) for i in range(depth) for part in ("mix", "ffn", "ple")]
    keys, groups = [], {}
    for ip in parts:
        names = part_keys(*ip)
        groups[ip] = list(range(len(keys), len(keys) + len(names)))
        keys += names
    small_shapes = [w[n].shape for n in SMALL_SHARDED]
    srcs = [_pack([w[n] for n in SMALL_SHARDED]) if n == "small" else w[n][l].astype(bf16) for n, l in keys]
    srcs, landing, gather_sems = _gather_start(srcs, [groups[ip] for ip in parts])
    gather_sems = dict(zip(parts, gather_sems))
    small_full = {}

    def layer_weights(i, part, h):
        idx = groups[(i, part)]
        got = _gather_wait([srcs[o] for o in idx], [landing[o] for o in idx], gather_sems[(i, part)], h, f"gather_wait_{part}_{i}")
        gw = dict(zip([keys[o] for o in idx], _gather_forward(got, f"gather_forward_{part}_{i}")))
        rows = lambda a: a.reshape(-1, a.shape[-1])
        j = i // 2
        if part == "ffn":
            return dict(wg=gw[("ffn_w_gate", i)], wu=gw[("ffn_w_up", i)], wd=gw[("ffn_w_down", i)])
        if part == "ple":
            return dict(wp=_from_chip_major(gw[("ple_w_proj", i)]), pn=ple_norm[i], gn=ple_gate_norm[i], wgate=rows(gw[("ple_w_gate", i)]))
        if i == 0:
            by_chip = [_unpack(gw[("small", 0)][k], small_shapes) for k in range(N_CHIPS)]
            small_full.update({n: jnp.concatenate([by_chip[k][t] for k in range(N_CHIPS)], axis=-1) for t, n in enumerate(SMALL_SHARDED)})
        if i % 2 == 0:
            return dict(w_inT=rows(gw[("ssd_w_in", j)]),
                        conv_w=small_full["ssd_conv_w"][j], conv_b=ssd_conv_b[j], dt_bias=ssd_dt_bias[j], a_log=ssd_a_log[j],
                        d=ssd_d[j], norm_w=ssd_norm_w[j], wout=rows(gw[("ssd_w_out", j)]))
        return dict(win=_from_chip_major(gw[("gmlp_w_in", j)]), b_in=small_full["gmlp_b_in"][j], ln_w=small_full["gmlp_ln_w"][j],
                    ln_b=small_full["gmlp_ln_b"][j], w_s=gmlp_w_s[j], b_s=gmlp_b_s[j], wout=rows(gw[("gmlp_w_out", j)]))

    rows4 = lambda a: a.reshape((N_CHIPS, a.shape[0] // N_CHIPS) + a.shape[1:])
    cut = lambda a, k: a[..., k * (a.shape[-1] // N_CHIPS):(k + 1) * (a.shape[-1] // N_CHIPS)]
    layer_grads = {}
    in_flight = {}
    pair_sums = {}

    def on_layer_grads(i, part, g):
        layer_grads[(i, part)] = g
        j = i // 2
        wholes = {}
        if part == "ffn":
            chunks = {("ffn_w_gate", i): g["wg"], ("ffn_w_up", i): g["wu"], ("ffn_w_down", i): g["wd"]}
        elif part == "ple":
            chunks = {("ple_w_proj", i): _chip_major(g["wp"]), ("ple_w_gate", i): rows4(g["wgate"])}
        elif i % 2 == 0:
            chunks = {("ssd_w_in", j): rows4(g["w_inT"]), ("ssd_w_out", j): rows4(g["wout"])}
        else:
            chunks = {("gmlp_w_in", j): _chip_major(g["win"]), ("gmlp_w_out", j): rows4(g["wout"])}
        stack = lambda prt, key, layers: jnp.stack([layer_grads[(l, prt)][key] for l in layers])
        ssd, gml, every = range(0, depth, 2), range(1, depth, 2), range(depth)
        if part == "mix" and i == 1:
            wholes["rep_early"] = _pack([stack("mix", "w_s", gml), stack("mix", "b_s", gml)])
        if part == "mix" and i == 0:
            small_g = dict(ssd_conv_w=stack("mix", "conv_w", ssd), gmlp_b_in=stack("mix", "b_in", gml),
                           gmlp_ln_w=stack("mix", "ln_w", gml), gmlp_ln_b=stack("mix", "ln_b", gml))
            chunks[("small", 0)] = jnp.stack([_pack([cut(small_g[n], k) for n in SMALL_SHARDED]) for k in range(N_CHIPS)])
            rep_g = dict(
                norm_mix=stack("mix", "norm_mix", every), norm_ffn=stack("ffn", "norm_ffn", every),
                ssd_conv_b=stack("mix", "conv_b", ssd), ssd_dt_bias=stack("mix", "dt_bias", ssd), ssd_a_log=stack("mix", "a_log", ssd),
                ssd_d=stack("mix", "d", ssd), ssd_norm_w=stack("mix", "norm_w", ssd), ple_norm=stack("ple", "pn", every),
                ple_gate_norm=stack("ple", "gn", every), final_norm=final_norm_grad[0])
            wholes["rep_late"] = _pack([rep_g[n] for n in REP_LATE])
        ks, wk = list(chunks), list(wholes)
        got, wgot = _sibling_split([chunks[k] for k in ks], [wholes[k] for k in wk], f"grads_sibling_split_{part}_{i}")
        sums = [_pair_sum(chunks[k], b, place, f"grads_pair_sum_{part}_{i}_{t}") for t, (k, b) in enumerate(zip(ks, got))]
        wsums = [_add2(wholes[k], b, f"grads_pair_sum_{k}") for k, b in zip(wk, wgot)]
        thru, lands, sems, token = _exchange_start(sums, wsums, f"grads_exchange_start_{part}_{i}")
        in_flight[(i, part)] = (ks, wk, thru, lands, sems)
        return token

    final_norm_grad = [None]
    norms = dict(norm_mix=norm_mix, norm_ffn=norm_ffn, final_norm=final_norm)
    loss_part, grad_x, g_fn, _ = _local_step(x[0], p[:, 0], loss_target[0], norms, layer_weights, on_layer_grads, final_norm_grad)
    loss = lax.psum(loss_part, ("x", "y", "c"))

    landed = {}
    for i, part in reversed(parts):
        ks, wk, thru, lands, sems = in_flight[(i, part)]
        thru, lands = _exchange_wait(thru, lands, len(ks), sems, grad_x, f"grads_exchange_wait_{part}_{i}")
        landed.update(dict(zip(ks + wk, lands)))
        pair_sums.update(dict(zip(ks + wk, thru)))
    rep_totals = {k: _chip_sum_rep(landed[k], pair_sums[k], place, f"grads_chip_sum_{k}") for k in ("rep_early", "rep_late")}
    bufs = []
    for n in BIG + ("small",):
        layers = w[n].shape[0] if n != "small" else 1
        buf = None
        for l in range(layers):
            buf = _chip_sum(landed[(n, l)], pair_sums[(n, l)], place, f"grads_chip_sum_{n}_{l}", into=buf, layer=l, layers=layers)
        bufs.append(buf)
    reduced = _sibling_join(bufs)

    res = {}
    for n, gsum in zip(BIG, reduced):
        res[n] = [view(n, a) for a in [gsum] + _adamw_nd(w[n], mom[n], var[n], gsum, "adamw_" + n)]
    for names, gsum, tag in ((SMALL_SHARDED, reduced[-1][0], "adamw_small_sharded"), (REP_EARLY, rep_totals["rep_early"], "adamw_rep_early"),
                             (REP_LATE, rep_totals["rep_late"], "adamw_rep_late")):
        packs = [gsum] + list(_adamw(_pack([w[n] for n in names]), _pack([mom[n] for n in names]), _pack([var[n] for n in names]), gsum, tag))
        per_kind = [_unpack(pk, [w[n].shape for n in names]) for pk in packs]
        for i, n in enumerate(names):
            res[n] = [per_kind[k][i] for k in range(4)]
    return (loss, grad_x[None], *[res[n][0] for n in WEIGHTS], *[res[n][1] for n in WEIGHTS],
            *[res[n][2] for n in WEIGHTS], *[res[n][3] for n in WEIGHTS])
```

```python
import functools
import math

import jax
import jax.numpy as jnp
from jax import lax
from jax.experimental import pallas as pl
from jax.experimental.pallas import tpu as pltpu

f32 = jnp.float32
bf16 = jnp.bfloat16
HI = lax.Precision.HIGHEST

LANES = 128
SUBLANES = 8
VMEM_LIMIT_BYTES = 56 * 1024 * 1024

HEADDIM = 64
STATE = 128
CHUNK = 128
CONV_K = 4
RMS_EPS = 1e-6
LN_EPS = 1e-5
ADAM_LR = 0.001
ADAM_B1 = 0.9
ADAM_B2 = 0.999
ADAM_EPS = 1e-08
ADAM_WD = 0.01
ADAM_STEP = 10

N_CHIPS = 4
N_DEV = 8
MESH = pl.DeviceIdType.MESH


def _cparams(sem):
    return pltpu.CompilerParams(dimension_semantics=sem, vmem_limit_bytes=VMEM_LIMIT_BYTES)


def _tile(n, want):
    if n <= want:
        return n
    t = want
    while n % t:
        t //= 2
    return t


def _row_spec(tm, c):
    return pl.BlockSpec((tm, c), lambda i: (i, 0))


def _full_spec(shape):
    nd = len(shape)
    return pl.BlockSpec(tuple(shape), lambda *_: (0,) * nd)


def _sigmoid(x):
    return 1.0 / (1.0 + jnp.exp(-x))


def _silu(x):
    return x * _sigmoid(x)


def _dsilu(x):
    s = _sigmoid(x)
    return s * (1.0 + x * (1.0 - s))


def _gelu(x):
    return 0.5 * x * (1.0 + lax.erf(x * (1.0 / math.sqrt(2.0))))


def _dgelu(x):
    return 0.5 * (1.0 + lax.erf(x * (1.0 / math.sqrt(2.0)))) + x * jnp.exp(-0.5 * x * x) * (1.0 / math.sqrt(2.0 * math.pi))


def _softplus(x):
    return jnp.maximum(x, 0.0) + jnp.log(1.0 + jnp.exp(-jnp.abs(x)))


def _rms(x, w, eps):
    r = lax.rsqrt(jnp.mean(x * x, axis=-1, keepdims=True) + eps)
    return x * r * w


def _rms_bwd(dy, x, w, eps):
    r = lax.rsqrt(jnp.mean(x * x, axis=-1, keepdims=True) + eps)
    xh = x * r
    g = dy * w
    dx = r * (g - xh * jnp.mean(g * xh, axis=-1, keepdims=True))
    dw = jnp.sum(dy * xh, axis=0, keepdims=True)
    return dx, dw


def _dot(a, b, dims=(((1,), (0,)), ((), ())), precision=None):
    return lax.dot_general(a, b, dims, precision=precision, preferred_element_type=f32)


NN = (((1,), (0,)), ((), ()))
NT = (((1,), (1,)), ((), ()))
TN = (((0,), (0,)), ((), ()))


def _split3(x):
    hi = x.astype(bf16)
    r1 = x - hi.astype(f32)
    mid = r1.astype(bf16)
    return hi, mid, (r1 - mid.astype(f32)).astype(bf16)


def _dot01_left(m01, x):
    mb = m01.astype(bf16)
    hi, mid, lo = _split3(x)
    return _dot(mb, hi, NN) + _dot(mb, mid, NN) + _dot(mb, lo, NN)


def _dot01_right(x, m01):
    mb = m01.astype(bf16)
    hi, mid, lo = _split3(x)
    return _dot(hi, mb, NN) + _dot(mid, mb, NN) + _dot(lo, mb, NN)


def _mm(a, b, *, mode="nn", out_dtype=f32, res=None, kbatch=False, brows=None, after=None, tm=1024, tn=1024, tk=1024, name):
    a3, b3 = a.ndim == 3, b.ndim == 3
    nb = a.shape[0] if a3 else (b.shape[0] if b3 else 1)
    ash, bsh = a.shape[-2:], b.shape[-2:]
    if brows is not None:
        bsh = (brows[1], bsh[1])
    if mode == "nn":
        M, K, N = ash[0], ash[1], bsh[1]
    elif mode == "nt":
        M, K, N = ash[0], ash[1], bsh[0]
    else:
        K, M, N = ash[0], ash[1], bsh[1]
    tm, tn, tk = _tile(M, tm), (N if N % LANES else _tile(N, tn)), (K if K % LANES else _tile(K, tk))
    b0 = 0
    if brows is not None:
        assert mode in ("nn", "nt") and bsh[0] == (K if mode == "nn" else N)
        blk = tk if mode == "nn" else tn
        while brows[0] % blk:
            blk //= 2
        assert blk % LANES == 0 or blk == brows[1]
        b0 = brows[0] // blk
        tn, tk = (tn, blk) if mode == "nn" else (blk, tk)
    nk = K // tk
    if kbatch:
        assert a3 and b3
        grid = (1, M // tm, N // tn, nb * nk)
        bi = lambda g, k: k // nk
        ki = lambda g, k: k % nk
    else:
        grid = (nb, M // tm, N // tn, nk)
        bi = lambda g, k: g
        ki = lambda g, k: k
    nsteps = grid[3]

    def spec(is3, blk, imap):
        if is3:
            return pl.BlockSpec((None,) + blk, lambda g, i, j, k: (bi(g, k),) + imap(i, j, ki(g, k)))
        return pl.BlockSpec(blk, lambda g, i, j, k: imap(i, j, ki(g, k)))

    if mode == "nn":
        a_spec = spec(a3, (tm, tk), lambda i, j, k: (i, k))
        b_spec = spec(b3, (tk, tn), lambda i, j, k: (k + b0, j))
        dims = NN
    elif mode == "nt":
        a_spec = spec(a3, (tm, tk), lambda i, j, k: (i, k))
        b_spec = spec(b3, (tn, tk), lambda i, j, k: (j + b0, k))
        dims = NT
    else:
        a_spec = spec(a3, (tk, tm), lambda i, j, k: (k, i))
        b_spec = spec(b3, (tk, tn), lambda i, j, k: (k, j))
        dims = TN
    out3 = (a3 or b3) and not kbatch
    if out3:
        o_spec = pl.BlockSpec((None, tm, tn), lambda g, i, j, k: (g, i, j))
        o_shape = jax.ShapeDtypeStruct((nb, M, N), out_dtype)
    else:
        o_spec = pl.BlockSpec((tm, tn), lambda g, i, j, k: (i, j))
        o_shape = jax.ShapeDtypeStruct((M, N), out_dtype)
    in_specs = [a_spec, b_spec]
    args = [a, b]
    if res is not None:
        in_specs.append(pl.BlockSpec((tm, tn), lambda g, i, j, k: (i, j)))
        args.append(res)
    if after is not None:
        in_specs.append(pl.BlockSpec(memory_space=pl.ANY))
        args.append(after)

    def body(*refs):
        a_ref, b_ref = refs[:2]
        r_ref = refs[2] if res is not None else None
        o_ref, acc_ref = refs[-2:]
        k = pl.program_id(3)

        @pl.when(k == 0)
        def _():
            acc_ref[...] = jnp.zeros_like(acc_ref)

        acc_ref[...] += _dot(a_ref[...].astype(bf16), b_ref[...].astype(bf16), dims)

        @pl.when(k == nsteps - 1)
        def _():
            r = acc_ref[...]
            if res is not None:
                r = r + r_ref[...]
            o_ref[...] = r.astype(o_ref.dtype)

    return pl.pallas_call(
        body, name=name, grid=grid, in_specs=in_specs, out_specs=o_spec, out_shape=o_shape,
        scratch_shapes=[pltpu.VMEM((tm, tn), f32)],
        compiler_params=_cparams(("parallel", "parallel", "parallel", "arbitrary")),
    )(*args)


def _rowcall(fn, *, name, rows, fulls, out_rows, out_accs=(), tm=512):
    S = rows[0].shape[0]
    tm = _tile(S, tm)
    n_r, n_f, n_or, n_oa = len(rows), len(fulls), len(out_rows), len(out_accs)

    def body(*refs):
        ins = [r[...] for r in refs[:n_r + n_f]]
        outs = fn(*ins)
        if not isinstance(outs, (tuple, list)):
            outs = (outs,)
        o_refs = refs[n_r + n_f:]
        for o_ref, v in zip(o_refs[:n_or], outs[:n_or]):
            o_ref[...] = v.astype(o_ref.dtype)
        if n_oa:
            first = pl.program_id(0) == 0

            @pl.when(first)
            def _():
                for o_ref, v in zip(o_refs[n_or:], outs[n_or:]):
                    o_ref[...] = v

            @pl.when(jnp.logical_not(first))
            def _():
                for o_ref, v in zip(o_refs[n_or:], outs[n_or:]):
                    o_ref[...] += v

    in_specs = [_row_spec(tm, r.shape[1]) for r in rows] + [_full_spec(f.shape) for f in fulls]
    out_specs = [_row_spec(tm, c) for c, _ in out_rows] + [_full_spec(s) for s in out_accs]
    out_shape = [jax.ShapeDtypeStruct((S, c), d) for c, d in out_rows] + [jax.ShapeDtypeStruct(s, f32) for s in out_accs]
    res = pl.pallas_call(
        body, name=name, grid=(S // tm,), in_specs=in_specs, out_specs=out_specs, out_shape=out_shape,
        compiler_params=_cparams(("arbitrary",) if n_oa else ("parallel",)),
    )(*rows, *fulls)
    return res


def _row2(v):
    return v.reshape(1, -1)


def _rms_fwd(h, w, name):
    D = h.shape[1]
    return _rowcall(lambda x, w_: _rms(x, w_, RMS_EPS), name=name, rows=[h], fulls=[_row2(w)], out_rows=[(D, bf16)])[0]


def _conv_fwd(xpre, w, b, name):
    S, C = xpre.shape
    tm, tc = _tile(S, 512), _tile(C, 1024)
    hb = tm // SUBLANES

    def body(x_ref, halo_ref, w_ref, b_ref, c_ref, o_ref):
        i = pl.program_id(1)
        x = x_ref[...]
        halo = jnp.where(i > 0, halo_ref[...], 0.0)
        row = lax.broadcasted_iota(jnp.int32, x.shape, 0)
        row8 = lax.broadcasted_iota(jnp.int32, halo.shape, 0)
        x0 = x[0:SUBLANES, :]
        acc = x * w_ref[CONV_K - 1:CONV_K, :] + b_ref[...]
        acc0 = x0 * w_ref[CONV_K - 1:CONV_K, :] + b_ref[...]
        for k in range(1, CONV_K):
            wk = w_ref[CONV_K - 1 - k:CONV_K - k, :]
            acc = acc + pltpu.roll(x, k, axis=0) * wk
            acc0 = acc0 + jnp.where(row8 < k, pltpu.roll(halo, k, axis=0), pltpu.roll(x0, k, axis=0)) * wk
        c_ref[...] = acc
        o_ref[...] = _silu(acc)
        c_ref[0:SUBLANES, :] = acc0
        o_ref[0:SUBLANES, :] = _silu(acc0)

    return pl.pallas_call(
        body, name=name, grid=(C // tc, S // tm),
        in_specs=[pl.BlockSpec((tm, tc), lambda j, i: (i, j)),
                  pl.BlockSpec((SUBLANES, tc), lambda j, i: (jnp.maximum(i * hb - 1, 0), j)),
                  pl.BlockSpec((CONV_K, tc), lambda j, i: (0, j)),
                  pl.BlockSpec((1, tc), lambda j, i: (0, j))],
        out_specs=[pl.BlockSpec((tm, tc), lambda j, i: (i, j))] * 2,
        out_shape=[jax.ShapeDtypeStruct((S, C), f32)] * 2,
        compiler_params=_cparams(("parallel", "parallel")),
    )(xpre, xpre, w, _row2(b))


def _conv_bwd_dc(dxbc, c, xpre, name):
    S, C = xpre.shape
    tm, tc = _tile(S, 512), _tile(C, 1024)
    hb = tm // SUBLANES

    def body(d_ref, c_ref, x_ref, halo_ref, dc_ref, dw_ref, db_ref):
        i = pl.program_id(1)
        x = x_ref[...]
        dc = d_ref[...] * _dsilu(c_ref[...])
        dc_ref[...] = dc
        halo = jnp.where(i > 0, halo_ref[...], 0.0)
        row = lax.broadcasted_iota(jnp.int32, x.shape, 0)
        row8 = lax.broadcasted_iota(jnp.int32, halo.shape, 0)
        x0 = x[0:SUBLANES, :]
        dc0 = dc[0:SUBLANES, :]
        parts = [jnp.sum(dc * x, axis=0, keepdims=True)]
        for k in range(1, CONV_K):
            xs_big = jnp.where(row < SUBLANES, 0.0, pltpu.roll(x, k, axis=0))
            xs0 = jnp.where(row8 < k, pltpu.roll(halo, k, axis=0), pltpu.roll(x0, k, axis=0))
            parts.append(jnp.sum(dc * xs_big, axis=0, keepdims=True) + jnp.sum(dc0 * xs0, axis=0, keepdims=True))
        dw = jnp.concatenate([parts[CONV_K - 1 - k] for k in range(CONV_K)] + [jnp.zeros((SUBLANES - CONV_K, x.shape[1]), f32)], axis=0)
        db = jnp.sum(dc, axis=0, keepdims=True)

        @pl.when(i == 0)
        def _():
            dw_ref[...] = dw
            db_ref[...] = db

        @pl.when(i > 0)
        def _():
            dw_ref[...] += dw
            db_ref[...] += db

    return pl.pallas_call(
        body, name=name, grid=(C // tc, S // tm),
        in_specs=[pl.BlockSpec((tm, tc), lambda j, i: (i, j))] * 3 +
                 [pl.BlockSpec((SUBLANES, tc), lambda j, i: (jnp.maximum(i * hb - 1, 0), j))],
        out_specs=[pl.BlockSpec((tm, tc), lambda j, i: (i, j)),
                   pl.BlockSpec((SUBLANES, tc), lambda j, i: (0, j)),
                   pl.BlockSpec((1, tc), lambda j, i: (0, j))],
        out_shape=[jax.ShapeDtypeStruct((S, C), f32), jax.ShapeDtypeStruct((SUBLANES, C), f32), jax.ShapeDtypeStruct((1, C), f32)],
        compiler_params=_cparams(("parallel", "arbitrary")),
    )(dxbc, c, xpre, xpre)


def _conv_bwd_dx(dc, w, name):
    S, C = dc.shape
    tm, tc = _tile(S, 512), _tile(C, 1024)
    hb = tm // SUBLANES
    nrow = S // tm
    last8 = S // SUBLANES - 1

    def body(d_ref, nxt_ref, w_ref, o_ref):
        i = pl.program_id(1)
        d = d_ref[...]
        nxt = jnp.where(i < nrow - 1, nxt_ref[...], 0.0)
        row8 = lax.broadcasted_iota(jnp.int32, nxt.shape, 0)
        dl = d[tm - SUBLANES:tm, :]
        acc = d * w_ref[CONV_K - 1:CONV_K, :]
        accl = dl * w_ref[CONV_K - 1:CONV_K, :]
        for j in range(1, CONV_K):
            wk = w_ref[CONV_K - 1 - j:CONV_K - j, :]
            acc = acc + pltpu.roll(d, tm - j, axis=0) * wk
            accl = accl + jnp.where(row8 >= SUBLANES - j, pltpu.roll(nxt, SUBLANES - j, axis=0), pltpu.roll(dl, SUBLANES - j, axis=0)) * wk
        o_ref[...] = acc.astype(o_ref.dtype)
        o_ref[tm - SUBLANES:tm, :] = accl.astype(o_ref.dtype)

    return pl.pallas_call(
        body, name=name, grid=(C // tc, nrow),
        in_specs=[pl.BlockSpec((tm, tc), lambda j, i: (i, j)),
                  pl.BlockSpec((SUBLANES, tc), lambda j, i: (jnp.minimum((i + 1) * hb, last8), j)),
                  pl.BlockSpec((CONV_K, tc), lambda j, i: (0, j))],
        out_specs=pl.BlockSpec((tm, tc), lambda j, i: (i, j)),
        out_shape=jax.ShapeDtypeStruct((S, C), f32),
        compiler_params=_cparams(("parallel", "parallel")),
    )(dc, dc, w)


def _halfsum(v, lane_lo):
    s0 = jnp.sum(jnp.where(lane_lo, v, 0.0), axis=1, keepdims=True)
    s1 = jnp.sum(jnp.where(lane_lo, 0.0, v), axis=1, keepdims=True)
    return jnp.where(lane_lo, s0, s1)


def _ssd_specs(S, inner, GN, nchunks, rev):
    L = CHUNK
    cm = (lambda c: nchunks - 1 - c) if rev else (lambda c: c)
    xs = pl.BlockSpec((L, inner), lambda c: (cm(c), 0))
    bb = pl.BlockSpec((L, GN), lambda c: (cm(c), inner // GN))
    cc = pl.BlockSpec((L, GN), lambda c: (cm(c), inner // GN + 1))
    row = pl.BlockSpec((L, inner), lambda c: (cm(c), 0))
    vec = pl.BlockSpec((1, inner), lambda c: (0, 0))
    st = pl.BlockSpec((None, inner, STATE), lambda c: (cm(c), 0, 0))
    return xs, bb, cc, row, vec, st


def _ssd_fwd(xbc, dtx, ax, dx, G, name):
    S, inner = dtx.shape
    GN = G * STATE
    L = CHUNK
    nchunks = S // L
    npairs = inner // LANES
    ppg = npairs // G
    assert inner % GN == 0 and L == LANES and STATE == LANES

    def body(xs_ref, b_ref, c_ref, dtx_ref, ax_ref, dx_ref, y_ref, so_ref, st_ref):
        ci = pl.program_id(0)

        @pl.when(ci == 0)
        def _():
            st_ref[...] = jnp.zeros_like(st_ref)

        r = lax.broadcasted_iota(jnp.int32, (L, L), 0)
        cidx = lax.broadcasted_iota(jnp.int32, (L, L), 1)
        tril = cidx <= r
        lane_lo = cidx < HEADDIM
        xs = xs_ref[...]
        dtv = dtx_ref[...]
        X = xs * dtv
        da = dtv * ax_ref[...]
        cs = _dot01_left(tril, da)
        cs_last = jnp.sum(da, axis=0, keepdims=True)
        so_ref[...] = st_ref[...]
        for g in range(G):
            Bg = b_ref[:, g * STATE:(g + 1) * STATE].astype(bf16)
            Cg = c_ref[:, g * STATE:(g + 1) * STATE].astype(bf16)
            CB = _dot(Cg, Bg, NT)
            for j in range(ppg):
                lo = (g * ppg + j) * LANES
                tile = cs[:, lo:lo + LANES]
                rl = pltpu.roll(tile, HEADDIM, axis=1)
                Xp = X[:, lo:lo + LANES]
                prev = st_ref[lo:lo + LANES, :]
                ypair = _dot(Cg, prev.astype(bf16), NT) * jnp.exp(tile)
                for half in (0, 1):
                    hm = lane_lo if half == 0 else jnp.logical_not(lane_lo)
                    colb = jnp.where(hm, tile, rl)
                    Lm = jnp.exp(jnp.where(tril, colb - colb.T, -1e30))
                    W = (CB * Lm).astype(bf16)
                    ypair = ypair + _dot(W, jnp.where(hm, Xp, 0.0).astype(bf16), NN)
                y_ref[:, lo:lo + LANES] = ypair + xs[:, lo:lo + LANES] * dx_ref[:, lo:lo + LANES]
                last = cs_last[:, lo:lo + LANES]
                snew = _dot((Xp * jnp.exp(last - tile)).astype(bf16), Bg, TN)
                dec_rows = jnp.broadcast_to(jnp.exp(last), (L, LANES)).T
                st_ref[lo:lo + LANES, :] = dec_rows * prev + snew

    xs_s, b_s, c_s, row_s, vec_s, st_s = _ssd_specs(S, inner, GN, nchunks, False)
    return pl.pallas_call(
        body, name=name, grid=(nchunks,),
        in_specs=[xs_s, b_s, c_s, row_s, vec_s, vec_s],
        out_specs=[row_s, st_s],
        out_shape=[jax.ShapeDtypeStruct((S, inner), f32), jax.ShapeDtypeStruct((nchunks, inner, STATE), f32)],
        scratch_shapes=[pltpu.VMEM((inner, STATE), f32)],
        compiler_params=_cparams(("arbitrary",)),
    )(xbc, xbc, xbc, dtx, ax, dx)


def _ssd_bwd(dy, y, xbc, dtx, ax, dx, states, et, G, name):
    S, inner = dtx.shape
    H = et.shape[1]
    GN = G * STATE
    Cc = inner + 2 * GN
    L = CHUNK
    nchunks = S // L
    npairs = inner // LANES
    ppg = npairs // G

    def body(dy_ref, y_ref, xs_ref, b_ref, c_ref, dtx_ref, ax_ref, dx_ref, si_ref, et_ref,
             dxbc_ref, ddt_ref, dax_ref, ddx_ref, dst_ref, dA_ref, dAl_ref, ddtp_ref):
        ci = pl.program_id(0)

        @pl.when(ci == 0)
        def _():
            dst_ref[...] = jnp.zeros_like(dst_ref)
            dax_ref[...] = jnp.zeros_like(dax_ref)
            ddx_ref[...] = jnp.zeros_like(ddx_ref)

        r = lax.broadcasted_iota(jnp.int32, (L, L), 0)
        cidx = lax.broadcasted_iota(jnp.int32, (L, L), 1)
        tril = cidx <= r
        lane_lo = cidx < HEADDIM
        lane_lo1 = lax.broadcasted_iota(jnp.int32, (1, LANES), 1) < HEADDIM
        xs = xs_ref[...]
        dtv = dtx_ref[...]
        dyv = dy_ref[...]
        X = xs * dtv
        da = dtv * ax_ref[...]
        cs = _dot01_left(tril, da)
        cs_last = jnp.sum(da, axis=0, keepdims=True)
        for g in range(G):
            Bg = b_ref[:, g * STATE:(g + 1) * STATE].astype(bf16)
            Cg = c_ref[:, g * STATE:(g + 1) * STATE].astype(bf16)
            CB = _dot(Cg, Bg, NT)
            dCB = jnp.zeros((L, L), f32)
            dBg = jnp.zeros((L, STATE), f32)
            dCg = jnp.zeros((L, STATE), f32)
            for j in range(ppg):
                lo = (g * ppg + j) * LANES
                tile = cs[:, lo:lo + LANES]
                rl = pltpu.roll(tile, HEADDIM, axis=1)
                eA = jnp.exp(tile)
                Xp = X[:, lo:lo + LANES]
                dYp = dyv[:, lo:lo + LANES]
                xsp = xs[:, lo:lo + LANES]
                prev = si_ref[lo:lo + LANES, :]
                dSn = dst_ref[lo:lo + LANES, :]
                prev_b = prev.astype(bf16)
                dSn_b = dSn.astype(bf16)
                dYe = (dYp * eA).astype(bf16)
                dCg = dCg + _dot(dYe, prev_b, NN)
                dprev = _dot(dYe, Cg, TN)
                last = cs_last[:, lo:lo + LANES]
                w = jnp.exp(last - tile)
                BdS = _dot(Bg, dSn_b, NT)
                Xw = Xp * w
                XwB = Xw * BdS
                dAl_t = _halfsum(jnp.sum(XwB, axis=0, keepdims=True), lane_lo1)
                dBg = dBg + _dot(Xw.astype(bf16), dSn_b, NN)
                dec_rows = jnp.broadcast_to(jnp.exp(last), (L, LANES)).T
                dprev = dprev + dec_rows * dSn
                rsum = jnp.sum(dSn * prev * dec_rows, axis=1, keepdims=True)
                s0 = jnp.sum(rsum[0:HEADDIM], axis=0, keepdims=True)
                s1 = jnp.sum(rsum[HEADDIM:LANES], axis=0, keepdims=True)
                dAl_t = dAl_t + jnp.where(lane_lo1, s0, s1)
                dXd = jnp.zeros((L, LANES), f32)
                for half in (0, 1):
                    hm = lane_lo if half == 0 else jnp.logical_not(lane_lo)
                    colb = jnp.where(hm, tile, rl)
                    Lm = jnp.exp(jnp.where(tril, colb - colb.T, -1e30))
                    dYh = jnp.where(hm, dYp, 0.0).astype(bf16)
                    dW = _dot(dYh, jnp.where(hm, Xp, 0.0).astype(bf16), NT)
                    dXd = dXd + _dot((CB * Lm).astype(bf16), dYh, TN)
                    dCB = dCB + dW * Lm
                yoff = _dot(Cg, prev_b, NT) * eA
                ydiag = y_ref[:, lo:lo + LANES] - xsp * dx_ref[:, lo:lo + LANES] - yoff
                dYb = dYp.astype(bf16).astype(f32)
                Xb = Xp.astype(bf16).astype(f32)
                dA_t = _halfsum(dYb * ydiag - Xb * dXd + dYp * yoff - XwB, lane_lo)
                dXp = w * BdS + dXd
                dxbc_ref[:, lo:lo + LANES] = dXp * dtv[:, lo:lo + LANES] + dYp * dx_ref[:, lo:lo + LANES]
                ddtp_ref[:, lo:lo + LANES] = dXp * xsp
                ddx_ref[:, lo:lo + LANES] += jnp.sum(dYp * xsp, axis=0, keepdims=True)
                dA_ref[:, lo:lo + LANES] = dA_t
                dAl_ref[:, lo:lo + LANES] = dAl_t
                dst_ref[lo:lo + LANES, :] = dprev
            dCBb = dCB.astype(bf16)
            dxbc_ref[:, inner + g * STATE:inner + (g + 1) * STATE] = dBg + _dot(dCBb, Cg, TN)
            dxbc_ref[:, inner + GN + g * STATE:inner + GN + (g + 1) * STATE] = dCg + _dot(dCBb, Bg, NN)
        dda = _dot01_left(cidx >= r, dA_ref[...]) + dAl_ref[...]
        ddt_full = ddtp_ref[...] + dda * ax_ref[...] * (1.0 / HEADDIM)
        ddt_ref[...] = _dot01_right(ddt_full, et_ref[...])
        dax_ref[...] += jnp.sum(dda * dtv, axis=0, keepdims=True)

    xs_s, b_s, c_s, row_s, vec_s, st_s = _ssd_specs(S, inner, GN, nchunks, True)
    return pl.pallas_call(
        body, name=name, grid=(nchunks,),
        in_specs=[row_s, row_s, xs_s, b_s, c_s, row_s, vec_s, vec_s, st_s, _full_spec(et.shape)],
        out_specs=[pl.BlockSpec((L, Cc), lambda c: (nchunks - 1 - c, 0)),
                   pl.BlockSpec((L, H), lambda c: (nchunks - 1 - c, 0)), vec_s, vec_s],
        out_shape=[jax.ShapeDtypeStruct((S, Cc), f32), jax.ShapeDtypeStruct((S, H), f32),
                   jax.ShapeDtypeStruct((1, inner), f32), jax.ShapeDtypeStruct((1, inner), f32)],
        scratch_shapes=[pltpu.VMEM((inner, STATE), f32), pltpu.VMEM((L, inner), f32),
                        pltpu.VMEM((1, inner), f32), pltpu.VMEM((L, inner), f32)],
        compiler_params=_cparams(("arbitrary",)),
    )(dy, y, xbc, xbc, xbc, dtx, ax, dx, states, et)


def _dt_fwd(dt_pre, bias, e, name):
    H, inner = e.shape

    def fn(dp, b, e_):
        dt = _softplus(dp + b)
        return dt, _dot01_right(dt, e_)

    return _rowcall(fn, name=name, rows=[dt_pre], fulls=[_row2(bias), e], out_rows=[(H, f32), (inner, f32)])


def _dt_bwd(ddt, dt_pre, bias, name):
    H = ddt.shape[1]

    def fn(dd, dp, b):
        g = dd * _sigmoid(dp + b)
        return g, jnp.sum(g, axis=0, keepdims=True)

    return _rowcall(fn, name=name, rows=[ddt, dt_pre], fulls=[_row2(bias)], out_rows=[(H, f32)], out_accs=[(1, H)])


def _gnorm_fwd(y, z, w, G, name):
    inner = y.shape[1]
    gs = inner // G

    def fn(y_, z_, w_):
        gg = y_ * _silu(z_)
        outs = []
        for g in range(G):
            sl = slice(g * gs, (g + 1) * gs)
            outs.append(_rms(gg[:, sl], w_[:, sl], LN_EPS))
        return jnp.concatenate(outs, axis=1)

    return _rowcall(fn, name=name, rows=[y, z], fulls=[_row2(w)], out_rows=[(inner, bf16)], tm=256)[0]


def _gnorm_bwd(dyn, y, z, w, G, name):
    inner = y.shape[1]
    gs = inner // G

    def fn(d_, y_, z_, w_):
        sz = _silu(z_)
        gg = y_ * sz
        dgs, dws = [], []
        for g in range(G):
            sl = slice(g * gs, (g + 1) * gs)
            dg, dw = _rms_bwd(d_[:, sl], gg[:, sl], w_[:, sl], LN_EPS)
            dgs.append(dg)
            dws.append(dw)
        dgg = jnp.concatenate(dgs, axis=1)
        return dgg * sz, dgg * y_ * _dsilu(z_), jnp.concatenate(dws, axis=1)

    return _rowcall(fn, name=name, rows=[dyn, y, z], fulls=[_row2(w)], out_rows=[(inner, f32), (inner, f32)],
                    out_accs=[(1, inner)], tm=256)


def _gmlp_parts(pre, lw, lb, I):
    hp = _gelu(pre)
    uu = hp[:, :I]
    vp = hp[:, I:]
    xc = vp - jnp.mean(vp, axis=-1, keepdims=True)
    rstd = lax.rsqrt(jnp.mean(xc * xc, axis=-1, keepdims=True) + LN_EPS)
    vhat = xc * rstd
    return uu, vhat, rstd, vhat * lw + lb


def _gmlp_mid_fwd(pre, b_in, ln_w, ln_b, w_s, bsx, name):
    S, two_i = pre.shape
    I = two_i // 2
    NG = w_s.shape[0]
    gd = I // NG
    L = CHUNK

    def body(pre_ref, bi_ref, lw_ref, lb_ref, ws_ref, bsx_ref, o_ref):
        uu, _, _, vv = _gmlp_parts(pre_ref[...] + bi_ref[...], lw_ref[...], lb_ref[...], I)
        r = lax.broadcasted_iota(jnp.int32, (L, L), 0)
        cidx = lax.broadcasted_iota(jnp.int32, (L, L), 1)
        tril = cidx <= r
        for g in range(NG):
            sl = slice(g * gd, (g + 1) * gd)
            wg = jnp.where(tril, ws_ref[g], 0.0).astype(bf16)
            mixed = _dot(wg, vv[:, sl].astype(bf16), NN) + bsx_ref[:, sl]
            o_ref[:, sl] = (uu[:, sl] * mixed).astype(o_ref.dtype)

    return pl.pallas_call(
        body, name=name, grid=(S // L,),
        in_specs=[_row_spec(L, two_i), _full_spec((1, two_i)), _full_spec((1, I)), _full_spec((1, I)), _full_spec(w_s.shape), _full_spec(bsx.shape)],
        out_specs=_row_spec(L, I), out_shape=jax.ShapeDtypeStruct((S, I), bf16),
        compiler_params=_cparams(("parallel",)),
    )(pre, _row2(b_in), _row2(ln_w), _row2(ln_b), w_s, bsx)


def _gmlp_mid_bwd(do, pre, b_in, ln_w, ln_b, w_s, bsx, name):
    S, two_i = pre.shape
    I = two_i // 2
    NG = w_s.shape[0]
    gd = I // NG
    L = CHUNK

    def body(do_ref, pre_ref, bi_ref, lw_ref, lb_ref, ws_ref, bsx_ref, dpre_ref, dbi_ref, dlw_ref, dlb_ref, dws_ref, dbs_ref, dvv_ref):
        ci = pl.program_id(0)

        @pl.when(ci == 0)
        def _():
            for ref in (dbi_ref, dlw_ref, dlb_ref, dws_ref, dbs_ref):
                ref[...] = jnp.zeros_like(ref)

        pre = pre_ref[...] + bi_ref[...]
        lw = lw_ref[...]
        uu, vhat, rstd, vv = _gmlp_parts(pre, lw, lb_ref[...], I)
        dov = do_ref[...]
        r = lax.broadcasted_iota(jnp.int32, (L, L), 0)
        cidx = lax.broadcasted_iota(jnp.int32, (L, L), 1)
        tril = cidx <= r
        duus = []
        for g in range(NG):
            sl = slice(g * gd, (g + 1) * gd)
            wg = jnp.where(tril, ws_ref[g], 0.0).astype(bf16)
            vg = vv[:, sl].astype(bf16)
            mixed = _dot(wg, vg, NN) + bsx_ref[:, sl]
            duus.append(dov[:, sl] * mixed)
            dmixed = dov[:, sl] * uu[:, sl]
            dbs_ref[:, sl] += dmixed
            dmb = dmixed.astype(bf16)
            dvv_ref[:, sl] = _dot(wg, dmb, TN)
            dws_ref[g] += jnp.where(tril, _dot(dmb, vg, NT), 0.0)
        duu = jnp.concatenate(duus, axis=1)
        dvv = dvv_ref[...]
        dlw_ref[...] += jnp.sum(dvv * vhat, axis=0, keepdims=True)
        dlb_ref[...] += jnp.sum(dvv, axis=0, keepdims=True)
        dvh = dvv * lw
        dvp = rstd * (dvh - jnp.mean(dvh, axis=-1, keepdims=True) - vhat * jnp.mean(dvh * vhat, axis=-1, keepdims=True))
        dpre = jnp.concatenate([duu, dvp], axis=1) * _dgelu(pre)
        dbi_ref[...] += jnp.sum(dpre, axis=0, keepdims=True)
        dpre_ref[...] = dpre.astype(dpre_ref.dtype)

    return pl.pallas_call(
        body, name=name, grid=(S // L,),
        in_specs=[_row_spec(L, I), _row_spec(L, two_i), _full_spec((1, two_i)), _full_spec((1, I)), _full_spec((1, I)),
                  _full_spec(w_s.shape), _full_spec(bsx.shape)],
        out_specs=[_row_spec(L, two_i), _full_spec((1, two_i)), _full_spec((1, I)), _full_spec((1, I)), _full_spec(w_s.shape), _full_spec((L, I))],
        out_shape=[jax.ShapeDtypeStruct((S, two_i), bf16), jax.ShapeDtypeStruct((1, two_i), f32), jax.ShapeDtypeStruct((1, I), f32),
                   jax.ShapeDtypeStruct((1, I), f32), jax.ShapeDtypeStruct(w_s.shape, f32), jax.ShapeDtypeStruct((L, I), f32)],
        scratch_shapes=[pltpu.VMEM((L, I), f32)],
        compiler_params=_cparams(("arbitrary",)),
    )(do, pre, _row2(b_in), _row2(ln_w), _row2(ln_b), w_s, bsx)


def _lane_group_sum(acc, eg, name):
    NG = eg.shape[1]
    return _rowcall(lambda a, e: _dot(a, e, NN, HI), name=name, rows=[acc], fulls=[eg], out_rows=[(NG, f32)])[0]


def _ffn_fwd_fused(h1, nf_w, wg, wu, wd, name):
    S, D = h1.shape
    nb, F4, _ = wg.shape
    tm = _tile(S, 512)

    def body(h_ref, nf_ref, wg_ref, wu_ref, wd_ref, h2_ref, u_ref, g_ref, up_ref, a_ref, acc_ref):
        k = pl.program_id(1)

        @pl.when(k == 0)
        def _():
            u_ref[...] = _rms(h_ref[...], nf_ref[...], RMS_EPS).astype(u_ref.dtype)

        uv = u_ref[...]
        g = _dot(uv, wg_ref[...], NT)
        up = _dot(uv, wu_ref[...], NT)
        a = (_silu(g) * up).astype(bf16)
        g_ref[...] = g.astype(g_ref.dtype)
        up_ref[...] = up.astype(up_ref.dtype)
        a_ref[...] = a
        part = _dot(a, wd_ref[...], NN)

        @pl.when(k == 0)
        def _():
            acc_ref[...] = part

        @pl.when(k > 0)
        def _():
            acc_ref[...] += part

        @pl.when(k == nb - 1)
        def _():
            h2_ref[...] = h_ref[...] + acc_ref[...]

    row = pl.BlockSpec((tm, D), lambda i, k: (i, 0))
    wspec = pl.BlockSpec((None, F4, D), lambda i, k: (k, 0, 0))
    cspec = pl.BlockSpec((None, tm, F4), lambda i, k: (k, i, 0))
    chunk = jax.ShapeDtypeStruct((nb, S, F4), bf16)
    return pl.pallas_call(
        body, name=name, grid=(S // tm, nb),
        in_specs=[row, _full_spec((1, D)), wspec, wspec, pl.BlockSpec((None, F4, D), lambda i, k: (k, 0, 0))],
        out_specs=[row, row, cspec, cspec, cspec],
        out_shape=[jax.ShapeDtypeStruct((S, D), f32), jax.ShapeDtypeStruct((S, D), bf16), chunk, chunk, chunk],
        scratch_shapes=[pltpu.VMEM((tm, D), f32)],
        compiler_params=_cparams(("parallel", "arbitrary")),
    )(h1, _row2(nf_w), wg, wu, wd)


def _ffn_bwd_fused(dh, h1, nf_w, wd, wg, wu, G, U, name, after=None):
    S, D = dh.shape
    nb, F4, _ = wd.shape
    tm = _tile(S, 512)

    def body(dh_ref, h_ref, nf_ref, wd_ref, wg_ref, wu_ref, g_ref, up_ref, *rest):
        dg_ref, du_ref, dh1_ref, dnf_ref, acc_ref = rest[-5:]
        i, k = pl.program_id(0), pl.program_id(1)
        dA = _dot(dh_ref[...].astype(bf16), wd_ref[...], NT)
        g = g_ref[...].astype(f32)
        dg = (dA * up_ref[...].astype(f32) * _dsilu(g)).astype(bf16)
        du = (dA * _silu(g)).astype(bf16)
        dg_ref[...] = dg
        du_ref[...] = du
        part = _dot(dg, wg_ref[...], NN) + _dot(du, wu_ref[...], NN)

        @pl.when(k == 0)
        def _():
            acc_ref[...] = part

        @pl.when(k > 0)
        def _():
            acc_ref[...] += part

        @pl.when(k == nb - 1)
        def _():
            dx, dw = _rms_bwd(acc_ref[...], h_ref[...], nf_ref[...], RMS_EPS)
            dh1_ref[...] = dh_ref[...] + dx

            @pl.when(i == 0)
            def _():
                dnf_ref[...] = dw

            @pl.when(i > 0)
            def _():
                dnf_ref[...] += dw

    row = pl.BlockSpec((tm, D), lambda i, k: (i, 0))
    wspec = pl.BlockSpec((None, F4, D), lambda i, k: (k, 0, 0))
    cspec = pl.BlockSpec((None, tm, F4), lambda i, k: (k, i, 0))
    chunk = jax.ShapeDtypeStruct((nb, S, F4), bf16)
    return pl.pallas_call(
        body, name=name, grid=(S // tm, nb),
        in_specs=[row, row, _full_spec((1, D)), wspec, wspec, wspec, cspec, cspec] + ([] if after is None else [pl.BlockSpec(memory_space=pl.ANY)]),
        out_specs=[cspec, cspec, row, _full_spec((1, D))],
        out_shape=[chunk, chunk, jax.ShapeDtypeStruct((S, D), f32), jax.ShapeDtypeStruct((1, D), f32)],
        scratch_shapes=[pltpu.VMEM((tm, D), f32)],
        compiler_params=_cparams(("arbitrary", "arbitrary")),
    )(dh, h1, _row2(nf_w), wd, wg, wu, G, U, *([] if after is None else [after]))


def _rms_bwd_add(dres, du, h, w, name):
    D = h.shape[1]

    def fn(dr, du_, h_, w_):
        dx, dw = _rms_bwd(du_, h_, w_, RMS_EPS)
        return dr + dx, dw

    return _rowcall(fn, name=name, rows=[dres, du, h], fulls=[_row2(w)], out_rows=[(D, f32)], out_accs=[(1, D)])


def _ple_fwd(h, p_i, wp, pn, gn, wgate, name):
    D = h.shape[1]

    def fn(h_, p_, wp_, pn_, gn_, wg_):
        pe = _dot(p_.astype(bf16), wp_, NN)
        e = _rms(pe, pn_, RMS_EPS)
        q = _rms(h_, gn_, RMS_EPS)
        gate = _sigmoid(_dot(q.astype(bf16), wg_, NN))
        return h_ + gate * e, pe, gate

    return _rowcall(fn, name=name, rows=[h, p_i], fulls=[wp, _row2(pn), _row2(gn), wgate],
                    out_rows=[(D, f32), (D, f32), (D, f32)], tm=256)


def _ple_bwd(dh3, h, pe, gate, pn, gn, wgate, name, after=None):
    D = h.shape[1]

    def fn(d_, h_, pe_, gate_, pn_, gn_, wg_, *_):
        e = _rms(pe_, pn_, RMS_EPS)
        dzg = d_ * e * gate_ * (1.0 - gate_)
        dq = _dot(dzg.astype(bf16), wg_, NT)
        dxq, dgn = _rms_bwd(dq, h_, gn_, RMS_EPS)
        dpe, dpn = _rms_bwd(d_ * gate_, pe_, pn_, RMS_EPS)
        return d_ + dxq, dzg, dpe, _rms(h_, gn_, RMS_EPS), dpn, dgn

    return _rowcall(fn, name=name, rows=[dh3, h, pe, gate], fulls=[_row2(pn), _row2(gn), wgate] + ([] if after is None else [after]),
                    out_rows=[(D, f32), (D, bf16), (D, bf16), (D, bf16)], out_accs=[(1, D), (1, D)], tm=256)


def _loss_head(h, target, fn_w, name):
    D = h.shape[1]

    def fn(h_, t_, w_):
        diff = _rms(h_, w_, RMS_EPS) - t_
        loss = 0.5 * jnp.sum(jnp.mean(diff * diff, axis=-1, keepdims=True), axis=0, keepdims=True)
        dh, dw = _rms_bwd(diff * (1.0 / D), h_, w_, RMS_EPS)
        return dh, jnp.broadcast_to(loss, (1, LANES)), dw

    return _rowcall(fn, name=name, rows=[h, target], fulls=[_row2(fn_w)], out_rows=[(D, f32)], out_accs=[(1, LANES), (1, D)])


def _adamw(w, m, v, g, name):
    R, C = w.shape
    tr, tc = R, C
    while tr * tc > 256 * 1024 and tr % (2 * SUBLANES) == 0:
        tr //= 2
    while tr * tc > 256 * 1024 and tc % (2 * LANES) == 0:
        tc //= 2

    def body(w_ref, m_ref, v_ref, g_ref, d_ref, mo_ref, vo_ref):
        g = g_ref[...]
        mn = ADAM_B1 * m_ref[...] + (1.0 - ADAM_B1) * g
        vn = ADAM_B2 * v_ref[...] + (1.0 - ADAM_B2) * (g * g)
        m_hat = mn / (1.0 - ADAM_B1 ** ADAM_STEP)
        v_hat = vn / (1.0 - ADAM_B2 ** ADAM_STEP)
        d_ref[...] = -ADAM_LR * (m_hat / (jnp.sqrt(v_hat) + ADAM_EPS) + ADAM_WD * w_ref[...])
        mo_ref[...] = mn
        vo_ref[...] = vn

    spec = pl.BlockSpec((tr, tc), lambda i, j: (i, j))
    return pl.pallas_call(
        body, name=name, grid=(R // tr, C // tc), in_specs=[spec] * 4,
        out_specs=[spec] * 3, out_shape=[jax.ShapeDtypeStruct((R, C), f32)] * 3,
        compiler_params=_cparams(("parallel", "parallel")),
    )(w, m, v, g)


def _expand_onehot(n, per):
    lane = lax.broadcasted_iota(jnp.int32, (n, n * per), 1)
    row = lax.broadcasted_iota(jnp.int32, (n, n * per), 0)
    return (lane // per == row).astype(f32)


def _ssd_layer_fwd(h, nm_w, W, t):
    H = W["dt_bias"].shape[0]
    inner = H * HEADDIM
    G = (W["conv_b"].shape[0] - inner) // (2 * STATE)
    hn = _rms_fwd(h, nm_w, f"rms_mix_{t}")
    conv_dim = W["conv_b"].shape[0]
    wT = W["w_inT"]
    z = _mm(hn, wT, mode="nt", brows=(0, inner), name=f"ssd_z_{t}")
    xpre = _mm(hn, wT, mode="nt", brows=(inner, conv_dim), name=f"ssd_xbc_{t}")
    dt_pre = _mm(hn, wT, mode="nt", brows=(inner + conv_dim, H), name=f"ssd_dt_{t}")
    c, xbc = _conv_fwd(xpre, W["conv_w"], W["conv_b"], f"ssd_conv_{t}")
    _, dtx = _dt_fwd(dt_pre, W["dt_bias"], _expand_onehot(H, HEADDIM), f"ssd_dtx_{t}")
    a = -jnp.exp(W["a_log"])
    ax = _row2(jnp.repeat(a, HEADDIM))
    dx = _row2(jnp.repeat(W["d"], HEADDIM))
    y, states = _ssd_fwd(xbc, dtx, ax, dx, G, f"ssd_scan_{t}")
    yn = _gnorm_fwd(y, z, W["norm_w"], G, f"ssd_gnorm_{t}")
    h1 = _mm(yn, W["wout"], res=h, name=f"ssd_out_{t}")
    return h1, (h, hn, z, xpre, dt_pre, c, xbc, dtx, a, ax, dx, y, states, yn)


def _ssd_layer_bwd(dh1, saved, nm_w, W, t, after=None):
    h, hn, z, xpre, dt_pre, c, xbc, dtx, a, ax, dx, y, states, yn = saved
    H = W["dt_bias"].shape[0]
    inner = H * HEADDIM
    G = (W["conv_b"].shape[0] - inner) // (2 * STATE)
    dyn = _mm(dh1, W["wout"], mode="nt", after=after, name=f"ssd_out_dx_{t}")
    g_wout = _mm(yn, dh1, mode="tn", out_dtype=bf16, name=f"ssd_out_dw_{t}")
    dy, dz, g_normw = _gnorm_bwd(dyn, y, z, W["norm_w"], G, f"ssd_gnorm_bwd_{t}")
    dxbc, ddt, dax, ddx = _ssd_bwd(dy, y, xbc, dtx, ax, dx, states, _expand_onehot(H, HEADDIM).T, G, f"ssd_scan_bwd_{t}")
    dc, g_convw8, g_convb = _conv_bwd_dc(dxbc, c, xpre, f"ssd_conv_bwd_dc_{t}")
    dxpre = _conv_bwd_dx(dc, W["conv_w"], f"ssd_conv_bwd_dx_{t}")
    ddt_pre, g_dtb = _dt_bwd(ddt, dt_pre, W["dt_bias"], f"ssd_dt_bwd_{t}")
    conv_dim = W["conv_b"].shape[0]
    wT = W["w_inT"]
    g_wz = _mm(dz, hn, mode="tn", out_dtype=bf16, name=f"ssd_z_dw_{t}")
    g_wxbc = _mm(dxpre, hn, mode="tn", out_dtype=bf16, name=f"ssd_xbc_dw_{t}")
    g_wdt = _mm(ddt_pre, hn, mode="tn", out_dtype=bf16, name=f"ssd_dt_dw_{t}")
    dhn = _mm(dz, wT, brows=(0, inner), name=f"ssd_z_dx_{t}")
    dhn = _mm(dxpre, wT, brows=(inner, conv_dim), res=dhn, name=f"ssd_xbc_dx_{t}")
    dhn = _mm(ddt_pre, wT, brows=(inner + conv_dim, H), res=dhn, name=f"ssd_dt_dx_{t}")
    dh, g_nm = _rms_bwd_add(dh1, dhn, h, nm_w, f"rms_mix_bwd_{t}")
    grads = dict(
        w_inT=jnp.concatenate([g_wz, g_wxbc, g_wdt], axis=0), wout=g_wout,
        conv_w=g_convw8[:CONV_K], conv_b=g_convb[0], dt_bias=g_dtb[0],
        a_log=dax[0].reshape(H, HEADDIM)[:, 0] * a, d=jnp.sum(ddx[0].reshape(H, HEADDIM), axis=1),
        norm_w=g_normw[0], norm_mix=g_nm[0])
    return dh, grads


def _gmlp_layer_fwd(h, nm_w, W, t):
    NG, L, _ = W["w_s"].shape
    I = W["ln_w"].shape[0]
    hn = _rms_fwd(h, nm_w, f"rms_mix_{t}")
    pre = _mm(hn, W["win"], name=f"gmlp_in_{t}")
    bsx = jnp.repeat(W["b_s"].T, I // NG, axis=1)
    o = _gmlp_mid_fwd(pre, W["b_in"], W["ln_w"], W["ln_b"], W["w_s"], bsx, f"gmlp_mid_{t}")
    h1 = _mm(o, W["wout"], res=h, name=f"gmlp_out_{t}")
    return h1, (h, hn, pre, bsx, o)


def _gmlp_layer_bwd(dh1, saved, nm_w, W, t, after=None):
    h, hn, pre, bsx, o = saved
    NG = W["w_s"].shape[0]
    I = W["ln_w"].shape[0]
    do = _mm(dh1, W["wout"], mode="nt", after=after, name=f"gmlp_out_dx_{t}")
    g_wout = _mm(o, dh1, mode="tn", out_dtype=bf16, name=f"gmlp_out_dw_{t}")
    dpre, g_bin, g_lnw, g_lnb, g_ws, dbs = _gmlp_mid_bwd(do, pre, W["b_in"], W["ln_w"], W["ln_b"], W["w_s"], bsx, f"gmlp_mid_bwd_{t}")
    g_bs = _lane_group_sum(dbs, _expand_onehot(NG, I // NG).T, f"gmlp_bs_{t}").T
    g_win = _mm(hn, dpre, mode="tn", out_dtype=bf16, name=f"gmlp_in_dw_{t}")
    dhn = _mm(dpre, W["win"], mode="nt", name=f"gmlp_in_dx_{t}")
    dh, g_nm = _rms_bwd_add(dh1, dhn, h, nm_w, f"rms_mix_bwd_{t}")
    grads = dict(win=g_win, wout=g_wout, b_in=g_bin[0], ln_w=g_lnw[0], ln_b=g_lnb[0], w_s=g_ws, b_s=g_bs, norm_mix=g_nm[0])
    return dh, grads


def _ffn_fwd(h1, nf_w, W, t):
    h2, u, Gm, Um, A = _ffn_fwd_fused(h1, nf_w, W["wg"], W["wu"], W["wd"], f"ffn_fwd_{t}")
    return h2, (h1, u, Gm, Um, A)


def _ffn_bwd(dh2, saved, nf_w, W, t, after=None):
    h1, u, Gm, Um, A = saved
    dG, dU, dh1, g_nf = _ffn_bwd_fused(dh2, h1, nf_w, W["wd"], W["wg"], W["wu"], Gm, Um, f"ffn_bwd_{t}", after=after)
    g_wd = _mm(A, dh2, mode="tn", out_dtype=bf16, name=f"ffn_down_dw_{t}")
    g_wg = _mm(dG, u, mode="tn", out_dtype=bf16, name=f"ffn_gate_dw_{t}")
    g_wu = _mm(dU, u, mode="tn", out_dtype=bf16, name=f"ffn_up_dw_{t}")
    return dh1, dict(wg=g_wg, wu=g_wu, wd=g_wd, norm_ffn=g_nf[0])


def _local_step(x, p, target, norms, layer_weights, on_layer_grads=None, final_norm_grad=None):
    depth = p.shape[0]
    h = x
    saved = []
    for i in range(depth):
        Wm = layer_weights(i, "mix", h)
        if i % 2 == 0:
            h1, s_mix = _ssd_layer_fwd(h, norms["norm_mix"][i], Wm, i)
        else:
            h1, s_mix = _gmlp_layer_fwd(h, norms["norm_mix"][i], Wm, i)
        Wf = layer_weights(i, "ffn", h1)
        h2, s_ffn = _ffn_fwd(h1, norms["norm_ffn"][i], Wf, i)
        P = layer_weights(i, "ple", h2)
        h3, pe, gate = _ple_fwd(h2, p[i], P["wp"], P["pn"], P["gn"], P["wgate"], f"ple_{i}")
        saved.append((Wm, Wf, P, s_mix, s_ffn, (h2, pe, gate)))
        h = h3
    dh, loss, g_fn = _loss_head(h, target, norms["final_norm"], "loss_head")
    if final_norm_grad is not None:
        final_norm_grad[0] = g_fn[0]
    grads = [None] * depth
    tell = on_layer_grads if on_layer_grads is not None else (lambda i, part, g: None)
    after = None
    for i in reversed(range(depth)):
        Wm, Wf, P, s_mix, s_ffn, (h2, pe, gate) = saved[i]
        dh, dzg, dpe, q, g_pn, g_gn = _ple_bwd(dh, h2, pe, gate, P["pn"], P["gn"], P["wgate"], f"ple_bwd_{i}", after=after)
        g_ple = dict(wgate=_mm(q, dzg, mode="tn", out_dtype=bf16, name=f"ple_gate_dw_{i}"),
                     wp=_mm(p[i], dpe, mode="tn", out_dtype=bf16, name=f"ple_proj_dw_{i}"), pn=g_pn[0], gn=g_gn[0])
        after = tell(i, "ple", g_ple)
        dh, g_ffn = _ffn_bwd(dh, s_ffn, norms["norm_ffn"][i], Wf, i, after=after)
        after = tell(i, "ffn", g_ffn)
        if i % 2 == 0:
            dh, g_mix = _ssd_layer_bwd(dh, s_mix, norms["norm_mix"][i], Wm, i, after=after)
        else:
            dh, g_mix = _gmlp_layer_bwd(dh, s_mix, norms["norm_mix"][i], Wm, i, after=after)
        after = tell(i, "mix", g_mix)
        grads[i] = dict(mix=g_mix, ffn=g_ffn, ple=g_ple)
    return loss[0, 0], dh, g_fn[0], grads


def _flip(v, f):
    return 1 - v if f else v


_ANY = pl.BlockSpec(memory_space=pl.ANY)


_SEM = pl.BlockSpec(memory_space=pltpu.SEMAPHORE)
_DATAFLOW = pltpu.SideEffectType.DATAFLOW_SIDE_EFFECTING
_CHIP_FLIPS = ((1, 0), (0, 1), (1, 1))
_DMA = pltpu.SemaphoreType.DMA


def _structs(arrs):
    return [jax.ShapeDtypeStruct(a.shape, a.dtype) for a in arrs]


def _gather_copy(src, buf, send_sems, recv_sems, k, j, slot, x, y, c):
    c2 = src.shape[1] // 2
    fx, fy = _CHIP_FLIPS[j]
    nf = len(_CHIP_FLIPS)
    return pltpu.make_async_remote_copy(
        src_ref=src.at[:, pl.ds(c * c2, c2)], dst_ref=buf.at[slot, :, pl.ds(c * c2, c2)], send_sem=send_sems.at[nf * k + j],
        recv_sem=recv_sems.at[nf * k + j], device_id=(_flip(x, fx), _flip(y, fy), c), device_id_type=MESH)


def _gather_start(srcs, groups):
    n = len(srcs)
    ng = len(groups)
    nf = len(_CHIP_FLIPS)

    def body(*refs):
        src_refs, buf_refs, sems = refs[:n], refs[n:2 * n], refs[4 * n:]
        x, y, c = lax.axis_index("x"), lax.axis_index("y"), lax.axis_index("c")
        for gi, group in enumerate(groups):
            for k, o in enumerate(group):
                for j in range(nf):
                    _gather_copy(src_refs[o], buf_refs[o], sems[2 * gi], sems[2 * gi + 1], k, j, 2 * x + y, x, y, c).start()

    mychip = 2 * lax.axis_index("x") + lax.axis_index("y")
    inits = [lax.dynamic_update_slice(lax.empty((N_CHIPS,) + s.shape, s.dtype), s[None], (mychip, 0, 0)) for s in srcs]
    sem_shapes = [_DMA((nf * len(g),)) for g in groups for _ in range(2)]
    outs = pl.pallas_call(
        body, name="gather_start", in_specs=[_ANY] * (2 * n), out_specs=[_ANY] * (2 * n) + [_SEM] * (2 * ng),
        out_shape=_structs(srcs) + _structs(inits) + sem_shapes, input_output_aliases={i: i for i in range(2 * n)},
        compiler_params=pltpu.CompilerParams(has_side_effects=_DATAFLOW),
    )(*srcs, *inits)
    return outs[:n], outs[n:2 * n], [(outs[2 * n + 2 * gi], outs[2 * n + 2 * gi + 1]) for gi in range(ng)]


def _gather_wait(srcs, bufs, sems, after, name):
    n = len(srcs)
    nf = len(_CHIP_FLIPS)

    def body(*refs):
        src_refs, buf_refs, send_sems, recv_sems = refs[:n], refs[n:2 * n], refs[2 * n], refs[2 * n + 1]
        x, y, c = lax.axis_index("x"), lax.axis_index("y"), lax.axis_index("c")
        for k in range(n):
            for j, (fx, fy) in enumerate(_CHIP_FLIPS):
                cp = _gather_copy(src_refs[k], buf_refs[k], send_sems, recv_sems, k, j, 2 * _flip(x, fx) + _flip(y, fy), x, y, c)
                cp.wait_send()
                cp.wait_recv()

    outs = pl.pallas_call(
        body, name=name, in_specs=[_ANY] * (2 * n) + [_SEM, _SEM, _ANY], out_specs=[_ANY] * (2 * n),
        out_shape=_structs(srcs) + _structs(bufs), input_output_aliases={i: i for i in range(2 * n)},
        compiler_params=pltpu.CompilerParams(has_side_effects=_DATAFLOW),
    )(*srcs, *bufs, *sems, after)
    return outs[n:]


def _gather_forward(bufs, name):
    n = len(bufs)
    nf = len(_CHIP_FLIPS)

    def body(*refs):
        outs = refs[n:2 * n]
        send_sems, recv_sems = refs[2 * n:]
        x, y, c = lax.axis_index("x"), lax.axis_index("y"), lax.axis_index("c")

        def forward(k, j, h):
            c2 = bufs[k].shape[2] // 2
            fx, fy = _CHIP_FLIPS[j]
            part = outs[k].at[2 * _flip(x, fx) + _flip(y, fy), :, pl.ds(h * c2, c2)]
            return pltpu.make_async_remote_copy(src_ref=part, dst_ref=part, send_sem=send_sems.at[nf * k + j],
                                                recv_sem=recv_sems.at[nf * k + j], device_id=(x, y, 1 - c), device_id_type=MESH)

        sends = [forward(k, j, c) for k in range(n) for j in range(nf)]
        for cp in sends:
            cp.start()
        for k in range(n):
            for j in range(nf):
                forward(k, j, 1 - c).wait_recv()
        for cp in sends:
            cp.wait_send()

    return pl.pallas_call(
        body, name=name, in_specs=[_ANY] * n, out_specs=[_ANY] * n, out_shape=_structs(bufs),
        input_output_aliases={i: i for i in range(n)}, scratch_shapes=[_DMA((nf * n,)), _DMA((nf * n,))],
    )(*bufs)


def _half_struct(a, lead):
    return jax.ShapeDtypeStruct(lead + (a.shape[-2], a.shape[-1] // 2), a.dtype)


_DEVICE_FLIPS = tuple((f >> 2 & 1, f >> 1 & 1, f & 1) for f in range(1, N_DEV))


def _exchange_copy(srcs, lands, n, send_sems, recv_sems, i, j, slot, x, y, c):
    nf = len(_DEVICE_FLIPS)
    px, py, pc = (_flip(v, f) for v, f in zip((x, y, c), _DEVICE_FLIPS[j]))
    src = srcs[i]
    if i < n:
        c2 = src.shape[2] // 2
        src = src.at[2 * px + py, :, pl.ds(pc * c2, c2)]
    return pltpu.make_async_remote_copy(src_ref=src, dst_ref=lands[i].at[slot], send_sem=send_sems.at[nf * i + j],
                                        recv_sem=recv_sems.at[nf * i + j], device_id=(px, py, pc), device_id_type=MESH)


def _exchange_start(tensors, wholes, name):
    n, m = len(tensors), len(wholes)
    nf = len(_DEVICE_FLIPS)
    t = n + m
    land_structs = ([_half_struct(a, (N_DEV,)) for a in tensors] + [jax.ShapeDtypeStruct((N_DEV,) + w.shape, w.dtype) for w in wholes])

    def body(*refs):
        srcs, lands, send_sems, recv_sems, token = refs[:t], refs[2 * t:3 * t], refs[3 * t], refs[3 * t + 1], refs[3 * t + 2]
        x, y, c = lax.axis_index("x"), lax.axis_index("y"), lax.axis_index("c")
        for i in range(t):
            for j in range(nf):
                _exchange_copy(srcs, lands, n, send_sems, recv_sems, i, j, 4 * x + 2 * y + c, x, y, c).start()
        token[...] = jnp.zeros_like(token)

    outs = pl.pallas_call(
        body, name=name, in_specs=[_ANY] * t,
        out_specs=[_ANY] * (2 * t) + [_SEM, _SEM, pl.BlockSpec(memory_space=pltpu.VMEM)],
        out_shape=_structs(tensors) + _structs(wholes) + land_structs + [_DMA((nf * t,)), _DMA((nf * t,)),
                                                                          jax.ShapeDtypeStruct((SUBLANES, LANES), f32)],
        input_output_aliases={i: i for i in range(t)},
        compiler_params=pltpu.CompilerParams(has_side_effects=_DATAFLOW),
    )(*tensors, *wholes)
    return outs[:t], outs[t:2 * t], (outs[2 * t], outs[2 * t + 1]), outs[2 * t + 2]


def _exchange_wait(srcs, lands, n, sems, after, name):
    t = len(srcs)

    def body(*refs):
        src_refs, land_refs, send_sems, recv_sems = refs[:t], refs[t:2 * t], refs[2 * t], refs[2 * t + 1]
        x, y, c = lax.axis_index("x"), lax.axis_index("y"), lax.axis_index("c")
        for i in range(t):
            for j, (fx, fy, fc) in enumerate(_DEVICE_FLIPS):
                sender = 4 * _flip(x, fx) + 2 * _flip(y, fy) + _flip(c, fc)
                cp = _exchange_copy(src_refs, land_refs, n, send_sems, recv_sems, i, j, sender, x, y, c)
                cp.wait_send()
                cp.wait_recv()

    outs = pl.pallas_call(
        body, name=name, in_specs=[_ANY] * (2 * t) + [_SEM, _SEM, _ANY], out_specs=[_ANY] * (2 * t),
        out_shape=_structs(srcs) + _structs(lands), input_output_aliases={i: i for i in range(2 * t)},
        compiler_params=pltpu.CompilerParams(has_side_effects=_DATAFLOW),
    )(*srcs, *lands, *sems, after)
    return outs[:t], outs[t:]


def _sibling_join(bufs):
    flat = [(gi, l) for gi, b in enumerate(bufs) for l in range(b.shape[0])]
    n, n_buf = len(flat), len(bufs)

    def body(*refs):
        outs = refs[n_buf:2 * n_buf]
        send_sems, recv_sems = refs[2 * n_buf:]
        x, y, c = lax.axis_index("x"), lax.axis_index("y"), lax.axis_index("c")

        def push(i, h):
            gi, l = flat[i]
            c2 = bufs[gi].shape[2] // 2
            part = outs[gi].at[l, :, pl.ds(h * c2, c2)]
            return pltpu.make_async_remote_copy(src_ref=part, dst_ref=part, send_sem=send_sems.at[i], recv_sem=recv_sems.at[i],
                                                device_id=(x, y, 1 - c), device_id_type=MESH)

        sends = [push(i, c) for i in range(n)]
        for cp in sends:
            cp.start()
        for i in range(n):
            push(i, 1 - c).wait_recv()
        for cp in sends:
            cp.wait_send()

    dma = pltpu.SemaphoreType.DMA
    return pl.pallas_call(
        body, name="grads_sibling_join", in_specs=[_ANY] * n_buf, out_specs=[_ANY] * n_buf,
        out_shape=[jax.ShapeDtypeStruct(b.shape, b.dtype) for b in bufs],
        input_output_aliases={i: i for i in range(n_buf)},
        scratch_shapes=[dma((n,)), dma((n,))],
    )(*bufs)


def _device_sum(landed, own, place, name, into=None, layer=0, layers=1):
    ndev, R, C2 = landed.shape
    tr, tc = R, C2
    while ndev * tr * tc > 1024 * 1024 and tr % (4 * SUBLANES) == 0:
        tr //= 2
    while ndev * tr * tc > 1024 * 1024 and tc % (2 * LANES) == 0:
        tc //= 2
    ncb = C2 // tc

    def body(*refs):
        place_ref, l_ref, m_ref, o_ref = refs[0], refs[1], refs[2], refs[-1]
        me = 2 * place_ref[0] + place_ref[1]
        s = jnp.where(me == 0, m_ref[...].astype(f32), l_ref[0].astype(f32))
        for d in range(1, ndev):
            s = s + jnp.where(me == d, m_ref[...].astype(f32), l_ref[d].astype(f32))
        o_ref[...] = s

    in_specs = [pl.BlockSpec((ndev, tr, tc), lambda i, j, pr: (0, i, j)),
                pl.BlockSpec((None, tr, tc), lambda i, j, pr: (pr[0], i, pr[1] * ncb + j))]
    args = [place, landed, own]
    if into is not None:
        in_specs.append(_ANY)
        args.append(into)
    return pl.pallas_call(
        body, name=name, out_shape=jax.ShapeDtypeStruct((layers, R, 2 * C2), f32),
        grid_spec=pltpu.PrefetchScalarGridSpec(
            num_scalar_prefetch=1, grid=(R // tr, ncb), in_specs=in_specs,
            out_specs=pl.BlockSpec((None, tr, tc), lambda i, j, pr: (layer, i, pr[1] * ncb + j))),
        input_output_aliases={3: 0} if into is not None else {},
        compiler_params=_cparams(("parallel", "parallel")),
    )(*args)


def _device_sum_whole(landed, own, place, name):
    ndev, R, C = landed.shape
    tr = R
    while ndev * tr * C > 1024 * 1024 and tr % (2 * SUBLANES) == 0:
        tr //= 2

    def body(place_ref, l_ref, m_ref, o_ref):
        me = 2 * place_ref[0] + place_ref[1]
        s = jnp.where(me == 0, m_ref[...], l_ref[0])
        for d in range(1, ndev):
            s = s + jnp.where(me == d, m_ref[...], l_ref[d])
        o_ref[...] = s

    return pl.pallas_call(
        body, name=name, out_shape=jax.ShapeDtypeStruct((R, C), f32),
        grid_spec=pltpu.PrefetchScalarGridSpec(
            num_scalar_prefetch=1, grid=(R // tr,),
            in_specs=[pl.BlockSpec((ndev, tr, C), lambda i, pr: (0, i, 0)), pl.BlockSpec((tr, C), lambda i, pr: (i, 0))],
            out_specs=pl.BlockSpec((tr, C), lambda i, pr: (i, 0))),
        compiler_params=_cparams(("parallel",)),
    )(place, landed, own)


PACK_COLS = 1024
PACK_ROW_MULTIPLE = 64

BIG = ("ssd_w_in", "ssd_w_out", "gmlp_w_in", "gmlp_w_out", "ffn_w_gate", "ffn_w_up", "ffn_w_down", "ple_w_proj", "ple_w_gate")
SMALL_SHARDED = ("ssd_conv_w", "gmlp_b_in", "gmlp_ln_w", "gmlp_ln_b")
REP_EARLY = ("gmlp_w_s", "gmlp_b_s")
REP_LATE = ("norm_mix", "norm_ffn", "ssd_conv_b", "ssd_dt_bias", "ssd_a_log", "ssd_d", "ssd_norm_w", "ple_norm", "ple_gate_norm",
            "final_norm")
WEIGHTS = ("norm_mix", "norm_ffn", "ssd_w_in", "ssd_conv_w", "ssd_conv_b", "ssd_dt_bias", "ssd_a_log", "ssd_d", "ssd_norm_w", "ssd_w_out",
           "gmlp_w_in", "gmlp_b_in", "gmlp_ln_w", "gmlp_ln_b", "gmlp_w_s", "gmlp_b_s", "gmlp_w_out", "ffn_w_gate", "ffn_w_up",
           "ffn_w_down", "ple_w_proj", "ple_norm", "ple_gate_norm", "ple_w_gate", "final_norm")
TRANSPOSED = ("ssd_w_in", "ffn_w_gate", "ffn_w_up")


def _pack(arrs):
    flat = jnp.concatenate([a.reshape(-1).astype(f32) for a in arrs])
    per = PACK_COLS * PACK_ROW_MULTIPLE
    n = -(-flat.shape[0] // per) * per
    return jnp.pad(flat, (0, n - flat.shape[0])).reshape(-1, PACK_COLS)


def _unpack(buf, shapes):
    flat = buf.reshape(-1)
    out, o = [], 0
    for s in shapes:
        n = math.prod(s)
        out.append(flat[o:o + n].reshape(s))
        o += n
    return out


def _chip_major(g):
    r, c4 = g.shape
    return g.reshape(r, N_CHIPS, c4 // N_CHIPS).transpose(1, 0, 2)


def _from_chip_major(g):
    k, r, c = g.shape
    return g.transpose(1, 0, 2).reshape(r, k * c)


def _adamw_nd(w, m, v, g, name):
    shp = w.shape
    two = lambda a: a.reshape(-1, shp[-1])
    return [o.reshape(shp) for o in _adamw(two(w), two(m), two(v), two(g), name)]


def kernel(x, p, norm_mix, norm_ffn, ssd_w_in, ssd_conv_w, ssd_conv_b, ssd_dt_bias, ssd_a_log, ssd_d, ssd_norm_w, ssd_w_out, gmlp_w_in, gmlp_b_in, gmlp_ln_w, gmlp_ln_b, gmlp_w_s, gmlp_b_s, gmlp_w_out, ffn_w_gate, ffn_w_up, ffn_w_down, ple_w_proj, ple_norm, ple_gate_norm, ple_w_gate, final_norm, loss_target, m_norm_mix, m_norm_ffn, m_ssd_w_in, m_ssd_conv_w, m_ssd_conv_b, m_ssd_dt_bias, m_ssd_a_log, m_ssd_d, m_ssd_norm_w, m_ssd_w_out, m_gmlp_w_in, m_gmlp_b_in, m_gmlp_ln_w, m_gmlp_ln_b, m_gmlp_w_s, m_gmlp_b_s, m_gmlp_w_out, m_ffn_w_gate, m_ffn_w_up, m_ffn_w_down, m_ple_w_proj, m_ple_norm, m_ple_gate_norm, m_ple_w_gate, m_final_norm, v_norm_mix, v_norm_ffn, v_ssd_w_in, v_ssd_conv_w, v_ssd_conv_b, v_ssd_dt_bias, v_ssd_a_log, v_ssd_d, v_ssd_norm_w, v_ssd_w_out, v_gmlp_w_in, v_gmlp_b_in, v_gmlp_ln_w, v_gmlp_ln_b, v_gmlp_w_s, v_gmlp_b_s, v_gmlp_w_out, v_ffn_w_gate, v_ffn_w_up, v_ffn_w_down, v_ple_w_proj, v_ple_norm, v_ple_gate_norm, v_ple_w_gate, v_final_norm):
    given = dict(locals())
    view = lambda n, a: jnp.swapaxes(a, 1, 2) if n in TRANSPOSED else a
    w = {n: view(n, given[n]) for n in WEIGHTS}
    mom = {n: view(n, given["m_" + n]) for n in WEIGHTS}
    var = {n: view(n, given["v_" + n]) for n in WEIGHTS}
    depth = p.shape[0]
    n_ssd, n_gmlp = ssd_w_in.shape[0], gmlp_w_in.shape[0]
    inner = ssd_dt_bias.shape[1] * HEADDIM
    conv_dim = ssd_conv_b.shape[1]

    place = jnp.stack([2 * lax.axis_index("x") + lax.axis_index("y"), lax.axis_index("c")]).astype(jnp.int32)

    def part_keys(i, part):
        j = i // 2
        if part == "mix":
            names = (("ssd_w_in", j), ("ssd_w_out", j)) if i % 2 == 0 else (("gmlp_w_in", j), ("gmlp_w_out", j))
            return ((("small", 0),) if i == 0 else ()) + names
        if part == "ffn":
            return (("ffn_w_gate", i), ("ffn_w_up", i), ("ffn_w_down", i))
        return (("ple_w_proj", i), ("ple_w_gate", i))

    parts = [(i, part) for i in range(depth) for part in ("mix", "ffn", "ple")]
    keys, groups = [], {}
    for ip in parts:
        names = part_keys(*ip)
        groups[ip] = list(range(len(keys), len(keys) + len(names)))
        keys += names
    small_shapes = [w[n].shape for n in SMALL_SHARDED]
    srcs = [_pack([w[n] for n in SMALL_SHARDED]) if n == "small" else w[n][l].astype(bf16) for n, l in keys]
    srcs, landing, gather_sems = _gather_start(srcs, [groups[ip] for ip in parts])
    gather_sems = dict(zip(parts, gather_sems))
    small_full = {}

    def layer_weights(i, part, h):
        idx = groups[(i, part)]
        got = _gather_wait([srcs[o] for o in idx], [landing[o] for o in idx], gather_sems[(i, part)], h, f"gather_wait_{part}_{i}")
        gw = dict(zip([keys[o] for o in idx], _gather_forward(got, f"gather_forward_{part}_{i}")))
        rows = lambda a: a.reshape(-1, a.shape[-1])
        j = i // 2
        if part == "ffn":
            return dict(wg=gw[("ffn_w_gate", i)], wu=gw[("ffn_w_up", i)], wd=gw[("ffn_w_down", i)])
        if part == "ple":
            return dict(wp=_from_chip_major(gw[("ple_w_proj", i)]), pn=ple_norm[i], gn=ple_gate_norm[i], wgate=rows(gw[("ple_w_gate", i)]))
        if i == 0:
            by_chip = [_unpack(gw[("small", 0)][k], small_shapes) for k in range(N_CHIPS)]
            small_full.update({n: jnp.concatenate([by_chip[k][t] for k in range(N_CHIPS)], axis=-1) for t, n in enumerate(SMALL_SHARDED)})
        if i % 2 == 0:
            return dict(w_inT=rows(gw[("ssd_w_in", j)]),
                        conv_w=small_full["ssd_conv_w"][j], conv_b=ssd_conv_b[j], dt_bias=ssd_dt_bias[j], a_log=ssd_a_log[j],
                        d=ssd_d[j], norm_w=ssd_norm_w[j], wout=rows(gw[("ssd_w_out", j)]))
        return dict(win=_from_chip_major(gw[("gmlp_w_in", j)]), b_in=small_full["gmlp_b_in"][j], ln_w=small_full["gmlp_ln_w"][j],
                    ln_b=small_full["gmlp_ln_b"][j], w_s=gmlp_w_s[j], b_s=gmlp_b_s[j], wout=rows(gw[("gmlp_w_out", j)]))

    rows4 = lambda a: a.reshape((N_CHIPS, a.shape[0] // N_CHIPS) + a.shape[1:])
    cut = lambda a, k: a[..., k * (a.shape[-1] // N_CHIPS):(k + 1) * (a.shape[-1] // N_CHIPS)]
    layer_grads = {}
    in_flight = {}
    owns = {}

    def on_layer_grads(i, part, g):
        layer_grads[(i, part)] = g
        j = i // 2
        wholes = {}
        if part == "ffn":
            chunks = {("ffn_w_gate", i): g["wg"], ("ffn_w_up", i): g["wu"], ("ffn_w_down", i): g["wd"]}
        elif part == "ple":
            chunks = {("ple_w_proj", i): _chip_major(g["wp"]), ("ple_w_gate", i): rows4(g["wgate"])}
        elif i % 2 == 0:
            chunks = {("ssd_w_in", j): rows4(g["w_inT"]), ("ssd_w_out", j): rows4(g["wout"])}
        else:
            chunks = {("gmlp_w_in", j): _chip_major(g["win"]), ("gmlp_w_out", j): rows4(g["wout"])}
        stack = lambda prt, key, layers: jnp.stack([layer_grads[(l, prt)][key] for l in layers])
        ssd, gml, every = range(0, depth, 2), range(1, depth, 2), range(depth)
        if part == "mix" and i == 1:
            wholes["rep_early"] = _pack([stack("mix", "w_s", gml), stack("mix", "b_s", gml)])
        if part == "mix" and i == 0:
            small_g = dict(ssd_conv_w=stack("mix", "conv_w", ssd), gmlp_b_in=stack("mix", "b_in", gml),
                           gmlp_ln_w=stack("mix", "ln_w", gml), gmlp_ln_b=stack("mix", "ln_b", gml))
            chunks[("small", 0)] = jnp.stack([_pack([cut(small_g[n], k) for n in SMALL_SHARDED]) for k in range(N_CHIPS)])
            rep_g = dict(
                norm_mix=stack("mix", "norm_mix", every), norm_ffn=stack("ffn", "norm_ffn", every),
                ssd_conv_b=stack("mix", "conv_b", ssd), ssd_dt_bias=stack("mix", "dt_bias", ssd), ssd_a_log=stack("mix", "a_log", ssd),
                ssd_d=stack("mix", "d", ssd), ssd_norm_w=stack("mix", "norm_w", ssd), ple_norm=stack("ple", "pn", every),
                ple_gate_norm=stack("ple", "gn", every), final_norm=final_norm_grad[0])
            wholes["rep_late"] = _pack([rep_g[n] for n in REP_LATE])
        ks, wk = list(chunks), list(wholes)
        thru, lands, sems, token = _exchange_start([chunks[k] for k in ks], [wholes[k] for k in wk], f"grads_exchange_start_{part}_{i}")
        in_flight[(i, part)] = (ks, wk, thru, lands, sems)
        return token

    final_norm_grad = [None]
    norms = dict(norm_mix=norm_mix, norm_ffn=norm_ffn, final_norm=final_norm)
    loss_part, grad_x, g_fn, _ = _local_step(x[0], p[:, 0], loss_target[0], norms, layer_weights, on_layer_grads, final_norm_grad)
    loss = lax.psum(loss_part, ("x", "y", "c"))

    landed = {}
    for i, part in reversed(parts):
        ks, wk, thru, lands, sems = in_flight[(i, part)]
        thru, lands = _exchange_wait(thru, lands, len(ks), sems, grad_x, f"grads_exchange_wait_{part}_{i}")
        landed.update(dict(zip(ks + wk, lands)))
        owns.update(dict(zip(ks + wk, thru)))
    rep_totals = {k: _device_sum_whole(landed[k], owns[k], place, f"grads_sum_{k}") for k in ("rep_early", "rep_late")}
    bufs = []
    for n in BIG + ("small",):
        layers = w[n].shape[0] if n != "small" else 1
        buf = None
        for l in range(layers):
            buf = _device_sum(landed[(n, l)], owns[(n, l)], place, f"grads_sum_{n}_{l}", into=buf, layer=l, layers=layers)
        bufs.append(buf)
    reduced = _sibling_join(bufs)

    res = {}
    for n, gsum in zip(BIG, reduced):
        res[n] = [view(n, a) for a in [gsum] + _adamw_nd(w[n], mom[n], var[n], gsum, "adamw_" + n)]
    for names, gsum, tag in ((SMALL_SHARDED, reduced[-1][0], "adamw_small_sharded"), (REP_EARLY, rep_totals["rep_early"], "adamw_rep_early"),
                             (REP_LATE, rep_totals["rep_late"], "adamw_rep_late")):
        packs = [gsum] + list(_adamw(_pack([w[n] for n in names]), _pack([mom[n] for n in names]), _pack([var[n] for n in names]), gsum, tag))
        per_kind = [_unpack(pk, [w[n].shape for n in names]) for pk in packs]
        for i, n in enumerate(names):
            res[n] = [per_kind[k][i] for k in range(4)]
    return (loss, grad_x[None], *[res[n][0] for n in WEIGHTS], *[res[n][1] for n in WEIGHTS],
            *[res[n][2] for n in WEIGHTS], *[res[n][3] for n in WEIGHTS])
```

```python
import functools
import math

import jax
import jax.numpy as jnp
from jax import lax
from jax.experimental import pallas as pl
from jax.experimental.pallas import tpu as pltpu

f32 = jnp.float32
bf16 = jnp.bfloat16
HI = lax.Precision.HIGHEST

LANES = 128
SUBLANES = 8
VMEM_LIMIT_BYTES = 56 * 1024 * 1024

HEADDIM = 64
STATE = 128
CHUNK = 128
CONV_K = 4
RMS_EPS = 1e-6
LN_EPS = 1e-5
ADAM_LR = 0.001
ADAM_B1 = 0.9
ADAM_B2 = 0.999
ADAM_EPS = 1e-08
ADAM_WD = 0.01
ADAM_STEP = 10

FFN_SUBTILES = 2

N_CHIPS = 4
N_DEV = 8
MESH = pl.DeviceIdType.MESH


def _cparams(sem):
    return pltpu.CompilerParams(dimension_semantics=sem, vmem_limit_bytes=VMEM_LIMIT_BYTES)


def _tile(n, want):
    if n <= want:
        return n
    t = want
    while n % t:
        t //= 2
    return t


def _row_spec(tm, c):
    return pl.BlockSpec((tm, c), lambda i: (i, 0))


def _full_spec(shape):
    nd = len(shape)
    return pl.BlockSpec(tuple(shape), lambda *_: (0,) * nd)


def _sigmoid(x):
    return 1.0 / (1.0 + jnp.exp(-x))


def _silu(x):
    return x * _sigmoid(x)


def _dsilu(x):
    s = _sigmoid(x)
    return s * (1.0 + x * (1.0 - s))


def _gelu(x):
    return 0.5 * x * (1.0 + lax.erf(x * (1.0 / math.sqrt(2.0))))


def _dgelu(x):
    return 0.5 * (1.0 + lax.erf(x * (1.0 / math.sqrt(2.0)))) + x * jnp.exp(-0.5 * x * x) * (1.0 / math.sqrt(2.0 * math.pi))


def _softplus(x):
    return jnp.maximum(x, 0.0) + jnp.log(1.0 + jnp.exp(-jnp.abs(x)))


def _rms(x, w, eps):
    r = lax.rsqrt(jnp.mean(x * x, axis=-1, keepdims=True) + eps)
    return x * r * w


def _rms_bwd(dy, x, w, eps):
    r = lax.rsqrt(jnp.mean(x * x, axis=-1, keepdims=True) + eps)
    xh = x * r
    g = dy * w
    dx = r * (g - xh * jnp.mean(g * xh, axis=-1, keepdims=True))
    dw = jnp.sum(dy * xh, axis=0, keepdims=True)
    return dx, dw


def _dot(a, b, dims=(((1,), (0,)), ((), ())), precision=None):
    return lax.dot_general(a, b, dims, precision=precision, preferred_element_type=f32)


NN = (((1,), (0,)), ((), ()))
NT = (((1,), (1,)), ((), ()))
TN = (((0,), (0,)), ((), ()))


def _split3(x):
    hi = x.astype(bf16)
    r1 = x - hi.astype(f32)
    mid = r1.astype(bf16)
    return hi, mid, (r1 - mid.astype(f32)).astype(bf16)


def _dot01_left(m01, x):
    mb = m01.astype(bf16)
    hi, mid, lo = _split3(x)
    return _dot(mb, hi, NN) + _dot(mb, mid, NN) + _dot(mb, lo, NN)


def _dot01_right(x, m01):
    mb = m01.astype(bf16)
    hi, mid, lo = _split3(x)
    return _dot(hi, mb, NN) + _dot(mid, mb, NN) + _dot(lo, mb, NN)


def _mm(a, b, *, mode="nn", out_dtype=f32, res=None, kbatch=False, brows=None, after=None, tm=1024, tn=1024, tk=1024, name):
    a3, b3 = a.ndim == 3, b.ndim == 3
    nb = a.shape[0] if a3 else (b.shape[0] if b3 else 1)
    ash, bsh = a.shape[-2:], b.shape[-2:]
    if brows is not None:
        bsh = (brows[1], bsh[1])
    if mode == "nn":
        M, K, N = ash[0], ash[1], bsh[1]
    elif mode == "nt":
        M, K, N = ash[0], ash[1], bsh[0]
    else:
        K, M, N = ash[0], ash[1], bsh[1]
    tm, tn, tk = _tile(M, tm), (N if N % LANES else _tile(N, tn)), (K if K % LANES else _tile(K, tk))
    b0 = 0
    if brows is not None:
        assert mode in ("nn", "nt") and bsh[0] == (K if mode == "nn" else N)
        blk = tk if mode == "nn" else tn
        while brows[0] % blk:
            blk //= 2
        assert blk % LANES == 0 or blk == brows[1]
        b0 = brows[0] // blk
        tn, tk = (tn, blk) if mode == "nn" else (blk, tk)
    nk = K // tk
    if kbatch:
        assert a3 and b3
        grid = (1, M // tm, N // tn, nb * nk)
        bi = lambda g, k: k // nk
        ki = lambda g, k: k % nk
    else:
        grid = (nb, M // tm, N // tn, nk)
        bi = lambda g, k: g
        ki = lambda g, k: k
    nsteps = grid[3]

    def spec(is3, blk, imap):
        if is3:
            return pl.BlockSpec((None,) + blk, lambda g, i, j, k: (bi(g, k),) + imap(i, j, ki(g, k)))
        return pl.BlockSpec(blk, lambda g, i, j, k: imap(i, j, ki(g, k)))

    if mode == "nn":
        a_spec = spec(a3, (tm, tk), lambda i, j, k: (i, k))
        b_spec = spec(b3, (tk, tn), lambda i, j, k: (k + b0, j))
        dims = NN
    elif mode == "nt":
        a_spec = spec(a3, (tm, tk), lambda i, j, k: (i, k))
        b_spec = spec(b3, (tn, tk), lambda i, j, k: (j + b0, k))
        dims = NT
    else:
        a_spec = spec(a3, (tk, tm), lambda i, j, k: (k, i))
        b_spec = spec(b3, (tk, tn), lambda i, j, k: (k, j))
        dims = TN
    out3 = (a3 or b3) and not kbatch
    if out3:
        o_spec = pl.BlockSpec((None, tm, tn), lambda g, i, j, k: (g, i, j))
        o_shape = jax.ShapeDtypeStruct((nb, M, N), out_dtype)
    else:
        o_spec = pl.BlockSpec((tm, tn), lambda g, i, j, k: (i, j))
        o_shape = jax.ShapeDtypeStruct((M, N), out_dtype)
    in_specs = [a_spec, b_spec]
    args = [a, b]
    if res is not None:
        in_specs.append(pl.BlockSpec((tm, tn), lambda g, i, j, k: (i, j)))
        args.append(res)
    if after is not None:
        in_specs.append(pl.BlockSpec(memory_space=pl.ANY))
        args.append(after)

    def body(*refs):
        a_ref, b_ref = refs[:2]
        r_ref = refs[2] if res is not None else None
        o_ref, acc_ref = refs[-2:]
        k = pl.program_id(3)

        @pl.when(k == 0)
        def _():
            acc_ref[...] = jnp.zeros_like(acc_ref)

        acc_ref[...] += _dot(a_ref[...].astype(bf16), b_ref[...].astype(bf16), dims)

        @pl.when(k == nsteps - 1)
        def _():
            r = acc_ref[...]
            if res is not None:
                r = r + r_ref[...]
            o_ref[...] = r.astype(o_ref.dtype)

    return pl.pallas_call(
        body, name=name, grid=grid, in_specs=in_specs, out_specs=o_spec, out_shape=o_shape,
        scratch_shapes=[pltpu.VMEM((tm, tn), f32)],
        compiler_params=_cparams(("parallel", "parallel", "parallel", "arbitrary")),
    )(*args)


def _rowcall(fn, *, name, rows, fulls, out_rows, out_accs=(), tm=512):
    S = rows[0].shape[0]
    tm = _tile(S, tm)
    n_r, n_f, n_or, n_oa = len(rows), len(fulls), len(out_rows), len(out_accs)

    def body(*refs):
        ins = [r[...] for r in refs[:n_r + n_f]]
        outs = fn(*ins)
        if not isinstance(outs, (tuple, list)):
            outs = (outs,)
        o_refs = refs[n_r + n_f:]
        for o_ref, v in zip(o_refs[:n_or], outs[:n_or]):
            o_ref[...] = v.astype(o_ref.dtype)
        if n_oa:
            first = pl.program_id(0) == 0

            @pl.when(first)
            def _():
                for o_ref, v in zip(o_refs[n_or:], outs[n_or:]):
                    o_ref[...] = v

            @pl.when(jnp.logical_not(first))
            def _():
                for o_ref, v in zip(o_refs[n_or:], outs[n_or:]):
                    o_ref[...] += v

    in_specs = [_row_spec(tm, r.shape[1]) for r in rows] + [_full_spec(f.shape) for f in fulls]
    out_specs = [_row_spec(tm, c) for c, _ in out_rows] + [_full_spec(s) for s in out_accs]
    out_shape = [jax.ShapeDtypeStruct((S, c), d) for c, d in out_rows] + [jax.ShapeDtypeStruct(s, f32) for s in out_accs]
    res = pl.pallas_call(
        body, name=name, grid=(S // tm,), in_specs=in_specs, out_specs=out_specs, out_shape=out_shape,
        compiler_params=_cparams(("arbitrary",) if n_oa else ("parallel",)),
    )(*rows, *fulls)
    return res


def _row2(v):
    return v.reshape(1, -1)


def _rms_fwd(h, w, name):
    D = h.shape[1]
    return _rowcall(lambda x, w_: _rms(x, w_, RMS_EPS), name=name, rows=[h], fulls=[_row2(w)], out_rows=[(D, bf16)])[0]


def _conv_fwd(xpre, w, b, name):
    S, C = xpre.shape
    tm, tc = _tile(S, 512), _tile(C, 1024)
    hb = tm // SUBLANES

    def body(x_ref, halo_ref, w_ref, b_ref, c_ref, o_ref):
        i = pl.program_id(1)
        x = x_ref[...]
        halo = jnp.where(i > 0, halo_ref[...], 0.0)
        row = lax.broadcasted_iota(jnp.int32, x.shape, 0)
        row8 = lax.broadcasted_iota(jnp.int32, halo.shape, 0)
        x0 = x[0:SUBLANES, :]
        acc = x * w_ref[CONV_K - 1:CONV_K, :] + b_ref[...]
        acc0 = x0 * w_ref[CONV_K - 1:CONV_K, :] + b_ref[...]
        for k in range(1, CONV_K):
            wk = w_ref[CONV_K - 1 - k:CONV_K - k, :]
            acc = acc + pltpu.roll(x, k, axis=0) * wk
            acc0 = acc0 + jnp.where(row8 < k, pltpu.roll(halo, k, axis=0), pltpu.roll(x0, k, axis=0)) * wk
        c_ref[...] = acc
        o_ref[...] = _silu(acc)
        c_ref[0:SUBLANES, :] = acc0
        o_ref[0:SUBLANES, :] = _silu(acc0)

    return pl.pallas_call(
        body, name=name, grid=(C // tc, S // tm),
        in_specs=[pl.BlockSpec((tm, tc), lambda j, i: (i, j)),
                  pl.BlockSpec((SUBLANES, tc), lambda j, i: (jnp.maximum(i * hb - 1, 0), j)),
                  pl.BlockSpec((CONV_K, tc), lambda j, i: (0, j)),
                  pl.BlockSpec((1, tc), lambda j, i: (0, j))],
        out_specs=[pl.BlockSpec((tm, tc), lambda j, i: (i, j))] * 2,
        out_shape=[jax.ShapeDtypeStruct((S, C), f32)] * 2,
        compiler_params=_cparams(("parallel", "parallel")),
    )(xpre, xpre, w, _row2(b))


def _conv_bwd_dc(dxbc, c, xpre, name):
    S, C = xpre.shape
    tm, tc = _tile(S, 512), _tile(C, 1024)
    hb = tm // SUBLANES

    def body(d_ref, c_ref, x_ref, halo_ref, dc_ref, dw_ref, db_ref):
        i = pl.program_id(1)
        x = x_ref[...]
        dc = d_ref[...] * _dsilu(c_ref[...])
        dc_ref[...] = dc
        halo = jnp.where(i > 0, halo_ref[...], 0.0)
        row = lax.broadcasted_iota(jnp.int32, x.shape, 0)
        row8 = lax.broadcasted_iota(jnp.int32, halo.shape, 0)
        x0 = x[0:SUBLANES, :]
        dc0 = dc[0:SUBLANES, :]
        parts = [jnp.sum(dc * x, axis=0, keepdims=True)]
        for k in range(1, CONV_K):
            xs_big = jnp.where(row < SUBLANES, 0.0, pltpu.roll(x, k, axis=0))
            xs0 = jnp.where(row8 < k, pltpu.roll(halo, k, axis=0), pltpu.roll(x0, k, axis=0))
            parts.append(jnp.sum(dc * xs_big, axis=0, keepdims=True) + jnp.sum(dc0 * xs0, axis=0, keepdims=True))
        dw = jnp.concatenate([parts[CONV_K - 1 - k] for k in range(CONV_K)] + [jnp.zeros((SUBLANES - CONV_K, x.shape[1]), f32)], axis=0)
        db = jnp.sum(dc, axis=0, keepdims=True)

        @pl.when(i == 0)
        def _():
            dw_ref[...] = dw
            db_ref[...] = db

        @pl.when(i > 0)
        def _():
            dw_ref[...] += dw
            db_ref[...] += db

    return pl.pallas_call(
        body, name=name, grid=(C // tc, S // tm),
        in_specs=[pl.BlockSpec((tm, tc), lambda j, i: (i, j))] * 3 +
                 [pl.BlockSpec((SUBLANES, tc), lambda j, i: (jnp.maximum(i * hb - 1, 0), j))],
        out_specs=[pl.BlockSpec((tm, tc), lambda j, i: (i, j)),
                   pl.BlockSpec((SUBLANES, tc), lambda j, i: (0, j)),
                   pl.BlockSpec((1, tc), lambda j, i: (0, j))],
        out_shape=[jax.ShapeDtypeStruct((S, C), f32), jax.ShapeDtypeStruct((SUBLANES, C), f32), jax.ShapeDtypeStruct((1, C), f32)],
        compiler_params=_cparams(("parallel", "arbitrary")),
    )(dxbc, c, xpre, xpre)


def _conv_bwd_dx(dc, w, name):
    S, C = dc.shape
    tm, tc = _tile(S, 512), _tile(C, 1024)
    hb = tm // SUBLANES
    nrow = S // tm
    last8 = S // SUBLANES - 1

    def body(d_ref, nxt_ref, w_ref, o_ref):
        i = pl.program_id(1)
        d = d_ref[...]
        nxt = jnp.where(i < nrow - 1, nxt_ref[...], 0.0)
        row8 = lax.broadcasted_iota(jnp.int32, nxt.shape, 0)
        dl = d[tm - SUBLANES:tm, :]
        acc = d * w_ref[CONV_K - 1:CONV_K, :]
        accl = dl * w_ref[CONV_K - 1:CONV_K, :]
        for j in range(1, CONV_K):
            wk = w_ref[CONV_K - 1 - j:CONV_K - j, :]
            acc = acc + pltpu.roll(d, tm - j, axis=0) * wk
            accl = accl + jnp.where(row8 >= SUBLANES - j, pltpu.roll(nxt, SUBLANES - j, axis=0), pltpu.roll(dl, SUBLANES - j, axis=0)) * wk
        o_ref[...] = acc.astype(o_ref.dtype)
        o_ref[tm - SUBLANES:tm, :] = accl.astype(o_ref.dtype)

    return pl.pallas_call(
        body, name=name, grid=(C // tc, nrow),
        in_specs=[pl.BlockSpec((tm, tc), lambda j, i: (i, j)),
                  pl.BlockSpec((SUBLANES, tc), lambda j, i: (jnp.minimum((i + 1) * hb, last8), j)),
                  pl.BlockSpec((CONV_K, tc), lambda j, i: (0, j))],
        out_specs=pl.BlockSpec((tm, tc), lambda j, i: (i, j)),
        out_shape=jax.ShapeDtypeStruct((S, C), f32),
        compiler_params=_cparams(("parallel", "parallel")),
    )(dc, dc, w)


def _halfsum(v, lane_lo):
    s0 = jnp.sum(jnp.where(lane_lo, v, 0.0), axis=1, keepdims=True)
    s1 = jnp.sum(jnp.where(lane_lo, 0.0, v), axis=1, keepdims=True)
    return jnp.where(lane_lo, s0, s1)


def _ssd_specs(S, inner, GN, nchunks, rev):
    L = CHUNK
    cm = (lambda c: nchunks - 1 - c) if rev else (lambda c: c)
    xs = pl.BlockSpec((L, inner), lambda c: (cm(c), 0))
    bb = pl.BlockSpec((L, GN), lambda c: (cm(c), inner // GN))
    cc = pl.BlockSpec((L, GN), lambda c: (cm(c), inner // GN + 1))
    row = pl.BlockSpec((L, inner), lambda c: (cm(c), 0))
    vec = pl.BlockSpec((1, inner), lambda c: (0, 0))
    st = pl.BlockSpec((None, inner, STATE), lambda c: (cm(c), 0, 0))
    return xs, bb, cc, row, vec, st


def _ssd_fwd(xbc, dtx, ax, dx, G, name):
    S, inner = dtx.shape
    GN = G * STATE
    L = CHUNK
    nchunks = S // L
    npairs = inner // LANES
    ppg = npairs // G
    assert inner % GN == 0 and L == LANES and STATE == LANES

    def body(xs_ref, b_ref, c_ref, dtx_ref, ax_ref, dx_ref, y_ref, so_ref, st_ref):
        ci = pl.program_id(0)

        @pl.when(ci == 0)
        def _():
            st_ref[...] = jnp.zeros_like(st_ref)

        r = lax.broadcasted_iota(jnp.int32, (L, L), 0)
        cidx = lax.broadcasted_iota(jnp.int32, (L, L), 1)
        tril = cidx <= r
        lane_lo = cidx < HEADDIM
        xs = xs_ref[...]
        dtv = dtx_ref[...]
        X = xs * dtv
        da = dtv * ax_ref[...]
        cs = _dot01_left(tril, da)
        cs_last = jnp.sum(da, axis=0, keepdims=True)
        so_ref[...] = st_ref[...]
        for g in range(G):
            Bg = b_ref[:, g * STATE:(g + 1) * STATE].astype(bf16)
            Cg = c_ref[:, g * STATE:(g + 1) * STATE].astype(bf16)
            CB = _dot(Cg, Bg, NT)
            for j in range(ppg):
                lo = (g * ppg + j) * LANES
                tile = cs[:, lo:lo + LANES]
                rl = pltpu.roll(tile, HEADDIM, axis=1)
                Xp = X[:, lo:lo + LANES]
                prev = st_ref[lo:lo + LANES, :]
                ypair = _dot(Cg, prev.astype(bf16), NT) * jnp.exp(tile)
                for half in (0, 1):
                    hm = lane_lo if half == 0 else jnp.logical_not(lane_lo)
                    colb = jnp.where(hm, tile, rl)
                    Lm = jnp.exp(jnp.where(tril, colb - colb.T, -1e30))
                    W = (CB * Lm).astype(bf16)
                    ypair = ypair + _dot(W, jnp.where(hm, Xp, 0.0).astype(bf16), NN)
                y_ref[:, lo:lo + LANES] = ypair + xs[:, lo:lo + LANES] * dx_ref[:, lo:lo + LANES]
                last = cs_last[:, lo:lo + LANES]
                snew = _dot((Xp * jnp.exp(last - tile)).astype(bf16), Bg, TN)
                dec_rows = jnp.broadcast_to(jnp.exp(last), (L, LANES)).T
                st_ref[lo:lo + LANES, :] = dec_rows * prev + snew

    xs_s, b_s, c_s, row_s, vec_s, st_s = _ssd_specs(S, inner, GN, nchunks, False)
    return pl.pallas_call(
        body, name=name, grid=(nchunks,),
        in_specs=[xs_s, b_s, c_s, row_s, vec_s, vec_s],
        out_specs=[row_s, st_s],
        out_shape=[jax.ShapeDtypeStruct((S, inner), f32), jax.ShapeDtypeStruct((nchunks, inner, STATE), f32)],
        scratch_shapes=[pltpu.VMEM((inner, STATE), f32)],
        compiler_params=_cparams(("arbitrary",)),
    )(xbc, xbc, xbc, dtx, ax, dx)


def _ssd_bwd(dy, y, xbc, dtx, ax, dx, states, et, G, name):
    S, inner = dtx.shape
    H = et.shape[1]
    GN = G * STATE
    Cc = inner + 2 * GN
    L = CHUNK
    nchunks = S // L
    npairs = inner // LANES
    ppg = npairs // G

    def body(dy_ref, y_ref, xs_ref, b_ref, c_ref, dtx_ref, ax_ref, dx_ref, si_ref, et_ref,
             dxbc_ref, ddt_ref, dax_ref, ddx_ref, dst_ref, dA_ref, dAl_ref, ddtp_ref):
        ci = pl.program_id(0)

        @pl.when(ci == 0)
        def _():
            dst_ref[...] = jnp.zeros_like(dst_ref)
            dax_ref[...] = jnp.zeros_like(dax_ref)
            ddx_ref[...] = jnp.zeros_like(ddx_ref)

        r = lax.broadcasted_iota(jnp.int32, (L, L), 0)
        cidx = lax.broadcasted_iota(jnp.int32, (L, L), 1)
        tril = cidx <= r
        lane_lo = cidx < HEADDIM
        lane_lo1 = lax.broadcasted_iota(jnp.int32, (1, LANES), 1) < HEADDIM
        xs = xs_ref[...]
        dtv = dtx_ref[...]
        dyv = dy_ref[...]
        X = xs * dtv
        da = dtv * ax_ref[...]
        cs = _dot01_left(tril, da)
        cs_last = jnp.sum(da, axis=0, keepdims=True)
        for g in range(G):
            Bg = b_ref[:, g * STATE:(g + 1) * STATE].astype(bf16)
            Cg = c_ref[:, g * STATE:(g + 1) * STATE].astype(bf16)
            CB = _dot(Cg, Bg, NT)
            dCB = jnp.zeros((L, L), f32)
            dBg = jnp.zeros((L, STATE), f32)
            dCg = jnp.zeros((L, STATE), f32)
            for j in range(ppg):
                lo = (g * ppg + j) * LANES
                tile = cs[:, lo:lo + LANES]
                rl = pltpu.roll(tile, HEADDIM, axis=1)
                eA = jnp.exp(tile)
                Xp = X[:, lo:lo + LANES]
                dYp = dyv[:, lo:lo + LANES]
                xsp = xs[:, lo:lo + LANES]
                prev = si_ref[lo:lo + LANES, :]
                dSn = dst_ref[lo:lo + LANES, :]
                prev_b = prev.astype(bf16)
                dSn_b = dSn.astype(bf16)
                dYe = (dYp * eA).astype(bf16)
                dCg = dCg + _dot(dYe, prev_b, NN)
                dprev = _dot(dYe, Cg, TN)
                last = cs_last[:, lo:lo + LANES]
                w = jnp.exp(last - tile)
                BdS = _dot(Bg, dSn_b, NT)
                Xw = Xp * w
                XwB = Xw * BdS
                dAl_t = _halfsum(jnp.sum(XwB, axis=0, keepdims=True), lane_lo1)
                dBg = dBg + _dot(Xw.astype(bf16), dSn_b, NN)
                dec_rows = jnp.broadcast_to(jnp.exp(last), (L, LANES)).T
                dprev = dprev + dec_rows * dSn
                rsum = jnp.sum(dSn * prev * dec_rows, axis=1, keepdims=True)
                s0 = jnp.sum(rsum[0:HEADDIM], axis=0, keepdims=True)
                s1 = jnp.sum(rsum[HEADDIM:LANES], axis=0, keepdims=True)
                dAl_t = dAl_t + jnp.where(lane_lo1, s0, s1)
                dXd = jnp.zeros((L, LANES), f32)
                for half in (0, 1):
                    hm = lane_lo if half == 0 else jnp.logical_not(lane_lo)
                    colb = jnp.where(hm, tile, rl)
                    Lm = jnp.exp(jnp.where(tril, colb - colb.T, -1e30))
                    dYh = jnp.where(hm, dYp, 0.0).astype(bf16)
                    dW = _dot(dYh, jnp.where(hm, Xp, 0.0).astype(bf16), NT)
                    dXd = dXd + _dot((CB * Lm).astype(bf16), dYh, TN)
                    dCB = dCB + dW * Lm
                yoff = _dot(Cg, prev_b, NT) * eA
                ydiag = y_ref[:, lo:lo + LANES] - xsp * dx_ref[:, lo:lo + LANES] - yoff
                dYb = dYp.astype(bf16).astype(f32)
                Xb = Xp.astype(bf16).astype(f32)
                dA_t = _halfsum(dYb * ydiag - Xb * dXd + dYp * yoff - XwB, lane_lo)
                dXp = w * BdS + dXd
                dxbc_ref[:, lo:lo + LANES] = dXp * dtv[:, lo:lo + LANES] + dYp * dx_ref[:, lo:lo + LANES]
                ddtp_ref[:, lo:lo + LANES] = dXp * xsp
                ddx_ref[:, lo:lo + LANES] += jnp.sum(dYp * xsp, axis=0, keepdims=True)
                dA_ref[:, lo:lo + LANES] = dA_t
                dAl_ref[:, lo:lo + LANES] = dAl_t
                dst_ref[lo:lo + LANES, :] = dprev
            dCBb = dCB.astype(bf16)
            dxbc_ref[:, inner + g * STATE:inner + (g + 1) * STATE] = dBg + _dot(dCBb, Cg, TN)
            dxbc_ref[:, inner + GN + g * STATE:inner + GN + (g + 1) * STATE] = dCg + _dot(dCBb, Bg, NN)
        dda = _dot01_left(cidx >= r, dA_ref[...]) + dAl_ref[...]
        ddt_full = ddtp_ref[...] + dda * ax_ref[...] * (1.0 / HEADDIM)
        ddt_ref[...] = _dot01_right(ddt_full, et_ref[...])
        dax_ref[...] += jnp.sum(dda * dtv, axis=0, keepdims=True)

    xs_s, b_s, c_s, row_s, vec_s, st_s = _ssd_specs(S, inner, GN, nchunks, True)
    return pl.pallas_call(
        body, name=name, grid=(nchunks,),
        in_specs=[row_s, row_s, xs_s, b_s, c_s, row_s, vec_s, vec_s, st_s, _full_spec(et.shape)],
        out_specs=[pl.BlockSpec((L, Cc), lambda c: (nchunks - 1 - c, 0)),
                   pl.BlockSpec((L, H), lambda c: (nchunks - 1 - c, 0)), vec_s, vec_s],
        out_shape=[jax.ShapeDtypeStruct((S, Cc), f32), jax.ShapeDtypeStruct((S, H), f32),
                   jax.ShapeDtypeStruct((1, inner), f32), jax.ShapeDtypeStruct((1, inner), f32)],
        scratch_shapes=[pltpu.VMEM((inner, STATE), f32), pltpu.VMEM((L, inner), f32),
                        pltpu.VMEM((1, inner), f32), pltpu.VMEM((L, inner), f32)],
        compiler_params=_cparams(("arbitrary",)),
    )(dy, y, xbc, xbc, xbc, dtx, ax, dx, states, et)


def _dt_fwd(dt_pre, bias, e, name):
    H, inner = e.shape

    def fn(dp, b, e_):
        dt = _softplus(dp + b)
        return dt, _dot01_right(dt, e_)

    return _rowcall(fn, name=name, rows=[dt_pre], fulls=[_row2(bias), e], out_rows=[(H, f32), (inner, f32)])


def _dt_bwd(ddt, dt_pre, bias, name):
    H = ddt.shape[1]

    def fn(dd, dp, b):
        g = dd * _sigmoid(dp + b)
        return g, jnp.sum(g, axis=0, keepdims=True)

    return _rowcall(fn, name=name, rows=[ddt, dt_pre], fulls=[_row2(bias)], out_rows=[(H, f32)], out_accs=[(1, H)])


def _gnorm_fwd(y, z, w, G, name):
    inner = y.shape[1]
    gs = inner // G

    def fn(y_, z_, w_):
        gg = y_ * _silu(z_)
        outs = []
        for g in range(G):
            sl = slice(g * gs, (g + 1) * gs)
            outs.append(_rms(gg[:, sl], w_[:, sl], LN_EPS))
        return jnp.concatenate(outs, axis=1)

    return _rowcall(fn, name=name, rows=[y, z], fulls=[_row2(w)], out_rows=[(inner, bf16)], tm=256)[0]


def _gnorm_bwd(dyn, y, z, w, G, name):
    inner = y.shape[1]
    gs = inner // G

    def fn(d_, y_, z_, w_):
        sz = _silu(z_)
        gg = y_ * sz
        dgs, dws = [], []
        for g in range(G):
            sl = slice(g * gs, (g + 1) * gs)
            dg, dw = _rms_bwd(d_[:, sl], gg[:, sl], w_[:, sl], LN_EPS)
            dgs.append(dg)
            dws.append(dw)
        dgg = jnp.concatenate(dgs, axis=1)
        return dgg * sz, dgg * y_ * _dsilu(z_), jnp.concatenate(dws, axis=1)

    return _rowcall(fn, name=name, rows=[dyn, y, z], fulls=[_row2(w)], out_rows=[(inner, f32), (inner, f32)],
                    out_accs=[(1, inner)], tm=256)


def _gmlp_parts(pre, lw, lb, I):
    hp = _gelu(pre)
    uu = hp[:, :I]
    vp = hp[:, I:]
    xc = vp - jnp.mean(vp, axis=-1, keepdims=True)
    rstd = lax.rsqrt(jnp.mean(xc * xc, axis=-1, keepdims=True) + LN_EPS)
    vhat = xc * rstd
    return uu, vhat, rstd, vhat * lw + lb


def _gmlp_mid_fwd(pre, b_in, ln_w, ln_b, w_s, bsx, name):
    S, two_i = pre.shape
    I = two_i // 2
    NG = w_s.shape[0]
    gd = I // NG
    L = CHUNK

    def body(pre_ref, bi_ref, lw_ref, lb_ref, ws_ref, bsx_ref, o_ref):
        uu, _, _, vv = _gmlp_parts(pre_ref[...] + bi_ref[...], lw_ref[...], lb_ref[...], I)
        r = lax.broadcasted_iota(jnp.int32, (L, L), 0)
        cidx = lax.broadcasted_iota(jnp.int32, (L, L), 1)
        tril = cidx <= r
        for g in range(NG):
            sl = slice(g * gd, (g + 1) * gd)
            wg = jnp.where(tril, ws_ref[g], 0.0).astype(bf16)
            mixed = _dot(wg, vv[:, sl].astype(bf16), NN) + bsx_ref[:, sl]
            o_ref[:, sl] = (uu[:, sl] * mixed).astype(o_ref.dtype)

    return pl.pallas_call(
        body, name=name, grid=(S // L,),
        in_specs=[_row_spec(L, two_i), _full_spec((1, two_i)), _full_spec((1, I)), _full_spec((1, I)), _full_spec(w_s.shape), _full_spec(bsx.shape)],
        out_specs=_row_spec(L, I), out_shape=jax.ShapeDtypeStruct((S, I), bf16),
        compiler_params=_cparams(("parallel",)),
    )(pre, _row2(b_in), _row2(ln_w), _row2(ln_b), w_s, bsx)


def _gmlp_mid_bwd(do, pre, b_in, ln_w, ln_b, w_s, bsx, name):
    S, two_i = pre.shape
    I = two_i // 2
    NG = w_s.shape[0]
    gd = I // NG
    L = CHUNK

    def body(do_ref, pre_ref, bi_ref, lw_ref, lb_ref, ws_ref, bsx_ref, dpre_ref, dbi_ref, dlw_ref, dlb_ref, dws_ref, dbs_ref, dvv_ref):
        ci = pl.program_id(0)

        @pl.when(ci == 0)
        def _():
            for ref in (dbi_ref, dlw_ref, dlb_ref, dws_ref, dbs_ref):
                ref[...] = jnp.zeros_like(ref)

        pre = pre_ref[...] + bi_ref[...]
        lw = lw_ref[...]
        uu, vhat, rstd, vv = _gmlp_parts(pre, lw, lb_ref[...], I)
        dov = do_ref[...]
        r = lax.broadcasted_iota(jnp.int32, (L, L), 0)
        cidx = lax.broadcasted_iota(jnp.int32, (L, L), 1)
        tril = cidx <= r
        duus = []
        for g in range(NG):
            sl = slice(g * gd, (g + 1) * gd)
            wg = jnp.where(tril, ws_ref[g], 0.0).astype(bf16)
            vg = vv[:, sl].astype(bf16)
            mixed = _dot(wg, vg, NN) + bsx_ref[:, sl]
            duus.append(dov[:, sl] * mixed)
            dmixed = dov[:, sl] * uu[:, sl]
            dbs_ref[:, sl] += dmixed
            dmb = dmixed.astype(bf16)
            dvv_ref[:, sl] = _dot(wg, dmb, TN)
            dws_ref[g] += jnp.where(tril, _dot(dmb, vg, NT), 0.0)
        duu = jnp.concatenate(duus, axis=1)
        dvv = dvv_ref[...]
        dlw_ref[...] += jnp.sum(dvv * vhat, axis=0, keepdims=True)
        dlb_ref[...] += jnp.sum(dvv, axis=0, keepdims=True)
        dvh = dvv * lw
        dvp = rstd * (dvh - jnp.mean(dvh, axis=-1, keepdims=True) - vhat * jnp.mean(dvh * vhat, axis=-1, keepdims=True))
        dpre = jnp.concatenate([duu, dvp], axis=1) * _dgelu(pre)
        dbi_ref[...] += jnp.sum(dpre, axis=0, keepdims=True)
        dpre_ref[...] = dpre.astype(dpre_ref.dtype)

    return pl.pallas_call(
        body, name=name, grid=(S // L,),
        in_specs=[_row_spec(L, I), _row_spec(L, two_i), _full_spec((1, two_i)), _full_spec((1, I)), _full_spec((1, I)),
                  _full_spec(w_s.shape), _full_spec(bsx.shape)],
        out_specs=[_row_spec(L, two_i), _full_spec((1, two_i)), _full_spec((1, I)), _full_spec((1, I)), _full_spec(w_s.shape), _full_spec((L, I))],
        out_shape=[jax.ShapeDtypeStruct((S, two_i), bf16), jax.ShapeDtypeStruct((1, two_i), f32), jax.ShapeDtypeStruct((1, I), f32),
                   jax.ShapeDtypeStruct((1, I), f32), jax.ShapeDtypeStruct(w_s.shape, f32), jax.ShapeDtypeStruct((L, I), f32)],
        scratch_shapes=[pltpu.VMEM((L, I), f32)],
        compiler_params=_cparams(("arbitrary",)),
    )(do, pre, _row2(b_in), _row2(ln_w), _row2(ln_b), w_s, bsx)


def _lane_group_sum(acc, eg, name):
    NG = eg.shape[1]
    return _rowcall(lambda a, e: _dot(a, e, NN, HI), name=name, rows=[acc], fulls=[eg], out_rows=[(NG, f32)])[0]


def _ffn_fwd_fused(h1, nf_w, wg, wu, wd, name):
    S, D = h1.shape
    nb, F4, _ = wg.shape
    tm = _tile(S, 512)

    def body(h_ref, nf_ref, wg_ref, wu_ref, wd_ref, h2_ref, u_ref, g_ref, up_ref, a_ref, acc_ref):
        k = pl.program_id(1)

        @pl.when(k == 0)
        def _():
            u_ref[...] = _rms(h_ref[...], nf_ref[...], RMS_EPS).astype(u_ref.dtype)
            acc_ref[...] = jnp.zeros_like(acc_ref)

        for r in range(FFN_SUBTILES):
            rs = pl.ds(r * (tm // FFN_SUBTILES), tm // FFN_SUBTILES)
            uv = u_ref[rs, :]
            g = _dot(uv, wg_ref[...], NT)
            up = _dot(uv, wu_ref[...], NT)
            a = (_silu(g) * up).astype(bf16)
            g_ref[rs, :] = g.astype(g_ref.dtype)
            up_ref[rs, :] = up.astype(up_ref.dtype)
            a_ref[rs, :] = a
            acc_ref[rs, :] += _dot(a, wd_ref[...], NN)

        @pl.when(k == nb - 1)
        def _():
            h2_ref[...] = h_ref[...] + acc_ref[...]

    row = pl.BlockSpec((tm, D), lambda i, k: (i, 0))
    wspec = pl.BlockSpec((None, F4, D), lambda i, k: (k, 0, 0))
    cspec = pl.BlockSpec((None, tm, F4), lambda i, k: (k, i, 0))
    chunk = jax.ShapeDtypeStruct((nb, S, F4), bf16)
    return pl.pallas_call(
        body, name=name, grid=(S // tm, nb),
        in_specs=[row, _full_spec((1, D)), wspec, wspec, pl.BlockSpec((None, F4, D), lambda i, k: (k, 0, 0))],
        out_specs=[row, row, cspec, cspec, cspec],
        out_shape=[jax.ShapeDtypeStruct((S, D), f32), jax.ShapeDtypeStruct((S, D), bf16), chunk, chunk, chunk],
        scratch_shapes=[pltpu.VMEM((tm, D), f32)],
        compiler_params=_cparams(("parallel", "arbitrary")),
    )(h1, _row2(nf_w), wg, wu, wd)


def _ffn_bwd_fused(dh, h1, nf_w, wd, wg, wu, G, U, name, after=None):
    S, D = dh.shape
    nb, F4, _ = wd.shape
    tm = _tile(S, 512)

    def body(dh_ref, h_ref, nf_ref, wd_ref, wg_ref, wu_ref, g_ref, up_ref, *rest):
        dg_ref, du_ref, dh1_ref, dnf_ref, acc_ref = rest[-5:]
        i, k = pl.program_id(0), pl.program_id(1)

        @pl.when(k == 0)
        def _():
            acc_ref[...] = jnp.zeros_like(acc_ref)

        for r in range(FFN_SUBTILES):
            rs = pl.ds(r * (tm // FFN_SUBTILES), tm // FFN_SUBTILES)
            dA = _dot(dh_ref[rs, :].astype(bf16), wd_ref[...], NT)
            g = g_ref[rs, :].astype(f32)
            dg = (dA * up_ref[rs, :].astype(f32) * _dsilu(g)).astype(bf16)
            du = (dA * _silu(g)).astype(bf16)
            dg_ref[rs, :] = dg
            du_ref[rs, :] = du
            acc_ref[rs, :] += _dot(dg, wg_ref[...], NN) + _dot(du, wu_ref[...], NN)

        @pl.when(k == nb - 1)
        def _():
            dx, dw = _rms_bwd(acc_ref[...], h_ref[...], nf_ref[...], RMS_EPS)
            dh1_ref[...] = dh_ref[...] + dx

            @pl.when(i == 0)
            def _():
                dnf_ref[...] = dw

            @pl.when(i > 0)
            def _():
                dnf_ref[...] += dw

    row = pl.BlockSpec((tm, D), lambda i, k: (i, 0))
    wspec = pl.BlockSpec((None, F4, D), lambda i, k: (k, 0, 0))
    cspec = pl.BlockSpec((None, tm, F4), lambda i, k: (k, i, 0))
    chunk = jax.ShapeDtypeStruct((nb, S, F4), bf16)
    return pl.pallas_call(
        body, name=name, grid=(S // tm, nb),
        in_specs=[row, row, _full_spec((1, D)), wspec, wspec, wspec, cspec, cspec] + ([] if after is None else [pl.BlockSpec(memory_space=pl.ANY)]),
        out_specs=[cspec, cspec, row, _full_spec((1, D))],
        out_shape=[chunk, chunk, jax.ShapeDtypeStruct((S, D), f32), jax.ShapeDtypeStruct((1, D), f32)],
        scratch_shapes=[pltpu.VMEM((tm, D), f32)],
        compiler_params=_cparams(("arbitrary", "arbitrary")),
    )(dh, h1, _row2(nf_w), wd, wg, wu, G, U, *([] if after is None else [after]))


def _rms_bwd_add(dres, du, h, w, name):
    D = h.shape[1]

    def fn(dr, du_, h_, w_):
        dx, dw = _rms_bwd(du_, h_, w_, RMS_EPS)
        return dr + dx, dw

    return _rowcall(fn, name=name, rows=[dres, du, h], fulls=[_row2(w)], out_rows=[(D, f32)], out_accs=[(1, D)])


def _ple_fwd(h, p_i, wp, pn, gn, wgate, name):
    D = h.shape[1]

    def fn(h_, p_, wp_, pn_, gn_, wg_):
        pe = _dot(p_.astype(bf16), wp_, NN)
        e = _rms(pe, pn_, RMS_EPS)
        q = _rms(h_, gn_, RMS_EPS)
        gate = _sigmoid(_dot(q.astype(bf16), wg_, NN))
        return h_ + gate * e, pe, gate

    return _rowcall(fn, name=name, rows=[h, p_i], fulls=[wp, _row2(pn), _row2(gn), wgate],
                    out_rows=[(D, f32), (D, f32), (D, f32)], tm=256)


def _ple_bwd(dh3, h, pe, gate, pn, gn, wgate, name, after=None):
    D = h.shape[1]

    def fn(d_, h_, pe_, gate_, pn_, gn_, wg_, *_):
        e = _rms(pe_, pn_, RMS_EPS)
        dzg = d_ * e * gate_ * (1.0 - gate_)
        dq = _dot(dzg.astype(bf16), wg_, NT)
        dxq, dgn = _rms_bwd(dq, h_, gn_, RMS_EPS)
        dpe, dpn = _rms_bwd(d_ * gate_, pe_, pn_, RMS_EPS)
        return d_ + dxq, dzg, dpe, _rms(h_, gn_, RMS_EPS), dpn, dgn

    return _rowcall(fn, name=name, rows=[dh3, h, pe, gate], fulls=[_row2(pn), _row2(gn), wgate] + ([] if after is None else [after]),
                    out_rows=[(D, f32), (D, bf16), (D, bf16), (D, bf16)], out_accs=[(1, D), (1, D)], tm=256)


def _loss_head(h, target, fn_w, name):
    D = h.shape[1]

    def fn(h_, t_, w_):
        diff = _rms(h_, w_, RMS_EPS) - t_
        loss = 0.5 * jnp.sum(jnp.mean(diff * diff, axis=-1, keepdims=True), axis=0, keepdims=True)
        dh, dw = _rms_bwd(diff * (1.0 / D), h_, w_, RMS_EPS)
        return dh, jnp.broadcast_to(loss, (1, LANES)), dw

    return _rowcall(fn, name=name, rows=[h, target], fulls=[_row2(fn_w)], out_rows=[(D, f32)], out_accs=[(1, LANES), (1, D)])


def _adamw(w, m, v, g, name):
    R, C = w.shape
    tr, tc = R, C
    while tr * tc > 256 * 1024 and tr % (2 * SUBLANES) == 0:
        tr //= 2
    while tr * tc > 256 * 1024 and tc % (2 * LANES) == 0:
        tc //= 2

    def body(w_ref, m_ref, v_ref, g_ref, d_ref, mo_ref, vo_ref):
        g = g_ref[...]
        mn = ADAM_B1 * m_ref[...] + (1.0 - ADAM_B1) * g
        vn = ADAM_B2 * v_ref[...] + (1.0 - ADAM_B2) * (g * g)
        m_hat = mn / (1.0 - ADAM_B1 ** ADAM_STEP)
        v_hat = vn / (1.0 - ADAM_B2 ** ADAM_STEP)
        d_ref[...] = -ADAM_LR * (m_hat / (jnp.sqrt(v_hat) + ADAM_EPS) + ADAM_WD * w_ref[...])
        mo_ref[...] = mn
        vo_ref[...] = vn

    spec = pl.BlockSpec((tr, tc), lambda i, j: (i, j))
    return pl.pallas_call(
        body, name=name, grid=(R // tr, C // tc), in_specs=[spec] * 4,
        out_specs=[spec] * 3, out_shape=[jax.ShapeDtypeStruct((R, C), f32)] * 3,
        compiler_params=_cparams(("parallel", "parallel")),
    )(w, m, v, g)


def _expand_onehot(n, per):
    lane = lax.broadcasted_iota(jnp.int32, (n, n * per), 1)
    row = lax.broadcasted_iota(jnp.int32, (n, n * per), 0)
    return (lane // per == row).astype(f32)


def _ssd_layer_fwd(h, nm_w, W, t):
    H = W["dt_bias"].shape[0]
    inner = H * HEADDIM
    G = (W["conv_b"].shape[0] - inner) // (2 * STATE)
    hn = _rms_fwd(h, nm_w, f"rms_mix_{t}")
    conv_dim = W["conv_b"].shape[0]
    wT = W["w_inT"]
    z = _mm(hn, wT, mode="nt", brows=(0, inner), name=f"ssd_z_{t}")
    xpre = _mm(hn, wT, mode="nt", brows=(inner, conv_dim), name=f"ssd_xbc_{t}")
    dt_pre = _mm(hn, wT, mode="nt", brows=(inner + conv_dim, H), name=f"ssd_dt_{t}")
    c, xbc = _conv_fwd(xpre, W["conv_w"], W["conv_b"], f"ssd_conv_{t}")
    _, dtx = _dt_fwd(dt_pre, W["dt_bias"], _expand_onehot(H, HEADDIM), f"ssd_dtx_{t}")
    a = -jnp.exp(W["a_log"])
    ax = _row2(jnp.repeat(a, HEADDIM))
    dx = _row2(jnp.repeat(W["d"], HEADDIM))
    y, states = _ssd_fwd(xbc, dtx, ax, dx, G, f"ssd_scan_{t}")
    yn = _gnorm_fwd(y, z, W["norm_w"], G, f"ssd_gnorm_{t}")
    h1 = _mm(yn, W["wout"], res=h, name=f"ssd_out_{t}")
    return h1, (h, hn, z, xpre, dt_pre, c, xbc, dtx, a, ax, dx, y, states, yn)


def _ssd_layer_bwd(dh1, saved, nm_w, W, t, after=None):
    h, hn, z, xpre, dt_pre, c, xbc, dtx, a, ax, dx, y, states, yn = saved
    H = W["dt_bias"].shape[0]
    inner = H * HEADDIM
    G = (W["conv_b"].shape[0] - inner) // (2 * STATE)
    dyn = _mm(dh1, W["wout"], mode="nt", after=after, name=f"ssd_out_dx_{t}")
    g_wout = _mm(yn, dh1, mode="tn", out_dtype=bf16, name=f"ssd_out_dw_{t}")
    dy, dz, g_normw = _gnorm_bwd(dyn, y, z, W["norm_w"], G, f"ssd_gnorm_bwd_{t}")
    dxbc, ddt, dax, ddx = _ssd_bwd(dy, y, xbc, dtx, ax, dx, states, _expand_onehot(H, HEADDIM).T, G, f"ssd_scan_bwd_{t}")
    dc, g_convw8, g_convb = _conv_bwd_dc(dxbc, c, xpre, f"ssd_conv_bwd_dc_{t}")
    dxpre = _conv_bwd_dx(dc, W["conv_w"], f"ssd_conv_bwd_dx_{t}")
    ddt_pre, g_dtb = _dt_bwd(ddt, dt_pre, W["dt_bias"], f"ssd_dt_bwd_{t}")
    conv_dim = W["conv_b"].shape[0]
    wT = W["w_inT"]
    g_wz = _mm(dz, hn, mode="tn", out_dtype=bf16, name=f"ssd_z_dw_{t}")
    g_wxbc = _mm(dxpre, hn, mode="tn", out_dtype=bf16, name=f"ssd_xbc_dw_{t}")
    g_wdt = _mm(ddt_pre, hn, mode="tn", out_dtype=bf16, name=f"ssd_dt_dw_{t}")
    dhn = _mm(dz, wT, brows=(0, inner), name=f"ssd_z_dx_{t}")
    dhn = _mm(dxpre, wT, brows=(inner, conv_dim), res=dhn, name=f"ssd_xbc_dx_{t}")
    dhn = _mm(ddt_pre, wT, brows=(inner + conv_dim, H), res=dhn, name=f"ssd_dt_dx_{t}")
    dh, g_nm = _rms_bwd_add(dh1, dhn, h, nm_w, f"rms_mix_bwd_{t}")
    grads = dict(
        w_inT=jnp.concatenate([g_wz, g_wxbc, g_wdt], axis=0), wout=g_wout,
        conv_w=g_convw8[:CONV_K], conv_b=g_convb[0], dt_bias=g_dtb[0],
        a_log=dax[0].reshape(H, HEADDIM)[:, 0] * a, d=jnp.sum(ddx[0].reshape(H, HEADDIM), axis=1),
        norm_w=g_normw[0], norm_mix=g_nm[0])
    return dh, grads


def _gmlp_layer_fwd(h, nm_w, W, t):
    NG, L, _ = W["w_s"].shape
    I = W["ln_w"].shape[0]
    hn = _rms_fwd(h, nm_w, f"rms_mix_{t}")
    pre = _mm(hn, W["win"], name=f"gmlp_in_{t}")
    bsx = jnp.repeat(W["b_s"].T, I // NG, axis=1)
    o = _gmlp_mid_fwd(pre, W["b_in"], W["ln_w"], W["ln_b"], W["w_s"], bsx, f"gmlp_mid_{t}")
    h1 = _mm(o, W["wout"], res=h, name=f"gmlp_out_{t}")
    return h1, (h, hn, pre, bsx, o)


def _gmlp_layer_bwd(dh1, saved, nm_w, W, t, after=None):
    h, hn, pre, bsx, o = saved
    NG = W["w_s"].shape[0]
    I = W["ln_w"].shape[0]
    do = _mm(dh1, W["wout"], mode="nt", after=after, name=f"gmlp_out_dx_{t}")
    g_wout = _mm(o, dh1, mode="tn", out_dtype=bf16, name=f"gmlp_out_dw_{t}")
    dpre, g_bin, g_lnw, g_lnb, g_ws, dbs = _gmlp_mid_bwd(do, pre, W["b_in"], W["ln_w"], W["ln_b"], W["w_s"], bsx, f"gmlp_mid_bwd_{t}")
    g_bs = _lane_group_sum(dbs, _expand_onehot(NG, I // NG).T, f"gmlp_bs_{t}").T
    g_win = _mm(hn, dpre, mode="tn", out_dtype=bf16, name=f"gmlp_in_dw_{t}")
    dhn = _mm(dpre, W["win"], mode="nt", name=f"gmlp_in_dx_{t}")
    dh, g_nm = _rms_bwd_add(dh1, dhn, h, nm_w, f"rms_mix_bwd_{t}")
    grads = dict(win=g_win, wout=g_wout, b_in=g_bin[0], ln_w=g_lnw[0], ln_b=g_lnb[0], w_s=g_ws, b_s=g_bs, norm_mix=g_nm[0])
    return dh, grads


def _ffn_fwd(h1, nf_w, W, t):
    h2, u, Gm, Um, A = _ffn_fwd_fused(h1, nf_w, W["wg"], W["wu"], W["wd"], f"ffn_fwd_{t}")
    return h2, (h1, u, Gm, Um, A)


def _ffn_bwd(dh2, saved, nf_w, W, t, after=None):
    h1, u, Gm, Um, A = saved
    dG, dU, dh1, g_nf = _ffn_bwd_fused(dh2, h1, nf_w, W["wd"], W["wg"], W["wu"], Gm, Um, f"ffn_bwd_{t}", after=after)
    g_wd = _mm(A, dh2, mode="tn", out_dtype=bf16, name=f"ffn_down_dw_{t}")
    g_wg = _mm(dG, u, mode="tn", out_dtype=bf16, name=f"ffn_gate_dw_{t}")
    g_wu = _mm(dU, u, mode="tn", out_dtype=bf16, name=f"ffn_up_dw_{t}")
    return dh1, dict(wg=g_wg, wu=g_wu, wd=g_wd, norm_ffn=g_nf[0])


def _local_step(x, p, target, norms, layer_weights, on_layer_grads=None, final_norm_grad=None):
    depth = p.shape[0]
    h = x
    saved = []
    for i in range(depth):
        Wm = layer_weights(i, "mix", h)
        if i % 2 == 0:
            h1, s_mix = _ssd_layer_fwd(h, norms["norm_mix"][i], Wm, i)
        else:
            h1, s_mix = _gmlp_layer_fwd(h, norms["norm_mix"][i], Wm, i)
        Wf = layer_weights(i, "ffn", h1)
        h2, s_ffn = _ffn_fwd(h1, norms["norm_ffn"][i], Wf, i)
        P = layer_weights(i, "ple", h2)
        h3, pe, gate = _ple_fwd(h2, p[i], P["wp"], P["pn"], P["gn"], P["wgate"], f"ple_{i}")
        saved.append((Wm, Wf, P, s_mix, s_ffn, (h2, pe, gate)))
        h = h3
    dh, loss, g_fn = _loss_head(h, target, norms["final_norm"], "loss_head")
    if final_norm_grad is not None:
        final_norm_grad[0] = g_fn[0]
    grads = [None] * depth
    tell = on_layer_grads if on_layer_grads is not None else (lambda i, part, g: None)
    after = None
    for i in reversed(range(depth)):
        Wm, Wf, P, s_mix, s_ffn, (h2, pe, gate) = saved[i]
        dh, dzg, dpe, q, g_pn, g_gn = _ple_bwd(dh, h2, pe, gate, P["pn"], P["gn"], P["wgate"], f"ple_bwd_{i}", after=after)
        g_ple = dict(wgate=_mm(q, dzg, mode="tn", out_dtype=bf16, name=f"ple_gate_dw_{i}"),
                     wp=_mm(p[i], dpe, mode="tn", out_dtype=bf16, name=f"ple_proj_dw_{i}"), pn=g_pn[0], gn=g_gn[0])
        after = tell(i, "ple", g_ple)
        dh, g_ffn = _ffn_bwd(dh, s_ffn, norms["norm_ffn"][i], Wf, i, after=after)
        after = tell(i, "ffn", g_ffn)
        if i % 2 == 0:
            dh, g_mix = _ssd_layer_bwd(dh, s_mix, norms["norm_mix"][i], Wm, i, after=after)
        else:
            dh, g_mix = _gmlp_layer_bwd(dh, s_mix, norms["norm_mix"][i], Wm, i, after=after)
        after = tell(i, "mix", g_mix)
        grads[i] = dict(mix=g_mix, ffn=g_ffn, ple=g_ple)
    return loss[0, 0], dh, g_fn[0], grads


def _flip(v, f):
    return 1 - v if f else v


_ANY = pl.BlockSpec(memory_space=pl.ANY)


_SEM = pl.BlockSpec(memory_space=pltpu.SEMAPHORE)
_DATAFLOW = pltpu.SideEffectType.DATAFLOW_SIDE_EFFECTING
_CHIP_FLIPS = ((1, 0), (0, 1), (1, 1))
_DMA = pltpu.SemaphoreType.DMA


def _structs(arrs):
    return [jax.ShapeDtypeStruct(a.shape, a.dtype) for a in arrs]


def _gather_copy(src, buf, send_sems, recv_sems, k, j, slot, x, y, c):
    c2 = src.shape[1] // 2
    fx, fy = _CHIP_FLIPS[j]
    nf = len(_CHIP_FLIPS)
    return pltpu.make_async_remote_copy(
        src_ref=src.at[:, pl.ds(c * c2, c2)], dst_ref=buf.at[slot, :, pl.ds(c * c2, c2)], send_sem=send_sems.at[nf * k + j],
        recv_sem=recv_sems.at[nf * k + j], device_id=(_flip(x, fx), _flip(y, fy), c), device_id_type=MESH)


def _gather_start(srcs, groups):
    n = len(srcs)
    ng = len(groups)
    nf = len(_CHIP_FLIPS)

    def body(*refs):
        src_refs, buf_refs, sems = refs[:n], refs[n:2 * n], refs[4 * n:]
        x, y, c = lax.axis_index("x"), lax.axis_index("y"), lax.axis_index("c")
        for gi, group in enumerate(groups):
            for k, o in enumerate(group):
                for j in range(nf):
                    _gather_copy(src_refs[o], buf_refs[o], sems[2 * gi], sems[2 * gi + 1], k, j, 2 * x + y, x, y, c).start()

    mychip = 2 * lax.axis_index("x") + lax.axis_index("y")
    inits = [lax.dynamic_update_slice(lax.empty((N_CHIPS,) + s.shape, s.dtype), s[None], (mychip, 0, 0)) for s in srcs]
    sem_shapes = [_DMA((nf * len(g),)) for g in groups for _ in range(2)]
    outs = pl.pallas_call(
        body, name="gather_start", in_specs=[_ANY] * (2 * n), out_specs=[_ANY] * (2 * n) + [_SEM] * (2 * ng),
        out_shape=_structs(srcs) + _structs(inits) + sem_shapes, input_output_aliases={i: i for i in range(2 * n)},
        compiler_params=pltpu.CompilerParams(has_side_effects=_DATAFLOW),
    )(*srcs, *inits)
    return outs[:n], outs[n:2 * n], [(outs[2 * n + 2 * gi], outs[2 * n + 2 * gi + 1]) for gi in range(ng)]


def _gather_wait(srcs, bufs, sems, after, name):
    n = len(srcs)
    nf = len(_CHIP_FLIPS)

    def body(*refs):
        src_refs, buf_refs, send_sems, recv_sems = refs[:n], refs[n:2 * n], refs[2 * n], refs[2 * n + 1]
        x, y, c = lax.axis_index("x"), lax.axis_index("y"), lax.axis_index("c")
        for k in range(n):
            for j, (fx, fy) in enumerate(_CHIP_FLIPS):
                cp = _gather_copy(src_refs[k], buf_refs[k], send_sems, recv_sems, k, j, 2 * _flip(x, fx) + _flip(y, fy), x, y, c)
                cp.wait_send()
                cp.wait_recv()

    outs = pl.pallas_call(
        body, name=name, in_specs=[_ANY] * (2 * n) + [_SEM, _SEM, _ANY], out_specs=[_ANY] * (2 * n),
        out_shape=_structs(srcs) + _structs(bufs), input_output_aliases={i: i for i in range(2 * n)},
        compiler_params=pltpu.CompilerParams(has_side_effects=_DATAFLOW),
    )(*srcs, *bufs, *sems, after)
    return outs[n:]


def _gather_forward(bufs, name):
    n = len(bufs)
    nf = len(_CHIP_FLIPS)

    def body(*refs):
        outs = refs[n:2 * n]
        send_sems, recv_sems = refs[2 * n:]
        x, y, c = lax.axis_index("x"), lax.axis_index("y"), lax.axis_index("c")

        def forward(k, j, h):
            c2 = bufs[k].shape[2] // 2
            fx, fy = _CHIP_FLIPS[j]
            part = outs[k].at[2 * _flip(x, fx) + _flip(y, fy), :, pl.ds(h * c2, c2)]
            return pltpu.make_async_remote_copy(src_ref=part, dst_ref=part, send_sem=send_sems.at[nf * k + j],
                                                recv_sem=recv_sems.at[nf * k + j], device_id=(x, y, 1 - c), device_id_type=MESH)

        sends = [forward(k, j, c) for k in range(n) for j in range(nf)]
        for cp in sends:
            cp.start()
        for k in range(n):
            for j in range(nf):
                forward(k, j, 1 - c).wait_recv()
        for cp in sends:
            cp.wait_send()

    return pl.pallas_call(
        body, name=name, in_specs=[_ANY] * n, out_specs=[_ANY] * n, out_shape=_structs(bufs),
        input_output_aliases={i: i for i in range(n)}, scratch_shapes=[_DMA((nf * n,)), _DMA((nf * n,))],
    )(*bufs)


def _half_struct(a, lead):
    return jax.ShapeDtypeStruct(lead + (a.shape[-2], a.shape[-1] // 2), a.dtype)


_DEVICE_FLIPS = tuple((f >> 2 & 1, f >> 1 & 1, f & 1) for f in range(1, N_DEV))


def _exchange_copy(srcs, lands, n, send_sems, recv_sems, i, j, slot, x, y, c):
    nf = len(_DEVICE_FLIPS)
    px, py, pc = (_flip(v, f) for v, f in zip((x, y, c), _DEVICE_FLIPS[j]))
    src = srcs[i]
    if i < n:
        c2 = src.shape[2] // 2
        src = src.at[2 * px + py, :, pl.ds(pc * c2, c2)]
    return pltpu.make_async_remote_copy(src_ref=src, dst_ref=lands[i].at[slot], send_sem=send_sems.at[nf * i + j],
                                        recv_sem=recv_sems.at[nf * i + j], device_id=(px, py, pc), device_id_type=MESH)


def _exchange_start(tensors, wholes, name):
    n, m = len(tensors), len(wholes)
    nf = len(_DEVICE_FLIPS)
    t = n + m
    land_structs = ([_half_struct(a, (N_DEV,)) for a in tensors] + [jax.ShapeDtypeStruct((N_DEV,) + w.shape, w.dtype) for w in wholes])

    def body(*refs):
        srcs, lands, send_sems, recv_sems, token = refs[:t], refs[2 * t:3 * t], refs[3 * t], refs[3 * t + 1], refs[3 * t + 2]
        x, y, c = lax.axis_index("x"), lax.axis_index("y"), lax.axis_index("c")
        for i in range(t):
            for j in range(nf):
                _exchange_copy(srcs, lands, n, send_sems, recv_sems, i, j, 4 * x + 2 * y + c, x, y, c).start()
        token[...] = jnp.zeros_like(token)

    outs = pl.pallas_call(
        body, name=name, in_specs=[_ANY] * t,
        out_specs=[_ANY] * (2 * t) + [_SEM, _SEM, pl.BlockSpec(memory_space=pltpu.VMEM)],
        out_shape=_structs(tensors) + _structs(wholes) + land_structs + [_DMA((nf * t,)), _DMA((nf * t,)),
                                                                          jax.ShapeDtypeStruct((SUBLANES, LANES), f32)],
        input_output_aliases={i: i for i in range(t)},
        compiler_params=pltpu.CompilerParams(has_side_effects=_DATAFLOW),
    )(*tensors, *wholes)
    return outs[:t], outs[t:2 * t], (outs[2 * t], outs[2 * t + 1]), outs[2 * t + 2]


def _exchange_wait(srcs, lands, n, sems, after, name):
    t = len(srcs)

    def body(*refs):
        src_refs, land_refs, send_sems, recv_sems = refs[:t], refs[t:2 * t], refs[2 * t], refs[2 * t + 1]
        x, y, c = lax.axis_index("x"), lax.axis_index("y"), lax.axis_index("c")
        for i in range(t):
            for j, (fx, fy, fc) in enumerate(_DEVICE_FLIPS):
                sender = 4 * _flip(x, fx) + 2 * _flip(y, fy) + _flip(c, fc)
                cp = _exchange_copy(src_refs, land_refs, n, send_sems, recv_sems, i, j, sender, x, y, c)
                cp.wait_send()
                cp.wait_recv()

    outs = pl.pallas_call(
        body, name=name, in_specs=[_ANY] * (2 * t) + [_SEM, _SEM, _ANY], out_specs=[_ANY] * (2 * t),
        out_shape=_structs(srcs) + _structs(lands), input_output_aliases={i: i for i in range(2 * t)},
        compiler_params=pltpu.CompilerParams(has_side_effects=_DATAFLOW),
    )(*srcs, *lands, *sems, after)
    return outs[:t], outs[t:]


def _sibling_join(bufs, name):
    flat = [(gi, l) for gi, b in enumerate(bufs) for l in range(b.shape[0])]
    n, n_buf = len(flat), len(bufs)

    def body(*refs):
        outs = refs[n_buf:2 * n_buf]
        send_sems, recv_sems = refs[2 * n_buf:]
        x, y, c = lax.axis_index("x"), lax.axis_index("y"), lax.axis_index("c")

        def push(i, h):
            gi, l = flat[i]
            c2 = bufs[gi].shape[2] // 2
            part = outs[gi].at[l, :, pl.ds(h * c2, c2)]
            return pltpu.make_async_remote_copy(src_ref=part, dst_ref=part, send_sem=send_sems.at[i], recv_sem=recv_sems.at[i],
                                                device_id=(x, y, 1 - c), device_id_type=MESH)

        sends = [push(i, c) for i in range(n)]
        for cp in sends:
            cp.start()
        for i in range(n):
            push(i, 1 - c).wait_recv()
        for cp in sends:
            cp.wait_send()

    dma = pltpu.SemaphoreType.DMA
    return pl.pallas_call(
        body, name=name, in_specs=[_ANY] * n_buf, out_specs=[_ANY] * n_buf,
        out_shape=[jax.ShapeDtypeStruct(b.shape, b.dtype) for b in bufs],
        input_output_aliases={i: i for i in range(n_buf)},
        scratch_shapes=[dma((n,)), dma((n,))],
    )(*bufs)


def _device_sum(landed, own, place, name, into=None, layer=0, layers=1):
    ndev, R, C2 = landed.shape
    tr, tc = R, C2
    while ndev * tr * tc > 1024 * 1024 and tr % (4 * SUBLANES) == 0:
        tr //= 2
    while ndev * tr * tc > 1024 * 1024 and tc % (2 * LANES) == 0:
        tc //= 2
    ncb = C2 // tc

    def body(*refs):
        place_ref, l_ref, m_ref, o_ref = refs[0], refs[1], refs[2], refs[-1]
        me = 2 * place_ref[0] + place_ref[1]
        s = jnp.where(me == 0, m_ref[...].astype(f32), l_ref[0].astype(f32))
        for d in range(1, ndev):
            s = s + jnp.where(me == d, m_ref[...].astype(f32), l_ref[d].astype(f32))
        o_ref[...] = s

    in_specs = [pl.BlockSpec((ndev, tr, tc), lambda i, j, pr: (0, i, j)),
                pl.BlockSpec((None, tr, tc), lambda i, j, pr: (pr[0], i, pr[1] * ncb + j))]
    args = [place, landed, own]
    if into is not None:
        in_specs.append(_ANY)
        args.append(into)
    return pl.pallas_call(
        body, name=name, out_shape=jax.ShapeDtypeStruct((layers, R, 2 * C2), f32),
        grid_spec=pltpu.PrefetchScalarGridSpec(
            num_scalar_prefetch=1, grid=(R // tr, ncb), in_specs=in_specs,
            out_specs=pl.BlockSpec((None, tr, tc), lambda i, j, pr: (layer, i, pr[1] * ncb + j))),
        input_output_aliases={3: 0} if into is not None else {},
        compiler_params=_cparams(("parallel", "parallel")),
    )(*args)


def _device_sum_whole(landed, own, place, name):
    ndev, R, C = landed.shape
    tr = R
    while ndev * tr * C > 1024 * 1024 and tr % (2 * SUBLANES) == 0:
        tr //= 2

    def body(place_ref, l_ref, m_ref, o_ref):
        me = 2 * place_ref[0] + place_ref[1]
        s = jnp.where(me == 0, m_ref[...], l_ref[0])
        for d in range(1, ndev):
            s = s + jnp.where(me == d, m_ref[...], l_ref[d])
        o_ref[...] = s

    return pl.pallas_call(
        body, name=name, out_shape=jax.ShapeDtypeStruct((R, C), f32),
        grid_spec=pltpu.PrefetchScalarGridSpec(
            num_scalar_prefetch=1, grid=(R // tr,),
            in_specs=[pl.BlockSpec((ndev, tr, C), lambda i, pr: (0, i, 0)), pl.BlockSpec((tr, C), lambda i, pr: (i, 0))],
            out_specs=pl.BlockSpec((tr, C), lambda i, pr: (i, 0))),
        compiler_params=_cparams(("parallel",)),
    )(place, landed, own)


PACK_COLS = 1024
PACK_ROW_MULTIPLE = 64

BIG = ("ssd_w_in", "ssd_w_out", "gmlp_w_in", "gmlp_w_out", "ffn_w_gate", "ffn_w_up", "ffn_w_down", "ple_w_proj", "ple_w_gate")
SMALL_SHARDED = ("ssd_conv_w", "gmlp_b_in", "gmlp_ln_w", "gmlp_ln_b")
REP_EARLY = ("gmlp_w_s", "gmlp_b_s")
REP_LATE = ("norm_mix", "norm_ffn", "ssd_conv_b", "ssd_dt_bias", "ssd_a_log", "ssd_d", "ssd_norm_w", "ple_norm", "ple_gate_norm",
            "final_norm")
WEIGHTS = ("norm_mix", "norm_ffn", "ssd_w_in", "ssd_conv_w", "ssd_conv_b", "ssd_dt_bias", "ssd_a_log", "ssd_d", "ssd_norm_w", "ssd_w_out",
           "gmlp_w_in", "gmlp_b_in", "gmlp_ln_w", "gmlp_ln_b", "gmlp_w_s", "gmlp_b_s", "gmlp_w_out", "ffn_w_gate", "ffn_w_up",
           "ffn_w_down", "ple_w_proj", "ple_norm", "ple_gate_norm", "ple_w_gate", "final_norm")
TRANSPOSED = ("ssd_w_in", "ffn_w_gate", "ffn_w_up")


def _pack(arrs):
    flat = jnp.concatenate([a.reshape(-1).astype(f32) for a in arrs])
    per = PACK_COLS * PACK_ROW_MULTIPLE
    n = -(-flat.shape[0] // per) * per
    return jnp.pad(flat, (0, n - flat.shape[0])).reshape(-1, PACK_COLS)


def _unpack(buf, shapes):
    flat = buf.reshape(-1)
    out, o = [], 0
    for s in shapes:
        n = math.prod(s)
        out.append(flat[o:o + n].reshape(s))
        o += n
    return out


def _chip_major(g):
    r, c4 = g.shape
    return g.reshape(r, N_CHIPS, c4 // N_CHIPS).transpose(1, 0, 2)


def _from_chip_major(g):
    k, r, c = g.shape
    return g.transpose(1, 0, 2).reshape(r, k * c)


def _adamw_nd(w, m, v, g, name):
    shp = w.shape
    two = lambda a: a.reshape(-1, shp[-1])
    return [o.reshape(shp) for o in _adamw(two(w), two(m), two(v), two(g), name)]


def kernel(x, p, norm_mix, norm_ffn, ssd_w_in, ssd_conv_w, ssd_conv_b, ssd_dt_bias, ssd_a_log, ssd_d, ssd_norm_w, ssd_w_out, gmlp_w_in, gmlp_b_in, gmlp_ln_w, gmlp_ln_b, gmlp_w_s, gmlp_b_s, gmlp_w_out, ffn_w_gate, ffn_w_up, ffn_w_down, ple_w_proj, ple_norm, ple_gate_norm, ple_w_gate, final_norm, loss_target, m_norm_mix, m_norm_ffn, m_ssd_w_in, m_ssd_conv_w, m_ssd_conv_b, m_ssd_dt_bias, m_ssd_a_log, m_ssd_d, m_ssd_norm_w, m_ssd_w_out, m_gmlp_w_in, m_gmlp_b_in, m_gmlp_ln_w, m_gmlp_ln_b, m_gmlp_w_s, m_gmlp_b_s, m_gmlp_w_out, m_ffn_w_gate, m_ffn_w_up, m_ffn_w_down, m_ple_w_proj, m_ple_norm, m_ple_gate_norm, m_ple_w_gate, m_final_norm, v_norm_mix, v_norm_ffn, v_ssd_w_in, v_ssd_conv_w, v_ssd_conv_b, v_ssd_dt_bias, v_ssd_a_log, v_ssd_d, v_ssd_norm_w, v_ssd_w_out, v_gmlp_w_in, v_gmlp_b_in, v_gmlp_ln_w, v_gmlp_ln_b, v_gmlp_w_s, v_gmlp_b_s, v_gmlp_w_out, v_ffn_w_gate, v_ffn_w_up, v_ffn_w_down, v_ple_w_proj, v_ple_norm, v_ple_gate_norm, v_ple_w_gate, v_final_norm):
    given = dict(locals())
    view = lambda n, a: jnp.swapaxes(a, 1, 2) if n in TRANSPOSED else a
    w = {n: view(n, given[n]) for n in WEIGHTS}
    mom = {n: view(n, given["m_" + n]) for n in WEIGHTS}
    var = {n: view(n, given["v_" + n]) for n in WEIGHTS}
    depth = p.shape[0]
    n_ssd, n_gmlp = ssd_w_in.shape[0], gmlp_w_in.shape[0]
    inner = ssd_dt_bias.shape[1] * HEADDIM
    conv_dim = ssd_conv_b.shape[1]

    place = jnp.stack([2 * lax.axis_index("x") + lax.axis_index("y"), lax.axis_index("c")]).astype(jnp.int32)

    def part_keys(i, part):
        j = i // 2
        if part == "mix":
            names = (("ssd_w_in", j), ("ssd_w_out", j)) if i % 2 == 0 else (("gmlp_w_in", j), ("gmlp_w_out", j))
            return ((("small", 0),) if i == 0 else ()) + names
        if part == "ffn":
            return (("ffn_w_gate", i), ("ffn_w_up", i), ("ffn_w_down", i))
        return (("ple_w_proj", i), ("ple_w_gate", i))

    parts = [(i, part) for i in range(depth) for part in ("mix", "ffn", "ple")]
    keys, groups = [], {}
    for ip in parts:
        names = part_keys(*ip)
        groups[ip] = list(range(len(keys), len(keys) + len(names)))
        keys += names
    small_shapes = [w[n].shape for n in SMALL_SHARDED]
    srcs = [_pack([w[n] for n in SMALL_SHARDED]) if n == "small" else w[n][l].astype(bf16) for n, l in keys]
    srcs, landing, gather_sems = _gather_start(srcs, [groups[ip] for ip in parts])
    gather_sems = dict(zip(parts, gather_sems))
    small_full = {}

    def layer_weights(i, part, h):
        idx = groups[(i, part)]
        got = _gather_wait([srcs[o] for o in idx], [landing[o] for o in idx], gather_sems[(i, part)], h, f"gather_wait_{part}_{i}")
        gw = dict(zip([keys[o] for o in idx], _gather_forward(got, f"gather_forward_{part}_{i}")))
        rows = lambda a: a.reshape(-1, a.shape[-1])
        j = i // 2
        if part == "ffn":
            return dict(wg=gw[("ffn_w_gate", i)], wu=gw[("ffn_w_up", i)], wd=gw[("ffn_w_down", i)])
        if part == "ple":
            return dict(wp=_from_chip_major(gw[("ple_w_proj", i)]), pn=ple_norm[i], gn=ple_gate_norm[i], wgate=rows(gw[("ple_w_gate", i)]))
        if i == 0:
            by_chip = [_unpack(gw[("small", 0)][k], small_shapes) for k in range(N_CHIPS)]
            small_full.update({n: jnp.concatenate([by_chip[k][t] for k in range(N_CHIPS)], axis=-1) for t, n in enumerate(SMALL_SHARDED)})
        if i % 2 == 0:
            return dict(w_inT=rows(gw[("ssd_w_in", j)]),
                        conv_w=small_full["ssd_conv_w"][j], conv_b=ssd_conv_b[j], dt_bias=ssd_dt_bias[j], a_log=ssd_a_log[j],
                        d=ssd_d[j], norm_w=ssd_norm_w[j], wout=rows(gw[("ssd_w_out", j)]))
        return dict(win=_from_chip_major(gw[("gmlp_w_in", j)]), b_in=small_full["gmlp_b_in"][j], ln_w=small_full["gmlp_ln_w"][j],
                    ln_b=small_full["gmlp_ln_b"][j], w_s=gmlp_w_s[j], b_s=gmlp_b_s[j], wout=rows(gw[("gmlp_w_out", j)]))

    rows4 = lambda a: a.reshape((N_CHIPS, a.shape[0] // N_CHIPS) + a.shape[1:])
    cut = lambda a, k: a[..., k * (a.shape[-1] // N_CHIPS):(k + 1) * (a.shape[-1] // N_CHIPS)]
    layer_grads = {}
    in_flight = {}
    owns = {}

    def on_layer_grads(i, part, g):
        layer_grads[(i, part)] = g
        j = i // 2
        wholes = {}
        if part == "ffn":
            chunks = {("ffn_w_gate", i): g["wg"], ("ffn_w_up", i): g["wu"], ("ffn_w_down", i): g["wd"]}
        elif part == "ple":
            chunks = {("ple_w_proj", i): _chip_major(g["wp"]), ("ple_w_gate", i): rows4(g["wgate"])}
        elif i % 2 == 0:
            chunks = {("ssd_w_in", j): rows4(g["w_inT"]), ("ssd_w_out", j): rows4(g["wout"])}
        else:
            chunks = {("gmlp_w_in", j): _chip_major(g["win"]), ("gmlp_w_out", j): rows4(g["wout"])}
        stack = lambda prt, key, layers: jnp.stack([layer_grads[(l, prt)][key] for l in layers])
        ssd, gml, every = range(0, depth, 2), range(1, depth, 2), range(depth)
        if part == "mix" and i == 1:
            wholes["rep_early"] = _pack([stack("mix", "w_s", gml), stack("mix", "b_s", gml)])
        if part == "mix" and i == 0:
            small_g = dict(ssd_conv_w=stack("mix", "conv_w", ssd), gmlp_b_in=stack("mix", "b_in", gml),
                           gmlp_ln_w=stack("mix", "ln_w", gml), gmlp_ln_b=stack("mix", "ln_b", gml))
            chunks[("small", 0)] = jnp.stack([_pack([cut(small_g[n], k) for n in SMALL_SHARDED]) for k in range(N_CHIPS)])
            rep_g = dict(
                norm_mix=stack("mix", "norm_mix", every), norm_ffn=stack("ffn", "norm_ffn", every),
                ssd_conv_b=stack("mix", "conv_b", ssd), ssd_dt_bias=stack("mix", "dt_bias", ssd), ssd_a_log=stack("mix", "a_log", ssd),
                ssd_d=stack("mix", "d", ssd), ssd_norm_w=stack("mix", "norm_w", ssd), ple_norm=stack("ple", "pn", every),
                ple_gate_norm=stack("ple", "gn", every), final_norm=final_norm_grad[0])
            wholes["rep_late"] = _pack([rep_g[n] for n in REP_LATE])
        ks, wk = list(chunks), list(wholes)
        thru, lands, sems, token = _exchange_start([chunks[k] for k in ks], [wholes[k] for k in wk], f"grads_exchange_start_{part}_{i}")
        in_flight[(i, part)] = (ks, wk, thru, lands, sems)
        return token

    final_norm_grad = [None]
    norms = dict(norm_mix=norm_mix, norm_ffn=norm_ffn, final_norm=final_norm)
    loss_part, grad_x, g_fn, _ = _local_step(x[0], p[:, 0], loss_target[0], norms, layer_weights, on_layer_grads, final_norm_grad)
    loss = lax.psum(loss_part, ("x", "y", "c"))

    landed, res = {}, {}

    def wait_for(which, after):
        for i, part in which:
            ks, wk, thru, lands, sems = in_flight[(i, part)]
            thru, lands = _exchange_wait(thru, lands, len(ks), sems, after, f"grads_exchange_wait_{part}_{i}")
            landed.update(dict(zip(ks + wk, lands)))
            owns.update(dict(zip(ks + wk, thru)))

    def packed_update(names, gsum, tag):
        packs = [gsum] + list(_adamw(_pack([w[n] for n in names]), _pack([mom[n] for n in names]), _pack([var[n] for n in names]), gsum, tag))
        per_kind = [_unpack(pk, [w[n].shape for n in names]) for pk in packs]
        for t, n in enumerate(names):
            res[n] = [per_kind[k][t] for k in range(4)]

    def finish(big_names, with_small, rep_key, rep_names, tag):
        bufs = []
        for n in big_names + (("small",) if with_small else ()):
            layers = w[n].shape[0] if n != "small" else 1
            buf = None
            for l in range(layers):
                buf = _device_sum(landed[(n, l)], owns[(n, l)], place, f"grads_sum_{n}_{l}", into=buf, layer=l, layers=layers)
            bufs.append(buf)
        reduced = _sibling_join(bufs, f"grads_sibling_join_{tag}")
        for n, gsum in zip(big_names, reduced):
            res[n] = [view(n, a) for a in [gsum] + _adamw_nd(w[n], mom[n], var[n], gsum, "adamw_" + n)]
        if with_small:
            packed_update(SMALL_SHARDED, reduced[-1][0], "adamw_small_sharded")
        packed_update(rep_names, _device_sum_whole(landed[rep_key], owns[rep_key], place, f"grads_sum_{rep_key}"), f"adamw_{rep_key}")

    last = (0, "mix")
    late_big = tuple(n for n in BIG if n.startswith("ssd_"))
    early_big = tuple(n for n in BIG if n not in late_big)
    wait_for([ip for ip in reversed(parts) if ip != last], grad_x)
    finish(early_big, False, "rep_early", REP_EARLY, "early")
    wait_for([last], res[early_big[-1]][1])
    finish(late_big, True, "rep_late", REP_LATE, "late")
    return (loss, grad_x[None], *[res[n][0] for n in WEIGHTS], *[res[n][1] for n in WEIGHTS],
            *[res[n][2] for n in WEIGHTS], *[res[n][3] for n in WEIGHTS])
```

```python
import functools
import math

import jax
import jax.numpy as jnp
from jax import lax
from jax.experimental import pallas as pl
from jax.experimental.pallas import tpu as pltpu

f32 = jnp.float32
bf16 = jnp.bfloat16
HI = lax.Precision.HIGHEST

LANES = 128
SUBLANES = 8
VMEM_LIMIT_BYTES = 56 * 1024 * 1024

HEADDIM = 64
STATE = 128
CHUNK = 128
CONV_K = 4
RMS_EPS = 1e-6
LN_EPS = 1e-5
ADAM_LR = 0.001
ADAM_B1 = 0.9
ADAM_B2 = 0.999
ADAM_EPS = 1e-08
ADAM_WD = 0.01
ADAM_STEP = 10

FFN_SUBTILES = 2

N_CHIPS = 4
N_DEV = 8
MESH = pl.DeviceIdType.MESH


def _cparams(sem):
    return pltpu.CompilerParams(dimension_semantics=sem, vmem_limit_bytes=VMEM_LIMIT_BYTES)


def _tile(n, want):
    if n <= want:
        return n
    t = want
    while n % t:
        t //= 2
    return t


def _row_spec(tm, c):
    return pl.BlockSpec((tm, c), lambda i: (i, 0))


def _full_spec(shape):
    nd = len(shape)
    return pl.BlockSpec(tuple(shape), lambda *_: (0,) * nd)


def _sigmoid(x):
    return 1.0 / (1.0 + jnp.exp(-x))


def _silu(x):
    return x * _sigmoid(x)


def _dsilu(x):
    s = _sigmoid(x)
    return s * (1.0 + x * (1.0 - s))


def _gelu(x):
    return 0.5 * x * (1.0 + lax.erf(x * (1.0 / math.sqrt(2.0))))


def _dgelu(x):
    return 0.5 * (1.0 + lax.erf(x * (1.0 / math.sqrt(2.0)))) + x * jnp.exp(-0.5 * x * x) * (1.0 / math.sqrt(2.0 * math.pi))


def _softplus(x):
    return jnp.maximum(x, 0.0) + jnp.log(1.0 + jnp.exp(-jnp.abs(x)))


def _rms(x, w, eps):
    r = lax.rsqrt(jnp.mean(x * x, axis=-1, keepdims=True) + eps)
    return x * r * w


def _rms_bwd(dy, x, w, eps):
    r = lax.rsqrt(jnp.mean(x * x, axis=-1, keepdims=True) + eps)
    xh = x * r
    g = dy * w
    dx = r * (g - xh * jnp.mean(g * xh, axis=-1, keepdims=True))
    dw = jnp.sum(dy * xh, axis=0, keepdims=True)
    return dx, dw


def _dot(a, b, dims=(((1,), (0,)), ((), ())), precision=None):
    return lax.dot_general(a, b, dims, precision=precision, preferred_element_type=f32)


NN = (((1,), (0,)), ((), ()))
NT = (((1,), (1,)), ((), ()))
TN = (((0,), (0,)), ((), ()))


def _split3(x):
    hi = x.astype(bf16)
    r1 = x - hi.astype(f32)
    mid = r1.astype(bf16)
    return hi, mid, (r1 - mid.astype(f32)).astype(bf16)


def _dot01_left(m01, x):
    mb = m01.astype(bf16)
    hi, mid, lo = _split3(x)
    return _dot(mb, hi, NN) + _dot(mb, mid, NN) + _dot(mb, lo, NN)


def _dot01_right(x, m01):
    mb = m01.astype(bf16)
    hi, mid, lo = _split3(x)
    return _dot(hi, mb, NN) + _dot(mid, mb, NN) + _dot(lo, mb, NN)


def _mm(a, b, *, mode="nn", out_dtype=f32, res=None, kbatch=False, brows=None, after=None, tm=1024, tn=1024, tk=1024, name):
    a3, b3 = a.ndim == 3, b.ndim == 3
    nb = a.shape[0] if a3 else (b.shape[0] if b3 else 1)
    ash, bsh = a.shape[-2:], b.shape[-2:]
    if brows is not None:
        bsh = (brows[1], bsh[1])
    if mode == "nn":
        M, K, N = ash[0], ash[1], bsh[1]
    elif mode == "nt":
        M, K, N = ash[0], ash[1], bsh[0]
    else:
        K, M, N = ash[0], ash[1], bsh[1]
    tm, tn, tk = _tile(M, tm), (N if N % LANES else _tile(N, tn)), (K if K % LANES else _tile(K, tk))
    b0 = 0
    if brows is not None:
        assert mode in ("nn", "nt") and bsh[0] == (K if mode == "nn" else N)
        blk = tk if mode == "nn" else tn
        while brows[0] % blk:
            blk //= 2
        assert blk % LANES == 0 or blk == brows[1]
        b0 = brows[0] // blk
        tn, tk = (tn, blk) if mode == "nn" else (blk, tk)
    nk = K // tk
    if kbatch:
        assert a3 and b3
        grid = (1, M // tm, N // tn, nb * nk)
        bi = lambda g, k: k // nk
        ki = lambda g, k: k % nk
    else:
        grid = (nb, M // tm, N // tn, nk)
        bi = lambda g, k: g
        ki = lambda g, k: k
    nsteps = grid[3]

    def spec(is3, blk, imap):
        if is3:
            return pl.BlockSpec((None,) + blk, lambda g, i, j, k: (bi(g, k),) + imap(i, j, ki(g, k)))
        return pl.BlockSpec(blk, lambda g, i, j, k: imap(i, j, ki(g, k)))

    if mode == "nn":
        a_spec = spec(a3, (tm, tk), lambda i, j, k: (i, k))
        b_spec = spec(b3, (tk, tn), lambda i, j, k: (k + b0, j))
        dims = NN
    elif mode == "nt":
        a_spec = spec(a3, (tm, tk), lambda i, j, k: (i, k))
        b_spec = spec(b3, (tn, tk), lambda i, j, k: (j + b0, k))
        dims = NT
    else:
        a_spec = spec(a3, (tk, tm), lambda i, j, k: (k, i))
        b_spec = spec(b3, (tk, tn), lambda i, j, k: (k, j))
        dims = TN
    out3 = (a3 or b3) and not kbatch
    if out3:
        o_spec = pl.BlockSpec((None, tm, tn), lambda g, i, j, k: (g, i, j))
        o_shape = jax.ShapeDtypeStruct((nb, M, N), out_dtype)
    else:
        o_spec = pl.BlockSpec((tm, tn), lambda g, i, j, k: (i, j))
        o_shape = jax.ShapeDtypeStruct((M, N), out_dtype)
    in_specs = [a_spec, b_spec]
    args = [a, b]
    if res is not None:
        in_specs.append(pl.BlockSpec((tm, tn), lambda g, i, j, k: (i, j)))
        args.append(res)
    if after is not None:
        in_specs.append(pl.BlockSpec(memory_space=pl.ANY))
        args.append(after)

    def body(*refs):
        a_ref, b_ref = refs[:2]
        r_ref = refs[2] if res is not None else None
        o_ref, acc_ref = refs[-2:]
        k = pl.program_id(3)

        @pl.when(k == 0)
        def _():
            acc_ref[...] = jnp.zeros_like(acc_ref)

        acc_ref[...] += _dot(a_ref[...].astype(bf16), b_ref[...].astype(bf16), dims)

        @pl.when(k == nsteps - 1)
        def _():
            r = acc_ref[...]
            if res is not None:
                r = r + r_ref[...]
            o_ref[...] = r.astype(o_ref.dtype)

    return pl.pallas_call(
        body, name=name, grid=grid, in_specs=in_specs, out_specs=o_spec, out_shape=o_shape,
        scratch_shapes=[pltpu.VMEM((tm, tn), f32)],
        compiler_params=_cparams(("parallel", "parallel", "parallel", "arbitrary")),
    )(*args)


def _rowcall(fn, *, name, rows, fulls, out_rows, out_accs=(), tm=512):
    S = rows[0].shape[0]
    tm = _tile(S, tm)
    n_r, n_f, n_or, n_oa = len(rows), len(fulls), len(out_rows), len(out_accs)

    def body(*refs):
        ins = [r[...] for r in refs[:n_r + n_f]]
        outs = fn(*ins)
        if not isinstance(outs, (tuple, list)):
            outs = (outs,)
        o_refs = refs[n_r + n_f:]
        for o_ref, v in zip(o_refs[:n_or], outs[:n_or]):
            o_ref[...] = v.astype(o_ref.dtype)
        if n_oa:
            first = pl.program_id(0) == 0

            @pl.when(first)
            def _():
                for o_ref, v in zip(o_refs[n_or:], outs[n_or:]):
                    o_ref[...] = v

            @pl.when(jnp.logical_not(first))
            def _():
                for o_ref, v in zip(o_refs[n_or:], outs[n_or:]):
                    o_ref[...] += v

    in_specs = [_row_spec(tm, r.shape[1]) for r in rows] + [_full_spec(f.shape) for f in fulls]
    out_specs = [_row_spec(tm, c) for c, _ in out_rows] + [_full_spec(s) for s in out_accs]
    out_shape = [jax.ShapeDtypeStruct((S, c), d) for c, d in out_rows] + [jax.ShapeDtypeStruct(s, f32) for s in out_accs]
    res = pl.pallas_call(
        body, name=name, grid=(S // tm,), in_specs=in_specs, out_specs=out_specs, out_shape=out_shape,
        compiler_params=_cparams(("arbitrary",) if n_oa else ("parallel",)),
    )(*rows, *fulls)
    return res


def _row2(v):
    return v.reshape(1, -1)


def _rms_fwd(h, w, name):
    D = h.shape[1]
    return _rowcall(lambda x, w_: _rms(x, w_, RMS_EPS), name=name, rows=[h], fulls=[_row2(w)], out_rows=[(D, bf16)])[0]


def _conv_fwd(xpre, w, b, name):
    S, C = xpre.shape
    tm, tc = _tile(S, 512), _tile(C, 1024)
    hb = tm // SUBLANES

    def body(x_ref, halo_ref, w_ref, b_ref, c_ref, o_ref):
        i = pl.program_id(1)
        x = x_ref[...]
        halo = jnp.where(i > 0, halo_ref[...], 0.0)
        row = lax.broadcasted_iota(jnp.int32, x.shape, 0)
        row8 = lax.broadcasted_iota(jnp.int32, halo.shape, 0)
        x0 = x[0:SUBLANES, :]
        acc = x * w_ref[CONV_K - 1:CONV_K, :] + b_ref[...]
        acc0 = x0 * w_ref[CONV_K - 1:CONV_K, :] + b_ref[...]
        for k in range(1, CONV_K):
            wk = w_ref[CONV_K - 1 - k:CONV_K - k, :]
            acc = acc + pltpu.roll(x, k, axis=0) * wk
            acc0 = acc0 + jnp.where(row8 < k, pltpu.roll(halo, k, axis=0), pltpu.roll(x0, k, axis=0)) * wk
        c_ref[...] = acc
        o_ref[...] = _silu(acc)
        c_ref[0:SUBLANES, :] = acc0
        o_ref[0:SUBLANES, :] = _silu(acc0)

    return pl.pallas_call(
        body, name=name, grid=(C // tc, S // tm),
        in_specs=[pl.BlockSpec((tm, tc), lambda j, i: (i, j)),
                  pl.BlockSpec((SUBLANES, tc), lambda j, i: (jnp.maximum(i * hb - 1, 0), j)),
                  pl.BlockSpec((CONV_K, tc), lambda j, i: (0, j)),
                  pl.BlockSpec((1, tc), lambda j, i: (0, j))],
        out_specs=[pl.BlockSpec((tm, tc), lambda j, i: (i, j))] * 2,
        out_shape=[jax.ShapeDtypeStruct((S, C), f32)] * 2,
        compiler_params=_cparams(("parallel", "parallel")),
    )(xpre, xpre, w, _row2(b))


def _conv_bwd_dc(dxbc, c, xpre, name):
    S, C = xpre.shape
    tm, tc = _tile(S, 512), _tile(C, 1024)
    hb = tm // SUBLANES

    def body(d_ref, c_ref, x_ref, halo_ref, dc_ref, dw_ref, db_ref):
        i = pl.program_id(1)
        x = x_ref[...]
        dc = d_ref[...] * _dsilu(c_ref[...])
        dc_ref[...] = dc
        halo = jnp.where(i > 0, halo_ref[...], 0.0)
        row = lax.broadcasted_iota(jnp.int32, x.shape, 0)
        row8 = lax.broadcasted_iota(jnp.int32, halo.shape, 0)
        x0 = x[0:SUBLANES, :]
        dc0 = dc[0:SUBLANES, :]
        parts = [jnp.sum(dc * x, axis=0, keepdims=True)]
        for k in range(1, CONV_K):
            xs_big = jnp.where(row < SUBLANES, 0.0, pltpu.roll(x, k, axis=0))
            xs0 = jnp.where(row8 < k, pltpu.roll(halo, k, axis=0), pltpu.roll(x0, k, axis=0))
            parts.append(jnp.sum(dc * xs_big, axis=0, keepdims=True) + jnp.sum(dc0 * xs0, axis=0, keepdims=True))
        dw = jnp.concatenate([parts[CONV_K - 1 - k] for k in range(CONV_K)] + [jnp.zeros((SUBLANES - CONV_K, x.shape[1]), f32)], axis=0)
        db = jnp.sum(dc, axis=0, keepdims=True)

        @pl.when(i == 0)
        def _():
            dw_ref[...] = dw
            db_ref[...] = db

        @pl.when(i > 0)
        def _():
            dw_ref[...] += dw
            db_ref[...] += db

    return pl.pallas_call(
        body, name=name, grid=(C // tc, S // tm),
        in_specs=[pl.BlockSpec((tm, tc), lambda j, i: (i, j))] * 3 +
                 [pl.BlockSpec((SUBLANES, tc), lambda j, i: (jnp.maximum(i * hb - 1, 0), j))],
        out_specs=[pl.BlockSpec((tm, tc), lambda j, i: (i, j)),
                   pl.BlockSpec((SUBLANES, tc), lambda j, i: (0, j)),
                   pl.BlockSpec((1, tc), lambda j, i: (0, j))],
        out_shape=[jax.ShapeDtypeStruct((S, C), f32), jax.ShapeDtypeStruct((SUBLANES, C), f32), jax.ShapeDtypeStruct((1, C), f32)],
        compiler_params=_cparams(("parallel", "arbitrary")),
    )(dxbc, c, xpre, xpre)


def _conv_bwd_dx(dc, w, name):
    S, C = dc.shape
    tm, tc = _tile(S, 512), _tile(C, 1024)
    hb = tm // SUBLANES
    nrow = S // tm
    last8 = S // SUBLANES - 1

    def body(d_ref, nxt_ref, w_ref, o_ref):
        i = pl.program_id(1)
        d = d_ref[...]
        nxt = jnp.where(i < nrow - 1, nxt_ref[...], 0.0)
        row8 = lax.broadcasted_iota(jnp.int32, nxt.shape, 0)
        dl = d[tm - SUBLANES:tm, :]
        acc = d * w_ref[CONV_K - 1:CONV_K, :]
        accl = dl * w_ref[CONV_K - 1:CONV_K, :]
        for j in range(1, CONV_K):
            wk = w_ref[CONV_K - 1 - j:CONV_K - j, :]
            acc = acc + pltpu.roll(d, tm - j, axis=0) * wk
            accl = accl + jnp.where(row8 >= SUBLANES - j, pltpu.roll(nxt, SUBLANES - j, axis=0), pltpu.roll(dl, SUBLANES - j, axis=0)) * wk
        o_ref[...] = acc.astype(o_ref.dtype)
        o_ref[tm - SUBLANES:tm, :] = accl.astype(o_ref.dtype)

    return pl.pallas_call(
        body, name=name, grid=(C // tc, nrow),
        in_specs=[pl.BlockSpec((tm, tc), lambda j, i: (i, j)),
                  pl.BlockSpec((SUBLANES, tc), lambda j, i: (jnp.minimum((i + 1) * hb, last8), j)),
                  pl.BlockSpec((CONV_K, tc), lambda j, i: (0, j))],
        out_specs=pl.BlockSpec((tm, tc), lambda j, i: (i, j)),
        out_shape=jax.ShapeDtypeStruct((S, C), f32),
        compiler_params=_cparams(("parallel", "parallel")),
    )(dc, dc, w)


def _halfsum(v, lane_lo):
    s0 = jnp.sum(jnp.where(lane_lo, v, 0.0), axis=1, keepdims=True)
    s1 = jnp.sum(jnp.where(lane_lo, 0.0, v), axis=1, keepdims=True)
    return jnp.where(lane_lo, s0, s1)


def _ssd_specs(S, inner, GN, nchunks, rev):
    L = CHUNK
    cm = (lambda c: nchunks - 1 - c) if rev else (lambda c: c)
    xs = pl.BlockSpec((L, inner), lambda c: (cm(c), 0))
    bb = pl.BlockSpec((L, GN), lambda c: (cm(c), inner // GN))
    cc = pl.BlockSpec((L, GN), lambda c: (cm(c), inner // GN + 1))
    row = pl.BlockSpec((L, inner), lambda c: (cm(c), 0))
    vec = pl.BlockSpec((1, inner), lambda c: (0, 0))
    st = pl.BlockSpec((None, inner, STATE), lambda c: (cm(c), 0, 0))
    return xs, bb, cc, row, vec, st


def _ssd_fwd(xbc, dtx, ax, dx, G, name):
    S, inner = dtx.shape
    GN = G * STATE
    L = CHUNK
    nchunks = S // L
    npairs = inner // LANES
    ppg = npairs // G
    assert inner % GN == 0 and L == LANES and STATE == LANES

    def body(xs_ref, b_ref, c_ref, dtx_ref, ax_ref, dx_ref, y_ref, so_ref, st_ref):
        ci = pl.program_id(0)

        @pl.when(ci == 0)
        def _():
            st_ref[...] = jnp.zeros_like(st_ref)

        r = lax.broadcasted_iota(jnp.int32, (L, L), 0)
        cidx = lax.broadcasted_iota(jnp.int32, (L, L), 1)
        tril = cidx <= r
        lane_lo = cidx < HEADDIM
        xs = xs_ref[...]
        dtv = dtx_ref[...]
        X = xs * dtv
        da = dtv * ax_ref[...]
        cs = _dot01_left(tril, da)
        cs_last = jnp.sum(da, axis=0, keepdims=True)
        so_ref[...] = st_ref[...]
        for g in range(G):
            Bg = b_ref[:, g * STATE:(g + 1) * STATE].astype(bf16)
            Cg = c_ref[:, g * STATE:(g + 1) * STATE].astype(bf16)
            CB = _dot(Cg, Bg, NT)
            for j in range(ppg):
                lo = (g * ppg + j) * LANES
                tile = cs[:, lo:lo + LANES]
                rl = pltpu.roll(tile, HEADDIM, axis=1)
                Xp = X[:, lo:lo + LANES]
                prev = st_ref[lo:lo + LANES, :]
                ypair = _dot(Cg, prev.astype(bf16), NT) * jnp.exp(tile)
                for half in (0, 1):
                    hm = lane_lo if half == 0 else jnp.logical_not(lane_lo)
                    colb = jnp.where(hm, tile, rl)
                    Lm = jnp.exp(jnp.where(tril, colb - colb.T, -1e30))
                    W = (CB * Lm).astype(bf16)
                    ypair = ypair + _dot(W, jnp.where(hm, Xp, 0.0).astype(bf16), NN)
                y_ref[:, lo:lo + LANES] = ypair + xs[:, lo:lo + LANES] * dx_ref[:, lo:lo + LANES]
                last = cs_last[:, lo:lo + LANES]
                snew = _dot((Xp * jnp.exp(last - tile)).astype(bf16), Bg, TN)
                dec_rows = jnp.broadcast_to(jnp.exp(last), (L, LANES)).T
                st_ref[lo:lo + LANES, :] = dec_rows * prev + snew

    xs_s, b_s, c_s, row_s, vec_s, st_s = _ssd_specs(S, inner, GN, nchunks, False)
    return pl.pallas_call(
        body, name=name, grid=(nchunks,),
        in_specs=[xs_s, b_s, c_s, row_s, vec_s, vec_s],
        out_specs=[row_s, st_s],
        out_shape=[jax.ShapeDtypeStruct((S, inner), f32), jax.ShapeDtypeStruct((nchunks, inner, STATE), f32)],
        scratch_shapes=[pltpu.VMEM((inner, STATE), f32)],
        compiler_params=_cparams(("arbitrary",)),
    )(xbc, xbc, xbc, dtx, ax, dx)


def _ssd_bwd(dy, y, xbc, dtx, ax, dx, states, et, G, name):
    S, inner = dtx.shape
    H = et.shape[1]
    GN = G * STATE
    Cc = inner + 2 * GN
    L = CHUNK
    nchunks = S // L
    npairs = inner // LANES
    ppg = npairs // G

    def body(dy_ref, y_ref, xs_ref, b_ref, c_ref, dtx_ref, ax_ref, dx_ref, si_ref, et_ref,
             dxbc_ref, ddt_ref, dax_ref, ddx_ref, dst_ref, dA_ref, dAl_ref, ddtp_ref):
        ci = pl.program_id(0)

        @pl.when(ci == 0)
        def _():
            dst_ref[...] = jnp.zeros_like(dst_ref)
            dax_ref[...] = jnp.zeros_like(dax_ref)
            ddx_ref[...] = jnp.zeros_like(ddx_ref)

        r = lax.broadcasted_iota(jnp.int32, (L, L), 0)
        cidx = lax.broadcasted_iota(jnp.int32, (L, L), 1)
        tril = cidx <= r
        lane_lo = cidx < HEADDIM
        lane_lo1 = lax.broadcasted_iota(jnp.int32, (1, LANES), 1) < HEADDIM
        xs = xs_ref[...]
        dtv = dtx_ref[...]
        dyv = dy_ref[...]
        X = xs * dtv
        da = dtv * ax_ref[...]
        cs = _dot01_left(tril, da)
        cs_last = jnp.sum(da, axis=0, keepdims=True)
        for g in range(G):
            Bg = b_ref[:, g * STATE:(g + 1) * STATE].astype(bf16)
            Cg = c_ref[:, g * STATE:(g + 1) * STATE].astype(bf16)
            CB = _dot(Cg, Bg, NT)
            dCB = jnp.zeros((L, L), f32)
            dBg = jnp.zeros((L, STATE), f32)
            dCg = jnp.zeros((L, STATE), f32)
            for j in range(ppg):
                lo = (g * ppg + j) * LANES
                tile = cs[:, lo:lo + LANES]
                rl = pltpu.roll(tile, HEADDIM, axis=1)
                eA = jnp.exp(tile)
                Xp = X[:, lo:lo + LANES]
                dYp = dyv[:, lo:lo + LANES]
                xsp = xs[:, lo:lo + LANES]
                prev = si_ref[lo:lo + LANES, :]
                dSn = dst_ref[lo:lo + LANES, :]
                prev_b = prev.astype(bf16)
                dSn_b = dSn.astype(bf16)
                dYe = (dYp * eA).astype(bf16)
                dCg = dCg + _dot(dYe, prev_b, NN)
                dprev = _dot(dYe, Cg, TN)
                last = cs_last[:, lo:lo + LANES]
                w = jnp.exp(last - tile)
                BdS = _dot(Bg, dSn_b, NT)
                Xw = Xp * w
                XwB = Xw * BdS
                dAl_t = _halfsum(jnp.sum(XwB, axis=0, keepdims=True), lane_lo1)
                dBg = dBg + _dot(Xw.astype(bf16), dSn_b, NN)
                dec_rows = jnp.broadcast_to(jnp.exp(last), (L, LANES)).T
                dprev = dprev + dec_rows * dSn
                rsum = jnp.sum(dSn * prev * dec_rows, axis=1, keepdims=True)
                s0 = jnp.sum(rsum[0:HEADDIM], axis=0, keepdims=True)
                s1 = jnp.sum(rsum[HEADDIM:LANES], axis=0, keepdims=True)
                dAl_t = dAl_t + jnp.where(lane_lo1, s0, s1)
                dXd = jnp.zeros((L, LANES), f32)
                for half in (0, 1):
                    hm = lane_lo if half == 0 else jnp.logical_not(lane_lo)
                    colb = jnp.where(hm, tile, rl)
                    Lm = jnp.exp(jnp.where(tril, colb - colb.T, -1e30))
                    dYh = jnp.where(hm, dYp, 0.0).astype(bf16)
                    dW = _dot(dYh, jnp.where(hm, Xp, 0.0).astype(bf16), NT)
                    dXd = dXd + _dot((CB * Lm).astype(bf16), dYh, TN)
                    dCB = dCB + dW * Lm
                yoff = _dot(Cg, prev_b, NT) * eA
                ydiag = y_ref[:, lo:lo + LANES] - xsp * dx_ref[:, lo:lo + LANES] - yoff
                dYb = dYp.astype(bf16).astype(f32)
                Xb = Xp.astype(bf16).astype(f32)
                dA_t = _halfsum(dYb * ydiag - Xb * dXd + dYp * yoff - XwB, lane_lo)
                dXp = w * BdS + dXd
                dxbc_ref[:, lo:lo + LANES] = dXp * dtv[:, lo:lo + LANES] + dYp * dx_ref[:, lo:lo + LANES]
                ddtp_ref[:, lo:lo + LANES] = dXp * xsp
                ddx_ref[:, lo:lo + LANES] += jnp.sum(dYp * xsp, axis=0, keepdims=True)
                dA_ref[:, lo:lo + LANES] = dA_t
                dAl_ref[:, lo:lo + LANES] = dAl_t
                dst_ref[lo:lo + LANES, :] = dprev
            dCBb = dCB.astype(bf16)
            dxbc_ref[:, inner + g * STATE:inner + (g + 1) * STATE] = dBg + _dot(dCBb, Cg, TN)
            dxbc_ref[:, inner + GN + g * STATE:inner + GN + (g + 1) * STATE] = dCg + _dot(dCBb, Bg, NN)
        dda = _dot01_left(cidx >= r, dA_ref[...]) + dAl_ref[...]
        ddt_full = ddtp_ref[...] + dda * ax_ref[...] * (1.0 / HEADDIM)
        ddt_ref[...] = _dot01_right(ddt_full, et_ref[...])
        dax_ref[...] += jnp.sum(dda * dtv, axis=0, keepdims=True)

    xs_s, b_s, c_s, row_s, vec_s, st_s = _ssd_specs(S, inner, GN, nchunks, True)
    return pl.pallas_call(
        body, name=name, grid=(nchunks,),
        in_specs=[row_s, row_s, xs_s, b_s, c_s, row_s, vec_s, vec_s, st_s, _full_spec(et.shape)],
        out_specs=[pl.BlockSpec((L, Cc), lambda c: (nchunks - 1 - c, 0)),
                   pl.BlockSpec((L, H), lambda c: (nchunks - 1 - c, 0)), vec_s, vec_s],
        out_shape=[jax.ShapeDtypeStruct((S, Cc), f32), jax.ShapeDtypeStruct((S, H), f32),
                   jax.ShapeDtypeStruct((1, inner), f32), jax.ShapeDtypeStruct((1, inner), f32)],
        scratch_shapes=[pltpu.VMEM((inner, STATE), f32), pltpu.VMEM((L, inner), f32),
                        pltpu.VMEM((1, inner), f32), pltpu.VMEM((L, inner), f32)],
        compiler_params=_cparams(("arbitrary",)),
    )(dy, y, xbc, xbc, xbc, dtx, ax, dx, states, et)


def _dt_fwd(dt_pre, bias, e, name):
    H, inner = e.shape

    def fn(dp, b, e_):
        dt = _softplus(dp + b)
        return dt, _dot01_right(dt, e_)

    return _rowcall(fn, name=name, rows=[dt_pre], fulls=[_row2(bias), e], out_rows=[(H, f32), (inner, f32)])


def _dt_bwd(ddt, dt_pre, bias, name):
    H = ddt.shape[1]

    def fn(dd, dp, b):
        g = dd * _sigmoid(dp + b)
        return g, jnp.sum(g, axis=0, keepdims=True)

    return _rowcall(fn, name=name, rows=[ddt, dt_pre], fulls=[_row2(bias)], out_rows=[(H, f32)], out_accs=[(1, H)])


def _gnorm_fwd(y, z, w, G, name):
    inner = y.shape[1]
    gs = inner // G

    def fn(y_, z_, w_):
        gg = y_ * _silu(z_)
        outs = []
        for g in range(G):
            sl = slice(g * gs, (g + 1) * gs)
            outs.append(_rms(gg[:, sl], w_[:, sl], LN_EPS))
        return jnp.concatenate(outs, axis=1)

    return _rowcall(fn, name=name, rows=[y, z], fulls=[_row2(w)], out_rows=[(inner, bf16)], tm=256)[0]


def _gnorm_bwd(dyn, y, z, w, G, name):
    inner = y.shape[1]
    gs = inner // G

    def fn(d_, y_, z_, w_):
        sz = _silu(z_)
        gg = y_ * sz
        dgs, dws = [], []
        for g in range(G):
            sl = slice(g * gs, (g + 1) * gs)
            dg, dw = _rms_bwd(d_[:, sl], gg[:, sl], w_[:, sl], LN_EPS)
            dgs.append(dg)
            dws.append(dw)
        dgg = jnp.concatenate(dgs, axis=1)
        return dgg * sz, dgg * y_ * _dsilu(z_), jnp.concatenate(dws, axis=1)

    return _rowcall(fn, name=name, rows=[dyn, y, z], fulls=[_row2(w)], out_rows=[(inner, f32), (inner, f32)],
                    out_accs=[(1, inner)], tm=256)


def _gmlp_parts(pre, lw, lb, I):
    hp = _gelu(pre)
    uu = hp[:, :I]
    vp = hp[:, I:]
    xc = vp - jnp.mean(vp, axis=-1, keepdims=True)
    rstd = lax.rsqrt(jnp.mean(xc * xc, axis=-1, keepdims=True) + LN_EPS)
    vhat = xc * rstd
    return uu, vhat, rstd, vhat * lw + lb


def _gmlp_mid_fwd(pre, b_in, ln_w, ln_b, w_s, bsx, name):
    S, two_i = pre.shape
    I = two_i // 2
    NG = w_s.shape[0]
    gd = I // NG
    L = CHUNK

    def body(pre_ref, bi_ref, lw_ref, lb_ref, ws_ref, bsx_ref, o_ref):
        uu, _, _, vv = _gmlp_parts(pre_ref[...] + bi_ref[...], lw_ref[...], lb_ref[...], I)
        r = lax.broadcasted_iota(jnp.int32, (L, L), 0)
        cidx = lax.broadcasted_iota(jnp.int32, (L, L), 1)
        tril = cidx <= r
        for g in range(NG):
            sl = slice(g * gd, (g + 1) * gd)
            wg = jnp.where(tril, ws_ref[g], 0.0).astype(bf16)
            mixed = _dot(wg, vv[:, sl].astype(bf16), NN) + bsx_ref[:, sl]
            o_ref[:, sl] = (uu[:, sl] * mixed).astype(o_ref.dtype)

    return pl.pallas_call(
        body, name=name, grid=(S // L,),
        in_specs=[_row_spec(L, two_i), _full_spec((1, two_i)), _full_spec((1, I)), _full_spec((1, I)), _full_spec(w_s.shape), _full_spec(bsx.shape)],
        out_specs=_row_spec(L, I), out_shape=jax.ShapeDtypeStruct((S, I), bf16),
        compiler_params=_cparams(("parallel",)),
    )(pre, _row2(b_in), _row2(ln_w), _row2(ln_b), w_s, bsx)


def _gmlp_mid_bwd(do, pre, b_in, ln_w, ln_b, w_s, bsx, name):
    S, two_i = pre.shape
    I = two_i // 2
    NG = w_s.shape[0]
    gd = I // NG
    L = CHUNK

    def body(do_ref, pre_ref, bi_ref, lw_ref, lb_ref, ws_ref, bsx_ref, dpre_ref, dbi_ref, dlw_ref, dlb_ref, dws_ref, dbs_ref, dvv_ref):
        ci = pl.program_id(0)

        @pl.when(ci == 0)
        def _():
            for ref in (dbi_ref, dlw_ref, dlb_ref, dws_ref, dbs_ref):
                ref[...] = jnp.zeros_like(ref)

        pre = pre_ref[...] + bi_ref[...]
        lw = lw_ref[...]
        uu, vhat, rstd, vv = _gmlp_parts(pre, lw, lb_ref[...], I)
        dov = do_ref[...]
        r = lax.broadcasted_iota(jnp.int32, (L, L), 0)
        cidx = lax.broadcasted_iota(jnp.int32, (L, L), 1)
        tril = cidx <= r
        duus = []
        for g in range(NG):
            sl = slice(g * gd, (g + 1) * gd)
            wg = jnp.where(tril, ws_ref[g], 0.0).astype(bf16)
            vg = vv[:, sl].astype(bf16)
            mixed = _dot(wg, vg, NN) + bsx_ref[:, sl]
            duus.append(dov[:, sl] * mixed)
            dmixed = dov[:, sl] * uu[:, sl]
            dbs_ref[:, sl] += dmixed
            dmb = dmixed.astype(bf16)
            dvv_ref[:, sl] = _dot(wg, dmb, TN)
            dws_ref[g] += jnp.where(tril, _dot(dmb, vg, NT), 0.0)
        duu = jnp.concatenate(duus, axis=1)
        dvv = dvv_ref[...]
        dlw_ref[...] += jnp.sum(dvv * vhat, axis=0, keepdims=True)
        dlb_ref[...] += jnp.sum(dvv, axis=0, keepdims=True)
        dvh = dvv * lw
        dvp = rstd * (dvh - jnp.mean(dvh, axis=-1, keepdims=True) - vhat * jnp.mean(dvh * vhat, axis=-1, keepdims=True))
        dpre = jnp.concatenate([duu, dvp], axis=1) * _dgelu(pre)
        dbi_ref[...] += jnp.sum(dpre, axis=0, keepdims=True)
        dpre_ref[...] = dpre.astype(dpre_ref.dtype)

    return pl.pallas_call(
        body, name=name, grid=(S // L,),
        in_specs=[_row_spec(L, I), _row_spec(L, two_i), _full_spec((1, two_i)), _full_spec((1, I)), _full_spec((1, I)),
                  _full_spec(w_s.shape), _full_spec(bsx.shape)],
        out_specs=[_row_spec(L, two_i), _full_spec((1, two_i)), _full_spec((1, I)), _full_spec((1, I)), _full_spec(w_s.shape), _full_spec((L, I))],
        out_shape=[jax.ShapeDtypeStruct((S, two_i), bf16), jax.ShapeDtypeStruct((1, two_i), f32), jax.ShapeDtypeStruct((1, I), f32),
                   jax.ShapeDtypeStruct((1, I), f32), jax.ShapeDtypeStruct(w_s.shape, f32), jax.ShapeDtypeStruct((L, I), f32)],
        scratch_shapes=[pltpu.VMEM((L, I), f32)],
        compiler_params=_cparams(("arbitrary",)),
    )(do, pre, _row2(b_in), _row2(ln_w), _row2(ln_b), w_s, bsx)


def _lane_group_sum(acc, eg, name):
    NG = eg.shape[1]
    return _rowcall(lambda a, e: _dot(a, e, NN, HI), name=name, rows=[acc], fulls=[eg], out_rows=[(NG, f32)])[0]


def _ffn_fwd_fused(h1, nf_w, wg, wu, wd, name):
    S, D = h1.shape
    nb, F4, _ = wg.shape
    tm = _tile(S, 512)

    def body(h_ref, nf_ref, wg_ref, wu_ref, wd_ref, h2_ref, u_ref, g_ref, up_ref, a_ref, acc_ref):
        k = pl.program_id(1)

        @pl.when(k == 0)
        def _():
            u_ref[...] = _rms(h_ref[...], nf_ref[...], RMS_EPS).astype(u_ref.dtype)
            acc_ref[...] = jnp.zeros_like(acc_ref)

        for r in range(FFN_SUBTILES):
            rs = pl.ds(r * (tm // FFN_SUBTILES), tm // FFN_SUBTILES)
            uv = u_ref[rs, :]
            g = _dot(uv, wg_ref[...], NT)
            up = _dot(uv, wu_ref[...], NT)
            a = (_silu(g) * up).astype(bf16)
            g_ref[rs, :] = g.astype(g_ref.dtype)
            up_ref[rs, :] = up.astype(up_ref.dtype)
            a_ref[rs, :] = a
            acc_ref[rs, :] += _dot(a, wd_ref[...], NN)

        @pl.when(k == nb - 1)
        def _():
            h2_ref[...] = h_ref[...] + acc_ref[...]

    row = pl.BlockSpec((tm, D), lambda i, k: (i, 0))
    wspec = pl.BlockSpec((None, F4, D), lambda i, k: (k, 0, 0))
    cspec = pl.BlockSpec((None, tm, F4), lambda i, k: (k, i, 0))
    chunk = jax.ShapeDtypeStruct((nb, S, F4), bf16)
    return pl.pallas_call(
        body, name=name, grid=(S // tm, nb),
        in_specs=[row, _full_spec((1, D)), wspec, wspec, pl.BlockSpec((None, F4, D), lambda i, k: (k, 0, 0))],
        out_specs=[row, row, cspec, cspec, cspec],
        out_shape=[jax.ShapeDtypeStruct((S, D), f32), jax.ShapeDtypeStruct((S, D), bf16), chunk, chunk, chunk],
        scratch_shapes=[pltpu.VMEM((tm, D), f32)],
        compiler_params=_cparams(("parallel", "arbitrary")),
    )(h1, _row2(nf_w), wg, wu, wd)


def _ffn_bwd_fused(dh, h1, nf_w, wd, wg, wu, G, U, name, after=None):
    S, D = dh.shape
    nb, F4, _ = wd.shape
    tm = _tile(S, 512)

    def body(dh_ref, h_ref, nf_ref, wd_ref, wg_ref, wu_ref, g_ref, up_ref, *rest):
        dg_ref, du_ref, dh1_ref, dnf_ref, acc_ref = rest[-5:]
        i, k = pl.program_id(0), pl.program_id(1)

        @pl.when(k == 0)
        def _():
            acc_ref[...] = jnp.zeros_like(acc_ref)

        for r in range(FFN_SUBTILES):
            rs = pl.ds(r * (tm // FFN_SUBTILES), tm // FFN_SUBTILES)
            dA = _dot(dh_ref[rs, :].astype(bf16), wd_ref[...], NT)
            g = g_ref[rs, :].astype(f32)
            dg = (dA * up_ref[rs, :].astype(f32) * _dsilu(g)).astype(bf16)
            du = (dA * _silu(g)).astype(bf16)
            dg_ref[rs, :] = dg
            du_ref[rs, :] = du
            acc_ref[rs, :] += _dot(dg, wg_ref[...], NN) + _dot(du, wu_ref[...], NN)

        @pl.when(k == nb - 1)
        def _():
            dx, dw = _rms_bwd(acc_ref[...], h_ref[...], nf_ref[...], RMS_EPS)
            dh1_ref[...] = dh_ref[...] + dx

            @pl.when(i == 0)
            def _():
                dnf_ref[...] = dw

            @pl.when(i > 0)
            def _():
                dnf_ref[...] += dw

    row = pl.BlockSpec((tm, D), lambda i, k: (i, 0))
    wspec = pl.BlockSpec((None, F4, D), lambda i, k: (k, 0, 0))
    cspec = pl.BlockSpec((None, tm, F4), lambda i, k: (k, i, 0))
    chunk = jax.ShapeDtypeStruct((nb, S, F4), bf16)
    return pl.pallas_call(
        body, name=name, grid=(S // tm, nb),
        in_specs=[row, row, _full_spec((1, D)), wspec, wspec, wspec, cspec, cspec] + ([] if after is None else [pl.BlockSpec(memory_space=pl.ANY)]),
        out_specs=[cspec, cspec, row, _full_spec((1, D))],
        out_shape=[chunk, chunk, jax.ShapeDtypeStruct((S, D), f32), jax.ShapeDtypeStruct((1, D), f32)],
        scratch_shapes=[pltpu.VMEM((tm, D), f32)],
        compiler_params=_cparams(("arbitrary", "arbitrary")),
    )(dh, h1, _row2(nf_w), wd, wg, wu, G, U, *([] if after is None else [after]))


def _rms_bwd_add(dres, du, h, w, name):
    D = h.shape[1]

    def fn(dr, du_, h_, w_):
        dx, dw = _rms_bwd(du_, h_, w_, RMS_EPS)
        return dr + dx, dw

    return _rowcall(fn, name=name, rows=[dres, du, h], fulls=[_row2(w)], out_rows=[(D, f32)], out_accs=[(1, D)])


def _ple_fwd(h, p_i, wp, pn, gn, wgate, name):
    D = h.shape[1]

    def fn(h_, p_, wp_, pn_, gn_, wg_):
        pe = _dot(p_.astype(bf16), wp_, NN)
        e = _rms(pe, pn_, RMS_EPS)
        q = _rms(h_, gn_, RMS_EPS)
        gate = _sigmoid(_dot(q.astype(bf16), wg_, NN))
        return h_ + gate * e, pe, gate

    return _rowcall(fn, name=name, rows=[h, p_i], fulls=[wp, _row2(pn), _row2(gn), wgate],
                    out_rows=[(D, f32), (D, f32), (D, f32)], tm=256)


def _ple_bwd(dh3, h, pe, gate, pn, gn, wgate, name, after=None):
    D = h.shape[1]

    def fn(d_, h_, pe_, gate_, pn_, gn_, wg_, *_):
        e = _rms(pe_, pn_, RMS_EPS)
        dzg = d_ * e * gate_ * (1.0 - gate_)
        dq = _dot(dzg.astype(bf16), wg_, NT)
        dxq, dgn = _rms_bwd(dq, h_, gn_, RMS_EPS)
        dpe, dpn = _rms_bwd(d_ * gate_, pe_, pn_, RMS_EPS)
        return d_ + dxq, dzg, dpe, _rms(h_, gn_, RMS_EPS), dpn, dgn

    return _rowcall(fn, name=name, rows=[dh3, h, pe, gate], fulls=[_row2(pn), _row2(gn), wgate] + ([] if after is None else [after]),
                    out_rows=[(D, f32), (D, bf16), (D, bf16), (D, bf16)], out_accs=[(1, D), (1, D)], tm=256)


def _loss_head(h, target, fn_w, name):
    D = h.shape[1]

    def fn(h_, t_, w_):
        diff = _rms(h_, w_, RMS_EPS) - t_
        loss = 0.5 * jnp.sum(jnp.mean(diff * diff, axis=-1, keepdims=True), axis=0, keepdims=True)
        dh, dw = _rms_bwd(diff * (1.0 / D), h_, w_, RMS_EPS)
        return dh, jnp.broadcast_to(loss, (1, LANES)), dw

    return _rowcall(fn, name=name, rows=[h, target], fulls=[_row2(fn_w)], out_rows=[(D, f32)], out_accs=[(1, LANES), (1, D)])


def _adamw(w, m, v, g, name):
    R, C = w.shape
    tr, tc = R, C
    while tr * tc > 256 * 1024 and tr % (2 * SUBLANES) == 0:
        tr //= 2
    while tr * tc > 256 * 1024 and tc % (2 * LANES) == 0:
        tc //= 2

    def body(w_ref, m_ref, v_ref, g_ref, d_ref, mo_ref, vo_ref):
        g = g_ref[...]
        mn = ADAM_B1 * m_ref[...] + (1.0 - ADAM_B1) * g
        vn = ADAM_B2 * v_ref[...] + (1.0 - ADAM_B2) * (g * g)
        m_hat = mn / (1.0 - ADAM_B1 ** ADAM_STEP)
        v_hat = vn / (1.0 - ADAM_B2 ** ADAM_STEP)
        d_ref[...] = -ADAM_LR * (m_hat / (jnp.sqrt(v_hat) + ADAM_EPS) + ADAM_WD * w_ref[...])
        mo_ref[...] = mn
        vo_ref[...] = vn

    spec = pl.BlockSpec((tr, tc), lambda i, j: (i, j))
    return pl.pallas_call(
        body, name=name, grid=(R // tr, C // tc), in_specs=[spec] * 4,
        out_specs=[spec] * 3, out_shape=[jax.ShapeDtypeStruct((R, C), f32)] * 3,
        compiler_params=_cparams(("parallel", "parallel")),
    )(w, m, v, g)


def _expand_onehot(n, per):
    lane = lax.broadcasted_iota(jnp.int32, (n, n * per), 1)
    row = lax.broadcasted_iota(jnp.int32, (n, n * per), 0)
    return (lane // per == row).astype(f32)


def _ssd_layer_fwd(h, nm_w, W, t):
    H = W["dt_bias"].shape[0]
    inner = H * HEADDIM
    G = (W["conv_b"].shape[0] - inner) // (2 * STATE)
    hn = _rms_fwd(h, nm_w, f"rms_mix_{t}")
    conv_dim = W["conv_b"].shape[0]
    wT = W["w_inT"]
    z = _mm(hn, wT, mode="nt", brows=(0, inner), name=f"ssd_z_{t}")
    xpre = _mm(hn, wT, mode="nt", brows=(inner, conv_dim), name=f"ssd_xbc_{t}")
    dt_pre = _mm(hn, wT, mode="nt", brows=(inner + conv_dim, H), name=f"ssd_dt_{t}")
    c, xbc = _conv_fwd(xpre, W["conv_w"], W["conv_b"], f"ssd_conv_{t}")
    _, dtx = _dt_fwd(dt_pre, W["dt_bias"], _expand_onehot(H, HEADDIM), f"ssd_dtx_{t}")
    a = -jnp.exp(W["a_log"])
    ax = _row2(jnp.repeat(a, HEADDIM))
    dx = _row2(jnp.repeat(W["d"], HEADDIM))
    y, states = _ssd_fwd(xbc, dtx, ax, dx, G, f"ssd_scan_{t}")
    yn = _gnorm_fwd(y, z, W["norm_w"], G, f"ssd_gnorm_{t}")
    h1 = _mm(yn, W["wout"], res=h, name=f"ssd_out_{t}")
    return h1, (h, hn, z, xpre, dt_pre, c, xbc, dtx, a, ax, dx, y, states, yn)


def _ssd_layer_bwd(dh1, saved, nm_w, W, t, after=None):
    h, hn, z, xpre, dt_pre, c, xbc, dtx, a, ax, dx, y, states, yn = saved
    H = W["dt_bias"].shape[0]
    inner = H * HEADDIM
    G = (W["conv_b"].shape[0] - inner) // (2 * STATE)
    dyn = _mm(dh1, W["wout"], mode="nt", after=after, name=f"ssd_out_dx_{t}")
    g_wout = _mm(yn, dh1, mode="tn", out_dtype=bf16, name=f"ssd_out_dw_{t}")
    dy, dz, g_normw = _gnorm_bwd(dyn, y, z, W["norm_w"], G, f"ssd_gnorm_bwd_{t}")
    dxbc, ddt, dax, ddx = _ssd_bwd(dy, y, xbc, dtx, ax, dx, states, _expand_onehot(H, HEADDIM).T, G, f"ssd_scan_bwd_{t}")
    dc, g_convw8, g_convb = _conv_bwd_dc(dxbc, c, xpre, f"ssd_conv_bwd_dc_{t}")
    dxpre = _conv_bwd_dx(dc, W["conv_w"], f"ssd_conv_bwd_dx_{t}")
    ddt_pre, g_dtb = _dt_bwd(ddt, dt_pre, W["dt_bias"], f"ssd_dt_bwd_{t}")
    conv_dim = W["conv_b"].shape[0]
    wT = W["w_inT"]
    g_wz = _mm(dz, hn, mode="tn", out_dtype=bf16, name=f"ssd_z_dw_{t}")
    g_wxbc = _mm(dxpre, hn, mode="tn", out_dtype=bf16, name=f"ssd_xbc_dw_{t}")
    g_wdt = _mm(ddt_pre, hn, mode="tn", out_dtype=bf16, name=f"ssd_dt_dw_{t}")
    dhn = _mm(dz, wT, brows=(0, inner), name=f"ssd_z_dx_{t}")
    dhn = _mm(dxpre, wT, brows=(inner, conv_dim), res=dhn, name=f"ssd_xbc_dx_{t}")
    dhn = _mm(ddt_pre, wT, brows=(inner + conv_dim, H), res=dhn, name=f"ssd_dt_dx_{t}")
    dh, g_nm = _rms_bwd_add(dh1, dhn, h, nm_w, f"rms_mix_bwd_{t}")
    grads = dict(
        w_inT=jnp.concatenate([g_wz, g_wxbc, g_wdt], axis=0), wout=g_wout,
        conv_w=g_convw8[:CONV_K], conv_b=g_convb[0], dt_bias=g_dtb[0],
        a_log=dax[0].reshape(H, HEADDIM)[:, 0] * a, d=jnp.sum(ddx[0].reshape(H, HEADDIM), axis=1),
        norm_w=g_normw[0], norm_mix=g_nm[0])
    return dh, grads


def _gmlp_layer_fwd(h, nm_w, W, t):
    NG, L, _ = W["w_s"].shape
    I = W["ln_w"].shape[0]
    hn = _rms_fwd(h, nm_w, f"rms_mix_{t}")
    pre = _mm(hn, W["win"], name=f"gmlp_in_{t}")
    bsx = jnp.repeat(W["b_s"].T, I // NG, axis=1)
    o = _gmlp_mid_fwd(pre, W["b_in"], W["ln_w"], W["ln_b"], W["w_s"], bsx, f"gmlp_mid_{t}")
    h1 = _mm(o, W["wout"], res=h, name=f"gmlp_out_{t}")
    return h1, (h, hn, pre, bsx, o)


def _gmlp_layer_bwd(dh1, saved, nm_w, W, t, after=None):
    h, hn, pre, bsx, o = saved
    NG = W["w_s"].shape[0]
    I = W["ln_w"].shape[0]
    do = _mm(dh1, W["wout"], mode="nt", after=after, name=f"gmlp_out_dx_{t}")
    g_wout = _mm(o, dh1, mode="tn", out_dtype=bf16, name=f"gmlp_out_dw_{t}")
    dpre, g_bin, g_lnw, g_lnb, g_ws, dbs = _gmlp_mid_bwd(do, pre, W["b_in"], W["ln_w"], W["ln_b"], W["w_s"], bsx, f"gmlp_mid_bwd_{t}")
    g_bs = _lane_group_sum(dbs, _expand_onehot(NG, I // NG).T, f"gmlp_bs_{t}").T
    g_win = _mm(hn, dpre, mode="tn", out_dtype=bf16, name=f"gmlp_in_dw_{t}")
    dhn = _mm(dpre, W["win"], mode="nt", name=f"gmlp_in_dx_{t}")
    dh, g_nm = _rms_bwd_add(dh1, dhn, h, nm_w, f"rms_mix_bwd_{t}")
    grads = dict(win=g_win, wout=g_wout, b_in=g_bin[0], ln_w=g_lnw[0], ln_b=g_lnb[0], w_s=g_ws, b_s=g_bs, norm_mix=g_nm[0])
    return dh, grads


def _ffn_fwd(h1, nf_w, W, t):
    h2, u, Gm, Um, A = _ffn_fwd_fused(h1, nf_w, W["wg"], W["wu"], W["wd"], f"ffn_fwd_{t}")
    return h2, (h1, u, Gm, Um, A)


def _ffn_bwd(dh2, saved, nf_w, W, t, after=None):
    h1, u, Gm, Um, A = saved
    dG, dU, dh1, g_nf = _ffn_bwd_fused(dh2, h1, nf_w, W["wd"], W["wg"], W["wu"], Gm, Um, f"ffn_bwd_{t}", after=after)
    g_wd = _mm(A, dh2, mode="tn", out_dtype=bf16, name=f"ffn_down_dw_{t}")
    g_wg = _mm(dG, u, mode="tn", out_dtype=bf16, name=f"ffn_gate_dw_{t}")
    g_wu = _mm(dU, u, mode="tn", out_dtype=bf16, name=f"ffn_up_dw_{t}")
    return dh1, dict(wg=g_wg, wu=g_wu, wd=g_wd, norm_ffn=g_nf[0])


def _local_step(x, p, target, norms, layer_weights, on_layer_grads=None, final_norm_grad=None):
    depth = p.shape[0]
    h = x
    saved = []
    for i in range(depth):
        Wm = layer_weights(i, "mix", h)
        if i % 2 == 0:
            h1, s_mix = _ssd_layer_fwd(h, norms["norm_mix"][i], Wm, i)
        else:
            h1, s_mix = _gmlp_layer_fwd(h, norms["norm_mix"][i], Wm, i)
        Wf = layer_weights(i, "ffn", h1)
        h2, s_ffn = _ffn_fwd(h1, norms["norm_ffn"][i], Wf, i)
        P = layer_weights(i, "ple", h2)
        h3, pe, gate = _ple_fwd(h2, p[i], P["wp"], P["pn"], P["gn"], P["wgate"], f"ple_{i}")
        saved.append((Wm, Wf, P, s_mix, s_ffn, (h2, pe, gate)))
        h = h3
    dh, loss, g_fn = _loss_head(h, target, norms["final_norm"], "loss_head")
    if final_norm_grad is not None:
        final_norm_grad[0] = g_fn[0]
    grads = [None] * depth
    tell = on_layer_grads if on_layer_grads is not None else (lambda i, part, g: None)
    after = None
    for i in reversed(range(depth)):
        Wm, Wf, P, s_mix, s_ffn, (h2, pe, gate) = saved[i]
        dh, dzg, dpe, q, g_pn, g_gn = _ple_bwd(dh, h2, pe, gate, P["pn"], P["gn"], P["wgate"], f"ple_bwd_{i}", after=after)
        g_ple = dict(wgate=_mm(q, dzg, mode="tn", out_dtype=bf16, name=f"ple_gate_dw_{i}"),
                     wp=_mm(p[i], dpe, mode="tn", out_dtype=bf16, name=f"ple_proj_dw_{i}"), pn=g_pn[0], gn=g_gn[0])
        after = tell(i, "ple", g_ple)
        dh, g_ffn = _ffn_bwd(dh, s_ffn, norms["norm_ffn"][i], Wf, i, after=after)
        after = tell(i, "ffn", g_ffn)
        if i % 2 == 0:
            dh, g_mix = _ssd_layer_bwd(dh, s_mix, norms["norm_mix"][i], Wm, i, after=after)
        else:
            dh, g_mix = _gmlp_layer_bwd(dh, s_mix, norms["norm_mix"][i], Wm, i, after=after)
        after = tell(i, "mix", g_mix)
        grads[i] = dict(mix=g_mix, ffn=g_ffn, ple=g_ple)
    return loss[0, 0], dh, g_fn[0], grads


def _flip(v, f):
    return 1 - v if f else v


_ANY = pl.BlockSpec(memory_space=pl.ANY)


_SEM = pl.BlockSpec(memory_space=pltpu.SEMAPHORE)
_DATAFLOW = pltpu.SideEffectType.DATAFLOW_SIDE_EFFECTING
_CHIP_FLIPS = ((1, 0), (0, 1), (1, 1))
_DMA = pltpu.SemaphoreType.DMA


def _structs(arrs):
    return [jax.ShapeDtypeStruct(a.shape, a.dtype) for a in arrs]


def _gather_copy(src, buf, send_sems, recv_sems, k, j, slot, x, y, c):
    c2 = src.shape[1] // 2
    fx, fy = _CHIP_FLIPS[j]
    nf = len(_CHIP_FLIPS)
    return pltpu.make_async_remote_copy(
        src_ref=src.at[:, pl.ds(c * c2, c2)], dst_ref=buf.at[slot, :, pl.ds(c * c2, c2)], send_sem=send_sems.at[nf * k + j],
        recv_sem=recv_sems.at[nf * k + j], device_id=(_flip(x, fx), _flip(y, fy), c), device_id_type=MESH)


def _gather_start(srcs, groups):
    n = len(srcs)
    ng = len(groups)
    nf = len(_CHIP_FLIPS)

    def body(*refs):
        src_refs, buf_refs, sems = refs[:n], refs[n:2 * n], refs[4 * n:]
        x, y, c = lax.axis_index("x"), lax.axis_index("y"), lax.axis_index("c")
        for gi, group in enumerate(groups):
            for k, o in enumerate(group):
                for j in range(nf):
                    _gather_copy(src_refs[o], buf_refs[o], sems[2 * gi], sems[2 * gi + 1], k, j, 2 * x + y, x, y, c).start()

    mychip = 2 * lax.axis_index("x") + lax.axis_index("y")
    inits = [lax.dynamic_update_slice(lax.empty((N_CHIPS,) + s.shape, s.dtype), s[None], (mychip, 0, 0)) for s in srcs]
    sem_shapes = [_DMA((nf * len(g),)) for g in groups for _ in range(2)]
    outs = pl.pallas_call(
        body, name="gather_start", in_specs=[_ANY] * (2 * n), out_specs=[_ANY] * (2 * n) + [_SEM] * (2 * ng),
        out_shape=_structs(srcs) + _structs(inits) + sem_shapes, input_output_aliases={i: i for i in range(2 * n)},
        compiler_params=pltpu.CompilerParams(has_side_effects=_DATAFLOW),
    )(*srcs, *inits)
    return outs[:n], outs[n:2 * n], [(outs[2 * n + 2 * gi], outs[2 * n + 2 * gi + 1]) for gi in range(ng)]


def _gather_wait(srcs, bufs, sems, after, name):
    n = len(srcs)
    nf = len(_CHIP_FLIPS)

    def body(*refs):
        src_refs, buf_refs, send_sems, recv_sems = refs[:n], refs[n:2 * n], refs[2 * n], refs[2 * n + 1]
        x, y, c = lax.axis_index("x"), lax.axis_index("y"), lax.axis_index("c")
        for k in range(n):
            for j, (fx, fy) in enumerate(_CHIP_FLIPS):
                cp = _gather_copy(src_refs[k], buf_refs[k], send_sems, recv_sems, k, j, 2 * _flip(x, fx) + _flip(y, fy), x, y, c)
                cp.wait_send()
                cp.wait_recv()

    outs = pl.pallas_call(
        body, name=name, in_specs=[_ANY] * (2 * n) + [_SEM, _SEM, _ANY], out_specs=[_ANY] * (2 * n),
        out_shape=_structs(srcs) + _structs(bufs), input_output_aliases={i: i for i in range(2 * n)},
        compiler_params=pltpu.CompilerParams(has_side_effects=_DATAFLOW),
    )(*srcs, *bufs, *sems, after)
    return outs[n:]


def _gather_forward(bufs, name):
    n = len(bufs)
    nf = len(_CHIP_FLIPS)

    def body(*refs):
        outs = refs[n:2 * n]
        send_sems, recv_sems = refs[2 * n:]
        x, y, c = lax.axis_index("x"), lax.axis_index("y"), lax.axis_index("c")

        def forward(k, j, h):
            c2 = bufs[k].shape[2] // 2
            fx, fy = _CHIP_FLIPS[j]
            part = outs[k].at[2 * _flip(x, fx) + _flip(y, fy), :, pl.ds(h * c2, c2)]
            return pltpu.make_async_remote_copy(src_ref=part, dst_ref=part, send_sem=send_sems.at[nf * k + j],
                                                recv_sem=recv_sems.at[nf * k + j], device_id=(x, y, 1 - c), device_id_type=MESH)

        sends = [forward(k, j, c) for k in range(n) for j in range(nf)]
        for cp in sends:
            cp.start()
        for k in range(n):
            for j in range(nf):
                forward(k, j, 1 - c).wait_recv()
        for cp in sends:
            cp.wait_send()

    return pl.pallas_call(
        body, name=name, in_specs=[_ANY] * n, out_specs=[_ANY] * n, out_shape=_structs(bufs),
        input_output_aliases={i: i for i in range(n)}, scratch_shapes=[_DMA((nf * n,)), _DMA((nf * n,))],
    )(*bufs)


def _half_struct(a, lead):
    return jax.ShapeDtypeStruct(lead + (a.shape[-2], a.shape[-1] // 2), a.dtype)


_DEVICE_FLIPS = tuple((f >> 2 & 1, f >> 1 & 1, f & 1) for f in range(1, N_DEV))


def _exchange_copy(srcs, lands, n, send_sems, recv_sems, i, j, slot, x, y, c):
    nf = len(_DEVICE_FLIPS)
    px, py, pc = (_flip(v, f) for v, f in zip((x, y, c), _DEVICE_FLIPS[j]))
    src = srcs[i]
    if i < n:
        c2 = src.shape[2] // 2
        src = src.at[2 * px + py, :, pl.ds(pc * c2, c2)]
    return pltpu.make_async_remote_copy(src_ref=src, dst_ref=lands[i].at[slot], send_sem=send_sems.at[nf * i + j],
                                        recv_sem=recv_sems.at[nf * i + j], device_id=(px, py, pc), device_id_type=MESH)


def _exchange_start(tensors, wholes, name):
    n, m = len(tensors), len(wholes)
    nf = len(_DEVICE_FLIPS)
    t = n + m
    land_structs = ([_half_struct(a, (N_DEV,)) for a in tensors] + [jax.ShapeDtypeStruct((N_DEV,) + w.shape, w.dtype) for w in wholes])

    def body(*refs):
        srcs, lands, send_sems, recv_sems, token = refs[:t], refs[2 * t:3 * t], refs[3 * t], refs[3 * t + 1], refs[3 * t + 2]
        x, y, c = lax.axis_index("x"), lax.axis_index("y"), lax.axis_index("c")
        for i in range(t):
            for j in range(nf):
                _exchange_copy(srcs, lands, n, send_sems, recv_sems, i, j, 4 * x + 2 * y + c, x, y, c).start()
        token[...] = jnp.zeros_like(token)

    outs = pl.pallas_call(
        body, name=name, in_specs=[_ANY] * t,
        out_specs=[_ANY] * (2 * t) + [_SEM, _SEM, pl.BlockSpec(memory_space=pltpu.VMEM)],
        out_shape=_structs(tensors) + _structs(wholes) + land_structs + [_DMA((nf * t,)), _DMA((nf * t,)),
                                                                          jax.ShapeDtypeStruct((SUBLANES, LANES), f32)],
        input_output_aliases={i: i for i in range(t)},
        compiler_params=pltpu.CompilerParams(has_side_effects=_DATAFLOW),
    )(*tensors, *wholes)
    return outs[:t], outs[t:2 * t], (outs[2 * t], outs[2 * t + 1]), outs[2 * t + 2]


def _exchange_wait(srcs, lands, n, sems, after, name):
    t = len(srcs)

    def body(*refs):
        src_refs, land_refs, send_sems, recv_sems = refs[:t], refs[t:2 * t], refs[2 * t], refs[2 * t + 1]
        x, y, c = lax.axis_index("x"), lax.axis_index("y"), lax.axis_index("c")
        for i in range(t):
            for j, (fx, fy, fc) in enumerate(_DEVICE_FLIPS):
                sender = 4 * _flip(x, fx) + 2 * _flip(y, fy) + _flip(c, fc)
                cp = _exchange_copy(src_refs, land_refs, n, send_sems, recv_sems, i, j, sender, x, y, c)
                cp.wait_send()
                cp.wait_recv()

    outs = pl.pallas_call(
        body, name=name, in_specs=[_ANY] * (2 * t) + [_SEM, _SEM, _ANY], out_specs=[_ANY] * (2 * t),
        out_shape=_structs(srcs) + _structs(lands), input_output_aliases={i: i for i in range(2 * t)},
        compiler_params=pltpu.CompilerParams(has_side_effects=_DATAFLOW),
    )(*srcs, *lands, *sems, after)
    return outs[:t], outs[t:]


def _sibling_join(bufs, name):
    flat = [(gi, l) for gi, b in enumerate(bufs) for l in range(b.shape[0])]
    n, n_buf = len(flat), len(bufs)

    def body(*refs):
        outs = refs[n_buf:2 * n_buf]
        send_sems, recv_sems = refs[2 * n_buf:]
        x, y, c = lax.axis_index("x"), lax.axis_index("y"), lax.axis_index("c")

        def push(i, h):
            gi, l = flat[i]
            c2 = bufs[gi].shape[2] // 2
            part = outs[gi].at[l, :, pl.ds(h * c2, c2)]
            return pltpu.make_async_remote_copy(src_ref=part, dst_ref=part, send_sem=send_sems.at[i], recv_sem=recv_sems.at[i],
                                                device_id=(x, y, 1 - c), device_id_type=MESH)

        sends = [push(i, c) for i in range(n)]
        for cp in sends:
            cp.start()
        for i in range(n):
            push(i, 1 - c).wait_recv()
        for cp in sends:
            cp.wait_send()

    dma = pltpu.SemaphoreType.DMA
    return pl.pallas_call(
        body, name=name, in_specs=[_ANY] * n_buf, out_specs=[_ANY] * n_buf,
        out_shape=[jax.ShapeDtypeStruct(b.shape, b.dtype) for b in bufs],
        input_output_aliases={i: i for i in range(n_buf)},
        scratch_shapes=[dma((n,)), dma((n,))],
    )(*bufs)


def _device_sum(landed, own, place, name, into=None, layer=0, layers=1):
    ndev, R, C2 = landed.shape
    tr, tc = R, C2
    while ndev * tr * tc > 1024 * 1024 and tr % (4 * SUBLANES) == 0:
        tr //= 2
    while ndev * tr * tc > 1024 * 1024 and tc % (2 * LANES) == 0:
        tc //= 2
    ncb = C2 // tc

    def body(*refs):
        place_ref, l_ref, m_ref, o_ref = refs[0], refs[1], refs[2], refs[-1]
        me = 2 * place_ref[0] + place_ref[1]
        s = jnp.where(me == 0, m_ref[...].astype(f32), l_ref[0].astype(f32))
        for d in range(1, ndev):
            s = s + jnp.where(me == d, m_ref[...].astype(f32), l_ref[d].astype(f32))
        o_ref[...] = s

    in_specs = [pl.BlockSpec((ndev, tr, tc), lambda i, j, pr: (0, i, j)),
                pl.BlockSpec((None, tr, tc), lambda i, j, pr: (pr[0], i, pr[1] * ncb + j))]
    args = [place, landed, own]
    if into is not None:
        in_specs.append(_ANY)
        args.append(into)
    return pl.pallas_call(
        body, name=name, out_shape=jax.ShapeDtypeStruct((layers, R, 2 * C2), f32),
        grid_spec=pltpu.PrefetchScalarGridSpec(
            num_scalar_prefetch=1, grid=(R // tr, ncb), in_specs=in_specs,
            out_specs=pl.BlockSpec((None, tr, tc), lambda i, j, pr: (layer, i, pr[1] * ncb + j))),
        input_output_aliases={3: 0} if into is not None else {},
        compiler_params=_cparams(("parallel", "parallel")),
    )(*args)


def _device_sum_whole(landed, own, place, name):
    ndev, R, C = landed.shape
    tr = R
    while ndev * tr * C > 1024 * 1024 and tr % (2 * SUBLANES) == 0:
        tr //= 2

    def body(place_ref, l_ref, m_ref, o_ref):
        me = 2 * place_ref[0] + place_ref[1]
        s = jnp.where(me == 0, m_ref[...], l_ref[0])
        for d in range(1, ndev):
            s = s + jnp.where(me == d, m_ref[...], l_ref[d])
        o_ref[...] = s

    return pl.pallas_call(
        body, name=name, out_shape=jax.ShapeDtypeStruct((R, C), f32),
        grid_spec=pltpu.PrefetchScalarGridSpec(
            num_scalar_prefetch=1, grid=(R // tr,),
            in_specs=[pl.BlockSpec((ndev, tr, C), lambda i, pr: (0, i, 0)), pl.BlockSpec((tr, C), lambda i, pr: (i, 0))],
            out_specs=pl.BlockSpec((tr, C), lambda i, pr: (i, 0))),
        compiler_params=_cparams(("parallel",)),
    )(place, landed, own)


PACK_COLS = 1024
PACK_ROW_MULTIPLE = 64

BIG = ("ssd_w_in", "ssd_w_out", "gmlp_w_in", "gmlp_w_out", "ffn_w_gate", "ffn_w_up", "ffn_w_down", "ple_w_proj", "ple_w_gate")
SMALL_SHARDED = ("ssd_conv_w", "gmlp_b_in", "gmlp_ln_w", "gmlp_ln_b")
REP_EARLY = ("gmlp_w_s", "gmlp_b_s")
REP_LATE = ("norm_mix", "norm_ffn", "ssd_conv_b", "ssd_dt_bias", "ssd_a_log", "ssd_d", "ssd_norm_w", "ple_norm", "ple_gate_norm",
            "final_norm")
WEIGHTS = ("norm_mix", "norm_ffn", "ssd_w_in", "ssd_conv_w", "ssd_conv_b", "ssd_dt_bias", "ssd_a_log", "ssd_d", "ssd_norm_w", "ssd_w_out",
           "gmlp_w_in", "gmlp_b_in", "gmlp_ln_w", "gmlp_ln_b", "gmlp_w_s", "gmlp_b_s", "gmlp_w_out", "ffn_w_gate", "ffn_w_up",
           "ffn_w_down", "ple_w_proj", "ple_norm", "ple_gate_norm", "ple_w_gate", "final_norm")
TRANSPOSED = ("ssd_w_in", "ffn_w_gate", "ffn_w_up")


def _pack(arrs):
    flat = jnp.concatenate([a.reshape(-1).astype(f32) for a in arrs])
    per = PACK_COLS * PACK_ROW_MULTIPLE
    n = -(-flat.shape[0] // per) * per
    return jnp.pad(flat, (0, n - flat.shape[0])).reshape(-1, PACK_COLS)


def _unpack(buf, shapes):
    flat = buf.reshape(-1)
    out, o = [], 0
    for s in shapes:
        n = math.prod(s)
        out.append(flat[o:o + n].reshape(s))
        o += n
    return out


def _chip_major(g):
    r, c4 = g.shape
    return g.reshape(r, N_CHIPS, c4 // N_CHIPS).transpose(1, 0, 2)


def _from_chip_major(g):
    k, r, c = g.shape
    return g.transpose(1, 0, 2).reshape(r, k * c)


def _adamw_nd(w, m, v, g, name):
    shp = w.shape
    two = lambda a: a.reshape(-1, shp[-1])
    return [o.reshape(shp) for o in _adamw(two(w), two(m), two(v), two(g), name)]


def kernel(x, p, norm_mix, norm_ffn, ssd_w_in, ssd_conv_w, ssd_conv_b, ssd_dt_bias, ssd_a_log, ssd_d, ssd_norm_w, ssd_w_out, gmlp_w_in, gmlp_b_in, gmlp_ln_w, gmlp_ln_b, gmlp_w_s, gmlp_b_s, gmlp_w_out, ffn_w_gate, ffn_w_up, ffn_w_down, ple_w_proj, ple_norm, ple_gate_norm, ple_w_gate, final_norm, loss_target, m_norm_mix, m_norm_ffn, m_ssd_w_in, m_ssd_conv_w, m_ssd_conv_b, m_ssd_dt_bias, m_ssd_a_log, m_ssd_d, m_ssd_norm_w, m_ssd_w_out, m_gmlp_w_in, m_gmlp_b_in, m_gmlp_ln_w, m_gmlp_ln_b, m_gmlp_w_s, m_gmlp_b_s, m_gmlp_w_out, m_ffn_w_gate, m_ffn_w_up, m_ffn_w_down, m_ple_w_proj, m_ple_norm, m_ple_gate_norm, m_ple_w_gate, m_final_norm, v_norm_mix, v_norm_ffn, v_ssd_w_in, v_ssd_conv_w, v_ssd_conv_b, v_ssd_dt_bias, v_ssd_a_log, v_ssd_d, v_ssd_norm_w, v_ssd_w_out, v_gmlp_w_in, v_gmlp_b_in, v_gmlp_ln_w, v_gmlp_ln_b, v_gmlp_w_s, v_gmlp_b_s, v_gmlp_w_out, v_ffn_w_gate, v_ffn_w_up, v_ffn_w_down, v_ple_w_proj, v_ple_norm, v_ple_gate_norm, v_ple_w_gate, v_final_norm):
    given = dict(locals())
    view = lambda n, a: jnp.swapaxes(a, 1, 2) if n in TRANSPOSED else a
    w = {n: view(n, given[n]) for n in WEIGHTS}
    mom = {n: view(n, given["m_" + n]) for n in WEIGHTS}
    var = {n: view(n, given["v_" + n]) for n in WEIGHTS}
    depth = p.shape[0]
    n_ssd, n_gmlp = ssd_w_in.shape[0], gmlp_w_in.shape[0]
    inner = ssd_dt_bias.shape[1] * HEADDIM
    conv_dim = ssd_conv_b.shape[1]

    place = jnp.stack([2 * lax.axis_index("x") + lax.axis_index("y"), lax.axis_index("c")]).astype(jnp.int32)

    def part_keys(i, part):
        j = i // 2
        if part == "mix":
            names = (("ssd_w_in", j), ("ssd_w_out", j)) if i % 2 == 0 else (("gmlp_w_in", j), ("gmlp_w_out", j))
            return ((("small", 0),) if i == 0 else ()) + names
        if part == "ffn":
            return (("ffn_w_gate", i), ("ffn_w_up", i), ("ffn_w_down", i))
        return (("ple_w_proj", i), ("ple_w_gate", i))

    parts = [(i, part) for i in range(depth) for part in ("mix", "ffn", "ple")]
    keys, groups = [], {}
    for ip in parts:
        names = part_keys(*ip)
        groups[ip] = list(range(len(keys), len(keys) + len(names)))
        keys += names
    small_shapes = [w[n].shape for n in SMALL_SHARDED]
    srcs = [_pack([w[n] for n in SMALL_SHARDED]) if n == "small" else w[n][l].astype(bf16) for n, l in keys]
    srcs, landing, gather_sems = _gather_start(srcs, [groups[ip] for ip in parts])
    gather_sems = dict(zip(parts, gather_sems))
    small_full = {}

    def layer_weights(i, part, h):
        idx = groups[(i, part)]
        got = _gather_wait([srcs[o] for o in idx], [landing[o] for o in idx], gather_sems[(i, part)], h, f"gather_wait_{part}_{i}")
        gw = dict(zip([keys[o] for o in idx], _gather_forward(got, f"gather_forward_{part}_{i}")))
        rows = lambda a: a.reshape(-1, a.shape[-1])
        j = i // 2
        if part == "ffn":
            return dict(wg=gw[("ffn_w_gate", i)], wu=gw[("ffn_w_up", i)], wd=gw[("ffn_w_down", i)])
        if part == "ple":
            return dict(wp=_from_chip_major(gw[("ple_w_proj", i)]), pn=ple_norm[i], gn=ple_gate_norm[i], wgate=rows(gw[("ple_w_gate", i)]))
        if i == 0:
            by_chip = [_unpack(gw[("small", 0)][k], small_shapes) for k in range(N_CHIPS)]
            small_full.update({n: jnp.concatenate([by_chip[k][t] for k in range(N_CHIPS)], axis=-1) for t, n in enumerate(SMALL_SHARDED)})
        if i % 2 == 0:
            return dict(w_inT=rows(gw[("ssd_w_in", j)]),
                        conv_w=small_full["ssd_conv_w"][j], conv_b=ssd_conv_b[j], dt_bias=ssd_dt_bias[j], a_log=ssd_a_log[j],
                        d=ssd_d[j], norm_w=ssd_norm_w[j], wout=rows(gw[("ssd_w_out", j)]))
        return dict(win=_from_chip_major(gw[("gmlp_w_in", j)]), b_in=small_full["gmlp_b_in"][j], ln_w=small_full["gmlp_ln_w"][j],
                    ln_b=small_full["gmlp_ln_b"][j], w_s=gmlp_w_s[j], b_s=gmlp_b_s[j], wout=rows(gw[("gmlp_w_out", j)]))

    rows4 = lambda a: a.reshape((N_CHIPS, a.shape[0] // N_CHIPS) + a.shape[1:])
    cut = lambda a, k: a[..., k * (a.shape[-1] // N_CHIPS):(k + 1) * (a.shape[-1] // N_CHIPS)]
    layer_grads = {}
    in_flight = {}
    tokens = {}
    owns = {}

    def on_layer_grads(i, part, g):
        layer_grads[(i, part)] = g
        j = i // 2
        wholes = {}
        if part == "ffn":
            chunks = {("ffn_w_gate", i): g["wg"], ("ffn_w_up", i): g["wu"], ("ffn_w_down", i): g["wd"]}
        elif part == "ple":
            chunks = {("ple_w_proj", i): _chip_major(g["wp"]), ("ple_w_gate", i): rows4(g["wgate"])}
        elif i % 2 == 0:
            chunks = {("ssd_w_in", j): rows4(g["w_inT"]), ("ssd_w_out", j): rows4(g["wout"])}
        else:
            chunks = {("gmlp_w_in", j): _chip_major(g["win"]), ("gmlp_w_out", j): rows4(g["wout"])}
        stack = lambda prt, key, layers: jnp.stack([layer_grads[(l, prt)][key] for l in layers])
        ssd, gml, every = range(0, depth, 2), range(1, depth, 2), range(depth)
        if part == "mix" and i == 1:
            wholes["rep_early"] = _pack([stack("mix", "w_s", gml), stack("mix", "b_s", gml)])
        if part == "mix" and i == 0:
            small_g = dict(ssd_conv_w=stack("mix", "conv_w", ssd), gmlp_b_in=stack("mix", "b_in", gml),
                           gmlp_ln_w=stack("mix", "ln_w", gml), gmlp_ln_b=stack("mix", "ln_b", gml))
            chunks[("small", 0)] = jnp.stack([_pack([cut(small_g[n], k) for n in SMALL_SHARDED]) for k in range(N_CHIPS)])
            rep_g = dict(
                norm_mix=stack("mix", "norm_mix", every), norm_ffn=stack("ffn", "norm_ffn", every),
                ssd_conv_b=stack("mix", "conv_b", ssd), ssd_dt_bias=stack("mix", "dt_bias", ssd), ssd_a_log=stack("mix", "a_log", ssd),
                ssd_d=stack("mix", "d", ssd), ssd_norm_w=stack("mix", "norm_w", ssd), ple_norm=stack("ple", "pn", every),
                ple_gate_norm=stack("ple", "gn", every), final_norm=final_norm_grad[0])
            wholes["rep_late"] = _pack([rep_g[n] for n in REP_LATE])
        ks, wk = list(chunks), list(wholes)
        thru, lands, sems, token = _exchange_start([chunks[k] for k in ks], [wholes[k] for k in wk], f"grads_exchange_start_{part}_{i}")
        in_flight[(i, part)] = (ks, wk, thru, lands, sems)
        tokens[(i, part)] = token
        return token

    final_norm_grad = [None]
    norms = dict(norm_mix=norm_mix, norm_ffn=norm_ffn, final_norm=final_norm)
    loss_part, grad_x, g_fn, _ = _local_step(x[0], p[:, 0], loss_target[0], norms, layer_weights, on_layer_grads, final_norm_grad)
    loss = lax.psum(loss_part, ("x", "y", "c"))

    landed, res = {}, {}

    def wait_for(which, after):
        for i, part in which:
            ks, wk, thru, lands, sems = in_flight[(i, part)]
            thru, lands = _exchange_wait(thru, lands, len(ks), sems, after, f"grads_exchange_wait_{part}_{i}")
            landed.update(dict(zip(ks + wk, lands)))
            owns.update(dict(zip(ks + wk, thru)))

    def packed_update(names, gsum, tag):
        packs = [gsum] + list(_adamw(_pack([w[n] for n in names]), _pack([mom[n] for n in names]), _pack([var[n] for n in names]), gsum, tag))
        per_kind = [_unpack(pk, [w[n].shape for n in names]) for pk in packs]
        for t, n in enumerate(names):
            res[n] = [per_kind[k][t] for k in range(4)]

    def finish(big_names, with_small, rep_key, rep_names, tag):
        bufs = []
        for n in big_names + (("small",) if with_small else ()):
            layers = w[n].shape[0] if n != "small" else 1
            buf = None
            for l in range(layers):
                buf = _device_sum(landed[(n, l)], owns[(n, l)], place, f"grads_sum_{n}_{l}", into=buf, layer=l, layers=layers)
            bufs.append(buf)
        reduced = _sibling_join(bufs, f"grads_sibling_join_{tag}")
        for n, gsum in zip(big_names, reduced):
            res[n] = [view(n, a) for a in [gsum] + _adamw_nd(w[n], mom[n], var[n], gsum, "adamw_" + n)]
        if with_small:
            packed_update(SMALL_SHARDED, reduced[-1][0], "adamw_small_sharded")
        packed_update(rep_names, _device_sum_whole(landed[rep_key], owns[rep_key], place, f"grads_sum_{rep_key}"), f"adamw_{rep_key}")

    last = (0, "mix")
    late_big = tuple(n for n in BIG if n.startswith("ssd_"))
    early_big = tuple(n for n in BIG if n not in late_big)
    wait_for([ip for ip in reversed(parts) if ip != last], tokens[last])
    finish(early_big, False, "rep_early", REP_EARLY, "early")
    wait_for([last], res[early_big[-1]][1])
    finish(late_big, True, "rep_late", REP_LATE, "late")
    return (loss, grad_x[None], *[res[n][0] for n in WEIGHTS], *[res[n][1] for n in WEIGHTS],
            *[res[n][2] for n in WEIGHTS], *[res[n][3] for n in WEIGHTS])
```

```python
import math

import jax
import jax.numpy as jnp
from jax import lax
from jax.experimental import pallas as pl
from jax.experimental.pallas import tpu as pltpu

f32 = jnp.float32
bf16 = jnp.bfloat16
HI = lax.Precision.HIGHEST

LANES = 128
SUBLANES = 8
VMEM_LIMIT_BYTES = 56 * 1024 * 1024

HEADDIM = 64
STATE = 128
CHUNK = 128
CONV_K = 4
RMS_EPS = 1e-6
LN_EPS = 1e-5
ADAM_LR = 0.001
ADAM_B1 = 0.9
ADAM_B2 = 0.999
ADAM_EPS = 1e-08
ADAM_WD = 0.01
ADAM_STEP = 10

FFN_SUBTILES = 2

N_CHIPS = 4
N_DEV = 8
MESH = pl.DeviceIdType.MESH


def _cparams(sem):
    return pltpu.CompilerParams(dimension_semantics=sem, vmem_limit_bytes=VMEM_LIMIT_BYTES)


def _tile(n, want):
    if n <= want:
        return n
    t = want
    while n % t:
        t //= 2
    return t


def _row_spec(tm, c):
    return pl.BlockSpec((tm, c), lambda i: (i, 0))


def _full_spec(shape):
    nd = len(shape)
    return pl.BlockSpec(tuple(shape), lambda *_: (0,) * nd)


def _sigmoid(x):
    return 1.0 / (1.0 + jnp.exp(-x))


def _silu(x):
    return x * _sigmoid(x)


def _dsilu(x):
    s = _sigmoid(x)
    return s * (1.0 + x * (1.0 - s))


def _gelu(x):
    return 0.5 * x * (1.0 + lax.erf(x * (1.0 / math.sqrt(2.0))))


def _dgelu(x):
    return 0.5 * (1.0 + lax.erf(x * (1.0 / math.sqrt(2.0)))) + x * jnp.exp(-0.5 * x * x) * (1.0 / math.sqrt(2.0 * math.pi))


def _softplus(x):
    return jnp.maximum(x, 0.0) + jnp.log(1.0 + jnp.exp(-jnp.abs(x)))


def _rms(x, w, eps):
    r = lax.rsqrt(jnp.mean(x * x, axis=-1, keepdims=True) + eps)
    return x * r * w


def _rms_bwd(dy, x, w, eps):
    r = lax.rsqrt(jnp.mean(x * x, axis=-1, keepdims=True) + eps)
    xh = x * r
    g = dy * w
    dx = r * (g - xh * jnp.mean(g * xh, axis=-1, keepdims=True))
    dw = jnp.sum(dy * xh, axis=0, keepdims=True)
    return dx, dw


def _dot(a, b, dims=(((1,), (0,)), ((), ())), precision=None):
    return lax.dot_general(a, b, dims, precision=precision, preferred_element_type=f32)


NN = (((1,), (0,)), ((), ()))
NT = (((1,), (1,)), ((), ()))
TN = (((0,), (0,)), ((), ()))


def _split3(x):
    hi = x.astype(bf16)
    r1 = x - hi.astype(f32)
    mid = r1.astype(bf16)
    return hi, mid, (r1 - mid.astype(f32)).astype(bf16)


def _dot01_left(m01, x):
    mb = m01.astype(bf16)
    hi, mid, lo = _split3(x)
    return _dot(mb, hi, NN) + _dot(mb, mid, NN) + _dot(mb, lo, NN)


def _dot01_right(x, m01):
    mb = m01.astype(bf16)
    hi, mid, lo = _split3(x)
    return _dot(hi, mb, NN) + _dot(mid, mb, NN) + _dot(lo, mb, NN)


def _mm(a, b, *, mode="nn", out_dtype=f32, res=None, kbatch=False, brows=None, after=None, tm=1024, tn=1024, tk=1024, name):
    a3, b3 = a.ndim == 3, b.ndim == 3
    nb = a.shape[0] if a3 else (b.shape[0] if b3 else 1)
    ash, bsh = a.shape[-2:], b.shape[-2:]
    if brows is not None:
        bsh = (brows[1], bsh[1])
    if mode == "nn":
        M, K, N = ash[0], ash[1], bsh[1]
    elif mode == "nt":
        M, K, N = ash[0], ash[1], bsh[0]
    else:
        K, M, N = ash[0], ash[1], bsh[1]
    tm, tn, tk = _tile(M, tm), (N if N % LANES else _tile(N, tn)), (K if K % LANES else _tile(K, tk))
    b0 = 0
    if brows is not None:
        assert mode in ("nn", "nt") and bsh[0] == (K if mode == "nn" else N)
        blk = tk if mode == "nn" else tn
        while brows[0] % blk:
            blk //= 2
        assert blk % LANES == 0 or blk == brows[1]
        b0 = brows[0] // blk
        tn, tk = (tn, blk) if mode == "nn" else (blk, tk)
    nk = K // tk
    if kbatch:
        assert a3 and b3
        grid = (1, M // tm, N // tn, nb * nk)
        bi = lambda g, k: k // nk
        ki = lambda g, k: k % nk
    else:
        grid = (nb, M // tm, N // tn, nk)
        bi = lambda g, k: g
        ki = lambda g, k: k
    nsteps = grid[3]

    def spec(is3, blk, imap):
        if is3:
            return pl.BlockSpec((None,) + blk, lambda g, i, j, k: (bi(g, k),) + imap(i, j, ki(g, k)))
        return pl.BlockSpec(blk, lambda g, i, j, k: imap(i, j, ki(g, k)))

    if mode == "nn":
        a_spec = spec(a3, (tm, tk), lambda i, j, k: (i, k))
        b_spec = spec(b3, (tk, tn), lambda i, j, k: (k + b0, j))
        dims = NN
    elif mode == "nt":
        a_spec = spec(a3, (tm, tk), lambda i, j, k: (i, k))
        b_spec = spec(b3, (tn, tk), lambda i, j, k: (j + b0, k))
        dims = NT
    else:
        a_spec = spec(a3, (tk, tm), lambda i, j, k: (k, i))
        b_spec = spec(b3, (tk, tn), lambda i, j, k: (k, j))
        dims = TN
    out3 = (a3 or b3) and not kbatch
    if out3:
        o_spec = pl.BlockSpec((None, tm, tn), lambda g, i, j, k: (g, i, j))
        o_shape = jax.ShapeDtypeStruct((nb, M, N), out_dtype)
    else:
        o_spec = pl.BlockSpec((tm, tn), lambda g, i, j, k: (i, j))
        o_shape = jax.ShapeDtypeStruct((M, N), out_dtype)
    in_specs = [a_spec, b_spec]
    args = [a, b]
    if res is not None:
        in_specs.append(pl.BlockSpec((tm, tn), lambda g, i, j, k: (i, j)))
        args.append(res)
    if after is not None:
        in_specs.append(pl.BlockSpec(memory_space=pl.ANY))
        args.append(after)

    def body(*refs):
        a_ref, b_ref = refs[:2]
        r_ref = refs[2] if res is not None else None
        o_ref, acc_ref = refs[-2:]
        k = pl.program_id(3)

        @pl.when(k == 0)
        def _():
            acc_ref[...] = jnp.zeros_like(acc_ref)

        acc_ref[...] += _dot(a_ref[...].astype(bf16), b_ref[...].astype(bf16), dims)

        @pl.when(k == nsteps - 1)
        def _():
            r = acc_ref[...]
            if res is not None:
                r = r + r_ref[...]
            o_ref[...] = r.astype(o_ref.dtype)

    return pl.pallas_call(
        body, name=name, grid=grid, in_specs=in_specs, out_specs=o_spec, out_shape=o_shape,
        scratch_shapes=[pltpu.VMEM((tm, tn), f32)],
        compiler_params=_cparams(("parallel", "parallel", "parallel", "arbitrary")),
    )(*args)


def _rowcall(fn, *, name, rows, fulls, out_rows, out_accs=(), tm=512):
    S = rows[0].shape[0]
    tm = _tile(S, tm)
    n_r, n_f, n_or, n_oa = len(rows), len(fulls), len(out_rows), len(out_accs)

    def body(*refs):
        ins = [r[...] for r in refs[:n_r + n_f]]
        outs = fn(*ins)
        if not isinstance(outs, (tuple, list)):
            outs = (outs,)
        o_refs = refs[n_r + n_f:]
        for o_ref, v in zip(o_refs[:n_or], outs[:n_or]):
            o_ref[...] = v.astype(o_ref.dtype)
        if n_oa:
            first = pl.program_id(0) == 0

            @pl.when(first)
            def _():
                for o_ref, v in zip(o_refs[n_or:], outs[n_or:]):
                    o_ref[...] = v

            @pl.when(jnp.logical_not(first))
            def _():
                for o_ref, v in zip(o_refs[n_or:], outs[n_or:]):
                    o_ref[...] += v

    in_specs = [_row_spec(tm, r.shape[1]) for r in rows] + [_full_spec(f.shape) for f in fulls]
    out_specs = [_row_spec(tm, c) for c, _ in out_rows] + [_full_spec(s) for s in out_accs]
    out_shape = [jax.ShapeDtypeStruct((S, c), d) for c, d in out_rows] + [jax.ShapeDtypeStruct(s, f32) for s in out_accs]
    res = pl.pallas_call(
        body, name=name, grid=(S // tm,), in_specs=in_specs, out_specs=out_specs, out_shape=out_shape,
        compiler_params=_cparams(("arbitrary",) if n_oa else ("parallel",)),
    )(*rows, *fulls)
    return res


def _row2(v):
    return v.reshape(1, -1)


def _rms_fwd(h, w, name):
    D = h.shape[1]
    return _rowcall(lambda x, w_: _rms(x, w_, RMS_EPS), name=name, rows=[h], fulls=[_row2(w)], out_rows=[(D, bf16)])[0]


def _conv_fwd(xpre, w, b, name):
    S, C = xpre.shape
    tm, tc = _tile(S, 512), _tile(C, 1024)
    hb = tm // SUBLANES

    def body(x_ref, halo_ref, w_ref, b_ref, c_ref, o_ref):
        i = pl.program_id(1)
        x = x_ref[...]
        halo = jnp.where(i > 0, halo_ref[...], 0.0)
        row = lax.broadcasted_iota(jnp.int32, x.shape, 0)
        row8 = lax.broadcasted_iota(jnp.int32, halo.shape, 0)
        x0 = x[0:SUBLANES, :]
        acc = x * w_ref[CONV_K - 1:CONV_K, :] + b_ref[...]
        acc0 = x0 * w_ref[CONV_K - 1:CONV_K, :] + b_ref[...]
        for k in range(1, CONV_K):
            wk = w_ref[CONV_K - 1 - k:CONV_K - k, :]
            acc = acc + pltpu.roll(x, k, axis=0) * wk
            acc0 = acc0 + jnp.where(row8 < k, pltpu.roll(halo, k, axis=0), pltpu.roll(x0, k, axis=0)) * wk
        c_ref[...] = acc
        o_ref[...] = _silu(acc)
        c_ref[0:SUBLANES, :] = acc0
        o_ref[0:SUBLANES, :] = _silu(acc0)

    return pl.pallas_call(
        body, name=name, grid=(C // tc, S // tm),
        in_specs=[pl.BlockSpec((tm, tc), lambda j, i: (i, j)),
                  pl.BlockSpec((SUBLANES, tc), lambda j, i: (jnp.maximum(i * hb - 1, 0), j)),
                  pl.BlockSpec((CONV_K, tc), lambda j, i: (0, j)),
                  pl.BlockSpec((1, tc), lambda j, i: (0, j))],
        out_specs=[pl.BlockSpec((tm, tc), lambda j, i: (i, j))] * 2,
        out_shape=[jax.ShapeDtypeStruct((S, C), f32)] * 2,
        compiler_params=_cparams(("parallel", "parallel")),
    )(xpre, xpre, w, _row2(b))


def _conv_bwd_dc(dxbc, c, xpre, name):
    S, C = xpre.shape
    tm, tc = _tile(S, 512), _tile(C, 1024)
    hb = tm // SUBLANES

    def body(d_ref, c_ref, x_ref, halo_ref, dc_ref, dw_ref, db_ref):
        i = pl.program_id(1)
        x = x_ref[...]
        dc = d_ref[...] * _dsilu(c_ref[...])
        dc_ref[...] = dc
        halo = jnp.where(i > 0, halo_ref[...], 0.0)
        row = lax.broadcasted_iota(jnp.int32, x.shape, 0)
        row8 = lax.broadcasted_iota(jnp.int32, halo.shape, 0)
        x0 = x[0:SUBLANES, :]
        dc0 = dc[0:SUBLANES, :]
        parts = [jnp.sum(dc * x, axis=0, keepdims=True)]
        for k in range(1, CONV_K):
            xs_big = jnp.where(row < SUBLANES, 0.0, pltpu.roll(x, k, axis=0))
            xs0 = jnp.where(row8 < k, pltpu.roll(halo, k, axis=0), pltpu.roll(x0, k, axis=0))
            parts.append(jnp.sum(dc * xs_big, axis=0, keepdims=True) + jnp.sum(dc0 * xs0, axis=0, keepdims=True))
        dw = jnp.concatenate([parts[CONV_K - 1 - k] for k in range(CONV_K)] + [jnp.zeros((SUBLANES - CONV_K, x.shape[1]), f32)], axis=0)
        db = jnp.sum(dc, axis=0, keepdims=True)

        @pl.when(i == 0)
        def _():
            dw_ref[...] = dw
            db_ref[...] = db

        @pl.when(i > 0)
        def _():
            dw_ref[...] += dw
            db_ref[...] += db

    return pl.pallas_call(
        body, name=name, grid=(C // tc, S // tm),
        in_specs=[pl.BlockSpec((tm, tc), lambda j, i: (i, j))] * 3 +
                 [pl.BlockSpec((SUBLANES, tc), lambda j, i: (jnp.maximum(i * hb - 1, 0), j))],
        out_specs=[pl.BlockSpec((tm, tc), lambda j, i: (i, j)),
                   pl.BlockSpec((SUBLANES, tc), lambda j, i: (0, j)),
                   pl.BlockSpec((1, tc), lambda j, i: (0, j))],
        out_shape=[jax.ShapeDtypeStruct((S, C), f32), jax.ShapeDtypeStruct((SUBLANES, C), f32), jax.ShapeDtypeStruct((1, C), f32)],
        compiler_params=_cparams(("parallel", "arbitrary")),
    )(dxbc, c, xpre, xpre)


def _conv_bwd_dx(dc, w, name):
    S, C = dc.shape
    tm, tc = _tile(S, 512), _tile(C, 1024)
    hb = tm // SUBLANES
    nrow = S // tm
    last8 = S // SUBLANES - 1

    def body(d_ref, nxt_ref, w_ref, o_ref):
        i = pl.program_id(1)
        d = d_ref[...]
        nxt = jnp.where(i < nrow - 1, nxt_ref[...], 0.0)
        row8 = lax.broadcasted_iota(jnp.int32, nxt.shape, 0)
        dl = d[tm - SUBLANES:tm, :]
        acc = d * w_ref[CONV_K - 1:CONV_K, :]
        accl = dl * w_ref[CONV_K - 1:CONV_K, :]
        for j in range(1, CONV_K):
            wk = w_ref[CONV_K - 1 - j:CONV_K - j, :]
            acc = acc + pltpu.roll(d, tm - j, axis=0) * wk
            accl = accl + jnp.where(row8 >= SUBLANES - j, pltpu.roll(nxt, SUBLANES - j, axis=0), pltpu.roll(dl, SUBLANES - j, axis=0)) * wk
        o_ref[...] = acc.astype(o_ref.dtype)
        o_ref[tm - SUBLANES:tm, :] = accl.astype(o_ref.dtype)

    return pl.pallas_call(
        body, name=name, grid=(C // tc, nrow),
        in_specs=[pl.BlockSpec((tm, tc), lambda j, i: (i, j)),
                  pl.BlockSpec((SUBLANES, tc), lambda j, i: (jnp.minimum((i + 1) * hb, last8), j)),
                  pl.BlockSpec((CONV_K, tc), lambda j, i: (0, j))],
        out_specs=pl.BlockSpec((tm, tc), lambda j, i: (i, j)),
        out_shape=jax.ShapeDtypeStruct((S, C), f32),
        compiler_params=_cparams(("parallel", "parallel")),
    )(dc, dc, w)


def _halfsum(v, lane_lo):
    s0 = jnp.sum(jnp.where(lane_lo, v, 0.0), axis=1, keepdims=True)
    s1 = jnp.sum(jnp.where(lane_lo, 0.0, v), axis=1, keepdims=True)
    return jnp.where(lane_lo, s0, s1)


def _ssd_specs(S, inner, GN, nchunks, rev):
    L = CHUNK
    cm = (lambda c: nchunks - 1 - c) if rev else (lambda c: c)
    xs = pl.BlockSpec((L, inner), lambda c: (cm(c), 0))
    bb = pl.BlockSpec((L, GN), lambda c: (cm(c), inner // GN))
    cc = pl.BlockSpec((L, GN), lambda c: (cm(c), inner // GN + 1))
    row = pl.BlockSpec((L, inner), lambda c: (cm(c), 0))
    vec = pl.BlockSpec((1, inner), lambda c: (0, 0))
    st = pl.BlockSpec((None, inner, STATE), lambda c: (cm(c), 0, 0))
    return xs, bb, cc, row, vec, st


def _ssd_fwd(xbc, dtx, ax, dx, G, name):
    S, inner = dtx.shape
    GN = G * STATE
    L = CHUNK
    nchunks = S // L
    npairs = inner // LANES
    ppg = npairs // G
    assert inner % GN == 0 and L == LANES and STATE == LANES

    def body(xs_ref, b_ref, c_ref, dtx_ref, ax_ref, dx_ref, y_ref, so_ref, st_ref):
        ci = pl.program_id(0)

        @pl.when(ci == 0)
        def _():
            st_ref[...] = jnp.zeros_like(st_ref)

        r = lax.broadcasted_iota(jnp.int32, (L, L), 0)
        cidx = lax.broadcasted_iota(jnp.int32, (L, L), 1)
        tril = cidx <= r
        lane_lo = cidx < HEADDIM
        xs = xs_ref[...]
        dtv = dtx_ref[...]
        X = xs * dtv
        da = dtv * ax_ref[...]
        cs = _dot01_left(tril, da)
        cs_last = jnp.sum(da, axis=0, keepdims=True)
        so_ref[...] = st_ref[...]
        for g in range(G):
            Bg = b_ref[:, g * STATE:(g + 1) * STATE].astype(bf16)
            Cg = c_ref[:, g * STATE:(g + 1) * STATE].astype(bf16)
            CB = _dot(Cg, Bg, NT)
            for j in range(ppg):
                lo = (g * ppg + j) * LANES
                tile = cs[:, lo:lo + LANES]
                rl = pltpu.roll(tile, HEADDIM, axis=1)
                Xp = X[:, lo:lo + LANES]
                prev = st_ref[lo:lo + LANES, :]
                ypair = _dot(Cg, prev.astype(bf16), NT) * jnp.exp(tile)
                for half in (0, 1):
                    hm = lane_lo if half == 0 else jnp.logical_not(lane_lo)
                    colb = jnp.where(hm, tile, rl)
                    Lm = jnp.exp(jnp.where(tril, colb - colb.T, -1e30))
                    W = (CB * Lm).astype(bf16)
                    ypair = ypair + _dot(W, jnp.where(hm, Xp, 0.0).astype(bf16), NN)
                y_ref[:, lo:lo + LANES] = ypair + xs[:, lo:lo + LANES] * dx_ref[:, lo:lo + LANES]
                last = cs_last[:, lo:lo + LANES]
                snew = _dot((Xp * jnp.exp(last - tile)).astype(bf16), Bg, TN)
                dec_rows = jnp.broadcast_to(jnp.exp(last), (L, LANES)).T
                st_ref[lo:lo + LANES, :] = dec_rows * prev + snew

    xs_s, b_s, c_s, row_s, vec_s, st_s = _ssd_specs(S, inner, GN, nchunks, False)
    return pl.pallas_call(
        body, name=name, grid=(nchunks,),
        in_specs=[xs_s, b_s, c_s, row_s, vec_s, vec_s],
        out_specs=[row_s, st_s],
        out_shape=[jax.ShapeDtypeStruct((S, inner), f32), jax.ShapeDtypeStruct((nchunks, inner, STATE), f32)],
        scratch_shapes=[pltpu.VMEM((inner, STATE), f32)],
        compiler_params=_cparams(("arbitrary",)),
    )(xbc, xbc, xbc, dtx, ax, dx)


def _ssd_bwd(dy, y, xbc, dtx, ax, dx, states, et, G, name):
    S, inner = dtx.shape
    H = et.shape[1]
    GN = G * STATE
    Cc = inner + 2 * GN
    L = CHUNK
    nchunks = S // L
    npairs = inner // LANES
    ppg = npairs // G

    def body(dy_ref, y_ref, xs_ref, b_ref, c_ref, dtx_ref, ax_ref, dx_ref, si_ref, et_ref,
             dxbc_ref, ddt_ref, dax_ref, ddx_ref, dst_ref, dA_ref, dAl_ref, ddtp_ref):
        ci = pl.program_id(0)

        @pl.when(ci == 0)
        def _():
            dst_ref[...] = jnp.zeros_like(dst_ref)
            dax_ref[...] = jnp.zeros_like(dax_ref)
            ddx_ref[...] = jnp.zeros_like(ddx_ref)

        r = lax.broadcasted_iota(jnp.int32, (L, L), 0)
        cidx = lax.broadcasted_iota(jnp.int32, (L, L), 1)
        tril = cidx <= r
        lane_lo = cidx < HEADDIM
        lane_lo1 = lax.broadcasted_iota(jnp.int32, (1, LANES), 1) < HEADDIM
        xs = xs_ref[...]
        dtv = dtx_ref[...]
        dyv = dy_ref[...]
        X = xs * dtv
        da = dtv * ax_ref[...]
        cs = _dot01_left(tril, da)
        cs_last = jnp.sum(da, axis=0, keepdims=True)
        for g in range(G):
            Bg = b_ref[:, g * STATE:(g + 1) * STATE].astype(bf16)
            Cg = c_ref[:, g * STATE:(g + 1) * STATE].astype(bf16)
            CB = _dot(Cg, Bg, NT)
            dCB = jnp.zeros((L, L), f32)
            dBg = jnp.zeros((L, STATE), f32)
            dCg = jnp.zeros((L, STATE), f32)
            for j in range(ppg):
                lo = (g * ppg + j) * LANES
                tile = cs[:, lo:lo + LANES]
                rl = pltpu.roll(tile, HEADDIM, axis=1)
                eA = jnp.exp(tile)
                Xp = X[:, lo:lo + LANES]
                dYp = dyv[:, lo:lo + LANES]
                xsp = xs[:, lo:lo + LANES]
                prev = si_ref[lo:lo + LANES, :]
                dSn = dst_ref[lo:lo + LANES, :]
                prev_b = prev.astype(bf16)
                dSn_b = dSn.astype(bf16)
                dYe = (dYp * eA).astype(bf16)
                dCg = dCg + _dot(dYe, prev_b, NN)
                dprev = _dot(dYe, Cg, TN)
                last = cs_last[:, lo:lo + LANES]
                w = jnp.exp(last - tile)
                BdS = _dot(Bg, dSn_b, NT)
                Xw = Xp * w
                XwB = Xw * BdS
                dAl_t = _halfsum(jnp.sum(XwB, axis=0, keepdims=True), lane_lo1)
                dBg = dBg + _dot(Xw.astype(bf16), dSn_b, NN)
                dec_rows = jnp.broadcast_to(jnp.exp(last), (L, LANES)).T
                dprev = dprev + dec_rows * dSn
                rsum = jnp.sum(dSn * prev * dec_rows, axis=1, keepdims=True)
                s0 = jnp.sum(rsum[0:HEADDIM], axis=0, keepdims=True)
                s1 = jnp.sum(rsum[HEADDIM:LANES], axis=0, keepdims=True)
                dAl_t = dAl_t + jnp.where(lane_lo1, s0, s1)
                dXd = jnp.zeros((L, LANES), f32)
                for half in (0, 1):
                    hm = lane_lo if half == 0 else jnp.logical_not(lane_lo)
                    colb = jnp.where(hm, tile, rl)
                    Lm = jnp.exp(jnp.where(tril, colb - colb.T, -1e30))
                    dYh = jnp.where(hm, dYp, 0.0).astype(bf16)
                    dW = _dot(dYh, jnp.where(hm, Xp, 0.0).astype(bf16), NT)
                    dXd = dXd + _dot((CB * Lm).astype(bf16), dYh, TN)
                    dCB = dCB + dW * Lm
                yoff = _dot(Cg, prev_b, NT) * eA
                ydiag = y_ref[:, lo:lo + LANES] - xsp * dx_ref[:, lo:lo + LANES] - yoff
                dYb = dYp.astype(bf16).astype(f32)
                Xb = Xp.astype(bf16).astype(f32)
                dA_t = _halfsum(dYb * ydiag - Xb * dXd + dYp * yoff - XwB, lane_lo)
                dXp = w * BdS + dXd
                dxbc_ref[:, lo:lo + LANES] = dXp * dtv[:, lo:lo + LANES] + dYp * dx_ref[:, lo:lo + LANES]
                ddtp_ref[:, lo:lo + LANES] = dXp * xsp
                ddx_ref[:, lo:lo + LANES] += jnp.sum(dYp * xsp, axis=0, keepdims=True)
                dA_ref[:, lo:lo + LANES] = dA_t
                dAl_ref[:, lo:lo + LANES] = dAl_t
                dst_ref[lo:lo + LANES, :] = dprev
            dCBb = dCB.astype(bf16)
            dxbc_ref[:, inner + g * STATE:inner + (g + 1) * STATE] = dBg + _dot(dCBb, Cg, TN)
            dxbc_ref[:, inner + GN + g * STATE:inner + GN + (g + 1) * STATE] = dCg + _dot(dCBb, Bg, NN)
        dda = _dot01_left(cidx >= r, dA_ref[...]) + dAl_ref[...]
        ddt_full = ddtp_ref[...] + dda * ax_ref[...] * (1.0 / HEADDIM)
        ddt_ref[...] = _dot01_right(ddt_full, et_ref[...])
        dax_ref[...] += jnp.sum(dda * dtv, axis=0, keepdims=True)

    xs_s, b_s, c_s, row_s, vec_s, st_s = _ssd_specs(S, inner, GN, nchunks, True)
    return pl.pallas_call(
        body, name=name, grid=(nchunks,),
        in_specs=[row_s, row_s, xs_s, b_s, c_s, row_s, vec_s, vec_s, st_s, _full_spec(et.shape)],
        out_specs=[pl.BlockSpec((L, Cc), lambda c: (nchunks - 1 - c, 0)),
                   pl.BlockSpec((L, H), lambda c: (nchunks - 1 - c, 0)), vec_s, vec_s],
        out_shape=[jax.ShapeDtypeStruct((S, Cc), f32), jax.ShapeDtypeStruct((S, H), f32),
                   jax.ShapeDtypeStruct((1, inner), f32), jax.ShapeDtypeStruct((1, inner), f32)],
        scratch_shapes=[pltpu.VMEM((inner, STATE), f32), pltpu.VMEM((L, inner), f32),
                        pltpu.VMEM((1, inner), f32), pltpu.VMEM((L, inner), f32)],
        compiler_params=_cparams(("arbitrary",)),
    )(dy, y, xbc, xbc, xbc, dtx, ax, dx, states, et)


def _dt_fwd(dt_pre, bias, e, name):
    H, inner = e.shape

    def fn(dp, b, e_):
        dt = _softplus(dp + b)
        return dt, _dot01_right(dt, e_)

    return _rowcall(fn, name=name, rows=[dt_pre], fulls=[_row2(bias), e], out_rows=[(H, f32), (inner, f32)])


def _dt_bwd(ddt, dt_pre, bias, name):
    H = ddt.shape[1]

    def fn(dd, dp, b):
        g = dd * _sigmoid(dp + b)
        return g, jnp.sum(g, axis=0, keepdims=True)

    return _rowcall(fn, name=name, rows=[ddt, dt_pre], fulls=[_row2(bias)], out_rows=[(H, f32)], out_accs=[(1, H)])


def _gnorm_fwd(y, z, w, G, name):
    inner = y.shape[1]
    gs = inner // G

    def fn(y_, z_, w_):
        gg = y_ * _silu(z_)
        outs = []
        for g in range(G):
            sl = slice(g * gs, (g + 1) * gs)
            outs.append(_rms(gg[:, sl], w_[:, sl], LN_EPS))
        return jnp.concatenate(outs, axis=1)

    return _rowcall(fn, name=name, rows=[y, z], fulls=[_row2(w)], out_rows=[(inner, bf16)], tm=256)[0]


def _gnorm_bwd(dyn, y, z, w, G, name):
    inner = y.shape[1]
    gs = inner // G

    def fn(d_, y_, z_, w_):
        sz = _silu(z_)
        gg = y_ * sz
        dgs, dws = [], []
        for g in range(G):
            sl = slice(g * gs, (g + 1) * gs)
            dg, dw = _rms_bwd(d_[:, sl], gg[:, sl], w_[:, sl], LN_EPS)
            dgs.append(dg)
            dws.append(dw)
        dgg = jnp.concatenate(dgs, axis=1)
        return dgg * sz, dgg * y_ * _dsilu(z_), jnp.concatenate(dws, axis=1)

    return _rowcall(fn, name=name, rows=[dyn, y, z], fulls=[_row2(w)], out_rows=[(inner, f32), (inner, f32)],
                    out_accs=[(1, inner)], tm=256)


def _gmlp_parts(pre, lw, lb, I):
    hp = _gelu(pre)
    uu = hp[:, :I]
    vp = hp[:, I:]
    xc = vp - jnp.mean(vp, axis=-1, keepdims=True)
    rstd = lax.rsqrt(jnp.mean(xc * xc, axis=-1, keepdims=True) + LN_EPS)
    vhat = xc * rstd
    return uu, vhat, rstd, vhat * lw + lb


def _gmlp_mid_fwd(pre, b_in, ln_w, ln_b, w_s, bsx, name):
    S, two_i = pre.shape
    I = two_i // 2
    NG = w_s.shape[0]
    gd = I // NG
    L = CHUNK

    def body(pre_ref, bi_ref, lw_ref, lb_ref, ws_ref, bsx_ref, o_ref):
        uu, _, _, vv = _gmlp_parts(pre_ref[...] + bi_ref[...], lw_ref[...], lb_ref[...], I)
        r = lax.broadcasted_iota(jnp.int32, (L, L), 0)
        cidx = lax.broadcasted_iota(jnp.int32, (L, L), 1)
        tril = cidx <= r
        for g in range(NG):
            sl = slice(g * gd, (g + 1) * gd)
            wg = jnp.where(tril, ws_ref[g], 0.0).astype(bf16)
            mixed = _dot(wg, vv[:, sl].astype(bf16), NN) + bsx_ref[:, sl]
            o_ref[:, sl] = (uu[:, sl] * mixed).astype(o_ref.dtype)

    return pl.pallas_call(
        body, name=name, grid=(S // L,),
        in_specs=[_row_spec(L, two_i), _full_spec((1, two_i)), _full_spec((1, I)), _full_spec((1, I)), _full_spec(w_s.shape), _full_spec(bsx.shape)],
        out_specs=_row_spec(L, I), out_shape=jax.ShapeDtypeStruct((S, I), bf16),
        compiler_params=_cparams(("parallel",)),
    )(pre, _row2(b_in), _row2(ln_w), _row2(ln_b), w_s, bsx)


def _gmlp_mid_bwd(do, pre, b_in, ln_w, ln_b, w_s, bsx, name):
    S, two_i = pre.shape
    I = two_i // 2
    NG = w_s.shape[0]
    gd = I // NG
    L = CHUNK

    def body(do_ref, pre_ref, bi_ref, lw_ref, lb_ref, ws_ref, bsx_ref, dpre_ref, dbi_ref, dlw_ref, dlb_ref, dws_ref, dbs_ref, dvv_ref):
        ci = pl.program_id(0)

        @pl.when(ci == 0)
        def _():
            for ref in (dbi_ref, dlw_ref, dlb_ref, dws_ref, dbs_ref):
                ref[...] = jnp.zeros_like(ref)

        pre = pre_ref[...] + bi_ref[...]
        lw = lw_ref[...]
        uu, vhat, rstd, vv = _gmlp_parts(pre, lw, lb_ref[...], I)
        dov = do_ref[...]
        r = lax.broadcasted_iota(jnp.int32, (L, L), 0)
        cidx = lax.broadcasted_iota(jnp.int32, (L, L), 1)
        tril = cidx <= r
        duus = []
        for g in range(NG):
            sl = slice(g * gd, (g + 1) * gd)
            wg = jnp.where(tril, ws_ref[g], 0.0).astype(bf16)
            vg = vv[:, sl].astype(bf16)
            mixed = _dot(wg, vg, NN) + bsx_ref[:, sl]
            duus.append(dov[:, sl] * mixed)
            dmixed = dov[:, sl] * uu[:, sl]
            dbs_ref[:, sl] += dmixed
            dmb = dmixed.astype(bf16)
            dvv_ref[:, sl] = _dot(wg, dmb, TN)
            dws_ref[g] += jnp.where(tril, _dot(dmb, vg, NT), 0.0)
        duu = jnp.concatenate(duus, axis=1)
        dvv = dvv_ref[...]
        dlw_ref[...] += jnp.sum(dvv * vhat, axis=0, keepdims=True)
        dlb_ref[...] += jnp.sum(dvv, axis=0, keepdims=True)
        dvh = dvv * lw
        dvp = rstd * (dvh - jnp.mean(dvh, axis=-1, keepdims=True) - vhat * jnp.mean(dvh * vhat, axis=-1, keepdims=True))
        dpre = jnp.concatenate([duu, dvp], axis=1) * _dgelu(pre)
        dbi_ref[...] += jnp.sum(dpre, axis=0, keepdims=True)
        dpre_ref[...] = dpre.astype(dpre_ref.dtype)

    return pl.pallas_call(
        body, name=name, grid=(S // L,),
        in_specs=[_row_spec(L, I), _row_spec(L, two_i), _full_spec((1, two_i)), _full_spec((1, I)), _full_spec((1, I)),
                  _full_spec(w_s.shape), _full_spec(bsx.shape)],
        out_specs=[_row_spec(L, two_i), _full_spec((1, two_i)), _full_spec((1, I)), _full_spec((1, I)), _full_spec(w_s.shape), _full_spec((L, I))],
        out_shape=[jax.ShapeDtypeStruct((S, two_i), bf16), jax.ShapeDtypeStruct((1, two_i), f32), jax.ShapeDtypeStruct((1, I), f32),
                   jax.ShapeDtypeStruct((1, I), f32), jax.ShapeDtypeStruct(w_s.shape, f32), jax.ShapeDtypeStruct((L, I), f32)],
        scratch_shapes=[pltpu.VMEM((L, I), f32)],
        compiler_params=_cparams(("arbitrary",)),
    )(do, pre, _row2(b_in), _row2(ln_w), _row2(ln_b), w_s, bsx)


def _lane_group_sum(acc, eg, name):
    NG = eg.shape[1]
    return _rowcall(lambda a, e: _dot(a, e, NN, HI), name=name, rows=[acc], fulls=[eg], out_rows=[(NG, f32)])[0]


def _ffn_fwd_fused(h1, nf_w, wg, wu, wd, name):
    S, D = h1.shape
    nb, F4, _ = wg.shape
    tm = _tile(S, 512)

    def body(h_ref, nf_ref, wg_ref, wu_ref, wd_ref, h2_ref, u_ref, g_ref, up_ref, a_ref, acc_ref):
        k = pl.program_id(1)

        @pl.when(k == 0)
        def _():
            u_ref[...] = _rms(h_ref[...], nf_ref[...], RMS_EPS).astype(u_ref.dtype)
            acc_ref[...] = jnp.zeros_like(acc_ref)

        for r in range(FFN_SUBTILES):
            rs = pl.ds(r * (tm // FFN_SUBTILES), tm // FFN_SUBTILES)
            uv = u_ref[rs, :]
            g = _dot(uv, wg_ref[...], NT)
            up = _dot(uv, wu_ref[...], NT)
            a = (_silu(g) * up).astype(bf16)
            g_ref[rs, :] = g.astype(g_ref.dtype)
            up_ref[rs, :] = up.astype(up_ref.dtype)
            a_ref[rs, :] = a
            acc_ref[rs, :] += _dot(a, wd_ref[...], NN)

        @pl.when(k == nb - 1)
        def _():
            h2_ref[...] = h_ref[...] + acc_ref[...]

    row = pl.BlockSpec((tm, D), lambda i, k: (i, 0))
    wspec = pl.BlockSpec((None, F4, D), lambda i, k: (k, 0, 0))
    cspec = pl.BlockSpec((None, tm, F4), lambda i, k: (k, i, 0))
    chunk = jax.ShapeDtypeStruct((nb, S, F4), bf16)
    return pl.pallas_call(
        body, name=name, grid=(S // tm, nb),
        in_specs=[row, _full_spec((1, D)), wspec, wspec, pl.BlockSpec((None, F4, D), lambda i, k: (k, 0, 0))],
        out_specs=[row, row, cspec, cspec, cspec],
        out_shape=[jax.ShapeDtypeStruct((S, D), f32), jax.ShapeDtypeStruct((S, D), bf16), chunk, chunk, chunk],
        scratch_shapes=[pltpu.VMEM((tm, D), f32)],
        compiler_params=_cparams(("parallel", "arbitrary")),
    )(h1, _row2(nf_w), wg, wu, wd)


def _ffn_bwd_fused(dh, h1, nf_w, wd, wg, wu, G, U, name, after=None):
    S, D = dh.shape
    nb, F4, _ = wd.shape
    tm = _tile(S, 512)

    def body(dh_ref, h_ref, nf_ref, wd_ref, wg_ref, wu_ref, g_ref, up_ref, *rest):
        dg_ref, du_ref, dh1_ref, dnf_ref, acc_ref = rest[-5:]
        i, k = pl.program_id(0), pl.program_id(1)

        @pl.when(k == 0)
        def _():
            acc_ref[...] = jnp.zeros_like(acc_ref)

        for r in range(FFN_SUBTILES):
            rs = pl.ds(r * (tm // FFN_SUBTILES), tm // FFN_SUBTILES)
            dA = _dot(dh_ref[rs, :].astype(bf16), wd_ref[...], NT)
            g = g_ref[rs, :].astype(f32)
            dg = (dA * up_ref[rs, :].astype(f32) * _dsilu(g)).astype(bf16)
            du = (dA * _silu(g)).astype(bf16)
            dg_ref[rs, :] = dg
            du_ref[rs, :] = du
            acc_ref[rs, :] += _dot(dg, wg_ref[...], NN) + _dot(du, wu_ref[...], NN)

        @pl.when(k == nb - 1)
        def _():
            dx, dw = _rms_bwd(acc_ref[...], h_ref[...], nf_ref[...], RMS_EPS)
            dh1_ref[...] = dh_ref[...] + dx

            @pl.when(i == 0)
            def _():
                dnf_ref[...] = dw

            @pl.when(i > 0)
            def _():
                dnf_ref[...] += dw

    row = pl.BlockSpec((tm, D), lambda i, k: (i, 0))
    wspec = pl.BlockSpec((None, F4, D), lambda i, k: (k, 0, 0))
    cspec = pl.BlockSpec((None, tm, F4), lambda i, k: (k, i, 0))
    chunk = jax.ShapeDtypeStruct((nb, S, F4), bf16)
    return pl.pallas_call(
        body, name=name, grid=(S // tm, nb),
        in_specs=[row, row, _full_spec((1, D)), wspec, wspec, wspec, cspec, cspec] + ([] if after is None else [pl.BlockSpec(memory_space=pl.ANY)]),
        out_specs=[cspec, cspec, row, _full_spec((1, D))],
        out_shape=[chunk, chunk, jax.ShapeDtypeStruct((S, D), f32), jax.ShapeDtypeStruct((1, D), f32)],
        scratch_shapes=[pltpu.VMEM((tm, D), f32)],
        compiler_params=_cparams(("arbitrary", "arbitrary")),
    )(dh, h1, _row2(nf_w), wd, wg, wu, G, U, *([] if after is None else [after]))


def _rms_bwd_add(dres, du, h, w, name):
    D = h.shape[1]

    def fn(dr, du_, h_, w_):
        dx, dw = _rms_bwd(du_, h_, w_, RMS_EPS)
        return dr + dx, dw

    return _rowcall(fn, name=name, rows=[dres, du, h], fulls=[_row2(w)], out_rows=[(D, f32)], out_accs=[(1, D)])


def _ple_fwd(h, p_i, wp, pn, gn, wgate, name):
    D = h.shape[1]

    def fn(h_, p_, wp_, pn_, gn_, wg_):
        pe = _dot(p_.astype(bf16), wp_, NN)
        e = _rms(pe, pn_, RMS_EPS)
        q = _rms(h_, gn_, RMS_EPS)
        gate = _sigmoid(_dot(q.astype(bf16), wg_, NN))
        return h_ + gate * e, pe, gate

    return _rowcall(fn, name=name, rows=[h, p_i], fulls=[wp, _row2(pn), _row2(gn), wgate],
                    out_rows=[(D, f32), (D, f32), (D, f32)], tm=256)


def _ple_bwd(dh3, h, pe, gate, pn, gn, wgate, name, after=None):
    D = h.shape[1]

    def fn(d_, h_, pe_, gate_, pn_, gn_, wg_, *_):
        e = _rms(pe_, pn_, RMS_EPS)
        dzg = d_ * e * gate_ * (1.0 - gate_)
        dq = _dot(dzg.astype(bf16), wg_, NT)
        dxq, dgn = _rms_bwd(dq, h_, gn_, RMS_EPS)
        dpe, dpn = _rms_bwd(d_ * gate_, pe_, pn_, RMS_EPS)
        return d_ + dxq, dzg, dpe, _rms(h_, gn_, RMS_EPS), dpn, dgn

    return _rowcall(fn, name=name, rows=[dh3, h, pe, gate], fulls=[_row2(pn), _row2(gn), wgate] + ([] if after is None else [after]),
                    out_rows=[(D, f32), (D, bf16), (D, bf16), (D, bf16)], out_accs=[(1, D), (1, D)], tm=256)


def _loss_head(h, target, fn_w, name):
    D = h.shape[1]

    def fn(h_, t_, w_):
        diff = _rms(h_, w_, RMS_EPS) - t_
        loss = 0.5 * jnp.sum(jnp.mean(diff * diff, axis=-1, keepdims=True), axis=0, keepdims=True)
        dh, dw = _rms_bwd(diff * (1.0 / D), h_, w_, RMS_EPS)
        return dh, jnp.broadcast_to(loss, (1, LANES)), dw

    return _rowcall(fn, name=name, rows=[h, target], fulls=[_row2(fn_w)], out_rows=[(D, f32)], out_accs=[(1, LANES), (1, D)])


def _adamw(w, m, v, g, name):
    R, C = w.shape
    tr, tc = R, C
    while tr * tc > 256 * 1024 and tr % (2 * SUBLANES) == 0:
        tr //= 2
    while tr * tc > 256 * 1024 and tc % (2 * LANES) == 0:
        tc //= 2

    def body(w_ref, m_ref, v_ref, g_ref, d_ref, mo_ref, vo_ref):
        g = g_ref[...]
        mn = ADAM_B1 * m_ref[...] + (1.0 - ADAM_B1) * g
        vn = ADAM_B2 * v_ref[...] + (1.0 - ADAM_B2) * (g * g)
        m_hat = mn / (1.0 - ADAM_B1 ** ADAM_STEP)
        v_hat = vn / (1.0 - ADAM_B2 ** ADAM_STEP)
        d_ref[...] = -ADAM_LR * (m_hat / (jnp.sqrt(v_hat) + ADAM_EPS) + ADAM_WD * w_ref[...])
        mo_ref[...] = mn
        vo_ref[...] = vn

    spec = pl.BlockSpec((tr, tc), lambda i, j: (i, j))
    return pl.pallas_call(
        body, name=name, grid=(R // tr, C // tc), in_specs=[spec] * 4,
        out_specs=[spec] * 3, out_shape=[jax.ShapeDtypeStruct((R, C), f32)] * 3,
        compiler_params=_cparams(("parallel", "parallel")),
    )(w, m, v, g)


def _expand_onehot(n, per):
    lane = lax.broadcasted_iota(jnp.int32, (n, n * per), 1)
    row = lax.broadcasted_iota(jnp.int32, (n, n * per), 0)
    return (lane // per == row).astype(f32)


def _ssd_layer_fwd(h, nm_w, W, t):
    H = W["dt_bias"].shape[0]
    inner = H * HEADDIM
    G = (W["conv_b"].shape[0] - inner) // (2 * STATE)
    hn = _rms_fwd(h, nm_w, f"rms_mix_{t}")
    conv_dim = W["conv_b"].shape[0]
    wT = W["w_inT"]
    z = _mm(hn, wT, mode="nt", brows=(0, inner), name=f"ssd_z_{t}")
    xpre = _mm(hn, wT, mode="nt", brows=(inner, conv_dim), name=f"ssd_xbc_{t}")
    dt_pre = _mm(hn, wT, mode="nt", brows=(inner + conv_dim, H), name=f"ssd_dt_{t}")
    c, xbc = _conv_fwd(xpre, W["conv_w"], W["conv_b"], f"ssd_conv_{t}")
    _, dtx = _dt_fwd(dt_pre, W["dt_bias"], _expand_onehot(H, HEADDIM), f"ssd_dtx_{t}")
    a = -jnp.exp(W["a_log"])
    ax = _row2(jnp.repeat(a, HEADDIM))
    dx = _row2(jnp.repeat(W["d"], HEADDIM))
    y, states = _ssd_fwd(xbc, dtx, ax, dx, G, f"ssd_scan_{t}")
    yn = _gnorm_fwd(y, z, W["norm_w"], G, f"ssd_gnorm_{t}")
    h1 = _mm(yn, W["wout"], res=h, name=f"ssd_out_{t}")
    return h1, (h, hn, z, xpre, dt_pre, c, xbc, dtx, a, ax, dx, y, states, yn)


def _ssd_layer_bwd(dh1, saved, nm_w, W, t, after=None):
    h, hn, z, xpre, dt_pre, c, xbc, dtx, a, ax, dx, y, states, yn = saved
    H = W["dt_bias"].shape[0]
    inner = H * HEADDIM
    G = (W["conv_b"].shape[0] - inner) // (2 * STATE)
    dyn = _mm(dh1, W["wout"], mode="nt", after=after, name=f"ssd_out_dx_{t}")
    g_wout = _mm(yn, dh1, mode="tn", out_dtype=bf16, name=f"ssd_out_dw_{t}")
    dy, dz, g_normw = _gnorm_bwd(dyn, y, z, W["norm_w"], G, f"ssd_gnorm_bwd_{t}")
    dxbc, ddt, dax, ddx = _ssd_bwd(dy, y, xbc, dtx, ax, dx, states, _expand_onehot(H, HEADDIM).T, G, f"ssd_scan_bwd_{t}")
    dc, g_convw8, g_convb = _conv_bwd_dc(dxbc, c, xpre, f"ssd_conv_bwd_dc_{t}")
    dxpre = _conv_bwd_dx(dc, W["conv_w"], f"ssd_conv_bwd_dx_{t}")
    ddt_pre, g_dtb = _dt_bwd(ddt, dt_pre, W["dt_bias"], f"ssd_dt_bwd_{t}")
    conv_dim = W["conv_b"].shape[0]
    wT = W["w_inT"]
    g_wz = _mm(dz, hn, mode="tn", out_dtype=bf16, name=f"ssd_z_dw_{t}")
    g_wxbc = _mm(dxpre, hn, mode="tn", out_dtype=bf16, name=f"ssd_xbc_dw_{t}")
    g_wdt = _mm(ddt_pre, hn, mode="tn", out_dtype=bf16, name=f"ssd_dt_dw_{t}")
    dhn = _mm(dz, wT, brows=(0, inner), name=f"ssd_z_dx_{t}")
    dhn = _mm(dxpre, wT, brows=(inner, conv_dim), res=dhn, name=f"ssd_xbc_dx_{t}")
    dhn = _mm(ddt_pre, wT, brows=(inner + conv_dim, H), res=dhn, name=f"ssd_dt_dx_{t}")
    dh, g_nm = _rms_bwd_add(dh1, dhn, h, nm_w, f"rms_mix_bwd_{t}")
    grads = dict(
        w_inT=jnp.concatenate([g_wz, g_wxbc, g_wdt], axis=0), wout=g_wout,
        conv_w=g_convw8[:CONV_K], conv_b=g_convb[0], dt_bias=g_dtb[0],
        a_log=dax[0].reshape(H, HEADDIM)[:, 0] * a, d=jnp.sum(ddx[0].reshape(H, HEADDIM), axis=1),
        norm_w=g_normw[0], norm_mix=g_nm[0])
    return dh, grads


def _gmlp_layer_fwd(h, nm_w, W, t):
    NG, L, _ = W["w_s"].shape
    I = W["ln_w"].shape[0]
    hn = _rms_fwd(h, nm_w, f"rms_mix_{t}")
    pre = _mm(hn, W["win"], name=f"gmlp_in_{t}")
    bsx = jnp.repeat(W["b_s"].T, I // NG, axis=1)
    o = _gmlp_mid_fwd(pre, W["b_in"], W["ln_w"], W["ln_b"], W["w_s"], bsx, f"gmlp_mid_{t}")
    h1 = _mm(o, W["wout"], res=h, name=f"gmlp_out_{t}")
    return h1, (h, hn, pre, bsx, o)


def _gmlp_layer_bwd(dh1, saved, nm_w, W, t, after=None):
    h, hn, pre, bsx, o = saved
    NG = W["w_s"].shape[0]
    I = W["ln_w"].shape[0]
    do = _mm(dh1, W["wout"], mode="nt", after=after, name=f"gmlp_out_dx_{t}")
    g_wout = _mm(o, dh1, mode="tn", out_dtype=bf16, name=f"gmlp_out_dw_{t}")
    dpre, g_bin, g_lnw, g_lnb, g_ws, dbs = _gmlp_mid_bwd(do, pre, W["b_in"], W["ln_w"], W["ln_b"], W["w_s"], bsx, f"gmlp_mid_bwd_{t}")
    g_bs = _lane_group_sum(dbs, _expand_onehot(NG, I // NG).T, f"gmlp_bs_{t}").T
    g_win = _mm(hn, dpre, mode="tn", out_dtype=bf16, name=f"gmlp_in_dw_{t}")
    dhn = _mm(dpre, W["win"], mode="nt", name=f"gmlp_in_dx_{t}")
    dh, g_nm = _rms_bwd_add(dh1, dhn, h, nm_w, f"rms_mix_bwd_{t}")
    grads = dict(win=g_win, wout=g_wout, b_in=g_bin[0], ln_w=g_lnw[0], ln_b=g_lnb[0], w_s=g_ws, b_s=g_bs, norm_mix=g_nm[0])
    return dh, grads


def _ffn_fwd(h1, nf_w, W, t):
    h2, u, Gm, Um, A = _ffn_fwd_fused(h1, nf_w, W["wg"], W["wu"], W["wd"], f"ffn_fwd_{t}")
    return h2, (h1, u, Gm, Um, A)


def _ffn_bwd(dh2, saved, nf_w, W, t, after=None):
    h1, u, Gm, Um, A = saved
    dG, dU, dh1, g_nf = _ffn_bwd_fused(dh2, h1, nf_w, W["wd"], W["wg"], W["wu"], Gm, Um, f"ffn_bwd_{t}", after=after)
    g_wd = _mm(A, dh2, mode="tn", out_dtype=bf16, name=f"ffn_down_dw_{t}")
    g_wg = _mm(dG, u, mode="tn", out_dtype=bf16, name=f"ffn_gate_dw_{t}")
    g_wu = _mm(dU, u, mode="tn", out_dtype=bf16, name=f"ffn_up_dw_{t}")
    return dh1, dict(wg=g_wg, wu=g_wu, wd=g_wd, norm_ffn=g_nf[0])


def _local_step(x, p, target, norms, layer_weights, on_layer_grads=None, final_norm_grad=None):
    depth = p.shape[0]
    h = x
    saved = []
    for i in range(depth):
        Wm = layer_weights(i, "mix", h)
        if i % 2 == 0:
            h1, s_mix = _ssd_layer_fwd(h, norms["norm_mix"][i], Wm, i)
        else:
            h1, s_mix = _gmlp_layer_fwd(h, norms["norm_mix"][i], Wm, i)
        Wf = layer_weights(i, "ffn", h1)
        h2, s_ffn = _ffn_fwd(h1, norms["norm_ffn"][i], Wf, i)
        P = layer_weights(i, "ple", h2)
        h3, pe, gate = _ple_fwd(h2, p[i], P["wp"], P["pn"], P["gn"], P["wgate"], f"ple_{i}")
        saved.append((Wm, Wf, P, s_mix, s_ffn, (h2, pe, gate)))
        h = h3
    dh, loss, g_fn = _loss_head(h, target, norms["final_norm"], "loss_head")
    if final_norm_grad is not None:
        final_norm_grad[0] = g_fn[0]
    grads = [None] * depth
    tell = on_layer_grads if on_layer_grads is not None else (lambda i, part, g: None)
    after = None
    for i in reversed(range(depth)):
        Wm, Wf, P, s_mix, s_ffn, (h2, pe, gate) = saved[i]
        dh, dzg, dpe, q, g_pn, g_gn = _ple_bwd(dh, h2, pe, gate, P["pn"], P["gn"], P["wgate"], f"ple_bwd_{i}", after=after)
        g_ple = dict(wgate=_mm(q, dzg, mode="tn", out_dtype=bf16, name=f"ple_gate_dw_{i}"),
                     wp=_mm(p[i], dpe, mode="tn", out_dtype=bf16, name=f"ple_proj_dw_{i}"), pn=g_pn[0], gn=g_gn[0])
        after = tell(i, "ple", g_ple)
        dh, g_ffn = _ffn_bwd(dh, s_ffn, norms["norm_ffn"][i], Wf, i, after=after)
        after = tell(i, "ffn", g_ffn)
        if i % 2 == 0:
            dh, g_mix = _ssd_layer_bwd(dh, s_mix, norms["norm_mix"][i], Wm, i, after=after)
        else:
            dh, g_mix = _gmlp_layer_bwd(dh, s_mix, norms["norm_mix"][i], Wm, i, after=after)
        after = tell(i, "mix", g_mix)
        grads[i] = dict(mix=g_mix, ffn=g_ffn, ple=g_ple)
    return loss[0, 0], dh, g_fn[0], grads


def _flip(v, f):
    return 1 - v if f else v


_ANY = pl.BlockSpec(memory_space=pl.ANY)


_SEM = pl.BlockSpec(memory_space=pltpu.SEMAPHORE)
_DATAFLOW = pltpu.SideEffectType.DATAFLOW_SIDE_EFFECTING
_CHIP_FLIPS = ((1, 0), (0, 1), (1, 1))
_DMA = pltpu.SemaphoreType.DMA


def _structs(arrs):
    return [jax.ShapeDtypeStruct(a.shape, a.dtype) for a in arrs]


def _gather_copy(src, buf, send_sems, recv_sems, k, j, slot, x, y, c):
    c2 = src.shape[1] // 2
    fx, fy = _CHIP_FLIPS[j]
    nf = len(_CHIP_FLIPS)
    return pltpu.make_async_remote_copy(
        src_ref=src.at[:, pl.ds(c * c2, c2)], dst_ref=buf.at[slot, :, pl.ds(c * c2, c2)], send_sem=send_sems.at[nf * k + j],
        recv_sem=recv_sems.at[nf * k + j], device_id=(_flip(x, fx), _flip(y, fy), c), device_id_type=MESH)


def _gather_start(srcs, groups):
    n = len(srcs)
    ng = len(groups)
    nf = len(_CHIP_FLIPS)

    def body(*refs):
        src_refs, buf_refs, sems = refs[:n], refs[n:2 * n], refs[4 * n:]
        x, y, c = lax.axis_index("x"), lax.axis_index("y"), lax.axis_index("c")
        for gi, group in enumerate(groups):
            for k, o in enumerate(group):
                for j in range(nf):
                    _gather_copy(src_refs[o], buf_refs[o], sems[2 * gi], sems[2 * gi + 1], k, j, 2 * x + y, x, y, c).start()

    mychip = 2 * lax.axis_index("x") + lax.axis_index("y")
    inits = [lax.dynamic_update_slice(lax.empty((N_CHIPS,) + s.shape, s.dtype), s[None], (mychip, 0, 0)) for s in srcs]
    sem_shapes = [_DMA((nf * len(g),)) for g in groups for _ in range(2)]
    outs = pl.pallas_call(
        body, name="gather_start", in_specs=[_ANY] * (2 * n), out_specs=[_ANY] * (2 * n) + [_SEM] * (2 * ng),
        out_shape=_structs(srcs) + _structs(inits) + sem_shapes, input_output_aliases={i: i for i in range(2 * n)},
        compiler_params=pltpu.CompilerParams(has_side_effects=_DATAFLOW),
    )(*srcs, *inits)
    return outs[:n], outs[n:2 * n], [(outs[2 * n + 2 * gi], outs[2 * n + 2 * gi + 1]) for gi in range(ng)]


def _gather_wait(srcs, bufs, sems, after, name):
    n = len(srcs)
    nf = len(_CHIP_FLIPS)

    def body(*refs):
        src_refs, buf_refs, send_sems, recv_sems = refs[:n], refs[n:2 * n], refs[2 * n], refs[2 * n + 1]
        x, y, c = lax.axis_index("x"), lax.axis_index("y"), lax.axis_index("c")
        for k in range(n):
            for j, (fx, fy) in enumerate(_CHIP_FLIPS):
                cp = _gather_copy(src_refs[k], buf_refs[k], send_sems, recv_sems, k, j, 2 * _flip(x, fx) + _flip(y, fy), x, y, c)
                cp.wait_send()
                cp.wait_recv()

    outs = pl.pallas_call(
        body, name=name, in_specs=[_ANY] * (2 * n) + [_SEM, _SEM, _ANY], out_specs=[_ANY] * (2 * n),
        out_shape=_structs(srcs) + _structs(bufs), input_output_aliases={i: i for i in range(2 * n)},
        compiler_params=pltpu.CompilerParams(has_side_effects=_DATAFLOW),
    )(*srcs, *bufs, *sems, after)
    return outs[n:]


def _gather_forward(bufs, name):
    n = len(bufs)
    nf = len(_CHIP_FLIPS)

    def body(*refs):
        outs = refs[n:2 * n]
        send_sems, recv_sems = refs[2 * n:]
        x, y, c = lax.axis_index("x"), lax.axis_index("y"), lax.axis_index("c")

        def forward(k, j, h):
            c2 = bufs[k].shape[2] // 2
            fx, fy = _CHIP_FLIPS[j]
            part = outs[k].at[2 * _flip(x, fx) + _flip(y, fy), :, pl.ds(h * c2, c2)]
            return pltpu.make_async_remote_copy(src_ref=part, dst_ref=part, send_sem=send_sems.at[nf * k + j],
                                                recv_sem=recv_sems.at[nf * k + j], device_id=(x, y, 1 - c), device_id_type=MESH)

        sends = [forward(k, j, c) for k in range(n) for j in range(nf)]
        for cp in sends:
            cp.start()
        for k in range(n):
            for j in range(nf):
                forward(k, j, 1 - c).wait_recv()
        for cp in sends:
            cp.wait_send()

    return pl.pallas_call(
        body, name=name, in_specs=[_ANY] * n, out_specs=[_ANY] * n, out_shape=_structs(bufs),
        input_output_aliases={i: i for i in range(n)}, scratch_shapes=[_DMA((nf * n,)), _DMA((nf * n,))],
    )(*bufs)


def _half_struct(a, lead):
    return jax.ShapeDtypeStruct(lead + (a.shape[-2], a.shape[-1] // 2), a.dtype)


_DEVICE_FLIPS = tuple((f >> 2 & 1, f >> 1 & 1, f & 1) for f in range(1, N_DEV))


def _exchange_copy(srcs, lands, n, send_sems, recv_sems, i, j, slot, x, y, c):
    nf = len(_DEVICE_FLIPS)
    px, py, pc = (_flip(v, f) for v, f in zip((x, y, c), _DEVICE_FLIPS[j]))
    src = srcs[i]
    if i < n:
        c2 = src.shape[2] // 2
        src = src.at[2 * px + py, :, pl.ds(pc * c2, c2)]
    return pltpu.make_async_remote_copy(src_ref=src, dst_ref=lands[i].at[slot], send_sem=send_sems.at[nf * i + j],
                                        recv_sem=recv_sems.at[nf * i + j], device_id=(px, py, pc), device_id_type=MESH)


def _exchange_start(tensors, wholes, name):
    n, m = len(tensors), len(wholes)
    nf = len(_DEVICE_FLIPS)
    t = n + m
    land_structs = ([_half_struct(a, (N_DEV,)) for a in tensors] + [jax.ShapeDtypeStruct((N_DEV,) + w.shape, w.dtype) for w in wholes])

    def body(*refs):
        srcs, lands, send_sems, recv_sems, token = refs[:t], refs[2 * t:3 * t], refs[3 * t], refs[3 * t + 1], refs[3 * t + 2]
        x, y, c = lax.axis_index("x"), lax.axis_index("y"), lax.axis_index("c")
        for i in range(t):
            for j in range(nf):
                _exchange_copy(srcs, lands, n, send_sems, recv_sems, i, j, 4 * x + 2 * y + c, x, y, c).start()
        token[...] = jnp.zeros_like(token)

    outs = pl.pallas_call(
        body, name=name, in_specs=[_ANY] * t,
        out_specs=[_ANY] * (2 * t) + [_SEM, _SEM, pl.BlockSpec(memory_space=pltpu.VMEM)],
        out_shape=_structs(tensors) + _structs(wholes) + land_structs + [_DMA((nf * t,)), _DMA((nf * t,)),
                                                                          jax.ShapeDtypeStruct((SUBLANES, LANES), f32)],
        input_output_aliases={i: i for i in range(t)},
        compiler_params=pltpu.CompilerParams(has_side_effects=_DATAFLOW),
    )(*tensors, *wholes)
    return outs[:t], outs[t:2 * t], (outs[2 * t], outs[2 * t + 1]), outs[2 * t + 2]


def _exchange_wait(srcs, lands, n, sems, after, name):
    t = len(srcs)

    def body(*refs):
        src_refs, land_refs, send_sems, recv_sems = refs[:t], refs[t:2 * t], refs[2 * t], refs[2 * t + 1]
        x, y, c = lax.axis_index("x"), lax.axis_index("y"), lax.axis_index("c")
        for i in range(t):
            for j, (fx, fy, fc) in enumerate(_DEVICE_FLIPS):
                sender = 4 * _flip(x, fx) + 2 * _flip(y, fy) + _flip(c, fc)
                cp = _exchange_copy(src_refs, land_refs, n, send_sems, recv_sems, i, j, sender, x, y, c)
                cp.wait_send()
                cp.wait_recv()

    outs = pl.pallas_call(
        body, name=name, in_specs=[_ANY] * (2 * t) + [_SEM, _SEM, _ANY], out_specs=[_ANY] * (2 * t),
        out_shape=_structs(srcs) + _structs(lands), input_output_aliases={i: i for i in range(2 * t)},
        compiler_params=pltpu.CompilerParams(has_side_effects=_DATAFLOW),
    )(*srcs, *lands, *sems, after)
    return outs[:t], outs[t:]


def _sibling_join(bufs, name):
    flat = [(gi, l) for gi, b in enumerate(bufs) for l in range(b.shape[0])]
    n, n_buf = len(flat), len(bufs)

    def body(*refs):
        outs = refs[n_buf:2 * n_buf]
        send_sems, recv_sems = refs[2 * n_buf:]
        x, y, c = lax.axis_index("x"), lax.axis_index("y"), lax.axis_index("c")

        def push(i, h):
            gi, l = flat[i]
            c2 = bufs[gi].shape[2] // 2
            part = outs[gi].at[l, :, pl.ds(h * c2, c2)]
            return pltpu.make_async_remote_copy(src_ref=part, dst_ref=part, send_sem=send_sems.at[i], recv_sem=recv_sems.at[i],
                                                device_id=(x, y, 1 - c), device_id_type=MESH)

        sends = [push(i, c) for i in range(n)]
        for cp in sends:
            cp.start()
        for i in range(n):
            push(i, 1 - c).wait_recv()
        for cp in sends:
            cp.wait_send()

    dma = pltpu.SemaphoreType.DMA
    return pl.pallas_call(
        body, name=name, in_specs=[_ANY] * n_buf, out_specs=[_ANY] * n_buf,
        out_shape=[jax.ShapeDtypeStruct(b.shape, b.dtype) for b in bufs],
        input_output_aliases={i: i for i in range(n_buf)},
        scratch_shapes=[dma((n,)), dma((n,))],
    )(*bufs)


def _device_sum(landed, own, place, name, into=None, layer=0, layers=1):
    ndev, R, C2 = landed.shape
    tr, tc = R, C2
    while ndev * tr * tc > 1024 * 1024 and tr % (4 * SUBLANES) == 0:
        tr //= 2
    while ndev * tr * tc > 1024 * 1024 and tc % (2 * LANES) == 0:
        tc //= 2
    ncb = C2 // tc

    def body(*refs):
        place_ref, l_ref, m_ref, o_ref = refs[0], refs[1], refs[2], refs[-1]
        me = 2 * place_ref[0] + place_ref[1]
        s = jnp.where(me == 0, m_ref[...].astype(f32), l_ref[0].astype(f32))
        for d in range(1, ndev):
            s = s + jnp.where(me == d, m_ref[...].astype(f32), l_ref[d].astype(f32))
        o_ref[...] = s

    in_specs = [pl.BlockSpec((ndev, tr, tc), lambda i, j, pr: (0, i, j)),
                pl.BlockSpec((None, tr, tc), lambda i, j, pr: (pr[0], i, pr[1] * ncb + j))]
    args = [place, landed, own]
    if into is not None:
        in_specs.append(_ANY)
        args.append(into)
    return pl.pallas_call(
        body, name=name, out_shape=jax.ShapeDtypeStruct((layers, R, 2 * C2), f32),
        grid_spec=pltpu.PrefetchScalarGridSpec(
            num_scalar_prefetch=1, grid=(R // tr, ncb), in_specs=in_specs,
            out_specs=pl.BlockSpec((None, tr, tc), lambda i, j, pr: (layer, i, pr[1] * ncb + j))),
        input_output_aliases={3: 0} if into is not None else {},
        compiler_params=_cparams(("parallel", "parallel")),
    )(*args)


def _device_sum_whole(landed, own, place, name):
    ndev, R, C = landed.shape
    tr = R
    while ndev * tr * C > 1024 * 1024 and tr % (2 * SUBLANES) == 0:
        tr //= 2

    def body(place_ref, l_ref, m_ref, o_ref):
        me = 2 * place_ref[0] + place_ref[1]
        s = jnp.where(me == 0, m_ref[...], l_ref[0])
        for d in range(1, ndev):
            s = s + jnp.where(me == d, m_ref[...], l_ref[d])
        o_ref[...] = s

    return pl.pallas_call(
        body, name=name, out_shape=jax.ShapeDtypeStruct((R, C), f32),
        grid_spec=pltpu.PrefetchScalarGridSpec(
            num_scalar_prefetch=1, grid=(R // tr,),
            in_specs=[pl.BlockSpec((ndev, tr, C), lambda i, pr: (0, i, 0)), pl.BlockSpec((tr, C), lambda i, pr: (i, 0))],
            out_specs=pl.BlockSpec((tr, C), lambda i, pr: (i, 0))),
        compiler_params=_cparams(("parallel",)),
    )(place, landed, own)


PACK_COLS = 1024
PACK_ROW_MULTIPLE = 64

BIG = ("ssd_w_in", "ssd_w_out", "gmlp_w_in", "gmlp_w_out", "ffn_w_gate", "ffn_w_up", "ffn_w_down", "ple_w_proj", "ple_w_gate")
SMALL_SHARDED = ("ssd_conv_w", "gmlp_b_in", "gmlp_ln_w", "gmlp_ln_b")
REP_EARLY = ("gmlp_w_s", "gmlp_b_s")
REP_LATE = ("norm_mix", "norm_ffn", "ssd_conv_b", "ssd_dt_bias", "ssd_a_log", "ssd_d", "ssd_norm_w", "ple_norm", "ple_gate_norm",
            "final_norm")
WEIGHTS = ("norm_mix", "norm_ffn", "ssd_w_in", "ssd_conv_w", "ssd_conv_b", "ssd_dt_bias", "ssd_a_log", "ssd_d", "ssd_norm_w", "ssd_w_out",
           "gmlp_w_in", "gmlp_b_in", "gmlp_ln_w", "gmlp_ln_b", "gmlp_w_s", "gmlp_b_s", "gmlp_w_out", "ffn_w_gate", "ffn_w_up",
           "ffn_w_down", "ple_w_proj", "ple_norm", "ple_gate_norm", "ple_w_gate", "final_norm")
TRANSPOSED = ("ssd_w_in", "ffn_w_gate", "ffn_w_up")


def _pack(arrs):
    flat = jnp.concatenate([a.reshape(-1).astype(f32) for a in arrs])
    per = PACK_COLS * PACK_ROW_MULTIPLE
    n = -(-flat.shape[0] // per) * per
    return jnp.pad(flat, (0, n - flat.shape[0])).reshape(-1, PACK_COLS)


def _unpack(buf, shapes):
    flat = buf.reshape(-1)
    out, o = [], 0
    for s in shapes:
        n = math.prod(s)
        out.append(flat[o:o + n].reshape(s))
        o += n
    return out


def _chip_major(g):
    r, c4 = g.shape
    return g.reshape(r, N_CHIPS, c4 // N_CHIPS).transpose(1, 0, 2)


def _from_chip_major(g):
    k, r, c = g.shape
    return g.transpose(1, 0, 2).reshape(r, k * c)


def _adamw_nd(w, m, v, g, name):
    shp = w.shape
    two = lambda a: a.reshape(-1, shp[-1])
    return [o.reshape(shp) for o in _adamw(two(w), two(m), two(v), two(g), name)]


def kernel(x, p, norm_mix, norm_ffn, ssd_w_in, ssd_conv_w, ssd_conv_b, ssd_dt_bias, ssd_a_log, ssd_d, ssd_norm_w, ssd_w_out, gmlp_w_in, gmlp_b_in, gmlp_ln_w, gmlp_ln_b, gmlp_w_s, gmlp_b_s, gmlp_w_out, ffn_w_gate, ffn_w_up, ffn_w_down, ple_w_proj, ple_norm, ple_gate_norm, ple_w_gate, final_norm, loss_target, m_norm_mix, m_norm_ffn, m_ssd_w_in, m_ssd_conv_w, m_ssd_conv_b, m_ssd_dt_bias, m_ssd_a_log, m_ssd_d, m_ssd_norm_w, m_ssd_w_out, m_gmlp_w_in, m_gmlp_b_in, m_gmlp_ln_w, m_gmlp_ln_b, m_gmlp_w_s, m_gmlp_b_s, m_gmlp_w_out, m_ffn_w_gate, m_ffn_w_up, m_ffn_w_down, m_ple_w_proj, m_ple_norm, m_ple_gate_norm, m_ple_w_gate, m_final_norm, v_norm_mix, v_norm_ffn, v_ssd_w_in, v_ssd_conv_w, v_ssd_conv_b, v_ssd_dt_bias, v_ssd_a_log, v_ssd_d, v_ssd_norm_w, v_ssd_w_out, v_gmlp_w_in, v_gmlp_b_in, v_gmlp_ln_w, v_gmlp_ln_b, v_gmlp_w_s, v_gmlp_b_s, v_gmlp_w_out, v_ffn_w_gate, v_ffn_w_up, v_ffn_w_down, v_ple_w_proj, v_ple_norm, v_ple_gate_norm, v_ple_w_gate, v_final_norm):
    given = dict(locals())
    view = lambda n, a: jnp.swapaxes(a, 1, 2) if n in TRANSPOSED else a
    w = {n: view(n, given[n]) for n in WEIGHTS}
    mom = {n: view(n, given["m_" + n]) for n in WEIGHTS}
    var = {n: view(n, given["v_" + n]) for n in WEIGHTS}
    depth = p.shape[0]
    n_ssd, n_gmlp = ssd_w_in.shape[0], gmlp_w_in.shape[0]
    inner = ssd_dt_bias.shape[1] * HEADDIM
    conv_dim = ssd_conv_b.shape[1]

    place = jnp.stack([2 * lax.axis_index("x") + lax.axis_index("y"), lax.axis_index("c")]).astype(jnp.int32)

    def part_keys(i, part):
        j = i // 2
        if part == "mix":
            names = (("ssd_w_in", j), ("ssd_w_out", j)) if i % 2 == 0 else (("gmlp_w_in", j), ("gmlp_w_out", j))
            return ((("small", 0),) if i == 0 else ()) + names
        if part == "ffn":
            return (("ffn_w_gate", i), ("ffn_w_up", i), ("ffn_w_down", i))
        return (("ple_w_proj", i), ("ple_w_gate", i))

    parts = [(i, part) for i in range(depth) for part in ("mix", "ffn", "ple")]
    keys, groups = [], {}
    for ip in parts:
        names = part_keys(*ip)
        groups[ip] = list(range(len(keys), len(keys) + len(names)))
        keys += names
    small_shapes = [w[n].shape for n in SMALL_SHARDED]
    srcs = [_pack([w[n] for n in SMALL_SHARDED]) if n == "small" else w[n][l].astype(bf16) for n, l in keys]
    srcs, landing, gather_sems = _gather_start(srcs, [groups[ip] for ip in parts])
    gather_sems = dict(zip(parts, gather_sems))
    small_full = {}

    gw = {}

    def fetch(i, which, h):
        got = []
        for part in which:
            idx = groups[(i, part)]
            got += _gather_wait([srcs[o] for o in idx], [landing[o] for o in idx], gather_sems[(i, part)], h, f"gather_wait_{part}_{i}")
        names = [keys[o] for part in which for o in groups[(i, part)]]
        gw.update(dict(zip(names, _gather_forward(got, f"gather_forward_{which[0]}_{i}"))))

    def layer_weights(i, part, h):
        if i == 0:
            fetch(i, (part,), h)
        elif part == "mix":
            fetch(i, ("mix", "ffn", "ple"), h)
        rows = lambda a: a.reshape(-1, a.shape[-1])
        j = i // 2
        if part == "ffn":
            return dict(wg=gw[("ffn_w_gate", i)], wu=gw[("ffn_w_up", i)], wd=gw[("ffn_w_down", i)])
        if part == "ple":
            return dict(wp=_from_chip_major(gw[("ple_w_proj", i)]), pn=ple_norm[i], gn=ple_gate_norm[i], wgate=rows(gw[("ple_w_gate", i)]))
        if i == 0:
            by_chip = [_unpack(gw[("small", 0)][k], small_shapes) for k in range(N_CHIPS)]
            small_full.update({n: jnp.concatenate([by_chip[k][t] for k in range(N_CHIPS)], axis=-1) for t, n in enumerate(SMALL_SHARDED)})
        if i % 2 == 0:
            return dict(w_inT=rows(gw[("ssd_w_in", j)]),
                        conv_w=small_full["ssd_conv_w"][j], conv_b=ssd_conv_b[j], dt_bias=ssd_dt_bias[j], a_log=ssd_a_log[j],
                        d=ssd_d[j], norm_w=ssd_norm_w[j], wout=rows(gw[("ssd_w_out", j)]))
        return dict(win=_from_chip_major(gw[("gmlp_w_in", j)]), b_in=small_full["gmlp_b_in"][j], ln_w=small_full["gmlp_ln_w"][j],
                    ln_b=small_full["gmlp_ln_b"][j], w_s=gmlp_w_s[j], b_s=gmlp_b_s[j], wout=rows(gw[("gmlp_w_out", j)]))

    rows4 = lambda a: a.reshape((N_CHIPS, a.shape[0] // N_CHIPS) + a.shape[1:])
    cut = lambda a, k: a[..., k * (a.shape[-1] // N_CHIPS):(k + 1) * (a.shape[-1] // N_CHIPS)]
    layer_grads = {}
    in_flight = {}
    tokens = {}
    owns = {}

    def on_layer_grads(i, part, g):
        layer_grads[(i, part)] = g
        j = i // 2
        wholes = {}
        if part == "ffn":
            chunks = {("ffn_w_gate", i): g["wg"], ("ffn_w_up", i): g["wu"], ("ffn_w_down", i): g["wd"]}
        elif part == "ple":
            chunks = {("ple_w_proj", i): _chip_major(g["wp"]), ("ple_w_gate", i): rows4(g["wgate"])}
        elif i % 2 == 0:
            chunks = {("ssd_w_in", j): rows4(g["w_inT"]), ("ssd_w_out", j): rows4(g["wout"])}
        else:
            chunks = {("gmlp_w_in", j): _chip_major(g["win"]), ("gmlp_w_out", j): rows4(g["wout"])}
        stack = lambda prt, key, layers: jnp.stack([layer_grads[(l, prt)][key] for l in layers])
        ssd, gml, every = range(0, depth, 2), range(1, depth, 2), range(depth)
        if part == "mix" and i == 1:
            wholes["rep_early"] = _pack([stack("mix", "w_s", gml), stack("mix", "b_s", gml)])
        if part == "mix" and i == 0:
            small_g = dict(ssd_conv_w=stack("mix", "conv_w", ssd), gmlp_b_in=stack("mix", "b_in", gml),
                           gmlp_ln_w=stack("mix", "ln_w", gml), gmlp_ln_b=stack("mix", "ln_b", gml))
            chunks[("small", 0)] = jnp.stack([_pack([cut(small_g[n], k) for n in SMALL_SHARDED]) for k in range(N_CHIPS)])
            rep_g = dict(
                norm_mix=stack("mix", "norm_mix", every), norm_ffn=stack("ffn", "norm_ffn", every),
                ssd_conv_b=stack("mix", "conv_b", ssd), ssd_dt_bias=stack("mix", "dt_bias", ssd), ssd_a_log=stack("mix", "a_log", ssd),
                ssd_d=stack("mix", "d", ssd), ssd_norm_w=stack("mix", "norm_w", ssd), ple_norm=stack("ple", "pn", every),
                ple_gate_norm=stack("ple", "gn", every), final_norm=final_norm_grad[0])
            wholes["rep_late"] = _pack([rep_g[n] for n in REP_LATE])
        ks, wk = list(chunks), list(wholes)
        thru, lands, sems, token = _exchange_start([chunks[k] for k in ks], [wholes[k] for k in wk], f"grads_exchange_start_{part}_{i}")
        in_flight[(i, part)] = (ks, wk, thru, lands, sems)
        tokens[(i, part)] = token
        return token

    final_norm_grad = [None]
    norms = dict(norm_mix=norm_mix, norm_ffn=norm_ffn, final_norm=final_norm)
    loss_part, grad_x, g_fn, _ = _local_step(x[0], p[:, 0], loss_target[0], norms, layer_weights, on_layer_grads, final_norm_grad)
    loss = lax.psum(loss_part, ("x", "y", "c"))

    landed, res = {}, {}

    def wait_for(which, after):
        for i, part in which:
            ks, wk, thru, lands, sems = in_flight[(i, part)]
            thru, lands = _exchange_wait(thru, lands, len(ks), sems, after, f"grads_exchange_wait_{part}_{i}")
            landed.update(dict(zip(ks + wk, lands)))
            owns.update(dict(zip(ks + wk, thru)))

    def packed_update(names, gsum, tag):
        packs = [gsum] + list(_adamw(_pack([w[n] for n in names]), _pack([mom[n] for n in names]), _pack([var[n] for n in names]), gsum, tag))
        per_kind = [_unpack(pk, [w[n].shape for n in names]) for pk in packs]
        for t, n in enumerate(names):
            res[n] = [per_kind[k][t] for k in range(4)]

    def finish(big_names, with_small, rep_key, rep_names, tag):
        bufs = []
        for n in big_names + (("small",) if with_small else ()):
            layers = w[n].shape[0] if n != "small" else 1
            buf = None
            for l in range(layers):
                buf = _device_sum(landed[(n, l)], owns[(n, l)], place, f"grads_sum_{n}_{l}", into=buf, layer=l, layers=layers)
            bufs.append(buf)
        reduced = _sibling_join(bufs, f"grads_sibling_join_{tag}")
        for n, gsum in zip(big_names, reduced):
            res[n] = [view(n, a) for a in [gsum] + _adamw_nd(w[n], mom[n], var[n], gsum, "adamw_" + n)]
        if with_small:
            packed_update(SMALL_SHARDED, reduced[-1][0], "adamw_small_sharded")
        packed_update(rep_names, _device_sum_whole(landed[rep_key], owns[rep_key], place, f"grads_sum_{rep_key}"), f"adamw_{rep_key}")

    last = (0, "mix")
    late_big = tuple(n for n in BIG if n.startswith("ssd_"))
    early_big = tuple(n for n in BIG if n not in late_big)
    wait_for([ip for ip in reversed(parts) if ip != last], tokens[last])
    finish(early_big, False, "rep_early", REP_EARLY, "early")
    wait_for([last], res[early_big[-1]][1])
    finish(late_big, True, "rep_late", REP_LATE, "late")
    return (loss, grad_x[None], *[res[n][0] for n in WEIGHTS], *[res[n][1] for n in WEIGHTS],
            *[res[n][2] for n in WEIGHTS], *[res[n][3] for n in WEIGHTS])
```

```python
import math

import jax
import jax.numpy as jnp
from jax import lax
from jax.experimental import pallas as pl
from jax.experimental.pallas import tpu as pltpu

f32 = jnp.float32
bf16 = jnp.bfloat16
HI = lax.Precision.HIGHEST

LANES = 128
SUBLANES = 8
VMEM_LIMIT_BYTES = 56 * 1024 * 1024

HEADDIM = 64
STATE = 128
CHUNK = 128
CONV_K = 4
RMS_EPS = 1e-6
LN_EPS = 1e-5
ADAM_LR = 0.001
ADAM_B1 = 0.9
ADAM_B2 = 0.999
ADAM_EPS = 1e-08
ADAM_WD = 0.01
ADAM_STEP = 10

FFN_SUBTILES = 2

N_CHIPS = 4
N_DEV = 8
MESH = pl.DeviceIdType.MESH


def _cparams(sem):
    return pltpu.CompilerParams(dimension_semantics=sem, vmem_limit_bytes=VMEM_LIMIT_BYTES)


def _tile(n, want):
    if n <= want:
        return n
    t = want
    while n % t:
        t //= 2
    return t


def _row_spec(tm, c):
    return pl.BlockSpec((tm, c), lambda i: (i, 0))


def _full_spec(shape):
    nd = len(shape)
    return pl.BlockSpec(tuple(shape), lambda *_: (0,) * nd)


def _sigmoid(x):
    return 1.0 / (1.0 + jnp.exp(-x))


def _silu(x):
    return x * _sigmoid(x)


def _dsilu(x):
    s = _sigmoid(x)
    return s * (1.0 + x * (1.0 - s))


def _gelu(x):
    return 0.5 * x * (1.0 + lax.erf(x * (1.0 / math.sqrt(2.0))))


def _dgelu(x):
    return 0.5 * (1.0 + lax.erf(x * (1.0 / math.sqrt(2.0)))) + x * jnp.exp(-0.5 * x * x) * (1.0 / math.sqrt(2.0 * math.pi))


def _softplus(x):
    return jnp.maximum(x, 0.0) + jnp.log(1.0 + jnp.exp(-jnp.abs(x)))


def _rms(x, w, eps):
    r = lax.rsqrt(jnp.mean(x * x, axis=-1, keepdims=True) + eps)
    return x * r * w


def _rms_bwd(dy, x, w, eps):
    r = lax.rsqrt(jnp.mean(x * x, axis=-1, keepdims=True) + eps)
    xh = x * r
    g = dy * w
    dx = r * (g - xh * jnp.mean(g * xh, axis=-1, keepdims=True))
    dw = jnp.sum(dy * xh, axis=0, keepdims=True)
    return dx, dw


def _dot(a, b, dims=(((1,), (0,)), ((), ())), precision=None):
    return lax.dot_general(a, b, dims, precision=precision, preferred_element_type=f32)


NN = (((1,), (0,)), ((), ()))
NT = (((1,), (1,)), ((), ()))
TN = (((0,), (0,)), ((), ()))


def _split3(x):
    hi = x.astype(bf16)
    r1 = x - hi.astype(f32)
    mid = r1.astype(bf16)
    return hi, mid, (r1 - mid.astype(f32)).astype(bf16)


def _dot01_left(m01, x):
    mb = m01.astype(bf16)
    hi, mid, lo = _split3(x)
    return _dot(mb, hi, NN) + _dot(mb, mid, NN) + _dot(mb, lo, NN)


def _dot01_right(x, m01):
    mb = m01.astype(bf16)
    hi, mid, lo = _split3(x)
    return _dot(hi, mb, NN) + _dot(mid, mb, NN) + _dot(lo, mb, NN)


def _mm(a, b, *, mode="nn", out_dtype=f32, res=None, kbatch=False, brows=None, after=None, tm=1024, tn=1024, tk=1024, name):
    a3, b3 = a.ndim == 3, b.ndim == 3
    nb = a.shape[0] if a3 else (b.shape[0] if b3 else 1)
    ash, bsh = a.shape[-2:], b.shape[-2:]
    if brows is not None:
        bsh = (brows[1], bsh[1])
    if mode == "nn":
        M, K, N = ash[0], ash[1], bsh[1]
    elif mode == "nt":
        M, K, N = ash[0], ash[1], bsh[0]
    else:
        K, M, N = ash[0], ash[1], bsh[1]
    tm, tn, tk = _tile(M, tm), (N if N % LANES else _tile(N, tn)), (K if K % LANES else _tile(K, tk))
    b0 = 0
    if brows is not None:
        assert mode in ("nn", "nt") and bsh[0] == (K if mode == "nn" else N)
        blk = tk if mode == "nn" else tn
        while brows[0] % blk:
            blk //= 2
        assert blk % LANES == 0 or blk == brows[1]
        b0 = brows[0] // blk
        tn, tk = (tn, blk) if mode == "nn" else (blk, tk)
    nk = K // tk
    if kbatch:
        assert a3 and b3
        grid = (1, M // tm, N // tn, nb * nk)
        bi = lambda g, k: k // nk
        ki = lambda g, k: k % nk
    else:
        grid = (nb, M // tm, N // tn, nk)
        bi = lambda g, k: g
        ki = lambda g, k: k
    nsteps = grid[3]

    def spec(is3, blk, imap):
        if is3:
            return pl.BlockSpec((None,) + blk, lambda g, i, j, k: (bi(g, k),) + imap(i, j, ki(g, k)))
        return pl.BlockSpec(blk, lambda g, i, j, k: imap(i, j, ki(g, k)))

    if mode == "nn":
        a_spec = spec(a3, (tm, tk), lambda i, j, k: (i, k))
        b_spec = spec(b3, (tk, tn), lambda i, j, k: (k + b0, j))
        dims = NN
    elif mode == "nt":
        a_spec = spec(a3, (tm, tk), lambda i, j, k: (i, k))
        b_spec = spec(b3, (tn, tk), lambda i, j, k: (j + b0, k))
        dims = NT
    else:
        a_spec = spec(a3, (tk, tm), lambda i, j, k: (k, i))
        b_spec = spec(b3, (tk, tn), lambda i, j, k: (k, j))
        dims = TN
    out3 = (a3 or b3) and not kbatch
    if out3:
        o_spec = pl.BlockSpec((None, tm, tn), lambda g, i, j, k: (g, i, j))
        o_shape = jax.ShapeDtypeStruct((nb, M, N), out_dtype)
    else:
        o_spec = pl.BlockSpec((tm, tn), lambda g, i, j, k: (i, j))
        o_shape = jax.ShapeDtypeStruct((M, N), out_dtype)
    in_specs = [a_spec, b_spec]
    args = [a, b]
    if res is not None:
        in_specs.append(pl.BlockSpec((tm, tn), lambda g, i, j, k: (i, j)))
        args.append(res)
    if after is not None:
        in_specs.append(pl.BlockSpec(memory_space=pl.ANY))
        args.append(after)

    def body(*refs):
        a_ref, b_ref = refs[:2]
        r_ref = refs[2] if res is not None else None
        o_ref, acc_ref = refs[-2:]
        k = pl.program_id(3)

        @pl.when(k == 0)
        def _():
            acc_ref[...] = jnp.zeros_like(acc_ref)

        acc_ref[...] += _dot(a_ref[...].astype(bf16), b_ref[...].astype(bf16), dims)

        @pl.when(k == nsteps - 1)
        def _():
            r = acc_ref[...]
            if res is not None:
                r = r + r_ref[...]
            o_ref[...] = r.astype(o_ref.dtype)

    return pl.pallas_call(
        body, name=name, grid=grid, in_specs=in_specs, out_specs=o_spec, out_shape=o_shape,
        scratch_shapes=[pltpu.VMEM((tm, tn), f32)],
        compiler_params=_cparams(("parallel", "parallel", "parallel", "arbitrary")),
    )(*args)


def _rowcall(fn, *, name, rows, fulls, out_rows, out_accs=(), tm=512):
    S = rows[0].shape[0]
    tm = _tile(S, tm)
    n_r, n_f, n_or, n_oa = len(rows), len(fulls), len(out_rows), len(out_accs)

    def body(*refs):
        ins = [r[...] for r in refs[:n_r + n_f]]
        outs = fn(*ins)
        if not isinstance(outs, (tuple, list)):
            outs = (outs,)
        o_refs = refs[n_r + n_f:]
        for o_ref, v in zip(o_refs[:n_or], outs[:n_or]):
            o_ref[...] = v.astype(o_ref.dtype)
        if n_oa:
            first = pl.program_id(0) == 0

            @pl.when(first)
            def _():
                for o_ref, v in zip(o_refs[n_or:], outs[n_or:]):
                    o_ref[...] = v

            @pl.when(jnp.logical_not(first))
            def _():
                for o_ref, v in zip(o_refs[n_or:], outs[n_or:]):
                    o_ref[...] += v

    in_specs = [_row_spec(tm, r.shape[1]) for r in rows] + [_full_spec(f.shape) for f in fulls]
    out_specs = [_row_spec(tm, c) for c, _ in out_rows] + [_full_spec(s) for s in out_accs]
    out_shape = [jax.ShapeDtypeStruct((S, c), d) for c, d in out_rows] + [jax.ShapeDtypeStruct(s, f32) for s in out_accs]
    res = pl.pallas_call(
        body, name=name, grid=(S // tm,), in_specs=in_specs, out_specs=out_specs, out_shape=out_shape,
        compiler_params=_cparams(("arbitrary",) if n_oa else ("parallel",)),
    )(*rows, *fulls)
    return res


def _row2(v):
    return v.reshape(1, -1)


def _rms_fwd(h, w, name):
    D = h.shape[1]
    return _rowcall(lambda x, w_: _rms(x, w_, RMS_EPS), name=name, rows=[h], fulls=[_row2(w)], out_rows=[(D, bf16)])[0]


def _conv_fwd(xpre, w, b, name):
    S, C = xpre.shape
    tm, tc = _tile(S, 512), _tile(C, 1024)
    hb = tm // SUBLANES

    def body(x_ref, halo_ref, w_ref, b_ref, c_ref, o_ref):
        i = pl.program_id(1)
        x = x_ref[...]
        halo = jnp.where(i > 0, halo_ref[...], 0.0)
        row = lax.broadcasted_iota(jnp.int32, x.shape, 0)
        row8 = lax.broadcasted_iota(jnp.int32, halo.shape, 0)
        x0 = x[0:SUBLANES, :]
        acc = x * w_ref[CONV_K - 1:CONV_K, :] + b_ref[...]
        acc0 = x0 * w_ref[CONV_K - 1:CONV_K, :] + b_ref[...]
        for k in range(1, CONV_K):
            wk = w_ref[CONV_K - 1 - k:CONV_K - k, :]
            acc = acc + pltpu.roll(x, k, axis=0) * wk
            acc0 = acc0 + jnp.where(row8 < k, pltpu.roll(halo, k, axis=0), pltpu.roll(x0, k, axis=0)) * wk
        c_ref[...] = acc
        o_ref[...] = _silu(acc)
        c_ref[0:SUBLANES, :] = acc0
        o_ref[0:SUBLANES, :] = _silu(acc0)

    return pl.pallas_call(
        body, name=name, grid=(C // tc, S // tm),
        in_specs=[pl.BlockSpec((tm, tc), lambda j, i: (i, j)),
                  pl.BlockSpec((SUBLANES, tc), lambda j, i: (jnp.maximum(i * hb - 1, 0), j)),
                  pl.BlockSpec((CONV_K, tc), lambda j, i: (0, j)),
                  pl.BlockSpec((1, tc), lambda j, i: (0, j))],
        out_specs=[pl.BlockSpec((tm, tc), lambda j, i: (i, j))] * 2,
        out_shape=[jax.ShapeDtypeStruct((S, C), f32)] * 2,
        compiler_params=_cparams(("parallel", "parallel")),
    )(xpre, xpre, w, _row2(b))


def _conv_bwd_dc(dxbc, c, xpre, name):
    S, C = xpre.shape
    tm, tc = _tile(S, 512), _tile(C, 1024)
    hb = tm // SUBLANES

    def body(d_ref, c_ref, x_ref, halo_ref, dc_ref, dw_ref, db_ref):
        i = pl.program_id(1)
        x = x_ref[...]
        dc = d_ref[...] * _dsilu(c_ref[...])
        dc_ref[...] = dc
        halo = jnp.where(i > 0, halo_ref[...], 0.0)
        row = lax.broadcasted_iota(jnp.int32, x.shape, 0)
        row8 = lax.broadcasted_iota(jnp.int32, halo.shape, 0)
        x0 = x[0:SUBLANES, :]
        dc0 = dc[0:SUBLANES, :]
        parts = [jnp.sum(dc * x, axis=0, keepdims=True)]
        for k in range(1, CONV_K):
            xs_big = jnp.where(row < SUBLANES, 0.0, pltpu.roll(x, k, axis=0))
            xs0 = jnp.where(row8 < k, pltpu.roll(halo, k, axis=0), pltpu.roll(x0, k, axis=0))
            parts.append(jnp.sum(dc * xs_big, axis=0, keepdims=True) + jnp.sum(dc0 * xs0, axis=0, keepdims=True))
        dw = jnp.concatenate([parts[CONV_K - 1 - k] for k in range(CONV_K)] + [jnp.zeros((SUBLANES - CONV_K, x.shape[1]), f32)], axis=0)
        db = jnp.sum(dc, axis=0, keepdims=True)

        @pl.when(i == 0)
        def _():
            dw_ref[...] = dw
            db_ref[...] = db

        @pl.when(i > 0)
        def _():
            dw_ref[...] += dw
            db_ref[...] += db

    return pl.pallas_call(
        body, name=name, grid=(C // tc, S // tm),
        in_specs=[pl.BlockSpec((tm, tc), lambda j, i: (i, j))] * 3 +
                 [pl.BlockSpec((SUBLANES, tc), lambda j, i: (jnp.maximum(i * hb - 1, 0), j))],
        out_specs=[pl.BlockSpec((tm, tc), lambda j, i: (i, j)),
                   pl.BlockSpec((SUBLANES, tc), lambda j, i: (0, j)),
                   pl.BlockSpec((1, tc), lambda j, i: (0, j))],
        out_shape=[jax.ShapeDtypeStruct((S, C), f32), jax.ShapeDtypeStruct((SUBLANES, C), f32), jax.ShapeDtypeStruct((1, C), f32)],
        compiler_params=_cparams(("parallel", "arbitrary")),
    )(dxbc, c, xpre, xpre)


def _conv_bwd_dx(dc, w, name):
    S, C = dc.shape
    tm, tc = _tile(S, 512), _tile(C, 1024)
    hb = tm // SUBLANES
    nrow = S // tm
    last8 = S // SUBLANES - 1

    def body(d_ref, nxt_ref, w_ref, o_ref):
        i = pl.program_id(1)
        d = d_ref[...]
        nxt = jnp.where(i < nrow - 1, nxt_ref[...], 0.0)
        row8 = lax.broadcasted_iota(jnp.int32, nxt.shape, 0)
        dl = d[tm - SUBLANES:tm, :]
        acc = d * w_ref[CONV_K - 1:CONV_K, :]
        accl = dl * w_ref[CONV_K - 1:CONV_K, :]
        for j in range(1, CONV_K):
            wk = w_ref[CONV_K - 1 - j:CONV_K - j, :]
            acc = acc + pltpu.roll(d, tm - j, axis=0) * wk
            accl = accl + jnp.where(row8 >= SUBLANES - j, pltpu.roll(nxt, SUBLANES - j, axis=0), pltpu.roll(dl, SUBLANES - j, axis=0)) * wk
        o_ref[...] = acc.astype(o_ref.dtype)
        o_ref[tm - SUBLANES:tm, :] = accl.astype(o_ref.dtype)

    return pl.pallas_call(
        body, name=name, grid=(C // tc, nrow),
        in_specs=[pl.BlockSpec((tm, tc), lambda j, i: (i, j)),
                  pl.BlockSpec((SUBLANES, tc), lambda j, i: (jnp.minimum((i + 1) * hb, last8), j)),
                  pl.BlockSpec((CONV_K, tc), lambda j, i: (0, j))],
        out_specs=pl.BlockSpec((tm, tc), lambda j, i: (i, j)),
        out_shape=jax.ShapeDtypeStruct((S, C), f32),
        compiler_params=_cparams(("parallel", "parallel")),
    )(dc, dc, w)


def _halfsum(v, lane_lo):
    s0 = jnp.sum(jnp.where(lane_lo, v, 0.0), axis=1, keepdims=True)
    s1 = jnp.sum(jnp.where(lane_lo, 0.0, v), axis=1, keepdims=True)
    return jnp.where(lane_lo, s0, s1)


def _ssd_specs(S, inner, GN, nchunks, rev):
    L = CHUNK
    cm = (lambda c: nchunks - 1 - c) if rev else (lambda c: c)
    xs = pl.BlockSpec((L, inner), lambda c: (cm(c), 0))
    bb = pl.BlockSpec((L, GN), lambda c: (cm(c), inner // GN))
    cc = pl.BlockSpec((L, GN), lambda c: (cm(c), inner // GN + 1))
    row = pl.BlockSpec((L, inner), lambda c: (cm(c), 0))
    vec = pl.BlockSpec((1, inner), lambda c: (0, 0))
    st = pl.BlockSpec((None, inner, STATE), lambda c: (cm(c), 0, 0))
    return xs, bb, cc, row, vec, st


def _ssd_fwd(xbc, dtx, ax, dx, G, name):
    S, inner = dtx.shape
    GN = G * STATE
    L = CHUNK
    nchunks = S // L
    npairs = inner // LANES
    ppg = npairs // G
    assert inner % GN == 0 and L == LANES and STATE == LANES

    def body(xs_ref, b_ref, c_ref, dtx_ref, ax_ref, dx_ref, y_ref, so_ref, st_ref):
        ci = pl.program_id(0)

        @pl.when(ci == 0)
        def _():
            st_ref[...] = jnp.zeros_like(st_ref)

        r = lax.broadcasted_iota(jnp.int32, (L, L), 0)
        cidx = lax.broadcasted_iota(jnp.int32, (L, L), 1)
        tril = cidx <= r
        lane_lo = cidx < HEADDIM
        xs = xs_ref[...]
        dtv = dtx_ref[...]
        X = xs * dtv
        da = dtv * ax_ref[...]
        cs = _dot01_left(tril, da)
        cs_last = jnp.sum(da, axis=0, keepdims=True)
        so_ref[...] = st_ref[...]
        for g in range(G):
            Bg = b_ref[:, g * STATE:(g + 1) * STATE].astype(bf16)
            Cg = c_ref[:, g * STATE:(g + 1) * STATE].astype(bf16)
            CB = _dot(Cg, Bg, NT)
            for j in range(ppg):
                lo = (g * ppg + j) * LANES
                tile = cs[:, lo:lo + LANES]
                rl = pltpu.roll(tile, HEADDIM, axis=1)
                Xp = X[:, lo:lo + LANES]
                prev = st_ref[lo:lo + LANES, :]
                ypair = _dot(Cg, prev.astype(bf16), NT) * jnp.exp(tile)
                for half in (0, 1):
                    hm = lane_lo if half == 0 else jnp.logical_not(lane_lo)
                    colb = jnp.where(hm, tile, rl)
                    Lm = jnp.exp(jnp.where(tril, colb - colb.T, -1e30))
                    W = (CB * Lm).astype(bf16)
                    ypair = ypair + _dot(W, jnp.where(hm, Xp, 0.0).astype(bf16), NN)
                y_ref[:, lo:lo + LANES] = ypair + xs[:, lo:lo + LANES] * dx_ref[:, lo:lo + LANES]
                last = cs_last[:, lo:lo + LANES]
                snew = _dot((Xp * jnp.exp(last - tile)).astype(bf16), Bg, TN)
                dec_rows = jnp.broadcast_to(jnp.exp(last), (L, LANES)).T
                st_ref[lo:lo + LANES, :] = dec_rows * prev + snew

    xs_s, b_s, c_s, row_s, vec_s, st_s = _ssd_specs(S, inner, GN, nchunks, False)
    return pl.pallas_call(
        body, name=name, grid=(nchunks,),
        in_specs=[xs_s, b_s, c_s, row_s, vec_s, vec_s],
        out_specs=[row_s, st_s],
        out_shape=[jax.ShapeDtypeStruct((S, inner), f32), jax.ShapeDtypeStruct((nchunks, inner, STATE), f32)],
        scratch_shapes=[pltpu.VMEM((inner, STATE), f32)],
        compiler_params=_cparams(("arbitrary",)),
    )(xbc, xbc, xbc, dtx, ax, dx)


def _ssd_bwd(dy, y, xbc, dtx, ax, dx, states, et, G, name):
    S, inner = dtx.shape
    H = et.shape[1]
    GN = G * STATE
    Cc = inner + 2 * GN
    L = CHUNK
    nchunks = S // L
    npairs = inner // LANES
    ppg = npairs // G

    def body(dy_ref, y_ref, xs_ref, b_ref, c_ref, dtx_ref, ax_ref, dx_ref, si_ref, et_ref,
             dxbc_ref, ddt_ref, dax_ref, ddx_ref, dst_ref, dA_ref, dAl_ref, ddtp_ref):
        ci = pl.program_id(0)

        @pl.when(ci == 0)
        def _():
            dst_ref[...] = jnp.zeros_like(dst_ref)
            dax_ref[...] = jnp.zeros_like(dax_ref)
            ddx_ref[...] = jnp.zeros_like(ddx_ref)

        r = lax.broadcasted_iota(jnp.int32, (L, L), 0)
        cidx = lax.broadcasted_iota(jnp.int32, (L, L), 1)
        tril = cidx <= r
        lane_lo = cidx < HEADDIM
        lane_lo1 = lax.broadcasted_iota(jnp.int32, (1, LANES), 1) < HEADDIM
        xs = xs_ref[...]
        dtv = dtx_ref[...]
        dyv = dy_ref[...]
        X = xs * dtv
        da = dtv * ax_ref[...]
        cs = _dot01_left(tril, da)
        cs_last = jnp.sum(da, axis=0, keepdims=True)
        for g in range(G):
            Bg = b_ref[:, g * STATE:(g + 1) * STATE].astype(bf16)
            Cg = c_ref[:, g * STATE:(g + 1) * STATE].astype(bf16)
            CB = _dot(Cg, Bg, NT)
            dCB = jnp.zeros((L, L), f32)
            dBg = jnp.zeros((L, STATE), f32)
            dCg = jnp.zeros((L, STATE), f32)
            for j in range(ppg):
                lo = (g * ppg + j) * LANES
                tile = cs[:, lo:lo + LANES]
                rl = pltpu.roll(tile, HEADDIM, axis=1)
                eA = jnp.exp(tile)
                Xp = X[:, lo:lo + LANES]
                dYp = dyv[:, lo:lo + LANES]
                xsp = xs[:, lo:lo + LANES]
                prev = si_ref[lo:lo + LANES, :]
                dSn = dst_ref[lo:lo + LANES, :]
                prev_b = prev.astype(bf16)
                dSn_b = dSn.astype(bf16)
                dYe = (dYp * eA).astype(bf16)
                dCg = dCg + _dot(dYe, prev_b, NN)
                dprev = _dot(dYe, Cg, TN)
                last = cs_last[:, lo:lo + LANES]
                w = jnp.exp(last - tile)
                BdS = _dot(Bg, dSn_b, NT)
                Xw = Xp * w
                XwB = Xw * BdS
                dAl_t = _halfsum(jnp.sum(XwB, axis=0, keepdims=True), lane_lo1)
                dBg = dBg + _dot(Xw.astype(bf16), dSn_b, NN)
                dec_rows = jnp.broadcast_to(jnp.exp(last), (L, LANES)).T
                dprev = dprev + dec_rows * dSn
                rsum = jnp.sum(dSn * prev * dec_rows, axis=1, keepdims=True)
                s0 = jnp.sum(rsum[0:HEADDIM], axis=0, keepdims=True)
                s1 = jnp.sum(rsum[HEADDIM:LANES], axis=0, keepdims=True)
                dAl_t = dAl_t + jnp.where(lane_lo1, s0, s1)
                dXd = jnp.zeros((L, LANES), f32)
                for half in (0, 1):
                    hm = lane_lo if half == 0 else jnp.logical_not(lane_lo)
                    colb = jnp.where(hm, tile, rl)
                    Lm = jnp.exp(jnp.where(tril, colb - colb.T, -1e30))
                    dYh = jnp.where(hm, dYp, 0.0).astype(bf16)
                    dW = _dot(dYh, jnp.where(hm, Xp, 0.0).astype(bf16), NT)
                    dXd = dXd + _dot((CB * Lm).astype(bf16), dYh, TN)
                    dCB = dCB + dW * Lm
                yoff = _dot(Cg, prev_b, NT) * eA
                ydiag = y_ref[:, lo:lo + LANES] - xsp * dx_ref[:, lo:lo + LANES] - yoff
                dYb = dYp.astype(bf16).astype(f32)
                Xb = Xp.astype(bf16).astype(f32)
                dA_t = _halfsum(dYb * ydiag - Xb * dXd + dYp * yoff - XwB, lane_lo)
                dXp = w * BdS + dXd
                dxbc_ref[:, lo:lo + LANES] = dXp * dtv[:, lo:lo + LANES] + dYp * dx_ref[:, lo:lo + LANES]
                ddtp_ref[:, lo:lo + LANES] = dXp * xsp
                ddx_ref[:, lo:lo + LANES] += jnp.sum(dYp * xsp, axis=0, keepdims=True)
                dA_ref[:, lo:lo + LANES] = dA_t
                dAl_ref[:, lo:lo + LANES] = dAl_t
                dst_ref[lo:lo + LANES, :] = dprev
            dCBb = dCB.astype(bf16)
            dxbc_ref[:, inner + g * STATE:inner + (g + 1) * STATE] = dBg + _dot(dCBb, Cg, TN)
            dxbc_ref[:, inner + GN + g * STATE:inner + GN + (g + 1) * STATE] = dCg + _dot(dCBb, Bg, NN)
        dda = _dot01_left(cidx >= r, dA_ref[...]) + dAl_ref[...]
        ddt_full = ddtp_ref[...] + dda * ax_ref[...] * (1.0 / HEADDIM)
        ddt_ref[...] = _dot01_right(ddt_full, et_ref[...])
        dax_ref[...] += jnp.sum(dda * dtv, axis=0, keepdims=True)

    xs_s, b_s, c_s, row_s, vec_s, st_s = _ssd_specs(S, inner, GN, nchunks, True)
    return pl.pallas_call(
        body, name=name, grid=(nchunks,),
        in_specs=[row_s, row_s, xs_s, b_s, c_s, row_s, vec_s, vec_s, st_s, _full_spec(et.shape)],
        out_specs=[pl.BlockSpec((L, Cc), lambda c: (nchunks - 1 - c, 0)),
                   pl.BlockSpec((L, H), lambda c: (nchunks - 1 - c, 0)), vec_s, vec_s],
        out_shape=[jax.ShapeDtypeStruct((S, Cc), f32), jax.ShapeDtypeStruct((S, H), f32),
                   jax.ShapeDtypeStruct((1, inner), f32), jax.ShapeDtypeStruct((1, inner), f32)],
        scratch_shapes=[pltpu.VMEM((inner, STATE), f32), pltpu.VMEM((L, inner), f32),
                        pltpu.VMEM((1, inner), f32), pltpu.VMEM((L, inner), f32)],
        compiler_params=_cparams(("arbitrary",)),
    )(dy, y, xbc, xbc, xbc, dtx, ax, dx, states, et)


def _dt_fwd(dt_pre, bias, e, name):
    H, inner = e.shape

    def fn(dp, b, e_):
        dt = _softplus(dp + b)
        return dt, _dot01_right(dt, e_)

    return _rowcall(fn, name=name, rows=[dt_pre], fulls=[_row2(bias), e], out_rows=[(H, f32), (inner, f32)])


def _dt_bwd(ddt, dt_pre, bias, name):
    H = ddt.shape[1]

    def fn(dd, dp, b):
        g = dd * _sigmoid(dp + b)
        return g, jnp.sum(g, axis=0, keepdims=True)

    return _rowcall(fn, name=name, rows=[ddt, dt_pre], fulls=[_row2(bias)], out_rows=[(H, f32)], out_accs=[(1, H)])


def _gnorm_fwd(y, z, w, G, name):
    inner = y.shape[1]
    gs = inner // G

    def fn(y_, z_, w_):
        gg = y_ * _silu(z_)
        outs = []
        for g in range(G):
            sl = slice(g * gs, (g + 1) * gs)
            outs.append(_rms(gg[:, sl], w_[:, sl], LN_EPS))
        return jnp.concatenate(outs, axis=1)

    return _rowcall(fn, name=name, rows=[y, z], fulls=[_row2(w)], out_rows=[(inner, bf16)], tm=256)[0]


def _gnorm_bwd(dyn, y, z, w, G, name):
    inner = y.shape[1]
    gs = inner // G

    def fn(d_, y_, z_, w_):
        sz = _silu(z_)
        gg = y_ * sz
        dgs, dws = [], []
        for g in range(G):
            sl = slice(g * gs, (g + 1) * gs)
            dg, dw = _rms_bwd(d_[:, sl], gg[:, sl], w_[:, sl], LN_EPS)
            dgs.append(dg)
            dws.append(dw)
        dgg = jnp.concatenate(dgs, axis=1)
        return dgg * sz, dgg * y_ * _dsilu(z_), jnp.concatenate(dws, axis=1)

    return _rowcall(fn, name=name, rows=[dyn, y, z], fulls=[_row2(w)], out_rows=[(inner, f32), (inner, f32)],
                    out_accs=[(1, inner)], tm=256)


def _gmlp_parts(pre, lw, lb, I):
    hp = _gelu(pre)
    uu = hp[:, :I]
    vp = hp[:, I:]
    xc = vp - jnp.mean(vp, axis=-1, keepdims=True)
    rstd = lax.rsqrt(jnp.mean(xc * xc, axis=-1, keepdims=True) + LN_EPS)
    vhat = xc * rstd
    return uu, vhat, rstd, vhat * lw + lb


def _gmlp_mid_fwd(pre, b_in, ln_w, ln_b, w_s, bsx, name):
    S, two_i = pre.shape
    I = two_i // 2
    NG = w_s.shape[0]
    gd = I // NG
    L = CHUNK

    def body(pre_ref, bi_ref, lw_ref, lb_ref, ws_ref, bsx_ref, o_ref):
        uu, _, _, vv = _gmlp_parts(pre_ref[...] + bi_ref[...], lw_ref[...], lb_ref[...], I)
        r = lax.broadcasted_iota(jnp.int32, (L, L), 0)
        cidx = lax.broadcasted_iota(jnp.int32, (L, L), 1)
        tril = cidx <= r
        for g in range(NG):
            sl = slice(g * gd, (g + 1) * gd)
            wg = jnp.where(tril, ws_ref[g], 0.0).astype(bf16)
            mixed = _dot(wg, vv[:, sl].astype(bf16), NN) + bsx_ref[:, sl]
            o_ref[:, sl] = (uu[:, sl] * mixed).astype(o_ref.dtype)

    return pl.pallas_call(
        body, name=name, grid=(S // L,),
        in_specs=[_row_spec(L, two_i), _full_spec((1, two_i)), _full_spec((1, I)), _full_spec((1, I)), _full_spec(w_s.shape), _full_spec(bsx.shape)],
        out_specs=_row_spec(L, I), out_shape=jax.ShapeDtypeStruct((S, I), bf16),
        compiler_params=_cparams(("parallel",)),
    )(pre, _row2(b_in), _row2(ln_w), _row2(ln_b), w_s, bsx)


def _gmlp_mid_bwd(do, pre, b_in, ln_w, ln_b, w_s, bsx, name):
    S, two_i = pre.shape
    I = two_i // 2
    NG = w_s.shape[0]
    gd = I // NG
    L = CHUNK

    def body(do_ref, pre_ref, bi_ref, lw_ref, lb_ref, ws_ref, bsx_ref, dpre_ref, dbi_ref, dlw_ref, dlb_ref, dws_ref, dbs_ref, dvv_ref):
        ci = pl.program_id(0)

        @pl.when(ci == 0)
        def _():
            for ref in (dbi_ref, dlw_ref, dlb_ref, dws_ref, dbs_ref):
                ref[...] = jnp.zeros_like(ref)

        pre = pre_ref[...] + bi_ref[...]
        lw = lw_ref[...]
        uu, vhat, rstd, vv = _gmlp_parts(pre, lw, lb_ref[...], I)
        dov = do_ref[...]
        r = lax.broadcasted_iota(jnp.int32, (L, L), 0)
        cidx = lax.broadcasted_iota(jnp.int32, (L, L), 1)
        tril = cidx <= r
        duus = []
        for g in range(NG):
            sl = slice(g * gd, (g + 1) * gd)
            wg = jnp.where(tril, ws_ref[g], 0.0).astype(bf16)
            vg = vv[:, sl].astype(bf16)
            mixed = _dot(wg, vg, NN) + bsx_ref[:, sl]
            duus.append(dov[:, sl] * mixed)
            dmixed = dov[:, sl] * uu[:, sl]
            dbs_ref[:, sl] += dmixed
            dmb = dmixed.astype(bf16)
            dvv_ref[:, sl] = _dot(wg, dmb, TN)
            dws_ref[g] += jnp.where(tril, _dot(dmb, vg, NT), 0.0)
        duu = jnp.concatenate(duus, axis=1)
        dvv = dvv_ref[...]
        dlw_ref[...] += jnp.sum(dvv * vhat, axis=0, keepdims=True)
        dlb_ref[...] += jnp.sum(dvv, axis=0, keepdims=True)
        dvh = dvv * lw
        dvp = rstd * (dvh - jnp.mean(dvh, axis=-1, keepdims=True) - vhat * jnp.mean(dvh * vhat, axis=-1, keepdims=True))
        dpre = jnp.concatenate([duu, dvp], axis=1) * _dgelu(pre)
        dbi_ref[...] += jnp.sum(dpre, axis=0, keepdims=True)
        dpre_ref[...] = dpre.astype(dpre_ref.dtype)

    return pl.pallas_call(
        body, name=name, grid=(S // L,),
        in_specs=[_row_spec(L, I), _row_spec(L, two_i), _full_spec((1, two_i)), _full_spec((1, I)), _full_spec((1, I)),
                  _full_spec(w_s.shape), _full_spec(bsx.shape)],
        out_specs=[_row_spec(L, two_i), _full_spec((1, two_i)), _full_spec((1, I)), _full_spec((1, I)), _full_spec(w_s.shape), _full_spec((L, I))],
        out_shape=[jax.ShapeDtypeStruct((S, two_i), bf16), jax.ShapeDtypeStruct((1, two_i), f32), jax.ShapeDtypeStruct((1, I), f32),
                   jax.ShapeDtypeStruct((1, I), f32), jax.ShapeDtypeStruct(w_s.shape, f32), jax.ShapeDtypeStruct((L, I), f32)],
        scratch_shapes=[pltpu.VMEM((L, I), f32)],
        compiler_params=_cparams(("arbitrary",)),
    )(do, pre, _row2(b_in), _row2(ln_w), _row2(ln_b), w_s, bsx)


def _lane_group_sum(acc, eg, name):
    NG = eg.shape[1]
    return _rowcall(lambda a, e: _dot(a, e, NN, HI), name=name, rows=[acc], fulls=[eg], out_rows=[(NG, f32)])[0]


def _ffn_fwd_fused(h1, nf_w, wg, wu, wd, name):
    S, D = h1.shape
    nb, F4, _ = wg.shape
    tm = _tile(S, 512)

    def body(h_ref, nf_ref, wg_ref, wu_ref, wd_ref, h2_ref, u_ref, g_ref, up_ref, a_ref, acc_ref):
        k = pl.program_id(1)

        @pl.when(k == 0)
        def _():
            u_ref[...] = _rms(h_ref[...], nf_ref[...], RMS_EPS).astype(u_ref.dtype)
            acc_ref[...] = jnp.zeros_like(acc_ref)

        for r in range(FFN_SUBTILES):
            rs = pl.ds(r * (tm // FFN_SUBTILES), tm // FFN_SUBTILES)
            uv = u_ref[rs, :]
            g = _dot(uv, wg_ref[...], NT)
            up = _dot(uv, wu_ref[...], NT)
            a = (_silu(g) * up).astype(bf16)
            g_ref[rs, :] = g.astype(g_ref.dtype)
            up_ref[rs, :] = up.astype(up_ref.dtype)
            a_ref[rs, :] = a
            acc_ref[rs, :] += _dot(a, wd_ref[...], NN)

        @pl.when(k == nb - 1)
        def _():
            h2_ref[...] = h_ref[...] + acc_ref[...]

    row = pl.BlockSpec((tm, D), lambda i, k: (i, 0))
    wspec = pl.BlockSpec((None, F4, D), lambda i, k: (k, 0, 0))
    cspec = pl.BlockSpec((None, tm, F4), lambda i, k: (k, i, 0))
    chunk = jax.ShapeDtypeStruct((nb, S, F4), bf16)
    return pl.pallas_call(
        body, name=name, grid=(S // tm, nb),
        in_specs=[row, _full_spec((1, D)), wspec, wspec, pl.BlockSpec((None, F4, D), lambda i, k: (k, 0, 0))],
        out_specs=[row, row, cspec, cspec, cspec],
        out_shape=[jax.ShapeDtypeStruct((S, D), f32), jax.ShapeDtypeStruct((S, D), bf16), chunk, chunk, chunk],
        scratch_shapes=[pltpu.VMEM((tm, D), f32)],
        compiler_params=_cparams(("parallel", "arbitrary")),
    )(h1, _row2(nf_w), wg, wu, wd)


def _ffn_bwd_fused(dh, h1, nf_w, wd, wg, wu, G, U, name, after=None):
    S, D = dh.shape
    nb, F4, _ = wd.shape
    tm = _tile(S, 512)

    def body(dh_ref, h_ref, nf_ref, wd_ref, wg_ref, wu_ref, g_ref, up_ref, *rest):
        dg_ref, du_ref, dh1_ref, dnf_ref, acc_ref = rest[-5:]
        i, k = pl.program_id(0), pl.program_id(1)

        @pl.when(k == 0)
        def _():
            acc_ref[...] = jnp.zeros_like(acc_ref)

        for r in range(FFN_SUBTILES):
            rs = pl.ds(r * (tm // FFN_SUBTILES), tm // FFN_SUBTILES)
            dA = _dot(dh_ref[rs, :].astype(bf16), wd_ref[...], NT)
            g = g_ref[rs, :].astype(f32)
            dg = (dA * up_ref[rs, :].astype(f32) * _dsilu(g)).astype(bf16)
            du = (dA * _silu(g)).astype(bf16)
            dg_ref[rs, :] = dg
            du_ref[rs, :] = du
            acc_ref[rs, :] += _dot(dg, wg_ref[...], NN) + _dot(du, wu_ref[...], NN)

        @pl.when(k == nb - 1)
        def _():
            dx, dw = _rms_bwd(acc_ref[...], h_ref[...], nf_ref[...], RMS_EPS)
            dh1_ref[...] = dh_ref[...] + dx

            @pl.when(i == 0)
            def _():
                dnf_ref[...] = dw

            @pl.when(i > 0)
            def _():
                dnf_ref[...] += dw

    row = pl.BlockSpec((tm, D), lambda i, k: (i, 0))
    wspec = pl.BlockSpec((None, F4, D), lambda i, k: (k, 0, 0))
    cspec = pl.BlockSpec((None, tm, F4), lambda i, k: (k, i, 0))
    chunk = jax.ShapeDtypeStruct((nb, S, F4), bf16)
    return pl.pallas_call(
        body, name=name, grid=(S // tm, nb),
        in_specs=[row, row, _full_spec((1, D)), wspec, wspec, wspec, cspec, cspec] + ([] if after is None else [pl.BlockSpec(memory_space=pl.ANY)]),
        out_specs=[cspec, cspec, row, _full_spec((1, D))],
        out_shape=[chunk, chunk, jax.ShapeDtypeStruct((S, D), f32), jax.ShapeDtypeStruct((1, D), f32)],
        scratch_shapes=[pltpu.VMEM((tm, D), f32)],
        compiler_params=_cparams(("arbitrary", "arbitrary")),
    )(dh, h1, _row2(nf_w), wd, wg, wu, G, U, *([] if after is None else [after]))


def _rms_bwd_add(dres, du, h, w, name):
    D = h.shape[1]

    def fn(dr, du_, h_, w_):
        dx, dw = _rms_bwd(du_, h_, w_, RMS_EPS)
        return dr + dx, dw

    return _rowcall(fn, name=name, rows=[dres, du, h], fulls=[_row2(w)], out_rows=[(D, f32)], out_accs=[(1, D)])


def _ple_fwd(h, p_i, wp, pn, gn, wgate, name):
    D = h.shape[1]

    def fn(h_, p_, wp_, pn_, gn_, wg_):
        pe = _dot(p_.astype(bf16), wp_, NN)
        e = _rms(pe, pn_, RMS_EPS)
        q = _rms(h_, gn_, RMS_EPS)
        gate = _sigmoid(_dot(q.astype(bf16), wg_, NN))
        return h_ + gate * e, pe, gate

    return _rowcall(fn, name=name, rows=[h, p_i], fulls=[wp, _row2(pn), _row2(gn), wgate],
                    out_rows=[(D, f32), (D, f32), (D, f32)], tm=256)


def _ple_bwd(dh3, h, pe, gate, pn, gn, wgate, name, after=None):
    D = h.shape[1]

    def fn(d_, h_, pe_, gate_, pn_, gn_, wg_, *_):
        e = _rms(pe_, pn_, RMS_EPS)
        dzg = d_ * e * gate_ * (1.0 - gate_)
        dq = _dot(dzg.astype(bf16), wg_, NT)
        dxq, dgn = _rms_bwd(dq, h_, gn_, RMS_EPS)
        dpe, dpn = _rms_bwd(d_ * gate_, pe_, pn_, RMS_EPS)
        return d_ + dxq, dzg, dpe, _rms(h_, gn_, RMS_EPS), dpn, dgn

    return _rowcall(fn, name=name, rows=[dh3, h, pe, gate], fulls=[_row2(pn), _row2(gn), wgate] + ([] if after is None else [after]),
                    out_rows=[(D, f32), (D, bf16), (D, bf16), (D, bf16)], out_accs=[(1, D), (1, D)], tm=256)


def _loss_head(h, target, fn_w, name):
    D = h.shape[1]

    def fn(h_, t_, w_):
        diff = _rms(h_, w_, RMS_EPS) - t_
        loss = 0.5 * jnp.sum(jnp.mean(diff * diff, axis=-1, keepdims=True), axis=0, keepdims=True)
        dh, dw = _rms_bwd(diff * (1.0 / D), h_, w_, RMS_EPS)
        return dh, jnp.broadcast_to(loss, (1, LANES)), dw

    return _rowcall(fn, name=name, rows=[h, target], fulls=[_row2(fn_w)], out_rows=[(D, f32)], out_accs=[(1, LANES), (1, D)])


def _adamw(w, m, v, g, name):
    R, C = w.shape
    tr, tc = R, C
    while tr * tc > 256 * 1024 and tr % (2 * SUBLANES) == 0:
        tr //= 2
    while tr * tc > 256 * 1024 and tc % (2 * LANES) == 0:
        tc //= 2

    def body(w_ref, m_ref, v_ref, g_ref, d_ref, mo_ref, vo_ref):
        g = g_ref[...]
        mn = ADAM_B1 * m_ref[...] + (1.0 - ADAM_B1) * g
        vn = ADAM_B2 * v_ref[...] + (1.0 - ADAM_B2) * (g * g)
        m_hat = mn / (1.0 - ADAM_B1 ** ADAM_STEP)
        v_hat = vn / (1.0 - ADAM_B2 ** ADAM_STEP)
        d_ref[...] = -ADAM_LR * (m_hat / (jnp.sqrt(v_hat) + ADAM_EPS) + ADAM_WD * w_ref[...])
        mo_ref[...] = mn
        vo_ref[...] = vn

    spec = pl.BlockSpec((tr, tc), lambda i, j: (i, j))
    return pl.pallas_call(
        body, name=name, grid=(R // tr, C // tc), in_specs=[spec] * 4,
        out_specs=[spec] * 3, out_shape=[jax.ShapeDtypeStruct((R, C), f32)] * 3,
        compiler_params=_cparams(("parallel", "parallel")),
    )(w, m, v, g)


def _expand_onehot(n, per):
    lane = lax.broadcasted_iota(jnp.int32, (n, n * per), 1)
    row = lax.broadcasted_iota(jnp.int32, (n, n * per), 0)
    return (lane // per == row).astype(f32)


def _ssd_layer_fwd(h, nm_w, W, t):
    H = W["dt_bias"].shape[0]
    inner = H * HEADDIM
    G = (W["conv_b"].shape[0] - inner) // (2 * STATE)
    hn = _rms_fwd(h, nm_w, f"rms_mix_{t}")
    conv_dim = W["conv_b"].shape[0]
    wT = W["w_inT"]
    z = _mm(hn, wT, mode="nt", brows=(0, inner), name=f"ssd_z_{t}")
    xpre = _mm(hn, wT, mode="nt", brows=(inner, conv_dim), name=f"ssd_xbc_{t}")
    dt_pre = _mm(hn, wT, mode="nt", brows=(inner + conv_dim, H), name=f"ssd_dt_{t}")
    c, xbc = _conv_fwd(xpre, W["conv_w"], W["conv_b"], f"ssd_conv_{t}")
    _, dtx = _dt_fwd(dt_pre, W["dt_bias"], _expand_onehot(H, HEADDIM), f"ssd_dtx_{t}")
    a = -jnp.exp(W["a_log"])
    ax = _row2(jnp.repeat(a, HEADDIM))
    dx = _row2(jnp.repeat(W["d"], HEADDIM))
    y, states = _ssd_fwd(xbc, dtx, ax, dx, G, f"ssd_scan_{t}")
    yn = _gnorm_fwd(y, z, W["norm_w"], G, f"ssd_gnorm_{t}")
    h1 = _mm(yn, W["wout"], res=h, name=f"ssd_out_{t}")
    return h1, (h, hn, z, xpre, dt_pre, c, xbc, dtx, a, ax, dx, y, states, yn)


def _ssd_layer_bwd(dh1, saved, nm_w, W, t, after=None):
    h, hn, z, xpre, dt_pre, c, xbc, dtx, a, ax, dx, y, states, yn = saved
    H = W["dt_bias"].shape[0]
    inner = H * HEADDIM
    G = (W["conv_b"].shape[0] - inner) // (2 * STATE)
    dyn = _mm(dh1, W["wout"], mode="nt", after=after, name=f"ssd_out_dx_{t}")
    g_wout = _mm(yn, dh1, mode="tn", out_dtype=bf16, name=f"ssd_out_dw_{t}")
    dy, dz, g_normw = _gnorm_bwd(dyn, y, z, W["norm_w"], G, f"ssd_gnorm_bwd_{t}")
    dxbc, ddt, dax, ddx = _ssd_bwd(dy, y, xbc, dtx, ax, dx, states, _expand_onehot(H, HEADDIM).T, G, f"ssd_scan_bwd_{t}")
    dc, g_convw8, g_convb = _conv_bwd_dc(dxbc, c, xpre, f"ssd_conv_bwd_dc_{t}")
    dxpre = _conv_bwd_dx(dc, W["conv_w"], f"ssd_conv_bwd_dx_{t}")
    ddt_pre, g_dtb = _dt_bwd(ddt, dt_pre, W["dt_bias"], f"ssd_dt_bwd_{t}")
    conv_dim = W["conv_b"].shape[0]
    wT = W["w_inT"]
    g_wz = _mm(dz, hn, mode="tn", out_dtype=bf16, name=f"ssd_z_dw_{t}")
    g_wxbc = _mm(dxpre, hn, mode="tn", out_dtype=bf16, name=f"ssd_xbc_dw_{t}")
    g_wdt = _mm(ddt_pre, hn, mode="tn", out_dtype=bf16, name=f"ssd_dt_dw_{t}")
    dhn = _mm(dz, wT, brows=(0, inner), name=f"ssd_z_dx_{t}")
    dhn = _mm(dxpre, wT, brows=(inner, conv_dim), res=dhn, name=f"ssd_xbc_dx_{t}")
    dhn = _mm(ddt_pre, wT, brows=(inner + conv_dim, H), res=dhn, name=f"ssd_dt_dx_{t}")
    dh, g_nm = _rms_bwd_add(dh1, dhn, h, nm_w, f"rms_mix_bwd_{t}")
    grads = dict(
        w_inT=jnp.concatenate([g_wz, g_wxbc, g_wdt], axis=0), wout=g_wout,
        conv_w=g_convw8[:CONV_K], conv_b=g_convb[0], dt_bias=g_dtb[0],
        a_log=dax[0].reshape(H, HEADDIM)[:, 0] * a, d=jnp.sum(ddx[0].reshape(H, HEADDIM), axis=1),
        norm_w=g_normw[0], norm_mix=g_nm[0])
    return dh, grads


def _gmlp_layer_fwd(h, nm_w, W, t):
    NG, L, _ = W["w_s"].shape
    I = W["ln_w"].shape[0]
    hn = _rms_fwd(h, nm_w, f"rms_mix_{t}")
    pre = _mm(hn, W["win"], name=f"gmlp_in_{t}")
    bsx = jnp.repeat(W["b_s"].T, I // NG, axis=1)
    o = _gmlp_mid_fwd(pre, W["b_in"], W["ln_w"], W["ln_b"], W["w_s"], bsx, f"gmlp_mid_{t}")
    h1 = _mm(o, W["wout"], res=h, name=f"gmlp_out_{t}")
    return h1, (h, hn, pre, bsx, o)


def _gmlp_layer_bwd(dh1, saved, nm_w, W, t, after=None):
    h, hn, pre, bsx, o = saved
    NG = W["w_s"].shape[0]
    I = W["ln_w"].shape[0]
    do = _mm(dh1, W["wout"], mode="nt", after=after, name=f"gmlp_out_dx_{t}")
    g_wout = _mm(o, dh1, mode="tn", out_dtype=bf16, name=f"gmlp_out_dw_{t}")
    dpre, g_bin, g_lnw, g_lnb, g_ws, dbs = _gmlp_mid_bwd(do, pre, W["b_in"], W["ln_w"], W["ln_b"], W["w_s"], bsx, f"gmlp_mid_bwd_{t}")
    g_bs = _lane_group_sum(dbs, _expand_onehot(NG, I // NG).T, f"gmlp_bs_{t}").T
    g_win = _mm(hn, dpre, mode="tn", out_dtype=bf16, name=f"gmlp_in_dw_{t}")
    dhn = _mm(dpre, W["win"], mode="nt", name=f"gmlp_in_dx_{t}")
    dh, g_nm = _rms_bwd_add(dh1, dhn, h, nm_w, f"rms_mix_bwd_{t}")
    grads = dict(win=g_win, wout=g_wout, b_in=g_bin[0], ln_w=g_lnw[0], ln_b=g_lnb[0], w_s=g_ws, b_s=g_bs, norm_mix=g_nm[0])
    return dh, grads


def _ffn_fwd(h1, nf_w, W, t):
    h2, u, Gm, Um, A = _ffn_fwd_fused(h1, nf_w, W["wg"], W["wu"], W["wd"], f"ffn_fwd_{t}")
    return h2, (h1, u, Gm, Um, A)


def _ffn_bwd(dh2, saved, nf_w, W, t, after=None):
    h1, u, Gm, Um, A = saved
    dG, dU, dh1, g_nf = _ffn_bwd_fused(dh2, h1, nf_w, W["wd"], W["wg"], W["wu"], Gm, Um, f"ffn_bwd_{t}", after=after)
    g_wd = _mm(A, dh2, mode="tn", out_dtype=bf16, name=f"ffn_down_dw_{t}")
    g_wg = _mm(dG, u, mode="tn", out_dtype=bf16, name=f"ffn_gate_dw_{t}")
    g_wu = _mm(dU, u, mode="tn", out_dtype=bf16, name=f"ffn_up_dw_{t}")
    return dh1, dict(wg=g_wg, wu=g_wu, wd=g_wd, norm_ffn=g_nf[0])


def _local_step(x, p, target, norms, layer_weights, on_layer_grads=None, final_norm_grad=None):
    depth = p.shape[0]
    h = x
    saved = []
    for i in range(depth):
        Wm = layer_weights(i, "mix", h)
        if i % 2 == 0:
            h1, s_mix = _ssd_layer_fwd(h, norms["norm_mix"][i], Wm, i)
        else:
            h1, s_mix = _gmlp_layer_fwd(h, norms["norm_mix"][i], Wm, i)
        Wf = layer_weights(i, "ffn", h1)
        h2, s_ffn = _ffn_fwd(h1, norms["norm_ffn"][i], Wf, i)
        P = layer_weights(i, "ple", h2)
        h3, pe, gate = _ple_fwd(h2, p[i], P["wp"], P["pn"], P["gn"], P["wgate"], f"ple_{i}")
        saved.append((Wm, Wf, P, s_mix, s_ffn, (h2, pe, gate)))
        h = h3
    dh, loss, g_fn = _loss_head(h, target, norms["final_norm"], "loss_head")
    if final_norm_grad is not None:
        final_norm_grad[0] = g_fn[0]
    grads = [None] * depth
    tell = on_layer_grads if on_layer_grads is not None else (lambda i, part, g: None)
    after = None
    for i in reversed(range(depth)):
        Wm, Wf, P, s_mix, s_ffn, (h2, pe, gate) = saved[i]
        dh, dzg, dpe, q, g_pn, g_gn = _ple_bwd(dh, h2, pe, gate, P["pn"], P["gn"], P["wgate"], f"ple_bwd_{i}", after=after)
        g_ple = dict(wgate=_mm(q, dzg, mode="tn", out_dtype=bf16, name=f"ple_gate_dw_{i}"),
                     wp=_mm(p[i], dpe, mode="tn", out_dtype=bf16, name=f"ple_proj_dw_{i}"), pn=g_pn[0], gn=g_gn[0])
        after = tell(i, "ple", g_ple)
        dh, g_ffn = _ffn_bwd(dh, s_ffn, norms["norm_ffn"][i], Wf, i, after=after)
        after = tell(i, "ffn", g_ffn)
        if i % 2 == 0:
            dh, g_mix = _ssd_layer_bwd(dh, s_mix, norms["norm_mix"][i], Wm, i, after=after)
        else:
            dh, g_mix = _gmlp_layer_bwd(dh, s_mix, norms["norm_mix"][i], Wm, i, after=after)
        after = tell(i, "mix", g_mix)
        grads[i] = dict(mix=g_mix, ffn=g_ffn, ple=g_ple)
    return loss[0, 0], dh, g_fn[0], grads


def _flip(v, f):
    return 1 - v if f else v


_ANY = pl.BlockSpec(memory_space=pl.ANY)


_SEM = pl.BlockSpec(memory_space=pltpu.SEMAPHORE)
_DATAFLOW = pltpu.SideEffectType.DATAFLOW_SIDE_EFFECTING
_CHIP_FLIPS = ((1, 0), (0, 1), (1, 1))
_DMA = pltpu.SemaphoreType.DMA


def _structs(arrs):
    return [jax.ShapeDtypeStruct(a.shape, a.dtype) for a in arrs]


def _gather_copy(src, buf, send_sems, recv_sems, k, j, slot, x, y, c):
    c2 = src.shape[1] // 2
    fx, fy = _CHIP_FLIPS[j]
    nf = len(_CHIP_FLIPS)
    return pltpu.make_async_remote_copy(
        src_ref=src.at[:, pl.ds(c * c2, c2)], dst_ref=buf.at[slot, :, pl.ds(c * c2, c2)], send_sem=send_sems.at[nf * k + j],
        recv_sem=recv_sems.at[nf * k + j], device_id=(_flip(x, fx), _flip(y, fy), c), device_id_type=MESH)


def _gather_start(srcs, groups):
    n = len(srcs)
    ng = len(groups)
    nf = len(_CHIP_FLIPS)

    def body(*refs):
        src_refs, buf_refs, sems = refs[:n], refs[n:2 * n], refs[4 * n:]
        x, y, c = lax.axis_index("x"), lax.axis_index("y"), lax.axis_index("c")
        for gi, group in enumerate(groups):
            for k, o in enumerate(group):
                for j in range(nf):
                    _gather_copy(src_refs[o], buf_refs[o], sems[2 * gi], sems[2 * gi + 1], k, j, 2 * x + y, x, y, c).start()

    mychip = 2 * lax.axis_index("x") + lax.axis_index("y")
    inits = [lax.dynamic_update_slice(lax.empty((N_CHIPS,) + s.shape, s.dtype), s[None], (mychip, 0, 0)) for s in srcs]
    sem_shapes = [_DMA((nf * len(g),)) for g in groups for _ in range(2)]
    outs = pl.pallas_call(
        body, name="gather_start", in_specs=[_ANY] * (2 * n), out_specs=[_ANY] * (2 * n) + [_SEM] * (2 * ng),
        out_shape=_structs(srcs) + _structs(inits) + sem_shapes, input_output_aliases={i: i for i in range(2 * n)},
        compiler_params=pltpu.CompilerParams(has_side_effects=_DATAFLOW),
    )(*srcs, *inits)
    return outs[:n], outs[n:2 * n], [(outs[2 * n + 2 * gi], outs[2 * n + 2 * gi + 1]) for gi in range(ng)]


def _gather_wait(srcs, bufs, sems, after, name):
    n = len(srcs)
    nf = len(_CHIP_FLIPS)

    def body(*refs):
        src_refs, buf_refs, send_sems, recv_sems = refs[:n], refs[n:2 * n], refs[2 * n], refs[2 * n + 1]
        x, y, c = lax.axis_index("x"), lax.axis_index("y"), lax.axis_index("c")
        for k in range(n):
            for j, (fx, fy) in enumerate(_CHIP_FLIPS):
                cp = _gather_copy(src_refs[k], buf_refs[k], send_sems, recv_sems, k, j, 2 * _flip(x, fx) + _flip(y, fy), x, y, c)
                cp.wait_send()
                cp.wait_recv()

    outs = pl.pallas_call(
        body, name=name, in_specs=[_ANY] * (2 * n) + [_SEM, _SEM, _ANY], out_specs=[_ANY] * (2 * n),
        out_shape=_structs(srcs) + _structs(bufs), input_output_aliases={i: i for i in range(2 * n)},
        compiler_params=pltpu.CompilerParams(has_side_effects=_DATAFLOW),
    )(*srcs, *bufs, *sems, after)
    return outs[n:]


def _gather_forward(bufs, name):
    n = len(bufs)
    nf = len(_CHIP_FLIPS)

    def body(*refs):
        outs = refs[n:2 * n]
        send_sems, recv_sems = refs[2 * n:]
        x, y, c = lax.axis_index("x"), lax.axis_index("y"), lax.axis_index("c")

        def forward(k, j, h):
            c2 = bufs[k].shape[2] // 2
            fx, fy = _CHIP_FLIPS[j]
            part = outs[k].at[2 * _flip(x, fx) + _flip(y, fy), :, pl.ds(h * c2, c2)]
            return pltpu.make_async_remote_copy(src_ref=part, dst_ref=part, send_sem=send_sems.at[nf * k + j],
                                                recv_sem=recv_sems.at[nf * k + j], device_id=(x, y, 1 - c), device_id_type=MESH)

        sends = [forward(k, j, c) for k in range(n) for j in range(nf)]
        for cp in sends:
            cp.start()
        for k in range(n):
            for j in range(nf):
                forward(k, j, 1 - c).wait_recv()
        for cp in sends:
            cp.wait_send()

    return pl.pallas_call(
        body, name=name, in_specs=[_ANY] * n, out_specs=[_ANY] * n, out_shape=_structs(bufs),
        input_output_aliases={i: i for i in range(n)}, scratch_shapes=[_DMA((nf * n,)), _DMA((nf * n,))],
    )(*bufs)


def _half_struct(a, lead):
    return jax.ShapeDtypeStruct(lead + (a.shape[-2], a.shape[-1] // 2), a.dtype)


_DEVICE_FLIPS = tuple((f >> 2 & 1, f >> 1 & 1, f & 1) for f in range(1, N_DEV))


def _exchange_copy(srcs, lands, n, send_sems, recv_sems, i, j, slot, x, y, c):
    nf = len(_DEVICE_FLIPS)
    px, py, pc = (_flip(v, f) for v, f in zip((x, y, c), _DEVICE_FLIPS[j]))
    src = srcs[i]
    if i < n:
        c2 = src.shape[2] // 2
        src = src.at[2 * px + py, :, pl.ds(pc * c2, c2)]
    return pltpu.make_async_remote_copy(src_ref=src, dst_ref=lands[i].at[slot], send_sem=send_sems.at[nf * i + j],
                                        recv_sem=recv_sems.at[nf * i + j], device_id=(px, py, pc), device_id_type=MESH)


def _exchange_start(tensors, wholes, name):
    n, m = len(tensors), len(wholes)
    nf = len(_DEVICE_FLIPS)
    t = n + m
    land_structs = ([_half_struct(a, (N_DEV,)) for a in tensors] + [jax.ShapeDtypeStruct((N_DEV,) + w.shape, w.dtype) for w in wholes])

    def body(*refs):
        srcs, lands, send_sems, recv_sems, token = refs[:t], refs[2 * t:3 * t], refs[3 * t], refs[3 * t + 1], refs[3 * t + 2]
        x, y, c = lax.axis_index("x"), lax.axis_index("y"), lax.axis_index("c")
        for i in range(t):
            for j in range(nf):
                _exchange_copy(srcs, lands, n, send_sems, recv_sems, i, j, 4 * x + 2 * y + c, x, y, c).start()
        token[...] = jnp.zeros_like(token)

    outs = pl.pallas_call(
        body, name=name, in_specs=[_ANY] * t,
        out_specs=[_ANY] * (2 * t) + [_SEM, _SEM, pl.BlockSpec(memory_space=pltpu.VMEM)],
        out_shape=_structs(tensors) + _structs(wholes) + land_structs + [_DMA((nf * t,)), _DMA((nf * t,)),
                                                                          jax.ShapeDtypeStruct((SUBLANES, LANES), f32)],
        input_output_aliases={i: i for i in range(t)},
        compiler_params=pltpu.CompilerParams(has_side_effects=_DATAFLOW),
    )(*tensors, *wholes)
    return outs[:t], outs[t:2 * t], (outs[2 * t], outs[2 * t + 1]), outs[2 * t + 2]


def _exchange_wait(srcs, lands, n, sems, after, name):
    t = len(srcs)

    def body(*refs):
        src_refs, land_refs, send_sems, recv_sems = refs[:t], refs[t:2 * t], refs[2 * t], refs[2 * t + 1]
        x, y, c = lax.axis_index("x"), lax.axis_index("y"), lax.axis_index("c")
        for i in range(t):
            for j, (fx, fy, fc) in enumerate(_DEVICE_FLIPS):
                sender = 4 * _flip(x, fx) + 2 * _flip(y, fy) + _flip(c, fc)
                cp = _exchange_copy(src_refs, land_refs, n, send_sems, recv_sems, i, j, sender, x, y, c)
                cp.wait_send()
                cp.wait_recv()

    outs = pl.pallas_call(
        body, name=name, in_specs=[_ANY] * (2 * t) + [_SEM, _SEM, _ANY], out_specs=[_ANY] * (2 * t),
        out_shape=_structs(srcs) + _structs(lands), input_output_aliases={i: i for i in range(2 * t)},
        compiler_params=pltpu.CompilerParams(has_side_effects=_DATAFLOW),
    )(*srcs, *lands, *sems, after)
    return outs[:t], outs[t:]


def _sibling_join(bufs, name):
    flat = [(gi, l) for gi, b in enumerate(bufs) for l in range(b.shape[0])]
    n, n_buf = len(flat), len(bufs)

    def body(*refs):
        outs = refs[n_buf:2 * n_buf]
        send_sems, recv_sems = refs[2 * n_buf:]
        x, y, c = lax.axis_index("x"), lax.axis_index("y"), lax.axis_index("c")

        def push(i, h):
            gi, l = flat[i]
            c2 = bufs[gi].shape[2] // 2
            part = outs[gi].at[l, :, pl.ds(h * c2, c2)]
            return pltpu.make_async_remote_copy(src_ref=part, dst_ref=part, send_sem=send_sems.at[i], recv_sem=recv_sems.at[i],
                                                device_id=(x, y, 1 - c), device_id_type=MESH)

        sends = [push(i, c) for i in range(n)]
        for cp in sends:
            cp.start()
        for i in range(n):
            push(i, 1 - c).wait_recv()
        for cp in sends:
            cp.wait_send()

    dma = pltpu.SemaphoreType.DMA
    return pl.pallas_call(
        body, name=name, in_specs=[_ANY] * n_buf, out_specs=[_ANY] * n_buf,
        out_shape=[jax.ShapeDtypeStruct(b.shape, b.dtype) for b in bufs],
        input_output_aliases={i: i for i in range(n_buf)},
        scratch_shapes=[dma((n,)), dma((n,))],
    )(*bufs)


def _device_sum(landed, own, place, name, into=None, layer=0, layers=1):
    ndev, R, C2 = landed.shape
    tr, tc = R, C2
    while ndev * tr * tc > 1024 * 1024 and tr % (4 * SUBLANES) == 0:
        tr //= 2
    while ndev * tr * tc > 1024 * 1024 and tc % (2 * LANES) == 0:
        tc //= 2
    ncb = C2 // tc

    def body(*refs):
        place_ref, l_ref, m_ref, o_ref = refs[0], refs[1], refs[2], refs[-1]
        me = 2 * place_ref[0] + place_ref[1]
        s = jnp.where(me == 0, m_ref[...].astype(f32), l_ref[0].astype(f32))
        for d in range(1, ndev):
            s = s + jnp.where(me == d, m_ref[...].astype(f32), l_ref[d].astype(f32))
        o_ref[...] = s

    in_specs = [pl.BlockSpec((ndev, tr, tc), lambda i, j, pr: (0, i, j)),
                pl.BlockSpec((None, tr, tc), lambda i, j, pr: (pr[0], i, pr[1] * ncb + j))]
    args = [place, landed, own]
    if into is not None:
        in_specs.append(_ANY)
        args.append(into)
    return pl.pallas_call(
        body, name=name, out_shape=jax.ShapeDtypeStruct((layers, R, 2 * C2), f32),
        grid_spec=pltpu.PrefetchScalarGridSpec(
            num_scalar_prefetch=1, grid=(R // tr, ncb), in_specs=in_specs,
            out_specs=pl.BlockSpec((None, tr, tc), lambda i, j, pr: (layer, i, pr[1] * ncb + j))),
        input_output_aliases={3: 0} if into is not None else {},
        compiler_params=_cparams(("parallel", "parallel")),
    )(*args)


def _device_sum_whole(landed, own, place, name):
    ndev, R, C = landed.shape
    tr = R
    while ndev * tr * C > 1024 * 1024 and tr % (2 * SUBLANES) == 0:
        tr //= 2

    def body(place_ref, l_ref, m_ref, o_ref):
        me = 2 * place_ref[0] + place_ref[1]
        s = jnp.where(me == 0, m_ref[...], l_ref[0])
        for d in range(1, ndev):
            s = s + jnp.where(me == d, m_ref[...], l_ref[d])
        o_ref[...] = s

    return pl.pallas_call(
        body, name=name, out_shape=jax.ShapeDtypeStruct((R, C), f32),
        grid_spec=pltpu.PrefetchScalarGridSpec(
            num_scalar_prefetch=1, grid=(R // tr,),
            in_specs=[pl.BlockSpec((ndev, tr, C), lambda i, pr: (0, i, 0)), pl.BlockSpec((tr, C), lambda i, pr: (i, 0))],
            out_specs=pl.BlockSpec((tr, C), lambda i, pr: (i, 0))),
        compiler_params=_cparams(("parallel",)),
    )(place, landed, own)


PACK_COLS = 1024
PACK_ROW_MULTIPLE = 64

BIG = ("ssd_w_in", "ssd_w_out", "gmlp_w_in", "gmlp_w_out", "ffn_w_gate", "ffn_w_up", "ffn_w_down", "ple_w_proj", "ple_w_gate")
SMALL_SHARDED = ("ssd_conv_w", "gmlp_b_in", "gmlp_ln_w", "gmlp_ln_b")
REP_EARLY = "gmlp_w_s"
REP_LATE = ("norm_mix", "norm_ffn", "ssd_conv_b", "ssd_dt_bias", "ssd_a_log", "ssd_d", "ssd_norm_w", "gmlp_b_s", "ple_norm",
            "ple_gate_norm", "final_norm")
WEIGHTS = ("norm_mix", "norm_ffn", "ssd_w_in", "ssd_conv_w", "ssd_conv_b", "ssd_dt_bias", "ssd_a_log", "ssd_d", "ssd_norm_w", "ssd_w_out",
           "gmlp_w_in", "gmlp_b_in", "gmlp_ln_w", "gmlp_ln_b", "gmlp_w_s", "gmlp_b_s", "gmlp_w_out", "ffn_w_gate", "ffn_w_up",
           "ffn_w_down", "ple_w_proj", "ple_norm", "ple_gate_norm", "ple_w_gate", "final_norm")
TRANSPOSED = ("ssd_w_in", "ffn_w_gate", "ffn_w_up")


def _pack(arrs):
    flat = jnp.concatenate([a.reshape(-1).astype(f32) for a in arrs])
    per = PACK_COLS * PACK_ROW_MULTIPLE
    n = -(-flat.shape[0] // per) * per
    return jnp.pad(flat, (0, n - flat.shape[0])).reshape(-1, PACK_COLS)


def _unpack(buf, shapes):
    flat = buf.reshape(-1)
    out, o = [], 0
    for s in shapes:
        n = math.prod(s)
        out.append(flat[o:o + n].reshape(s))
        o += n
    return out


def _chip_major(g):
    r, c4 = g.shape
    return g.reshape(r, N_CHIPS, c4 // N_CHIPS).transpose(1, 0, 2)


def _from_chip_major(g):
    k, r, c = g.shape
    return g.transpose(1, 0, 2).reshape(r, k * c)


def _adamw_nd(w, m, v, g, name):
    shp = w.shape
    two = lambda a: a.reshape(-1, shp[-1])
    return [o.reshape(shp) for o in _adamw(two(w), two(m), two(v), two(g), name)]


def kernel(x, p, norm_mix, norm_ffn, ssd_w_in, ssd_conv_w, ssd_conv_b, ssd_dt_bias, ssd_a_log, ssd_d, ssd_norm_w, ssd_w_out, gmlp_w_in, gmlp_b_in, gmlp_ln_w, gmlp_ln_b, gmlp_w_s, gmlp_b_s, gmlp_w_out, ffn_w_gate, ffn_w_up, ffn_w_down, ple_w_proj, ple_norm, ple_gate_norm, ple_w_gate, final_norm, loss_target, m_norm_mix, m_norm_ffn, m_ssd_w_in, m_ssd_conv_w, m_ssd_conv_b, m_ssd_dt_bias, m_ssd_a_log, m_ssd_d, m_ssd_norm_w, m_ssd_w_out, m_gmlp_w_in, m_gmlp_b_in, m_gmlp_ln_w, m_gmlp_ln_b, m_gmlp_w_s, m_gmlp_b_s, m_gmlp_w_out, m_ffn_w_gate, m_ffn_w_up, m_ffn_w_down, m_ple_w_proj, m_ple_norm, m_ple_gate_norm, m_ple_w_gate, m_final_norm, v_norm_mix, v_norm_ffn, v_ssd_w_in, v_ssd_conv_w, v_ssd_conv_b, v_ssd_dt_bias, v_ssd_a_log, v_ssd_d, v_ssd_norm_w, v_ssd_w_out, v_gmlp_w_in, v_gmlp_b_in, v_gmlp_ln_w, v_gmlp_ln_b, v_gmlp_w_s, v_gmlp_b_s, v_gmlp_w_out, v_ffn_w_gate, v_ffn_w_up, v_ffn_w_down, v_ple_w_proj, v_ple_norm, v_ple_gate_norm, v_ple_w_gate, v_final_norm):
    given = dict(locals())
    view = lambda n, a: jnp.swapaxes(a, 1, 2) if n in TRANSPOSED else a
    w = {n: view(n, given[n]) for n in WEIGHTS}
    mom = {n: view(n, given["m_" + n]) for n in WEIGHTS}
    var = {n: view(n, given["v_" + n]) for n in WEIGHTS}
    depth = p.shape[0]
    n_ssd, n_gmlp = ssd_w_in.shape[0], gmlp_w_in.shape[0]
    inner = ssd_dt_bias.shape[1] * HEADDIM
    conv_dim = ssd_conv_b.shape[1]

    place = jnp.stack([2 * lax.axis_index("x") + lax.axis_index("y"), lax.axis_index("c")]).astype(jnp.int32)

    def part_keys(i, part):
        j = i // 2
        if part == "mix":
            names = (("ssd_w_in", j), ("ssd_w_out", j)) if i % 2 == 0 else (("gmlp_w_in", j), ("gmlp_w_out", j))
            return ((("small", 0),) if i == 0 else ()) + names
        if part == "ffn":
            return (("ffn_w_gate", i), ("ffn_w_up", i), ("ffn_w_down", i))
        return (("ple_w_proj", i), ("ple_w_gate", i))

    parts = [(i, part) for i in range(depth) for part in ("mix", "ffn", "ple")]
    keys, groups = [], {}
    for ip in parts:
        names = part_keys(*ip)
        groups[ip] = list(range(len(keys), len(keys) + len(names)))
        keys += names
    small_shapes = [w[n].shape for n in SMALL_SHARDED]
    srcs = [_pack([w[n] for n in SMALL_SHARDED]) if n == "small" else w[n][l].astype(bf16) for n, l in keys]
    srcs, landing, gather_sems = _gather_start(srcs, [groups[ip] for ip in parts])
    gather_sems = dict(zip(parts, gather_sems))
    small_full = {}

    gw = {}

    def fetch(i, which, h):
        got = []
        for part in which:
            idx = groups[(i, part)]
            got += _gather_wait([srcs[o] for o in idx], [landing[o] for o in idx], gather_sems[(i, part)], h, f"gather_wait_{part}_{i}")
        names = [keys[o] for part in which for o in groups[(i, part)]]
        gw.update(dict(zip(names, _gather_forward(got, f"gather_forward_{which[0]}_{i}"))))

    def layer_weights(i, part, h):
        if i == 0:
            fetch(i, (part,), h)
        elif part == "mix":
            fetch(i, ("mix", "ffn", "ple"), h)
        rows = lambda a: a.reshape(-1, a.shape[-1])
        j = i // 2
        if part == "ffn":
            return dict(wg=gw[("ffn_w_gate", i)], wu=gw[("ffn_w_up", i)], wd=gw[("ffn_w_down", i)])
        if part == "ple":
            return dict(wp=_from_chip_major(gw[("ple_w_proj", i)]), pn=ple_norm[i], gn=ple_gate_norm[i], wgate=rows(gw[("ple_w_gate", i)]))
        if i == 0:
            by_chip = [_unpack(gw[("small", 0)][k], small_shapes) for k in range(N_CHIPS)]
            small_full.update({n: jnp.concatenate([by_chip[k][t] for k in range(N_CHIPS)], axis=-1) for t, n in enumerate(SMALL_SHARDED)})
        if i % 2 == 0:
            return dict(w_inT=rows(gw[("ssd_w_in", j)]),
                        conv_w=small_full["ssd_conv_w"][j], conv_b=ssd_conv_b[j], dt_bias=ssd_dt_bias[j], a_log=ssd_a_log[j],
                        d=ssd_d[j], norm_w=ssd_norm_w[j], wout=rows(gw[("ssd_w_out", j)]))
        return dict(win=_from_chip_major(gw[("gmlp_w_in", j)]), b_in=small_full["gmlp_b_in"][j], ln_w=small_full["gmlp_ln_w"][j],
                    ln_b=small_full["gmlp_ln_b"][j], w_s=gmlp_w_s[j], b_s=gmlp_b_s[j], wout=rows(gw[("gmlp_w_out", j)]))

    rows4 = lambda a: a.reshape((N_CHIPS, a.shape[0] // N_CHIPS) + a.shape[1:])
    cut = lambda a, k: a[..., k * (a.shape[-1] // N_CHIPS):(k + 1) * (a.shape[-1] // N_CHIPS)]
    layer_grads = {}
    in_flight = {}
    tokens = {}
    owns = {}

    def on_layer_grads(i, part, g):
        layer_grads[(i, part)] = g
        j = i // 2
        wholes = {}
        if part == "ffn":
            chunks = {("ffn_w_gate", i): g["wg"], ("ffn_w_up", i): g["wu"], ("ffn_w_down", i): g["wd"]}
        elif part == "ple":
            chunks = {("ple_w_proj", i): _chip_major(g["wp"]), ("ple_w_gate", i): rows4(g["wgate"])}
        elif i % 2 == 0:
            chunks = {("ssd_w_in", j): rows4(g["w_inT"]), ("ssd_w_out", j): rows4(g["wout"])}
        else:
            chunks = {("gmlp_w_in", j): _chip_major(g["win"]), ("gmlp_w_out", j): rows4(g["wout"])}
        stack = lambda prt, key, layers: jnp.stack([layer_grads[(l, prt)][key] for l in layers])
        ssd, gml, every = range(0, depth, 2), range(1, depth, 2), range(depth)
        if part == "mix" and i == 1:
            wholes["rep_early"] = stack("mix", "w_s", gml).reshape(-1, w[REP_EARLY].shape[-1])
        if part == "mix" and i == 0:
            small_g = dict(ssd_conv_w=stack("mix", "conv_w", ssd), gmlp_b_in=stack("mix", "b_in", gml),
                           gmlp_ln_w=stack("mix", "ln_w", gml), gmlp_ln_b=stack("mix", "ln_b", gml))
            chunks[("small", 0)] = jnp.stack([_pack([cut(small_g[n], k) for n in SMALL_SHARDED]) for k in range(N_CHIPS)])
            rep_g = dict(
                norm_mix=stack("mix", "norm_mix", every), norm_ffn=stack("ffn", "norm_ffn", every),
                ssd_conv_b=stack("mix", "conv_b", ssd), ssd_dt_bias=stack("mix", "dt_bias", ssd), ssd_a_log=stack("mix", "a_log", ssd),
                ssd_d=stack("mix", "d", ssd), ssd_norm_w=stack("mix", "norm_w", ssd), gmlp_b_s=stack("mix", "b_s", gml),
                ple_norm=stack("ple", "pn", every), ple_gate_norm=stack("ple", "gn", every), final_norm=final_norm_grad[0])
            wholes["rep_late"] = _pack([rep_g[n] for n in REP_LATE])
        ks, wk = list(chunks), list(wholes)
        thru, lands, sems, token = _exchange_start([chunks[k] for k in ks], [wholes[k] for k in wk], f"grads_exchange_start_{part}_{i}")
        in_flight[(i, part)] = (ks, wk, thru, lands, sems)
        tokens[(i, part)] = token
        return token

    final_norm_grad = [None]
    norms = dict(norm_mix=norm_mix, norm_ffn=norm_ffn, final_norm=final_norm)
    loss_part, grad_x, g_fn, _ = _local_step(x[0], p[:, 0], loss_target[0], norms, layer_weights, on_layer_grads, final_norm_grad)
    loss = lax.psum(loss_part, ("x", "y", "c"))

    landed, res = {}, {}

    def wait_for(which, after):
        for i, part in which:
            ks, wk, thru, lands, sems = in_flight[(i, part)]
            thru, lands = _exchange_wait(thru, lands, len(ks), sems, after, f"grads_exchange_wait_{part}_{i}")
            landed.update(dict(zip(ks + wk, lands)))
            owns.update(dict(zip(ks + wk, thru)))

    def packed_update(names, gsum, tag):
        packs = [gsum] + list(_adamw(_pack([w[n] for n in names]), _pack([mom[n] for n in names]), _pack([var[n] for n in names]), gsum, tag))
        per_kind = [_unpack(pk, [w[n].shape for n in names]) for pk in packs]
        for t, n in enumerate(names):
            res[n] = [per_kind[k][t] for k in range(4)]

    def finish(big_names, with_small, tag):
        bufs = []
        for n in big_names + (("small",) if with_small else ()):
            layers = w[n].shape[0] if n != "small" else 1
            buf = None
            for l in range(layers):
                buf = _device_sum(landed[(n, l)], owns[(n, l)], place, f"grads_sum_{n}_{l}", into=buf, layer=l, layers=layers)
            bufs.append(buf)
        reduced = _sibling_join(bufs, f"grads_sibling_join_{tag}")
        for n, gsum in zip(big_names, reduced):
            res[n] = [view(n, a) for a in [gsum] + _adamw_nd(w[n], mom[n], var[n], gsum, "adamw_" + n)]
        if with_small:
            packed_update(SMALL_SHARDED, reduced[-1][0], "adamw_small_sharded")
            packed_update(REP_LATE, _device_sum_whole(landed["rep_late"], owns["rep_late"], place, "grads_sum_rep_late"), "adamw_rep_late")
        else:
            n, shp = REP_EARLY, w[REP_EARLY].shape
            two = lambda a: a.reshape(-1, shp[-1])
            gsum = _device_sum_whole(landed["rep_early"], owns["rep_early"], place, "grads_sum_rep_early")
            res[n] = [a.reshape(shp) for a in [gsum] + list(_adamw(two(w[n]), two(mom[n]), two(var[n]), gsum, "adamw_" + n))]

    last = (0, "mix")
    late_big = tuple(n for n in BIG if n.startswith("ssd_"))
    early_big = tuple(n for n in BIG if n not in late_big)
    wait_for([ip for ip in reversed(parts) if ip != last], tokens[last])
    finish(early_big, False, "early")
    wait_for([last], res[early_big[-1]][1])
    finish(late_big, True, "late")
    return (loss, grad_x[None], *[res[n][0] for n in WEIGHTS], *[res[n][1] for n in WEIGHTS],
            *[res[n][2] for n in WEIGHTS], *[res[n][3] for n in WEIGHTS])
```

```python
import math

import jax
import jax.numpy as jnp
from jax import lax
from jax.experimental import pallas as pl
from jax.experimental.pallas import tpu as pltpu

f32 = jnp.float32
bf16 = jnp.bfloat16
HI = lax.Precision.HIGHEST

LANES = 128
SUBLANES = 8
VMEM_LIMIT_BYTES = 56 * 1024 * 1024

HEADDIM = 64
STATE = 128
CHUNK = 128
CONV_K = 4
RMS_EPS = 1e-6
LN_EPS = 1e-5
ADAM_LR = 0.001
ADAM_B1 = 0.9
ADAM_B2 = 0.999
ADAM_EPS = 1e-08
ADAM_WD = 0.01
ADAM_STEP = 10

FFN_SUBTILES = 2

N_CHIPS = 4
N_DEV = 8
MESH = pl.DeviceIdType.MESH


def _cparams(sem):
    return pltpu.CompilerParams(dimension_semantics=sem, vmem_limit_bytes=VMEM_LIMIT_BYTES)


def _tile(n, want):
    if n <= want:
        return n
    t = want
    while n % t:
        t //= 2
    return t


def _row_spec(tm, c):
    return pl.BlockSpec((tm, c), lambda i: (i, 0))


def _full_spec(shape):
    nd = len(shape)
    return pl.BlockSpec(tuple(shape), lambda *_: (0,) * nd)


def _sigmoid(x):
    return 1.0 / (1.0 + jnp.exp(-x))


def _silu(x):
    return x * _sigmoid(x)


def _dsilu(x):
    s = _sigmoid(x)
    return s * (1.0 + x * (1.0 - s))


def _gelu(x):
    return 0.5 * x * (1.0 + lax.erf(x * (1.0 / math.sqrt(2.0))))


def _dgelu(x):
    return 0.5 * (1.0 + lax.erf(x * (1.0 / math.sqrt(2.0)))) + x * jnp.exp(-0.5 * x * x) * (1.0 / math.sqrt(2.0 * math.pi))


def _softplus(x):
    return jnp.maximum(x, 0.0) + jnp.log(1.0 + jnp.exp(-jnp.abs(x)))


def _rms(x, w, eps):
    r = lax.rsqrt(jnp.mean(x * x, axis=-1, keepdims=True) + eps)
    return x * r * w


def _rms_bwd(dy, x, w, eps):
    r = lax.rsqrt(jnp.mean(x * x, axis=-1, keepdims=True) + eps)
    xh = x * r
    g = dy * w
    dx = r * (g - xh * jnp.mean(g * xh, axis=-1, keepdims=True))
    dw = jnp.sum(dy * xh, axis=0, keepdims=True)
    return dx, dw


def _dot(a, b, dims=(((1,), (0,)), ((), ())), precision=None):
    return lax.dot_general(a, b, dims, precision=precision, preferred_element_type=f32)


NN = (((1,), (0,)), ((), ()))
NT = (((1,), (1,)), ((), ()))
TN = (((0,), (0,)), ((), ()))


def _split3(x):
    hi = x.astype(bf16)
    r1 = x - hi.astype(f32)
    mid = r1.astype(bf16)
    return hi, mid, (r1 - mid.astype(f32)).astype(bf16)


def _dot01_left(m01, x):
    mb = m01.astype(bf16)
    hi, mid, lo = _split3(x)
    return _dot(mb, hi, NN) + _dot(mb, mid, NN) + _dot(mb, lo, NN)


def _dot01_right(x, m01):
    mb = m01.astype(bf16)
    hi, mid, lo = _split3(x)
    return _dot(hi, mb, NN) + _dot(mid, mb, NN) + _dot(lo, mb, NN)


def _mm(a, b, *, mode="nn", out_dtype=f32, res=None, kbatch=False, brows=None, after=None, tm=1024, tn=1024, tk=1024, name):
    a3, b3 = a.ndim == 3, b.ndim == 3
    nb = a.shape[0] if a3 else (b.shape[0] if b3 else 1)
    ash, bsh = a.shape[-2:], b.shape[-2:]
    if brows is not None:
        bsh = (brows[1], bsh[1])
    if mode == "nn":
        M, K, N = ash[0], ash[1], bsh[1]
    elif mode == "nt":
        M, K, N = ash[0], ash[1], bsh[0]
    else:
        K, M, N = ash[0], ash[1], bsh[1]
    tm, tn, tk = _tile(M, tm), (N if N % LANES else _tile(N, tn)), (K if K % LANES else _tile(K, tk))
    b0 = 0
    if brows is not None:
        assert mode in ("nn", "nt") and bsh[0] == (K if mode == "nn" else N)
        blk = tk if mode == "nn" else tn
        while brows[0] % blk:
            blk //= 2
        assert blk % LANES == 0 or blk == brows[1]
        b0 = brows[0] // blk
        tn, tk = (tn, blk) if mode == "nn" else (blk, tk)
    nk = K // tk
    if kbatch:
        assert a3 and b3
        grid = (1, M // tm, N // tn, nb * nk)
        bi = lambda g, k: k // nk
        ki = lambda g, k: k % nk
    else:
        grid = (nb, M // tm, N // tn, nk)
        bi = lambda g, k: g
        ki = lambda g, k: k
    nsteps = grid[3]

    def spec(is3, blk, imap):
        if is3:
            return pl.BlockSpec((None,) + blk, lambda g, i, j, k: (bi(g, k),) + imap(i, j, ki(g, k)))
        return pl.BlockSpec(blk, lambda g, i, j, k: imap(i, j, ki(g, k)))

    if mode == "nn":
        a_spec = spec(a3, (tm, tk), lambda i, j, k: (i, k))
        b_spec = spec(b3, (tk, tn), lambda i, j, k: (k + b0, j))
        dims = NN
    elif mode == "nt":
        a_spec = spec(a3, (tm, tk), lambda i, j, k: (i, k))
        b_spec = spec(b3, (tn, tk), lambda i, j, k: (j + b0, k))
        dims = NT
    else:
        a_spec = spec(a3, (tk, tm), lambda i, j, k: (k, i))
        b_spec = spec(b3, (tk, tn), lambda i, j, k: (k, j))
        dims = TN
    out3 = (a3 or b3) and not kbatch
    if out3:
        o_spec = pl.BlockSpec((None, tm, tn), lambda g, i, j, k: (g, i, j))
        o_shape = jax.ShapeDtypeStruct((nb, M, N), out_dtype)
    else:
        o_spec = pl.BlockSpec((tm, tn), lambda g, i, j, k: (i, j))
        o_shape = jax.ShapeDtypeStruct((M, N), out_dtype)
    in_specs = [a_spec, b_spec]
    args = [a, b]
    if res is not None:
        in_specs.append(pl.BlockSpec((tm, tn), lambda g, i, j, k: (i, j)))
        args.append(res)
    if after is not None:
        in_specs.append(pl.BlockSpec(memory_space=pl.ANY))
        args.append(after)

    def body(*refs):
        a_ref, b_ref = refs[:2]
        r_ref = refs[2] if res is not None else None
        o_ref, acc_ref = refs[-2:]
        k = pl.program_id(3)

        @pl.when(k == 0)
        def _():
            acc_ref[...] = jnp.zeros_like(acc_ref)

        acc_ref[...] += _dot(a_ref[...].astype(bf16), b_ref[...].astype(bf16), dims)

        @pl.when(k == nsteps - 1)
        def _():
            r = acc_ref[...]
            if res is not None:
                r = r + r_ref[...]
            o_ref[...] = r.astype(o_ref.dtype)

    return pl.pallas_call(
        body, name=name, grid=grid, in_specs=in_specs, out_specs=o_spec, out_shape=o_shape,
        scratch_shapes=[pltpu.VMEM((tm, tn), f32)],
        compiler_params=_cparams(("parallel", "parallel", "parallel", "arbitrary")),
    )(*args)


def _rowcall(fn, *, name, rows, fulls, out_rows, out_accs=(), tm=512):
    S = rows[0].shape[0]
    tm = _tile(S, tm)
    n_r, n_f, n_or, n_oa = len(rows), len(fulls), len(out_rows), len(out_accs)

    def body(*refs):
        ins = [r[...] for r in refs[:n_r + n_f]]
        outs = fn(*ins)
        if not isinstance(outs, (tuple, list)):
            outs = (outs,)
        o_refs = refs[n_r + n_f:]
        for o_ref, v in zip(o_refs[:n_or], outs[:n_or]):
            o_ref[...] = v.astype(o_ref.dtype)
        if n_oa:
            first = pl.program_id(0) == 0

            @pl.when(first)
            def _():
                for o_ref, v in zip(o_refs[n_or:], outs[n_or:]):
                    o_ref[...] = v

            @pl.when(jnp.logical_not(first))
            def _():
                for o_ref, v in zip(o_refs[n_or:], outs[n_or:]):
                    o_ref[...] += v

    in_specs = [_row_spec(tm, r.shape[1]) for r in rows] + [_full_spec(f.shape) for f in fulls]
    out_specs = [_row_spec(tm, c) for c, _ in out_rows] + [_full_spec(s) for s in out_accs]
    out_shape = [jax.ShapeDtypeStruct((S, c), d) for c, d in out_rows] + [jax.ShapeDtypeStruct(s, f32) for s in out_accs]
    res = pl.pallas_call(
        body, name=name, grid=(S // tm,), in_specs=in_specs, out_specs=out_specs, out_shape=out_shape,
        compiler_params=_cparams(("arbitrary",) if n_oa else ("parallel",)),
    )(*rows, *fulls)
    return res


def _row2(v):
    return v.reshape(1, -1)


def _rms_fwd(h, w, name):
    D = h.shape[1]
    return _rowcall(lambda x, w_: _rms(x, w_, RMS_EPS), name=name, rows=[h], fulls=[_row2(w)], out_rows=[(D, bf16)])[0]


def _conv_taps(x, halo, w_ref, b_ref):
    row = lax.broadcasted_iota(jnp.int32, x.shape, 0)
    row8 = lax.broadcasted_iota(jnp.int32, halo.shape, 0)
    x0 = x[0:SUBLANES, :]
    acc = x * w_ref[CONV_K - 1:CONV_K, :] + b_ref[...]
    acc0 = x0 * w_ref[CONV_K - 1:CONV_K, :] + b_ref[...]
    shifted = []
    for k in range(1, CONV_K):
        wk = w_ref[CONV_K - 1 - k:CONV_K - k, :]
        xk = pltpu.roll(x, k, axis=0)
        xk0 = jnp.where(row8 < k, pltpu.roll(halo, k, axis=0), pltpu.roll(x0, k, axis=0))
        acc = acc + xk * wk
        acc0 = acc0 + xk0 * wk
        shifted.append((jnp.where(row < SUBLANES, 0.0, xk), xk0))
    return acc, acc0, shifted


def _conv_fwd(xpre, w, b, name):
    S, C = xpre.shape
    tm, tc = _tile(S, 512), _tile(C, 1024)
    hb = tm // SUBLANES

    def body(x_ref, halo_ref, w_ref, b_ref, o_ref):
        halo = jnp.where(pl.program_id(1) > 0, halo_ref[...], 0.0)
        acc, acc0, _ = _conv_taps(x_ref[...], halo, w_ref, b_ref)
        o_ref[...] = _silu(acc)
        o_ref[0:SUBLANES, :] = _silu(acc0)

    return pl.pallas_call(
        body, name=name, grid=(C // tc, S // tm),
        in_specs=[pl.BlockSpec((tm, tc), lambda j, i: (i, j)),
                  pl.BlockSpec((SUBLANES, tc), lambda j, i: (jnp.maximum(i * hb - 1, 0), j)),
                  pl.BlockSpec((CONV_K, tc), lambda j, i: (0, j)),
                  pl.BlockSpec((1, tc), lambda j, i: (0, j))],
        out_specs=pl.BlockSpec((tm, tc), lambda j, i: (i, j)),
        out_shape=jax.ShapeDtypeStruct((S, C), f32),
        compiler_params=_cparams(("parallel", "parallel")),
    )(xpre, xpre, w, _row2(b))


def _conv_bwd_dc(dxbc, xpre, w, b, name):
    S, C = xpre.shape
    tm, tc = _tile(S, 512), _tile(C, 1024)
    hb = tm // SUBLANES

    def body(d_ref, x_ref, halo_ref, w_ref, b_ref, dc_ref, dw_ref, db_ref):
        i = pl.program_id(1)
        x = x_ref[...]
        halo = jnp.where(i > 0, halo_ref[...], 0.0)
        acc, acc0, shifted = _conv_taps(x, halo, w_ref, b_ref)
        row = lax.broadcasted_iota(jnp.int32, x.shape, 0)
        d = d_ref[...]
        dc0 = d[0:SUBLANES, :] * _dsilu(acc0)
        dc = jnp.where(row < SUBLANES, 0.0, d * _dsilu(acc))
        dc_ref[...] = dc
        dc_ref[0:SUBLANES, :] = dc0
        parts = [jnp.sum(dc * x, axis=0, keepdims=True) + jnp.sum(dc0 * x[0:SUBLANES, :], axis=0, keepdims=True)]
        for xs_big, xs0 in shifted:
            parts.append(jnp.sum(dc * xs_big, axis=0, keepdims=True) + jnp.sum(dc0 * xs0, axis=0, keepdims=True))
        dw = jnp.concatenate([parts[CONV_K - 1 - k] for k in range(CONV_K)] + [jnp.zeros((SUBLANES - CONV_K, x.shape[1]), f32)], axis=0)
        db = jnp.sum(dc, axis=0, keepdims=True) + jnp.sum(dc0, axis=0, keepdims=True)

        @pl.when(i == 0)
        def _():
            dw_ref[...] = dw
            db_ref[...] = db

        @pl.when(i > 0)
        def _():
            dw_ref[...] += dw
            db_ref[...] += db

    return pl.pallas_call(
        body, name=name, grid=(C // tc, S // tm),
        in_specs=[pl.BlockSpec((tm, tc), lambda j, i: (i, j))] * 2 +
                 [pl.BlockSpec((SUBLANES, tc), lambda j, i: (jnp.maximum(i * hb - 1, 0), j)),
                  pl.BlockSpec((CONV_K, tc), lambda j, i: (0, j)),
                  pl.BlockSpec((1, tc), lambda j, i: (0, j))],
        out_specs=[pl.BlockSpec((tm, tc), lambda j, i: (i, j)),
                   pl.BlockSpec((SUBLANES, tc), lambda j, i: (0, j)),
                   pl.BlockSpec((1, tc), lambda j, i: (0, j))],
        out_shape=[jax.ShapeDtypeStruct((S, C), f32), jax.ShapeDtypeStruct((SUBLANES, C), f32), jax.ShapeDtypeStruct((1, C), f32)],
        compiler_params=_cparams(("parallel", "arbitrary")),
    )(dxbc, xpre, xpre, w, _row2(b))


def _conv_bwd_dx(dc, w, name):
    S, C = dc.shape
    tm, tc = _tile(S, 512), _tile(C, 1024)
    hb = tm // SUBLANES
    nrow = S // tm
    last8 = S // SUBLANES - 1

    def body(d_ref, nxt_ref, w_ref, o_ref):
        i = pl.program_id(1)
        d = d_ref[...]
        nxt = jnp.where(i < nrow - 1, nxt_ref[...], 0.0)
        row8 = lax.broadcasted_iota(jnp.int32, nxt.shape, 0)
        dl = d[tm - SUBLANES:tm, :]
        acc = d * w_ref[CONV_K - 1:CONV_K, :]
        accl = dl * w_ref[CONV_K - 1:CONV_K, :]
        for j in range(1, CONV_K):
            wk = w_ref[CONV_K - 1 - j:CONV_K - j, :]
            acc = acc + pltpu.roll(d, tm - j, axis=0) * wk
            accl = accl + jnp.where(row8 >= SUBLANES - j, pltpu.roll(nxt, SUBLANES - j, axis=0), pltpu.roll(dl, SUBLANES - j, axis=0)) * wk
        o_ref[...] = acc.astype(o_ref.dtype)
        o_ref[tm - SUBLANES:tm, :] = accl.astype(o_ref.dtype)

    return pl.pallas_call(
        body, name=name, grid=(C // tc, nrow),
        in_specs=[pl.BlockSpec((tm, tc), lambda j, i: (i, j)),
                  pl.BlockSpec((SUBLANES, tc), lambda j, i: (jnp.minimum((i + 1) * hb, last8), j)),
                  pl.BlockSpec((CONV_K, tc), lambda j, i: (0, j))],
        out_specs=pl.BlockSpec((tm, tc), lambda j, i: (i, j)),
        out_shape=jax.ShapeDtypeStruct((S, C), f32),
        compiler_params=_cparams(("parallel", "parallel")),
    )(dc, dc, w)


def _halfsum(v, lane_lo):
    s0 = jnp.sum(jnp.where(lane_lo, v, 0.0), axis=1, keepdims=True)
    s1 = jnp.sum(jnp.where(lane_lo, 0.0, v), axis=1, keepdims=True)
    return jnp.where(lane_lo, s0, s1)


def _ssd_specs(S, inner, GN, nchunks, rev):
    L = CHUNK
    cm = (lambda c: nchunks - 1 - c) if rev else (lambda c: c)
    xs = pl.BlockSpec((L, inner), lambda c: (cm(c), 0))
    bb = pl.BlockSpec((L, GN), lambda c: (cm(c), inner // GN))
    cc = pl.BlockSpec((L, GN), lambda c: (cm(c), inner // GN + 1))
    row = pl.BlockSpec((L, inner), lambda c: (cm(c), 0))
    vec = pl.BlockSpec((1, inner), lambda c: (0, 0))
    st = pl.BlockSpec((None, inner, STATE), lambda c: (cm(c), 0, 0))
    return xs, bb, cc, row, vec, st


def _ssd_fwd(xbc, dtx, ax, dx, G, name):
    S, inner = dtx.shape
    GN = G * STATE
    L = CHUNK
    nchunks = S // L
    npairs = inner // LANES
    ppg = npairs // G
    assert inner % GN == 0 and L == LANES and STATE == LANES

    def body(xs_ref, b_ref, c_ref, dtx_ref, ax_ref, dx_ref, y_ref, so_ref, st_ref):
        ci = pl.program_id(0)

        @pl.when(ci == 0)
        def _():
            st_ref[...] = jnp.zeros_like(st_ref)

        r = lax.broadcasted_iota(jnp.int32, (L, L), 0)
        cidx = lax.broadcasted_iota(jnp.int32, (L, L), 1)
        tril = cidx <= r
        lane_lo = cidx < HEADDIM
        xs = xs_ref[...]
        dtv = dtx_ref[...]
        X = xs * dtv
        da = dtv * ax_ref[...]
        cs = _dot01_left(tril, da)
        cs_last = jnp.sum(da, axis=0, keepdims=True)
        so_ref[...] = st_ref[...]
        for g in range(G):
            Bg = b_ref[:, g * STATE:(g + 1) * STATE].astype(bf16)
            Cg = c_ref[:, g * STATE:(g + 1) * STATE].astype(bf16)
            CB = _dot(Cg, Bg, NT)
            for j in range(ppg):
                lo = (g * ppg + j) * LANES
                tile = cs[:, lo:lo + LANES]
                rl = pltpu.roll(tile, HEADDIM, axis=1)
                Xp = X[:, lo:lo + LANES]
                prev = st_ref[lo:lo + LANES, :]
                ypair = _dot(Cg, prev.astype(bf16), NT) * jnp.exp(tile)
                for half in (0, 1):
                    hm = lane_lo if half == 0 else jnp.logical_not(lane_lo)
                    colb = jnp.where(hm, tile, rl)
                    Lm = jnp.exp(jnp.where(tril, colb - colb.T, -1e30))
                    W = (CB * Lm).astype(bf16)
                    ypair = ypair + _dot(W, jnp.where(hm, Xp, 0.0).astype(bf16), NN)
                y_ref[:, lo:lo + LANES] = ypair + xs[:, lo:lo + LANES] * dx_ref[:, lo:lo + LANES]
                last = cs_last[:, lo:lo + LANES]
                snew = _dot((Xp * jnp.exp(last - tile)).astype(bf16), Bg, TN)
                dec_rows = jnp.broadcast_to(jnp.exp(last), (L, LANES)).T
                st_ref[lo:lo + LANES, :] = dec_rows * prev + snew

    xs_s, b_s, c_s, row_s, vec_s, st_s = _ssd_specs(S, inner, GN, nchunks, False)
    return pl.pallas_call(
        body, name=name, grid=(nchunks,),
        in_specs=[xs_s, b_s, c_s, row_s, vec_s, vec_s],
        out_specs=[row_s, st_s],
        out_shape=[jax.ShapeDtypeStruct((S, inner), f32), jax.ShapeDtypeStruct((nchunks, inner, STATE), f32)],
        scratch_shapes=[pltpu.VMEM((inner, STATE), f32)],
        compiler_params=_cparams(("arbitrary",)),
    )(xbc, xbc, xbc, dtx, ax, dx)


def _ssd_bwd(dy, y, xbc, dtx, ax, dx, states, et, G, name):
    S, inner = dtx.shape
    H = et.shape[1]
    GN = G * STATE
    Cc = inner + 2 * GN
    L = CHUNK
    nchunks = S // L
    npairs = inner // LANES
    ppg = npairs // G

    def body(dy_ref, y_ref, xs_ref, b_ref, c_ref, dtx_ref, ax_ref, dx_ref, si_ref, et_ref,
             dxbc_ref, ddt_ref, dax_ref, ddx_ref, dst_ref, dA_ref, dAl_ref, ddtp_ref):
        ci = pl.program_id(0)

        @pl.when(ci == 0)
        def _():
            dst_ref[...] = jnp.zeros_like(dst_ref)
            dax_ref[...] = jnp.zeros_like(dax_ref)
            ddx_ref[...] = jnp.zeros_like(ddx_ref)

        r = lax.broadcasted_iota(jnp.int32, (L, L), 0)
        cidx = lax.broadcasted_iota(jnp.int32, (L, L), 1)
        tril = cidx <= r
        lane_lo = cidx < HEADDIM
        lane_lo1 = lax.broadcasted_iota(jnp.int32, (1, LANES), 1) < HEADDIM
        xs = xs_ref[...]
        dtv = dtx_ref[...]
        dyv = dy_ref[...]
        X = xs * dtv
        da = dtv * ax_ref[...]
        cs = _dot01_left(tril, da)
        cs_last = jnp.sum(da, axis=0, keepdims=True)
        for g in range(G):
            Bg = b_ref[:, g * STATE:(g + 1) * STATE].astype(bf16)
            Cg = c_ref[:, g * STATE:(g + 1) * STATE].astype(bf16)
            CB = _dot(Cg, Bg, NT)
            dCB = jnp.zeros((L, L), f32)
            dBg = jnp.zeros((L, STATE), f32)
            dCg = jnp.zeros((L, STATE), f32)
            for j in range(ppg):
                lo = (g * ppg + j) * LANES
                tile = cs[:, lo:lo + LANES]
                rl = pltpu.roll(tile, HEADDIM, axis=1)
                eA = jnp.exp(tile)
                Xp = X[:, lo:lo + LANES]
                dYp = dyv[:, lo:lo + LANES]
                xsp = xs[:, lo:lo + LANES]
                prev = si_ref[lo:lo + LANES, :]
                dSn = dst_ref[lo:lo + LANES, :]
                prev_b = prev.astype(bf16)
                dSn_b = dSn.astype(bf16)
                dYe = (dYp * eA).astype(bf16)
                dCg = dCg + _dot(dYe, prev_b, NN)
                dprev = _dot(dYe, Cg, TN)
                last = cs_last[:, lo:lo + LANES]
                w = jnp.exp(last - tile)
                BdS = _dot(Bg, dSn_b, NT)
                Xw = Xp * w
                XwB = Xw * BdS
                dAl_t = _halfsum(jnp.sum(XwB, axis=0, keepdims=True), lane_lo1)
                dBg = dBg + _dot(Xw.astype(bf16), dSn_b, NN)
                dec_rows = jnp.broadcast_to(jnp.exp(last), (L, LANES)).T
                dprev = dprev + dec_rows * dSn
                rsum = jnp.sum(dSn * prev * dec_rows, axis=1, keepdims=True)
                s0 = jnp.sum(rsum[0:HEADDIM], axis=0, keepdims=True)
                s1 = jnp.sum(rsum[HEADDIM:LANES], axis=0, keepdims=True)
                dAl_t = dAl_t + jnp.where(lane_lo1, s0, s1)
                dXd = jnp.zeros((L, LANES), f32)
                for half in (0, 1):
                    hm = lane_lo if half == 0 else jnp.logical_not(lane_lo)
                    colb = jnp.where(hm, tile, rl)
                    Lm = jnp.exp(jnp.where(tril, colb - colb.T, -1e30))
                    dYh = jnp.where(hm, dYp, 0.0).astype(bf16)
                    dW = _dot(dYh, jnp.where(hm, Xp, 0.0).astype(bf16), NT)
                    dXd = dXd + _dot((CB * Lm).astype(bf16), dYh, TN)
                    dCB = dCB + dW * Lm
                yoff = _dot(Cg, prev_b, NT) * eA
                ydiag = y_ref[:, lo:lo + LANES] - xsp * dx_ref[:, lo:lo + LANES] - yoff
                dYb = dYp.astype(bf16).astype(f32)
                Xb = Xp.astype(bf16).astype(f32)
                dA_t = _halfsum(dYb * ydiag - Xb * dXd + dYp * yoff - XwB, lane_lo)
                dXp = w * BdS + dXd
                dxbc_ref[:, lo:lo + LANES] = dXp * dtv[:, lo:lo + LANES] + dYp * dx_ref[:, lo:lo + LANES]
                ddtp_ref[:, lo:lo + LANES] = dXp * xsp
                ddx_ref[:, lo:lo + LANES] += jnp.sum(dYp * xsp, axis=0, keepdims=True)
                dA_ref[:, lo:lo + LANES] = dA_t
                dAl_ref[:, lo:lo + LANES] = dAl_t
                dst_ref[lo:lo + LANES, :] = dprev
            dCBb = dCB.astype(bf16)
            dxbc_ref[:, inner + g * STATE:inner + (g + 1) * STATE] = dBg + _dot(dCBb, Cg, TN)
            dxbc_ref[:, inner + GN + g * STATE:inner + GN + (g + 1) * STATE] = dCg + _dot(dCBb, Bg, NN)
        dda = _dot01_left(cidx >= r, dA_ref[...]) + dAl_ref[...]
        ddt_full = ddtp_ref[...] + dda * ax_ref[...] * (1.0 / HEADDIM)
        ddt_ref[...] = _dot01_right(ddt_full, et_ref[...])
        dax_ref[...] += jnp.sum(dda * dtv, axis=0, keepdims=True)

    xs_s, b_s, c_s, row_s, vec_s, st_s = _ssd_specs(S, inner, GN, nchunks, True)
    return pl.pallas_call(
        body, name=name, grid=(nchunks,),
        in_specs=[row_s, row_s, xs_s, b_s, c_s, row_s, vec_s, vec_s, st_s, _full_spec(et.shape)],
        out_specs=[pl.BlockSpec((L, Cc), lambda c: (nchunks - 1 - c, 0)),
                   pl.BlockSpec((L, H), lambda c: (nchunks - 1 - c, 0)), vec_s, vec_s],
        out_shape=[jax.ShapeDtypeStruct((S, Cc), f32), jax.ShapeDtypeStruct((S, H), f32),
                   jax.ShapeDtypeStruct((1, inner), f32), jax.ShapeDtypeStruct((1, inner), f32)],
        scratch_shapes=[pltpu.VMEM((inner, STATE), f32), pltpu.VMEM((L, inner), f32),
                        pltpu.VMEM((1, inner), f32), pltpu.VMEM((L, inner), f32)],
        compiler_params=_cparams(("arbitrary",)),
    )(dy, y, xbc, xbc, xbc, dtx, ax, dx, states, et)


def _dt_fwd(dt_pre, bias, e, name):
    H, inner = e.shape

    def fn(dp, b, e_):
        dt = _softplus(dp + b)
        return dt, _dot01_right(dt, e_)

    return _rowcall(fn, name=name, rows=[dt_pre], fulls=[_row2(bias), e], out_rows=[(H, f32), (inner, f32)])


def _dt_bwd(ddt, dt_pre, bias, name):
    H = ddt.shape[1]

    def fn(dd, dp, b):
        g = dd * _sigmoid(dp + b)
        return g, jnp.sum(g, axis=0, keepdims=True)

    return _rowcall(fn, name=name, rows=[ddt, dt_pre], fulls=[_row2(bias)], out_rows=[(H, f32)], out_accs=[(1, H)])


def _gnorm_fwd(y, z, w, G, name):
    inner = y.shape[1]
    gs = inner // G

    def fn(y_, z_, w_):
        gg = y_ * _silu(z_)
        outs = []
        for g in range(G):
            sl = slice(g * gs, (g + 1) * gs)
            outs.append(_rms(gg[:, sl], w_[:, sl], LN_EPS))
        return jnp.concatenate(outs, axis=1)

    return _rowcall(fn, name=name, rows=[y, z], fulls=[_row2(w)], out_rows=[(inner, bf16)], tm=256)[0]


def _gnorm_bwd(dyn, y, z, w, G, name):
    inner = y.shape[1]
    gs = inner // G

    def fn(d_, y_, z_, w_):
        sz = _silu(z_)
        gg = y_ * sz
        dgs, dws = [], []
        for g in range(G):
            sl = slice(g * gs, (g + 1) * gs)
            dg, dw = _rms_bwd(d_[:, sl], gg[:, sl], w_[:, sl], LN_EPS)
            dgs.append(dg)
            dws.append(dw)
        dgg = jnp.concatenate(dgs, axis=1)
        return dgg * sz, dgg * y_ * _dsilu(z_), jnp.concatenate(dws, axis=1)

    return _rowcall(fn, name=name, rows=[dyn, y, z], fulls=[_row2(w)], out_rows=[(inner, f32), (inner, f32)],
                    out_accs=[(1, inner)], tm=256)


def _gmlp_parts(pre, lw, lb, I):
    hp = _gelu(pre)
    uu = hp[:, :I]
    vp = hp[:, I:]
    xc = vp - jnp.mean(vp, axis=-1, keepdims=True)
    rstd = lax.rsqrt(jnp.mean(xc * xc, axis=-1, keepdims=True) + LN_EPS)
    vhat = xc * rstd
    return uu, vhat, rstd, vhat * lw + lb


def _gmlp_mid_fwd(pre, b_in, ln_w, ln_b, w_s, bsx, name):
    S, two_i = pre.shape
    I = two_i // 2
    NG = w_s.shape[0]
    gd = I // NG
    L = CHUNK

    def body(pre_ref, bi_ref, lw_ref, lb_ref, ws_ref, bsx_ref, o_ref):
        uu, _, _, vv = _gmlp_parts(pre_ref[...] + bi_ref[...], lw_ref[...], lb_ref[...], I)
        r = lax.broadcasted_iota(jnp.int32, (L, L), 0)
        cidx = lax.broadcasted_iota(jnp.int32, (L, L), 1)
        tril = cidx <= r
        for g in range(NG):
            sl = slice(g * gd, (g + 1) * gd)
            wg = jnp.where(tril, ws_ref[g], 0.0).astype(bf16)
            mixed = _dot(wg, vv[:, sl].astype(bf16), NN) + bsx_ref[:, sl]
            o_ref[:, sl] = (uu[:, sl] * mixed).astype(o_ref.dtype)

    return pl.pallas_call(
        body, name=name, grid=(S // L,),
        in_specs=[_row_spec(L, two_i), _full_spec((1, two_i)), _full_spec((1, I)), _full_spec((1, I)), _full_spec(w_s.shape), _full_spec(bsx.shape)],
        out_specs=_row_spec(L, I), out_shape=jax.ShapeDtypeStruct((S, I), bf16),
        compiler_params=_cparams(("parallel",)),
    )(pre, _row2(b_in), _row2(ln_w), _row2(ln_b), w_s, bsx)


def _gmlp_mid_bwd(do, pre, b_in, ln_w, ln_b, w_s, bsx, name):
    S, two_i = pre.shape
    I = two_i // 2
    NG = w_s.shape[0]
    gd = I // NG
    L = CHUNK

    def body(do_ref, pre_ref, bi_ref, lw_ref, lb_ref, ws_ref, bsx_ref, dpre_ref, dbi_ref, dlw_ref, dlb_ref, dws_ref, dbs_ref, dvv_ref):
        ci = pl.program_id(0)

        @pl.when(ci == 0)
        def _():
            for ref in (dbi_ref, dlw_ref, dlb_ref, dws_ref, dbs_ref):
                ref[...] = jnp.zeros_like(ref)

        pre = pre_ref[...] + bi_ref[...]
        lw = lw_ref[...]
        uu, vhat, rstd, vv = _gmlp_parts(pre, lw, lb_ref[...], I)
        dov = do_ref[...]
        r = lax.broadcasted_iota(jnp.int32, (L, L), 0)
        cidx = lax.broadcasted_iota(jnp.int32, (L, L), 1)
        tril = cidx <= r
        duus = []
        for g in range(NG):
            sl = slice(g * gd, (g + 1) * gd)
            wg = jnp.where(tril, ws_ref[g], 0.0).astype(bf16)
            vg = vv[:, sl].astype(bf16)
            mixed = _dot(wg, vg, NN) + bsx_ref[:, sl]
            duus.append(dov[:, sl] * mixed)
            dmixed = dov[:, sl] * uu[:, sl]
            dbs_ref[:, sl] += dmixed
            dmb = dmixed.astype(bf16)
            dvv_ref[:, sl] = _dot(wg, dmb, TN)
            dws_ref[g] += jnp.where(tril, _dot(dmb, vg, NT), 0.0)
        duu = jnp.concatenate(duus, axis=1)
        dvv = dvv_ref[...]
        dlw_ref[...] += jnp.sum(dvv * vhat, axis=0, keepdims=True)
        dlb_ref[...] += jnp.sum(dvv, axis=0, keepdims=True)
        dvh = dvv * lw
        dvp = rstd * (dvh - jnp.mean(dvh, axis=-1, keepdims=True) - vhat * jnp.mean(dvh * vhat, axis=-1, keepdims=True))
        dpre = jnp.concatenate([duu, dvp], axis=1) * _dgelu(pre)
        dbi_ref[...] += jnp.sum(dpre, axis=0, keepdims=True)
        dpre_ref[...] = dpre.astype(dpre_ref.dtype)

    return pl.pallas_call(
        body, name=name, grid=(S // L,),
        in_specs=[_row_spec(L, I), _row_spec(L, two_i), _full_spec((1, two_i)), _full_spec((1, I)), _full_spec((1, I)),
                  _full_spec(w_s.shape), _full_spec(bsx.shape)],
        out_specs=[_row_spec(L, two_i), _full_spec((1, two_i)), _full_spec((1, I)), _full_spec((1, I)), _full_spec(w_s.shape), _full_spec((L, I))],
        out_shape=[jax.ShapeDtypeStruct((S, two_i), bf16), jax.ShapeDtypeStruct((1, two_i), f32), jax.ShapeDtypeStruct((1, I), f32),
                   jax.ShapeDtypeStruct((1, I), f32), jax.ShapeDtypeStruct(w_s.shape, f32), jax.ShapeDtypeStruct((L, I), f32)],
        scratch_shapes=[pltpu.VMEM((L, I), f32)],
        compiler_params=_cparams(("arbitrary",)),
    )(do, pre, _row2(b_in), _row2(ln_w), _row2(ln_b), w_s, bsx)


def _lane_group_sum(acc, eg, name):
    NG = eg.shape[1]
    return _rowcall(lambda a, e: _dot(a, e, NN, HI), name=name, rows=[acc], fulls=[eg], out_rows=[(NG, f32)])[0]


def _ffn_fwd_fused(h1, nf_w, wg, wu, wd, name):
    S, D = h1.shape
    nb, F4, _ = wg.shape
    tm = _tile(S, 512)

    def body(h_ref, nf_ref, wg_ref, wu_ref, wd_ref, h2_ref, u_ref, g_ref, up_ref, a_ref, acc_ref):
        k = pl.program_id(1)

        @pl.when(k == 0)
        def _():
            u_ref[...] = _rms(h_ref[...], nf_ref[...], RMS_EPS).astype(u_ref.dtype)
            acc_ref[...] = jnp.zeros_like(acc_ref)

        for r in range(FFN_SUBTILES):
            rs = pl.ds(r * (tm // FFN_SUBTILES), tm // FFN_SUBTILES)
            uv = u_ref[rs, :]
            g = _dot(uv, wg_ref[...], NT)
            up = _dot(uv, wu_ref[...], NT)
            a = (_silu(g) * up).astype(bf16)
            g_ref[rs, :] = g.astype(g_ref.dtype)
            up_ref[rs, :] = up.astype(up_ref.dtype)
            a_ref[rs, :] = a
            acc_ref[rs, :] += _dot(a, wd_ref[...], NN)

        @pl.when(k == nb - 1)
        def _():
            h2_ref[...] = h_ref[...] + acc_ref[...]

    row = pl.BlockSpec((tm, D), lambda i, k: (i, 0))
    wspec = pl.BlockSpec((None, F4, D), lambda i, k: (k, 0, 0))
    cspec = pl.BlockSpec((None, tm, F4), lambda i, k: (k, i, 0))
    chunk = jax.ShapeDtypeStruct((nb, S, F4), bf16)
    return pl.pallas_call(
        body, name=name, grid=(S // tm, nb),
        in_specs=[row, _full_spec((1, D)), wspec, wspec, pl.BlockSpec((None, F4, D), lambda i, k: (k, 0, 0))],
        out_specs=[row, row, cspec, cspec, cspec],
        out_shape=[jax.ShapeDtypeStruct((S, D), f32), jax.ShapeDtypeStruct((S, D), bf16), chunk, chunk, chunk],
        scratch_shapes=[pltpu.VMEM((tm, D), f32)],
        compiler_params=_cparams(("parallel", "arbitrary")),
    )(h1, _row2(nf_w), wg, wu, wd)


def _ffn_bwd_fused(dh, h1, nf_w, wd, wg, wu, G, U, name, after=None):
    S, D = dh.shape
    nb, F4, _ = wd.shape
    tm = _tile(S, 512)

    def body(dh_ref, h_ref, nf_ref, wd_ref, wg_ref, wu_ref, g_ref, up_ref, *rest):
        dg_ref, du_ref, dh1_ref, dnf_ref, acc_ref = rest[-5:]
        i, k = pl.program_id(0), pl.program_id(1)

        @pl.when(k == 0)
        def _():
            acc_ref[...] = jnp.zeros_like(acc_ref)

        for r in range(FFN_SUBTILES):
            rs = pl.ds(r * (tm // FFN_SUBTILES), tm // FFN_SUBTILES)
            dA = _dot(dh_ref[rs, :].astype(bf16), wd_ref[...], NT)
            g = g_ref[rs, :].astype(f32)
            dg = (dA * up_ref[rs, :].astype(f32) * _dsilu(g)).astype(bf16)
            du = (dA * _silu(g)).astype(bf16)
            dg_ref[rs, :] = dg
            du_ref[rs, :] = du
            acc_ref[rs, :] += _dot(dg, wg_ref[...], NN) + _dot(du, wu_ref[...], NN)

        @pl.when(k == nb - 1)
        def _():
            dx, dw = _rms_bwd(acc_ref[...], h_ref[...], nf_ref[...], RMS_EPS)
            dh1_ref[...] = dh_ref[...] + dx

            @pl.when(i == 0)
            def _():
                dnf_ref[...] = dw

            @pl.when(i > 0)
            def _():
                dnf_ref[...] += dw

    row = pl.BlockSpec((tm, D), lambda i, k: (i, 0))
    wspec = pl.BlockSpec((None, F4, D), lambda i, k: (k, 0, 0))
    cspec = pl.BlockSpec((None, tm, F4), lambda i, k: (k, i, 0))
    chunk = jax.ShapeDtypeStruct((nb, S, F4), bf16)
    return pl.pallas_call(
        body, name=name, grid=(S // tm, nb),
        in_specs=[row, row, _full_spec((1, D)), wspec, wspec, wspec, cspec, cspec] + ([] if after is None else [pl.BlockSpec(memory_space=pl.ANY)]),
        out_specs=[cspec, cspec, row, _full_spec((1, D))],
        out_shape=[chunk, chunk, jax.ShapeDtypeStruct((S, D), f32), jax.ShapeDtypeStruct((1, D), f32)],
        scratch_shapes=[pltpu.VMEM((tm, D), f32)],
        compiler_params=_cparams(("arbitrary", "arbitrary")),
    )(dh, h1, _row2(nf_w), wd, wg, wu, G, U, *([] if after is None else [after]))


def _rms_bwd_add(dres, du, h, w, name):
    D = h.shape[1]

    def fn(dr, du_, h_, w_):
        dx, dw = _rms_bwd(du_, h_, w_, RMS_EPS)
        return dr + dx, dw

    return _rowcall(fn, name=name, rows=[dres, du, h], fulls=[_row2(w)], out_rows=[(D, f32)], out_accs=[(1, D)])


def _ple_fwd(h, p_i, wp, pn, gn, wgate, name):
    D = h.shape[1]

    def fn(h_, p_, wp_, pn_, gn_, wg_):
        pe = _dot(p_.astype(bf16), wp_, NN)
        e = _rms(pe, pn_, RMS_EPS)
        q = _rms(h_, gn_, RMS_EPS)
        gate = _sigmoid(_dot(q.astype(bf16), wg_, NN))
        return h_ + gate * e, pe, gate

    return _rowcall(fn, name=name, rows=[h, p_i], fulls=[wp, _row2(pn), _row2(gn), wgate],
                    out_rows=[(D, f32), (D, f32), (D, f32)], tm=256)


def _ple_bwd(dh3, h, pe, gate, pn, gn, wgate, name, after=None):
    D = h.shape[1]

    def fn(d_, h_, pe_, gate_, pn_, gn_, wg_, *_):
        e = _rms(pe_, pn_, RMS_EPS)
        dzg = d_ * e * gate_ * (1.0 - gate_)
        dq = _dot(dzg.astype(bf16), wg_, NT)
        dxq, dgn = _rms_bwd(dq, h_, gn_, RMS_EPS)
        dpe, dpn = _rms_bwd(d_ * gate_, pe_, pn_, RMS_EPS)
        return d_ + dxq, dzg, dpe, _rms(h_, gn_, RMS_EPS), dpn, dgn

    return _rowcall(fn, name=name, rows=[dh3, h, pe, gate], fulls=[_row2(pn), _row2(gn), wgate] + ([] if after is None else [after]),
                    out_rows=[(D, f32), (D, bf16), (D, bf16), (D, bf16)], out_accs=[(1, D), (1, D)], tm=256)


def _loss_head(h, target, fn_w, name):
    D = h.shape[1]

    def fn(h_, t_, w_):
        diff = _rms(h_, w_, RMS_EPS) - t_
        loss = 0.5 * jnp.sum(jnp.mean(diff * diff, axis=-1, keepdims=True), axis=0, keepdims=True)
        dh, dw = _rms_bwd(diff * (1.0 / D), h_, w_, RMS_EPS)
        return dh, jnp.broadcast_to(loss, (1, LANES)), dw

    return _rowcall(fn, name=name, rows=[h, target], fulls=[_row2(fn_w)], out_rows=[(D, f32)], out_accs=[(1, LANES), (1, D)])


def _adamw(w, m, v, g, name):
    R, C = w.shape
    tr, tc = R, C
    while tr * tc > 256 * 1024 and tr % (2 * SUBLANES) == 0:
        tr //= 2
    while tr * tc > 256 * 1024 and tc % (2 * LANES) == 0:
        tc //= 2

    def body(w_ref, m_ref, v_ref, g_ref, d_ref, mo_ref, vo_ref):
        g = g_ref[...]
        mn = ADAM_B1 * m_ref[...] + (1.0 - ADAM_B1) * g
        vn = ADAM_B2 * v_ref[...] + (1.0 - ADAM_B2) * (g * g)
        m_hat = mn / (1.0 - ADAM_B1 ** ADAM_STEP)
        v_hat = vn / (1.0 - ADAM_B2 ** ADAM_STEP)
        d_ref[...] = -ADAM_LR * (m_hat / (jnp.sqrt(v_hat) + ADAM_EPS) + ADAM_WD * w_ref[...])
        mo_ref[...] = mn
        vo_ref[...] = vn

    spec = pl.BlockSpec((tr, tc), lambda i, j: (i, j))
    return pl.pallas_call(
        body, name=name, grid=(R // tr, C // tc), in_specs=[spec] * 4,
        out_specs=[spec] * 3, out_shape=[jax.ShapeDtypeStruct((R, C), f32)] * 3,
        compiler_params=_cparams(("parallel", "parallel")),
    )(w, m, v, g)


def _expand_onehot(n, per):
    lane = lax.broadcasted_iota(jnp.int32, (n, n * per), 1)
    row = lax.broadcasted_iota(jnp.int32, (n, n * per), 0)
    return (lane // per == row).astype(f32)


def _ssd_layer_fwd(h, nm_w, W, t):
    H = W["dt_bias"].shape[0]
    inner = H * HEADDIM
    G = (W["conv_b"].shape[0] - inner) // (2 * STATE)
    hn = _rms_fwd(h, nm_w, f"rms_mix_{t}")
    conv_dim = W["conv_b"].shape[0]
    wT = W["w_inT"]
    z = _mm(hn, wT, mode="nt", brows=(0, inner), name=f"ssd_z_{t}")
    xpre = _mm(hn, wT, mode="nt", brows=(inner, conv_dim), name=f"ssd_xbc_{t}")
    dt_pre = _mm(hn, wT, mode="nt", brows=(inner + conv_dim, H), name=f"ssd_dt_{t}")
    xbc = _conv_fwd(xpre, W["conv_w"], W["conv_b"], f"ssd_conv_{t}")
    _, dtx = _dt_fwd(dt_pre, W["dt_bias"], _expand_onehot(H, HEADDIM), f"ssd_dtx_{t}")
    a = -jnp.exp(W["a_log"])
    ax = _row2(jnp.repeat(a, HEADDIM))
    dx = _row2(jnp.repeat(W["d"], HEADDIM))
    y, states = _ssd_fwd(xbc, dtx, ax, dx, G, f"ssd_scan_{t}")
    yn = _gnorm_fwd(y, z, W["norm_w"], G, f"ssd_gnorm_{t}")
    h1 = _mm(yn, W["wout"], res=h, name=f"ssd_out_{t}")
    return h1, (h, hn, z, xpre, dt_pre, xbc, dtx, a, ax, dx, y, states, yn)


def _ssd_layer_bwd(dh1, saved, nm_w, W, t, after=None):
    h, hn, z, xpre, dt_pre, xbc, dtx, a, ax, dx, y, states, yn = saved
    H = W["dt_bias"].shape[0]
    inner = H * HEADDIM
    G = (W["conv_b"].shape[0] - inner) // (2 * STATE)
    dyn = _mm(dh1, W["wout"], mode="nt", after=after, name=f"ssd_out_dx_{t}")
    g_wout = _mm(yn, dh1, mode="tn", out_dtype=bf16, name=f"ssd_out_dw_{t}")
    dy, dz, g_normw = _gnorm_bwd(dyn, y, z, W["norm_w"], G, f"ssd_gnorm_bwd_{t}")
    dxbc, ddt, dax, ddx = _ssd_bwd(dy, y, xbc, dtx, ax, dx, states, _expand_onehot(H, HEADDIM).T, G, f"ssd_scan_bwd_{t}")
    dc, g_convw8, g_convb = _conv_bwd_dc(dxbc, xpre, W["conv_w"], W["conv_b"], f"ssd_conv_bwd_dc_{t}")
    dxpre = _conv_bwd_dx(dc, W["conv_w"], f"ssd_conv_bwd_dx_{t}")
    ddt_pre, g_dtb = _dt_bwd(ddt, dt_pre, W["dt_bias"], f"ssd_dt_bwd_{t}")
    conv_dim = W["conv_b"].shape[0]
    wT = W["w_inT"]
    g_wz = _mm(dz, hn, mode="tn", out_dtype=bf16, name=f"ssd_z_dw_{t}")
    g_wxbc = _mm(dxpre, hn, mode="tn", out_dtype=bf16, name=f"ssd_xbc_dw_{t}")
    g_wdt = _mm(ddt_pre, hn, mode="tn", out_dtype=bf16, name=f"ssd_dt_dw_{t}")
    dhn = _mm(dz, wT, brows=(0, inner), name=f"ssd_z_dx_{t}")
    dhn = _mm(dxpre, wT, brows=(inner, conv_dim), res=dhn, name=f"ssd_xbc_dx_{t}")
    dhn = _mm(ddt_pre, wT, brows=(inner + conv_dim, H), res=dhn, name=f"ssd_dt_dx_{t}")
    dh, g_nm = _rms_bwd_add(dh1, dhn, h, nm_w, f"rms_mix_bwd_{t}")
    grads = dict(
        w_inT=jnp.concatenate([g_wz, g_wxbc, g_wdt], axis=0), wout=g_wout,
        conv_w=g_convw8[:CONV_K], conv_b=g_convb[0], dt_bias=g_dtb[0],
        a_log=dax[0].reshape(H, HEADDIM)[:, 0] * a, d=jnp.sum(ddx[0].reshape(H, HEADDIM), axis=1),
        norm_w=g_normw[0], norm_mix=g_nm[0])
    return dh, grads


def _gmlp_layer_fwd(h, nm_w, W, t):
    NG, L, _ = W["w_s"].shape
    I = W["ln_w"].shape[0]
    hn = _rms_fwd(h, nm_w, f"rms_mix_{t}")
    pre = _mm(hn, W["win"], name=f"gmlp_in_{t}")
    bsx = jnp.repeat(W["b_s"].T, I // NG, axis=1)
    o = _gmlp_mid_fwd(pre, W["b_in"], W["ln_w"], W["ln_b"], W["w_s"], bsx, f"gmlp_mid_{t}")
    h1 = _mm(o, W["wout"], res=h, name=f"gmlp_out_{t}")
    return h1, (h, hn, pre, bsx, o)


def _gmlp_layer_bwd(dh1, saved, nm_w, W, t, after=None):
    h, hn, pre, bsx, o = saved
    NG = W["w_s"].shape[0]
    I = W["ln_w"].shape[0]
    do = _mm(dh1, W["wout"], mode="nt", after=after, name=f"gmlp_out_dx_{t}")
    g_wout = _mm(o, dh1, mode="tn", out_dtype=bf16, name=f"gmlp_out_dw_{t}")
    dpre, g_bin, g_lnw, g_lnb, g_ws, dbs = _gmlp_mid_bwd(do, pre, W["b_in"], W["ln_w"], W["ln_b"], W["w_s"], bsx, f"gmlp_mid_bwd_{t}")
    g_bs = _lane_group_sum(dbs, _expand_onehot(NG, I // NG).T, f"gmlp_bs_{t}").T
    g_win = _mm(hn, dpre, mode="tn", out_dtype=bf16, name=f"gmlp_in_dw_{t}")
    dhn = _mm(dpre, W["win"], mode="nt", name=f"gmlp_in_dx_{t}")
    dh, g_nm = _rms_bwd_add(dh1, dhn, h, nm_w, f"rms_mix_bwd_{t}")
    grads = dict(win=g_win, wout=g_wout, b_in=g_bin[0], ln_w=g_lnw[0], ln_b=g_lnb[0], w_s=g_ws, b_s=g_bs, norm_mix=g_nm[0])
    return dh, grads


def _ffn_fwd(h1, nf_w, W, t):
    h2, u, Gm, Um, A = _ffn_fwd_fused(h1, nf_w, W["wg"], W["wu"], W["wd"], f"ffn_fwd_{t}")
    return h2, (h1, u, Gm, Um, A)


def _ffn_bwd(dh2, saved, nf_w, W, t, after=None):
    h1, u, Gm, Um, A = saved
    dG, dU, dh1, g_nf = _ffn_bwd_fused(dh2, h1, nf_w, W["wd"], W["wg"], W["wu"], Gm, Um, f"ffn_bwd_{t}", after=after)
    g_wd = _mm(A, dh2, mode="tn", out_dtype=bf16, name=f"ffn_down_dw_{t}")
    g_wg = _mm(dG, u, mode="tn", out_dtype=bf16, name=f"ffn_gate_dw_{t}")
    g_wu = _mm(dU, u, mode="tn", out_dtype=bf16, name=f"ffn_up_dw_{t}")
    return dh1, dict(wg=g_wg, wu=g_wu, wd=g_wd, norm_ffn=g_nf[0])


def _local_step(x, p, target, norms, layer_weights, on_layer_grads=None, final_norm_grad=None):
    depth = p.shape[0]
    h = x
    saved = []
    for i in range(depth):
        Wm = layer_weights(i, "mix", h)
        if i % 2 == 0:
            h1, s_mix = _ssd_layer_fwd(h, norms["norm_mix"][i], Wm, i)
        else:
            h1, s_mix = _gmlp_layer_fwd(h, norms["norm_mix"][i], Wm, i)
        Wf = layer_weights(i, "ffn", h1)
        h2, s_ffn = _ffn_fwd(h1, norms["norm_ffn"][i], Wf, i)
        P = layer_weights(i, "ple", h2)
        h3, pe, gate = _ple_fwd(h2, p[i], P["wp"], P["pn"], P["gn"], P["wgate"], f"ple_{i}")
        saved.append((Wm, Wf, P, s_mix, s_ffn, (h2, pe, gate)))
        h = h3
    dh, loss, g_fn = _loss_head(h, target, norms["final_norm"], "loss_head")
    if final_norm_grad is not None:
        final_norm_grad[0] = g_fn[0]
    grads = [None] * depth
    tell = on_layer_grads if on_layer_grads is not None else (lambda i, part, g: None)
    after = None
    for i in reversed(range(depth)):
        Wm, Wf, P, s_mix, s_ffn, (h2, pe, gate) = saved[i]
        dh, dzg, dpe, q, g_pn, g_gn = _ple_bwd(dh, h2, pe, gate, P["pn"], P["gn"], P["wgate"], f"ple_bwd_{i}", after=after)
        g_ple = dict(wgate=_mm(q, dzg, mode="tn", out_dtype=bf16, name=f"ple_gate_dw_{i}"),
                     wp=_mm(p[i], dpe, mode="tn", out_dtype=bf16, name=f"ple_proj_dw_{i}"), pn=g_pn[0], gn=g_gn[0])
        after = tell(i, "ple", g_ple)
        dh, g_ffn = _ffn_bwd(dh, s_ffn, norms["norm_ffn"][i], Wf, i, after=after)
        after = tell(i, "ffn", g_ffn)
        if i % 2 == 0:
            dh, g_mix = _ssd_layer_bwd(dh, s_mix, norms["norm_mix"][i], Wm, i, after=after)
        else:
            dh, g_mix = _gmlp_layer_bwd(dh, s_mix, norms["norm_mix"][i], Wm, i, after=after)
        after = tell(i, "mix", g_mix)
        grads[i] = dict(mix=g_mix, ffn=g_ffn, ple=g_ple)
    return loss[0, 0], dh, g_fn[0], grads


def _flip(v, f):
    return 1 - v if f else v


_ANY = pl.BlockSpec(memory_space=pl.ANY)


_SEM = pl.BlockSpec(memory_space=pltpu.SEMAPHORE)
_DATAFLOW = pltpu.SideEffectType.DATAFLOW_SIDE_EFFECTING
_CHIP_FLIPS = ((1, 0), (0, 1), (1, 1))
_DMA = pltpu.SemaphoreType.DMA


def _structs(arrs):
    return [jax.ShapeDtypeStruct(a.shape, a.dtype) for a in arrs]


def _gather_copy(src, buf, send_sems, recv_sems, k, j, slot, x, y, c):
    c2 = src.shape[1] // 2
    fx, fy = _CHIP_FLIPS[j]
    nf = len(_CHIP_FLIPS)
    return pltpu.make_async_remote_copy(
        src_ref=src.at[:, pl.ds(c * c2, c2)], dst_ref=buf.at[slot, :, pl.ds(c * c2, c2)], send_sem=send_sems.at[nf * k + j],
        recv_sem=recv_sems.at[nf * k + j], device_id=(_flip(x, fx), _flip(y, fy), c), device_id_type=MESH)


def _gather_start(srcs, groups):
    n = len(srcs)
    ng = len(groups)
    nf = len(_CHIP_FLIPS)

    def body(*refs):
        src_refs, buf_refs, sems = refs[:n], refs[n:2 * n], refs[4 * n:]
        x, y, c = lax.axis_index("x"), lax.axis_index("y"), lax.axis_index("c")
        for gi, group in enumerate(groups):
            for k, o in enumerate(group):
                for j in range(nf):
                    _gather_copy(src_refs[o], buf_refs[o], sems[2 * gi], sems[2 * gi + 1], k, j, 2 * x + y, x, y, c).start()

    mychip = 2 * lax.axis_index("x") + lax.axis_index("y")
    inits = [lax.dynamic_update_slice(lax.empty((N_CHIPS,) + s.shape, s.dtype), s[None], (mychip, 0, 0)) for s in srcs]
    sem_shapes = [_DMA((nf * len(g),)) for g in groups for _ in range(2)]
    outs = pl.pallas_call(
        body, name="gather_start", in_specs=[_ANY] * (2 * n), out_specs=[_ANY] * (2 * n) + [_SEM] * (2 * ng),
        out_shape=_structs(srcs) + _structs(inits) + sem_shapes, input_output_aliases={i: i for i in range(2 * n)},
        compiler_params=pltpu.CompilerParams(has_side_effects=_DATAFLOW),
    )(*srcs, *inits)
    return outs[:n], outs[n:2 * n], [(outs[2 * n + 2 * gi], outs[2 * n + 2 * gi + 1]) for gi in range(ng)]


def _gather_wait(srcs, bufs, sems, after, name):
    n = len(srcs)
    nf = len(_CHIP_FLIPS)

    def body(*refs):
        src_refs, buf_refs, send_sems, recv_sems = refs[:n], refs[n:2 * n], refs[2 * n], refs[2 * n + 1]
        x, y, c = lax.axis_index("x"), lax.axis_index("y"), lax.axis_index("c")
        for k in range(n):
            for j, (fx, fy) in enumerate(_CHIP_FLIPS):
                cp = _gather_copy(src_refs[k], buf_refs[k], send_sems, recv_sems, k, j, 2 * _flip(x, fx) + _flip(y, fy), x, y, c)
                cp.wait_send()
                cp.wait_recv()

    outs = pl.pallas_call(
        body, name=name, in_specs=[_ANY] * (2 * n) + [_SEM, _SEM, _ANY], out_specs=[_ANY] * (2 * n),
        out_shape=_structs(srcs) + _structs(bufs), input_output_aliases={i: i for i in range(2 * n)},
        compiler_params=pltpu.CompilerParams(has_side_effects=_DATAFLOW),
    )(*srcs, *bufs, *sems, after)
    return outs[n:]


def _gather_forward(bufs, name):
    n = len(bufs)
    nf = len(_CHIP_FLIPS)

    def body(*refs):
        outs = refs[n:2 * n]
        send_sems, recv_sems = refs[2 * n:]
        x, y, c = lax.axis_index("x"), lax.axis_index("y"), lax.axis_index("c")

        def forward(k, j, h):
            c2 = bufs[k].shape[2] // 2
            fx, fy = _CHIP_FLIPS[j]
            part = outs[k].at[2 * _flip(x, fx) + _flip(y, fy), :, pl.ds(h * c2, c2)]
            return pltpu.make_async_remote_copy(src_ref=part, dst_ref=part, send_sem=send_sems.at[nf * k + j],
                                                recv_sem=recv_sems.at[nf * k + j], device_id=(x, y, 1 - c), device_id_type=MESH)

        sends = [forward(k, j, c) for k in range(n) for j in range(nf)]
        for cp in sends:
            cp.start()
        for k in range(n):
            for j in range(nf):
                forward(k, j, 1 - c).wait_recv()
        for cp in sends:
            cp.wait_send()

    return pl.pallas_call(
        body, name=name, in_specs=[_ANY] * n, out_specs=[_ANY] * n, out_shape=_structs(bufs),
        input_output_aliases={i: i for i in range(n)}, scratch_shapes=[_DMA((nf * n,)), _DMA((nf * n,))],
    )(*bufs)


def _half_struct(a, lead):
    return jax.ShapeDtypeStruct(lead + (a.shape[-2], a.shape[-1] // 2), a.dtype)


_DEVICE_FLIPS = tuple((f >> 2 & 1, f >> 1 & 1, f & 1) for f in range(1, N_DEV))


def _exchange_copy(srcs, lands, n, send_sems, recv_sems, i, j, slot, x, y, c):
    nf = len(_DEVICE_FLIPS)
    px, py, pc = (_flip(v, f) for v, f in zip((x, y, c), _DEVICE_FLIPS[j]))
    src = srcs[i]
    if i < n:
        c2 = src.shape[2] // 2
        src = src.at[2 * px + py, :, pl.ds(pc * c2, c2)]
    return pltpu.make_async_remote_copy(src_ref=src, dst_ref=lands[i].at[slot], send_sem=send_sems.at[nf * i + j],
                                        recv_sem=recv_sems.at[nf * i + j], device_id=(px, py, pc), device_id_type=MESH)


def _exchange_start(tensors, wholes, name):
    n, m = len(tensors), len(wholes)
    nf = len(_DEVICE_FLIPS)
    t = n + m
    land_structs = ([_half_struct(a, (N_DEV,)) for a in tensors] + [jax.ShapeDtypeStruct((N_DEV,) + w.shape, w.dtype) for w in wholes])

    def body(*refs):
        srcs, lands, send_sems, recv_sems, token = refs[:t], refs[2 * t:3 * t], refs[3 * t], refs[3 * t + 1], refs[3 * t + 2]
        x, y, c = lax.axis_index("x"), lax.axis_index("y"), lax.axis_index("c")
        for i in range(t):
            for j in range(nf):
                _exchange_copy(srcs, lands, n, send_sems, recv_sems, i, j, 4 * x + 2 * y + c, x, y, c).start()
        token[...] = jnp.zeros_like(token)

    outs = pl.pallas_call(
        body, name=name, in_specs=[_ANY] * t,
        out_specs=[_ANY] * (2 * t) + [_SEM, _SEM, pl.BlockSpec(memory_space=pltpu.VMEM)],
        out_shape=_structs(tensors) + _structs(wholes) + land_structs + [_DMA((nf * t,)), _DMA((nf * t,)),
                                                                          jax.ShapeDtypeStruct((SUBLANES, LANES), f32)],
        input_output_aliases={i: i for i in range(t)},
        compiler_params=pltpu.CompilerParams(has_side_effects=_DATAFLOW),
    )(*tensors, *wholes)
    return outs[:t], outs[t:2 * t], (outs[2 * t], outs[2 * t + 1]), outs[2 * t + 2]


def _exchange_wait(srcs, lands, n, sems, after, name):
    t = len(srcs)

    def body(*refs):
        src_refs, land_refs, send_sems, recv_sems = refs[:t], refs[t:2 * t], refs[2 * t], refs[2 * t + 1]
        x, y, c = lax.axis_index("x"), lax.axis_index("y"), lax.axis_index("c")
        for i in range(t):
            for j, (fx, fy, fc) in enumerate(_DEVICE_FLIPS):
                sender = 4 * _flip(x, fx) + 2 * _flip(y, fy) + _flip(c, fc)
                cp = _exchange_copy(src_refs, land_refs, n, send_sems, recv_sems, i, j, sender, x, y, c)
                cp.wait_send()
                cp.wait_recv()

    outs = pl.pallas_call(
        body, name=name, in_specs=[_ANY] * (2 * t) + [_SEM, _SEM, _ANY], out_specs=[_ANY] * (2 * t),
        out_shape=_structs(srcs) + _structs(lands), input_output_aliases={i: i for i in range(2 * t)},
        compiler_params=pltpu.CompilerParams(has_side_effects=_DATAFLOW),
    )(*srcs, *lands, *sems, after)
    return outs[:t], outs[t:]


def _sibling_join(bufs, name):
    flat = [(gi, l) for gi, b in enumerate(bufs) for l in range(b.shape[0])]
    n, n_buf = len(flat), len(bufs)

    def body(*refs):
        outs = refs[n_buf:2 * n_buf]
        send_sems, recv_sems = refs[2 * n_buf:]
        x, y, c = lax.axis_index("x"), lax.axis_index("y"), lax.axis_index("c")

        def push(i, h):
            gi, l = flat[i]
            c2 = bufs[gi].shape[2] // 2
            part = outs[gi].at[l, :, pl.ds(h * c2, c2)]
            return pltpu.make_async_remote_copy(src_ref=part, dst_ref=part, send_sem=send_sems.at[i], recv_sem=recv_sems.at[i],
                                                device_id=(x, y, 1 - c), device_id_type=MESH)

        sends = [push(i, c) for i in range(n)]
        for cp in sends:
            cp.start()
        for i in range(n):
            push(i, 1 - c).wait_recv()
        for cp in sends:
            cp.wait_send()

    dma = pltpu.SemaphoreType.DMA
    return pl.pallas_call(
        body, name=name, in_specs=[_ANY] * n_buf, out_specs=[_ANY] * n_buf,
        out_shape=[jax.ShapeDtypeStruct(b.shape, b.dtype) for b in bufs],
        input_output_aliases={i: i for i in range(n_buf)},
        scratch_shapes=[dma((n,)), dma((n,))],
    )(*bufs)


def _device_sum(landed, own, place, name, into=None, layer=0, layers=1):
    ndev, R, C2 = landed.shape
    tr, tc = R, C2
    while ndev * tr * tc > 1024 * 1024 and tr % (4 * SUBLANES) == 0:
        tr //= 2
    while ndev * tr * tc > 1024 * 1024 and tc % (2 * LANES) == 0:
        tc //= 2
    ncb = C2 // tc

    def body(*refs):
        place_ref, l_ref, m_ref, o_ref = refs[0], refs[1], refs[2], refs[-1]
        me = 2 * place_ref[0] + place_ref[1]
        s = jnp.where(me == 0, m_ref[...].astype(f32), l_ref[0].astype(f32))
        for d in range(1, ndev):
            s = s + jnp.where(me == d, m_ref[...].astype(f32), l_ref[d].astype(f32))
        o_ref[...] = s

    in_specs = [pl.BlockSpec((ndev, tr, tc), lambda i, j, pr: (0, i, j)),
                pl.BlockSpec((None, tr, tc), lambda i, j, pr: (pr[0], i, pr[1] * ncb + j))]
    args = [place, landed, own]
    if into is not None:
        in_specs.append(_ANY)
        args.append(into)
    return pl.pallas_call(
        body, name=name, out_shape=jax.ShapeDtypeStruct((layers, R, 2 * C2), f32),
        grid_spec=pltpu.PrefetchScalarGridSpec(
            num_scalar_prefetch=1, grid=(R // tr, ncb), in_specs=in_specs,
            out_specs=pl.BlockSpec((None, tr, tc), lambda i, j, pr: (layer, i, pr[1] * ncb + j))),
        input_output_aliases={3: 0} if into is not None else {},
        compiler_params=_cparams(("parallel", "parallel")),
    )(*args)


def _device_sum_whole(landed, own, place, name):
    ndev, R, C = landed.shape
    tr = R
    while ndev * tr * C > 1024 * 1024 and tr % (2 * SUBLANES) == 0:
        tr //= 2

    def body(place_ref, l_ref, m_ref, o_ref):
        me = 2 * place_ref[0] + place_ref[1]
        s = jnp.where(me == 0, m_ref[...], l_ref[0])
        for d in range(1, ndev):
            s = s + jnp.where(me == d, m_ref[...], l_ref[d])
        o_ref[...] = s

    return pl.pallas_call(
        body, name=name, out_shape=jax.ShapeDtypeStruct((R, C), f32),
        grid_spec=pltpu.PrefetchScalarGridSpec(
            num_scalar_prefetch=1, grid=(R // tr,),
            in_specs=[pl.BlockSpec((ndev, tr, C), lambda i, pr: (0, i, 0)), pl.BlockSpec((tr, C), lambda i, pr: (i, 0))],
            out_specs=pl.BlockSpec((tr, C), lambda i, pr: (i, 0))),
        compiler_params=_cparams(("parallel",)),
    )(place, landed, own)


PACK_COLS = 1024
PACK_ROW_MULTIPLE = 64

BIG = ("ssd_w_in", "ssd_w_out", "gmlp_w_in", "gmlp_w_out", "ffn_w_gate", "ffn_w_up", "ffn_w_down", "ple_w_proj", "ple_w_gate")
SMALL_SHARDED = ("ssd_conv_w", "gmlp_b_in", "gmlp_ln_w", "gmlp_ln_b")
REP_EARLY = "gmlp_w_s"
REP_LATE = ("norm_mix", "norm_ffn", "ssd_conv_b", "ssd_dt_bias", "ssd_a_log", "ssd_d", "ssd_norm_w", "gmlp_b_s", "ple_norm",
            "ple_gate_norm", "final_norm")
WEIGHTS = ("norm_mix", "norm_ffn", "ssd_w_in", "ssd_conv_w", "ssd_conv_b", "ssd_dt_bias", "ssd_a_log", "ssd_d", "ssd_norm_w", "ssd_w_out",
           "gmlp_w_in", "gmlp_b_in", "gmlp_ln_w", "gmlp_ln_b", "gmlp_w_s", "gmlp_b_s", "gmlp_w_out", "ffn_w_gate", "ffn_w_up",
           "ffn_w_down", "ple_w_proj", "ple_norm", "ple_gate_norm", "ple_w_gate", "final_norm")
TRANSPOSED = ("ssd_w_in", "ffn_w_gate", "ffn_w_up")


def _pack(arrs):
    flat = jnp.concatenate([a.reshape(-1).astype(f32) for a in arrs])
    per = PACK_COLS * PACK_ROW_MULTIPLE
    n = -(-flat.shape[0] // per) * per
    return jnp.pad(flat, (0, n - flat.shape[0])).reshape(-1, PACK_COLS)


def _unpack(buf, shapes):
    flat = buf.reshape(-1)
    out, o = [], 0
    for s in shapes:
        n = math.prod(s)
        out.append(flat[o:o + n].reshape(s))
        o += n
    return out


def _chip_major(g):
    r, c4 = g.shape
    return g.reshape(r, N_CHIPS, c4 // N_CHIPS).transpose(1, 0, 2)


def _from_chip_major(g):
    k, r, c = g.shape
    return g.transpose(1, 0, 2).reshape(r, k * c)


def _adamw_nd(w, m, v, g, name):
    shp = w.shape
    two = lambda a: a.reshape(-1, shp[-1])
    return [o.reshape(shp) for o in _adamw(two(w), two(m), two(v), two(g), name)]


def kernel(x, p, norm_mix, norm_ffn, ssd_w_in, ssd_conv_w, ssd_conv_b, ssd_dt_bias, ssd_a_log, ssd_d, ssd_norm_w, ssd_w_out, gmlp_w_in, gmlp_b_in, gmlp_ln_w, gmlp_ln_b, gmlp_w_s, gmlp_b_s, gmlp_w_out, ffn_w_gate, ffn_w_up, ffn_w_down, ple_w_proj, ple_norm, ple_gate_norm, ple_w_gate, final_norm, loss_target, m_norm_mix, m_norm_ffn, m_ssd_w_in, m_ssd_conv_w, m_ssd_conv_b, m_ssd_dt_bias, m_ssd_a_log, m_ssd_d, m_ssd_norm_w, m_ssd_w_out, m_gmlp_w_in, m_gmlp_b_in, m_gmlp_ln_w, m_gmlp_ln_b, m_gmlp_w_s, m_gmlp_b_s, m_gmlp_w_out, m_ffn_w_gate, m_ffn_w_up, m_ffn_w_down, m_ple_w_proj, m_ple_norm, m_ple_gate_norm, m_ple_w_gate, m_final_norm, v_norm_mix, v_norm_ffn, v_ssd_w_in, v_ssd_conv_w, v_ssd_conv_b, v_ssd_dt_bias, v_ssd_a_log, v_ssd_d, v_ssd_norm_w, v_ssd_w_out, v_gmlp_w_in, v_gmlp_b_in, v_gmlp_ln_w, v_gmlp_ln_b, v_gmlp_w_s, v_gmlp_b_s, v_gmlp_w_out, v_ffn_w_gate, v_ffn_w_up, v_ffn_w_down, v_ple_w_proj, v_ple_norm, v_ple_gate_norm, v_ple_w_gate, v_final_norm):
    given = dict(locals())
    view = lambda n, a: jnp.swapaxes(a, 1, 2) if n in TRANSPOSED else a
    w = {n: view(n, given[n]) for n in WEIGHTS}
    mom = {n: view(n, given["m_" + n]) for n in WEIGHTS}
    var = {n: view(n, given["v_" + n]) for n in WEIGHTS}
    depth = p.shape[0]
    n_ssd, n_gmlp = ssd_w_in.shape[0], gmlp_w_in.shape[0]
    inner = ssd_dt_bias.shape[1] * HEADDIM
    conv_dim = ssd_conv_b.shape[1]

    place = jnp.stack([2 * lax.axis_index("x") + lax.axis_index("y"), lax.axis_index("c")]).astype(jnp.int32)

    def part_keys(i, part):
        j = i // 2
        if part == "mix":
            names = (("ssd_w_in", j), ("ssd_w_out", j)) if i % 2 == 0 else (("gmlp_w_in", j), ("gmlp_w_out", j))
            return ((("small", 0),) if i == 0 else ()) + names
        if part == "ffn":
            return (("ffn_w_gate", i), ("ffn_w_up", i), ("ffn_w_down", i))
        return (("ple_w_proj", i), ("ple_w_gate", i))

    parts = [(i, part) for i in range(depth) for part in ("mix", "ffn", "ple")]
    keys, groups = [], {}
    for ip in parts:
        names = part_keys(*ip)
        groups[ip] = list(range(len(keys), len(keys) + len(names)))
        keys += names
    small_shapes = [w[n].shape for n in SMALL_SHARDED]
    srcs = [_pack([w[n] for n in SMALL_SHARDED]) if n == "small" else w[n][l].astype(bf16) for n, l in keys]
    srcs, landing, gather_sems = _gather_start(srcs, [groups[ip] for ip in parts])
    gather_sems = dict(zip(parts, gather_sems))
    small_full = {}

    gw = {}

    def fetch(i, which, h):
        got = []
        for part in which:
            idx = groups[(i, part)]
            got += _gather_wait([srcs[o] for o in idx], [landing[o] for o in idx], gather_sems[(i, part)], h, f"gather_wait_{part}_{i}")
        names = [keys[o] for part in which for o in groups[(i, part)]]
        gw.update(dict(zip(names, _gather_forward(got, f"gather_forward_{which[0]}_{i}"))))

    def layer_weights(i, part, h):
        if i == 0:
            fetch(i, (part,), h)
        elif part == "mix":
            fetch(i, ("mix", "ffn", "ple"), h)
        rows = lambda a: a.reshape(-1, a.shape[-1])
        j = i // 2
        if part == "ffn":
            return dict(wg=gw[("ffn_w_gate", i)], wu=gw[("ffn_w_up", i)], wd=gw[("ffn_w_down", i)])
        if part == "ple":
            return dict(wp=_from_chip_major(gw[("ple_w_proj", i)]), pn=ple_norm[i], gn=ple_gate_norm[i], wgate=rows(gw[("ple_w_gate", i)]))
        if i == 0:
            by_chip = [_unpack(gw[("small", 0)][k], small_shapes) for k in range(N_CHIPS)]
            small_full.update({n: jnp.concatenate([by_chip[k][t] for k in range(N_CHIPS)], axis=-1) for t, n in enumerate(SMALL_SHARDED)})
        if i % 2 == 0:
            return dict(w_inT=rows(gw[("ssd_w_in", j)]),
                        conv_w=small_full["ssd_conv_w"][j], conv_b=ssd_conv_b[j], dt_bias=ssd_dt_bias[j], a_log=ssd_a_log[j],
                        d=ssd_d[j], norm_w=ssd_norm_w[j], wout=rows(gw[("ssd_w_out", j)]))
        return dict(win=_from_chip_major(gw[("gmlp_w_in", j)]), b_in=small_full["gmlp_b_in"][j], ln_w=small_full["gmlp_ln_w"][j],
                    ln_b=small_full["gmlp_ln_b"][j], w_s=gmlp_w_s[j], b_s=gmlp_b_s[j], wout=rows(gw[("gmlp_w_out", j)]))

    rows4 = lambda a: a.reshape((N_CHIPS, a.shape[0] // N_CHIPS) + a.shape[1:])
    cut = lambda a, k: a[..., k * (a.shape[-1] // N_CHIPS):(k + 1) * (a.shape[-1] // N_CHIPS)]
    layer_grads = {}
    in_flight = {}
    tokens = {}
    owns = {}

    def on_layer_grads(i, part, g):
        layer_grads[(i, part)] = g
        j = i // 2
        wholes = {}
        if part == "ffn":
            chunks = {("ffn_w_gate", i): g["wg"], ("ffn_w_up", i): g["wu"], ("ffn_w_down", i): g["wd"]}
        elif part == "ple":
            chunks = {("ple_w_proj", i): _chip_major(g["wp"]), ("ple_w_gate", i): rows4(g["wgate"])}
        elif i % 2 == 0:
            chunks = {("ssd_w_in", j): rows4(g["w_inT"]), ("ssd_w_out", j): rows4(g["wout"])}
        else:
            chunks = {("gmlp_w_in", j): _chip_major(g["win"]), ("gmlp_w_out", j): rows4(g["wout"])}
        stack = lambda prt, key, layers: jnp.stack([layer_grads[(l, prt)][key] for l in layers])
        ssd, gml, every = range(0, depth, 2), range(1, depth, 2), range(depth)
        if part == "mix" and i == 1:
            wholes["rep_early"] = stack("mix", "w_s", gml).reshape(-1, w[REP_EARLY].shape[-1])
        if part == "mix" and i == 0:
            small_g = dict(ssd_conv_w=stack("mix", "conv_w", ssd), gmlp_b_in=stack("mix", "b_in", gml),
                           gmlp_ln_w=stack("mix", "ln_w", gml), gmlp_ln_b=stack("mix", "ln_b", gml))
            chunks[("small", 0)] = jnp.stack([_pack([cut(small_g[n], k) for n in SMALL_SHARDED]) for k in range(N_CHIPS)])
            rep_g = dict(
                norm_mix=stack("mix", "norm_mix", every), norm_ffn=stack("ffn", "norm_ffn", every),
                ssd_conv_b=stack("mix", "conv_b", ssd), ssd_dt_bias=stack("mix", "dt_bias", ssd), ssd_a_log=stack("mix", "a_log", ssd),
                ssd_d=stack("mix", "d", ssd), ssd_norm_w=stack("mix", "norm_w", ssd), gmlp_b_s=stack("mix", "b_s", gml),
                ple_norm=stack("ple", "pn", every), ple_gate_norm=stack("ple", "gn", every), final_norm=final_norm_grad[0])
            wholes["rep_late"] = _pack([rep_g[n] for n in REP_LATE])
        ks, wk = list(chunks), list(wholes)
        thru, lands, sems, token = _exchange_start([chunks[k] for k in ks], [wholes[k] for k in wk], f"grads_exchange_start_{part}_{i}")
        in_flight[(i, part)] = (ks, wk, thru, lands, sems)
        tokens[(i, part)] = token
        return token

    final_norm_grad = [None]
    norms = dict(norm_mix=norm_mix, norm_ffn=norm_ffn, final_norm=final_norm)
    loss_part, grad_x, g_fn, _ = _local_step(x[0], p[:, 0], loss_target[0], norms, layer_weights, on_layer_grads, final_norm_grad)
    loss = lax.psum(loss_part, ("x", "y", "c"))

    landed, res = {}, {}

    def wait_for(which, after):
        for i, part in which:
            ks, wk, thru, lands, sems = in_flight[(i, part)]
            thru, lands = _exchange_wait(thru, lands, len(ks), sems, after, f"grads_exchange_wait_{part}_{i}")
            landed.update(dict(zip(ks + wk, lands)))
            owns.update(dict(zip(ks + wk, thru)))

    def packed_update(names, gsum, tag):
        packs = [gsum] + list(_adamw(_pack([w[n] for n in names]), _pack([mom[n] for n in names]), _pack([var[n] for n in names]), gsum, tag))
        per_kind = [_unpack(pk, [w[n].shape for n in names]) for pk in packs]
        for t, n in enumerate(names):
            res[n] = [per_kind[k][t] for k in range(4)]

    def finish(big_names, with_small, tag):
        bufs = []
        for n in big_names + (("small",) if with_small else ()):
            layers = w[n].shape[0] if n != "small" else 1
            buf = None
            for l in range(layers):
                buf = _device_sum(landed[(n, l)], owns[(n, l)], place, f"grads_sum_{n}_{l}", into=buf, layer=l, layers=layers)
            bufs.append(buf)
        reduced = _sibling_join(bufs, f"grads_sibling_join_{tag}")
        for n, gsum in zip(big_names, reduced):
            res[n] = [view(n, a) for a in [gsum] + _adamw_nd(w[n], mom[n], var[n], gsum, "adamw_" + n)]
        if with_small:
            packed_update(SMALL_SHARDED, reduced[-1][0], "adamw_small_sharded")
            packed_update(REP_LATE, _device_sum_whole(landed["rep_late"], owns["rep_late"], place, "grads_sum_rep_late"), "adamw_rep_late")
        else:
            n, shp = REP_EARLY, w[REP_EARLY].shape
            two = lambda a: a.reshape(-1, shp[-1])
            gsum = _device_sum_whole(landed["rep_early"], owns["rep_early"], place, "grads_sum_rep_early")
            res[n] = [a.reshape(shp) for a in [gsum] + list(_adamw(two(w[n]), two(mom[n]), two(var[n]), gsum, "adamw_" + n))]

    last = (0, "mix")
    late_big = tuple(n for n in BIG if n.startswith("ssd_"))
    early_big = tuple(n for n in BIG if n not in late_big)
    wait_for([ip for ip in reversed(parts) if ip != last], tokens[last])
    finish(early_big, False, "early")
    wait_for([last], res[early_big[-1]][1])
    finish(late_big, True, "late")
    return (loss, grad_x[None], *[res[n][0] for n in WEIGHTS], *[res[n][1] for n in WEIGHTS],
            *[res[n][2] for n in WEIGHTS], *[res[n][3] for n in WEIGHTS])
```

```python
import math

import jax
import jax.numpy as jnp
from jax import lax
from jax.experimental import pallas as pl
from jax.experimental.pallas import tpu as pltpu

f32 = jnp.float32
bf16 = jnp.bfloat16
HI = lax.Precision.HIGHEST

LANES = 128
SUBLANES = 8
VMEM_LIMIT_BYTES = 56 * 1024 * 1024

HEADDIM = 64
STATE = 128
CHUNK = 128
CONV_K = 4
RMS_EPS = 1e-6
LN_EPS = 1e-5
ADAM_LR = 0.001
ADAM_B1 = 0.9
ADAM_B2 = 0.999
ADAM_EPS = 1e-08
ADAM_WD = 0.01
ADAM_STEP = 10

FFN_SUBTILES = 2

N_CHIPS = 4
N_DEV = 8
MESH = pl.DeviceIdType.MESH


def _cparams(sem):
    return pltpu.CompilerParams(dimension_semantics=sem, vmem_limit_bytes=VMEM_LIMIT_BYTES)


def _tile(n, want):
    if n <= want:
        return n
    t = want
    while n % t:
        t //= 2
    return t


def _row_spec(tm, c):
    return pl.BlockSpec((tm, c), lambda i: (i, 0))


def _full_spec(shape):
    nd = len(shape)
    return pl.BlockSpec(tuple(shape), lambda *_: (0,) * nd)


def _sigmoid(x):
    return 1.0 / (1.0 + jnp.exp(-x))


def _silu(x):
    return x * _sigmoid(x)


def _dsilu(x):
    s = _sigmoid(x)
    return s * (1.0 + x * (1.0 - s))


def _gelu(x):
    return 0.5 * x * (1.0 + lax.erf(x * (1.0 / math.sqrt(2.0))))


def _dgelu(x):
    return 0.5 * (1.0 + lax.erf(x * (1.0 / math.sqrt(2.0)))) + x * jnp.exp(-0.5 * x * x) * (1.0 / math.sqrt(2.0 * math.pi))


def _softplus(x):
    return jnp.maximum(x, 0.0) + jnp.log(1.0 + jnp.exp(-jnp.abs(x)))


def _rms(x, w, eps):
    r = lax.rsqrt(jnp.mean(x * x, axis=-1, keepdims=True) + eps)
    return x * r * w


def _rms_bwd(dy, x, w, eps):
    r = lax.rsqrt(jnp.mean(x * x, axis=-1, keepdims=True) + eps)
    xh = x * r
    g = dy * w
    dx = r * (g - xh * jnp.mean(g * xh, axis=-1, keepdims=True))
    dw = jnp.sum(dy * xh, axis=0, keepdims=True)
    return dx, dw


def _dot(a, b, dims=(((1,), (0,)), ((), ())), precision=None):
    return lax.dot_general(a, b, dims, precision=precision, preferred_element_type=f32)


NN = (((1,), (0,)), ((), ()))
NT = (((1,), (1,)), ((), ()))
TN = (((0,), (0,)), ((), ()))


def _split3(x):
    hi = x.astype(bf16)
    r1 = x - hi.astype(f32)
    mid = r1.astype(bf16)
    return hi, mid, (r1 - mid.astype(f32)).astype(bf16)


def _dot01_left(m01, x):
    mb = m01.astype(bf16)
    hi, mid, lo = _split3(x)
    return _dot(mb, hi, NN) + _dot(mb, mid, NN) + _dot(mb, lo, NN)


def _dot01_right(x, m01):
    mb = m01.astype(bf16)
    hi, mid, lo = _split3(x)
    return _dot(hi, mb, NN) + _dot(mid, mb, NN) + _dot(lo, mb, NN)


def _mm(a, b, *, mode="nn", out_dtype=f32, res=None, kbatch=False, brows=None, after=None, tm=1024, tn=1024, tk=1024, name):
    a3, b3 = a.ndim == 3, b.ndim == 3
    nb = a.shape[0] if a3 else (b.shape[0] if b3 else 1)
    ash, bsh = a.shape[-2:], b.shape[-2:]
    if brows is not None:
        bsh = (brows[1], bsh[1])
    if mode == "nn":
        M, K, N = ash[0], ash[1], bsh[1]
    elif mode == "nt":
        M, K, N = ash[0], ash[1], bsh[0]
    else:
        K, M, N = ash[0], ash[1], bsh[1]
    tm, tn, tk = _tile(M, tm), (N if N % LANES else _tile(N, tn)), (K if K % LANES else _tile(K, tk))
    b0 = 0
    if brows is not None:
        assert mode in ("nn", "nt") and bsh[0] == (K if mode == "nn" else N)
        blk = tk if mode == "nn" else tn
        while brows[0] % blk:
            blk //= 2
        assert blk % LANES == 0 or blk == brows[1]
        b0 = brows[0] // blk
        tn, tk = (tn, blk) if mode == "nn" else (blk, tk)
    nk = K // tk
    if kbatch:
        assert a3 and b3
        grid = (1, M // tm, N // tn, nb * nk)
        bi = lambda g, k: k // nk
        ki = lambda g, k: k % nk
    else:
        grid = (nb, M // tm, N // tn, nk)
        bi = lambda g, k: g
        ki = lambda g, k: k
    nsteps = grid[3]

    def spec(is3, blk, imap):
        if is3:
            return pl.BlockSpec((None,) + blk, lambda g, i, j, k: (bi(g, k),) + imap(i, j, ki(g, k)))
        return pl.BlockSpec(blk, lambda g, i, j, k: imap(i, j, ki(g, k)))

    if mode == "nn":
        a_spec = spec(a3, (tm, tk), lambda i, j, k: (i, k))
        b_spec = spec(b3, (tk, tn), lambda i, j, k: (k + b0, j))
        dims = NN
    elif mode == "nt":
        a_spec = spec(a3, (tm, tk), lambda i, j, k: (i, k))
        b_spec = spec(b3, (tn, tk), lambda i, j, k: (j + b0, k))
        dims = NT
    else:
        a_spec = spec(a3, (tk, tm), lambda i, j, k: (k, i))
        b_spec = spec(b3, (tk, tn), lambda i, j, k: (k, j))
        dims = TN
    out3 = (a3 or b3) and not kbatch
    if out3:
        o_spec = pl.BlockSpec((None, tm, tn), lambda g, i, j, k: (g, i, j))
        o_shape = jax.ShapeDtypeStruct((nb, M, N), out_dtype)
    else:
        o_spec = pl.BlockSpec((tm, tn), lambda g, i, j, k: (i, j))
        o_shape = jax.ShapeDtypeStruct((M, N), out_dtype)
    in_specs = [a_spec, b_spec]
    args = [a, b]
    if res is not None:
        in_specs.append(pl.BlockSpec((tm, tn), lambda g, i, j, k: (i, j)))
        args.append(res)
    if after is not None:
        in_specs.append(pl.BlockSpec(memory_space=pl.ANY))
        args.append(after)

    def body(*refs):
        a_ref, b_ref = refs[:2]
        r_ref = refs[2] if res is not None else None
        o_ref, acc_ref = refs[-2:]
        k = pl.program_id(3)

        @pl.when(k == 0)
        def _():
            acc_ref[...] = jnp.zeros_like(acc_ref)

        acc_ref[...] += _dot(a_ref[...].astype(bf16), b_ref[...].astype(bf16), dims)

        @pl.when(k == nsteps - 1)
        def _():
            r = acc_ref[...]
            if res is not None:
                r = r + r_ref[...]
            o_ref[...] = r.astype(o_ref.dtype)

    return pl.pallas_call(
        body, name=name, grid=grid, in_specs=in_specs, out_specs=o_spec, out_shape=o_shape,
        scratch_shapes=[pltpu.VMEM((tm, tn), f32)],
        compiler_params=_cparams(("parallel", "parallel", "parallel", "arbitrary")),
    )(*args)


def _rowcall(fn, *, name, rows, fulls, out_rows, out_accs=(), tm=512):
    S = rows[0].shape[0]
    tm = _tile(S, tm)
    n_r, n_f, n_or, n_oa = len(rows), len(fulls), len(out_rows), len(out_accs)

    def body(*refs):
        ins = [r[...] for r in refs[:n_r + n_f]]
        outs = fn(*ins)
        if not isinstance(outs, (tuple, list)):
            outs = (outs,)
        o_refs = refs[n_r + n_f:]
        for o_ref, v in zip(o_refs[:n_or], outs[:n_or]):
            o_ref[...] = v.astype(o_ref.dtype)
        if n_oa:
            first = pl.program_id(0) == 0

            @pl.when(first)
            def _():
                for o_ref, v in zip(o_refs[n_or:], outs[n_or:]):
                    o_ref[...] = v

            @pl.when(jnp.logical_not(first))
            def _():
                for o_ref, v in zip(o_refs[n_or:], outs[n_or:]):
                    o_ref[...] += v

    in_specs = [_row_spec(tm, r.shape[1]) for r in rows] + [_full_spec(f.shape) for f in fulls]
    out_specs = [_row_spec(tm, c) for c, _ in out_rows] + [_full_spec(s) for s in out_accs]
    out_shape = [jax.ShapeDtypeStruct((S, c), d) for c, d in out_rows] + [jax.ShapeDtypeStruct(s, f32) for s in out_accs]
    res = pl.pallas_call(
        body, name=name, grid=(S // tm,), in_specs=in_specs, out_specs=out_specs, out_shape=out_shape,
        compiler_params=_cparams(("arbitrary",) if n_oa else ("parallel",)),
    )(*rows, *fulls)
    return res


def _row2(v):
    return v.reshape(1, -1)


def _rms_fwd(h, w, name):
    D = h.shape[1]
    return _rowcall(lambda x, w_: _rms(x, w_, RMS_EPS), name=name, rows=[h], fulls=[_row2(w)], out_rows=[(D, bf16)])[0]


def _conv_taps(x, halo, w_ref, b_ref):
    row = lax.broadcasted_iota(jnp.int32, x.shape, 0)
    row8 = lax.broadcasted_iota(jnp.int32, halo.shape, 0)
    x0 = x[0:SUBLANES, :]
    acc = x * w_ref[CONV_K - 1:CONV_K, :] + b_ref[...]
    acc0 = x0 * w_ref[CONV_K - 1:CONV_K, :] + b_ref[...]
    shifted = []
    for k in range(1, CONV_K):
        wk = w_ref[CONV_K - 1 - k:CONV_K - k, :]
        xk = pltpu.roll(x, k, axis=0)
        xk0 = jnp.where(row8 < k, pltpu.roll(halo, k, axis=0), pltpu.roll(x0, k, axis=0))
        acc = acc + xk * wk
        acc0 = acc0 + xk0 * wk
        shifted.append((jnp.where(row < SUBLANES, 0.0, xk), xk0))
    return acc, acc0, shifted


def _conv_fwd(xpre, w, b, name):
    S, C = xpre.shape
    tm, tc = _tile(S, 512), _tile(C, 1024)
    hb = tm // SUBLANES

    def body(x_ref, halo_ref, w_ref, b_ref, o_ref):
        halo = jnp.where(pl.program_id(1) > 0, halo_ref[...], 0.0)
        acc, acc0, _ = _conv_taps(x_ref[...], halo, w_ref, b_ref)
        o_ref[...] = _silu(acc)
        o_ref[0:SUBLANES, :] = _silu(acc0)

    return pl.pallas_call(
        body, name=name, grid=(C // tc, S // tm),
        in_specs=[pl.BlockSpec((tm, tc), lambda j, i: (i, j)),
                  pl.BlockSpec((SUBLANES, tc), lambda j, i: (jnp.maximum(i * hb - 1, 0), j)),
                  pl.BlockSpec((CONV_K, tc), lambda j, i: (0, j)),
                  pl.BlockSpec((1, tc), lambda j, i: (0, j))],
        out_specs=pl.BlockSpec((tm, tc), lambda j, i: (i, j)),
        out_shape=jax.ShapeDtypeStruct((S, C), f32),
        compiler_params=_cparams(("parallel", "parallel")),
    )(xpre, xpre, w, _row2(b))


def _conv_bwd_dc(dxbc, xpre, w, b, name):
    S, C = xpre.shape
    tm, tc = _tile(S, 512), _tile(C, 1024)
    hb = tm // SUBLANES

    def body(d_ref, x_ref, halo_ref, w_ref, b_ref, dc_ref, dw_ref, db_ref):
        i = pl.program_id(1)
        x = x_ref[...]
        halo = jnp.where(i > 0, halo_ref[...], 0.0)
        acc, acc0, shifted = _conv_taps(x, halo, w_ref, b_ref)
        row = lax.broadcasted_iota(jnp.int32, x.shape, 0)
        d = d_ref[...]
        dc0 = d[0:SUBLANES, :] * _dsilu(acc0)
        dc = jnp.where(row < SUBLANES, 0.0, d * _dsilu(acc))
        dc_ref[...] = dc
        dc_ref[0:SUBLANES, :] = dc0
        parts = [jnp.sum(dc * x, axis=0, keepdims=True) + jnp.sum(dc0 * x[0:SUBLANES, :], axis=0, keepdims=True)]
        for xs_big, xs0 in shifted:
            parts.append(jnp.sum(dc * xs_big, axis=0, keepdims=True) + jnp.sum(dc0 * xs0, axis=0, keepdims=True))
        dw = jnp.concatenate([parts[CONV_K - 1 - k] for k in range(CONV_K)] + [jnp.zeros((SUBLANES - CONV_K, x.shape[1]), f32)], axis=0)
        db = jnp.sum(dc, axis=0, keepdims=True) + jnp.sum(dc0, axis=0, keepdims=True)

        @pl.when(i == 0)
        def _():
            dw_ref[...] = dw
            db_ref[...] = db

        @pl.when(i > 0)
        def _():
            dw_ref[...] += dw
            db_ref[...] += db

    return pl.pallas_call(
        body, name=name, grid=(C // tc, S // tm),
        in_specs=[pl.BlockSpec((tm, tc), lambda j, i: (i, j))] * 2 +
                 [pl.BlockSpec((SUBLANES, tc), lambda j, i: (jnp.maximum(i * hb - 1, 0), j)),
                  pl.BlockSpec((CONV_K, tc), lambda j, i: (0, j)),
                  pl.BlockSpec((1, tc), lambda j, i: (0, j))],
        out_specs=[pl.BlockSpec((tm, tc), lambda j, i: (i, j)),
                   pl.BlockSpec((SUBLANES, tc), lambda j, i: (0, j)),
                   pl.BlockSpec((1, tc), lambda j, i: (0, j))],
        out_shape=[jax.ShapeDtypeStruct((S, C), f32), jax.ShapeDtypeStruct((SUBLANES, C), f32), jax.ShapeDtypeStruct((1, C), f32)],
        compiler_params=_cparams(("parallel", "arbitrary")),
    )(dxbc, xpre, xpre, w, _row2(b))


def _conv_bwd_dx(dc, w, name):
    S, C = dc.shape
    tm, tc = _tile(S, 512), _tile(C, 1024)
    hb = tm // SUBLANES
    nrow = S // tm
    last8 = S // SUBLANES - 1

    def body(d_ref, nxt_ref, w_ref, o_ref):
        i = pl.program_id(1)
        d = d_ref[...]
        nxt = jnp.where(i < nrow - 1, nxt_ref[...], 0.0)
        row8 = lax.broadcasted_iota(jnp.int32, nxt.shape, 0)
        dl = d[tm - SUBLANES:tm, :]
        acc = d * w_ref[CONV_K - 1:CONV_K, :]
        accl = dl * w_ref[CONV_K - 1:CONV_K, :]
        for j in range(1, CONV_K):
            wk = w_ref[CONV_K - 1 - j:CONV_K - j, :]
            acc = acc + pltpu.roll(d, tm - j, axis=0) * wk
            accl = accl + jnp.where(row8 >= SUBLANES - j, pltpu.roll(nxt, SUBLANES - j, axis=0), pltpu.roll(dl, SUBLANES - j, axis=0)) * wk
        o_ref[...] = acc.astype(o_ref.dtype)
        o_ref[tm - SUBLANES:tm, :] = accl.astype(o_ref.dtype)

    return pl.pallas_call(
        body, name=name, grid=(C // tc, nrow),
        in_specs=[pl.BlockSpec((tm, tc), lambda j, i: (i, j)),
                  pl.BlockSpec((SUBLANES, tc), lambda j, i: (jnp.minimum((i + 1) * hb, last8), j)),
                  pl.BlockSpec((CONV_K, tc), lambda j, i: (0, j))],
        out_specs=pl.BlockSpec((tm, tc), lambda j, i: (i, j)),
        out_shape=jax.ShapeDtypeStruct((S, C), f32),
        compiler_params=_cparams(("parallel", "parallel")),
    )(dc, dc, w)


def _halfsum(v, lane_lo):
    s0 = jnp.sum(jnp.where(lane_lo, v, 0.0), axis=1, keepdims=True)
    s1 = jnp.sum(jnp.where(lane_lo, 0.0, v), axis=1, keepdims=True)
    return jnp.where(lane_lo, s0, s1)


def _ssd_specs(S, inner, GN, nchunks, rev):
    L = CHUNK
    cm = (lambda c: nchunks - 1 - c) if rev else (lambda c: c)
    xs = pl.BlockSpec((L, inner), lambda c: (cm(c), 0))
    bb = pl.BlockSpec((L, GN), lambda c: (cm(c), inner // GN))
    cc = pl.BlockSpec((L, GN), lambda c: (cm(c), inner // GN + 1))
    row = pl.BlockSpec((L, inner), lambda c: (cm(c), 0))
    vec = pl.BlockSpec((1, inner), lambda c: (0, 0))
    st = pl.BlockSpec((None, inner, STATE), lambda c: (cm(c), 0, 0))
    return xs, bb, cc, row, vec, st


def _ssd_fwd(xbc, dtx, ax, dx, G, name):
    S, inner = dtx.shape
    GN = G * STATE
    L = CHUNK
    nchunks = S // L
    npairs = inner // LANES
    ppg = npairs // G
    assert inner % GN == 0 and L == LANES and STATE == LANES

    def body(xs_ref, b_ref, c_ref, dtx_ref, ax_ref, dx_ref, y_ref, so_ref, st_ref):
        ci = pl.program_id(0)

        @pl.when(ci == 0)
        def _():
            st_ref[...] = jnp.zeros_like(st_ref)

        r = lax.broadcasted_iota(jnp.int32, (L, L), 0)
        cidx = lax.broadcasted_iota(jnp.int32, (L, L), 1)
        tril = cidx <= r
        lane_lo = cidx < HEADDIM
        xs = xs_ref[...]
        dtv = dtx_ref[...]
        X = xs * dtv
        da = dtv * ax_ref[...]
        cs = _dot01_left(tril, da)
        cs_last = jnp.sum(da, axis=0, keepdims=True)
        so_ref[...] = st_ref[...]
        for g in range(G):
            Bg = b_ref[:, g * STATE:(g + 1) * STATE].astype(bf16)
            Cg = c_ref[:, g * STATE:(g + 1) * STATE].astype(bf16)
            CB = _dot(Cg, Bg, NT)
            for j in range(ppg):
                lo = (g * ppg + j) * LANES
                tile = cs[:, lo:lo + LANES]
                rl = pltpu.roll(tile, HEADDIM, axis=1)
                Xp = X[:, lo:lo + LANES]
                prev = st_ref[lo:lo + LANES, :]
                ypair = _dot(Cg, prev.astype(bf16), NT) * jnp.exp(tile)
                for half in (0, 1):
                    hm = lane_lo if half == 0 else jnp.logical_not(lane_lo)
                    colb = jnp.where(hm, tile, rl)
                    Lm = jnp.exp(jnp.where(tril, colb - colb.T, -1e30))
                    W = (CB * Lm).astype(bf16)
                    ypair = ypair + _dot(W, jnp.where(hm, Xp, 0.0).astype(bf16), NN)
                y_ref[:, lo:lo + LANES] = ypair + xs[:, lo:lo + LANES] * dx_ref[:, lo:lo + LANES]
                last = cs_last[:, lo:lo + LANES]
                snew = _dot((Xp * jnp.exp(last - tile)).astype(bf16), Bg, TN)
                dec_rows = jnp.broadcast_to(jnp.exp(last), (L, LANES)).T
                st_ref[lo:lo + LANES, :] = dec_rows * prev + snew

    xs_s, b_s, c_s, row_s, vec_s, st_s = _ssd_specs(S, inner, GN, nchunks, False)
    return pl.pallas_call(
        body, name=name, grid=(nchunks,),
        in_specs=[xs_s, b_s, c_s, row_s, vec_s, vec_s],
        out_specs=[row_s, st_s],
        out_shape=[jax.ShapeDtypeStruct((S, inner), f32), jax.ShapeDtypeStruct((nchunks, inner, STATE), f32)],
        scratch_shapes=[pltpu.VMEM((inner, STATE), f32)],
        compiler_params=_cparams(("arbitrary",)),
    )(xbc, xbc, xbc, dtx, ax, dx)


def _ssd_bwd(dy, y, xbc, dtx, ax, dx, states, et, G, name):
    S, inner = dtx.shape
    H = et.shape[1]
    GN = G * STATE
    Cc = inner + 2 * GN
    L = CHUNK
    nchunks = S // L
    npairs = inner // LANES
    ppg = npairs // G

    def body(dy_ref, y_ref, xs_ref, b_ref, c_ref, dtx_ref, ax_ref, dx_ref, si_ref, et_ref,
             dxbc_ref, ddt_ref, dax_ref, ddx_ref, dst_ref, dA_ref, dAl_ref, ddtp_ref):
        ci = pl.program_id(0)

        @pl.when(ci == 0)
        def _():
            dst_ref[...] = jnp.zeros_like(dst_ref)
            dax_ref[...] = jnp.zeros_like(dax_ref)
            ddx_ref[...] = jnp.zeros_like(ddx_ref)

        r = lax.broadcasted_iota(jnp.int32, (L, L), 0)
        cidx = lax.broadcasted_iota(jnp.int32, (L, L), 1)
        tril = cidx <= r
        lane_lo = cidx < HEADDIM
        lane_lo1 = lax.broadcasted_iota(jnp.int32, (1, LANES), 1) < HEADDIM
        xs = xs_ref[...]
        dtv = dtx_ref[...]
        dyv = dy_ref[...]
        X = xs * dtv
        da = dtv * ax_ref[...]
        cs = _dot01_left(tril, da)
        cs_last = jnp.sum(da, axis=0, keepdims=True)
        for g in range(G):
            Bg = b_ref[:, g * STATE:(g + 1) * STATE].astype(bf16)
            Cg = c_ref[:, g * STATE:(g + 1) * STATE].astype(bf16)
            CB = _dot(Cg, Bg, NT)
            dCB = jnp.zeros((L, L), f32)
            dBg = jnp.zeros((L, STATE), f32)
            dCg = jnp.zeros((L, STATE), f32)
            for j in range(ppg):
                lo = (g * ppg + j) * LANES
                tile = cs[:, lo:lo + LANES]
                rl = pltpu.roll(tile, HEADDIM, axis=1)
                eA = jnp.exp(tile)
                Xp = X[:, lo:lo + LANES]
                dYp = dyv[:, lo:lo + LANES]
                xsp = xs[:, lo:lo + LANES]
                prev = si_ref[lo:lo + LANES, :]
                dSn = dst_ref[lo:lo + LANES, :]
                prev_b = prev.astype(bf16)
                dSn_b = dSn.astype(bf16)
                dYe = (dYp * eA).astype(bf16)
                dCg = dCg + _dot(dYe, prev_b, NN)
                dprev = _dot(dYe, Cg, TN)
                last = cs_last[:, lo:lo + LANES]
                w = jnp.exp(last - tile)
                BdS = _dot(Bg, dSn_b, NT)
                Xw = Xp * w
                XwB = Xw * BdS
                dAl_t = _halfsum(jnp.sum(XwB, axis=0, keepdims=True), lane_lo1)
                dBg = dBg + _dot(Xw.astype(bf16), dSn_b, NN)
                dec_rows = jnp.broadcast_to(jnp.exp(last), (L, LANES)).T
                dprev = dprev + dec_rows * dSn
                rsum = jnp.sum(dSn * prev * dec_rows, axis=1, keepdims=True)
                s0 = jnp.sum(rsum[0:HEADDIM], axis=0, keepdims=True)
                s1 = jnp.sum(rsum[HEADDIM:LANES], axis=0, keepdims=True)
                dAl_t = dAl_t + jnp.where(lane_lo1, s0, s1)
                dXd = jnp.zeros((L, LANES), f32)
                for half in (0, 1):
                    hm = lane_lo if half == 0 else jnp.logical_not(lane_lo)
                    colb = jnp.where(hm, tile, rl)
                    Lm = jnp.exp(jnp.where(tril, colb - colb.T, -1e30))
                    dYh = jnp.where(hm, dYp, 0.0).astype(bf16)
                    dW = _dot(dYh, jnp.where(hm, Xp, 0.0).astype(bf16), NT)
                    dXd = dXd + _dot((CB * Lm).astype(bf16), dYh, TN)
                    dCB = dCB + dW * Lm
                yoff = _dot(Cg, prev_b, NT) * eA
                ydiag = y_ref[:, lo:lo + LANES] - xsp * dx_ref[:, lo:lo + LANES] - yoff
                dYb = dYp.astype(bf16).astype(f32)
                Xb = Xp.astype(bf16).astype(f32)
                dA_t = _halfsum(dYb * ydiag - Xb * dXd + dYp * yoff - XwB, lane_lo)
                dXp = w * BdS + dXd
                dxbc_ref[:, lo:lo + LANES] = dXp * dtv[:, lo:lo + LANES] + dYp * dx_ref[:, lo:lo + LANES]
                ddtp_ref[:, lo:lo + LANES] = dXp * xsp
                ddx_ref[:, lo:lo + LANES] += jnp.sum(dYp * xsp, axis=0, keepdims=True)
                dA_ref[:, lo:lo + LANES] = dA_t
                dAl_ref[:, lo:lo + LANES] = dAl_t
                dst_ref[lo:lo + LANES, :] = dprev
            dCBb = dCB.astype(bf16)
            dxbc_ref[:, inner + g * STATE:inner + (g + 1) * STATE] = dBg + _dot(dCBb, Cg, TN)
            dxbc_ref[:, inner + GN + g * STATE:inner + GN + (g + 1) * STATE] = dCg + _dot(dCBb, Bg, NN)
        dda = _dot01_left(cidx >= r, dA_ref[...]) + dAl_ref[...]
        ddt_full = ddtp_ref[...] + dda * ax_ref[...] * (1.0 / HEADDIM)
        ddt_ref[...] = _dot01_right(ddt_full, et_ref[...])
        dax_ref[...] += jnp.sum(dda * dtv, axis=0, keepdims=True)

    xs_s, b_s, c_s, row_s, vec_s, st_s = _ssd_specs(S, inner, GN, nchunks, True)
    return pl.pallas_call(
        body, name=name, grid=(nchunks,),
        in_specs=[row_s, row_s, xs_s, b_s, c_s, row_s, vec_s, vec_s, st_s, _full_spec(et.shape)],
        out_specs=[pl.BlockSpec((L, Cc), lambda c: (nchunks - 1 - c, 0)),
                   pl.BlockSpec((L, H), lambda c: (nchunks - 1 - c, 0)), vec_s, vec_s],
        out_shape=[jax.ShapeDtypeStruct((S, Cc), f32), jax.ShapeDtypeStruct((S, H), f32),
                   jax.ShapeDtypeStruct((1, inner), f32), jax.ShapeDtypeStruct((1, inner), f32)],
        scratch_shapes=[pltpu.VMEM((inner, STATE), f32), pltpu.VMEM((L, inner), f32),
                        pltpu.VMEM((1, inner), f32), pltpu.VMEM((L, inner), f32)],
        compiler_params=_cparams(("arbitrary",)),
    )(dy, y, xbc, xbc, xbc, dtx, ax, dx, states, et)


def _dt_fwd(dt_pre, bias, e, name):
    H, inner = e.shape

    def fn(dp, b, e_):
        dt = _softplus(dp + b)
        return dt, _dot01_right(dt, e_)

    return _rowcall(fn, name=name, rows=[dt_pre], fulls=[_row2(bias), e], out_rows=[(H, f32), (inner, f32)])


def _dt_bwd(ddt, dt_pre, bias, name):
    H = ddt.shape[1]

    def fn(dd, dp, b):
        g = dd * _sigmoid(dp + b)
        return g, jnp.sum(g, axis=0, keepdims=True)

    return _rowcall(fn, name=name, rows=[ddt, dt_pre], fulls=[_row2(bias)], out_rows=[(H, f32)], out_accs=[(1, H)])


def _gnorm_fwd(y, z, w, G, name):
    inner = y.shape[1]
    gs = inner // G

    def fn(y_, z_, w_):
        gg = y_ * _silu(z_)
        outs = []
        for g in range(G):
            sl = slice(g * gs, (g + 1) * gs)
            outs.append(_rms(gg[:, sl], w_[:, sl], LN_EPS))
        return jnp.concatenate(outs, axis=1)

    return _rowcall(fn, name=name, rows=[y, z], fulls=[_row2(w)], out_rows=[(inner, bf16)], tm=256)[0]


def _gnorm_bwd(dyn, y, z, w, G, name):
    inner = y.shape[1]
    gs = inner // G

    def fn(d_, y_, z_, w_):
        sz = _silu(z_)
        gg = y_ * sz
        dgs, dws = [], []
        for g in range(G):
            sl = slice(g * gs, (g + 1) * gs)
            dg, dw = _rms_bwd(d_[:, sl], gg[:, sl], w_[:, sl], LN_EPS)
            dgs.append(dg)
            dws.append(dw)
        dgg = jnp.concatenate(dgs, axis=1)
        return dgg * sz, dgg * y_ * _dsilu(z_), jnp.concatenate(dws, axis=1)

    return _rowcall(fn, name=name, rows=[dyn, y, z], fulls=[_row2(w)], out_rows=[(inner, f32), (inner, f32)],
                    out_accs=[(1, inner)], tm=256)


def _gmlp_parts(pre, lw, lb, I):
    hp = _gelu(pre)
    uu = hp[:, :I]
    vp = hp[:, I:]
    xc = vp - jnp.mean(vp, axis=-1, keepdims=True)
    rstd = lax.rsqrt(jnp.mean(xc * xc, axis=-1, keepdims=True) + LN_EPS)
    vhat = xc * rstd
    return uu, vhat, rstd, vhat * lw + lb


def _gmlp_mid_fwd(pre, b_in, ln_w, ln_b, w_s, bsx, name):
    S, two_i = pre.shape
    I = two_i // 2
    NG = w_s.shape[0]
    gd = I // NG
    L = CHUNK

    def body(pre_ref, bi_ref, lw_ref, lb_ref, ws_ref, bsx_ref, o_ref):
        uu, _, _, vv = _gmlp_parts(pre_ref[...] + bi_ref[...], lw_ref[...], lb_ref[...], I)
        r = lax.broadcasted_iota(jnp.int32, (L, L), 0)
        cidx = lax.broadcasted_iota(jnp.int32, (L, L), 1)
        tril = cidx <= r
        for g in range(NG):
            sl = slice(g * gd, (g + 1) * gd)
            wg = jnp.where(tril, ws_ref[g], 0.0).astype(bf16)
            mixed = _dot(wg, vv[:, sl].astype(bf16), NN) + bsx_ref[:, sl]
            o_ref[:, sl] = (uu[:, sl] * mixed).astype(o_ref.dtype)

    return pl.pallas_call(
        body, name=name, grid=(S // L,),
        in_specs=[_row_spec(L, two_i), _full_spec((1, two_i)), _full_spec((1, I)), _full_spec((1, I)), _full_spec(w_s.shape), _full_spec(bsx.shape)],
        out_specs=_row_spec(L, I), out_shape=jax.ShapeDtypeStruct((S, I), bf16),
        compiler_params=_cparams(("parallel",)),
    )(pre, _row2(b_in), _row2(ln_w), _row2(ln_b), w_s, bsx)


def _gmlp_mid_bwd(do, pre, b_in, ln_w, ln_b, w_s, bsx, name):
    S, two_i = pre.shape
    I = two_i // 2
    NG = w_s.shape[0]
    gd = I // NG
    L = CHUNK

    def body(do_ref, pre_ref, bi_ref, lw_ref, lb_ref, ws_ref, bsx_ref, dpre_ref, dbi_ref, dlw_ref, dlb_ref, dws_ref, dbs_ref, dvv_ref):
        ci = pl.program_id(0)

        @pl.when(ci == 0)
        def _():
            for ref in (dbi_ref, dlw_ref, dlb_ref, dws_ref, dbs_ref):
                ref[...] = jnp.zeros_like(ref)

        pre = pre_ref[...] + bi_ref[...]
        lw = lw_ref[...]
        uu, vhat, rstd, vv = _gmlp_parts(pre, lw, lb_ref[...], I)
        dov = do_ref[...]
        r = lax.broadcasted_iota(jnp.int32, (L, L), 0)
        cidx = lax.broadcasted_iota(jnp.int32, (L, L), 1)
        tril = cidx <= r
        duus = []
        for g in range(NG):
            sl = slice(g * gd, (g + 1) * gd)
            wg = jnp.where(tril, ws_ref[g], 0.0).astype(bf16)
            vg = vv[:, sl].astype(bf16)
            mixed = _dot(wg, vg, NN) + bsx_ref[:, sl]
            duus.append(dov[:, sl] * mixed)
            dmixed = dov[:, sl] * uu[:, sl]
            dbs_ref[:, sl] += dmixed
            dmb = dmixed.astype(bf16)
            dvv_ref[:, sl] = _dot(wg, dmb, TN)
            dws_ref[g] += jnp.where(tril, _dot(dmb, vg, NT), 0.0)
        duu = jnp.concatenate(duus, axis=1)
        dvv = dvv_ref[...]
        dlw_ref[...] += jnp.sum(dvv * vhat, axis=0, keepdims=True)
        dlb_ref[...] += jnp.sum(dvv, axis=0, keepdims=True)
        dvh = dvv * lw
        dvp = rstd * (dvh - jnp.mean(dvh, axis=-1, keepdims=True) - vhat * jnp.mean(dvh * vhat, axis=-1, keepdims=True))
        dpre = jnp.concatenate([duu, dvp], axis=1) * _dgelu(pre)
        dbi_ref[...] += jnp.sum(dpre, axis=0, keepdims=True)
        dpre_ref[...] = dpre.astype(dpre_ref.dtype)

    return pl.pallas_call(
        body, name=name, grid=(S // L,),
        in_specs=[_row_spec(L, I), _row_spec(L, two_i), _full_spec((1, two_i)), _full_spec((1, I)), _full_spec((1, I)),
                  _full_spec(w_s.shape), _full_spec(bsx.shape)],
        out_specs=[_row_spec(L, two_i), _full_spec((1, two_i)), _full_spec((1, I)), _full_spec((1, I)), _full_spec(w_s.shape), _full_spec((L, I))],
        out_shape=[jax.ShapeDtypeStruct((S, two_i), bf16), jax.ShapeDtypeStruct((1, two_i), f32), jax.ShapeDtypeStruct((1, I), f32),
                   jax.ShapeDtypeStruct((1, I), f32), jax.ShapeDtypeStruct(w_s.shape, f32), jax.ShapeDtypeStruct((L, I), f32)],
        scratch_shapes=[pltpu.VMEM((L, I), f32)],
        compiler_params=_cparams(("arbitrary",)),
    )(do, pre, _row2(b_in), _row2(ln_w), _row2(ln_b), w_s, bsx)


def _lane_group_sum(acc, eg, name):
    NG = eg.shape[1]
    return _rowcall(lambda a, e: _dot(a, e, NN, HI), name=name, rows=[acc], fulls=[eg], out_rows=[(NG, f32)])[0]


def _ffn_fwd_fused(h1, nf_w, wg, wu, wd, name):
    S, D = h1.shape
    nb, F4, _ = wg.shape
    tm = _tile(S, 512)

    def body(h_ref, nf_ref, wg_ref, wu_ref, wd_ref, h2_ref, u_ref, g_ref, up_ref, a_ref, acc_ref):
        k = pl.program_id(1)

        @pl.when(k == 0)
        def _():
            u_ref[...] = _rms(h_ref[...], nf_ref[...], RMS_EPS).astype(u_ref.dtype)
            acc_ref[...] = jnp.zeros_like(acc_ref)

        for r in range(FFN_SUBTILES):
            rs = pl.ds(r * (tm // FFN_SUBTILES), tm // FFN_SUBTILES)
            uv = u_ref[rs, :]
            g = _dot(uv, wg_ref[...], NT)
            up = _dot(uv, wu_ref[...], NT)
            a = (_silu(g) * up).astype(bf16)
            g_ref[rs, :] = g.astype(g_ref.dtype)
            up_ref[rs, :] = up.astype(up_ref.dtype)
            a_ref[rs, :] = a
            acc_ref[rs, :] += _dot(a, wd_ref[...], NN)

        @pl.when(k == nb - 1)
        def _():
            h2_ref[...] = h_ref[...] + acc_ref[...]

    row = pl.BlockSpec((tm, D), lambda i, k: (i, 0))
    wspec = pl.BlockSpec((None, F4, D), lambda i, k: (k, 0, 0))
    cspec = pl.BlockSpec((None, tm, F4), lambda i, k: (k, i, 0))
    chunk = jax.ShapeDtypeStruct((nb, S, F4), bf16)
    return pl.pallas_call(
        body, name=name, grid=(S // tm, nb),
        in_specs=[row, _full_spec((1, D)), wspec, wspec, pl.BlockSpec((None, F4, D), lambda i, k: (k, 0, 0))],
        out_specs=[row, row, cspec, cspec, cspec],
        out_shape=[jax.ShapeDtypeStruct((S, D), f32), jax.ShapeDtypeStruct((S, D), bf16), chunk, chunk, chunk],
        scratch_shapes=[pltpu.VMEM((tm, D), f32)],
        compiler_params=_cparams(("parallel", "arbitrary")),
    )(h1, _row2(nf_w), wg, wu, wd)


def _ffn_bwd_fused(dh, h1, nf_w, wd, wg, wu, G, U, name, after=None):
    S, D = dh.shape
    nb, F4, _ = wd.shape
    tm = _tile(S, 512)

    def body(dh_ref, h_ref, nf_ref, wd_ref, wg_ref, wu_ref, g_ref, up_ref, *rest):
        dg_ref, du_ref, dh1_ref, dnf_ref, acc_ref = rest[-5:]
        i, k = pl.program_id(0), pl.program_id(1)

        @pl.when(k == 0)
        def _():
            acc_ref[...] = jnp.zeros_like(acc_ref)

        for r in range(FFN_SUBTILES):
            rs = pl.ds(r * (tm // FFN_SUBTILES), tm // FFN_SUBTILES)
            dA = _dot(dh_ref[rs, :].astype(bf16), wd_ref[...], NT)
            g = g_ref[rs, :].astype(f32)
            dg = (dA * up_ref[rs, :].astype(f32) * _dsilu(g)).astype(bf16)
            du = (dA * _silu(g)).astype(bf16)
            dg_ref[rs, :] = dg
            du_ref[rs, :] = du
            acc_ref[rs, :] += _dot(dg, wg_ref[...], NN) + _dot(du, wu_ref[...], NN)

        @pl.when(k == nb - 1)
        def _():
            dx, dw = _rms_bwd(acc_ref[...], h_ref[...], nf_ref[...], RMS_EPS)
            dh1_ref[...] = dh_ref[...] + dx

            @pl.when(i == 0)
            def _():
                dnf_ref[...] = dw

            @pl.when(i > 0)
            def _():
                dnf_ref[...] += dw

    row = pl.BlockSpec((tm, D), lambda i, k: (i, 0))
    wspec = pl.BlockSpec((None, F4, D), lambda i, k: (k, 0, 0))
    cspec = pl.BlockSpec((None, tm, F4), lambda i, k: (k, i, 0))
    chunk = jax.ShapeDtypeStruct((nb, S, F4), bf16)
    return pl.pallas_call(
        body, name=name, grid=(S // tm, nb),
        in_specs=[row, row, _full_spec((1, D)), wspec, wspec, wspec, cspec, cspec] + ([] if after is None else [pl.BlockSpec(memory_space=pl.ANY)]),
        out_specs=[cspec, cspec, row, _full_spec((1, D))],
        out_shape=[chunk, chunk, jax.ShapeDtypeStruct((S, D), f32), jax.ShapeDtypeStruct((1, D), f32)],
        scratch_shapes=[pltpu.VMEM((tm, D), f32)],
        compiler_params=_cparams(("arbitrary", "arbitrary")),
    )(dh, h1, _row2(nf_w), wd, wg, wu, G, U, *([] if after is None else [after]))


def _rms_bwd_add(dres, du, h, w, name):
    D = h.shape[1]

    def fn(dr, du_, h_, w_):
        dx, dw = _rms_bwd(du_, h_, w_, RMS_EPS)
        return dr + dx, dw

    return _rowcall(fn, name=name, rows=[dres, du, h], fulls=[_row2(w)], out_rows=[(D, f32)], out_accs=[(1, D)])


def _ple_fwd(h, p_i, wp, pn, gn, wgate, name):
    D = h.shape[1]

    def fn(h_, p_, wp_, pn_, gn_, wg_):
        pe = _dot(p_.astype(bf16), wp_, NN)
        e = _rms(pe, pn_, RMS_EPS)
        q = _rms(h_, gn_, RMS_EPS)
        gate = _sigmoid(_dot(q.astype(bf16), wg_, NN))
        return h_ + gate * e, pe, gate

    return _rowcall(fn, name=name, rows=[h, p_i], fulls=[wp, _row2(pn), _row2(gn), wgate],
                    out_rows=[(D, f32), (D, bf16), (D, bf16)], tm=256)


def _ple_bwd(dh3, h, pe, gate, pn, gn, wgate, name, after=None):
    D = h.shape[1]

    def fn(d_, h_, pe_, gate_, pn_, gn_, wg_, *_):
        pe_, gate_ = pe_.astype(f32), gate_.astype(f32)
        e = _rms(pe_, pn_, RMS_EPS)
        dzg = d_ * e * gate_ * (1.0 - gate_)
        dq = _dot(dzg.astype(bf16), wg_, NT)
        dxq, dgn = _rms_bwd(dq, h_, gn_, RMS_EPS)
        dpe, dpn = _rms_bwd(d_ * gate_, pe_, pn_, RMS_EPS)
        return d_ + dxq, dzg, dpe, _rms(h_, gn_, RMS_EPS), dpn, dgn

    return _rowcall(fn, name=name, rows=[dh3, h, pe, gate], fulls=[_row2(pn), _row2(gn), wgate] + ([] if after is None else [after]),
                    out_rows=[(D, f32), (D, bf16), (D, bf16), (D, bf16)], out_accs=[(1, D), (1, D)], tm=256)


def _loss_head(h, target, fn_w, name):
    D = h.shape[1]

    def fn(h_, t_, w_):
        diff = _rms(h_, w_, RMS_EPS) - t_
        loss = 0.5 * jnp.sum(jnp.mean(diff * diff, axis=-1, keepdims=True), axis=0, keepdims=True)
        dh, dw = _rms_bwd(diff * (1.0 / D), h_, w_, RMS_EPS)
        return dh, jnp.broadcast_to(loss, (1, LANES)), dw

    return _rowcall(fn, name=name, rows=[h, target], fulls=[_row2(fn_w)], out_rows=[(D, f32)], out_accs=[(1, LANES), (1, D)])


def _adamw(w, m, v, g, name):
    R, C = w.shape
    tr, tc = R, C
    while tr * tc > 256 * 1024 and tr % (2 * SUBLANES) == 0:
        tr //= 2
    while tr * tc > 256 * 1024 and tc % (2 * LANES) == 0:
        tc //= 2

    def body(w_ref, m_ref, v_ref, g_ref, d_ref, mo_ref, vo_ref):
        g = g_ref[...]
        mn = ADAM_B1 * m_ref[...] + (1.0 - ADAM_B1) * g
        vn = ADAM_B2 * v_ref[...] + (1.0 - ADAM_B2) * (g * g)
        m_hat = mn / (1.0 - ADAM_B1 ** ADAM_STEP)
        v_hat = vn / (1.0 - ADAM_B2 ** ADAM_STEP)
        d_ref[...] = -ADAM_LR * (m_hat / (jnp.sqrt(v_hat) + ADAM_EPS) + ADAM_WD * w_ref[...])
        mo_ref[...] = mn
        vo_ref[...] = vn

    spec = pl.BlockSpec((tr, tc), lambda i, j: (i, j))
    return pl.pallas_call(
        body, name=name, grid=(R // tr, C // tc), in_specs=[spec] * 4,
        out_specs=[spec] * 3, out_shape=[jax.ShapeDtypeStruct((R, C), f32)] * 3,
        compiler_params=_cparams(("parallel", "parallel")),
    )(w, m, v, g)


def _expand_onehot(n, per):
    lane = lax.broadcasted_iota(jnp.int32, (n, n * per), 1)
    row = lax.broadcasted_iota(jnp.int32, (n, n * per), 0)
    return (lane // per == row).astype(f32)


def _ssd_layer_fwd(h, nm_w, W, t):
    H = W["dt_bias"].shape[0]
    inner = H * HEADDIM
    G = (W["conv_b"].shape[0] - inner) // (2 * STATE)
    hn = _rms_fwd(h, nm_w, f"rms_mix_{t}")
    conv_dim = W["conv_b"].shape[0]
    wT = W["w_inT"]
    z = _mm(hn, wT, mode="nt", brows=(0, inner), name=f"ssd_z_{t}")
    xpre = _mm(hn, wT, mode="nt", brows=(inner, conv_dim), name=f"ssd_xbc_{t}")
    dt_pre = _mm(hn, wT, mode="nt", brows=(inner + conv_dim, H), name=f"ssd_dt_{t}")
    xbc = _conv_fwd(xpre, W["conv_w"], W["conv_b"], f"ssd_conv_{t}")
    _, dtx = _dt_fwd(dt_pre, W["dt_bias"], _expand_onehot(H, HEADDIM), f"ssd_dtx_{t}")
    a = -jnp.exp(W["a_log"])
    ax = _row2(jnp.repeat(a, HEADDIM))
    dx = _row2(jnp.repeat(W["d"], HEADDIM))
    y, states = _ssd_fwd(xbc, dtx, ax, dx, G, f"ssd_scan_{t}")
    yn = _gnorm_fwd(y, z, W["norm_w"], G, f"ssd_gnorm_{t}")
    h1 = _mm(yn, W["wout"], res=h, name=f"ssd_out_{t}")
    return h1, (h, hn, z, xpre, dt_pre, xbc, dtx, a, ax, dx, y, states, yn)


def _ssd_layer_bwd(dh1, saved, nm_w, W, t, after=None):
    h, hn, z, xpre, dt_pre, xbc, dtx, a, ax, dx, y, states, yn = saved
    H = W["dt_bias"].shape[0]
    inner = H * HEADDIM
    G = (W["conv_b"].shape[0] - inner) // (2 * STATE)
    dyn = _mm(dh1, W["wout"], mode="nt", after=after, name=f"ssd_out_dx_{t}")
    g_wout = _mm(yn, dh1, mode="tn", out_dtype=bf16, name=f"ssd_out_dw_{t}")
    dy, dz, g_normw = _gnorm_bwd(dyn, y, z, W["norm_w"], G, f"ssd_gnorm_bwd_{t}")
    dxbc, ddt, dax, ddx = _ssd_bwd(dy, y, xbc, dtx, ax, dx, states, _expand_onehot(H, HEADDIM).T, G, f"ssd_scan_bwd_{t}")
    dc, g_convw8, g_convb = _conv_bwd_dc(dxbc, xpre, W["conv_w"], W["conv_b"], f"ssd_conv_bwd_dc_{t}")
    dxpre = _conv_bwd_dx(dc, W["conv_w"], f"ssd_conv_bwd_dx_{t}")
    ddt_pre, g_dtb = _dt_bwd(ddt, dt_pre, W["dt_bias"], f"ssd_dt_bwd_{t}")
    conv_dim = W["conv_b"].shape[0]
    wT = W["w_inT"]
    g_wz = _mm(dz, hn, mode="tn", out_dtype=bf16, name=f"ssd_z_dw_{t}")
    g_wxbc = _mm(dxpre, hn, mode="tn", out_dtype=bf16, name=f"ssd_xbc_dw_{t}")
    g_wdt = _mm(ddt_pre, hn, mode="tn", out_dtype=bf16, name=f"ssd_dt_dw_{t}")
    dhn = _mm(dz, wT, brows=(0, inner), name=f"ssd_z_dx_{t}")
    dhn = _mm(dxpre, wT, brows=(inner, conv_dim), res=dhn, name=f"ssd_xbc_dx_{t}")
    dhn = _mm(ddt_pre, wT, brows=(inner + conv_dim, H), res=dhn, name=f"ssd_dt_dx_{t}")
    dh, g_nm = _rms_bwd_add(dh1, dhn, h, nm_w, f"rms_mix_bwd_{t}")
    grads = dict(
        w_inT=jnp.concatenate([g_wz, g_wxbc, g_wdt], axis=0), wout=g_wout,
        conv_w=g_convw8[:CONV_K], conv_b=g_convb[0], dt_bias=g_dtb[0],
        a_log=dax[0].reshape(H, HEADDIM)[:, 0] * a, d=jnp.sum(ddx[0].reshape(H, HEADDIM), axis=1),
        norm_w=g_normw[0], norm_mix=g_nm[0])
    return dh, grads


def _gmlp_layer_fwd(h, nm_w, W, t):
    NG, L, _ = W["w_s"].shape
    I = W["ln_w"].shape[0]
    hn = _rms_fwd(h, nm_w, f"rms_mix_{t}")
    pre = _mm(hn, W["win"], name=f"gmlp_in_{t}")
    bsx = jnp.repeat(W["b_s"].T, I // NG, axis=1)
    o = _gmlp_mid_fwd(pre, W["b_in"], W["ln_w"], W["ln_b"], W["w_s"], bsx, f"gmlp_mid_{t}")
    h1 = _mm(o, W["wout"], res=h, name=f"gmlp_out_{t}")
    return h1, (h, hn, pre, bsx, o)


def _gmlp_layer_bwd(dh1, saved, nm_w, W, t, after=None):
    h, hn, pre, bsx, o = saved
    NG = W["w_s"].shape[0]
    I = W["ln_w"].shape[0]
    do = _mm(dh1, W["wout"], mode="nt", after=after, name=f"gmlp_out_dx_{t}")
    g_wout = _mm(o, dh1, mode="tn", out_dtype=bf16, name=f"gmlp_out_dw_{t}")
    dpre, g_bin, g_lnw, g_lnb, g_ws, dbs = _gmlp_mid_bwd(do, pre, W["b_in"], W["ln_w"], W["ln_b"], W["w_s"], bsx, f"gmlp_mid_bwd_{t}")
    g_bs = _lane_group_sum(dbs, _expand_onehot(NG, I // NG).T, f"gmlp_bs_{t}").T
    g_win = _mm(hn, dpre, mode="tn", out_dtype=bf16, name=f"gmlp_in_dw_{t}")
    dhn = _mm(dpre, W["win"], mode="nt", name=f"gmlp_in_dx_{t}")
    dh, g_nm = _rms_bwd_add(dh1, dhn, h, nm_w, f"rms_mix_bwd_{t}")
    grads = dict(win=g_win, wout=g_wout, b_in=g_bin[0], ln_w=g_lnw[0], ln_b=g_lnb[0], w_s=g_ws, b_s=g_bs, norm_mix=g_nm[0])
    return dh, grads


def _ffn_fwd(h1, nf_w, W, t):
    h2, u, Gm, Um, A = _ffn_fwd_fused(h1, nf_w, W["wg"], W["wu"], W["wd"], f"ffn_fwd_{t}")
    return h2, (h1, u, Gm, Um, A)


def _ffn_bwd(dh2, saved, nf_w, W, t, after=None):
    h1, u, Gm, Um, A = saved
    dG, dU, dh1, g_nf = _ffn_bwd_fused(dh2, h1, nf_w, W["wd"], W["wg"], W["wu"], Gm, Um, f"ffn_bwd_{t}", after=after)
    g_wd = _mm(A, dh2, mode="tn", out_dtype=bf16, name=f"ffn_down_dw_{t}")
    g_wg = _mm(dG, u, mode="tn", out_dtype=bf16, name=f"ffn_gate_dw_{t}")
    g_wu = _mm(dU, u, mode="tn", out_dtype=bf16, name=f"ffn_up_dw_{t}")
    return dh1, dict(wg=g_wg, wu=g_wu, wd=g_wd, norm_ffn=g_nf[0])


def _local_step(x, p, target, norms, layer_weights, on_layer_grads=None, final_norm_grad=None):
    depth = p.shape[0]
    h = x
    saved = []
    for i in range(depth):
        Wm = layer_weights(i, "mix", h)
        if i % 2 == 0:
            h1, s_mix = _ssd_layer_fwd(h, norms["norm_mix"][i], Wm, i)
        else:
            h1, s_mix = _gmlp_layer_fwd(h, norms["norm_mix"][i], Wm, i)
        Wf = layer_weights(i, "ffn", h1)
        h2, s_ffn = _ffn_fwd(h1, norms["norm_ffn"][i], Wf, i)
        P = layer_weights(i, "ple", h2)
        h3, pe, gate = _ple_fwd(h2, p[i], P["wp"], P["pn"], P["gn"], P["wgate"], f"ple_{i}")
        saved.append((Wm, Wf, P, s_mix, s_ffn, (h2, pe, gate)))
        h = h3
    dh, loss, g_fn = _loss_head(h, target, norms["final_norm"], "loss_head")
    if final_norm_grad is not None:
        final_norm_grad[0] = g_fn[0]
    grads = [None] * depth
    tell = on_layer_grads if on_layer_grads is not None else (lambda i, part, g: None)
    after = None
    for i in reversed(range(depth)):
        Wm, Wf, P, s_mix, s_ffn, (h2, pe, gate) = saved[i]
        dh, dzg, dpe, q, g_pn, g_gn = _ple_bwd(dh, h2, pe, gate, P["pn"], P["gn"], P["wgate"], f"ple_bwd_{i}", after=after)
        g_ple = dict(wgate=_mm(q, dzg, mode="tn", out_dtype=bf16, name=f"ple_gate_dw_{i}"),
                     wp=_mm(p[i], dpe, mode="tn", out_dtype=bf16, name=f"ple_proj_dw_{i}"), pn=g_pn[0], gn=g_gn[0])
        after = tell(i, "ple", g_ple)
        dh, g_ffn = _ffn_bwd(dh, s_ffn, norms["norm_ffn"][i], Wf, i, after=after)
        after = tell(i, "ffn", g_ffn)
        if i % 2 == 0:
            dh, g_mix = _ssd_layer_bwd(dh, s_mix, norms["norm_mix"][i], Wm, i, after=after)
        else:
            dh, g_mix = _gmlp_layer_bwd(dh, s_mix, norms["norm_mix"][i], Wm, i, after=after)
        after = tell(i, "mix", g_mix)
        grads[i] = dict(mix=g_mix, ffn=g_ffn, ple=g_ple)
    return loss[0, 0], dh, g_fn[0], grads


def _flip(v, f):
    return 1 - v if f else v


_ANY = pl.BlockSpec(memory_space=pl.ANY)


_SEM = pl.BlockSpec(memory_space=pltpu.SEMAPHORE)
_DATAFLOW = pltpu.SideEffectType.DATAFLOW_SIDE_EFFECTING
_CHIP_FLIPS = ((1, 0), (0, 1), (1, 1))
_DMA = pltpu.SemaphoreType.DMA


def _structs(arrs):
    return [jax.ShapeDtypeStruct(a.shape, a.dtype) for a in arrs]


def _gather_copy(src, buf, send_sems, recv_sems, k, j, slot, x, y, c):
    c2 = src.shape[1] // 2
    fx, fy = _CHIP_FLIPS[j]
    nf = len(_CHIP_FLIPS)
    return pltpu.make_async_remote_copy(
        src_ref=src.at[:, pl.ds(c * c2, c2)], dst_ref=buf.at[slot, :, pl.ds(c * c2, c2)], send_sem=send_sems.at[nf * k + j],
        recv_sem=recv_sems.at[nf * k + j], device_id=(_flip(x, fx), _flip(y, fy), c), device_id_type=MESH)


def _gather_start(srcs, groups):
    n = len(srcs)
    ng = len(groups)
    nf = len(_CHIP_FLIPS)

    def body(*refs):
        src_refs, buf_refs, sems = refs[:n], refs[n:2 * n], refs[4 * n:]
        x, y, c = lax.axis_index("x"), lax.axis_index("y"), lax.axis_index("c")
        for gi, group in enumerate(groups):
            for k, o in enumerate(group):
                for j in range(nf):
                    _gather_copy(src_refs[o], buf_refs[o], sems[2 * gi], sems[2 * gi + 1], k, j, 2 * x + y, x, y, c).start()

    mychip = 2 * lax.axis_index("x") + lax.axis_index("y")
    inits = [lax.dynamic_update_slice(lax.empty((N_CHIPS,) + s.shape, s.dtype), s[None], (mychip, 0, 0)) for s in srcs]
    sem_shapes = [_DMA((nf * len(g),)) for g in groups for _ in range(2)]
    outs = pl.pallas_call(
        body, name="gather_start", in_specs=[_ANY] * (2 * n), out_specs=[_ANY] * (2 * n) + [_SEM] * (2 * ng),
        out_shape=_structs(srcs) + _structs(inits) + sem_shapes, input_output_aliases={i: i for i in range(2 * n)},
        compiler_params=pltpu.CompilerParams(has_side_effects=_DATAFLOW),
    )(*srcs, *inits)
    return outs[:n], outs[n:2 * n], [(outs[2 * n + 2 * gi], outs[2 * n + 2 * gi + 1]) for gi in range(ng)]


def _gather_wait(srcs, bufs, sems, after, name):
    n = len(srcs)
    nf = len(_CHIP_FLIPS)

    def body(*refs):
        src_refs, buf_refs, send_sems, recv_sems = refs[:n], refs[n:2 * n], refs[2 * n], refs[2 * n + 1]
        x, y, c = lax.axis_index("x"), lax.axis_index("y"), lax.axis_index("c")
        for k in range(n):
            for j, (fx, fy) in enumerate(_CHIP_FLIPS):
                cp = _gather_copy(src_refs[k], buf_refs[k], send_sems, recv_sems, k, j, 2 * _flip(x, fx) + _flip(y, fy), x, y, c)
                cp.wait_send()
                cp.wait_recv()

    outs = pl.pallas_call(
        body, name=name, in_specs=[_ANY] * (2 * n) + [_SEM, _SEM, _ANY], out_specs=[_ANY] * (2 * n),
        out_shape=_structs(srcs) + _structs(bufs), input_output_aliases={i: i for i in range(2 * n)},
        compiler_params=pltpu.CompilerParams(has_side_effects=_DATAFLOW),
    )(*srcs, *bufs, *sems, after)
    return outs[n:]


def _gather_forward(bufs, name):
    n = len(bufs)
    nf = len(_CHIP_FLIPS)

    def body(*refs):
        outs = refs[n:2 * n]
        send_sems, recv_sems = refs[2 * n:]
        x, y, c = lax.axis_index("x"), lax.axis_index("y"), lax.axis_index("c")

        def forward(k, j, h):
            c2 = bufs[k].shape[2] // 2
            fx, fy = _CHIP_FLIPS[j]
            part = outs[k].at[2 * _flip(x, fx) + _flip(y, fy), :, pl.ds(h * c2, c2)]
            return pltpu.make_async_remote_copy(src_ref=part, dst_ref=part, send_sem=send_sems.at[nf * k + j],
                                                recv_sem=recv_sems.at[nf * k + j], device_id=(x, y, 1 - c), device_id_type=MESH)

        sends = [forward(k, j, c) for k in range(n) for j in range(nf)]
        for cp in sends:
            cp.start()
        for k in range(n):
            for j in range(nf):
                forward(k, j, 1 - c).wait_recv()
        for cp in sends:
            cp.wait_send()

    return pl.pallas_call(
        body, name=name, in_specs=[_ANY] * n, out_specs=[_ANY] * n, out_shape=_structs(bufs),
        input_output_aliases={i: i for i in range(n)}, scratch_shapes=[_DMA((nf * n,)), _DMA((nf * n,))],
    )(*bufs)


def _half_struct(a, lead):
    return jax.ShapeDtypeStruct(lead + (a.shape[-2], a.shape[-1] // 2), a.dtype)


_DEVICE_FLIPS = tuple((f >> 2 & 1, f >> 1 & 1, f & 1) for f in range(1, N_DEV))


def _exchange_copy(srcs, lands, n, send_sems, recv_sems, i, j, slot, x, y, c):
    nf = len(_DEVICE_FLIPS)
    px, py, pc = (_flip(v, f) for v, f in zip((x, y, c), _DEVICE_FLIPS[j]))
    src = srcs[i]
    if i < n:
        c2 = src.shape[2] // 2
        src = src.at[2 * px + py, :, pl.ds(pc * c2, c2)]
    return pltpu.make_async_remote_copy(src_ref=src, dst_ref=lands[i].at[slot], send_sem=send_sems.at[nf * i + j],
                                        recv_sem=recv_sems.at[nf * i + j], device_id=(px, py, pc), device_id_type=MESH)


def _exchange_start(tensors, wholes, name):
    n, m = len(tensors), len(wholes)
    nf = len(_DEVICE_FLIPS)
    t = n + m
    land_structs = ([_half_struct(a, (N_DEV,)) for a in tensors] + [jax.ShapeDtypeStruct((N_DEV,) + w.shape, w.dtype) for w in wholes])

    def body(*refs):
        srcs, lands, send_sems, recv_sems, token = refs[:t], refs[2 * t:3 * t], refs[3 * t], refs[3 * t + 1], refs[3 * t + 2]
        x, y, c = lax.axis_index("x"), lax.axis_index("y"), lax.axis_index("c")
        for i in range(t):
            for j in range(nf):
                _exchange_copy(srcs, lands, n, send_sems, recv_sems, i, j, 4 * x + 2 * y + c, x, y, c).start()
        token[...] = jnp.zeros_like(token)

    outs = pl.pallas_call(
        body, name=name, in_specs=[_ANY] * t,
        out_specs=[_ANY] * (2 * t) + [_SEM, _SEM, pl.BlockSpec(memory_space=pltpu.VMEM)],
        out_shape=_structs(tensors) + _structs(wholes) + land_structs + [_DMA((nf * t,)), _DMA((nf * t,)),
                                                                          jax.ShapeDtypeStruct((SUBLANES, LANES), f32)],
        input_output_aliases={i: i for i in range(t)},
        compiler_params=pltpu.CompilerParams(has_side_effects=_DATAFLOW),
    )(*tensors, *wholes)
    return outs[:t], outs[t:2 * t], (outs[2 * t], outs[2 * t + 1]), outs[2 * t + 2]


def _exchange_wait(srcs, lands, n, sems, after, name):
    t = len(srcs)

    def body(*refs):
        src_refs, land_refs, send_sems, recv_sems = refs[:t], refs[t:2 * t], refs[2 * t], refs[2 * t + 1]
        x, y, c = lax.axis_index("x"), lax.axis_index("y"), lax.axis_index("c")
        for i in range(t):
            for j, (fx, fy, fc) in enumerate(_DEVICE_FLIPS):
                sender = 4 * _flip(x, fx) + 2 * _flip(y, fy) + _flip(c, fc)
                cp = _exchange_copy(src_refs, land_refs, n, send_sems, recv_sems, i, j, sender, x, y, c)
                cp.wait_send()
                cp.wait_recv()

    outs = pl.pallas_call(
        body, name=name, in_specs=[_ANY] * (2 * t) + [_SEM, _SEM, _ANY], out_specs=[_ANY] * (2 * t),
        out_shape=_structs(srcs) + _structs(lands), input_output_aliases={i: i for i in range(2 * t)},
        compiler_params=pltpu.CompilerParams(has_side_effects=_DATAFLOW),
    )(*srcs, *lands, *sems, after)
    return outs[:t], outs[t:]


def _sibling_join(bufs, name):
    flat = [(gi, l) for gi, b in enumerate(bufs) for l in range(b.shape[0])]
    n, n_buf = len(flat), len(bufs)

    def body(*refs):
        outs = refs[n_buf:2 * n_buf]
        send_sems, recv_sems = refs[2 * n_buf:]
        x, y, c = lax.axis_index("x"), lax.axis_index("y"), lax.axis_index("c")

        def push(i, h):
            gi, l = flat[i]
            c2 = bufs[gi].shape[2] // 2
            part = outs[gi].at[l, :, pl.ds(h * c2, c2)]
            return pltpu.make_async_remote_copy(src_ref=part, dst_ref=part, send_sem=send_sems.at[i], recv_sem=recv_sems.at[i],
                                                device_id=(x, y, 1 - c), device_id_type=MESH)

        sends = [push(i, c) for i in range(n)]
        for cp in sends:
            cp.start()
        for i in range(n):
            push(i, 1 - c).wait_recv()
        for cp in sends:
            cp.wait_send()

    dma = pltpu.SemaphoreType.DMA
    return pl.pallas_call(
        body, name=name, in_specs=[_ANY] * n_buf, out_specs=[_ANY] * n_buf,
        out_shape=[jax.ShapeDtypeStruct(b.shape, b.dtype) for b in bufs],
        input_output_aliases={i: i for i in range(n_buf)},
        scratch_shapes=[dma((n,)), dma((n,))],
    )(*bufs)


def _device_sum(landed, own, place, name, into=None, layer=0, layers=1):
    ndev, R, C2 = landed.shape
    tr, tc = R, C2
    while ndev * tr * tc > 1024 * 1024 and tr % (4 * SUBLANES) == 0:
        tr //= 2
    while ndev * tr * tc > 1024 * 1024 and tc % (2 * LANES) == 0:
        tc //= 2
    ncb = C2 // tc

    def body(*refs):
        place_ref, l_ref, m_ref, o_ref = refs[0], refs[1], refs[2], refs[-1]
        me = 2 * place_ref[0] + place_ref[1]
        s = jnp.where(me == 0, m_ref[...].astype(f32), l_ref[0].astype(f32))
        for d in range(1, ndev):
            s = s + jnp.where(me == d, m_ref[...].astype(f32), l_ref[d].astype(f32))
        o_ref[...] = s

    in_specs = [pl.BlockSpec((ndev, tr, tc), lambda i, j, pr: (0, i, j)),
                pl.BlockSpec((None, tr, tc), lambda i, j, pr: (pr[0], i, pr[1] * ncb + j))]
    args = [place, landed, own]
    if into is not None:
        in_specs.append(_ANY)
        args.append(into)
    return pl.pallas_call(
        body, name=name, out_shape=jax.ShapeDtypeStruct((layers, R, 2 * C2), f32),
        grid_spec=pltpu.PrefetchScalarGridSpec(
            num_scalar_prefetch=1, grid=(R // tr, ncb), in_specs=in_specs,
            out_specs=pl.BlockSpec((None, tr, tc), lambda i, j, pr: (layer, i, pr[1] * ncb + j))),
        input_output_aliases={3: 0} if into is not None else {},
        compiler_params=_cparams(("parallel", "parallel")),
    )(*args)


def _device_sum_whole(landed, own, place, name):
    ndev, R, C = landed.shape
    tr = R
    while ndev * tr * C > 1024 * 1024 and tr % (2 * SUBLANES) == 0:
        tr //= 2

    def body(place_ref, l_ref, m_ref, o_ref):
        me = 2 * place_ref[0] + place_ref[1]
        s = jnp.where(me == 0, m_ref[...], l_ref[0])
        for d in range(1, ndev):
            s = s + jnp.where(me == d, m_ref[...], l_ref[d])
        o_ref[...] = s

    return pl.pallas_call(
        body, name=name, out_shape=jax.ShapeDtypeStruct((R, C), f32),
        grid_spec=pltpu.PrefetchScalarGridSpec(
            num_scalar_prefetch=1, grid=(R // tr,),
            in_specs=[pl.BlockSpec((ndev, tr, C), lambda i, pr: (0, i, 0)), pl.BlockSpec((tr, C), lambda i, pr: (i, 0))],
            out_specs=pl.BlockSpec((tr, C), lambda i, pr: (i, 0))),
        compiler_params=_cparams(("parallel",)),
    )(place, landed, own)


PACK_COLS = 1024
PACK_ROW_MULTIPLE = 64

BIG = ("ssd_w_in", "ssd_w_out", "gmlp_w_in", "gmlp_w_out", "ffn_w_gate", "ffn_w_up", "ffn_w_down", "ple_w_proj", "ple_w_gate")
SMALL_SHARDED = ("ssd_conv_w", "gmlp_b_in", "gmlp_ln_w", "gmlp_ln_b")
REP_EARLY = "gmlp_w_s"
REP_LATE = ("norm_mix", "norm_ffn", "ssd_conv_b", "ssd_dt_bias", "ssd_a_log", "ssd_d", "ssd_norm_w", "gmlp_b_s", "ple_norm",
            "ple_gate_norm", "final_norm")
WEIGHTS = ("norm_mix", "norm_ffn", "ssd_w_in", "ssd_conv_w", "ssd_conv_b", "ssd_dt_bias", "ssd_a_log", "ssd_d", "ssd_norm_w", "ssd_w_out",
           "gmlp_w_in", "gmlp_b_in", "gmlp_ln_w", "gmlp_ln_b", "gmlp_w_s", "gmlp_b_s", "gmlp_w_out", "ffn_w_gate", "ffn_w_up",
           "ffn_w_down", "ple_w_proj", "ple_norm", "ple_gate_norm", "ple_w_gate", "final_norm")
TRANSPOSED = ("ssd_w_in", "ffn_w_gate", "ffn_w_up")


def _pack(arrs):
    flat = jnp.concatenate([a.reshape(-1).astype(f32) for a in arrs])
    per = PACK_COLS * PACK_ROW_MULTIPLE
    n = -(-flat.shape[0] // per) * per
    return jnp.pad(flat, (0, n - flat.shape[0])).reshape(-1, PACK_COLS)


def _unpack(buf, shapes):
    flat = buf.reshape(-1)
    out, o = [], 0
    for s in shapes:
        n = math.prod(s)
        out.append(flat[o:o + n].reshape(s))
        o += n
    return out


def _chip_major(g):
    r, c4 = g.shape
    return g.reshape(r, N_CHIPS, c4 // N_CHIPS).transpose(1, 0, 2)


def _from_chip_major(g):
    k, r, c = g.shape
    return g.transpose(1, 0, 2).reshape(r, k * c)


def _adamw_nd(w, m, v, g, name):
    shp = w.shape
    two = lambda a: a.reshape(-1, shp[-1])
    return [o.reshape(shp) for o in _adamw(two(w), two(m), two(v), two(g), name)]


def kernel(x, p, norm_mix, norm_ffn, ssd_w_in, ssd_conv_w, ssd_conv_b, ssd_dt_bias, ssd_a_log, ssd_d, ssd_norm_w, ssd_w_out, gmlp_w_in, gmlp_b_in, gmlp_ln_w, gmlp_ln_b, gmlp_w_s, gmlp_b_s, gmlp_w_out, ffn_w_gate, ffn_w_up, ffn_w_down, ple_w_proj, ple_norm, ple_gate_norm, ple_w_gate, final_norm, loss_target, m_norm_mix, m_norm_ffn, m_ssd_w_in, m_ssd_conv_w, m_ssd_conv_b, m_ssd_dt_bias, m_ssd_a_log, m_ssd_d, m_ssd_norm_w, m_ssd_w_out, m_gmlp_w_in, m_gmlp_b_in, m_gmlp_ln_w, m_gmlp_ln_b, m_gmlp_w_s, m_gmlp_b_s, m_gmlp_w_out, m_ffn_w_gate, m_ffn_w_up, m_ffn_w_down, m_ple_w_proj, m_ple_norm, m_ple_gate_norm, m_ple_w_gate, m_final_norm, v_norm_mix, v_norm_ffn, v_ssd_w_in, v_ssd_conv_w, v_ssd_conv_b, v_ssd_dt_bias, v_ssd_a_log, v_ssd_d, v_ssd_norm_w, v_ssd_w_out, v_gmlp_w_in, v_gmlp_b_in, v_gmlp_ln_w, v_gmlp_ln_b, v_gmlp_w_s, v_gmlp_b_s, v_gmlp_w_out, v_ffn_w_gate, v_ffn_w_up, v_ffn_w_down, v_ple_w_proj, v_ple_norm, v_ple_gate_norm, v_ple_w_gate, v_final_norm):
    given = dict(locals())
    view = lambda n, a: jnp.swapaxes(a, 1, 2) if n in TRANSPOSED else a
    w = {n: view(n, given[n]) for n in WEIGHTS}
    mom = {n: view(n, given["m_" + n]) for n in WEIGHTS}
    var = {n: view(n, given["v_" + n]) for n in WEIGHTS}
    depth = p.shape[0]
    n_ssd, n_gmlp = ssd_w_in.shape[0], gmlp_w_in.shape[0]
    inner = ssd_dt_bias.shape[1] * HEADDIM
    conv_dim = ssd_conv_b.shape[1]

    place = jnp.stack([2 * lax.axis_index("x") + lax.axis_index("y"), lax.axis_index("c")]).astype(jnp.int32)

    def part_keys(i, part):
        j = i // 2
        if part == "mix":
            names = (("ssd_w_in", j), ("ssd_w_out", j)) if i % 2 == 0 else (("gmlp_w_in", j), ("gmlp_w_out", j))
            return ((("small", 0),) if i == 0 else ()) + names
        if part == "ffn":
            return (("ffn_w_gate", i), ("ffn_w_up", i), ("ffn_w_down", i))
        return (("ple_w_proj", i), ("ple_w_gate", i))

    parts = [(i, part) for i in range(depth) for part in ("mix", "ffn", "ple")]
    keys, groups = [], {}
    for ip in parts:
        names = part_keys(*ip)
        groups[ip] = list(range(len(keys), len(keys) + len(names)))
        keys += names
    small_shapes = [w[n].shape for n in SMALL_SHARDED]
    srcs = [_pack([w[n] for n in SMALL_SHARDED]) if n == "small" else w[n][l].astype(bf16) for n, l in keys]
    srcs, landing, gather_sems = _gather_start(srcs, [groups[ip] for ip in parts])
    gather_sems = dict(zip(parts, gather_sems))
    small_full = {}

    gw = {}

    def fetch(i, which, h):
        got = []
        for part in which:
            idx = groups[(i, part)]
            got += _gather_wait([srcs[o] for o in idx], [landing[o] for o in idx], gather_sems[(i, part)], h, f"gather_wait_{part}_{i}")
        names = [keys[o] for part in which for o in groups[(i, part)]]
        gw.update(dict(zip(names, _gather_forward(got, f"gather_forward_{which[0]}_{i}"))))

    def layer_weights(i, part, h):
        if i == 0:
            fetch(i, (part,), h)
        elif part == "mix":
            fetch(i, ("mix", "ffn", "ple"), h)
        rows = lambda a: a.reshape(-1, a.shape[-1])
        j = i // 2
        if part == "ffn":
            return dict(wg=gw[("ffn_w_gate", i)], wu=gw[("ffn_w_up", i)], wd=gw[("ffn_w_down", i)])
        if part == "ple":
            return dict(wp=_from_chip_major(gw[("ple_w_proj", i)]), pn=ple_norm[i], gn=ple_gate_norm[i], wgate=rows(gw[("ple_w_gate", i)]))
        if i == 0:
            by_chip = [_unpack(gw[("small", 0)][k], small_shapes) for k in range(N_CHIPS)]
            small_full.update({n: jnp.concatenate([by_chip[k][t] for k in range(N_CHIPS)], axis=-1) for t, n in enumerate(SMALL_SHARDED)})
        if i % 2 == 0:
            return dict(w_inT=rows(gw[("ssd_w_in", j)]),
                        conv_w=small_full["ssd_conv_w"][j], conv_b=ssd_conv_b[j], dt_bias=ssd_dt_bias[j], a_log=ssd_a_log[j],
                        d=ssd_d[j], norm_w=ssd_norm_w[j], wout=rows(gw[("ssd_w_out", j)]))
        return dict(win=_from_chip_major(gw[("gmlp_w_in", j)]), b_in=small_full["gmlp_b_in"][j], ln_w=small_full["gmlp_ln_w"][j],
                    ln_b=small_full["gmlp_ln_b"][j], w_s=gmlp_w_s[j], b_s=gmlp_b_s[j], wout=rows(gw[("gmlp_w_out", j)]))

    rows4 = lambda a: a.reshape((N_CHIPS, a.shape[0] // N_CHIPS) + a.shape[1:])
    cut = lambda a, k: a[..., k * (a.shape[-1] // N_CHIPS):(k + 1) * (a.shape[-1] // N_CHIPS)]
    layer_grads = {}
    in_flight = {}
    tokens = {}
    owns = {}

    def on_layer_grads(i, part, g):
        layer_grads[(i, part)] = g
        j = i // 2
        wholes = {}
        if part == "ffn":
            chunks = {("ffn_w_gate", i): g["wg"], ("ffn_w_up", i): g["wu"], ("ffn_w_down", i): g["wd"]}
        elif part == "ple":
            chunks = {("ple_w_proj", i): _chip_major(g["wp"]), ("ple_w_gate", i): rows4(g["wgate"])}
        elif i % 2 == 0:
            chunks = {("ssd_w_in", j): rows4(g["w_inT"]), ("ssd_w_out", j): rows4(g["wout"])}
        else:
            chunks = {("gmlp_w_in", j): _chip_major(g["win"]), ("gmlp_w_out", j): rows4(g["wout"])}
        stack = lambda prt, key, layers: jnp.stack([layer_grads[(l, prt)][key] for l in layers])
        ssd, gml, every = range(0, depth, 2), range(1, depth, 2), range(depth)
        if part == "mix" and i == 1:
            wholes["rep_early"] = stack("mix", "w_s", gml).reshape(-1, w[REP_EARLY].shape[-1])
        if part == "mix" and i == 0:
            small_g = dict(ssd_conv_w=stack("mix", "conv_w", ssd), gmlp_b_in=stack("mix", "b_in", gml),
                           gmlp_ln_w=stack("mix", "ln_w", gml), gmlp_ln_b=stack("mix", "ln_b", gml))
            chunks[("small", 0)] = jnp.stack([_pack([cut(small_g[n], k) for n in SMALL_SHARDED]) for k in range(N_CHIPS)])
            rep_g = dict(
                norm_mix=stack("mix", "norm_mix", every), norm_ffn=stack("ffn", "norm_ffn", every),
                ssd_conv_b=stack("mix", "conv_b", ssd), ssd_dt_bias=stack("mix", "dt_bias", ssd), ssd_a_log=stack("mix", "a_log", ssd),
                ssd_d=stack("mix", "d", ssd), ssd_norm_w=stack("mix", "norm_w", ssd), gmlp_b_s=stack("mix", "b_s", gml),
                ple_norm=stack("ple", "pn", every), ple_gate_norm=stack("ple", "gn", every), final_norm=final_norm_grad[0])
            wholes["rep_late"] = _pack([rep_g[n] for n in REP_LATE])
        ks, wk = list(chunks), list(wholes)
        thru, lands, sems, token = _exchange_start([chunks[k] for k in ks], [wholes[k] for k in wk], f"grads_exchange_start_{part}_{i}")
        in_flight[(i, part)] = (ks, wk, thru, lands, sems)
        tokens[(i, part)] = token
        return token

    final_norm_grad = [None]
    norms = dict(norm_mix=norm_mix, norm_ffn=norm_ffn, final_norm=final_norm)
    loss_part, grad_x, g_fn, _ = _local_step(x[0], p[:, 0], loss_target[0], norms, layer_weights, on_layer_grads, final_norm_grad)
    loss = lax.psum(loss_part, ("x", "y", "c"))

    landed, res = {}, {}

    def wait_for(which, after):
        for i, part in which:
            ks, wk, thru, lands, sems = in_flight[(i, part)]
            thru, lands = _exchange_wait(thru, lands, len(ks), sems, after, f"grads_exchange_wait_{part}_{i}")
            landed.update(dict(zip(ks + wk, lands)))
            owns.update(dict(zip(ks + wk, thru)))

    def packed_update(names, gsum, tag):
        packs = [gsum] + list(_adamw(_pack([w[n] for n in names]), _pack([mom[n] for n in names]), _pack([var[n] for n in names]), gsum, tag))
        per_kind = [_unpack(pk, [w[n].shape for n in names]) for pk in packs]
        for t, n in enumerate(names):
            res[n] = [per_kind[k][t] for k in range(4)]

    def finish(big_names, with_small, tag):
        bufs = []
        for n in big_names + (("small",) if with_small else ()):
            layers = w[n].shape[0] if n != "small" else 1
            buf = None
            for l in range(layers):
                buf = _device_sum(landed[(n, l)], owns[(n, l)], place, f"grads_sum_{n}_{l}", into=buf, layer=l, layers=layers)
            bufs.append(buf)
        reduced = _sibling_join(bufs, f"grads_sibling_join_{tag}")
        for n, gsum in zip(big_names, reduced):
            res[n] = [view(n, a) for a in [gsum] + _adamw_nd(w[n], mom[n], var[n], gsum, "adamw_" + n)]
        if with_small:
            packed_update(SMALL_SHARDED, reduced[-1][0], "adamw_small_sharded")
            packed_update(REP_LATE, _device_sum_whole(landed["rep_late"], owns["rep_late"], place, "grads_sum_rep_late"), "adamw_rep_late")
        else:
            n, shp = REP_EARLY, w[REP_EARLY].shape
            two = lambda a: a.reshape(-1, shp[-1])
            gsum = _device_sum_whole(landed["rep_early"], owns["rep_early"], place, "grads_sum_rep_early")
            res[n] = [a.reshape(shp) for a in [gsum] + list(_adamw(two(w[n]), two(mom[n]), two(var[n]), gsum, "adamw_" + n))]

    last = (0, "mix")
    late_big = tuple(n for n in BIG if n.startswith("ssd_"))
    early_big = tuple(n for n in BIG if n not in late_big)
    wait_for([ip for ip in reversed(parts) if ip != last], tokens[last])
    finish(early_big, False, "early")
    wait_for([last], res[early_big[-1]][1])
    finish(late_big, True, "late")
    return (loss, grad_x[None], *[res[n][0] for n in WEIGHTS], *[res[n][1] for n in WEIGHTS],
            *[res[n][2] for n in WEIGHTS], *[res[n][3] for n in WEIGHTS])
```

```python
import math

import jax
import jax.numpy as jnp
from jax import lax
from jax.experimental import pallas as pl
from jax.experimental.pallas import tpu as pltpu

f32 = jnp.float32
bf16 = jnp.bfloat16
HI = lax.Precision.HIGHEST

LANES = 128
SUBLANES = 8
VMEM_LIMIT_BYTES = 56 * 1024 * 1024

HEADDIM = 64
STATE = 128
CHUNK = 128
CONV_K = 4
RMS_EPS = 1e-6
LN_EPS = 1e-5
ADAM_LR = 0.001
ADAM_B1 = 0.9
ADAM_B2 = 0.999
ADAM_EPS = 1e-08
ADAM_WD = 0.01
ADAM_STEP = 10

FFN_SUBTILES = 2

N_CHIPS = 4
N_DEV = 8
MESH = pl.DeviceIdType.MESH


def _cparams(sem):
    return pltpu.CompilerParams(dimension_semantics=sem, vmem_limit_bytes=VMEM_LIMIT_BYTES)


def _tile(n, want):
    if n <= want:
        return n
    t = want
    while n % t:
        t //= 2
    return t


def _row_spec(tm, c):
    return pl.BlockSpec((tm, c), lambda i: (i, 0))


def _full_spec(shape):
    nd = len(shape)
    return pl.BlockSpec(tuple(shape), lambda *_: (0,) * nd)


def _sigmoid(x):
    return 1.0 / (1.0 + jnp.exp(-x))


def _silu(x):
    return x * _sigmoid(x)


def _dsilu(x):
    s = _sigmoid(x)
    return s * (1.0 + x * (1.0 - s))


def _gelu(x):
    return 0.5 * x * (1.0 + lax.erf(x * (1.0 / math.sqrt(2.0))))


def _dgelu(x):
    return 0.5 * (1.0 + lax.erf(x * (1.0 / math.sqrt(2.0)))) + x * jnp.exp(-0.5 * x * x) * (1.0 / math.sqrt(2.0 * math.pi))


def _softplus(x):
    return jnp.maximum(x, 0.0) + jnp.log(1.0 + jnp.exp(-jnp.abs(x)))


def _rms(x, w, eps):
    r = lax.rsqrt(jnp.mean(x * x, axis=-1, keepdims=True) + eps)
    return x * r * w


def _rms_bwd(dy, x, w, eps):
    r = lax.rsqrt(jnp.mean(x * x, axis=-1, keepdims=True) + eps)
    xh = x * r
    g = dy * w
    dx = r * (g - xh * jnp.mean(g * xh, axis=-1, keepdims=True))
    dw = jnp.sum(dy * xh, axis=0, keepdims=True)
    return dx, dw


def _dot(a, b, dims=(((1,), (0,)), ((), ())), precision=None):
    return lax.dot_general(a, b, dims, precision=precision, preferred_element_type=f32)


NN = (((1,), (0,)), ((), ()))
NT = (((1,), (1,)), ((), ()))
TN = (((0,), (0,)), ((), ()))


def _split3(x):
    hi = x.astype(bf16)
    r1 = x - hi.astype(f32)
    mid = r1.astype(bf16)
    return hi, mid, (r1 - mid.astype(f32)).astype(bf16)


def _dot01_left(m01, x):
    mb = m01.astype(bf16)
    hi, mid, lo = _split3(x)
    return _dot(mb, hi, NN) + _dot(mb, mid, NN) + _dot(mb, lo, NN)


def _dot01_right(x, m01):
    mb = m01.astype(bf16)
    hi, mid, lo = _split3(x)
    return _dot(hi, mb, NN) + _dot(mid, mb, NN) + _dot(lo, mb, NN)


def _mm(a, b, *, mode="nn", out_dtype=f32, res=None, kbatch=False, brows=None, after=None, tm=1024, tn=1024, tk=1024, name):
    a3, b3 = a.ndim == 3, b.ndim == 3
    nb = a.shape[0] if a3 else (b.shape[0] if b3 else 1)
    ash, bsh = a.shape[-2:], b.shape[-2:]
    if brows is not None:
        bsh = (brows[1], bsh[1])
    if mode == "nn":
        M, K, N = ash[0], ash[1], bsh[1]
    elif mode == "nt":
        M, K, N = ash[0], ash[1], bsh[0]
    else:
        K, M, N = ash[0], ash[1], bsh[1]
    tm, tn, tk = _tile(M, tm), (N if N % LANES else _tile(N, tn)), (K if K % LANES else _tile(K, tk))
    b0 = 0
    if brows is not None:
        assert mode in ("nn", "nt") and bsh[0] == (K if mode == "nn" else N)
        blk = tk if mode == "nn" else tn
        while brows[0] % blk:
            blk //= 2
        assert blk % LANES == 0 or blk == brows[1]
        b0 = brows[0] // blk
        tn, tk = (tn, blk) if mode == "nn" else (blk, tk)
    nk = K // tk
    if kbatch:
        assert a3 and b3
        grid = (1, M // tm, N // tn, nb * nk)
        bi = lambda g, k: k // nk
        ki = lambda g, k: k % nk
    else:
        grid = (nb, M // tm, N // tn, nk)
        bi = lambda g, k: g
        ki = lambda g, k: k
    nsteps = grid[3]

    def spec(is3, blk, imap):
        if is3:
            return pl.BlockSpec((None,) + blk, lambda g, i, j, k: (bi(g, k),) + imap(i, j, ki(g, k)))
        return pl.BlockSpec(blk, lambda g, i, j, k: imap(i, j, ki(g, k)))

    if mode == "nn":
        a_spec = spec(a3, (tm, tk), lambda i, j, k: (i, k))
        b_spec = spec(b3, (tk, tn), lambda i, j, k: (k + b0, j))
        dims = NN
    elif mode == "nt":
        a_spec = spec(a3, (tm, tk), lambda i, j, k: (i, k))
        b_spec = spec(b3, (tn, tk), lambda i, j, k: (j + b0, k))
        dims = NT
    else:
        a_spec = spec(a3, (tk, tm), lambda i, j, k: (k, i))
        b_spec = spec(b3, (tk, tn), lambda i, j, k: (k, j))
        dims = TN
    out3 = (a3 or b3) and not kbatch
    if out3:
        o_spec = pl.BlockSpec((None, tm, tn), lambda g, i, j, k: (g, i, j))
        o_shape = jax.ShapeDtypeStruct((nb, M, N), out_dtype)
    else:
        o_spec = pl.BlockSpec((tm, tn), lambda g, i, j, k: (i, j))
        o_shape = jax.ShapeDtypeStruct((M, N), out_dtype)
    in_specs = [a_spec, b_spec]
    args = [a, b]
    if res is not None:
        in_specs.append(pl.BlockSpec((tm, tn), lambda g, i, j, k: (i, j)))
        args.append(res)
    if after is not None:
        in_specs.append(pl.BlockSpec(memory_space=pl.ANY))
        args.append(after)

    def body(*refs):
        a_ref, b_ref = refs[:2]
        r_ref = refs[2] if res is not None else None
        o_ref, acc_ref = refs[-2:]
        k = pl.program_id(3)

        @pl.when(k == 0)
        def _():
            acc_ref[...] = jnp.zeros_like(acc_ref)

        acc_ref[...] += _dot(a_ref[...].astype(bf16), b_ref[...].astype(bf16), dims)

        @pl.when(k == nsteps - 1)
        def _():
            r = acc_ref[...]
            if res is not None:
                r = r + r_ref[...]
            o_ref[...] = r.astype(o_ref.dtype)

    return pl.pallas_call(
        body, name=name, grid=grid, in_specs=in_specs, out_specs=o_spec, out_shape=o_shape,
        scratch_shapes=[pltpu.VMEM((tm, tn), f32)],
        compiler_params=_cparams(("parallel", "parallel", "parallel", "arbitrary")),
    )(*args)


def _rowcall(fn, *, name, rows, fulls, out_rows, out_accs=(), tm=512):
    S = rows[0].shape[0]
    tm = _tile(S, tm)
    n_r, n_f, n_or, n_oa = len(rows), len(fulls), len(out_rows), len(out_accs)

    def body(*refs):
        ins = [r[...] for r in refs[:n_r + n_f]]
        outs = fn(*ins)
        if not isinstance(outs, (tuple, list)):
            outs = (outs,)
        o_refs = refs[n_r + n_f:]
        for o_ref, v in zip(o_refs[:n_or], outs[:n_or]):
            o_ref[...] = v.astype(o_ref.dtype)
        if n_oa:
            first = pl.program_id(0) == 0

            @pl.when(first)
            def _():
                for o_ref, v in zip(o_refs[n_or:], outs[n_or:]):
                    o_ref[...] = v

            @pl.when(jnp.logical_not(first))
            def _():
                for o_ref, v in zip(o_refs[n_or:], outs[n_or:]):
                    o_ref[...] += v

    in_specs = [_row_spec(tm, r.shape[1]) for r in rows] + [_full_spec(f.shape) for f in fulls]
    out_specs = [_row_spec(tm, c) for c, _ in out_rows] + [_full_spec(s) for s in out_accs]
    out_shape = [jax.ShapeDtypeStruct((S, c), d) for c, d in out_rows] + [jax.ShapeDtypeStruct(s, f32) for s in out_accs]
    res = pl.pallas_call(
        body, name=name, grid=(S // tm,), in_specs=in_specs, out_specs=out_specs, out_shape=out_shape,
        compiler_params=_cparams(("arbitrary",) if n_oa else ("parallel",)),
    )(*rows, *fulls)
    return res


def _row2(v):
    return v.reshape(1, -1)


def _rms_fwd(h, w, name):
    D = h.shape[1]
    return _rowcall(lambda x, w_: _rms(x, w_, RMS_EPS), name=name, rows=[h], fulls=[_row2(w)], out_rows=[(D, bf16)])[0]


def _conv_taps(x, halo, w_ref, b_ref):
    row = lax.broadcasted_iota(jnp.int32, x.shape, 0)
    row8 = lax.broadcasted_iota(jnp.int32, halo.shape, 0)
    x0 = x[0:SUBLANES, :]
    acc = x * w_ref[CONV_K - 1:CONV_K, :] + b_ref[...]
    acc0 = x0 * w_ref[CONV_K - 1:CONV_K, :] + b_ref[...]
    shifted = []
    for k in range(1, CONV_K):
        wk = w_ref[CONV_K - 1 - k:CONV_K - k, :]
        xk = pltpu.roll(x, k, axis=0)
        xk0 = jnp.where(row8 < k, pltpu.roll(halo, k, axis=0), pltpu.roll(x0, k, axis=0))
        acc = acc + xk * wk
        acc0 = acc0 + xk0 * wk
        shifted.append((jnp.where(row < SUBLANES, 0.0, xk), xk0))
    return acc, acc0, shifted


def _conv_fwd(xpre, w, b, name):
    S, C = xpre.shape
    tm, tc = _tile(S, 512), _tile(C, 1024)
    hb = tm // SUBLANES

    def body(x_ref, halo_ref, w_ref, b_ref, o_ref):
        halo = jnp.where(pl.program_id(1) > 0, halo_ref[...], 0.0)
        acc, acc0, _ = _conv_taps(x_ref[...], halo, w_ref, b_ref)
        o_ref[...] = _silu(acc)
        o_ref[0:SUBLANES, :] = _silu(acc0)

    return pl.pallas_call(
        body, name=name, grid=(C // tc, S // tm),
        in_specs=[pl.BlockSpec((tm, tc), lambda j, i: (i, j)),
                  pl.BlockSpec((SUBLANES, tc), lambda j, i: (jnp.maximum(i * hb - 1, 0), j)),
                  pl.BlockSpec((CONV_K, tc), lambda j, i: (0, j)),
                  pl.BlockSpec((1, tc), lambda j, i: (0, j))],
        out_specs=pl.BlockSpec((tm, tc), lambda j, i: (i, j)),
        out_shape=jax.ShapeDtypeStruct((S, C), f32),
        compiler_params=_cparams(("parallel", "parallel")),
    )(xpre, xpre, w, _row2(b))


def _conv_bwd_dc(dxbc, xpre, w, b, name):
    S, C = xpre.shape
    tm, tc = _tile(S, 512), _tile(C, 1024)
    hb = tm // SUBLANES

    def body(d_ref, x_ref, halo_ref, w_ref, b_ref, dc_ref, dw_ref, db_ref):
        i = pl.program_id(1)
        x = x_ref[...]
        halo = jnp.where(i > 0, halo_ref[...], 0.0)
        acc, acc0, shifted = _conv_taps(x, halo, w_ref, b_ref)
        row = lax.broadcasted_iota(jnp.int32, x.shape, 0)
        d = d_ref[...]
        dc0 = d[0:SUBLANES, :] * _dsilu(acc0)
        dc = jnp.where(row < SUBLANES, 0.0, d * _dsilu(acc))
        dc_ref[...] = dc
        dc_ref[0:SUBLANES, :] = dc0
        parts = [jnp.sum(dc * x, axis=0, keepdims=True) + jnp.sum(dc0 * x[0:SUBLANES, :], axis=0, keepdims=True)]
        for xs_big, xs0 in shifted:
            parts.append(jnp.sum(dc * xs_big, axis=0, keepdims=True) + jnp.sum(dc0 * xs0, axis=0, keepdims=True))
        dw = jnp.concatenate([parts[CONV_K - 1 - k] for k in range(CONV_K)] + [jnp.zeros((SUBLANES - CONV_K, x.shape[1]), f32)], axis=0)
        db = jnp.sum(dc, axis=0, keepdims=True) + jnp.sum(dc0, axis=0, keepdims=True)

        @pl.when(i == 0)
        def _():
            dw_ref[...] = dw
            db_ref[...] = db

        @pl.when(i > 0)
        def _():
            dw_ref[...] += dw
            db_ref[...] += db

    return pl.pallas_call(
        body, name=name, grid=(C // tc, S // tm),
        in_specs=[pl.BlockSpec((tm, tc), lambda j, i: (i, j))] * 2 +
                 [pl.BlockSpec((SUBLANES, tc), lambda j, i: (jnp.maximum(i * hb - 1, 0), j)),
                  pl.BlockSpec((CONV_K, tc), lambda j, i: (0, j)),
                  pl.BlockSpec((1, tc), lambda j, i: (0, j))],
        out_specs=[pl.BlockSpec((tm, tc), lambda j, i: (i, j)),
                   pl.BlockSpec((SUBLANES, tc), lambda j, i: (0, j)),
                   pl.BlockSpec((1, tc), lambda j, i: (0, j))],
        out_shape=[jax.ShapeDtypeStruct((S, C), f32), jax.ShapeDtypeStruct((SUBLANES, C), f32), jax.ShapeDtypeStruct((1, C), f32)],
        compiler_params=_cparams(("parallel", "arbitrary")),
    )(dxbc, xpre, xpre, w, _row2(b))


def _conv_bwd_dx(dc, w, name):
    S, C = dc.shape
    tm, tc = _tile(S, 512), _tile(C, 1024)
    hb = tm // SUBLANES
    nrow = S // tm
    last8 = S // SUBLANES - 1

    def body(d_ref, nxt_ref, w_ref, o_ref):
        i = pl.program_id(1)
        d = d_ref[...]
        nxt = jnp.where(i < nrow - 1, nxt_ref[...], 0.0)
        row8 = lax.broadcasted_iota(jnp.int32, nxt.shape, 0)
        dl = d[tm - SUBLANES:tm, :]
        acc = d * w_ref[CONV_K - 1:CONV_K, :]
        accl = dl * w_ref[CONV_K - 1:CONV_K, :]
        for j in range(1, CONV_K):
            wk = w_ref[CONV_K - 1 - j:CONV_K - j, :]
            acc = acc + pltpu.roll(d, tm - j, axis=0) * wk
            accl = accl + jnp.where(row8 >= SUBLANES - j, pltpu.roll(nxt, SUBLANES - j, axis=0), pltpu.roll(dl, SUBLANES - j, axis=0)) * wk
        o_ref[...] = acc.astype(o_ref.dtype)
        o_ref[tm - SUBLANES:tm, :] = accl.astype(o_ref.dtype)

    return pl.pallas_call(
        body, name=name, grid=(C // tc, nrow),
        in_specs=[pl.BlockSpec((tm, tc), lambda j, i: (i, j)),
                  pl.BlockSpec((SUBLANES, tc), lambda j, i: (jnp.minimum((i + 1) * hb, last8), j)),
                  pl.BlockSpec((CONV_K, tc), lambda j, i: (0, j))],
        out_specs=pl.BlockSpec((tm, tc), lambda j, i: (i, j)),
        out_shape=jax.ShapeDtypeStruct((S, C), f32),
        compiler_params=_cparams(("parallel", "parallel")),
    )(dc, dc, w)


def _halfsum(v, lane_lo):
    s0 = jnp.sum(jnp.where(lane_lo, v, 0.0), axis=1, keepdims=True)
    s1 = jnp.sum(jnp.where(lane_lo, 0.0, v), axis=1, keepdims=True)
    return jnp.where(lane_lo, s0, s1)


def _ssd_specs(S, inner, GN, nchunks, rev):
    L = CHUNK
    cm = (lambda c: nchunks - 1 - c) if rev else (lambda c: c)
    xs = pl.BlockSpec((L, inner), lambda c: (cm(c), 0))
    bb = pl.BlockSpec((L, GN), lambda c: (cm(c), inner // GN))
    cc = pl.BlockSpec((L, GN), lambda c: (cm(c), inner // GN + 1))
    row = pl.BlockSpec((L, inner), lambda c: (cm(c), 0))
    vec = pl.BlockSpec((1, inner), lambda c: (0, 0))
    st = pl.BlockSpec((None, inner, STATE), lambda c: (cm(c), 0, 0))
    return xs, bb, cc, row, vec, st


def _ssd_fwd(xbc, dtx, ax, dx, G, name):
    S, inner = dtx.shape
    GN = G * STATE
    L = CHUNK
    nchunks = S // L
    npairs = inner // LANES
    ppg = npairs // G
    assert inner % GN == 0 and L == LANES and STATE == LANES

    def body(xs_ref, b_ref, c_ref, dtx_ref, ax_ref, dx_ref, y_ref, so_ref, st_ref):
        ci = pl.program_id(0)

        @pl.when(ci == 0)
        def _():
            st_ref[...] = jnp.zeros_like(st_ref)

        r = lax.broadcasted_iota(jnp.int32, (L, L), 0)
        cidx = lax.broadcasted_iota(jnp.int32, (L, L), 1)
        tril = cidx <= r
        lane_lo = cidx < HEADDIM
        xs = xs_ref[...]
        dtv = dtx_ref[...]
        X = xs * dtv
        da = dtv * ax_ref[...]
        cs = _dot01_left(tril, da)
        cs_last = jnp.sum(da, axis=0, keepdims=True)
        so_ref[...] = st_ref[...]
        for g in range(G):
            Bg = b_ref[:, g * STATE:(g + 1) * STATE].astype(bf16)
            Cg = c_ref[:, g * STATE:(g + 1) * STATE].astype(bf16)
            CB = _dot(Cg, Bg, NT)
            for j in range(ppg):
                lo = (g * ppg + j) * LANES
                tile = cs[:, lo:lo + LANES]
                rl = pltpu.roll(tile, HEADDIM, axis=1)
                Xp = X[:, lo:lo + LANES]
                prev = st_ref[lo:lo + LANES, :]
                ypair = _dot(Cg, prev.astype(bf16), NT) * jnp.exp(tile)
                for half in (0, 1):
                    hm = lane_lo if half == 0 else jnp.logical_not(lane_lo)
                    colb = jnp.where(hm, tile, rl)
                    Lm = jnp.exp(jnp.where(tril, colb - colb.T, -1e30))
                    W = (CB * Lm).astype(bf16)
                    ypair = ypair + _dot(W, jnp.where(hm, Xp, 0.0).astype(bf16), NN)
                y_ref[:, lo:lo + LANES] = ypair + xs[:, lo:lo + LANES] * dx_ref[:, lo:lo + LANES]
                last = cs_last[:, lo:lo + LANES]
                snew = _dot((Xp * jnp.exp(last - tile)).astype(bf16), Bg, TN)
                dec_rows = jnp.broadcast_to(jnp.exp(last), (L, LANES)).T
                st_ref[lo:lo + LANES, :] = dec_rows * prev + snew

    xs_s, b_s, c_s, row_s, vec_s, st_s = _ssd_specs(S, inner, GN, nchunks, False)
    return pl.pallas_call(
        body, name=name, grid=(nchunks,),
        in_specs=[xs_s, b_s, c_s, row_s, vec_s, vec_s],
        out_specs=[row_s, st_s],
        out_shape=[jax.ShapeDtypeStruct((S, inner), f32), jax.ShapeDtypeStruct((nchunks, inner, STATE), f32)],
        scratch_shapes=[pltpu.VMEM((inner, STATE), f32)],
        compiler_params=_cparams(("arbitrary",)),
    )(xbc, xbc, xbc, dtx, ax, dx)


def _ssd_bwd(dy, y, xbc, dtx, ax, dx, states, et, G, name):
    S, inner = dtx.shape
    H = et.shape[1]
    GN = G * STATE
    Cc = inner + 2 * GN
    L = CHUNK
    nchunks = S // L
    npairs = inner // LANES
    ppg = npairs // G

    def body(dy_ref, y_ref, xs_ref, b_ref, c_ref, dtx_ref, ax_ref, dx_ref, si_ref, et_ref,
             dxbc_ref, ddt_ref, dax_ref, ddx_ref, dst_ref, dA_ref, dAl_ref, ddtp_ref, cs_ref, csl_ref):
        ci = pl.program_id(0)

        @pl.when(ci == 0)
        def _():
            dst_ref[...] = jnp.zeros_like(dst_ref)
            dax_ref[...] = jnp.zeros_like(dax_ref)
            ddx_ref[...] = jnp.zeros_like(ddx_ref)

        r = lax.broadcasted_iota(jnp.int32, (L, L), 0)
        cidx = lax.broadcasted_iota(jnp.int32, (L, L), 1)
        tril = cidx <= r
        lane_lo = cidx < HEADDIM
        lane_lo1 = lax.broadcasted_iota(jnp.int32, (1, LANES), 1) < HEADDIM
        da = dtx_ref[...] * ax_ref[...]
        cs_ref[...] = _dot01_left(tril, da)
        csl_ref[...] = jnp.sum(da, axis=0, keepdims=True)
        for g in range(G):
            Bg = b_ref[:, g * STATE:(g + 1) * STATE].astype(bf16)
            Cg = c_ref[:, g * STATE:(g + 1) * STATE].astype(bf16)
            CB = _dot(Cg, Bg, NT)
            dCB = jnp.zeros((L, L), f32)
            dBg = jnp.zeros((L, STATE), f32)
            dCg = jnp.zeros((L, STATE), f32)
            for j in range(ppg):
                lo = (g * ppg + j) * LANES
                tile = cs_ref[:, lo:lo + LANES]
                rl = pltpu.roll(tile, HEADDIM, axis=1)
                eA = jnp.exp(tile)
                xsp = xs_ref[:, lo:lo + LANES]
                dtp = dtx_ref[:, lo:lo + LANES]
                Xp = xsp * dtp
                dYp = dy_ref[:, lo:lo + LANES]
                prev = si_ref[lo:lo + LANES, :]
                dSn = dst_ref[lo:lo + LANES, :]
                prev_b = prev.astype(bf16)
                dSn_b = dSn.astype(bf16)
                dYe = (dYp * eA).astype(bf16)
                dCg = dCg + _dot(dYe, prev_b, NN)
                dprev = _dot(dYe, Cg, TN)
                last = csl_ref[:, lo:lo + LANES]
                w = jnp.exp(last - tile)
                BdS = _dot(Bg, dSn_b, NT)
                Xw = Xp * w
                XwB = Xw * BdS
                dAl_t = _halfsum(jnp.sum(XwB, axis=0, keepdims=True), lane_lo1)
                dBg = dBg + _dot(Xw.astype(bf16), dSn_b, NN)
                dec_rows = jnp.broadcast_to(jnp.exp(last), (L, LANES)).T
                dprev = dprev + dec_rows * dSn
                rsum = jnp.sum(dSn * prev * dec_rows, axis=1, keepdims=True)
                s0 = jnp.sum(rsum[0:HEADDIM], axis=0, keepdims=True)
                s1 = jnp.sum(rsum[HEADDIM:LANES], axis=0, keepdims=True)
                dAl_t = dAl_t + jnp.where(lane_lo1, s0, s1)
                dXd = jnp.zeros((L, LANES), f32)
                for half in (0, 1):
                    hm = lane_lo if half == 0 else jnp.logical_not(lane_lo)
                    colb = jnp.where(hm, tile, rl)
                    Lm = jnp.exp(jnp.where(tril, colb - colb.T, -1e30))
                    dYh = jnp.where(hm, dYp, 0.0).astype(bf16)
                    dW = _dot(dYh, jnp.where(hm, Xp, 0.0).astype(bf16), NT)
                    dXd = dXd + _dot((CB * Lm).astype(bf16), dYh, TN)
                    dCB = dCB + dW * Lm
                yoff = _dot(Cg, prev_b, NT) * eA
                ydiag = y_ref[:, lo:lo + LANES] - xsp * dx_ref[:, lo:lo + LANES] - yoff
                dYb = dYp.astype(bf16).astype(f32)
                Xb = Xp.astype(bf16).astype(f32)
                dA_t = _halfsum(dYb * ydiag - Xb * dXd + dYp * yoff - XwB, lane_lo)
                dXp = w * BdS + dXd
                dxbc_ref[:, lo:lo + LANES] = dXp * dtp + dYp * dx_ref[:, lo:lo + LANES]
                ddtp_ref[:, lo:lo + LANES] = dXp * xsp
                ddx_ref[:, lo:lo + LANES] += jnp.sum(dYp * xsp, axis=0, keepdims=True)
                dA_ref[:, lo:lo + LANES] = dA_t
                dAl_ref[:, lo:lo + LANES] = dAl_t
                dst_ref[lo:lo + LANES, :] = dprev
            dCBb = dCB.astype(bf16)
            dxbc_ref[:, inner + g * STATE:inner + (g + 1) * STATE] = dBg + _dot(dCBb, Cg, TN)
            dxbc_ref[:, inner + GN + g * STATE:inner + GN + (g + 1) * STATE] = dCg + _dot(dCBb, Bg, NN)
        dda = _dot01_left(cidx >= r, dA_ref[...]) + dAl_ref[...]
        ddt_full = ddtp_ref[...] + dda * ax_ref[...] * (1.0 / HEADDIM)
        ddt_ref[...] = _dot01_right(ddt_full, et_ref[...])
        dax_ref[...] += jnp.sum(dda * dtx_ref[...], axis=0, keepdims=True)

    xs_s, b_s, c_s, row_s, vec_s, st_s = _ssd_specs(S, inner, GN, nchunks, True)
    return pl.pallas_call(
        body, name=name, grid=(nchunks,),
        in_specs=[row_s, row_s, xs_s, b_s, c_s, row_s, vec_s, vec_s, st_s, _full_spec(et.shape)],
        out_specs=[pl.BlockSpec((L, Cc), lambda c: (nchunks - 1 - c, 0)),
                   pl.BlockSpec((L, H), lambda c: (nchunks - 1 - c, 0)), vec_s, vec_s],
        out_shape=[jax.ShapeDtypeStruct((S, Cc), f32), jax.ShapeDtypeStruct((S, H), f32),
                   jax.ShapeDtypeStruct((1, inner), f32), jax.ShapeDtypeStruct((1, inner), f32)],
        scratch_shapes=[pltpu.VMEM((inner, STATE), f32), pltpu.VMEM((L, inner), f32),
                        pltpu.VMEM((1, inner), f32), pltpu.VMEM((L, inner), f32),
                        pltpu.VMEM((L, inner), f32), pltpu.VMEM((1, inner), f32)],
        compiler_params=_cparams(("arbitrary",)),
    )(dy, y, xbc, xbc, xbc, dtx, ax, dx, states, et)


def _dt_fwd(dt_pre, bias, e, name):
    H, inner = e.shape

    def fn(dp, b, e_):
        dt = _softplus(dp + b)
        return dt, _dot01_right(dt, e_)

    return _rowcall(fn, name=name, rows=[dt_pre], fulls=[_row2(bias), e], out_rows=[(H, f32), (inner, f32)])


def _dt_bwd(ddt, dt_pre, bias, name):
    H = ddt.shape[1]

    def fn(dd, dp, b):
        g = dd * _sigmoid(dp + b)
        return g, jnp.sum(g, axis=0, keepdims=True)

    return _rowcall(fn, name=name, rows=[ddt, dt_pre], fulls=[_row2(bias)], out_rows=[(H, f32)], out_accs=[(1, H)])


def _gnorm_fwd(y, z, w, G, name):
    inner = y.shape[1]
    gs = inner // G

    def fn(y_, z_, w_):
        gg = y_ * _silu(z_)
        outs = []
        for g in range(G):
            sl = slice(g * gs, (g + 1) * gs)
            outs.append(_rms(gg[:, sl], w_[:, sl], LN_EPS))
        return jnp.concatenate(outs, axis=1)

    return _rowcall(fn, name=name, rows=[y, z], fulls=[_row2(w)], out_rows=[(inner, bf16)], tm=256)[0]


def _gnorm_bwd(dyn, y, z, w, G, name):
    inner = y.shape[1]
    gs = inner // G

    def fn(d_, y_, z_, w_):
        sz = _silu(z_)
        gg = y_ * sz
        dgs, dws = [], []
        for g in range(G):
            sl = slice(g * gs, (g + 1) * gs)
            dg, dw = _rms_bwd(d_[:, sl], gg[:, sl], w_[:, sl], LN_EPS)
            dgs.append(dg)
            dws.append(dw)
        dgg = jnp.concatenate(dgs, axis=1)
        return dgg * sz, dgg * y_ * _dsilu(z_), jnp.concatenate(dws, axis=1)

    return _rowcall(fn, name=name, rows=[dyn, y, z], fulls=[_row2(w)], out_rows=[(inner, f32), (inner, f32)],
                    out_accs=[(1, inner)], tm=256)


def _gmlp_parts(pre, lw, lb, I):
    hp = _gelu(pre)
    uu = hp[:, :I]
    vp = hp[:, I:]
    xc = vp - jnp.mean(vp, axis=-1, keepdims=True)
    rstd = lax.rsqrt(jnp.mean(xc * xc, axis=-1, keepdims=True) + LN_EPS)
    vhat = xc * rstd
    return uu, vhat, rstd, vhat * lw + lb


def _gmlp_mid_fwd(pre, b_in, ln_w, ln_b, w_s, bsx, name):
    S, two_i = pre.shape
    I = two_i // 2
    NG = w_s.shape[0]
    gd = I // NG
    L = CHUNK

    def body(pre_ref, bi_ref, lw_ref, lb_ref, ws_ref, bsx_ref, o_ref):
        uu, _, _, vv = _gmlp_parts(pre_ref[...] + bi_ref[...], lw_ref[...], lb_ref[...], I)
        r = lax.broadcasted_iota(jnp.int32, (L, L), 0)
        cidx = lax.broadcasted_iota(jnp.int32, (L, L), 1)
        tril = cidx <= r
        for g in range(NG):
            sl = slice(g * gd, (g + 1) * gd)
            wg = jnp.where(tril, ws_ref[g], 0.0).astype(bf16)
            mixed = _dot(wg, vv[:, sl].astype(bf16), NN) + bsx_ref[:, sl]
            o_ref[:, sl] = (uu[:, sl] * mixed).astype(o_ref.dtype)

    return pl.pallas_call(
        body, name=name, grid=(S // L,),
        in_specs=[_row_spec(L, two_i), _full_spec((1, two_i)), _full_spec((1, I)), _full_spec((1, I)), _full_spec(w_s.shape), _full_spec(bsx.shape)],
        out_specs=_row_spec(L, I), out_shape=jax.ShapeDtypeStruct((S, I), bf16),
        compiler_params=_cparams(("parallel",)),
    )(pre, _row2(b_in), _row2(ln_w), _row2(ln_b), w_s, bsx)


def _gmlp_mid_bwd(do, pre, b_in, ln_w, ln_b, w_s, bsx, name):
    S, two_i = pre.shape
    I = two_i // 2
    NG = w_s.shape[0]
    gd = I // NG
    L = CHUNK

    def body(do_ref, pre_ref, bi_ref, lw_ref, lb_ref, ws_ref, bsx_ref, dpre_ref, dbi_ref, dlw_ref, dlb_ref, dws_ref, dbs_ref, dvv_ref):
        ci = pl.program_id(0)

        @pl.when(ci == 0)
        def _():
            for ref in (dbi_ref, dlw_ref, dlb_ref, dws_ref, dbs_ref):
                ref[...] = jnp.zeros_like(ref)

        pre = pre_ref[...] + bi_ref[...]
        lw = lw_ref[...]
        uu, vhat, rstd, vv = _gmlp_parts(pre, lw, lb_ref[...], I)
        dov = do_ref[...]
        r = lax.broadcasted_iota(jnp.int32, (L, L), 0)
        cidx = lax.broadcasted_iota(jnp.int32, (L, L), 1)
        tril = cidx <= r
        duus = []
        for g in range(NG):
            sl = slice(g * gd, (g + 1) * gd)
            wg = jnp.where(tril, ws_ref[g], 0.0).astype(bf16)
            vg = vv[:, sl].astype(bf16)
            mixed = _dot(wg, vg, NN) + bsx_ref[:, sl]
            duus.append(dov[:, sl] * mixed)
            dmixed = dov[:, sl] * uu[:, sl]
            dbs_ref[:, sl] += dmixed
            dmb = dmixed.astype(bf16)
            dvv_ref[:, sl] = _dot(wg, dmb, TN)
            dws_ref[g] += jnp.where(tril, _dot(dmb, vg, NT), 0.0)
        duu = jnp.concatenate(duus, axis=1)
        dvv = dvv_ref[...]
        dlw_ref[...] += jnp.sum(dvv * vhat, axis=0, keepdims=True)
        dlb_ref[...] += jnp.sum(dvv, axis=0, keepdims=True)
        dvh = dvv * lw
        dvp = rstd * (dvh - jnp.mean(dvh, axis=-1, keepdims=True) - vhat * jnp.mean(dvh * vhat, axis=-1, keepdims=True))
        dpre = jnp.concatenate([duu, dvp], axis=1) * _dgelu(pre)
        dbi_ref[...] += jnp.sum(dpre, axis=0, keepdims=True)
        dpre_ref[...] = dpre.astype(dpre_ref.dtype)

    return pl.pallas_call(
        body, name=name, grid=(S // L,),
        in_specs=[_row_spec(L, I), _row_spec(L, two_i), _full_spec((1, two_i)), _full_spec((1, I)), _full_spec((1, I)),
                  _full_spec(w_s.shape), _full_spec(bsx.shape)],
        out_specs=[_row_spec(L, two_i), _full_spec((1, two_i)), _full_spec((1, I)), _full_spec((1, I)), _full_spec(w_s.shape), _full_spec((L, I))],
        out_shape=[jax.ShapeDtypeStruct((S, two_i), bf16), jax.ShapeDtypeStruct((1, two_i), f32), jax.ShapeDtypeStruct((1, I), f32),
                   jax.ShapeDtypeStruct((1, I), f32), jax.ShapeDtypeStruct(w_s.shape, f32), jax.ShapeDtypeStruct((L, I), f32)],
        scratch_shapes=[pltpu.VMEM((L, I), f32)],
        compiler_params=_cparams(("arbitrary",)),
    )(do, pre, _row2(b_in), _row2(ln_w), _row2(ln_b), w_s, bsx)


def _lane_group_sum(acc, eg, name):
    NG = eg.shape[1]
    return _rowcall(lambda a, e: _dot(a, e, NN, HI), name=name, rows=[acc], fulls=[eg], out_rows=[(NG, f32)])[0]


def _ffn_fwd_fused(h1, nf_w, wg, wu, wd, name):
    S, D = h1.shape
    nb, F4, _ = wg.shape
    tm = _tile(S, 512)

    def body(h_ref, nf_ref, wg_ref, wu_ref, wd_ref, h2_ref, u_ref, g_ref, up_ref, a_ref, acc_ref):
        k = pl.program_id(1)

        @pl.when(k == 0)
        def _():
            u_ref[...] = _rms(h_ref[...], nf_ref[...], RMS_EPS).astype(u_ref.dtype)
            acc_ref[...] = jnp.zeros_like(acc_ref)

        for r in range(FFN_SUBTILES):
            rs = pl.ds(r * (tm // FFN_SUBTILES), tm // FFN_SUBTILES)
            uv = u_ref[rs, :]
            g = _dot(uv, wg_ref[...], NT)
            up = _dot(uv, wu_ref[...], NT)
            a = (_silu(g) * up).astype(bf16)
            g_ref[rs, :] = g.astype(g_ref.dtype)
            up_ref[rs, :] = up.astype(up_ref.dtype)
            a_ref[rs, :] = a
            acc_ref[rs, :] += _dot(a, wd_ref[...], NN)

        @pl.when(k == nb - 1)
        def _():
            h2_ref[...] = h_ref[...] + acc_ref[...]

    row = pl.BlockSpec((tm, D), lambda i, k: (i, 0))
    wspec = pl.BlockSpec((None, F4, D), lambda i, k: (k, 0, 0))
    cspec = pl.BlockSpec((None, tm, F4), lambda i, k: (k, i, 0))
    chunk = jax.ShapeDtypeStruct((nb, S, F4), bf16)
    return pl.pallas_call(
        body, name=name, grid=(S // tm, nb),
        in_specs=[row, _full_spec((1, D)), wspec, wspec, pl.BlockSpec((None, F4, D), lambda i, k: (k, 0, 0))],
        out_specs=[row, row, cspec, cspec, cspec],
        out_shape=[jax.ShapeDtypeStruct((S, D), f32), jax.ShapeDtypeStruct((S, D), bf16), chunk, chunk, chunk],
        scratch_shapes=[pltpu.VMEM((tm, D), f32)],
        compiler_params=_cparams(("parallel", "arbitrary")),
    )(h1, _row2(nf_w), wg, wu, wd)


def _ffn_bwd_fused(dh, h1, nf_w, wd, wg, wu, G, U, name, after=None):
    S, D = dh.shape
    nb, F4, _ = wd.shape
    tm = _tile(S, 512)

    def body(dh_ref, h_ref, nf_ref, wd_ref, wg_ref, wu_ref, g_ref, up_ref, *rest):
        dg_ref, du_ref, dh1_ref, dnf_ref, acc_ref = rest[-5:]
        i, k = pl.program_id(0), pl.program_id(1)

        @pl.when(k == 0)
        def _():
            acc_ref[...] = jnp.zeros_like(acc_ref)

        for r in range(FFN_SUBTILES):
            rs = pl.ds(r * (tm // FFN_SUBTILES), tm // FFN_SUBTILES)
            dA = _dot(dh_ref[rs, :].astype(bf16), wd_ref[...], NT)
            g = g_ref[rs, :].astype(f32)
            dg = (dA * up_ref[rs, :].astype(f32) * _dsilu(g)).astype(bf16)
            du = (dA * _silu(g)).astype(bf16)
            dg_ref[rs, :] = dg
            du_ref[rs, :] = du
            acc_ref[rs, :] += _dot(dg, wg_ref[...], NN) + _dot(du, wu_ref[...], NN)

        @pl.when(k == nb - 1)
        def _():
            dx, dw = _rms_bwd(acc_ref[...], h_ref[...], nf_ref[...], RMS_EPS)
            dh1_ref[...] = dh_ref[...] + dx

            @pl.when(i == 0)
            def _():
                dnf_ref[...] = dw

            @pl.when(i > 0)
            def _():
                dnf_ref[...] += dw

    row = pl.BlockSpec((tm, D), lambda i, k: (i, 0))
    wspec = pl.BlockSpec((None, F4, D), lambda i, k: (k, 0, 0))
    cspec = pl.BlockSpec((None, tm, F4), lambda i, k: (k, i, 0))
    chunk = jax.ShapeDtypeStruct((nb, S, F4), bf16)
    return pl.pallas_call(
        body, name=name, grid=(S // tm, nb),
        in_specs=[row, row, _full_spec((1, D)), wspec, wspec, wspec, cspec, cspec] + ([] if after is None else [pl.BlockSpec(memory_space=pl.ANY)]),
        out_specs=[cspec, cspec, row, _full_spec((1, D))],
        out_shape=[chunk, chunk, jax.ShapeDtypeStruct((S, D), f32), jax.ShapeDtypeStruct((1, D), f32)],
        scratch_shapes=[pltpu.VMEM((tm, D), f32)],
        compiler_params=_cparams(("arbitrary", "arbitrary")),
    )(dh, h1, _row2(nf_w), wd, wg, wu, G, U, *([] if after is None else [after]))


def _rms_bwd_add(dres, du, h, w, name):
    D = h.shape[1]

    def fn(dr, du_, h_, w_):
        dx, dw = _rms_bwd(du_, h_, w_, RMS_EPS)
        return dr + dx, dw

    return _rowcall(fn, name=name, rows=[dres, du, h], fulls=[_row2(w)], out_rows=[(D, f32)], out_accs=[(1, D)])


def _ple_fwd(h, p_i, wp, pn, gn, wgate, name):
    D = h.shape[1]

    def fn(h_, p_, wp_, pn_, gn_, wg_):
        pe = _dot(p_.astype(bf16), wp_, NN)
        e = _rms(pe, pn_, RMS_EPS)
        q = _rms(h_, gn_, RMS_EPS)
        gate = _sigmoid(_dot(q.astype(bf16), wg_, NN))
        return h_ + gate * e, pe, gate

    return _rowcall(fn, name=name, rows=[h, p_i], fulls=[wp, _row2(pn), _row2(gn), wgate],
                    out_rows=[(D, f32), (D, bf16), (D, bf16)], tm=256)


def _ple_bwd(dh3, h, pe, gate, pn, gn, wgate, name, after=None):
    D = h.shape[1]

    def fn(d_, h_, pe_, gate_, pn_, gn_, wg_, *_):
        pe_, gate_ = pe_.astype(f32), gate_.astype(f32)
        e = _rms(pe_, pn_, RMS_EPS)
        dzg = d_ * e * gate_ * (1.0 - gate_)
        dq = _dot(dzg.astype(bf16), wg_, NT)
        dxq, dgn = _rms_bwd(dq, h_, gn_, RMS_EPS)
        dpe, dpn = _rms_bwd(d_ * gate_, pe_, pn_, RMS_EPS)
        return d_ + dxq, dzg, dpe, _rms(h_, gn_, RMS_EPS), dpn, dgn

    return _rowcall(fn, name=name, rows=[dh3, h, pe, gate], fulls=[_row2(pn), _row2(gn), wgate] + ([] if after is None else [after]),
                    out_rows=[(D, f32), (D, bf16), (D, bf16), (D, bf16)], out_accs=[(1, D), (1, D)], tm=256)


def _loss_head(h, target, fn_w, name):
    D = h.shape[1]

    def fn(h_, t_, w_):
        diff = _rms(h_, w_, RMS_EPS) - t_
        loss = 0.5 * jnp.sum(jnp.mean(diff * diff, axis=-1, keepdims=True), axis=0, keepdims=True)
        dh, dw = _rms_bwd(diff * (1.0 / D), h_, w_, RMS_EPS)
        return dh, jnp.broadcast_to(loss, (1, LANES)), dw

    return _rowcall(fn, name=name, rows=[h, target], fulls=[_row2(fn_w)], out_rows=[(D, f32)], out_accs=[(1, LANES), (1, D)])


def _adamw(w, m, v, g, name):
    R, C = w.shape
    tr, tc = R, C
    while tr * tc > 256 * 1024 and tr % (2 * SUBLANES) == 0:
        tr //= 2
    while tr * tc > 256 * 1024 and tc % (2 * LANES) == 0:
        tc //= 2

    def body(w_ref, m_ref, v_ref, g_ref, d_ref, mo_ref, vo_ref):
        g = g_ref[...]
        mn = ADAM_B1 * m_ref[...] + (1.0 - ADAM_B1) * g
        vn = ADAM_B2 * v_ref[...] + (1.0 - ADAM_B2) * (g * g)
        m_hat = mn / (1.0 - ADAM_B1 ** ADAM_STEP)
        v_hat = vn / (1.0 - ADAM_B2 ** ADAM_STEP)
        d_ref[...] = -ADAM_LR * (m_hat / (jnp.sqrt(v_hat) + ADAM_EPS) + ADAM_WD * w_ref[...])
        mo_ref[...] = mn
        vo_ref[...] = vn

    spec = pl.BlockSpec((tr, tc), lambda i, j: (i, j))
    return pl.pallas_call(
        body, name=name, grid=(R // tr, C // tc), in_specs=[spec] * 4,
        out_specs=[spec] * 3, out_shape=[jax.ShapeDtypeStruct((R, C), f32)] * 3,
        compiler_params=_cparams(("parallel", "parallel")),
    )(w, m, v, g)


def _expand_onehot(n, per):
    lane = lax.broadcasted_iota(jnp.int32, (n, n * per), 1)
    row = lax.broadcasted_iota(jnp.int32, (n, n * per), 0)
    return (lane // per == row).astype(f32)


def _ssd_layer_fwd(h, nm_w, W, t):
    H = W["dt_bias"].shape[0]
    inner = H * HEADDIM
    G = (W["conv_b"].shape[0] - inner) // (2 * STATE)
    hn = _rms_fwd(h, nm_w, f"rms_mix_{t}")
    conv_dim = W["conv_b"].shape[0]
    wT = W["w_inT"]
    z = _mm(hn, wT, mode="nt", brows=(0, inner), name=f"ssd_z_{t}")
    xpre = _mm(hn, wT, mode="nt", brows=(inner, conv_dim), name=f"ssd_xbc_{t}")
    dt_pre = _mm(hn, wT, mode="nt", brows=(inner + conv_dim, H), name=f"ssd_dt_{t}")
    xbc = _conv_fwd(xpre, W["conv_w"], W["conv_b"], f"ssd_conv_{t}")
    _, dtx = _dt_fwd(dt_pre, W["dt_bias"], _expand_onehot(H, HEADDIM), f"ssd_dtx_{t}")
    a = -jnp.exp(W["a_log"])
    ax = _row2(jnp.repeat(a, HEADDIM))
    dx = _row2(jnp.repeat(W["d"], HEADDIM))
    y, states = _ssd_fwd(xbc, dtx, ax, dx, G, f"ssd_scan_{t}")
    yn = _gnorm_fwd(y, z, W["norm_w"], G, f"ssd_gnorm_{t}")
    h1 = _mm(yn, W["wout"], res=h, name=f"ssd_out_{t}")
    return h1, (h, hn, z, xpre, dt_pre, xbc, dtx, a, ax, dx, y, states, yn)


def _ssd_layer_bwd(dh1, saved, nm_w, W, t, after=None):
    h, hn, z, xpre, dt_pre, xbc, dtx, a, ax, dx, y, states, yn = saved
    H = W["dt_bias"].shape[0]
    inner = H * HEADDIM
    G = (W["conv_b"].shape[0] - inner) // (2 * STATE)
    dyn = _mm(dh1, W["wout"], mode="nt", after=after, name=f"ssd_out_dx_{t}")
    g_wout = _mm(yn, dh1, mode="tn", out_dtype=bf16, name=f"ssd_out_dw_{t}")
    dy, dz, g_normw = _gnorm_bwd(dyn, y, z, W["norm_w"], G, f"ssd_gnorm_bwd_{t}")
    dxbc, ddt, dax, ddx = _ssd_bwd(dy, y, xbc, dtx, ax, dx, states, _expand_onehot(H, HEADDIM).T, G, f"ssd_scan_bwd_{t}")
    dc, g_convw8, g_convb = _conv_bwd_dc(dxbc, xpre, W["conv_w"], W["conv_b"], f"ssd_conv_bwd_dc_{t}")
    dxpre = _conv_bwd_dx(dc, W["conv_w"], f"ssd_conv_bwd_dx_{t}")
    ddt_pre, g_dtb = _dt_bwd(ddt, dt_pre, W["dt_bias"], f"ssd_dt_bwd_{t}")
    conv_dim = W["conv_b"].shape[0]
    wT = W["w_inT"]
    g_wz = _mm(dz, hn, mode="tn", out_dtype=bf16, name=f"ssd_z_dw_{t}")
    g_wxbc = _mm(dxpre, hn, mode="tn", out_dtype=bf16, name=f"ssd_xbc_dw_{t}")
    g_wdt = _mm(ddt_pre, hn, mode="tn", out_dtype=bf16, name=f"ssd_dt_dw_{t}")
    dhn = _mm(dz, wT, brows=(0, inner), name=f"ssd_z_dx_{t}")
    dhn = _mm(dxpre, wT, brows=(inner, conv_dim), res=dhn, name=f"ssd_xbc_dx_{t}")
    dhn = _mm(ddt_pre, wT, brows=(inner + conv_dim, H), res=dhn, name=f"ssd_dt_dx_{t}")
    dh, g_nm = _rms_bwd_add(dh1, dhn, h, nm_w, f"rms_mix_bwd_{t}")
    grads = dict(
        w_inT=jnp.concatenate([g_wz, g_wxbc, g_wdt], axis=0), wout=g_wout,
        conv_w=g_convw8[:CONV_K], conv_b=g_convb[0], dt_bias=g_dtb[0],
        a_log=dax[0].reshape(H, HEADDIM)[:, 0] * a, d=jnp.sum(ddx[0].reshape(H, HEADDIM), axis=1),
        norm_w=g_normw[0], norm_mix=g_nm[0])
    return dh, grads


def _gmlp_layer_fwd(h, nm_w, W, t):
    NG, L, _ = W["w_s"].shape
    I = W["ln_w"].shape[0]
    hn = _rms_fwd(h, nm_w, f"rms_mix_{t}")
    pre = _mm(hn, W["win"], name=f"gmlp_in_{t}")
    bsx = jnp.repeat(W["b_s"].T, I // NG, axis=1)
    o = _gmlp_mid_fwd(pre, W["b_in"], W["ln_w"], W["ln_b"], W["w_s"], bsx, f"gmlp_mid_{t}")
    h1 = _mm(o, W["wout"], res=h, name=f"gmlp_out_{t}")
    return h1, (h, hn, pre, bsx, o)


def _gmlp_layer_bwd(dh1, saved, nm_w, W, t, after=None):
    h, hn, pre, bsx, o = saved
    NG = W["w_s"].shape[0]
    I = W["ln_w"].shape[0]
    do = _mm(dh1, W["wout"], mode="nt", after=after, name=f"gmlp_out_dx_{t}")
    g_wout = _mm(o, dh1, mode="tn", out_dtype=bf16, name=f"gmlp_out_dw_{t}")
    dpre, g_bin, g_lnw, g_lnb, g_ws, dbs = _gmlp_mid_bwd(do, pre, W["b_in"], W["ln_w"], W["ln_b"], W["w_s"], bsx, f"gmlp_mid_bwd_{t}")
    g_bs = _lane_group_sum(dbs, _expand_onehot(NG, I // NG).T, f"gmlp_bs_{t}").T
    g_win = _mm(hn, dpre, mode="tn", out_dtype=bf16, name=f"gmlp_in_dw_{t}")
    dhn = _mm(dpre, W["win"], mode="nt", name=f"gmlp_in_dx_{t}")
    dh, g_nm = _rms_bwd_add(dh1, dhn, h, nm_w, f"rms_mix_bwd_{t}")
    grads = dict(win=g_win, wout=g_wout, b_in=g_bin[0], ln_w=g_lnw[0], ln_b=g_lnb[0], w_s=g_ws, b_s=g_bs, norm_mix=g_nm[0])
    return dh, grads


def _ffn_fwd(h1, nf_w, W, t):
    h2, u, Gm, Um, A = _ffn_fwd_fused(h1, nf_w, W["wg"], W["wu"], W["wd"], f"ffn_fwd_{t}")
    return h2, (h1, u, Gm, Um, A)


def _ffn_bwd(dh2, saved, nf_w, W, t, after=None):
    h1, u, Gm, Um, A = saved
    dG, dU, dh1, g_nf = _ffn_bwd_fused(dh2, h1, nf_w, W["wd"], W["wg"], W["wu"], Gm, Um, f"ffn_bwd_{t}", after=after)
    g_wd = _mm(A, dh2, mode="tn", out_dtype=bf16, name=f"ffn_down_dw_{t}")
    g_wg = _mm(dG, u, mode="tn", out_dtype=bf16, name=f"ffn_gate_dw_{t}")
    g_wu = _mm(dU, u, mode="tn", out_dtype=bf16, name=f"ffn_up_dw_{t}")
    return dh1, dict(wg=g_wg, wu=g_wu, wd=g_wd, norm_ffn=g_nf[0])


def _local_step(x, p, target, norms, layer_weights, on_layer_grads=None, final_norm_grad=None):
    depth = p.shape[0]
    h = x
    saved = []
    for i in range(depth):
        Wm = layer_weights(i, "mix", h)
        if i % 2 == 0:
            h1, s_mix = _ssd_layer_fwd(h, norms["norm_mix"][i], Wm, i)
        else:
            h1, s_mix = _gmlp_layer_fwd(h, norms["norm_mix"][i], Wm, i)
        Wf = layer_weights(i, "ffn", h1)
        h2, s_ffn = _ffn_fwd(h1, norms["norm_ffn"][i], Wf, i)
        P = layer_weights(i, "ple", h2)
        h3, pe, gate = _ple_fwd(h2, p[i], P["wp"], P["pn"], P["gn"], P["wgate"], f"ple_{i}")
        saved.append((Wm, Wf, P, s_mix, s_ffn, (h2, pe, gate)))
        h = h3
    dh, loss, g_fn = _loss_head(h, target, norms["final_norm"], "loss_head")
    if final_norm_grad is not None:
        final_norm_grad[0] = g_fn[0]
    grads = [None] * depth
    tell = on_layer_grads if on_layer_grads is not None else (lambda i, part, g: None)
    after = None
    for i in reversed(range(depth)):
        Wm, Wf, P, s_mix, s_ffn, (h2, pe, gate) = saved[i]
        dh, dzg, dpe, q, g_pn, g_gn = _ple_bwd(dh, h2, pe, gate, P["pn"], P["gn"], P["wgate"], f"ple_bwd_{i}", after=after)
        g_ple = dict(wgate=_mm(q, dzg, mode="tn", out_dtype=bf16, name=f"ple_gate_dw_{i}"),
                     wp=_mm(p[i], dpe, mode="tn", out_dtype=bf16, name=f"ple_proj_dw_{i}"), pn=g_pn[0], gn=g_gn[0])
        after = tell(i, "ple", g_ple)
        dh, g_ffn = _ffn_bwd(dh, s_ffn, norms["norm_ffn"][i], Wf, i, after=after)
        after = tell(i, "ffn", g_ffn)
        if i % 2 == 0:
            dh, g_mix = _ssd_layer_bwd(dh, s_mix, norms["norm_mix"][i], Wm, i, after=after)
        else:
            dh, g_mix = _gmlp_layer_bwd(dh, s_mix, norms["norm_mix"][i], Wm, i, after=after)
        after = tell(i, "mix", g_mix)
        grads[i] = dict(mix=g_mix, ffn=g_ffn, ple=g_ple)
    return loss[0, 0], dh, g_fn[0], grads


def _flip(v, f):
    return 1 - v if f else v


_ANY = pl.BlockSpec(memory_space=pl.ANY)


_SEM = pl.BlockSpec(memory_space=pltpu.SEMAPHORE)
_DATAFLOW = pltpu.SideEffectType.DATAFLOW_SIDE_EFFECTING
_CHIP_FLIPS = ((1, 0), (0, 1), (1, 1))
_DMA = pltpu.SemaphoreType.DMA


def _structs(arrs):
    return [jax.ShapeDtypeStruct(a.shape, a.dtype) for a in arrs]


def _gather_copy(src, buf, send_sems, recv_sems, k, j, slot, x, y, c):
    c2 = src.shape[1] // 2
    fx, fy = _CHIP_FLIPS[j]
    nf = len(_CHIP_FLIPS)
    return pltpu.make_async_remote_copy(
        src_ref=src.at[:, pl.ds(c * c2, c2)], dst_ref=buf.at[slot, :, pl.ds(c * c2, c2)], send_sem=send_sems.at[nf * k + j],
        recv_sem=recv_sems.at[nf * k + j], device_id=(_flip(x, fx), _flip(y, fy), c), device_id_type=MESH)


def _gather_start(srcs, groups):
    n = len(srcs)
    ng = len(groups)
    nf = len(_CHIP_FLIPS)

    def body(*refs):
        src_refs, buf_refs, sems = refs[:n], refs[n:2 * n], refs[4 * n:]
        x, y, c = lax.axis_index("x"), lax.axis_index("y"), lax.axis_index("c")
        for gi, group in enumerate(groups):
            for k, o in enumerate(group):
                for j in range(nf):
                    _gather_copy(src_refs[o], buf_refs[o], sems[2 * gi], sems[2 * gi + 1], k, j, 2 * x + y, x, y, c).start()

    mychip = 2 * lax.axis_index("x") + lax.axis_index("y")
    inits = [lax.dynamic_update_slice(lax.empty((N_CHIPS,) + s.shape, s.dtype), s[None], (mychip, 0, 0)) for s in srcs]
    sem_shapes = [_DMA((nf * len(g),)) for g in groups for _ in range(2)]
    outs = pl.pallas_call(
        body, name="gather_start", in_specs=[_ANY] * (2 * n), out_specs=[_ANY] * (2 * n) + [_SEM] * (2 * ng),
        out_shape=_structs(srcs) + _structs(inits) + sem_shapes, input_output_aliases={i: i for i in range(2 * n)},
        compiler_params=pltpu.CompilerParams(has_side_effects=_DATAFLOW),
    )(*srcs, *inits)
    return outs[:n], outs[n:2 * n], [(outs[2 * n + 2 * gi], outs[2 * n + 2 * gi + 1]) for gi in range(ng)]


def _gather_wait(srcs, bufs, sems, after, name):
    n = len(srcs)
    nf = len(_CHIP_FLIPS)

    def body(*refs):
        src_refs, buf_refs, send_sems, recv_sems = refs[:n], refs[n:2 * n], refs[2 * n], refs[2 * n + 1]
        x, y, c = lax.axis_index("x"), lax.axis_index("y"), lax.axis_index("c")
        for k in range(n):
            for j, (fx, fy) in enumerate(_CHIP_FLIPS):
                cp = _gather_copy(src_refs[k], buf_refs[k], send_sems, recv_sems, k, j, 2 * _flip(x, fx) + _flip(y, fy), x, y, c)
                cp.wait_send()
                cp.wait_recv()

    outs = pl.pallas_call(
        body, name=name, in_specs=[_ANY] * (2 * n) + [_SEM, _SEM, _ANY], out_specs=[_ANY] * (2 * n),
        out_shape=_structs(srcs) + _structs(bufs), input_output_aliases={i: i for i in range(2 * n)},
        compiler_params=pltpu.CompilerParams(has_side_effects=_DATAFLOW),
    )(*srcs, *bufs, *sems, after)
    return outs[n:]


def _gather_forward(bufs, name):
    n = len(bufs)
    nf = len(_CHIP_FLIPS)

    def body(*refs):
        outs = refs[n:2 * n]
        send_sems, recv_sems = refs[2 * n:]
        x, y, c = lax.axis_index("x"), lax.axis_index("y"), lax.axis_index("c")

        def forward(k, j, h):
            c2 = bufs[k].shape[2] // 2
            fx, fy = _CHIP_FLIPS[j]
            part = outs[k].at[2 * _flip(x, fx) + _flip(y, fy), :, pl.ds(h * c2, c2)]
            return pltpu.make_async_remote_copy(src_ref=part, dst_ref=part, send_sem=send_sems.at[nf * k + j],
                                                recv_sem=recv_sems.at[nf * k + j], device_id=(x, y, 1 - c), device_id_type=MESH)

        sends = [forward(k, j, c) for k in range(n) for j in range(nf)]
        for cp in sends:
            cp.start()
        for k in range(n):
            for j in range(nf):
                forward(k, j, 1 - c).wait_recv()
        for cp in sends:
            cp.wait_send()

    return pl.pallas_call(
        body, name=name, in_specs=[_ANY] * n, out_specs=[_ANY] * n, out_shape=_structs(bufs),
        input_output_aliases={i: i for i in range(n)}, scratch_shapes=[_DMA((nf * n,)), _DMA((nf * n,))],
    )(*bufs)


def _half_struct(a, lead):
    return jax.ShapeDtypeStruct(lead + (a.shape[-2], a.shape[-1] // 2), a.dtype)


_DEVICE_FLIPS = tuple((f >> 2 & 1, f >> 1 & 1, f & 1) for f in range(1, N_DEV))


def _exchange_copy(srcs, lands, n, send_sems, recv_sems, i, j, slot, x, y, c):
    nf = len(_DEVICE_FLIPS)
    px, py, pc = (_flip(v, f) for v, f in zip((x, y, c), _DEVICE_FLIPS[j]))
    src = srcs[i]
    if i < n:
        c2 = src.shape[2] // 2
        src = src.at[2 * px + py, :, pl.ds(pc * c2, c2)]
    return pltpu.make_async_remote_copy(src_ref=src, dst_ref=lands[i].at[slot], send_sem=send_sems.at[nf * i + j],
                                        recv_sem=recv_sems.at[nf * i + j], device_id=(px, py, pc), device_id_type=MESH)


def _exchange_start(tensors, wholes, name):
    n, m = len(tensors), len(wholes)
    nf = len(_DEVICE_FLIPS)
    t = n + m
    land_structs = ([_half_struct(a, (N_DEV,)) for a in tensors] + [jax.ShapeDtypeStruct((N_DEV,) + w.shape, w.dtype) for w in wholes])

    def body(*refs):
        srcs, lands, send_sems, recv_sems, token = refs[:t], refs[2 * t:3 * t], refs[3 * t], refs[3 * t + 1], refs[3 * t + 2]
        x, y, c = lax.axis_index("x"), lax.axis_index("y"), lax.axis_index("c")
        for i in range(t):
            for j in range(nf):
                _exchange_copy(srcs, lands, n, send_sems, recv_sems, i, j, 4 * x + 2 * y + c, x, y, c).start()
        token[...] = jnp.zeros_like(token)

    outs = pl.pallas_call(
        body, name=name, in_specs=[_ANY] * t,
        out_specs=[_ANY] * (2 * t) + [_SEM, _SEM, pl.BlockSpec(memory_space=pltpu.VMEM)],
        out_shape=_structs(tensors) + _structs(wholes) + land_structs + [_DMA((nf * t,)), _DMA((nf * t,)),
                                                                          jax.ShapeDtypeStruct((SUBLANES, LANES), f32)],
        input_output_aliases={i: i for i in range(t)},
        compiler_params=pltpu.CompilerParams(has_side_effects=_DATAFLOW),
    )(*tensors, *wholes)
    return outs[:t], outs[t:2 * t], (outs[2 * t], outs[2 * t + 1]), outs[2 * t + 2]


def _exchange_wait(srcs, lands, n, sems, after, name):
    t = len(srcs)

    def body(*refs):
        src_refs, land_refs, send_sems, recv_sems = refs[:t], refs[t:2 * t], refs[2 * t], refs[2 * t + 1]
        x, y, c = lax.axis_index("x"), lax.axis_index("y"), lax.axis_index("c")
        for i in range(t):
            for j, (fx, fy, fc) in enumerate(_DEVICE_FLIPS):
                sender = 4 * _flip(x, fx) + 2 * _flip(y, fy) + _flip(c, fc)
                cp = _exchange_copy(src_refs, land_refs, n, send_sems, recv_sems, i, j, sender, x, y, c)
                cp.wait_send()
                cp.wait_recv()

    outs = pl.pallas_call(
        body, name=name, in_specs=[_ANY] * (2 * t) + [_SEM, _SEM, _ANY], out_specs=[_ANY] * (2 * t),
        out_shape=_structs(srcs) + _structs(lands), input_output_aliases={i: i for i in range(2 * t)},
        compiler_params=pltpu.CompilerParams(has_side_effects=_DATAFLOW),
    )(*srcs, *lands, *sems, after)
    return outs[:t], outs[t:]


def _sibling_join(bufs, name):
    flat = [(gi, l) for gi, b in enumerate(bufs) for l in range(b.shape[0])]
    n, n_buf = len(flat), len(bufs)

    def body(*refs):
        outs = refs[n_buf:2 * n_buf]
        send_sems, recv_sems = refs[2 * n_buf:]
        x, y, c = lax.axis_index("x"), lax.axis_index("y"), lax.axis_index("c")

        def push(i, h):
            gi, l = flat[i]
            c2 = bufs[gi].shape[2] // 2
            part = outs[gi].at[l, :, pl.ds(h * c2, c2)]
            return pltpu.make_async_remote_copy(src_ref=part, dst_ref=part, send_sem=send_sems.at[i], recv_sem=recv_sems.at[i],
                                                device_id=(x, y, 1 - c), device_id_type=MESH)

        sends = [push(i, c) for i in range(n)]
        for cp in sends:
            cp.start()
        for i in range(n):
            push(i, 1 - c).wait_recv()
        for cp in sends:
            cp.wait_send()

    dma = pltpu.SemaphoreType.DMA
    return pl.pallas_call(
        body, name=name, in_specs=[_ANY] * n_buf, out_specs=[_ANY] * n_buf,
        out_shape=[jax.ShapeDtypeStruct(b.shape, b.dtype) for b in bufs],
        input_output_aliases={i: i for i in range(n_buf)},
        scratch_shapes=[dma((n,)), dma((n,))],
    )(*bufs)


def _device_sum(landed, own, place, name, into=None, layer=0, layers=1):
    ndev, R, C2 = landed.shape
    tr, tc = R, C2
    while ndev * tr * tc > 1024 * 1024 and tr % (4 * SUBLANES) == 0:
        tr //= 2
    while ndev * tr * tc > 1024 * 1024 and tc % (2 * LANES) == 0:
        tc //= 2
    ncb = C2 // tc

    def body(*refs):
        place_ref, l_ref, m_ref, o_ref = refs[0], refs[1], refs[2], refs[-1]
        me = 2 * place_ref[0] + place_ref[1]
        s = jnp.where(me == 0, m_ref[...].astype(f32), l_ref[0].astype(f32))
        for d in range(1, ndev):
            s = s + jnp.where(me == d, m_ref[...].astype(f32), l_ref[d].astype(f32))
        o_ref[...] = s

    in_specs = [pl.BlockSpec((ndev, tr, tc), lambda i, j, pr: (0, i, j)),
                pl.BlockSpec((None, tr, tc), lambda i, j, pr: (pr[0], i, pr[1] * ncb + j))]
    args = [place, landed, own]
    if into is not None:
        in_specs.append(_ANY)
        args.append(into)
    return pl.pallas_call(
        body, name=name, out_shape=jax.ShapeDtypeStruct((layers, R, 2 * C2), f32),
        grid_spec=pltpu.PrefetchScalarGridSpec(
            num_scalar_prefetch=1, grid=(R // tr, ncb), in_specs=in_specs,
            out_specs=pl.BlockSpec((None, tr, tc), lambda i, j, pr: (layer, i, pr[1] * ncb + j))),
        input_output_aliases={3: 0} if into is not None else {},
        compiler_params=_cparams(("parallel", "parallel")),
    )(*args)


def _device_sum_whole(landed, own, place, name):
    ndev, R, C = landed.shape
    tr = R
    while ndev * tr * C > 1024 * 1024 and tr % (2 * SUBLANES) == 0:
        tr //= 2

    def body(place_ref, l_ref, m_ref, o_ref):
        me = 2 * place_ref[0] + place_ref[1]
        s = jnp.where(me == 0, m_ref[...], l_ref[0])
        for d in range(1, ndev):
            s = s + jnp.where(me == d, m_ref[...], l_ref[d])
        o_ref[...] = s

    return pl.pallas_call(
        body, name=name, out_shape=jax.ShapeDtypeStruct((R, C), f32),
        grid_spec=pltpu.PrefetchScalarGridSpec(
            num_scalar_prefetch=1, grid=(R // tr,),
            in_specs=[pl.BlockSpec((ndev, tr, C), lambda i, pr: (0, i, 0)), pl.BlockSpec((tr, C), lambda i, pr: (i, 0))],
            out_specs=pl.BlockSpec((tr, C), lambda i, pr: (i, 0))),
        compiler_params=_cparams(("parallel",)),
    )(place, landed, own)


PACK_COLS = 1024
PACK_ROW_MULTIPLE = 64

BIG = ("ssd_w_in", "ssd_w_out", "gmlp_w_in", "gmlp_w_out", "ffn_w_gate", "ffn_w_up", "ffn_w_down", "ple_w_proj", "ple_w_gate")
SMALL_SHARDED = ("ssd_conv_w", "gmlp_b_in", "gmlp_ln_w", "gmlp_ln_b")
REP_EARLY = "gmlp_w_s"
REP_LATE = ("norm_mix", "norm_ffn", "ssd_conv_b", "ssd_dt_bias", "ssd_a_log", "ssd_d", "ssd_norm_w", "gmlp_b_s", "ple_norm",
            "ple_gate_norm", "final_norm")
WEIGHTS = ("norm_mix", "norm_ffn", "ssd_w_in", "ssd_conv_w", "ssd_conv_b", "ssd_dt_bias", "ssd_a_log", "ssd_d", "ssd_norm_w", "ssd_w_out",
           "gmlp_w_in", "gmlp_b_in", "gmlp_ln_w", "gmlp_ln_b", "gmlp_w_s", "gmlp_b_s", "gmlp_w_out", "ffn_w_gate", "ffn_w_up",
           "ffn_w_down", "ple_w_proj", "ple_norm", "ple_gate_norm", "ple_w_gate", "final_norm")
TRANSPOSED = ("ssd_w_in", "ffn_w_gate", "ffn_w_up")


def _pack(arrs):
    flat = jnp.concatenate([a.reshape(-1).astype(f32) for a in arrs])
    per = PACK_COLS * PACK_ROW_MULTIPLE
    n = -(-flat.shape[0] // per) * per
    return jnp.pad(flat, (0, n - flat.shape[0])).reshape(-1, PACK_COLS)


def _unpack(buf, shapes):
    flat = buf.reshape(-1)
    out, o = [], 0
    for s in shapes:
        n = math.prod(s)
        out.append(flat[o:o + n].reshape(s))
        o += n
    return out


def _chip_major(g):
    r, c4 = g.shape
    return g.reshape(r, N_CHIPS, c4 // N_CHIPS).transpose(1, 0, 2)


def _from_chip_major(g):
    k, r, c = g.shape
    return g.transpose(1, 0, 2).reshape(r, k * c)


def _adamw_nd(w, m, v, g, name):
    shp = w.shape
    two = lambda a: a.reshape(-1, shp[-1])
    return [o.reshape(shp) for o in _adamw(two(w), two(m), two(v), two(g), name)]


def kernel(x, p, norm_mix, norm_ffn, ssd_w_in, ssd_conv_w, ssd_conv_b, ssd_dt_bias, ssd_a_log, ssd_d, ssd_norm_w, ssd_w_out, gmlp_w_in, gmlp_b_in, gmlp_ln_w, gmlp_ln_b, gmlp_w_s, gmlp_b_s, gmlp_w_out, ffn_w_gate, ffn_w_up, ffn_w_down, ple_w_proj, ple_norm, ple_gate_norm, ple_w_gate, final_norm, loss_target, m_norm_mix, m_norm_ffn, m_ssd_w_in, m_ssd_conv_w, m_ssd_conv_b, m_ssd_dt_bias, m_ssd_a_log, m_ssd_d, m_ssd_norm_w, m_ssd_w_out, m_gmlp_w_in, m_gmlp_b_in, m_gmlp_ln_w, m_gmlp_ln_b, m_gmlp_w_s, m_gmlp_b_s, m_gmlp_w_out, m_ffn_w_gate, m_ffn_w_up, m_ffn_w_down, m_ple_w_proj, m_ple_norm, m_ple_gate_norm, m_ple_w_gate, m_final_norm, v_norm_mix, v_norm_ffn, v_ssd_w_in, v_ssd_conv_w, v_ssd_conv_b, v_ssd_dt_bias, v_ssd_a_log, v_ssd_d, v_ssd_norm_w, v_ssd_w_out, v_gmlp_w_in, v_gmlp_b_in, v_gmlp_ln_w, v_gmlp_ln_b, v_gmlp_w_s, v_gmlp_b_s, v_gmlp_w_out, v_ffn_w_gate, v_ffn_w_up, v_ffn_w_down, v_ple_w_proj, v_ple_norm, v_ple_gate_norm, v_ple_w_gate, v_final_norm):
    given = dict(locals())
    view = lambda n, a: jnp.swapaxes(a, 1, 2) if n in TRANSPOSED else a
    w = {n: view(n, given[n]) for n in WEIGHTS}
    mom = {n: view(n, given["m_" + n]) for n in WEIGHTS}
    var = {n: view(n, given["v_" + n]) for n in WEIGHTS}
    depth = p.shape[0]
    n_ssd, n_gmlp = ssd_w_in.shape[0], gmlp_w_in.shape[0]
    inner = ssd_dt_bias.shape[1] * HEADDIM
    conv_dim = ssd_conv_b.shape[1]

    place = jnp.stack([2 * lax.axis_index("x") + lax.axis_index("y"), lax.axis_index("c")]).astype(jnp.int32)

    def part_keys(i, part):
        j = i // 2
        if part == "mix":
            names = (("ssd_w_in", j), ("ssd_w_out", j)) if i % 2 == 0 else (("gmlp_w_in", j), ("gmlp_w_out", j))
            return ((("small", 0),) if i == 0 else ()) + names
        if part == "ffn":
            return (("ffn_w_gate", i), ("ffn_w_up", i), ("ffn_w_down", i))
        return (("ple_w_proj", i), ("ple_w_gate", i))

    parts = [(i, part) for i in range(depth) for part in ("mix", "ffn", "ple")]
    keys, groups = [], {}
    for ip in parts:
        names = part_keys(*ip)
        groups[ip] = list(range(len(keys), len(keys) + len(names)))
        keys += names
    small_shapes = [w[n].shape for n in SMALL_SHARDED]
    srcs = [_pack([w[n] for n in SMALL_SHARDED]) if n == "small" else w[n][l].astype(bf16) for n, l in keys]
    srcs, landing, gather_sems = _gather_start(srcs, [groups[ip] for ip in parts])
    gather_sems = dict(zip(parts, gather_sems))
    small_full = {}

    gw = {}

    def fetch(i, which, h):
        got = []
        for part in which:
            idx = groups[(i, part)]
            got += _gather_wait([srcs[o] for o in idx], [landing[o] for o in idx], gather_sems[(i, part)], h, f"gather_wait_{part}_{i}")
        names = [keys[o] for part in which for o in groups[(i, part)]]
        gw.update(dict(zip(names, _gather_forward(got, f"gather_forward_{which[0]}_{i}"))))

    def layer_weights(i, part, h):
        if i == 0:
            fetch(i, (part,), h)
        elif part == "mix":
            fetch(i, ("mix", "ffn", "ple"), h)
        rows = lambda a: a.reshape(-1, a.shape[-1])
        j = i // 2
        if part == "ffn":
            return dict(wg=gw[("ffn_w_gate", i)], wu=gw[("ffn_w_up", i)], wd=gw[("ffn_w_down", i)])
        if part == "ple":
            return dict(wp=_from_chip_major(gw[("ple_w_proj", i)]), pn=ple_norm[i], gn=ple_gate_norm[i], wgate=rows(gw[("ple_w_gate", i)]))
        if i == 0:
            by_chip = [_unpack(gw[("small", 0)][k], small_shapes) for k in range(N_CHIPS)]
            small_full.update({n: jnp.concatenate([by_chip[k][t] for k in range(N_CHIPS)], axis=-1) for t, n in enumerate(SMALL_SHARDED)})
        if i % 2 == 0:
            return dict(w_inT=rows(gw[("ssd_w_in", j)]),
                        conv_w=small_full["ssd_conv_w"][j], conv_b=ssd_conv_b[j], dt_bias=ssd_dt_bias[j], a_log=ssd_a_log[j],
                        d=ssd_d[j], norm_w=ssd_norm_w[j], wout=rows(gw[("ssd_w_out", j)]))
        return dict(win=_from_chip_major(gw[("gmlp_w_in", j)]), b_in=small_full["gmlp_b_in"][j], ln_w=small_full["gmlp_ln_w"][j],
                    ln_b=small_full["gmlp_ln_b"][j], w_s=gmlp_w_s[j], b_s=gmlp_b_s[j], wout=rows(gw[("gmlp_w_out", j)]))

    rows4 = lambda a: a.reshape((N_CHIPS, a.shape[0] // N_CHIPS) + a.shape[1:])
    cut = lambda a, k: a[..., k * (a.shape[-1] // N_CHIPS):(k + 1) * (a.shape[-1] // N_CHIPS)]
    layer_grads = {}
    in_flight = {}
    tokens = {}
    owns = {}

    def on_layer_grads(i, part, g):
        layer_grads[(i, part)] = g
        j = i // 2
        wholes = {}
        if part == "ffn":
            chunks = {("ffn_w_gate", i): g["wg"], ("ffn_w_up", i): g["wu"], ("ffn_w_down", i): g["wd"]}
        elif part == "ple":
            chunks = {("ple_w_proj", i): _chip_major(g["wp"]), ("ple_w_gate", i): rows4(g["wgate"])}
        elif i % 2 == 0:
            chunks = {("ssd_w_in", j): rows4(g["w_inT"]), ("ssd_w_out", j): rows4(g["wout"])}
        else:
            chunks = {("gmlp_w_in", j): _chip_major(g["win"]), ("gmlp_w_out", j): rows4(g["wout"])}
        stack = lambda prt, key, layers: jnp.stack([layer_grads[(l, prt)][key] for l in layers])
        ssd, gml, every = range(0, depth, 2), range(1, depth, 2), range(depth)
        if part == "mix" and i == 1:
            wholes["rep_early"] = stack("mix", "w_s", gml).reshape(-1, w[REP_EARLY].shape[-1])
        if part == "mix" and i == 0:
            small_g = dict(ssd_conv_w=stack("mix", "conv_w", ssd), gmlp_b_in=stack("mix", "b_in", gml),
                           gmlp_ln_w=stack("mix", "ln_w", gml), gmlp_ln_b=stack("mix", "ln_b", gml))
            chunks[("small", 0)] = jnp.stack([_pack([cut(small_g[n], k) for n in SMALL_SHARDED]) for k in range(N_CHIPS)])
            rep_g = dict(
                norm_mix=stack("mix", "norm_mix", every), norm_ffn=stack("ffn", "norm_ffn", every),
                ssd_conv_b=stack("mix", "conv_b", ssd), ssd_dt_bias=stack("mix", "dt_bias", ssd), ssd_a_log=stack("mix", "a_log", ssd),
                ssd_d=stack("mix", "d", ssd), ssd_norm_w=stack("mix", "norm_w", ssd), gmlp_b_s=stack("mix", "b_s", gml),
                ple_norm=stack("ple", "pn", every), ple_gate_norm=stack("ple", "gn", every), final_norm=final_norm_grad[0])
            wholes["rep_late"] = _pack([rep_g[n] for n in REP_LATE])
        ks, wk = list(chunks), list(wholes)
        thru, lands, sems, token = _exchange_start([chunks[k] for k in ks], [wholes[k] for k in wk], f"grads_exchange_start_{part}_{i}")
        in_flight[(i, part)] = (ks, wk, thru, lands, sems)
        tokens[(i, part)] = token
        return token

    final_norm_grad = [None]
    norms = dict(norm_mix=norm_mix, norm_ffn=norm_ffn, final_norm=final_norm)
    loss_part, grad_x, g_fn, _ = _local_step(x[0], p[:, 0], loss_target[0], norms, layer_weights, on_layer_grads, final_norm_grad)
    loss = lax.psum(loss_part, ("x", "y", "c"))

    landed, res = {}, {}

    def wait_for(which, after):
        for i, part in which:
            ks, wk, thru, lands, sems = in_flight[(i, part)]
            thru, lands = _exchange_wait(thru, lands, len(ks), sems, after, f"grads_exchange_wait_{part}_{i}")
            landed.update(dict(zip(ks + wk, lands)))
            owns.update(dict(zip(ks + wk, thru)))

    def packed_update(names, gsum, tag):
        packs = [gsum] + list(_adamw(_pack([w[n] for n in names]), _pack([mom[n] for n in names]), _pack([var[n] for n in names]), gsum, tag))
        per_kind = [_unpack(pk, [w[n].shape for n in names]) for pk in packs]
        for t, n in enumerate(names):
            res[n] = [per_kind[k][t] for k in range(4)]

    def finish(big_names, with_small, tag):
        bufs = []
        for n in big_names + (("small",) if with_small else ()):
            layers = w[n].shape[0] if n != "small" else 1
            buf = None
            for l in range(layers):
                buf = _device_sum(landed[(n, l)], owns[(n, l)], place, f"grads_sum_{n}_{l}", into=buf, layer=l, layers=layers)
            bufs.append(buf)
        reduced = _sibling_join(bufs, f"grads_sibling_join_{tag}")
        for n, gsum in zip(big_names, reduced):
            res[n] = [view(n, a) for a in [gsum] + _adamw_nd(w[n], mom[n], var[n], gsum, "adamw_" + n)]
        if with_small:
            packed_update(SMALL_SHARDED, reduced[-1][0], "adamw_small_sharded")
            packed_update(REP_LATE, _device_sum_whole(landed["rep_late"], owns["rep_late"], place, "grads_sum_rep_late"), "adamw_rep_late")
        else:
            n, shp = REP_EARLY, w[REP_EARLY].shape
            two = lambda a: a.reshape(-1, shp[-1])
            gsum = _device_sum_whole(landed["rep_early"], owns["rep_early"], place, "grads_sum_rep_early")
            res[n] = [a.reshape(shp) for a in [gsum] + list(_adamw(two(w[n]), two(mom[n]), two(var[n]), gsum, "adamw_" + n))]

    last = (0, "mix")
    late_big = tuple(n for n in BIG if n.startswith("ssd_"))
    early_big = tuple(n for n in BIG if n not in late_big)
    wait_for([ip for ip in reversed(parts) if ip != last], tokens[last])
    finish(early_big, False, "early")
    wait_for([last], res[early_big[-1]][1])
    finish(late_big, True, "late")
    return (loss, grad_x[None], *[res[n][0] for n in WEIGHTS], *[res[n][1] for n in WEIGHTS],
            *[res[n][2] for n in WEIGHTS], *[res[n][3] for n in WEIGHTS])
```
